```python
import jax, jax.numpy as jnp
from jax import lax
import numpy as np

D_MODEL = 2048
BATCH = 8
SEQ = 4096
DEPTH = 1

POOL_WIDTH = D_MODEL
POOL_WINDOWS = (2, 4, 8, 16)
N_POOL_GROUPS = len(POOL_WINDOWS)
POOL_GROUP_WIDTH = POOL_WIDTH // N_POOL_GROUPS
LRU_WIDTH = D_MODEL
LRU_BLOCK = 256
N_LRU_HEADS = LRU_WIDTH // LRU_BLOCK
CONV_WIDTH = 4
LRU_C = 8.0
N_DIRS = 2
N_BRANCHES = 2
D_FF = 4 * D_MODEL
IN_WIDTH = POOL_WIDTH + 2 * LRU_WIDTH + N_BRANCHES * D_MODEL
DN_ALPHA = (2.0 * DEPTH) ** 0.25
DN_BETA = (8.0 * DEPTH) ** -0.25
LN_EPS = 1e-5

kernel_name = "hybrid_pool_rglru_encoder_block"


def layer_norm(x, g, b):
    xf = x.astype(jnp.float32)
    mu = jnp.mean(xf, axis=-1, keepdims=True)
    xc = xf - mu
    var = jnp.mean(xc * xc, axis=-1, keepdims=True)
    y = xc * lax.rsqrt(var + LN_EPS) * g.astype(jnp.float32) + b.astype(jnp.float32)
    return y.astype(x.dtype)


def multiscale_pool(u, pool_w, pool_scale):
    B, S, P = u.shape
    uf = u.astype(jnp.float32)
    csum = jnp.pad(jnp.cumsum(uf, axis=1), ((0, 0), (1, 0), (0, 0)))
    t = jnp.arange(S)
    outs = []
    for g, w in enumerate(POOL_WINDOWS):
        lo = jnp.clip(t - w // 2, 0, S)
        hi = jnp.clip(t + w // 2, 0, S)
        sl = slice(g * POOL_GROUP_WIDTH, (g + 1) * POOL_GROUP_WIDTH)
        c = csum[:, :, sl]
        mean = (c[:, hi] - c[:, lo]) / (hi - lo).astype(jnp.float32)[None, :, None]
        outs.append(mean - uf[:, :, sl])
    d = jnp.stack(outs, axis=2)
    y = jnp.einsum('bsgi,gio->bsgo', d, pool_w.astype(jnp.float32)).reshape(B, S, P)
    return (y * pool_scale.astype(jnp.float32)).astype(u.dtype)


def centred_depthwise_conv(u, w, b):
    S = u.shape[1]
    left = CONV_WIDTH // 2
    right = CONV_WIDTH - 1 - left
    up = jnp.pad(u, ((0, 0), (left, right), (0, 0)))
    y = b
    for k in range(CONV_WIDTH):
        y = y + up[:, k:k + S, :] * w[k]
    return y


def _lin_combine(p, q):
    a1, b1 = p
    a2, b2 = q
    return a1 * a2, a2 * b1 + b2


def rg_lru(xc, wa, ba, wx, bx, lam, reverse):
    B, S, R = xc.shape
    xf = xc.astype(jnp.float32)
    xh = xf.reshape(B, S, N_LRU_HEADS, LRU_BLOCK)
    r = jax.nn.sigmoid(jnp.einsum('bshi,hio->bsho', xh, wa.astype(jnp.float32)).reshape(B, S, R) + ba.astype(jnp.float32))
    i = jax.nn.sigmoid(jnp.einsum('bshi,hio->bsho', xh, wx.astype(jnp.float32)).reshape(B, S, R) + bx.astype(jnp.float32))
    log_a = -LRU_C * jax.nn.softplus(-lam.astype(jnp.float32)) * r
    a = jnp.exp(log_a)
    inp = jnp.sqrt(-jnp.expm1(2.0 * log_a)) * (i * xf)
    _, h = lax.associative_scan(_lin_combine, (a, inp), axis=1, reverse=reverse)
    return h


def hybrid_mixer(x, w_in, pool_w, pool_scale, conv_w, conv_b, lru_wa, lru_ba, lru_wx, lru_bx,
                 lru_lambda, w_pool_up, w_lru_up, w_out, b_out):
    B, S, D = x.shape
    z = jnp.einsum('bsd,de->bse', x, w_in)
    o1 = POOL_WIDTH
    o2 = o1 + LRU_WIDTH
    o3 = o2 + LRU_WIDTH
    u_pool, u_lru, u_gate, g_logits = z[..., :o1], z[..., o1:o2], z[..., o2:o3], z[..., o3:]
    y_pool = multiscale_pool(u_pool, pool_w, pool_scale)
    xc = centred_depthwise_conv(u_lru, conv_w, conv_b)
    h = (rg_lru(xc, lru_wa[0], lru_ba[0], lru_wx[0], lru_bx[0], lru_lambda[0], False)
         + rg_lru(xc, lru_wa[1], lru_ba[1], lru_wx[1], lru_bx[1], lru_lambda[1], True))
    y_lru = h.astype(x.dtype) * jax.nn.gelu(u_gate)
    g = jax.nn.sigmoid(g_logits.astype(jnp.float32)).astype(x.dtype).reshape(B, S, N_BRANCHES, D)
    m = (g[:, :, 0] * jnp.einsum('bsp,pd->bsd', y_pool, w_pool_up)
         + g[:, :, 1] * jnp.einsum('bsr,rd->bsd', y_lru, w_lru_up))
    return jnp.einsum('bsd,de->bse', m, w_out) + b_out


def sq_relu_mlp(x, w1, b1, w2, b2):
    hdn = jnp.square(jax.nn.relu(jnp.einsum('bsd,df->bsf', x, w1) + b1))
    return jnp.einsum('bsf,fd->bsd', hdn, w2) + b2


def _fwd_setup_inputs(seed: int = 0) -> dict:
    key = jax.random.key(seed)
    ks = jax.random.split(key, 24)
    f32 = jnp.float32
    L, D, P, R = DEPTH, D_MODEL, POOL_WIDTH, LRU_WIDTH
    nrm = lambda k, shape, s: jax.random.normal(k, shape, f32) * s
    a_base = jax.random.uniform(ks[10], (L, N_DIRS, R), f32, minval=0.9, maxval=0.999)
    s = a_base ** (1.0 / LRU_C)
    lam = jnp.log(s) - jnp.log1p(-s)
    return {
        "x": nrm(ks[0], (BATCH, SEQ, D), 1.0),
        "w_in": nrm(ks[1], (L, D, IN_WIDTH), D ** -0.5),
        "pool_w": nrm(ks[2], (L, N_POOL_GROUPS, POOL_GROUP_WIDTH, POOL_GROUP_WIDTH), POOL_GROUP_WIDTH ** -0.5),
        "pool_scale": 1.0 + nrm(ks[3], (L, P), 0.1),
        "conv_w": nrm(ks[4], (L, CONV_WIDTH, R), CONV_WIDTH ** -0.5),
        "conv_b": nrm(ks[5], (L, R), 0.01),
        "lru_wa": nrm(ks[6], (L, N_DIRS, N_LRU_HEADS, LRU_BLOCK, LRU_BLOCK), LRU_BLOCK ** -0.5),
        "lru_ba": nrm(ks[7], (L, N_DIRS, R), 0.01),
        "lru_wx": nrm(ks[8], (L, N_DIRS, N_LRU_HEADS, LRU_BLOCK, LRU_BLOCK), LRU_BLOCK ** -0.5),
        "lru_bx": nrm(ks[9], (L, N_DIRS, R), 0.01),
        "lru_lambda": lam,
        "w_pool_up": nrm(ks[11], (L, P, D), DN_BETA * P ** -0.5),
        "w_lru_up": nrm(ks[12], (L, R, D), DN_BETA * R ** -0.5),
        "w_out": nrm(ks[13], (L, D, D), DN_BETA * D ** -0.5),
        "b_out": nrm(ks[14], (L, D), 0.01),
        "ln1_g": 1.0 + nrm(ks[15], (L, D), 0.1),
        "ln1_b": nrm(ks[16], (L, D), 0.01),
        "w_ff1": nrm(ks[17], (L, D, D_FF), D ** -0.5),
        "b_ff1": nrm(ks[18], (L, D_FF), 0.01),
        "w_ff2": nrm(ks[19], (L, D_FF, D), DN_BETA * D_FF ** -0.5),
        "b_ff2": nrm(ks[20], (L, D), 0.01),
        "ln2_g": 1.0 + nrm(ks[21], (L, D), 0.1),
        "ln2_b": nrm(ks[22], (L, D), 0.01),
    }


def _fwd_reference(x, w_in, pool_w, pool_scale, conv_w, conv_b, lru_wa, lru_ba, lru_wx, lru_bx,
              lru_lambda, w_pool_up, w_lru_up, w_out, b_out, ln1_g, ln1_b,
              w_ff1, b_ff1, w_ff2, b_ff2, ln2_g, ln2_b):
    for l in range(DEPTH):
        mix = hybrid_mixer(x, w_in[l], pool_w[l], pool_scale[l], conv_w[l], conv_b[l],
                           lru_wa[l], lru_ba[l], lru_wx[l], lru_bx[l], lru_lambda[l],
                           w_pool_up[l], w_lru_up[l], w_out[l], b_out[l])
        x = layer_norm(DN_ALPHA * x + mix, ln1_g[l], ln1_b[l])
        ff = sq_relu_mlp(x, w_ff1[l], b_ff1[l], w_ff2[l], b_ff2[l])
        x = layer_norm(DN_ALPHA * x + ff, ln2_g[l], ln2_b[l])
    return x


import jax as _jax
import jax.numpy as _jnp

TWIN_FORMAT = 'train_step'
FWD_PARAMS = ['x', 'w_in', 'pool_w', 'pool_scale', 'conv_w', 'conv_b', 'lru_wa', 'lru_ba', 'lru_wx', 'lru_bx', 'lru_lambda', 'w_pool_up', 'w_lru_up', 'w_out', 'b_out', 'ln1_g', 'ln1_b', 'w_ff1', 'b_ff1', 'w_ff2', 'b_ff2', 'ln2_g', 'ln2_b']
TWIN_WEIGHTS = ['w_in', 'pool_w', 'pool_scale', 'conv_w', 'conv_b', 'lru_wa', 'lru_ba', 'lru_wx', 'lru_bx', 'lru_lambda', 'w_pool_up', 'w_lru_up', 'w_out', 'b_out', 'ln1_g', 'ln1_b', 'w_ff1', 'b_ff1', 'w_ff2', 'b_ff2', 'ln2_g', 'ln2_b']
TWIN_DIFF_INPUT = 'x'
TWIN_INPUTS = ['x', 'w_in', 'pool_w', 'pool_scale', 'conv_w', 'conv_b', 'lru_wa', 'lru_ba', 'lru_wx', 'lru_bx', 'lru_lambda', 'w_pool_up', 'w_lru_up', 'w_out', 'b_out', 'ln1_g', 'ln1_b', 'w_ff1', 'b_ff1', 'w_ff2', 'b_ff2', 'ln2_g', 'ln2_b', 'loss_target', 'm_w_in', 'm_pool_w', 'm_pool_scale', 'm_conv_w', 'm_conv_b', 'm_lru_wa', 'm_lru_ba', 'm_lru_wx', 'm_lru_bx', 'm_lru_lambda', 'm_w_pool_up', 'm_w_lru_up', 'm_w_out', 'm_b_out', 'm_ln1_g', 'm_ln1_b', 'm_w_ff1', 'm_b_ff1', 'm_w_ff2', 'm_b_ff2', 'm_ln2_g', 'm_ln2_b', 'v_w_in', 'v_pool_w', 'v_pool_scale', 'v_conv_w', 'v_conv_b', 'v_lru_wa', 'v_lru_ba', 'v_lru_wx', 'v_lru_bx', 'v_lru_lambda', 'v_w_pool_up', 'v_w_lru_up', 'v_w_out', 'v_b_out', 'v_ln1_g', 'v_ln1_b', 'v_w_ff1', 'v_b_ff1', 'v_w_ff2', 'v_b_ff2', 'v_ln2_g', 'v_ln2_b']
TWIN_OUTPUTS = ['loss', 'grad_x', 'grad_w_in', 'grad_pool_w', 'grad_pool_scale', 'grad_conv_w', 'grad_conv_b', 'grad_lru_wa', 'grad_lru_ba', 'grad_lru_wx', 'grad_lru_bx', 'grad_lru_lambda', 'grad_w_pool_up', 'grad_w_lru_up', 'grad_w_out', 'grad_b_out', 'grad_ln1_g', 'grad_ln1_b', 'grad_w_ff1', 'grad_b_ff1', 'grad_w_ff2', 'grad_b_ff2', 'grad_ln2_g', 'grad_ln2_b', 'delta_w_in', 'delta_pool_w', 'delta_pool_scale', 'delta_conv_w', 'delta_conv_b', 'delta_lru_wa', 'delta_lru_ba', 'delta_lru_wx', 'delta_lru_bx', 'delta_lru_lambda', 'delta_w_pool_up', 'delta_w_lru_up', 'delta_w_out', 'delta_b_out', 'delta_ln1_g', 'delta_ln1_b', 'delta_w_ff1', 'delta_b_ff1', 'delta_w_ff2', 'delta_b_ff2', 'delta_ln2_g', 'delta_ln2_b', 'new_m_w_in', 'new_m_pool_w', 'new_m_pool_scale', 'new_m_conv_w', 'new_m_conv_b', 'new_m_lru_wa', 'new_m_lru_ba', 'new_m_lru_wx', 'new_m_lru_bx', 'new_m_lru_lambda', 'new_m_w_pool_up', 'new_m_w_lru_up', 'new_m_w_out', 'new_m_b_out', 'new_m_ln1_g', 'new_m_ln1_b', 'new_m_w_ff1', 'new_m_b_ff1', 'new_m_w_ff2', 'new_m_b_ff2', 'new_m_ln2_g', 'new_m_ln2_b', 'new_v_w_in', 'new_v_pool_w', 'new_v_pool_scale', 'new_v_conv_w', 'new_v_conv_b', 'new_v_lru_wa', 'new_v_lru_ba', 'new_v_lru_wx', 'new_v_lru_bx', 'new_v_lru_lambda', 'new_v_w_pool_up', 'new_v_w_lru_up', 'new_v_w_out', 'new_v_b_out', 'new_v_ln1_g', 'new_v_ln1_b', 'new_v_w_ff1', 'new_v_b_ff1', 'new_v_w_ff2', 'new_v_b_ff2', 'new_v_ln2_g', 'new_v_ln2_b']
TWIN_LEAF_KINDS = {'loss': 'loss', 'grad_x': 'grad_x', 'grad_w_in': 'grad_w', 'grad_pool_w': 'grad_w', 'grad_pool_scale': 'grad_w', 'grad_conv_w': 'grad_w', 'grad_conv_b': 'grad_w', 'grad_lru_wa': 'grad_w', 'grad_lru_ba': 'grad_w', 'grad_lru_wx': 'grad_w', 'grad_lru_bx': 'grad_w', 'grad_lru_lambda': 'grad_w', 'grad_w_pool_up': 'grad_w', 'grad_w_lru_up': 'grad_w', 'grad_w_out': 'grad_w', 'grad_b_out': 'grad_w', 'grad_ln1_g': 'grad_w', 'grad_ln1_b': 'grad_w', 'grad_w_ff1': 'grad_w', 'grad_b_ff1': 'grad_w', 'grad_w_ff2': 'grad_w', 'grad_b_ff2': 'grad_w', 'grad_ln2_g': 'grad_w', 'grad_ln2_b': 'grad_w', 'delta_w_in': 'delta_w', 'delta_pool_w': 'delta_w', 'delta_pool_scale': 'delta_w', 'delta_conv_w': 'delta_w', 'delta_conv_b': 'delta_w', 'delta_lru_wa': 'delta_w', 'delta_lru_ba': 'delta_w', 'delta_lru_wx': 'delta_w', 'delta_lru_bx': 'delta_w', 'delta_lru_lambda': 'delta_w', 'delta_w_pool_up': 'delta_w', 'delta_w_lru_up': 'delta_w', 'delta_w_out': 'delta_w', 'delta_b_out': 'delta_w', 'delta_ln1_g': 'delta_w', 'delta_ln1_b': 'delta_w', 'delta_w_ff1': 'delta_w', 'delta_b_ff1': 'delta_w', 'delta_w_ff2': 'delta_w', 'delta_b_ff2': 'delta_w', 'delta_ln2_g': 'delta_w', 'delta_ln2_b': 'delta_w', 'new_m_w_in': 'new_m', 'new_m_pool_w': 'new_m', 'new_m_pool_scale': 'new_m', 'new_m_conv_w': 'new_m', 'new_m_conv_b': 'new_m', 'new_m_lru_wa': 'new_m', 'new_m_lru_ba': 'new_m', 'new_m_lru_wx': 'new_m', 'new_m_lru_bx': 'new_m', 'new_m_lru_lambda': 'new_m', 'new_m_w_pool_up': 'new_m', 'new_m_w_lru_up': 'new_m', 'new_m_w_out': 'new_m', 'new_m_b_out': 'new_m', 'new_m_ln1_g': 'new_m', 'new_m_ln1_b': 'new_m', 'new_m_w_ff1': 'new_m', 'new_m_b_ff1': 'new_m', 'new_m_w_ff2': 'new_m', 'new_m_b_ff2': 'new_m', 'new_m_ln2_g': 'new_m', 'new_m_ln2_b': 'new_m', 'new_v_w_in': 'new_v', 'new_v_pool_w': 'new_v', 'new_v_pool_scale': 'new_v', 'new_v_conv_w': 'new_v', 'new_v_conv_b': 'new_v', 'new_v_lru_wa': 'new_v', 'new_v_lru_ba': 'new_v', 'new_v_lru_wx': 'new_v', 'new_v_lru_bx': 'new_v', 'new_v_lru_lambda': 'new_v', 'new_v_w_pool_up': 'new_v', 'new_v_w_lru_up': 'new_v', 'new_v_w_out': 'new_v', 'new_v_b_out': 'new_v', 'new_v_ln1_g': 'new_v', 'new_v_ln1_b': 'new_v', 'new_v_w_ff1': 'new_v', 'new_v_b_ff1': 'new_v', 'new_v_w_ff2': 'new_v', 'new_v_b_ff2': 'new_v', 'new_v_ln2_g': 'new_v', 'new_v_ln2_b': 'new_v'}


def _forward(args):
    return _fwd_reference(*[args[k] for k in FWD_PARAMS])


def _output_shape():
    def fwd():
        inp = _fwd_setup_inputs(0)
        return _fwd_reference(*[inp[k] for k in FWD_PARAMS])
    out = _jax.eval_shape(fwd)
    return out.shape, out.dtype

N_MICROBATCH = 1
ADAM_LR = 0.001
ADAM_B1 = 0.9
ADAM_B2 = 0.999
ADAM_EPS = 1e-08
ADAM_WD = 0.01
ADAM_STEP = 10
PER_EXAMPLE_BATCH_AXIS = {'x': 0, 'loss_target': 0}
SHARED_INPUTS = []
_WEIGHT_DTYPES = {'w_in': _jnp.float32, 'pool_w': _jnp.float32, 'pool_scale': _jnp.float32, 'conv_w': _jnp.float32, 'conv_b': _jnp.float32, 'lru_wa': _jnp.float32, 'lru_ba': _jnp.float32, 'lru_wx': _jnp.float32, 'lru_bx': _jnp.float32, 'lru_lambda': _jnp.float32, 'w_pool_up': _jnp.float32, 'w_lru_up': _jnp.float32, 'w_out': _jnp.float32, 'b_out': _jnp.float32, 'ln1_g': _jnp.float32, 'ln1_b': _jnp.float32, 'w_ff1': _jnp.float32, 'b_ff1': _jnp.float32, 'w_ff2': _jnp.float32, 'b_ff2': _jnp.float32, 'ln2_g': _jnp.float32, 'ln2_b': _jnp.float32}
MOMENT_SCALE = {'w_in': 8.894716e-03, 'pool_w': 1.411993e-02, 'pool_scale': 1.390051e-02, 'conv_w': 1.028834e-02, 'conv_b': 2.836534e-01, 'lru_wa': 2.387600e-03, 'lru_ba': 1.865113e-03, 'lru_wx': 4.355637e-03, 'lru_bx': 2.421241e-03, 'lru_lambda': 3.416552e-03, 'w_pool_up': 2.372298e-02, 'w_lru_up': 1.754624e-02, 'w_out': 2.949899e-02, 'b_out': 4.952807e-01, 'ln1_g': 2.349113e+00, 'ln1_b': 6.018041e-01, 'w_ff1': 2.867714e-02, 'b_ff1': 1.081179e-01, 'w_ff2': 2.222967e-01, 'b_ff2': 4.755494e-01, 'ln2_g': 1.630271e+01, 'ln2_b': 3.382898e+00}


def _to_microbatches(a, axis):
    t = _jnp.moveaxis(a, axis, 0)
    t = t.reshape((N_MICROBATCH, t.shape[0] // N_MICROBATCH) + t.shape[1:])
    return _jnp.moveaxis(t, 1, axis + 1)


def setup_inputs(seed: int = 0) -> dict:
    inp = _fwd_setup_inputs(seed)
    key = _jax.random.fold_in(_jax.random.key(seed), 7919)
    shape, _ = _output_shape()
    out = dict(inp)
    out["loss_target"] = _jax.random.normal(_jax.random.fold_in(key, 0), shape, _jnp.float32)
    for i, name in enumerate(TWIN_WEIGHTS):
        w = inp[name].astype(_jnp.float32)
        if MOMENT_SCALE is None:
            s = _jnp.sqrt(_jnp.mean(_jnp.square(w)) + 1e-30)
        else:
            s = MOMENT_SCALE[name]
        km, kv = _jax.random.split(_jax.random.fold_in(key, i + 1))
        out[name] = w
        out["m_" + name] = s * _jax.random.normal(km, w.shape, _jnp.float32)
        out["v_" + name] = (s * s) * _jax.random.uniform(kv, w.shape, _jnp.float32, 0.5, 1.5)
    if N_MICROBATCH > 1:
        for name, axis in PER_EXAMPLE_BATCH_AXIS.items():
            out[name] = _to_microbatches(out[name], axis)
    return {'x': out['x'], 'w_in': out['w_in'], 'pool_w': out['pool_w'], 'pool_scale': out['pool_scale'], 'conv_w': out['conv_w'], 'conv_b': out['conv_b'], 'lru_wa': out['lru_wa'], 'lru_ba': out['lru_ba'], 'lru_wx': out['lru_wx'], 'lru_bx': out['lru_bx'], 'lru_lambda': out['lru_lambda'], 'w_pool_up': out['w_pool_up'], 'w_lru_up': out['w_lru_up'], 'w_out': out['w_out'], 'b_out': out['b_out'], 'ln1_g': out['ln1_g'], 'ln1_b': out['ln1_b'], 'w_ff1': out['w_ff1'], 'b_ff1': out['b_ff1'], 'w_ff2': out['w_ff2'], 'b_ff2': out['b_ff2'], 'ln2_g': out['ln2_g'], 'ln2_b': out['ln2_b'], 'loss_target': out['loss_target'], 'm_w_in': out['m_w_in'], 'm_pool_w': out['m_pool_w'], 'm_pool_scale': out['m_pool_scale'], 'm_conv_w': out['m_conv_w'], 'm_conv_b': out['m_conv_b'], 'm_lru_wa': out['m_lru_wa'], 'm_lru_ba': out['m_lru_ba'], 'm_lru_wx': out['m_lru_wx'], 'm_lru_bx': out['m_lru_bx'], 'm_lru_lambda': out['m_lru_lambda'], 'm_w_pool_up': out['m_w_pool_up'], 'm_w_lru_up': out['m_w_lru_up'], 'm_w_out': out['m_w_out'], 'm_b_out': out['m_b_out'], 'm_ln1_g': out['m_ln1_g'], 'm_ln1_b': out['m_ln1_b'], 'm_w_ff1': out['m_w_ff1'], 'm_b_ff1': out['m_b_ff1'], 'm_w_ff2': out['m_w_ff2'], 'm_b_ff2': out['m_b_ff2'], 'm_ln2_g': out['m_ln2_g'], 'm_ln2_b': out['m_ln2_b'], 'v_w_in': out['v_w_in'], 'v_pool_w': out['v_pool_w'], 'v_pool_scale': out['v_pool_scale'], 'v_conv_w': out['v_conv_w'], 'v_conv_b': out['v_conv_b'], 'v_lru_wa': out['v_lru_wa'], 'v_lru_ba': out['v_lru_ba'], 'v_lru_wx': out['v_lru_wx'], 'v_lru_bx': out['v_lru_bx'], 'v_lru_lambda': out['v_lru_lambda'], 'v_w_pool_up': out['v_w_pool_up'], 'v_w_lru_up': out['v_w_lru_up'], 'v_w_out': out['v_w_out'], 'v_b_out': out['v_b_out'], 'v_ln1_g': out['v_ln1_g'], 'v_ln1_b': out['v_ln1_b'], 'v_w_ff1': out['v_w_ff1'], 'v_b_ff1': out['v_b_ff1'], 'v_w_ff2': out['v_w_ff2'], 'v_b_ff2': out['v_b_ff2'], 'v_ln2_g': out['v_ln2_g'], 'v_ln2_b': out['v_ln2_b']}


def _loss(weights, diff, rest, loss_target):
    with _jax.named_scope("forward"):
        args = {**rest, TWIN_DIFF_INPUT: diff, **{k: w.astype(_WEIGHT_DTYPES[k]) for k, w in weights.items()}}
        y = _forward(args)
    with _jax.named_scope("loss_head"):
        err = _jnp.square(y.astype(_jnp.float32) - loss_target)
        return 0.5 * _jnp.sum(_jnp.mean(err, axis=-1)) if err.ndim else 0.5 * err


def _adamw(w, g, m, v):
    m = ADAM_B1 * m + (1.0 - ADAM_B1) * g
    v = ADAM_B2 * v + (1.0 - ADAM_B2) * _jnp.square(g)
    m_hat = m / (1.0 - ADAM_B1 ** ADAM_STEP)
    v_hat = v / (1.0 - ADAM_B2 ** ADAM_STEP)
    delta = -ADAM_LR * (m_hat / (_jnp.sqrt(v_hat) + ADAM_EPS) + ADAM_WD * w)
    return delta, m, v


def reference(x, w_in, pool_w, pool_scale, conv_w, conv_b, lru_wa, lru_ba, lru_wx, lru_bx, lru_lambda, w_pool_up, w_lru_up, w_out, b_out, ln1_g, ln1_b, w_ff1, b_ff1, w_ff2, b_ff2, ln2_g, ln2_b, loss_target, m_w_in, m_pool_w, m_pool_scale, m_conv_w, m_conv_b, m_lru_wa, m_lru_ba, m_lru_wx, m_lru_bx, m_lru_lambda, m_w_pool_up, m_w_lru_up, m_w_out, m_b_out, m_ln1_g, m_ln1_b, m_w_ff1, m_b_ff1, m_w_ff2, m_b_ff2, m_ln2_g, m_ln2_b, v_w_in, v_pool_w, v_pool_scale, v_conv_w, v_conv_b, v_lru_wa, v_lru_ba, v_lru_wx, v_lru_bx, v_lru_lambda, v_w_pool_up, v_w_lru_up, v_w_out, v_b_out, v_ln1_g, v_ln1_b, v_w_ff1, v_b_ff1, v_w_ff2, v_b_ff2, v_ln2_g, v_ln2_b):
    given = dict(x=x, w_in=w_in, pool_w=pool_w, pool_scale=pool_scale, conv_w=conv_w, conv_b=conv_b, lru_wa=lru_wa, lru_ba=lru_ba, lru_wx=lru_wx, lru_bx=lru_bx, lru_lambda=lru_lambda, w_pool_up=w_pool_up, w_lru_up=w_lru_up, w_out=w_out, b_out=b_out, ln1_g=ln1_g, ln1_b=ln1_b, w_ff1=w_ff1, b_ff1=b_ff1, w_ff2=w_ff2, b_ff2=b_ff2, ln2_g=ln2_g, ln2_b=ln2_b, loss_target=loss_target, m_w_in=m_w_in, m_pool_w=m_pool_w, m_pool_scale=m_pool_scale, m_conv_w=m_conv_w, m_conv_b=m_conv_b, m_lru_wa=m_lru_wa, m_lru_ba=m_lru_ba, m_lru_wx=m_lru_wx, m_lru_bx=m_lru_bx, m_lru_lambda=m_lru_lambda, m_w_pool_up=m_w_pool_up, m_w_lru_up=m_w_lru_up, m_w_out=m_w_out, m_b_out=m_b_out, m_ln1_g=m_ln1_g, m_ln1_b=m_ln1_b, m_w_ff1=m_w_ff1, m_b_ff1=m_b_ff1, m_w_ff2=m_w_ff2, m_b_ff2=m_b_ff2, m_ln2_g=m_ln2_g, m_ln2_b=m_ln2_b, v_w_in=v_w_in, v_pool_w=v_pool_w, v_pool_scale=v_pool_scale, v_conv_w=v_conv_w, v_conv_b=v_conv_b, v_lru_wa=v_lru_wa, v_lru_ba=v_lru_ba, v_lru_wx=v_lru_wx, v_lru_bx=v_lru_bx, v_lru_lambda=v_lru_lambda, v_w_pool_up=v_w_pool_up, v_w_lru_up=v_w_lru_up, v_w_out=v_w_out, v_b_out=v_b_out, v_ln1_g=v_ln1_g, v_ln1_b=v_ln1_b, v_w_ff1=v_w_ff1, v_b_ff1=v_b_ff1, v_w_ff2=v_w_ff2, v_b_ff2=v_b_ff2, v_ln2_g=v_ln2_g, v_ln2_b=v_ln2_b)
    weights = {n: given[n] for n in TWIN_WEIGHTS}
    shared = {n: given[n] for n in SHARED_INPUTS}
    per_example = {n: given[n] for n in ['x']}
    grad_fn = _jax.value_and_grad(_loss, argnums=(0, 1))

    def one_microbatch(ex, loss_target):
        ex = dict(ex)
        diff = ex.pop(TWIN_DIFF_INPUT)
        return grad_fn(weights, diff, {**shared, **ex}, loss_target)

    if N_MICROBATCH == 1:
        loss, (grad_w, grad_x) = one_microbatch(per_example, given["loss_target"])
    else:
        def body(carry, xs):
            loss_sum, grad_sum = carry
            l_k, (gw_k, gx_k) = one_microbatch(xs[0], xs[1])
            with _jax.named_scope("update"):
                return (loss_sum + l_k, _jax.tree.map(_jnp.add, grad_sum, gw_k)), gx_k

        init = (_jnp.zeros((), _jnp.float32), _jax.tree.map(_jnp.zeros_like, weights))
        (loss, grad_w), grad_x = _jax.lax.scan(body, init, (per_example, given["loss_target"]))
    with _jax.named_scope("update"):
        delta_w, new_m, new_v = {}, {}, {}
        for n in TWIN_WEIGHTS:
            delta_w[n], new_m[n], new_v[n] = _adamw(weights[n], grad_w[n], given["m_" + n], given["v_" + n])
    return (loss, grad_x, *[grad_w[n] for n in TWIN_WEIGHTS], *[delta_w[n] for n in TWIN_WEIGHTS],
            *[new_m[n] for n in TWIN_WEIGHTS], *[new_v[n] for n in TWIN_WEIGHTS])
```

```python
import functools

import jax
import jax.numpy as jnp
from jax import lax
from jax.experimental import pallas as pl
from jax.experimental.pallas import tpu as pltpu

F32 = jnp.float32
BF16 = jnp.bfloat16
MESH = pl.DeviceIdType.MESH
AXES = ("x", "y", "c")
N_DEV = 8

DN_ALPHA = 2.0 ** 0.25
LN_EPS = 1e-5
LRU_C = 8.0
ADAM_LR = 0.001
ADAM_B1 = 0.9
ADAM_B2 = 0.999
ADAM_EPS = 1e-08
ADAM_WD = 0.01
ADAM_STEP = 10
GELU_C = 0.7978845608028654
GELU_K = 0.044715

HALO = 16
SUBLANES = 8
VMEM_MB = 56


def _cparams(sem, vmem_mb=VMEM_MB):
    return pltpu.CompilerParams(dimension_semantics=sem, vmem_limit_bytes=vmem_mb << 20)


def _dot(mode, a, b):
    if mode == "nn":
        dims = (((1,), (0,)), ((), ()))
    elif mode == "nt":
        dims = (((1,), (1,)), ((), ()))
    else:
        dims = (((0,), (0,)), ((), ()))
    return lax.dot_general(a, b, dims, preferred_element_type=F32)


def _sig(x):
    return 1.0 / (1.0 + jnp.exp(-x))


def _gelu(x):
    t = jnp.tanh(GELU_C * (x + GELU_K * x * x * x))
    return 0.5 * x * (1.0 + t)


def _gelu_grad(x):
    x2 = x * x
    t = jnp.tanh(GELU_C * (x + GELU_K * x * x2))
    return 0.5 * (1.0 + t) + 0.5 * x * (1.0 - t * t) * GELU_C * (1.0 + 3.0 * GELU_K * x2)


def _colsum(v):
    return jnp.sum(v, axis=0, keepdims=True)


def _mm(name, mode, grid, a, a_spec, b, b_spec, extras, out_shapes, out_specs, epi, *,
        order="ij", acc_shape=None, aliases=None, vmem_mb=VMEM_MB):
    gm, gn, gk = grid

    def spec(s):
        bs, f = s
        if f is None:
            return pl.BlockSpec(memory_space=pl.ANY)
        if order == "ij":
            return pl.BlockSpec(bs, lambda i, j, k, f=f: f(i, j, k))
        return pl.BlockSpec(bs, lambda j, i, k, f=f: f(i, j, k))

    ne, no = len(extras), len(out_shapes)

    def kern(*refs):
        a_ref, b_ref = refs[0], refs[1]
        ex = refs[2:2 + ne]
        outs = refs[2 + ne:2 + ne + no]
        if order == "ij":
            i, j = pl.program_id(0), pl.program_id(1)
        else:
            j, i = pl.program_id(0), pl.program_id(1)
        k = pl.program_id(2)
        prod = _dot(mode, a_ref[...], b_ref[...])
        if gk == 1:
            epi(prod, i, j, ex, outs)
        else:
            acc = refs[-1]

            @pl.when(k == 0)
            def _():
                acc[...] = prod

            @pl.when(k > 0)
            def _():
                acc[...] += prod

            @pl.when(k == gk - 1)
            def _():
                epi(acc[...], i, j, ex, outs)

    g = (gm, gn, gk) if order == "ij" else (gn, gm, gk)
    scratch = [pltpu.VMEM(acc_shape, F32)] if gk > 1 else []
    return pl.pallas_call(
        kern, name=name, grid=g,
        in_specs=[spec(a_spec), spec(b_spec)] + [spec(s) for _, s in extras],
        out_specs=[spec(s) for s in out_specs],
        out_shape=out_shapes, scratch_shapes=scratch,
        input_output_aliases=aliases or {},
        compiler_params=_cparams(("arbitrary", "arbitrary", "arbitrary"), vmem_mb),
    )(a, b, *[e for e, _ in extras])


def _ext(ref, c, n_chunks, tc, seq):
    c0 = pl.multiple_of(c * tc, tc)
    body = ref[pl.ds(c0, tc), :].astype(F32)
    t0 = pl.multiple_of(jnp.maximum(c0 - HALO, 0), HALO)
    b0 = pl.multiple_of(jnp.minimum(c0 + tc, seq - HALO), HALO)
    top = ref[pl.ds(t0, HALO), :].astype(F32)
    bot = ref[pl.ds(b0, HALO), :].astype(F32)
    top = jnp.where(c > 0, top, 0.0)
    bot = jnp.where(c < n_chunks - 1, bot, 0.0)
    return jnp.concatenate([top, body, bot], axis=0)


def _shifted(vext, off, tc):
    n = vext.shape[0]
    r = vext if off == 0 else pltpu.roll(vext, (n - off) % n, 0)
    return r[HALO:HALO + tc]


def _win_sum(vext, g, extra, tc):
    s2 = vext + pltpu.roll(vext, 1, 0)
    s4 = s2 + pltpu.roll(s2, 2, 0)
    s8 = s4 + pltpu.roll(s4, 4, 0)
    s16 = s8 + pltpu.roll(s8, 8, 0)
    outs = [_shifted(s, extra + hw - 1, tc) for s, hw in ((s2, 1), (s4, 2), (s8, 4), (s16, 8))]
    return jnp.where(g == 0, outs[0], jnp.where(g == 1, outs[1], jnp.where(g == 2, outs[2], outs[3])))


def _win_cnt(t, hw, seq):
    return (jnp.minimum(t + hw, seq) - jnp.maximum(t - hw, 0)).astype(F32)


def _pool_d(uext, g, c, tc, seq):
    hw = jnp.left_shift(1, g)
    t = c * tc + lax.broadcasted_iota(jnp.int32, (tc, 1), 0)
    ws = _win_sum(uext, g, 0, tc)
    return ws / _win_cnt(t, hw, seq) - uext[HALO:HALO + tc]


def _scan_tiles(a_ref, b_ref, h_ref, carry_ref, n_tiles, reverse):
    blk = a_ref.shape[1]
    row = lax.broadcasted_iota(jnp.int32, (SUBLANES, blk), 0)

    def tile(j, hc):
        jj = (n_tiles - 1 - j) if reverse else j
        off = pl.multiple_of(jj * SUBLANES, SUBLANES)
        a = a_ref[pl.ds(off, SUBLANES), :]
        b = b_ref[pl.ds(off, SUBLANES), :]
        for kk in (1, 2, 4):
            sh = (SUBLANES - kk) if reverse else kk
            a_s = pltpu.roll(a, sh, 0)
            b_s = pltpu.roll(b, sh, 0)
            m = (row < SUBLANES - kk) if reverse else (row >= kk)
            a_s = jnp.where(m, a_s, 1.0)
            b_s = jnp.where(m, b_s, 0.0)
            b = a * b_s + b
            a = a * a_s
        h = a * hc + b
        h_ref[pl.ds(off, SUBLANES), :] = h
        return h[0:1, :] if reverse else h[SUBLANES - 1:SUBLANES, :]

    hc = lax.fori_loop(0, n_tiles, tile, carry_ref[0:1, :])
    carry_ref[0:1, :] = hc


def _lru_k(lam):
    y = -lam
    e = jnp.exp(-jnp.abs(y))
    u = 1.0 + e
    l1p = jnp.where(u == 1.0, e, jnp.log(u) * (e / (u - 1.0)))
    return -LRU_C * (jnp.maximum(y, 0.0) + l1p)


def _lru_gates(xc, wa, wx, ba, bx, lam):
    xb = xc.astype(BF16)
    r = _sig(jnp.dot(xb, wa, preferred_element_type=F32) + ba)
    i = _sig(jnp.dot(xb, wx, preferred_element_type=F32) + bx)
    k = _lru_k(lam)
    la = k * r
    a = jnp.exp(la)
    s = jnp.sqrt(-jnp.tanh(la) * (a * a + 1.0))
    return r, i, k, a, s


SV_CONV, SV_BA, SV_BX, SV_LAM = 0, 4, 6, 8


def _pool_fwd(z, pw, scale, seq, d, tc):
    n_g = pw.shape[0]
    pg = d // n_g
    n_chunks = seq // tc

    def kern(z_ref, pw_ref, sc_ref, y_ref):
        g, c = pl.program_id(0), pl.program_id(1)
        uext = _ext(z_ref, c, n_chunks, tc, seq)
        dd = _pool_d(uext, g, c, tc, seq)
        q = jnp.dot(dd.astype(BF16), pw_ref[...], preferred_element_type=F32)
        y_ref[...] = (q * sc_ref[...]).astype(BF16)

    return pl.pallas_call(
        kern, name="pool_fwd", grid=(n_g, n_chunks),
        in_specs=[pl.BlockSpec((seq, pg), lambda g, c: (0, g)),
                  pl.BlockSpec((None, pg, pg), lambda g, c: (g, 0, 0)),
                  pl.BlockSpec((1, pg), lambda g, c: (0, g))],
        out_specs=pl.BlockSpec((tc, pg), lambda g, c: (c, g)),
        out_shape=jax.ShapeDtypeStruct((seq, d), BF16),
        compiler_params=_cparams(("arbitrary", "arbitrary")),
    )(z, pw, scale)


def _lru_fwd(z, sv, conv_b, wa, wx, seq, d, tc):
    n_h, blk = wa.shape[1], wa.shape[2]
    n_chunks = seq // tc
    lru_off = d // blk

    def kern(z_ref, sv_ref, cb_ref, wa_ref, wx_ref, xc_ref, h_ref, a_s, b_s, carry):
        c = pl.program_id(1)
        uext = _ext(z_ref, c, n_chunks, tc, seq)
        xc = cb_ref[...]
        for k in range(4):
            xc = xc + _shifted(uext, k - 2, tc) * sv_ref[SV_CONV + k:SV_CONV + k + 1, :]
        xc_ref[...] = xc
        _, i, _, a, s = _lru_gates(xc, wa_ref[...], wx_ref[...], sv_ref[SV_BA:SV_BA + 1, :],
                                   sv_ref[SV_BX:SV_BX + 1, :], sv_ref[SV_LAM:SV_LAM + 1, :])
        a_s[...] = a
        b_s[...] = s * (i * xc)

        @pl.when(c == 0)
        def _():
            carry[...] = jnp.zeros_like(carry)

        _scan_tiles(a_s, b_s, h_ref, carry, tc // SUBLANES, False)

    col = lambda h, c: (c, h)
    return pl.pallas_call(
        kern, name="lru_fwd", grid=(n_h, n_chunks),
        in_specs=[pl.BlockSpec((seq, blk), lambda h, c: (0, lru_off + h)),
                  pl.BlockSpec((16, blk), lambda h, c: (0, h)),
                  pl.BlockSpec((1, blk), lambda h, c: (0, h)),
                  pl.BlockSpec((None, None, blk, blk), lambda h, c: (0, h, 0, 0)),
                  pl.BlockSpec((None, None, blk, blk), lambda h, c: (0, h, 0, 0))],
        out_specs=[pl.BlockSpec((tc, blk), col), pl.BlockSpec((tc, blk), col)],
        out_shape=[jax.ShapeDtypeStruct((seq, d), F32), jax.ShapeDtypeStruct((seq, d), F32)],
        scratch_shapes=[pltpu.VMEM((tc, blk), F32), pltpu.VMEM((tc, blk), F32), pltpu.VMEM((SUBLANES, blk), F32)],
        compiler_params=_cparams(("arbitrary", "arbitrary")),
    )(z, sv, conv_b, wa, wx)


def _lru_rev(z, sv, wa, wx, xc, h_f, seq, d, tc):
    n_h, blk = wa.shape[1], wa.shape[2]
    n_chunks = seq // tc
    gate_off = 2 * d // blk

    def kern(z_ref, sv_ref, wa_ref, wx_ref, xc_ref, hf_ref, hb_ref, y_ref, a_s, b_s, carry):
        c = pl.program_id(1)
        xc = xc_ref[...]
        _, i, _, a, s = _lru_gates(xc, wa_ref[...], wx_ref[...], sv_ref[SV_BA + 1:SV_BA + 2, :],
                                   sv_ref[SV_BX + 1:SV_BX + 2, :], sv_ref[SV_LAM + 1:SV_LAM + 2, :])
        a_s[...] = a
        b_s[...] = s * (i * xc)

        @pl.when(c == 0)
        def _():
            carry[...] = jnp.zeros_like(carry)

        _scan_tiles(a_s, b_s, hb_ref, carry, tc // SUBLANES, True)
        y_ref[...] = ((hf_ref[...] + hb_ref[...]) * _gelu(z_ref[...])).astype(BF16)

    rev = lambda h, c: (n_chunks - 1 - c, h)
    return pl.pallas_call(
        kern, name="lru_rev", grid=(n_h, n_chunks),
        in_specs=[pl.BlockSpec((tc, blk), lambda h, c: (n_chunks - 1 - c, gate_off + h)),
                  pl.BlockSpec((16, blk), lambda h, c: (0, h)),
                  pl.BlockSpec((None, None, blk, blk), lambda h, c: (1, h, 0, 0)),
                  pl.BlockSpec((None, None, blk, blk), lambda h, c: (1, h, 0, 0)),
                  pl.BlockSpec((tc, blk), rev), pl.BlockSpec((tc, blk), rev)],
        out_specs=[pl.BlockSpec((tc, blk), rev), pl.BlockSpec((tc, blk), rev)],
        out_shape=[jax.ShapeDtypeStruct((seq, d), F32), jax.ShapeDtypeStruct((seq, d), BF16)],
        scratch_shapes=[pltpu.VMEM((tc, blk), F32), pltpu.VMEM((tc, blk), F32), pltpu.VMEM((SUBLANES, blk), F32)],
        compiler_params=_cparams(("arbitrary", "arbitrary")),
    )(z, sv, wa, wx, xc, h_f)


def _merge(y_pool, w_pu, y_lru, w_lu, z, seq, d, tm, tn):
    n_n = d // tn

    def kern(yp_ref, wp_ref, yl_ref, wl_ref, la_ref, lb_ref, m_ref, pa_ref, pb_ref):
        pa = jnp.dot(yp_ref[...], wp_ref[...], preferred_element_type=F32)
        pb = jnp.dot(yl_ref[...], wl_ref[...], preferred_element_type=F32)
        m_ref[...] = (_sig(la_ref[...]) * pa + _sig(lb_ref[...]) * pb).astype(BF16)
        pa_ref[...] = pa.astype(BF16)
        pb_ref[...] = pb.astype(BF16)

    row = pl.BlockSpec((tm, d), lambda i, j: (i, 0))
    wcol = pl.BlockSpec((d, tn), lambda i, j: (0, j))
    out = pl.BlockSpec((tm, tn), lambda i, j: (i, j))
    sh = jax.ShapeDtypeStruct((seq, d), BF16)
    return pl.pallas_call(
        kern, name="merge", grid=(seq // tm, n_n),
        in_specs=[row, wcol, row, wcol,
                  pl.BlockSpec((tm, tn), lambda i, j: (i, 3 * n_n + j)),
                  pl.BlockSpec((tm, tn), lambda i, j: (i, 4 * n_n + j))],
        out_specs=[out, out, out], out_shape=[sh, sh, sh],
        compiler_params=_cparams(("arbitrary", "arbitrary")),
    )(y_pool, w_pu, y_lru, w_lu, z, z)


def _ln_fwd(s, g, b):
    mu = jnp.mean(s, axis=-1, keepdims=True)
    xc = s - mu
    var = jnp.mean(xc * xc, axis=-1, keepdims=True)
    rstd = lax.rsqrt(var + LN_EPS)
    xhat = xc * rstd
    return xhat, rstd, xhat * g + b


def _ln_bwd(dy, xhat, rstd, g):
    dyg = dy * g
    m1 = jnp.mean(dyg, axis=-1, keepdims=True)
    m2 = jnp.mean(dyg * xhat, axis=-1, keepdims=True)
    return rstd * (dyg - m1 - xhat * m2)


def _pool_bwd(z, dy_pool, pw, scale, dz, seq, d, tc):
    n_g = pw.shape[0]
    pg = d // n_g
    n_chunks = seq // tc

    def kern(z_ref, dy_ref, pw_ref, sc_ref, dz_in, dz_ref, dpw_ref, dsc_ref):
        del dz_in
        g, c = pl.program_id(0), pl.program_id(1)
        hw = jnp.left_shift(1, g)
        uext = _ext(z_ref, c, n_chunks, tc, seq)
        dd = _pool_d(uext, g, c, tc, seq).astype(BF16)
        pwv = pw_ref[...]
        q = jnp.dot(dd, pwv, preferred_element_type=F32)
        dyext = _ext(dy_ref, c, n_chunks, tc, seq)

        @pl.when(c == 0)
        def _():
            dsc_ref[...] = jnp.zeros_like(dsc_ref)
            dpw_ref[...] = jnp.zeros_like(dpw_ref)

        dsc_ref[0:1, :] += _colsum(dyext[HALO:HALO + tc] * q)
        dqext = (dyext * sc_ref[...]).astype(BF16)
        dpw_ref[...] += _dot("tn", dd, dqext[HALO:HALO + tc])
        ddext = _dot("nt", dqext, pwv)
        text = c * tc - HALO + lax.broadcasted_iota(jnp.int32, (tc + 2 * HALO, 1), 0)
        v = ddext / jnp.maximum(_win_cnt(text, hw, seq), 1.0)
        dz_ref[...] = (_win_sum(v, g, 1, tc) - ddext[HALO:HALO + tc]).astype(BF16)

    return pl.pallas_call(
        kern, name="pool_bwd", grid=(n_g, n_chunks),
        in_specs=[pl.BlockSpec((seq, pg), lambda g, c: (0, g)),
                  pl.BlockSpec((seq, pg), lambda g, c: (0, g)),
                  pl.BlockSpec((None, pg, pg), lambda g, c: (g, 0, 0)),
                  pl.BlockSpec((1, pg), lambda g, c: (0, g)),
                  pl.BlockSpec(memory_space=pl.ANY)],
        out_specs=[pl.BlockSpec((None, tc, pg), lambda g, c: (2, c, g)),
                   pl.BlockSpec((None, pg, pg), lambda g, c: (g, 0, 0)),
                   pl.BlockSpec((SUBLANES, pg), lambda g, c: (0, g))],
        out_shape=[jax.ShapeDtypeStruct(dz.shape, dz.dtype),
                   jax.ShapeDtypeStruct((n_g, pg, pg), F32),
                   jax.ShapeDtypeStruct((SUBLANES, d), F32)],
        input_output_aliases={4: 0},
        compiler_params=_cparams(("arbitrary", "arbitrary")),
    )(z, dy_pool, pw, scale, dz)


def _lru_bwd(direction, xc, dh, h_dir, sv, wa, wx, dxc_prev, seq, d, tc):
    reverse = direction == 1
    n_h, blk = wa.shape[1], wa.shape[2]
    n_chunks = seq // tc
    has_prev = dxc_prev is not None

    def kern(*refs):
        xc_ref, dh_ref, h_ref, sv_ref, wa_ref, wx_ref = refs[:6]
        p = 6
        prev_ref = None
        if has_prev:
            prev_ref = refs[p]
            p += 1
        dxc_ref, dwa_ref, dwx_ref, sm_ref, at_s, g_s, carry, acarry = refs[p:p + 8]
        c = pl.program_id(1)
        cr = c if reverse else n_chunks - 1 - c
        c0 = pl.multiple_of(cr * tc, tc)

        @pl.when(c == 0)
        def _():
            carry[...] = jnp.zeros_like(carry)
            acarry[...] = jnp.zeros_like(acarry)
            dwa_ref[...] = jnp.zeros_like(dwa_ref)
            dwx_ref[...] = jnp.zeros_like(dwx_ref)
            sm_ref[...] = jnp.zeros_like(sm_ref)

        xc = xc_ref[...]
        wav, wxv = wa_ref[...], wx_ref[...]
        lam = sv_ref[SV_LAM + direction:SV_LAM + direction + 1, :]
        r, i, k, a, s = _lru_gates(xc, wav, wxv, sv_ref[SV_BA + direction:SV_BA + direction + 1, :],
                                   sv_ref[SV_BX + direction:SV_BX + direction + 1, :], lam)
        rowi = lax.broadcasted_iota(jnp.int32, (tc, blk), 0)
        hbody = h_ref[pl.ds(c0, tc), :]
        if not reverse:
            p0 = pl.multiple_of(jnp.maximum(c0 - SUBLANES, 0), SUBLANES)
            edge = jnp.where(cr > 0, h_ref[pl.ds(p0, SUBLANES), :][SUBLANES - 1:SUBLANES, :], 0.0)
            hprev = jnp.where(rowi == 0, edge, pltpu.roll(hbody, 1, 0))
            at = jnp.where(rowi == tc - 1, acarry[0:1, :], pltpu.roll(a, tc - 1, 0))
        else:
            n0 = pl.multiple_of(jnp.minimum(c0 + tc, seq - SUBLANES), SUBLANES)
            edge = jnp.where(cr < n_chunks - 1, h_ref[pl.ds(n0, SUBLANES), :][0:1, :], 0.0)
            hprev = jnp.where(rowi == tc - 1, edge, pltpu.roll(hbody, tc - 1, 0))
            at = jnp.where(rowi == 0, acarry[0:1, :], pltpu.roll(a, 1, 0))
        at_s[...] = at
        _scan_tiles(at_s, dh_ref, g_s, carry, tc // SUBLANES, not reverse)
        acarry[0:1, :] = a[tc - 1:tc, :] if reverse else a[0:1, :]

        gt = g_s[...]
        da = gt * hprev
        di = gt * s * xc
        dxc = gt * s * i
        ds = gt * (i * xc)
        dl = da * a - ds * (a * a) / s
        dpr = (dl * k) * r * (1.0 - r)
        dpi = di * i * (1.0 - i)
        sm_ref[0:1, :] += _colsum(dpr)
        sm_ref[1:2, :] += _colsum(dpi)
        sm_ref[2:3, :] += _colsum(dl * r) * (LRU_C * _sig(-lam))
        xb, dprb, dpib = xc.astype(BF16), dpr.astype(BF16), dpi.astype(BF16)
        dwa_ref[...] += _dot("tn", xb, dprb)
        dwx_ref[...] += _dot("tn", xb, dpib)
        dxc = dxc + _dot("nt", dprb, wav) + _dot("nt", dpib, wxv)
        if has_prev:
            dxc = dxc + prev_ref[...]
        dxc_ref[...] = dxc

    if reverse:
        chunk = lambda h, c: (c, h)
    else:
        chunk = lambda h, c: (n_chunks - 1 - c, h)
    wspec = pl.BlockSpec((None, None, blk, blk), lambda h, c: (direction, h, 0, 0))
    ins = [xc, dh, h_dir, sv, wa, wx] + ([dxc_prev] if has_prev else [])
    in_specs = [pl.BlockSpec((tc, blk), chunk), pl.BlockSpec((tc, blk), chunk),
                pl.BlockSpec((seq, blk), lambda h, c: (0, h)),
                pl.BlockSpec((16, blk), lambda h, c: (0, h)), wspec, wspec]
    if has_prev:
        in_specs.append(pl.BlockSpec((tc, blk), chunk))
    return pl.pallas_call(
        kern, name="lru_bwd_%d" % direction, grid=(n_h, n_chunks),
        in_specs=in_specs,
        out_specs=[pl.BlockSpec((tc, blk), chunk),
                   pl.BlockSpec((None, blk, blk), lambda h, c: (h, 0, 0)),
                   pl.BlockSpec((None, blk, blk), lambda h, c: (h, 0, 0)),
                   pl.BlockSpec((SUBLANES, blk), lambda h, c: (0, h))],
        out_shape=[jax.ShapeDtypeStruct((seq, d), F32),
                   jax.ShapeDtypeStruct((n_h, blk, blk), F32),
                   jax.ShapeDtypeStruct((n_h, blk, blk), F32),
                   jax.ShapeDtypeStruct((SUBLANES, d), F32)],
        scratch_shapes=[pltpu.VMEM((tc, blk), F32), pltpu.VMEM((tc, blk), F32),
                        pltpu.VMEM((SUBLANES, blk), F32), pltpu.VMEM((SUBLANES, blk), F32)],
        compiler_params=_cparams(("arbitrary", "arbitrary")),
    )(*ins)


def _conv_bwd(z, dxc, sv, dz, seq, d, tc, tcol):
    n_chunks = seq // tc
    lru_off = d // tcol

    def kern(z_ref, dx_ref, sv_ref, dz_in, dz_ref, sm_ref):
        del dz_in
        c = pl.program_id(1)
        uext = _ext(z_ref, c, n_chunks, tc, seq)
        dext = _ext(dx_ref, c, n_chunks, tc, seq)
        dbody = dext[HALO:HALO + tc]

        @pl.when(c == 0)
        def _():
            sm_ref[...] = jnp.zeros_like(sm_ref)

        du = jnp.zeros_like(dbody)
        for k in range(4):
            du = du + _shifted(dext, 2 - k, tc) * sv_ref[SV_CONV + k:SV_CONV + k + 1, :]
            sm_ref[k:k + 1, :] += _colsum(dbody * _shifted(uext, k - 2, tc))
        sm_ref[4:5, :] += _colsum(dbody)
        dz_ref[...] = du.astype(BF16)

    return pl.pallas_call(
        kern, name="conv_bwd", grid=(d // tcol, n_chunks),
        in_specs=[pl.BlockSpec((seq, tcol), lambda j, c: (0, lru_off + j)),
                  pl.BlockSpec((seq, tcol), lambda j, c: (0, j)),
                  pl.BlockSpec((16, tcol), lambda j, c: (0, j)),
                  pl.BlockSpec(memory_space=pl.ANY)],
        out_specs=[pl.BlockSpec((None, tc, tcol), lambda j, c: (3, c, j)),
                   pl.BlockSpec((SUBLANES, tcol), lambda j, c: (0, j))],
        out_shape=[jax.ShapeDtypeStruct(dz.shape, dz.dtype), jax.ShapeDtypeStruct((SUBLANES, d), F32)],
        input_output_aliases={3: 0},
        compiler_params=_cparams(("arbitrary", "arbitrary")),
    )(z, dxc, sv, dz)


BIG = (("w_in", "col"), ("pool_w", "row"), ("lru_wa", "row"), ("lru_wx", "row"), ("w_pool_up", "row"),
       ("w_lru_up", "row"), ("w_out", "row"), ("w_ff1", "col"), ("w_ff2", "row"))


def _shard_view(w, fam):
    if fam == "col":
        return w.reshape(w.shape[-2:])
    return w.reshape((-1,) + w.shape[-2:])


def _full_shape(sv_shape, fam):
    if fam == "col":
        return (sv_shape[0], N_DEV * sv_shape[1])
    return (sv_shape[0], N_DEV * sv_shape[1], sv_shape[2])


def _slot(ref, fam, p, size):
    start = pl.multiple_of(p * size, size)
    if fam == "col":
        return ref.at[:, pl.ds(start, size)]
    if fam == "row":
        return ref.at[:, pl.ds(start, size), :]
    return ref.at[p]


def _shard_extent(shape, fam):
    return shape[1]


def _coords():
    return lax.axis_index("x"), lax.axis_index("y"), lax.axis_index("c")


HBM_SPEC = pl.BlockSpec(memory_space=pl.ANY)


def _all_gather(name, shards, fams):
    n = len(shards)
    fulls = []
    for s, fam in zip(shards, fams):
        if fam == "lead":
            fulls.append(jax.ShapeDtypeStruct((N_DEV,) + s.shape, s.dtype))
        else:
            fulls.append(jax.ShapeDtypeStruct(_full_shape(s.shape, fam), s.dtype))
    sizes = [1 if fam == "lead" else _shard_extent(s.shape, fam) for s, fam in zip(shards, fams)]

    def body(*refs):
        ins, outs = refs[:n], refs[n:2 * n]
        send, recv, loc = refs[2 * n:]
        x, y, c = _coords()
        me = 4 * x + 2 * y + c
        sibling = (x, y, 1 - c)
        chips = [(1 - x, y), (x, 1 - y), (1 - x, 1 - y)]

        def copy(a, k, owner, to, src=None):
            dst = _slot(outs[a], fams[a], owner, sizes[a])
            return pltpu.make_async_remote_copy(
                src_ref=dst if src is None else src, dst_ref=dst,
                send_sem=send.at[a, k], recv_sem=recv.at[a, k], device_id=to, device_id_type=MESH)

        mine, first, passed = [], [], []
        for a in range(n):
            m = pltpu.make_async_copy(ins[a], _slot(outs[a], fams[a], me, sizes[a]), loc.at[a])
            m.start()
            mine.append(m)
            cps = [copy(a, 0, me, sibling, ins[a])]
            cps += [copy(a, 1 + j, me, (cx, cy, c), ins[a]) for j, (cx, cy) in enumerate(chips)]
            for cp in cps:
                cp.start()
            first.append(cps)
        for a in range(n):
            fw = []
            for j, (cx, cy) in enumerate(chips):
                owner = 4 * cx + 2 * cy + c
                copy(a, 1 + j, owner, (x, y, c)).wait_recv()
                cp = copy(a, 4 + j, owner, sibling)
                cp.start()
                fw.append(cp)
            passed.append(fw)
        for a in range(n):
            copy(a, 0, 4 * x + 2 * y + (1 - c), (x, y, c)).wait_recv()
            for j, (cx, cy) in enumerate(chips):
                copy(a, 4 + j, 4 * cx + 2 * cy + (1 - c), (x, y, c)).wait_recv()
            for cp in first[a] + passed[a]:
                cp.wait_send()
            mine[a].wait()

    return pl.pallas_call(
        body, name=name,
        in_specs=[HBM_SPEC] * n, out_specs=[HBM_SPEC] * n, out_shape=fulls,
        scratch_shapes=[pltpu.SemaphoreType.DMA((n, 7)), pltpu.SemaphoreType.DMA((n, 7)),
                        pltpu.SemaphoreType.DMA((n,))],
    )(*shards)


def _rs_sibling(name, fulls, fams, sizes):
    n = len(fulls)
    outs = []
    for f, fam, sz in zip(fulls, fams, sizes):
        if fam == "col":
            outs.append(jax.ShapeDtypeStruct((4, f.shape[0], sz), f.dtype))
        else:
            outs.append(jax.ShapeDtypeStruct((4, f.shape[0], sz, f.shape[2]), f.dtype))

    def body(*refs):
        ins, rcv = refs[:n], refs[n:2 * n]
        send, recv = refs[2 * n:]
        x, y, c = _coords()
        cps = []
        for a in range(n):
            for q in range(4):
                cp = pltpu.make_async_remote_copy(
                    src_ref=_slot(ins[a], fams[a], 2 * q + (1 - c), sizes[a]), dst_ref=rcv[a].at[q],
                    send_sem=send.at[a, q], recv_sem=recv.at[a, q], device_id=(x, y, 1 - c), device_id_type=MESH)
                cp.start()
                cps.append(cp)
        for cp in cps:
            cp.wait()

    return pl.pallas_call(
        body, name=name, in_specs=[HBM_SPEC] * n, out_specs=[HBM_SPEC] * n, out_shape=outs,
        scratch_shapes=[pltpu.SemaphoreType.DMA((n, 4)), pltpu.SemaphoreType.DMA((n, 4))],
    )(*fulls)


def _rs_chips(name, parts):
    n = len(parts)
    outs = [jax.ShapeDtypeStruct((3,) + p.shape[1:], p.dtype) for p in parts]

    def body(*refs):
        ins, rcv = refs[:n], refs[n:2 * n]
        send, recv = refs[2 * n:]
        x, y, c = _coords()
        cps = []
        for a in range(n):
            for r in (1, 2, 3):
                tx, ty = (1 - x) if r & 2 else x, (1 - y) if r & 1 else y
                cp = pltpu.make_async_remote_copy(
                    src_ref=ins[a].at[2 * tx + ty], dst_ref=rcv[a].at[r - 1],
                    send_sem=send.at[a, r - 1], recv_sem=recv.at[a, r - 1],
                    device_id=(tx, ty, c), device_id_type=MESH)
                cp.start()
                cps.append(cp)
        for cp in cps:
            cp.wait()

    return pl.pallas_call(
        body, name=name, in_specs=[HBM_SPEC] * n, out_specs=[HBM_SPEC] * n, out_shape=outs,
        scratch_shapes=[pltpu.SemaphoreType.DMA((n, 3)), pltpu.SemaphoreType.DMA((n, 3))],
    )(*parts)


def _tile_rows(rows, cols):
    tr = rows
    while tr * cols > (1 << 18) and tr % (2 * SUBLANES) == 0:
        tr //= 2
    return tr


def _rs_add(name, full, recv_a, fam, size, cidx):
    if fam == "col":
        rows = full.shape[0]
        tr = _tile_rows(rows, size)
        grid = (4, rows // tr)
        f_spec = pl.BlockSpec((tr, size), lambda q, i, cr: (i, 2 * q + cr[0]))
        s_spec = pl.BlockSpec((None, tr, size), lambda q, i, cr: (q, i, 0))
    else:
        nb, cols = full.shape[0], full.shape[2]
        tr = _tile_rows(size, cols)
        nt = size // tr
        grid = (4, nb * nt)
        f_spec = pl.BlockSpec((None, tr, cols), lambda q, i, cr: (i // nt, (2 * q + cr[0]) * nt + i % nt, 0))
        s_spec = pl.BlockSpec((None, None, tr, cols), lambda q, i, cr: (q, i // nt, i % nt, 0))

    def kern(c_ref, f_ref, r_ref, o_ref):
        del c_ref
        o_ref[...] = f_ref[...] + r_ref[...]

    return pl.pallas_call(
        kern, name=name,
        grid_spec=pltpu.PrefetchScalarGridSpec(num_scalar_prefetch=1, grid=grid, in_specs=[f_spec, s_spec],
                                               out_specs=s_spec),
        out_shape=jax.ShapeDtypeStruct(recv_a.shape, F32),
        compiler_params=_cparams(("arbitrary", "arbitrary"), 32),
    )(cidx, full, recv_a)


def _adam(w, g, m, v):
    m2 = ADAM_B1 * m + (1.0 - ADAM_B1) * g
    v2 = ADAM_B2 * v + (1.0 - ADAM_B2) * (g * g)
    m_hat = m2 / (1.0 - ADAM_B1 ** ADAM_STEP)
    v_hat = v2 / (1.0 - ADAM_B2 ** ADAM_STEP)
    delta = -ADAM_LR * (m_hat / (jnp.sqrt(v_hat) + ADAM_EPS) + ADAM_WD * w)
    return delta, m2, v2


def _rs_final_adam(name, parts, recv_b, w, m, v, fam, qidx):
    shp = w.shape
    if fam == "col":
        rows, cols = shp
        tr = _tile_rows(rows, cols)
        grid = (rows // tr,)
        w_spec = pl.BlockSpec((tr, cols), lambda i, qr: (i, 0))
        p_spec = pl.BlockSpec((None, tr, cols), lambda i, qr: (qr[0], i, 0))
        r_spec = pl.BlockSpec((3, tr, cols), lambda i, qr: (0, i, 0))
    else:
        nb, rows, cols = shp
        tr = _tile_rows(rows, cols)
        nt = rows // tr
        grid = (nb * nt,)
        w_spec = pl.BlockSpec((None, tr, cols), lambda i, qr: (i // nt, i % nt, 0))
        p_spec = pl.BlockSpec((None, None, tr, cols), lambda i, qr: (qr[0], i // nt, i % nt, 0))
        r_spec = pl.BlockSpec((3, None, tr, cols), lambda i, qr: (0, i // nt, i % nt, 0))

    def kern(q_ref, p_ref, r_ref, w_ref, m_ref, v_ref, g_out, d_out, m_out, v_out):
        del q_ref
        g = ((p_ref[...] + r_ref[0].astype(F32)) + r_ref[1].astype(F32)) + r_ref[2].astype(F32)
        delta, m2, v2 = _adam(w_ref[...], g, m_ref[...], v_ref[...])
        g_out[...] = g
        d_out[...] = delta
        m_out[...] = m2
        v_out[...] = v2

    sh = jax.ShapeDtypeStruct(shp, F32)
    return pl.pallas_call(
        kern, name=name,
        grid_spec=pltpu.PrefetchScalarGridSpec(
            num_scalar_prefetch=1, grid=grid, in_specs=[p_spec, r_spec, w_spec, w_spec, w_spec],
            out_specs=[w_spec] * 4),
        out_shape=[sh] * 4,
        compiler_params=_cparams(("arbitrary",), 32),
    )(qidx, parts, recv_b, w, m, v)


def _sum8(name, parts):
    def kern(p_ref, o_ref):
        acc = p_ref[0]
        for p in range(1, N_DEV):
            acc = acc + p_ref[p]
        o_ref[...] = acc

    return pl.pallas_call(
        kern, name=name, out_shape=jax.ShapeDtypeStruct(parts.shape[1:], F32),
        compiler_params=pltpu.CompilerParams(vmem_limit_bytes=32 << 20),
    )(parts)


def _adam_small(name, w, g, m, v):
    def kern(w_ref, g_ref, m_ref, v_ref, d_out, m_out, v_out):
        delta, m2, v2 = _adam(w_ref[...], g_ref[...], m_ref[...], v_ref[...])
        d_out[...] = delta
        m_out[...] = m2
        v_out[...] = v2

    sh = jax.ShapeDtypeStruct(w.shape, F32)
    return pl.pallas_call(kern, name=name, out_shape=[sh] * 3)(w, g, m, v)


def _local_step(x, target, wts, vec):
    seq, d = x.shape
    w_in, pool_w, wa, wx = wts["w_in"], wts["pool_w"], wts["lru_wa"], wts["lru_wx"]
    w_pu, w_lu, w_out, w1, w2 = wts["w_pool_up"], wts["w_lru_up"], wts["w_out"], wts["w_ff1"], wts["w_ff2"]
    sv = vec["sv"]
    ff = w1.shape[1]
    n_in = w_in.shape[1]
    blk = wa.shape[2]
    tc = min(512, seq)
    t1k, t512, t256 = min(1024, seq), min(512, seq), min(256, seq)
    n512 = min(512, d)
    tkd = min(512, d)
    tkf = min(512, ff)
    tw = min(1024, d)

    x_bf = x.astype(BF16)
    full = lambda i, j, k: (0, 0)

    def epi_store(acc, i, j, ex, outs):
        outs[0][...] = acc

    (z,) = _mm("z_proj", "nn", (seq // t1k, n_in // n512, 1),
               x_bf, ((t1k, d), lambda i, j, k: (i, 0)), w_in, ((d, n512), lambda i, j, k: (0, j)),
               [], [jax.ShapeDtypeStruct((seq, n_in), F32)], [((t1k, n512), lambda i, j, k: (i, j))], epi_store)

    y_pool = _pool_fwd(z, pool_w, vec["pool_scale"], seq, d, tc)
    xc, h_f = _lru_fwd(z, sv, vec["conv_b"], wa, wx, seq, d, tc)
    h_b, y_lru = _lru_rev(z, sv, wa, wx, xc, h_f, seq, d, tc)
    m_bf, p_a, p_b = _merge(y_pool, w_pu, y_lru, w_lu, z, seq, d, t512, n512)

    def epi_ln1(acc, i, j, ex, outs):
        x_ref, bo, g1, b1 = ex
        s1 = DN_ALPHA * x_ref[...] + (acc + bo[...])
        xhat, rstd, x1 = _ln_fwd(s1, g1[...], b1[...])
        outs[0][...] = xhat
        outs[1][...] = x1.astype(BF16)
        outs[2][...] = rstd

    rowd = lambda t: ((t, d), lambda i, j, k: (i, 0))
    vecd = ((1, d), full)
    xhat1, x1_bf, rstd1 = _mm(
        "out_ln1", "nn", (seq // t256, 1, 1), m_bf, rowd(t256), w_out, ((d, d), full),
        [(x, rowd(t256)), (vec["b_out"], vecd), (vec["ln1_g"], vecd), (vec["ln1_b"], vecd)],
        [jax.ShapeDtypeStruct((seq, d), F32), jax.ShapeDtypeStruct((seq, d), BF16),
         jax.ShapeDtypeStruct((seq, 1), F32)],
        [rowd(t256), rowd(t256), ((t256, 1), lambda i, j, k: (i, 0))], epi_ln1)

    def epi_ff1(acc, i, j, ex, outs):
        r = jnp.maximum(acc + ex[0][...], 0.0)
        outs[0][...] = r.astype(BF16)
        outs[1][...] = (r * r).astype(BF16)

    tile_f = ((t1k, n512), lambda i, j, k: (i, j))
    relu_h, hdn = _mm(
        "ff1", "nn", (seq // t1k, ff // n512, 1), x1_bf, rowd(t1k), w1, ((d, n512), lambda i, j, k: (0, j)),
        [(vec["b_ff1"], ((1, n512), lambda i, j, k: (0, j)))],
        [jax.ShapeDtypeStruct((seq, ff), BF16)] * 2, [tile_f, tile_f], epi_ff1)

    def epi_ln2(acc, i, j, ex, outs):
        xh1, tgt, g1, b1, bf2, g2, b2 = ex
        ds_ref, dsb_ref, sm_ref, loss_ref = outs
        x1 = xh1[...] * g1[...] + b1[...]
        s2 = DN_ALPHA * x1 + (acc + bf2[...])
        xhat, rstd, y = _ln_fwd(s2, g2[...], b2[...])
        e = y - tgt[...]
        part = 0.5 * jnp.sum(jnp.mean(e * e, axis=-1, keepdims=True))
        dy = e * (1.0 / d)
        ds2 = _ln_bwd(dy, xhat, rstd, g2[...])
        ds_ref[...] = ds2
        dsb_ref[...] = ds2.astype(BF16)

        @pl.when(i == 0)
        def _():
            sm_ref[...] = jnp.zeros_like(sm_ref)
            loss_ref[...] = jnp.zeros_like(loss_ref)

        sm_ref[0:1, :] += _colsum(dy * xhat)
        sm_ref[1:2, :] += _colsum(dy)
        sm_ref[2:3, :] += _colsum(ds2)
        loss_ref[...] += jnp.full(loss_ref.shape, part, F32)

    ds2, ds2_bf, sm_ln2, loss_blk = _mm(
        "ff2_ln2", "nn", (seq // t256, 1, ff // tkf), hdn, ((t256, tkf), lambda i, j, k: (i, k)),
        w2, ((tkf, d), lambda i, j, k: (k, 0)),
        [(xhat1, rowd(t256)), (target, rowd(t256)), (vec["ln1_g"], vecd), (vec["ln1_b"], vecd),
         (vec["b_ff2"], vecd), (vec["ln2_g"], vecd), (vec["ln2_b"], vecd)],
        [jax.ShapeDtypeStruct((seq, d), F32), jax.ShapeDtypeStruct((seq, d), BF16),
         jax.ShapeDtypeStruct((SUBLANES, d), F32), jax.ShapeDtypeStruct((SUBLANES, 128), F32)],
        [rowd(t256), rowd(t256), ((SUBLANES, d), full), ((SUBLANES, 128), full)],
        epi_ln2, acc_shape=(t256, d))

    def dw(name, a, b, m_dim, n_dim, b_spec=None, tn=None):
        tn = tn or min(2048, n_dim)
        tm = min(1024, m_dim)
        b_spec = b_spec or ((tc, tn), lambda i, j, k: (k, j))
        (out,) = _mm(name, "tn", (m_dim // tm, n_dim // tn, seq // tc),
                     a, ((tc, tm), lambda i, j, k: (k, i)), b, b_spec, [],
                     [jax.ShapeDtypeStruct((m_dim, n_dim), F32)], [((tm, tn), lambda i, j, k: (i, j))],
                     epi_store, acc_shape=(tm, tn))
        return out

    g_w2 = dw("dw_ff2", hdn, ds2_bf, ff, d)

    def epi_dpre(acc, i, j, ex, outs):
        dpre = acc * (2.0 * ex[0][...].astype(F32))
        outs[0][...] = dpre.astype(BF16)

        @pl.when(i == 0)
        def _():
            outs[1][...] = jnp.zeros_like(outs[1])

        outs[1][0:1, :] += _colsum(dpre)

    dpre, sm_bff1 = _mm(
        "dhdn", "nt", (seq // t1k, ff // n512, 1), ds2_bf, rowd(t1k), w2, ((n512, d), lambda i, j, k: (j, 0)),
        [(relu_h, tile_f)],
        [jax.ShapeDtypeStruct((seq, ff), BF16), jax.ShapeDtypeStruct((SUBLANES, ff), F32)],
        [tile_f, ((SUBLANES, n512), lambda i, j, k: (0, j))], epi_dpre, order="ji")

    g_w1 = dw("dw_ff1", x1_bf, dpre, d, ff)

    def epi_ln1b(acc, i, j, ex, outs):
        ds2_ref, xh1, rs1, g1 = ex
        ds_ref, dsb_ref, sm_ref = outs
        dy1 = acc + DN_ALPHA * ds2_ref[...]
        xhat = xh1[...]
        ds1 = _ln_bwd(dy1, xhat, rs1[...], g1[...])
        ds_ref[...] = ds1
        dsb_ref[...] = ds1.astype(BF16)

        @pl.when(i == 0)
        def _():
            sm_ref[...] = jnp.zeros_like(sm_ref)

        sm_ref[0:1, :] += _colsum(dy1 * xhat)
        sm_ref[1:2, :] += _colsum(dy1)
        sm_ref[2:3, :] += _colsum(ds1)

    ds1, ds1_bf, sm_ln1 = _mm(
        "dx1_ln1", "nt", (seq // t256, 1, ff // tkf), dpre, ((t256, tkf), lambda i, j, k: (i, k)),
        w1, ((d, tkf), lambda i, j, k: (0, k)),
        [(ds2, rowd(t256)), (xhat1, rowd(t256)), (rstd1, ((t256, 1), lambda i, j, k: (i, 0))),
         (vec["ln1_g"], vecd)],
        [jax.ShapeDtypeStruct((seq, d), F32), jax.ShapeDtypeStruct((seq, d), BF16),
         jax.ShapeDtypeStruct((SUBLANES, d), F32)],
        [rowd(t256), rowd(t256), ((SUBLANES, d), full)], epi_ln1b, acc_shape=(t256, d))

    g_wout = dw("dw_out", m_bf, ds1_bf, d, d)
    n_n = d // n512
    tile_d = ((t512, n512), lambda i, j, k: (i, j))

    def epi_dm(acc, i, j, ex, outs):
        la, lb, pa, pb = ex
        ga, gb = _sig(la[...]), _sig(lb[...])
        outs[0][...] = (acc * ga).astype(BF16)
        outs[1][...] = (acc * gb).astype(BF16)
        outs[2][0] = (acc * pa[...].astype(F32) * ga * (1.0 - ga)).astype(BF16)
        outs[2][1] = (acc * pb[...].astype(F32) * gb * (1.0 - gb)).astype(BF16)

    dp_a, dp_b, dz = _mm(
        "dm", "nt", (seq // t512, n_n, 1), ds1_bf, rowd(t512), w_out, ((n512, d), lambda i, j, k: (j, 0)),
        [(z, ((t512, n512), lambda i, j, k: (i, 3 * n_n + j))),
         (z, ((t512, n512), lambda i, j, k: (i, 4 * n_n + j))), (p_a, tile_d), (p_b, tile_d)],
        [jax.ShapeDtypeStruct((seq, d), BF16), jax.ShapeDtypeStruct((seq, d), BF16),
         jax.ShapeDtypeStruct((5, seq, d), BF16)],
        [tile_d, tile_d, ((2, t512, n512), lambda i, j, k: (0, i, j))], epi_dm)

    g_wpu = dw("dw_pool_up", y_pool, dp_a, d, d)
    g_wlu = dw("dw_lru_up", y_lru, dp_b, d, d)

    def epi_bf(acc, i, j, ex, outs):
        outs[0][...] = acc.astype(BF16)

    (dy_pool,) = _mm("dy_pool", "nt", (seq // t512, n_n, 1), dp_a, rowd(t512), w_pu,
                     ((n512, d), lambda i, j, k: (j, 0)), [],
                     [jax.ShapeDtypeStruct((seq, d), BF16)], [tile_d], epi_bf)

    def epi_dylru(acc, i, j, ex, outs):
        hf, hb, ug, _ = ex
        u = ug[...]
        outs[0][...] = acc * _gelu(u)
        outs[1][...] = (acc * (hf[...] + hb[...]) * _gelu_grad(u)).astype(BF16)

    dh, dz = _mm(
        "dy_lru", "nt", (seq // t512, n_n, 1), dp_b, rowd(t512), w_lu, ((n512, d), lambda i, j, k: (j, 0)),
        [(h_f, tile_d), (h_b, tile_d), (z, ((t512, n512), lambda i, j, k: (i, 2 * n_n + j))), (dz, (None, None))],
        [jax.ShapeDtypeStruct((seq, d), F32), jax.ShapeDtypeStruct(dz.shape, BF16)],
        [tile_d, ((None, t512, n512), lambda i, j, k: (4, i, j))], epi_dylru, aliases={5: 1})

    dz, g_pw, sm_pool = _pool_bwd(z, dy_pool, pool_w, vec["pool_scale"], dz, seq, d, tc)
    dxc, g_wa0, g_wx0, sm_l0 = _lru_bwd(0, xc, dh, h_f, sv, wa, wx, None, seq, d, tc)
    dxc, g_wa1, g_wx1, sm_l1 = _lru_bwd(1, xc, dh, h_b, sv, wa, wx, dxc, seq, d, tc)
    dz, sm_conv = _conv_bwd(z, dxc, sv, dz, seq, d, tc, blk)

    g_win = dw("dw_in", x_bf, dz, d, n_in,
               b_spec=((None, tc, d), lambda i, j, k: ((j + 2) % 5, k, 0)), tn=d)

    nk = d // tkd

    def epi_dx(acc, i, j, ex, outs):
        outs[0][...] = acc + DN_ALPHA * ex[0][...]

    (grad_x,) = _mm(
        "dx", "nt", (seq // t512, 1, n_in // tkd), dz,
        ((None, t512, tkd), lambda i, j, k: ((k // nk + 2) % 5, i, k % nk)),
        w_in, ((d, tkd), lambda i, j, k: (0, k)), [(ds1, rowd(t512))],
        [jax.ShapeDtypeStruct((seq, d), F32)], [rowd(t512)], epi_dx, acc_shape=(t512, d))

    grads = {"w_in": g_win, "pool_w": g_pw, "lru_wa": jnp.concatenate([g_wa0, g_wa1], axis=0),
             "lru_wx": jnp.concatenate([g_wx0, g_wx1], axis=0), "w_pool_up": g_wpu, "w_lru_up": g_wlu,
             "w_out": g_wout, "w_ff1": g_w1, "w_ff2": g_w2}
    small = {"ln2": sm_ln2, "b_ff1": sm_bff1, "ln1": sm_ln1, "pool": sm_pool, "lru0": sm_l0, "lru1": sm_l1,
             "conv": sm_conv}
    return loss_blk[0, 0], grad_x, grads, small


REP = ("pool_scale", "conv_b", "b_out", "ln1_g", "ln1_b", "b_ff2", "ln2_g", "ln2_b")
SHARDED_SMALL = (("conv_w", 4), ("lru_ba", 2), ("lru_bx", 2), ("lru_lambda", 2))
WEIGHT_ORDER = ("w_in", "pool_w", "pool_scale", "conv_w", "conv_b", "lru_wa", "lru_ba", "lru_wx", "lru_bx",
                "lru_lambda", "w_pool_up", "w_lru_up", "w_out", "b_out", "ln1_g", "ln1_b", "w_ff1", "b_ff1",
                "w_ff2", "b_ff2", "ln2_g", "ln2_b")


def _pad_rows(a, rows):
    return jnp.concatenate([a, jnp.zeros((rows - a.shape[0], a.shape[1]), a.dtype)], axis=0)


def kernel(x, w_in, pool_w, pool_scale, conv_w, conv_b, lru_wa, lru_ba, lru_wx, lru_bx, lru_lambda, w_pool_up, w_lru_up, w_out, b_out, ln1_g, ln1_b, w_ff1, b_ff1, w_ff2, b_ff2, ln2_g, ln2_b, loss_target, m_w_in, m_pool_w, m_pool_scale, m_conv_w, m_conv_b, m_lru_wa, m_lru_ba, m_lru_wx, m_lru_bx, m_lru_lambda, m_w_pool_up, m_w_lru_up, m_w_out, m_b_out, m_ln1_g, m_ln1_b, m_w_ff1, m_b_ff1, m_w_ff2, m_b_ff2, m_ln2_g, m_ln2_b, v_w_in, v_pool_w, v_pool_scale, v_conv_w, v_conv_b, v_lru_wa, v_lru_ba, v_lru_wx, v_lru_bx, v_lru_lambda, v_w_pool_up, v_w_lru_up, v_w_out, v_b_out, v_ln1_g, v_ln1_b, v_w_ff1, v_b_ff1, v_w_ff2, v_b_ff2, v_ln2_g, v_ln2_b):
    args = dict(locals())
    w = {n: args[n] for n in WEIGHT_ORDER}
    mom = {n: args["m_" + n] for n in WEIGHT_ORDER}
    var = {n: args["v_" + n] for n in WEIGHT_ORDER}
    seq, d = x.shape[1], x.shape[2]
    n_heads, blk = lru_wa.shape[2], lru_wa.shape[4]
    n_groups = pool_w.shape[1]
    ff = b_ff1.shape[1]
    cx, cy, cc = _coords()
    me = 4 * cx + 2 * cy + cc
    cidx = jnp.reshape(cc, (1,)).astype(jnp.int32)
    qidx = jnp.reshape(2 * cx + cy, (1,)).astype(jnp.int32)

    fams = [fam for _, fam in BIG]
    shards = [_shard_view(w[n], fam).astype(BF16) for n, fam in BIG]
    sv_shard = _pad_rows(jnp.concatenate([w[n].reshape(r, -1) for n, r in SHARDED_SMALL], axis=0), 16)
    gathered = _all_gather("ag_weights", shards + [sv_shard], fams + ["col"])
    wts = dict(zip([n for n, _ in BIG], gathered[:-1]))
    wts["pool_w"] = wts["pool_w"].reshape(n_groups, d // n_groups, d // n_groups)
    wts["lru_wa"] = wts["lru_wa"].reshape(2, n_heads, blk, blk)
    wts["lru_wx"] = wts["lru_wx"].reshape(2, n_heads, blk, blk)
    for n in ("w_pool_up", "w_lru_up", "w_out", "w_ff2"):
        wts[n] = wts[n].reshape(wts[n].shape[1:])
    vec = {n: w[n] for n in REP}
    vec["b_ff1"] = b_ff1
    vec["sv"] = gathered[-1]

    loss_part, grad_x, grads, small = _local_step(x.reshape(seq, d), loss_target.reshape(seq, d), wts, vec)
    loss = lax.psum(loss_part, AXES)

    gfull = []
    for n, fam in BIG:
        g = grads[n]
        gfull.append(g if fam == "col" else g.reshape((-1,) + g.shape[-2:]))
    sviews = {n: _shard_view(w[n], fam) for n, fam in BIG}
    sizes = [sviews[n].shape[1] for n, _ in BIG]
    recv_a = _rs_sibling("rs_sibling", gfull, fams, sizes)
    parts = [_rs_add("rs_add_" + n, gfull[a], recv_a[a], fam, sizes[a], cidx) for a, (n, fam) in enumerate(BIG)]
    recv_b = _rs_chips("rs_chips", parts)
    out_g, out_d, out_m, out_v = {}, {}, {}, {}
    for a, (n, fam) in enumerate(BIG):
        res = _rs_final_adam("adam_" + n, parts[a], recv_b[a], sviews[n], _shard_view(mom[n], fam),
                             _shard_view(var[n], fam), fam, qidx)
        out_g[n], out_d[n], out_m[n], out_v[n] = [r.reshape(w[n].shape) for r in res]

    rows = [small["pool"][0:1], small["conv"][4:5], small["ln1"][2:3], small["ln1"][0:1], small["ln1"][1:2],
            small["ln2"][2:3], small["ln2"][0:1], small["ln2"][1:2], small["b_ff1"][0:1].reshape(ff // d, d),
            small["conv"][0:4], small["lru0"][0:1], small["lru1"][0:1], small["lru0"][1:2], small["lru1"][1:2],
            small["lru0"][2:3], small["lru1"][2:3]]
    n_rep = len(REP) + ff // d
    n_rows = n_rep + sum(r for _, r in SHARDED_SMALL)
    pad_rows = -(-n_rows // SUBLANES) * SUBLANES
    packed = _pad_rows(jnp.concatenate(rows, axis=0), pad_rows)
    (all_small,) = _all_gather("ag_small", [packed], ["lead"])
    g_small = _sum8("sum_small", all_small)

    def pack_rep(t):
        return jnp.concatenate([t[n] for n in REP] + [t["b_ff1"].reshape(ff // d, d)], axis=0)

    def pack_sh(t):
        return jnp.concatenate([t[n].reshape(r, -1) for n, r in SHARDED_SMALL], axis=0)

    g_rep = g_small[:n_rep]
    cs = d // N_DEV
    g_sh = lax.dynamic_slice_in_dim(g_small[n_rep:n_rows], me * cs, cs, axis=1)
    d_rep, m_rep, v_rep = _adam_small("adam_rep", pack_rep(w), g_rep, pack_rep(mom), pack_rep(var))
    d_sh, m_sh, v_sh = _adam_small("adam_sharded", pack_sh(w), g_sh, pack_sh(mom), pack_sh(var))

    def unpack(rep_t, sh_t, dst):
        for i, n in enumerate(REP):
            dst[n] = rep_t[i:i + 1].reshape(w[n].shape)
        dst["b_ff1"] = rep_t[len(REP):n_rep].reshape(w["b_ff1"].shape)
        r0 = 0
        for n, r in SHARDED_SMALL:
            dst[n] = sh_t[r0:r0 + r].reshape(w[n].shape)
            r0 += r

    unpack(g_rep, g_sh, out_g)
    unpack(d_rep, d_sh, out_d)
    unpack(m_rep, m_sh, out_m)
    unpack(v_rep, v_sh, out_v)

    outs = [loss, grad_x.reshape(x.shape)]
    for t in (out_g, out_d, out_m, out_v):
        outs += [t[n] for n in WEIGHT_ORDER]
    return tuple(outs)
```

```python
import functools

import jax
import jax.numpy as jnp
from jax import lax
from jax.experimental import pallas as pl
from jax.experimental.pallas import tpu as pltpu

F32 = jnp.float32
BF16 = jnp.bfloat16
MESH = pl.DeviceIdType.MESH
AXES = ("x", "y", "c")
N_DEV = 8

DN_ALPHA = 2.0 ** 0.25
LN_EPS = 1e-5
LRU_C = 8.0
ADAM_LR = 0.001
ADAM_B1 = 0.9
ADAM_B2 = 0.999
ADAM_EPS = 1e-08
ADAM_WD = 0.01
ADAM_STEP = 10
GELU_C = 0.7978845608028654
GELU_K = 0.044715

HALO = 16
SUBLANES = 8
VMEM_MB = 56


def _cparams(sem, vmem_mb=VMEM_MB):
    return pltpu.CompilerParams(dimension_semantics=sem, vmem_limit_bytes=vmem_mb << 20)


def _dot(mode, a, b):
    if mode == "nn":
        dims = (((1,), (0,)), ((), ()))
    elif mode == "nt":
        dims = (((1,), (1,)), ((), ()))
    else:
        dims = (((0,), (0,)), ((), ()))
    return lax.dot_general(a, b, dims, preferred_element_type=F32)


def _sig(x):
    return 1.0 / (1.0 + jnp.exp(-x))


def _gelu(x):
    t = jnp.tanh(GELU_C * (x + GELU_K * x * x * x))
    return 0.5 * x * (1.0 + t)


def _gelu_grad(x):
    x2 = x * x
    t = jnp.tanh(GELU_C * (x + GELU_K * x * x2))
    return 0.5 * (1.0 + t) + 0.5 * x * (1.0 - t * t) * GELU_C * (1.0 + 3.0 * GELU_K * x2)


def _colsum(v):
    return jnp.sum(v, axis=0, keepdims=True)


def _mm(name, mode, grid, a, a_spec, b, b_spec, extras, out_shapes, out_specs, epi, *,
        order="ij", acc_shape=None, aliases=None, vmem_mb=VMEM_MB, epi_init=None, sub=None):
    gm, gn, gk = grid

    def spec(s):
        bs, f = s
        if f is None:
            return pl.BlockSpec(memory_space=pl.ANY)
        if order == "ij":
            return pl.BlockSpec(bs, lambda i, j, k, f=f: f(i, j, k))
        return pl.BlockSpec(bs, lambda j, i, k, f=f: f(i, j, k))

    ne, no = len(extras), len(out_shapes)

    def kern(*refs):
        a_ref, b_ref = refs[0], refs[1]
        ex = refs[2:2 + ne]
        outs = refs[2 + ne:2 + ne + no]
        if order == "ij":
            i, j = pl.program_id(0), pl.program_id(1)
        else:
            j, i = pl.program_id(0), pl.program_id(1)
        k = pl.program_id(2)
        prod = _dot(mode, a_ref[...], b_ref[...])
        if gk == 1:
            if epi_init is not None:
                epi_init(i, j, outs)
            epi(prod, i, j, ex, outs)
        elif epi is None:
            @pl.when(k == 0)
            def _():
                outs[0][...] = prod

            @pl.when(k > 0)
            def _():
                outs[0][...] += prod
        else:
            acc = refs[-1]

            @pl.when(k == 0)
            def _():
                acc[...] = prod

            @pl.when(k > 0)
            def _():
                acc[...] += prod

            @pl.when(k == gk - 1)
            def _():
                if epi_init is not None:
                    epi_init(i, j, outs)
                if sub is None:
                    epi(acc[...], i, j, ex, outs)
                else:
                    tm = acc_shape[0]

                    def rows_of(r, rs):
                        return r.at[rs, :] if r.shape[0] == tm else r

                    def blk(t, carry):
                        rs = pl.ds(pl.multiple_of(t * sub, sub), sub)
                        epi(acc[rs, :], i, j, [rows_of(r, rs) for r in ex], [rows_of(r, rs) for r in outs])
                        return carry

                    lax.fori_loop(0, tm // sub, blk, 0)

    g = (gm, gn, gk) if order == "ij" else (gn, gm, gk)
    scratch = [pltpu.VMEM(acc_shape, F32)] if gk > 1 and epi is not None else []
    return pl.pallas_call(
        kern, name=name, grid=g,
        in_specs=[spec(a_spec), spec(b_spec)] + [spec(s) for _, s in extras],
        out_specs=[spec(s) for s in out_specs],
        out_shape=out_shapes, scratch_shapes=scratch,
        input_output_aliases=aliases or {},
        compiler_params=_cparams(("arbitrary", "arbitrary", "arbitrary"), vmem_mb),
    )(a, b, *[e for e, _ in extras])


def _ext(ref, c, n_chunks, tc, seq):
    c0 = pl.multiple_of(c * tc, tc)
    body = ref[pl.ds(c0, tc), :].astype(F32)
    t0 = pl.multiple_of(jnp.maximum(c0 - HALO, 0), HALO)
    b0 = pl.multiple_of(jnp.minimum(c0 + tc, seq - HALO), HALO)
    top = ref[pl.ds(t0, HALO), :].astype(F32)
    bot = ref[pl.ds(b0, HALO), :].astype(F32)
    top = jnp.where(c > 0, top, 0.0)
    bot = jnp.where(c < n_chunks - 1, bot, 0.0)
    return jnp.concatenate([top, body, bot], axis=0)


def _shifted(vext, off, tc):
    n = vext.shape[0]
    r = vext if off == 0 else pltpu.roll(vext, (n - off) % n, 0)
    return r[HALO:HALO + tc]


def _win_sum(vext, g, extra, tc):
    s2 = vext + pltpu.roll(vext, 1, 0)
    s4 = s2 + pltpu.roll(s2, 2, 0)
    s8 = s4 + pltpu.roll(s4, 4, 0)
    s16 = s8 + pltpu.roll(s8, 8, 0)
    outs = [_shifted(s, extra + hw - 1, tc) for s, hw in ((s2, 1), (s4, 2), (s8, 4), (s16, 8))]
    return jnp.where(g == 0, outs[0], jnp.where(g == 1, outs[1], jnp.where(g == 2, outs[2], outs[3])))


def _win_cnt(t, hw, seq):
    return (jnp.minimum(t + hw, seq) - jnp.maximum(t - hw, 0)).astype(F32)


def _pool_d(uext, g, c, tc, seq):
    hw = jnp.left_shift(1, g)
    t = c * tc + lax.broadcasted_iota(jnp.int32, (tc, 1), 0)
    ws = _win_sum(uext, g, 0, tc)
    return ws / _win_cnt(t, hw, seq) - uext[HALO:HALO + tc]


def _scan_tiles(a_ref, b_ref, h_ref, carry_ref, n_tiles, reverse):
    blk = a_ref.shape[1]
    row = lax.broadcasted_iota(jnp.int32, (SUBLANES, blk), 0)

    def tile(j, hc):
        jj = (n_tiles - 1 - j) if reverse else j
        off = pl.multiple_of(jj * SUBLANES, SUBLANES)
        a = a_ref[pl.ds(off, SUBLANES), :]
        b = b_ref[pl.ds(off, SUBLANES), :]
        for kk in (1, 2, 4):
            sh = (SUBLANES - kk) if reverse else kk
            a_s = pltpu.roll(a, sh, 0)
            b_s = pltpu.roll(b, sh, 0)
            m = (row < SUBLANES - kk) if reverse else (row >= kk)
            a_s = jnp.where(m, a_s, 1.0)
            b_s = jnp.where(m, b_s, 0.0)
            b = a * b_s + b
            a = a * a_s
        h = a * hc + b
        h_ref[pl.ds(off, SUBLANES), :] = h
        return h[0:1, :] if reverse else h[SUBLANES - 1:SUBLANES, :]

    hc = lax.fori_loop(0, n_tiles, tile, carry_ref[0:1, :])
    carry_ref[0:1, :] = hc


def _lru_k(lam):
    y = -lam
    e = jnp.exp(-jnp.abs(y))
    u = 1.0 + e
    l1p = jnp.where(u == 1.0, e, jnp.log(u) * (e / (u - 1.0)))
    return -LRU_C * (jnp.maximum(y, 0.0) + l1p)


def _lru_gates(xc, wa, wx, ba, bx, lam):
    xb = xc.astype(BF16)
    r = _sig(jnp.dot(xb, wa, preferred_element_type=F32) + ba)
    i = _sig(jnp.dot(xb, wx, preferred_element_type=F32) + bx)
    k = _lru_k(lam)
    la = k * r
    a = jnp.exp(la)
    s = jnp.sqrt(-jnp.tanh(la) * (a * a + 1.0))
    return r, i, k, a, s


SV_CONV, SV_BA, SV_BX, SV_LAM = 0, 4, 6, 8


def _pool_fwd(z, pw, scale, seq, d, tc):
    n_g = pw.shape[0]
    pg = d // n_g
    n_chunks = seq // tc

    def kern(z_ref, pw_ref, sc_ref, y_ref):
        g, c = pl.program_id(0), pl.program_id(1)
        uext = _ext(z_ref, c, n_chunks, tc, seq)
        dd = _pool_d(uext, g, c, tc, seq)
        q = jnp.dot(dd.astype(BF16), pw_ref[...], preferred_element_type=F32)
        y_ref[...] = (q * sc_ref[...]).astype(BF16)

    return pl.pallas_call(
        kern, name="pool_fwd", grid=(n_g, n_chunks),
        in_specs=[pl.BlockSpec((seq, pg), lambda g, c: (0, g)),
                  pl.BlockSpec((None, pg, pg), lambda g, c: (g, 0, 0)),
                  pl.BlockSpec((1, pg), lambda g, c: (0, g))],
        out_specs=pl.BlockSpec((tc, pg), lambda g, c: (c, g)),
        out_shape=jax.ShapeDtypeStruct((seq, d), BF16),
        compiler_params=_cparams(("arbitrary", "arbitrary")),
    )(z, pw, scale)


def _lru_fwd(z, sv, conv_b, wa, wx, seq, d, tc):
    n_h, blk = wa.shape[1], wa.shape[2]
    n_chunks = seq // tc
    lru_off = d // blk

    def kern(z_ref, sv_ref, cb_ref, wa_ref, wx_ref, xc_ref, h_ref, a_s, b_s, carry):
        c = pl.program_id(1)
        uext = _ext(z_ref, c, n_chunks, tc, seq)
        xc = cb_ref[...]
        for k in range(4):
            xc = xc + _shifted(uext, k - 2, tc) * sv_ref[SV_CONV + k:SV_CONV + k + 1, :]
        xc_ref[...] = xc
        _, i, _, a, s = _lru_gates(xc, wa_ref[...], wx_ref[...], sv_ref[SV_BA:SV_BA + 1, :],
                                   sv_ref[SV_BX:SV_BX + 1, :], sv_ref[SV_LAM:SV_LAM + 1, :])
        a_s[...] = a
        b_s[...] = s * (i * xc)

        @pl.when(c == 0)
        def _():
            carry[...] = jnp.zeros_like(carry)

        _scan_tiles(a_s, b_s, h_ref, carry, tc // SUBLANES, False)

    col = lambda h, c: (c, h)
    return pl.pallas_call(
        kern, name="lru_fwd", grid=(n_h, n_chunks),
        in_specs=[pl.BlockSpec((seq, blk), lambda h, c: (0, lru_off + h)),
                  pl.BlockSpec((16, blk), lambda h, c: (0, h)),
                  pl.BlockSpec((1, blk), lambda h, c: (0, h)),
                  pl.BlockSpec((None, None, blk, blk), lambda h, c: (0, h, 0, 0)),
                  pl.BlockSpec((None, None, blk, blk), lambda h, c: (0, h, 0, 0))],
        out_specs=[pl.BlockSpec((tc, blk), col), pl.BlockSpec((tc, blk), col)],
        out_shape=[jax.ShapeDtypeStruct((seq, d), F32), jax.ShapeDtypeStruct((seq, d), F32)],
        scratch_shapes=[pltpu.VMEM((tc, blk), F32), pltpu.VMEM((tc, blk), F32), pltpu.VMEM((SUBLANES, blk), F32)],
        compiler_params=_cparams(("arbitrary", "arbitrary")),
    )(z, sv, conv_b, wa, wx)


def _lru_rev(z, sv, wa, wx, xc, h_f, seq, d, tc):
    n_h, blk = wa.shape[1], wa.shape[2]
    n_chunks = seq // tc
    gate_off = 2 * d // blk

    def kern(z_ref, sv_ref, wa_ref, wx_ref, xc_ref, hf_ref, hb_ref, y_ref, a_s, b_s, carry):
        c = pl.program_id(1)
        xc = xc_ref[...]
        _, i, _, a, s = _lru_gates(xc, wa_ref[...], wx_ref[...], sv_ref[SV_BA + 1:SV_BA + 2, :],
                                   sv_ref[SV_BX + 1:SV_BX + 2, :], sv_ref[SV_LAM + 1:SV_LAM + 2, :])
        a_s[...] = a
        b_s[...] = s * (i * xc)

        @pl.when(c == 0)
        def _():
            carry[...] = jnp.zeros_like(carry)

        _scan_tiles(a_s, b_s, hb_ref, carry, tc // SUBLANES, True)
        y_ref[...] = ((hf_ref[...] + hb_ref[...]) * _gelu(z_ref[...])).astype(BF16)

    rev = lambda h, c: (n_chunks - 1 - c, h)
    return pl.pallas_call(
        kern, name="lru_rev", grid=(n_h, n_chunks),
        in_specs=[pl.BlockSpec((tc, blk), lambda h, c: (n_chunks - 1 - c, gate_off + h)),
                  pl.BlockSpec((16, blk), lambda h, c: (0, h)),
                  pl.BlockSpec((None, None, blk, blk), lambda h, c: (1, h, 0, 0)),
                  pl.BlockSpec((None, None, blk, blk), lambda h, c: (1, h, 0, 0)),
                  pl.BlockSpec((tc, blk), rev), pl.BlockSpec((tc, blk), rev)],
        out_specs=[pl.BlockSpec((tc, blk), rev), pl.BlockSpec((tc, blk), rev)],
        out_shape=[jax.ShapeDtypeStruct((seq, d), F32), jax.ShapeDtypeStruct((seq, d), BF16)],
        scratch_shapes=[pltpu.VMEM((tc, blk), F32), pltpu.VMEM((tc, blk), F32), pltpu.VMEM((SUBLANES, blk), F32)],
        compiler_params=_cparams(("arbitrary", "arbitrary")),
    )(z, sv, wa, wx, xc, h_f)


def _merge(y_pool, w_pu, y_lru, w_lu, z, seq, d, tm, tn):
    n_n = d // tn

    def kern(yp_ref, wp_ref, yl_ref, wl_ref, la_ref, lb_ref, m_ref, pa_ref, pb_ref):
        pa = jnp.dot(yp_ref[...], wp_ref[...], preferred_element_type=F32)
        pb = jnp.dot(yl_ref[...], wl_ref[...], preferred_element_type=F32)
        m_ref[...] = (_sig(la_ref[...]) * pa + _sig(lb_ref[...]) * pb).astype(BF16)
        pa_ref[...] = pa.astype(BF16)
        pb_ref[...] = pb.astype(BF16)

    row = pl.BlockSpec((tm, d), lambda i, j: (i, 0))
    wcol = pl.BlockSpec((d, tn), lambda i, j: (0, j))
    out = pl.BlockSpec((tm, tn), lambda i, j: (i, j))
    sh = jax.ShapeDtypeStruct((seq, d), BF16)
    return pl.pallas_call(
        kern, name="merge", grid=(seq // tm, n_n),
        in_specs=[row, wcol, row, wcol,
                  pl.BlockSpec((tm, tn), lambda i, j: (i, 3 * n_n + j)),
                  pl.BlockSpec((tm, tn), lambda i, j: (i, 4 * n_n + j))],
        out_specs=[out, out, out], out_shape=[sh, sh, sh],
        compiler_params=_cparams(("arbitrary", "arbitrary")),
    )(y_pool, w_pu, y_lru, w_lu, z, z)


def _ln_fwd(s, g, b):
    mu = jnp.mean(s, axis=-1, keepdims=True)
    xc = s - mu
    var = jnp.mean(xc * xc, axis=-1, keepdims=True)
    rstd = lax.rsqrt(var + LN_EPS)
    xhat = xc * rstd
    return xhat, rstd, xhat * g + b


def _ln_bwd(dy, xhat, rstd, g):
    dyg = dy * g
    m1 = jnp.mean(dyg, axis=-1, keepdims=True)
    m2 = jnp.mean(dyg * xhat, axis=-1, keepdims=True)
    return rstd * (dyg - m1 - xhat * m2)


def _pool_bwd(z, dy_pool, pw, scale, dz, seq, d, tc):
    n_g = pw.shape[0]
    pg = d // n_g
    n_chunks = seq // tc

    def kern(z_ref, dy_ref, pw_ref, sc_ref, dz_in, dz_ref, dpw_ref, dsc_ref):
        del dz_in
        g, c = pl.program_id(0), pl.program_id(1)
        hw = jnp.left_shift(1, g)
        uext = _ext(z_ref, c, n_chunks, tc, seq)
        dd = _pool_d(uext, g, c, tc, seq).astype(BF16)
        pwv = pw_ref[...]
        q = jnp.dot(dd, pwv, preferred_element_type=F32)
        dyext = _ext(dy_ref, c, n_chunks, tc, seq)

        @pl.when(c == 0)
        def _():
            dsc_ref[...] = jnp.zeros_like(dsc_ref)
            dpw_ref[...] = jnp.zeros_like(dpw_ref)

        dsc_ref[0:1, :] += _colsum(dyext[HALO:HALO + tc] * q)
        dqext = (dyext * sc_ref[...]).astype(BF16)
        dpw_ref[...] += _dot("tn", dd, dqext[HALO:HALO + tc])
        ddext = _dot("nt", dqext, pwv)
        text = c * tc - HALO + lax.broadcasted_iota(jnp.int32, (tc + 2 * HALO, 1), 0)
        v = ddext / jnp.maximum(_win_cnt(text, hw, seq), 1.0)
        dz_ref[...] = (_win_sum(v, g, 1, tc) - ddext[HALO:HALO + tc]).astype(BF16)

    return pl.pallas_call(
        kern, name="pool_bwd", grid=(n_g, n_chunks),
        in_specs=[pl.BlockSpec((seq, pg), lambda g, c: (0, g)),
                  pl.BlockSpec((seq, pg), lambda g, c: (0, g)),
                  pl.BlockSpec((None, pg, pg), lambda g, c: (g, 0, 0)),
                  pl.BlockSpec((1, pg), lambda g, c: (0, g)),
                  pl.BlockSpec(memory_space=pl.ANY)],
        out_specs=[pl.BlockSpec((None, tc, pg), lambda g, c: (2, c, g)),
                   pl.BlockSpec((None, pg, pg), lambda g, c: (g, 0, 0)),
                   pl.BlockSpec((SUBLANES, pg), lambda g, c: (0, g))],
        out_shape=[jax.ShapeDtypeStruct(dz.shape, dz.dtype),
                   jax.ShapeDtypeStruct((n_g, pg, pg), F32),
                   jax.ShapeDtypeStruct((SUBLANES, d), F32)],
        input_output_aliases={4: 0},
        compiler_params=_cparams(("arbitrary", "arbitrary")),
    )(z, dy_pool, pw, scale, dz)


def _lru_bwd(direction, xc, dh, h_dir, sv, wa, wx, dxc_prev, seq, d, tc):
    reverse = direction == 1
    n_h, blk = wa.shape[1], wa.shape[2]
    n_chunks = seq // tc
    has_prev = dxc_prev is not None

    def kern(*refs):
        xc_ref, dh_ref, h_ref, sv_ref, wa_ref, wx_ref = refs[:6]
        p = 6
        prev_ref = None
        if has_prev:
            prev_ref = refs[p]
            p += 1
        dxc_ref, dwa_ref, dwx_ref, sm_ref, at_s, g_s, carry, acarry = refs[p:p + 8]
        c = pl.program_id(1)
        cr = c if reverse else n_chunks - 1 - c
        c0 = pl.multiple_of(cr * tc, tc)

        @pl.when(c == 0)
        def _():
            carry[...] = jnp.zeros_like(carry)
            acarry[...] = jnp.zeros_like(acarry)
            dwa_ref[...] = jnp.zeros_like(dwa_ref)
            dwx_ref[...] = jnp.zeros_like(dwx_ref)
            sm_ref[...] = jnp.zeros_like(sm_ref)

        xc = xc_ref[...]
        wav, wxv = wa_ref[...], wx_ref[...]
        lam = sv_ref[SV_LAM + direction:SV_LAM + direction + 1, :]
        r, i, k, a, s = _lru_gates(xc, wav, wxv, sv_ref[SV_BA + direction:SV_BA + direction + 1, :],
                                   sv_ref[SV_BX + direction:SV_BX + direction + 1, :], lam)
        rowi = lax.broadcasted_iota(jnp.int32, (tc, blk), 0)
        hbody = h_ref[pl.ds(c0, tc), :]
        if not reverse:
            p0 = pl.multiple_of(jnp.maximum(c0 - SUBLANES, 0), SUBLANES)
            edge = jnp.where(cr > 0, h_ref[pl.ds(p0, SUBLANES), :][SUBLANES - 1:SUBLANES, :], 0.0)
            hprev = jnp.where(rowi == 0, edge, pltpu.roll(hbody, 1, 0))
            at = jnp.where(rowi == tc - 1, acarry[0:1, :], pltpu.roll(a, tc - 1, 0))
        else:
            n0 = pl.multiple_of(jnp.minimum(c0 + tc, seq - SUBLANES), SUBLANES)
            edge = jnp.where(cr < n_chunks - 1, h_ref[pl.ds(n0, SUBLANES), :][0:1, :], 0.0)
            hprev = jnp.where(rowi == tc - 1, edge, pltpu.roll(hbody, tc - 1, 0))
            at = jnp.where(rowi == 0, acarry[0:1, :], pltpu.roll(a, 1, 0))
        at_s[...] = at
        _scan_tiles(at_s, dh_ref, g_s, carry, tc // SUBLANES, not reverse)
        acarry[0:1, :] = a[tc - 1:tc, :] if reverse else a[0:1, :]

        gt = g_s[...]
        da = gt * hprev
        di = gt * s * xc
        dxc = gt * s * i
        ds = gt * (i * xc)
        dl = da * a - ds * (a * a) / s
        dpr = (dl * k) * r * (1.0 - r)
        dpi = di * i * (1.0 - i)
        sm_ref[0:1, :] += _colsum(dpr)
        sm_ref[1:2, :] += _colsum(dpi)
        sm_ref[2:3, :] += _colsum(dl * r) * (LRU_C * _sig(-lam))
        xb, dprb, dpib = xc.astype(BF16), dpr.astype(BF16), dpi.astype(BF16)
        dwa_ref[...] += _dot("tn", xb, dprb)
        dwx_ref[...] += _dot("tn", xb, dpib)
        dxc = dxc + _dot("nt", dprb, wav) + _dot("nt", dpib, wxv)
        if has_prev:
            dxc = dxc + prev_ref[...]
        dxc_ref[...] = dxc

    if reverse:
        chunk = lambda h, c: (c, h)
    else:
        chunk = lambda h, c: (n_chunks - 1 - c, h)
    wspec = pl.BlockSpec((None, None, blk, blk), lambda h, c: (direction, h, 0, 0))
    ins = [xc, dh, h_dir, sv, wa, wx] + ([dxc_prev] if has_prev else [])
    in_specs = [pl.BlockSpec((tc, blk), chunk), pl.BlockSpec((tc, blk), chunk),
                pl.BlockSpec((seq, blk), lambda h, c: (0, h)),
                pl.BlockSpec((16, blk), lambda h, c: (0, h)), wspec, wspec]
    if has_prev:
        in_specs.append(pl.BlockSpec((tc, blk), chunk))
    return pl.pallas_call(
        kern, name="lru_bwd_%d" % direction, grid=(n_h, n_chunks),
        in_specs=in_specs,
        out_specs=[pl.BlockSpec((tc, blk), chunk),
                   pl.BlockSpec((None, blk, blk), lambda h, c: (h, 0, 0)),
                   pl.BlockSpec((None, blk, blk), lambda h, c: (h, 0, 0)),
                   pl.BlockSpec((SUBLANES, blk), lambda h, c: (0, h))],
        out_shape=[jax.ShapeDtypeStruct((seq, d), F32),
                   jax.ShapeDtypeStruct((n_h, blk, blk), F32),
                   jax.ShapeDtypeStruct((n_h, blk, blk), F32),
                   jax.ShapeDtypeStruct((SUBLANES, d), F32)],
        scratch_shapes=[pltpu.VMEM((tc, blk), F32), pltpu.VMEM((tc, blk), F32),
                        pltpu.VMEM((SUBLANES, blk), F32), pltpu.VMEM((SUBLANES, blk), F32)],
        compiler_params=_cparams(("arbitrary", "arbitrary")),
    )(*ins)


def _conv_bwd(z, dxc, sv, dz, seq, d, tc, tcol):
    n_chunks = seq // tc
    lru_off = d // tcol

    def kern(z_ref, dx_ref, sv_ref, dz_in, dz_ref, sm_ref):
        del dz_in
        c = pl.program_id(1)
        uext = _ext(z_ref, c, n_chunks, tc, seq)
        dext = _ext(dx_ref, c, n_chunks, tc, seq)
        dbody = dext[HALO:HALO + tc]

        @pl.when(c == 0)
        def _():
            sm_ref[...] = jnp.zeros_like(sm_ref)

        du = jnp.zeros_like(dbody)
        for k in range(4):
            du = du + _shifted(dext, 2 - k, tc) * sv_ref[SV_CONV + k:SV_CONV + k + 1, :]
            sm_ref[k:k + 1, :] += _colsum(dbody * _shifted(uext, k - 2, tc))
        sm_ref[4:5, :] += _colsum(dbody)
        dz_ref[...] = du.astype(BF16)

    return pl.pallas_call(
        kern, name="conv_bwd", grid=(d // tcol, n_chunks),
        in_specs=[pl.BlockSpec((seq, tcol), lambda j, c: (0, lru_off + j)),
                  pl.BlockSpec((seq, tcol), lambda j, c: (0, j)),
                  pl.BlockSpec((16, tcol), lambda j, c: (0, j)),
                  pl.BlockSpec(memory_space=pl.ANY)],
        out_specs=[pl.BlockSpec((None, tc, tcol), lambda j, c: (3, c, j)),
                   pl.BlockSpec((SUBLANES, tcol), lambda j, c: (0, j))],
        out_shape=[jax.ShapeDtypeStruct(dz.shape, dz.dtype), jax.ShapeDtypeStruct((SUBLANES, d), F32)],
        input_output_aliases={3: 0},
        compiler_params=_cparams(("arbitrary", "arbitrary")),
    )(z, dxc, sv, dz)


BIG = (("w_in", "col"), ("pool_w", "row"), ("lru_wa", "row"), ("lru_wx", "row"), ("w_pool_up", "row"),
       ("w_lru_up", "row"), ("w_out", "row"), ("w_ff1", "col"), ("w_ff2", "row"))


def _shard_view(w, fam):
    if fam == "col":
        return w.reshape(w.shape[-2:])
    return w.reshape((-1,) + w.shape[-2:])


def _full_shape(sv_shape, fam):
    if fam == "col":
        return (sv_shape[0], N_DEV * sv_shape[1])
    return (sv_shape[0], N_DEV * sv_shape[1], sv_shape[2])


def _slot(ref, fam, p, size):
    start = pl.multiple_of(p * size, size)
    if fam == "col":
        return ref.at[:, pl.ds(start, size)]
    if fam == "row":
        return ref.at[:, pl.ds(start, size), :]
    return ref.at[p]


def _shard_extent(shape, fam):
    return shape[1]


def _coords():
    return lax.axis_index("x"), lax.axis_index("y"), lax.axis_index("c")


HBM_SPEC = pl.BlockSpec(memory_space=pl.ANY)


def _all_gather(name, shards, fams):
    n = len(shards)
    fulls = []
    for s, fam in zip(shards, fams):
        if fam == "lead":
            fulls.append(jax.ShapeDtypeStruct((N_DEV,) + s.shape, s.dtype))
        else:
            fulls.append(jax.ShapeDtypeStruct(_full_shape(s.shape, fam), s.dtype))
    sizes = [1 if fam == "lead" else _shard_extent(s.shape, fam) for s, fam in zip(shards, fams)]

    def body(*refs):
        ins, outs = refs[:n], refs[n:2 * n]
        send, recv, loc = refs[2 * n:]
        x, y, c = _coords()
        me = 4 * x + 2 * y + c
        sibling = (x, y, 1 - c)
        chips = [(1 - x, y), (x, 1 - y), (1 - x, 1 - y)]

        def copy(a, k, owner, to, src=None):
            dst = _slot(outs[a], fams[a], owner, sizes[a])
            return pltpu.make_async_remote_copy(
                src_ref=dst if src is None else src, dst_ref=dst,
                send_sem=send.at[a, k], recv_sem=recv.at[a, k], device_id=to, device_id_type=MESH)

        mine, first, passed = [], [], []
        for a in range(n):
            m = pltpu.make_async_copy(ins[a], _slot(outs[a], fams[a], me, sizes[a]), loc.at[a])
            m.start()
            mine.append(m)
            cps = [copy(a, 0, me, sibling, ins[a])]
            cps += [copy(a, 1 + j, me, (cx, cy, c), ins[a]) for j, (cx, cy) in enumerate(chips)]
            for cp in cps:
                cp.start()
            first.append(cps)
        for a in range(n):
            fw = []
            for j, (cx, cy) in enumerate(chips):
                owner = 4 * cx + 2 * cy + c
                copy(a, 1 + j, owner, (x, y, c)).wait_recv()
                cp = copy(a, 4 + j, owner, sibling)
                cp.start()
                fw.append(cp)
            passed.append(fw)
        for a in range(n):
            copy(a, 0, 4 * x + 2 * y + (1 - c), (x, y, c)).wait_recv()
            for j, (cx, cy) in enumerate(chips):
                copy(a, 4 + j, 4 * cx + 2 * cy + (1 - c), (x, y, c)).wait_recv()
            for cp in first[a] + passed[a]:
                cp.wait_send()
            mine[a].wait()

    return pl.pallas_call(
        body, name=name,
        in_specs=[HBM_SPEC] * n, out_specs=[HBM_SPEC] * n, out_shape=fulls,
        scratch_shapes=[pltpu.SemaphoreType.DMA((n, 7)), pltpu.SemaphoreType.DMA((n, 7)),
                        pltpu.SemaphoreType.DMA((n,))],
    )(*shards)


def _rs_sibling(name, fulls, fams, sizes):
    n = len(fulls)
    outs = []
    for f, fam, sz in zip(fulls, fams, sizes):
        if fam == "col":
            outs.append(jax.ShapeDtypeStruct((4, f.shape[0], sz), f.dtype))
        else:
            outs.append(jax.ShapeDtypeStruct((4, f.shape[0], sz, f.shape[2]), f.dtype))

    def body(*refs):
        ins, rcv = refs[:n], refs[n:2 * n]
        send, recv = refs[2 * n:]
        x, y, c = _coords()
        cps = []
        for a in range(n):
            for q in range(4):
                cp = pltpu.make_async_remote_copy(
                    src_ref=_slot(ins[a], fams[a], 2 * q + (1 - c), sizes[a]), dst_ref=rcv[a].at[q],
                    send_sem=send.at[a, q], recv_sem=recv.at[a, q], device_id=(x, y, 1 - c), device_id_type=MESH)
                cp.start()
                cps.append(cp)
        for cp in cps:
            cp.wait()

    return pl.pallas_call(
        body, name=name, in_specs=[HBM_SPEC] * n, out_specs=[HBM_SPEC] * n, out_shape=outs,
        scratch_shapes=[pltpu.SemaphoreType.DMA((n, 4)), pltpu.SemaphoreType.DMA((n, 4))],
    )(*fulls)


def _rs_chips(name, parts):
    n = len(parts)
    outs = [jax.ShapeDtypeStruct((3,) + p.shape[1:], p.dtype) for p in parts]

    def body(*refs):
        ins, rcv = refs[:n], refs[n:2 * n]
        send, recv = refs[2 * n:]
        x, y, c = _coords()
        cps = []
        for a in range(n):
            for r in (1, 2, 3):
                tx, ty = (1 - x) if r & 2 else x, (1 - y) if r & 1 else y
                cp = pltpu.make_async_remote_copy(
                    src_ref=ins[a].at[2 * tx + ty], dst_ref=rcv[a].at[r - 1],
                    send_sem=send.at[a, r - 1], recv_sem=recv.at[a, r - 1],
                    device_id=(tx, ty, c), device_id_type=MESH)
                cp.start()
                cps.append(cp)
        for cp in cps:
            cp.wait()

    return pl.pallas_call(
        body, name=name, in_specs=[HBM_SPEC] * n, out_specs=[HBM_SPEC] * n, out_shape=outs,
        scratch_shapes=[pltpu.SemaphoreType.DMA((n, 3)), pltpu.SemaphoreType.DMA((n, 3))],
    )(*parts)


def _tile_rows(rows, cols):
    tr = rows
    while tr * cols > (1 << 18) and tr % (2 * SUBLANES) == 0:
        tr //= 2
    return tr


def _rs_add(name, full, recv_a, fam, size, cidx):
    if fam == "col":
        rows = full.shape[0]
        tr = _tile_rows(rows, size)
        grid = (4, rows // tr)
        f_spec = pl.BlockSpec((tr, size), lambda q, i, cr: (i, 2 * q + cr[0]))
        s_spec = pl.BlockSpec((None, tr, size), lambda q, i, cr: (q, i, 0))
    else:
        nb, cols = full.shape[0], full.shape[2]
        tr = _tile_rows(size, cols)
        nt = size // tr
        grid = (4, nb * nt)
        f_spec = pl.BlockSpec((None, tr, cols), lambda q, i, cr: (i // nt, (2 * q + cr[0]) * nt + i % nt, 0))
        s_spec = pl.BlockSpec((None, None, tr, cols), lambda q, i, cr: (q, i // nt, i % nt, 0))

    def kern(c_ref, f_ref, r_ref, o_ref):
        del c_ref
        o_ref[...] = (f_ref[...] + r_ref[...]).astype(BF16)

    return pl.pallas_call(
        kern, name=name,
        grid_spec=pltpu.PrefetchScalarGridSpec(num_scalar_prefetch=1, grid=grid, in_specs=[f_spec, s_spec],
                                               out_specs=s_spec),
        out_shape=jax.ShapeDtypeStruct(recv_a.shape, BF16),
        compiler_params=_cparams(("arbitrary", "arbitrary"), 32),
    )(cidx, full, recv_a)


def _adam(w, g, m, v):
    m2 = ADAM_B1 * m + (1.0 - ADAM_B1) * g
    v2 = ADAM_B2 * v + (1.0 - ADAM_B2) * (g * g)
    m_hat = m2 / (1.0 - ADAM_B1 ** ADAM_STEP)
    v_hat = v2 / (1.0 - ADAM_B2 ** ADAM_STEP)
    delta = -ADAM_LR * (m_hat / (jnp.sqrt(v_hat) + ADAM_EPS) + ADAM_WD * w)
    return delta, m2, v2


def _rs_final_adam(name, parts, recv_b, w, m, v, fam, qidx):
    shp = w.shape
    if fam == "col":
        rows, cols = shp
        tr = _tile_rows(rows, cols)
        grid = (rows // tr,)
        w_spec = pl.BlockSpec((tr, cols), lambda i, qr: (i, 0))
        p_spec = pl.BlockSpec((None, tr, cols), lambda i, qr: (qr[0], i, 0))
        r_spec = pl.BlockSpec((3, tr, cols), lambda i, qr: (0, i, 0))
    else:
        nb, rows, cols = shp
        tr = _tile_rows(rows, cols)
        nt = rows // tr
        grid = (nb * nt,)
        w_spec = pl.BlockSpec((None, tr, cols), lambda i, qr: (i // nt, i % nt, 0))
        p_spec = pl.BlockSpec((None, None, tr, cols), lambda i, qr: (qr[0], i // nt, i % nt, 0))
        r_spec = pl.BlockSpec((3, None, tr, cols), lambda i, qr: (0, i // nt, i % nt, 0))

    def kern(q_ref, p_ref, r_ref, w_ref, m_ref, v_ref, g_out, d_out, m_out, v_out):
        del q_ref
        g = ((p_ref[...].astype(F32) + r_ref[0].astype(F32)) + r_ref[1].astype(F32)) + r_ref[2].astype(F32)
        delta, m2, v2 = _adam(w_ref[...], g, m_ref[...], v_ref[...])
        g_out[...] = g
        d_out[...] = delta
        m_out[...] = m2
        v_out[...] = v2

    sh = jax.ShapeDtypeStruct(shp, F32)
    return pl.pallas_call(
        kern, name=name,
        grid_spec=pltpu.PrefetchScalarGridSpec(
            num_scalar_prefetch=1, grid=grid, in_specs=[p_spec, r_spec, w_spec, w_spec, w_spec],
            out_specs=[w_spec] * 4),
        out_shape=[sh] * 4,
        compiler_params=_cparams(("arbitrary",), 32),
    )(qidx, parts, recv_b, w, m, v)


def _sum8(name, parts):
    def kern(p_ref, o_ref):
        acc = p_ref[0]
        for p in range(1, N_DEV):
            acc = acc + p_ref[p]
        o_ref[...] = acc

    return pl.pallas_call(
        kern, name=name, out_shape=jax.ShapeDtypeStruct(parts.shape[1:], F32),
        compiler_params=pltpu.CompilerParams(vmem_limit_bytes=32 << 20),
    )(parts)


def _adam_small(name, w, g, m, v):
    def kern(w_ref, g_ref, m_ref, v_ref, d_out, m_out, v_out):
        delta, m2, v2 = _adam(w_ref[...], g_ref[...], m_ref[...], v_ref[...])
        d_out[...] = delta
        m_out[...] = m2
        v_out[...] = v2

    sh = jax.ShapeDtypeStruct(w.shape, F32)
    return pl.pallas_call(kern, name=name, out_shape=[sh] * 3)(w, g, m, v)


def _local_step(x, target, wts, vec):
    seq, d = x.shape
    w_in, pool_w, wa, wx = wts["w_in"], wts["pool_w"], wts["lru_wa"], wts["lru_wx"]
    w_pu, w_lu, w_out, w1, w2 = wts["w_pool_up"], wts["w_lru_up"], wts["w_out"], wts["w_ff1"], wts["w_ff2"]
    sv = vec["sv"]
    ff = w1.shape[1]
    n_in = w_in.shape[1]
    blk = wa.shape[2]
    tc = min(512, seq)
    t1k, t512, t256 = min(1024, seq), min(512, seq), min(256, seq)
    n512 = min(512, d)
    tkd = d
    tkf = min(1024, ff)
    sub_rows = min(64, seq)

    x_bf = x.astype(BF16)
    full = lambda i, j, k: (0, 0)

    def epi_store(acc, i, j, ex, outs):
        outs[0][...] = acc

    (z,) = _mm("z_proj", "nn", (seq // t1k, n_in // n512, 1),
               x_bf, ((t1k, d), lambda i, j, k: (i, 0)), w_in, ((d, n512), lambda i, j, k: (0, j)),
               [], [jax.ShapeDtypeStruct((seq, n_in), F32)], [((t1k, n512), lambda i, j, k: (i, j))], epi_store)

    y_pool = _pool_fwd(z, pool_w, vec["pool_scale"], seq, d, tc)
    xc, h_f = _lru_fwd(z, sv, vec["conv_b"], wa, wx, seq, d, tc)
    h_b, y_lru = _lru_rev(z, sv, wa, wx, xc, h_f, seq, d, tc)
    m_bf, p_a, p_b = _merge(y_pool, w_pu, y_lru, w_lu, z, seq, d, t512, n512)

    def epi_ln1(acc, i, j, ex, outs):
        x_ref, bo, g1, b1 = ex
        s1 = DN_ALPHA * x_ref[...] + (acc + bo[...])
        xhat, rstd, x1 = _ln_fwd(s1, g1[...], b1[...])
        outs[0][...] = xhat
        outs[1][...] = x1.astype(BF16)
        outs[2][...] = rstd

    rowd = lambda t: ((t, d), lambda i, j, k: (i, 0))
    vecd = ((1, d), full)
    xhat1, x1_bf, rstd1 = _mm(
        "out_ln1", "nn", (seq // t256, 1, 1), m_bf, rowd(t256), w_out, ((d, d), full),
        [(x, rowd(t256)), (vec["b_out"], vecd), (vec["ln1_g"], vecd), (vec["ln1_b"], vecd)],
        [jax.ShapeDtypeStruct((seq, d), F32), jax.ShapeDtypeStruct((seq, d), BF16),
         jax.ShapeDtypeStruct((seq, 1), F32)],
        [rowd(t256), rowd(t256), ((t256, 1), lambda i, j, k: (i, 0))], epi_ln1)

    def epi_ff1(acc, i, j, ex, outs):
        r = jnp.maximum(acc + ex[0][...], 0.0)
        outs[0][...] = r.astype(BF16)
        outs[1][...] = (r * r).astype(BF16)

    tile_f = ((t1k, n512), lambda i, j, k: (i, j))
    relu_h, hdn = _mm(
        "ff1", "nn", (seq // t1k, ff // n512, 1), x1_bf, rowd(t1k), w1, ((d, n512), lambda i, j, k: (0, j)),
        [(vec["b_ff1"], ((1, n512), lambda i, j, k: (0, j)))],
        [jax.ShapeDtypeStruct((seq, ff), BF16)] * 2, [tile_f, tile_f], epi_ff1)

    def epi_ln2(acc, i, j, ex, outs):
        xh1, tgt, g1, b1, bf2, g2, b2 = ex
        ds_ref, dsb_ref, sm_ref, loss_ref = outs
        x1 = xh1[...] * g1[...] + b1[...]
        s2 = DN_ALPHA * x1 + (acc + bf2[...])
        xhat, rstd, y = _ln_fwd(s2, g2[...], b2[...])
        e = y - tgt[...]
        part = 0.5 * jnp.sum(jnp.mean(e * e, axis=-1, keepdims=True))
        dy = e * (1.0 / d)
        ds2 = _ln_bwd(dy, xhat, rstd, g2[...])
        ds_ref[...] = ds2
        dsb_ref[...] = ds2.astype(BF16)
        sm_ref[0:1, :] += _colsum(dy * xhat)
        sm_ref[1:2, :] += _colsum(dy)
        sm_ref[2:3, :] += _colsum(ds2)
        loss_ref[...] += jnp.full(loss_ref.shape, part, F32)

    def zero_tail(n_tail):
        def init(i, j, outs):
            @pl.when(i == 0)
            def _():
                for o in outs[-n_tail:]:
                    o[...] = jnp.zeros_like(o)
        return init

    ds2, ds2_bf, sm_ln2, loss_blk = _mm(
        "ff2_ln2", "nn", (seq // t512, 1, ff // tkf), hdn, ((t512, tkf), lambda i, j, k: (i, k)),
        w2, ((tkf, d), lambda i, j, k: (k, 0)),
        [(xhat1, rowd(t512)), (target, rowd(t512)), (vec["ln1_g"], vecd), (vec["ln1_b"], vecd),
         (vec["b_ff2"], vecd), (vec["ln2_g"], vecd), (vec["ln2_b"], vecd)],
        [jax.ShapeDtypeStruct((seq, d), F32), jax.ShapeDtypeStruct((seq, d), BF16),
         jax.ShapeDtypeStruct((SUBLANES, d), F32), jax.ShapeDtypeStruct((SUBLANES, 128), F32)],
        [rowd(t512), rowd(t512), ((SUBLANES, d), full), ((SUBLANES, 128), full)],
        epi_ln2, acc_shape=(t512, d), epi_init=zero_tail(2), sub=sub_rows)

    tkw = min(2048, seq)

    def dw(name, a, b, m_dim, n_dim, b_spec=None):
        tm, tn = min(1024, m_dim), min(1024, n_dim)
        b_spec = b_spec or ((tkw, tn), lambda i, j, k: (k, j))
        (out,) = _mm(name, "tn", (m_dim // tm, n_dim // tn, seq // tkw),
                     a, ((tkw, tm), lambda i, j, k: (k, i)), b, b_spec, [],
                     [jax.ShapeDtypeStruct((m_dim, n_dim), F32)], [((tm, tn), lambda i, j, k: (i, j))],
                     epi_store if seq == tkw else None)
        return out

    g_w2 = dw("dw_ff2", hdn, ds2_bf, ff, d)

    def epi_dpre(acc, i, j, ex, outs):
        dpre = acc * (2.0 * ex[0][...].astype(F32))
        outs[0][...] = dpre.astype(BF16)

        @pl.when(i == 0)
        def _():
            outs[1][...] = jnp.zeros_like(outs[1])

        outs[1][0:1, :] += _colsum(dpre)

    dpre, sm_bff1 = _mm(
        "dhdn", "nt", (seq // t1k, ff // n512, 1), ds2_bf, rowd(t1k), w2, ((n512, d), lambda i, j, k: (j, 0)),
        [(relu_h, tile_f)],
        [jax.ShapeDtypeStruct((seq, ff), BF16), jax.ShapeDtypeStruct((SUBLANES, ff), F32)],
        [tile_f, ((SUBLANES, n512), lambda i, j, k: (0, j))], epi_dpre, order="ji")

    g_w1 = dw("dw_ff1", x1_bf, dpre, d, ff)

    def epi_ln1b(acc, i, j, ex, outs):
        ds2_ref, xh1, rs1, g1 = ex
        ds_ref, dsb_ref, sm_ref = outs
        dy1 = acc + DN_ALPHA * ds2_ref[...]
        xhat = xh1[...]
        ds1 = _ln_bwd(dy1, xhat, rs1[...], g1[...])
        ds_ref[...] = ds1
        dsb_ref[...] = ds1.astype(BF16)
        sm_ref[0:1, :] += _colsum(dy1 * xhat)
        sm_ref[1:2, :] += _colsum(dy1)
        sm_ref[2:3, :] += _colsum(ds1)

    ds1, ds1_bf, sm_ln1 = _mm(
        "dx1_ln1", "nt", (seq // t512, 1, ff // tkf), dpre, ((t512, tkf), lambda i, j, k: (i, k)),
        w1, ((d, tkf), lambda i, j, k: (0, k)),
        [(ds2, rowd(t512)), (xhat1, rowd(t512)), (rstd1, ((t512, 1), lambda i, j, k: (i, 0))),
         (vec["ln1_g"], vecd)],
        [jax.ShapeDtypeStruct((seq, d), F32), jax.ShapeDtypeStruct((seq, d), BF16),
         jax.ShapeDtypeStruct((SUBLANES, d), F32)],
        [rowd(t512), rowd(t512), ((SUBLANES, d), full)], epi_ln1b, acc_shape=(t512, d),
        epi_init=zero_tail(1), sub=sub_rows)

    g_wout = dw("dw_out", m_bf, ds1_bf, d, d)
    n_n = d // n512
    tile_d = ((t512, n512), lambda i, j, k: (i, j))

    def epi_dm(acc, i, j, ex, outs):
        la, lb, pa, pb = ex
        ga, gb = _sig(la[...]), _sig(lb[...])
        outs[0][...] = (acc * ga).astype(BF16)
        outs[1][...] = (acc * gb).astype(BF16)
        outs[2][0] = (acc * pa[...].astype(F32) * ga * (1.0 - ga)).astype(BF16)
        outs[2][1] = (acc * pb[...].astype(F32) * gb * (1.0 - gb)).astype(BF16)

    dp_a, dp_b, dz = _mm(
        "dm", "nt", (seq // t512, n_n, 1), ds1_bf, rowd(t512), w_out, ((n512, d), lambda i, j, k: (j, 0)),
        [(z, ((t512, n512), lambda i, j, k: (i, 3 * n_n + j))),
         (z, ((t512, n512), lambda i, j, k: (i, 4 * n_n + j))), (p_a, tile_d), (p_b, tile_d)],
        [jax.ShapeDtypeStruct((seq, d), BF16), jax.ShapeDtypeStruct((seq, d), BF16),
         jax.ShapeDtypeStruct((5, seq, d), BF16)],
        [tile_d, tile_d, ((2, t512, n512), lambda i, j, k: (0, i, j))], epi_dm)

    g_wpu = dw("dw_pool_up", y_pool, dp_a, d, d)
    g_wlu = dw("dw_lru_up", y_lru, dp_b, d, d)

    def epi_bf(acc, i, j, ex, outs):
        outs[0][...] = acc.astype(BF16)

    (dy_pool,) = _mm("dy_pool", "nt", (seq // t512, n_n, 1), dp_a, rowd(t512), w_pu,
                     ((n512, d), lambda i, j, k: (j, 0)), [],
                     [jax.ShapeDtypeStruct((seq, d), BF16)], [tile_d], epi_bf)

    def epi_dylru(acc, i, j, ex, outs):
        hf, hb, ug, _ = ex
        u = ug[...]
        outs[0][...] = acc * _gelu(u)
        outs[1][...] = (acc * (hf[...] + hb[...]) * _gelu_grad(u)).astype(BF16)

    dh, dz = _mm(
        "dy_lru", "nt", (seq // t512, n_n, 1), dp_b, rowd(t512), w_lu, ((n512, d), lambda i, j, k: (j, 0)),
        [(h_f, tile_d), (h_b, tile_d), (z, ((t512, n512), lambda i, j, k: (i, 2 * n_n + j))), (dz, (None, None))],
        [jax.ShapeDtypeStruct((seq, d), F32), jax.ShapeDtypeStruct(dz.shape, BF16)],
        [tile_d, ((None, t512, n512), lambda i, j, k: (4, i, j))], epi_dylru, aliases={5: 1})

    dz, g_pw, sm_pool = _pool_bwd(z, dy_pool, pool_w, vec["pool_scale"], dz, seq, d, tc)
    dxc, g_wa0, g_wx0, sm_l0 = _lru_bwd(0, xc, dh, h_f, sv, wa, wx, None, seq, d, tc)
    dxc, g_wa1, g_wx1, sm_l1 = _lru_bwd(1, xc, dh, h_b, sv, wa, wx, dxc, seq, d, tc)
    dz, sm_conv = _conv_bwd(z, dxc, sv, dz, seq, d, tc, blk)

    tnw = min(1024, d)
    per_seg = d // tnw
    g_win = dw("dw_in", x_bf, dz, d, n_in,
               b_spec=((None, tkw, tnw), lambda i, j, k: ((j // per_seg + 2) % 5, k, j % per_seg)))

    nk = d // tkd

    def epi_dx(acc, i, j, ex, outs):
        outs[0][...] = acc + DN_ALPHA * ex[0][...]

    (grad_x,) = _mm(
        "dx", "nt", (seq // t512, 1, n_in // tkd), dz,
        ((None, t512, tkd), lambda i, j, k: ((k // nk + 2) % 5, i, k % nk)),
        w_in, ((d, tkd), lambda i, j, k: (0, k)), [(ds1, rowd(t512))],
        [jax.ShapeDtypeStruct((seq, d), F32)], [rowd(t512)], epi_dx, acc_shape=(t512, d))

    grads = {"w_in": g_win, "pool_w": g_pw, "lru_wa": jnp.concatenate([g_wa0, g_wa1], axis=0),
             "lru_wx": jnp.concatenate([g_wx0, g_wx1], axis=0), "w_pool_up": g_wpu, "w_lru_up": g_wlu,
             "w_out": g_wout, "w_ff1": g_w1, "w_ff2": g_w2}
    small = {"ln2": sm_ln2, "b_ff1": sm_bff1, "ln1": sm_ln1, "pool": sm_pool, "lru0": sm_l0, "lru1": sm_l1,
             "conv": sm_conv}
    return loss_blk[0, 0], grad_x, grads, small


REP = ("pool_scale", "conv_b", "b_out", "ln1_g", "ln1_b", "b_ff2", "ln2_g", "ln2_b")
SHARDED_SMALL = (("conv_w", 4), ("lru_ba", 2), ("lru_bx", 2), ("lru_lambda", 2))
WEIGHT_ORDER = ("w_in", "pool_w", "pool_scale", "conv_w", "conv_b", "lru_wa", "lru_ba", "lru_wx", "lru_bx",
                "lru_lambda", "w_pool_up", "w_lru_up", "w_out", "b_out", "ln1_g", "ln1_b", "w_ff1", "b_ff1",
                "w_ff2", "b_ff2", "ln2_g", "ln2_b")


def _pad_rows(a, rows):
    return jnp.concatenate([a, jnp.zeros((rows - a.shape[0], a.shape[1]), a.dtype)], axis=0)


def kernel(x, w_in, pool_w, pool_scale, conv_w, conv_b, lru_wa, lru_ba, lru_wx, lru_bx, lru_lambda, w_pool_up, w_lru_up, w_out, b_out, ln1_g, ln1_b, w_ff1, b_ff1, w_ff2, b_ff2, ln2_g, ln2_b, loss_target, m_w_in, m_pool_w, m_pool_scale, m_conv_w, m_conv_b, m_lru_wa, m_lru_ba, m_lru_wx, m_lru_bx, m_lru_lambda, m_w_pool_up, m_w_lru_up, m_w_out, m_b_out, m_ln1_g, m_ln1_b, m_w_ff1, m_b_ff1, m_w_ff2, m_b_ff2, m_ln2_g, m_ln2_b, v_w_in, v_pool_w, v_pool_scale, v_conv_w, v_conv_b, v_lru_wa, v_lru_ba, v_lru_wx, v_lru_bx, v_lru_lambda, v_w_pool_up, v_w_lru_up, v_w_out, v_b_out, v_ln1_g, v_ln1_b, v_w_ff1, v_b_ff1, v_w_ff2, v_b_ff2, v_ln2_g, v_ln2_b):
    args = dict(locals())
    w = {n: args[n] for n in WEIGHT_ORDER}
    mom = {n: args["m_" + n] for n in WEIGHT_ORDER}
    var = {n: args["v_" + n] for n in WEIGHT_ORDER}
    seq, d = x.shape[1], x.shape[2]
    n_heads, blk = lru_wa.shape[2], lru_wa.shape[4]
    n_groups = pool_w.shape[1]
    ff = b_ff1.shape[1]
    cx, cy, cc = _coords()
    me = 4 * cx + 2 * cy + cc
    cidx = jnp.reshape(cc, (1,)).astype(jnp.int32)
    qidx = jnp.reshape(2 * cx + cy, (1,)).astype(jnp.int32)

    fams = [fam for _, fam in BIG]
    shards = [_shard_view(w[n], fam).astype(BF16) for n, fam in BIG]
    sv_shard = _pad_rows(jnp.concatenate([w[n].reshape(r, -1) for n, r in SHARDED_SMALL], axis=0), 16)
    gathered = _all_gather("ag_weights", shards + [sv_shard], fams + ["col"])
    wts = dict(zip([n for n, _ in BIG], gathered[:-1]))
    wts["pool_w"] = wts["pool_w"].reshape(n_groups, d // n_groups, d // n_groups)
    wts["lru_wa"] = wts["lru_wa"].reshape(2, n_heads, blk, blk)
    wts["lru_wx"] = wts["lru_wx"].reshape(2, n_heads, blk, blk)
    for n in ("w_pool_up", "w_lru_up", "w_out", "w_ff2"):
        wts[n] = wts[n].reshape(wts[n].shape[1:])
    vec = {n: w[n] for n in REP}
    vec["b_ff1"] = b_ff1
    vec["sv"] = gathered[-1]

    loss_part, grad_x, grads, small = _local_step(x.reshape(seq, d), loss_target.reshape(seq, d), wts, vec)
    loss = lax.psum(loss_part, AXES)

    gfull = []
    for n, fam in BIG:
        g = grads[n]
        gfull.append(g if fam == "col" else g.reshape((-1,) + g.shape[-2:]))
    sviews = {n: _shard_view(w[n], fam) for n, fam in BIG}
    sizes = [sviews[n].shape[1] for n, _ in BIG]
    recv_a = _rs_sibling("rs_sibling", gfull, fams, sizes)
    parts = [_rs_add("rs_add_" + n, gfull[a], recv_a[a], fam, sizes[a], cidx) for a, (n, fam) in enumerate(BIG)]
    recv_b = _rs_chips("rs_chips", parts)
    out_g, out_d, out_m, out_v = {}, {}, {}, {}
    for a, (n, fam) in enumerate(BIG):
        res = _rs_final_adam("adam_" + n, parts[a], recv_b[a], sviews[n], _shard_view(mom[n], fam),
                             _shard_view(var[n], fam), fam, qidx)
        out_g[n], out_d[n], out_m[n], out_v[n] = [r.reshape(w[n].shape) for r in res]

    rows = [small["pool"][0:1], small["conv"][4:5], small["ln1"][2:3], small["ln1"][0:1], small["ln1"][1:2],
            small["ln2"][2:3], small["ln2"][0:1], small["ln2"][1:2], small["b_ff1"][0:1].reshape(ff // d, d),
            small["conv"][0:4], small["lru0"][0:1], small["lru1"][0:1], small["lru0"][1:2], small["lru1"][1:2],
            small["lru0"][2:3], small["lru1"][2:3]]
    n_rep = len(REP) + ff // d
    n_rows = n_rep + sum(r for _, r in SHARDED_SMALL)
    pad_rows = -(-n_rows // SUBLANES) * SUBLANES
    packed = _pad_rows(jnp.concatenate(rows, axis=0), pad_rows)
    (all_small,) = _all_gather("ag_small", [packed], ["lead"])
    g_small = _sum8("sum_small", all_small)

    def pack_rep(t):
        return jnp.concatenate([t[n] for n in REP] + [t["b_ff1"].reshape(ff // d, d)], axis=0)

    def pack_sh(t):
        return jnp.concatenate([t[n].reshape(r, -1) for n, r in SHARDED_SMALL], axis=0)

    g_rep = g_small[:n_rep]
    cs = d // N_DEV
    g_sh = lax.dynamic_slice_in_dim(g_small[n_rep:n_rows], me * cs, cs, axis=1)
    d_rep, m_rep, v_rep = _adam_small("adam_rep", pack_rep(w), g_rep, pack_rep(mom), pack_rep(var))
    d_sh, m_sh, v_sh = _adam_small("adam_sharded", pack_sh(w), g_sh, pack_sh(mom), pack_sh(var))

    def unpack(rep_t, sh_t, dst):
        for i, n in enumerate(REP):
            dst[n] = rep_t[i:i + 1].reshape(w[n].shape)
        dst["b_ff1"] = rep_t[len(REP):n_rep].reshape(w["b_ff1"].shape)
        r0 = 0
        for n, r in SHARDED_SMALL:
            dst[n] = sh_t[r0:r0 + r].reshape(w[n].shape)
            r0 += r

    unpack(g_rep, g_sh, out_g)
    unpack(d_rep, d_sh, out_d)
    unpack(m_rep, m_sh, out_m)
    unpack(v_rep, v_sh, out_v)

    outs = [loss, grad_x.reshape(x.shape)]
    for t in (out_g, out_d, out_m, out_v):
        outs += [t[n] for n in WEIGHT_ORDER]
    return tuple(outs)
```

```python
import functools

import jax
import jax.numpy as jnp
from jax import lax
from jax.experimental import pallas as pl
from jax.experimental.pallas import tpu as pltpu

F32 = jnp.float32
BF16 = jnp.bfloat16
MESH = pl.DeviceIdType.MESH
AXES = ("x", "y", "c")
N_DEV = 8

DN_ALPHA = 2.0 ** 0.25
LN_EPS = 1e-5
LRU_C = 8.0
ADAM_LR = 0.001
ADAM_B1 = 0.9
ADAM_B2 = 0.999
ADAM_EPS = 1e-08
ADAM_WD = 0.01
ADAM_STEP = 10
GELU_C = 0.7978845608028654
GELU_K = 0.044715

HALO = 16
SUBLANES = 8
VMEM_MB = 56


def _cparams(sem, vmem_mb=VMEM_MB):
    return pltpu.CompilerParams(dimension_semantics=sem, vmem_limit_bytes=vmem_mb << 20)


HBM_SPEC = pl.BlockSpec(memory_space=pl.ANY)


class _Job:
    def __init__(self, ins, outs, sems, start, finish, mid=None):
        self.ins, self.outs, self.sems = list(ins), list(outs), list(sems)
        self.start, self.finish, self.mid = start, finish, mid
        self.results = None


def _pcall(name, body, grid, in_specs, out_specs, out_shape, inputs, scratch=(), aliases=None,
           vmem_mb=VMEM_MB, job=None):
    in_specs, out_specs, out_shape, scratch = list(in_specs), list(out_specs), list(out_shape), list(scratch)
    params = _cparams(("arbitrary",) * len(grid), vmem_mb)
    if job is None:
        return pl.pallas_call(body, name=name, grid=grid, in_specs=in_specs, out_specs=out_specs,
                              out_shape=out_shape, scratch_shapes=scratch,
                              input_output_aliases=aliases or {}, compiler_params=params)(*inputs)
    n_in, n_out, n_scr = len(inputs), len(out_shape), len(scratch)
    ji, jo = len(job.ins), len(job.outs)
    total = 1
    for g in grid:
        total *= g
    mid_step = (3 * total) // 4 if total >= 4 else None

    def wrapped(*refs):
        p = 0
        ins = refs[p:p + n_in]
        p += n_in
        jins = refs[p:p + ji]
        p += ji
        outs = refs[p:p + n_out]
        p += n_out
        jouts = refs[p:p + jo]
        p += jo
        scr = refs[p:p + n_scr]
        sems = refs[p + n_scr:]
        step = pl.program_id(0)
        for ax in range(1, len(grid)):
            step = step * grid[ax] + pl.program_id(ax)

        @pl.when(step == 0)
        def _():
            job.start(jins, jouts, sems)

        if job.mid is not None and mid_step is not None:
            @pl.when(step == mid_step)
            def _():
                job.mid(jins, jouts, sems)

        body(*ins, *outs, *scr)

        @pl.when(step == total - 1)
        def _():
            if job.mid is not None and mid_step is None:
                job.mid(jins, jouts, sems)
            job.finish(jins, jouts, sems)

    res = pl.pallas_call(
        wrapped, name=name, grid=grid, in_specs=in_specs + [HBM_SPEC] * ji,
        out_specs=out_specs + [HBM_SPEC] * jo, out_shape=out_shape + job.outs,
        scratch_shapes=scratch + job.sems, input_output_aliases=aliases or {}, compiler_params=params,
    )(*inputs, *job.ins)
    job.results = list(res[n_out:])
    return list(res[:n_out])


def _run_job(name, job):
    ji, jo = len(job.ins), len(job.outs)

    def body(*refs):
        jins, jouts, sems = refs[:ji], refs[ji:ji + jo], refs[ji + jo:]
        job.start(jins, jouts, sems)
        if job.mid is not None:
            job.mid(jins, jouts, sems)
        job.finish(jins, jouts, sems)

    res = pl.pallas_call(body, name=name, in_specs=[HBM_SPEC] * ji, out_specs=[HBM_SPEC] * jo,
                         out_shape=job.outs, scratch_shapes=job.sems)(*job.ins)
    job.results = list(res)
    return job.results


def _dot(mode, a, b):
    if mode == "nn":
        dims = (((1,), (0,)), ((), ()))
    elif mode == "nt":
        dims = (((1,), (1,)), ((), ()))
    else:
        dims = (((0,), (0,)), ((), ()))
    return lax.dot_general(a, b, dims, preferred_element_type=F32)


def _sig(x):
    return 1.0 / (1.0 + jnp.exp(-x))


def _gelu(x):
    t = jnp.tanh(GELU_C * (x + GELU_K * x * x * x))
    return 0.5 * x * (1.0 + t)


def _gelu_grad(x):
    x2 = x * x
    t = jnp.tanh(GELU_C * (x + GELU_K * x * x2))
    return 0.5 * (1.0 + t) + 0.5 * x * (1.0 - t * t) * GELU_C * (1.0 + 3.0 * GELU_K * x2)


def _colsum(v):
    return jnp.sum(v, axis=0, keepdims=True)


def _mm(name, mode, grid, a, a_spec, b, b_spec, extras, out_shapes, out_specs, epi, *,
        order="ij", acc_shape=None, aliases=None, vmem_mb=VMEM_MB, epi_init=None, sub=None, job=None):
    gm, gn, gk = grid

    def spec(s):
        bs, f = s
        if f is None:
            return pl.BlockSpec(memory_space=pl.ANY)
        if order == "ij":
            return pl.BlockSpec(bs, lambda i, j, k, f=f: f(i, j, k))
        return pl.BlockSpec(bs, lambda j, i, k, f=f: f(i, j, k))

    ne, no = len(extras), len(out_shapes)

    def kern(*refs):
        a_ref, b_ref = refs[0], refs[1]
        ex = refs[2:2 + ne]
        outs = refs[2 + ne:2 + ne + no]
        if order == "ij":
            i, j = pl.program_id(0), pl.program_id(1)
        else:
            j, i = pl.program_id(0), pl.program_id(1)
        k = pl.program_id(2)
        prod = _dot(mode, a_ref[...], b_ref[...])
        if gk == 1:
            if epi_init is not None:
                epi_init(i, j, outs)
            epi(prod, i, j, ex, outs)
        elif epi is None:
            @pl.when(k == 0)
            def _():
                outs[0][...] = prod

            @pl.when(k > 0)
            def _():
                outs[0][...] += prod
        else:
            acc = refs[-1]

            @pl.when(k == 0)
            def _():
                acc[...] = prod

            @pl.when(k > 0)
            def _():
                acc[...] += prod

            @pl.when(k == gk - 1)
            def _():
                if epi_init is not None:
                    epi_init(i, j, outs)
                if sub is None:
                    epi(acc[...], i, j, ex, outs)
                else:
                    tm = acc_shape[0]

                    def rows_of(r, rs):
                        return r.at[rs, :] if r.shape[0] == tm else r

                    def blk(t, carry):
                        rs = pl.ds(pl.multiple_of(t * sub, sub), sub)
                        epi(acc[rs, :], i, j, [rows_of(r, rs) for r in ex], [rows_of(r, rs) for r in outs])
                        return carry

                    lax.fori_loop(0, tm // sub, blk, 0)

    g = (gm, gn, gk) if order == "ij" else (gn, gm, gk)
    scratch = [pltpu.VMEM(acc_shape, F32)] if gk > 1 and epi is not None else []
    return _pcall(name, kern, g, [spec(a_spec), spec(b_spec)] + [spec(s) for _, s in extras],
                  [spec(s) for s in out_specs], out_shapes, [a, b] + [e for e, _ in extras],
                  scratch=scratch, aliases=aliases, vmem_mb=vmem_mb, job=job)


def _ext(ref, c, n_chunks, tc, seq):
    c0 = pl.multiple_of(c * tc, tc)
    body = ref[pl.ds(c0, tc), :].astype(F32)
    t0 = pl.multiple_of(jnp.maximum(c0 - HALO, 0), HALO)
    b0 = pl.multiple_of(jnp.minimum(c0 + tc, seq - HALO), HALO)
    top = ref[pl.ds(t0, HALO), :].astype(F32)
    bot = ref[pl.ds(b0, HALO), :].astype(F32)
    top = jnp.where(c > 0, top, 0.0)
    bot = jnp.where(c < n_chunks - 1, bot, 0.0)
    return jnp.concatenate([top, body, bot], axis=0)


def _shifted(vext, off, tc):
    n = vext.shape[0]
    r = vext if off == 0 else pltpu.roll(vext, (n - off) % n, 0)
    return r[HALO:HALO + tc]


def _win_sum(vext, g, extra, tc):
    s2 = vext + pltpu.roll(vext, 1, 0)
    s4 = s2 + pltpu.roll(s2, 2, 0)
    s8 = s4 + pltpu.roll(s4, 4, 0)
    s16 = s8 + pltpu.roll(s8, 8, 0)
    outs = [_shifted(s, extra + hw - 1, tc) for s, hw in ((s2, 1), (s4, 2), (s8, 4), (s16, 8))]
    return jnp.where(g == 0, outs[0], jnp.where(g == 1, outs[1], jnp.where(g == 2, outs[2], outs[3])))


def _win_cnt(t, hw, seq):
    return (jnp.minimum(t + hw, seq) - jnp.maximum(t - hw, 0)).astype(F32)


def _pool_d(uext, g, c, tc, seq):
    hw = jnp.left_shift(1, g)
    t = c * tc + lax.broadcasted_iota(jnp.int32, (tc, 1), 0)
    ws = _win_sum(uext, g, 0, tc)
    return ws / _win_cnt(t, hw, seq) - uext[HALO:HALO + tc]


def _scan_tiles(a_ref, b_ref, h_ref, carry_ref, n_tiles, reverse):
    blk = a_ref.shape[1]
    row = lax.broadcasted_iota(jnp.int32, (SUBLANES, blk), 0)

    def tile(j, hc):
        jj = (n_tiles - 1 - j) if reverse else j
        off = pl.multiple_of(jj * SUBLANES, SUBLANES)
        a = a_ref[pl.ds(off, SUBLANES), :]
        b = b_ref[pl.ds(off, SUBLANES), :]
        for kk in (1, 2, 4):
            sh = (SUBLANES - kk) if reverse else kk
            a_s = pltpu.roll(a, sh, 0)
            b_s = pltpu.roll(b, sh, 0)
            m = (row < SUBLANES - kk) if reverse else (row >= kk)
            a_s = jnp.where(m, a_s, 1.0)
            b_s = jnp.where(m, b_s, 0.0)
            b = a * b_s + b
            a = a * a_s
        h = a * hc + b
        h_ref[pl.ds(off, SUBLANES), :] = h
        return h[0:1, :] if reverse else h[SUBLANES - 1:SUBLANES, :]

    hc = lax.fori_loop(0, n_tiles, tile, carry_ref[0:1, :])
    carry_ref[0:1, :] = hc


def _lru_k(lam):
    y = -lam
    e = jnp.exp(-jnp.abs(y))
    u = 1.0 + e
    l1p = jnp.where(u == 1.0, e, jnp.log(u) * (e / (u - 1.0)))
    return -LRU_C * (jnp.maximum(y, 0.0) + l1p)


def _lru_gates(xc, wa, wx, ba, bx, lam):
    xb = xc.astype(BF16)
    r = _sig(jnp.dot(xb, wa, preferred_element_type=F32) + ba)
    i = _sig(jnp.dot(xb, wx, preferred_element_type=F32) + bx)
    k = _lru_k(lam)
    la = k * r
    a = jnp.exp(la)
    s = jnp.sqrt(-jnp.tanh(la) * (a * a + 1.0))
    return r, i, k, a, s


SV_CONV, SV_BA, SV_BX, SV_LAM = 0, 4, 6, 8


def _pool_fwd(z, pw, scale, seq, d, tc):
    n_g = pw.shape[0]
    pg = d // n_g
    n_chunks = seq // tc

    def kern(z_ref, pw_ref, sc_ref, y_ref):
        g, c = pl.program_id(0), pl.program_id(1)
        uext = _ext(z_ref, c, n_chunks, tc, seq)
        dd = _pool_d(uext, g, c, tc, seq)
        q = jnp.dot(dd.astype(BF16), pw_ref[...], preferred_element_type=F32)
        y_ref[...] = (q * sc_ref[...]).astype(BF16)

    return pl.pallas_call(
        kern, name="pool_fwd", grid=(n_g, n_chunks),
        in_specs=[pl.BlockSpec((seq, pg), lambda g, c: (0, g)),
                  pl.BlockSpec((None, pg, pg), lambda g, c: (g, 0, 0)),
                  pl.BlockSpec((1, pg), lambda g, c: (0, g))],
        out_specs=pl.BlockSpec((tc, pg), lambda g, c: (c, g)),
        out_shape=jax.ShapeDtypeStruct((seq, d), BF16),
        compiler_params=_cparams(("arbitrary", "arbitrary")),
    )(z, pw, scale)


def _lru_fwd(z, sv, conv_b, wa, wx, seq, d, tc, job=None):
    n_h, blk = wa.shape[1], wa.shape[2]
    n_chunks = seq // tc
    lru_off = d // blk

    def kern(z_ref, sv_ref, cb_ref, wa_ref, wx_ref, xc_ref, h_ref, a_s, b_s, carry):
        c = pl.program_id(1)
        uext = _ext(z_ref, c, n_chunks, tc, seq)
        xc = cb_ref[...]
        for k in range(4):
            xc = xc + _shifted(uext, k - 2, tc) * sv_ref[SV_CONV + k:SV_CONV + k + 1, :]
        xc_ref[...] = xc
        _, i, _, a, s = _lru_gates(xc, wa_ref[...], wx_ref[...], sv_ref[SV_BA:SV_BA + 1, :],
                                   sv_ref[SV_BX:SV_BX + 1, :], sv_ref[SV_LAM:SV_LAM + 1, :])
        a_s[...] = a
        b_s[...] = s * (i * xc)

        @pl.when(c == 0)
        def _():
            carry[...] = jnp.zeros_like(carry)

        _scan_tiles(a_s, b_s, h_ref, carry, tc // SUBLANES, False)

    col = lambda h, c: (c, h)
    return _pcall(
        "lru_fwd", kern, (n_h, n_chunks),
        [pl.BlockSpec((seq, blk), lambda h, c: (0, lru_off + h)),
         pl.BlockSpec((16, blk), lambda h, c: (0, h)),
         pl.BlockSpec((1, blk), lambda h, c: (0, h)),
         pl.BlockSpec((None, None, blk, blk), lambda h, c: (0, h, 0, 0)),
         pl.BlockSpec((None, None, blk, blk), lambda h, c: (0, h, 0, 0))],
        [pl.BlockSpec((tc, blk), col), pl.BlockSpec((tc, blk), col)],
        [jax.ShapeDtypeStruct((seq, d), F32), jax.ShapeDtypeStruct((seq, d), F32)],
        [z, sv, conv_b, wa, wx],
        scratch=[pltpu.VMEM((tc, blk), F32), pltpu.VMEM((tc, blk), F32), pltpu.VMEM((SUBLANES, blk), F32)],
        job=job)


def _lru_rev(z, sv, wa, wx, xc, h_f, seq, d, tc, job=None):
    n_h, blk = wa.shape[1], wa.shape[2]
    n_chunks = seq // tc
    gate_off = 2 * d // blk

    def kern(z_ref, sv_ref, wa_ref, wx_ref, xc_ref, hf_ref, hb_ref, y_ref, a_s, b_s, carry):
        c = pl.program_id(1)
        xc = xc_ref[...]
        _, i, _, a, s = _lru_gates(xc, wa_ref[...], wx_ref[...], sv_ref[SV_BA + 1:SV_BA + 2, :],
                                   sv_ref[SV_BX + 1:SV_BX + 2, :], sv_ref[SV_LAM + 1:SV_LAM + 2, :])
        a_s[...] = a
        b_s[...] = s * (i * xc)

        @pl.when(c == 0)
        def _():
            carry[...] = jnp.zeros_like(carry)

        _scan_tiles(a_s, b_s, hb_ref, carry, tc // SUBLANES, True)
        y_ref[...] = ((hf_ref[...] + hb_ref[...]) * _gelu(z_ref[...])).astype(BF16)

    rev = lambda h, c: (n_chunks - 1 - c, h)
    return _pcall(
        "lru_rev", kern, (n_h, n_chunks),
        [pl.BlockSpec((tc, blk), lambda h, c: (n_chunks - 1 - c, gate_off + h)),
         pl.BlockSpec((16, blk), lambda h, c: (0, h)),
         pl.BlockSpec((None, None, blk, blk), lambda h, c: (1, h, 0, 0)),
         pl.BlockSpec((None, None, blk, blk), lambda h, c: (1, h, 0, 0)),
         pl.BlockSpec((tc, blk), rev), pl.BlockSpec((tc, blk), rev)],
        [pl.BlockSpec((tc, blk), rev), pl.BlockSpec((tc, blk), rev)],
        [jax.ShapeDtypeStruct((seq, d), F32), jax.ShapeDtypeStruct((seq, d), BF16)],
        [z, sv, wa, wx, xc, h_f],
        scratch=[pltpu.VMEM((tc, blk), F32), pltpu.VMEM((tc, blk), F32), pltpu.VMEM((SUBLANES, blk), F32)],
        job=job)


def _merge(y_pool, w_pu, y_lru, w_lu, z, seq, d, tm, tn):
    n_n = d // tn

    def kern(yp_ref, wp_ref, yl_ref, wl_ref, la_ref, lb_ref, m_ref, pa_ref, pb_ref):
        pa = jnp.dot(yp_ref[...], wp_ref[...], preferred_element_type=F32)
        pb = jnp.dot(yl_ref[...], wl_ref[...], preferred_element_type=F32)
        m_ref[...] = (_sig(la_ref[...]) * pa + _sig(lb_ref[...]) * pb).astype(BF16)
        pa_ref[...] = pa.astype(BF16)
        pb_ref[...] = pb.astype(BF16)

    row = pl.BlockSpec((tm, d), lambda i, j: (i, 0))
    wcol = pl.BlockSpec((d, tn), lambda i, j: (0, j))
    out = pl.BlockSpec((tm, tn), lambda i, j: (i, j))
    sh = jax.ShapeDtypeStruct((seq, d), BF16)
    return pl.pallas_call(
        kern, name="merge", grid=(seq // tm, n_n),
        in_specs=[row, wcol, row, wcol,
                  pl.BlockSpec((tm, tn), lambda i, j: (i, 3 * n_n + j)),
                  pl.BlockSpec((tm, tn), lambda i, j: (i, 4 * n_n + j))],
        out_specs=[out, out, out], out_shape=[sh, sh, sh],
        compiler_params=_cparams(("arbitrary", "arbitrary")),
    )(y_pool, w_pu, y_lru, w_lu, z, z)


def _ln_fwd(s, g, b):
    mu = jnp.mean(s, axis=-1, keepdims=True)
    xc = s - mu
    var = jnp.mean(xc * xc, axis=-1, keepdims=True)
    rstd = lax.rsqrt(var + LN_EPS)
    xhat = xc * rstd
    return xhat, rstd, xhat * g + b


def _ln_bwd(dy, xhat, rstd, g):
    dyg = dy * g
    m1 = jnp.mean(dyg, axis=-1, keepdims=True)
    m2 = jnp.mean(dyg * xhat, axis=-1, keepdims=True)
    return rstd * (dyg - m1 - xhat * m2)


def _pool_bwd(z, dy_pool, pw, scale, dz, seq, d, tc):
    n_g = pw.shape[0]
    pg = d // n_g
    n_chunks = seq // tc

    def kern(z_ref, dy_ref, pw_ref, sc_ref, dz_in, dz_ref, dpw_ref, dsc_ref):
        del dz_in
        g, c = pl.program_id(0), pl.program_id(1)
        hw = jnp.left_shift(1, g)
        uext = _ext(z_ref, c, n_chunks, tc, seq)
        dd = _pool_d(uext, g, c, tc, seq).astype(BF16)
        pwv = pw_ref[...]
        q = jnp.dot(dd, pwv, preferred_element_type=F32)
        dyext = _ext(dy_ref, c, n_chunks, tc, seq)

        @pl.when(c == 0)
        def _():
            dsc_ref[...] = jnp.zeros_like(dsc_ref)
            dpw_ref[...] = jnp.zeros_like(dpw_ref)

        dsc_ref[0:1, :] += _colsum(dyext[HALO:HALO + tc] * q)
        dqext = (dyext * sc_ref[...]).astype(BF16)
        dpw_ref[...] += _dot("tn", dd, dqext[HALO:HALO + tc])
        ddext = _dot("nt", dqext, pwv)
        text = c * tc - HALO + lax.broadcasted_iota(jnp.int32, (tc + 2 * HALO, 1), 0)
        v = ddext / jnp.maximum(_win_cnt(text, hw, seq), 1.0)
        dz_ref[...] = (_win_sum(v, g, 1, tc) - ddext[HALO:HALO + tc]).astype(BF16)

    return pl.pallas_call(
        kern, name="pool_bwd", grid=(n_g, n_chunks),
        in_specs=[pl.BlockSpec((seq, pg), lambda g, c: (0, g)),
                  pl.BlockSpec((seq, pg), lambda g, c: (0, g)),
                  pl.BlockSpec((None, pg, pg), lambda g, c: (g, 0, 0)),
                  pl.BlockSpec((1, pg), lambda g, c: (0, g)),
                  pl.BlockSpec(memory_space=pl.ANY)],
        out_specs=[pl.BlockSpec((None, tc, pg), lambda g, c: (2, c, g)),
                   pl.BlockSpec((None, pg, pg), lambda g, c: (g, 0, 0)),
                   pl.BlockSpec((SUBLANES, pg), lambda g, c: (0, g))],
        out_shape=[jax.ShapeDtypeStruct(dz.shape, dz.dtype),
                   jax.ShapeDtypeStruct((n_g, pg, pg), F32),
                   jax.ShapeDtypeStruct((SUBLANES, d), F32)],
        input_output_aliases={4: 0},
        compiler_params=_cparams(("arbitrary", "arbitrary")),
    )(z, dy_pool, pw, scale, dz)


def _lru_bwd(direction, xc, dh, h_dir, sv, wa, wx, dxc_prev, seq, d, tc, job=None):
    reverse = direction == 1
    n_h, blk = wa.shape[1], wa.shape[2]
    n_chunks = seq // tc
    has_prev = dxc_prev is not None

    def kern(*refs):
        xc_ref, dh_ref, h_ref, sv_ref, wa_ref, wx_ref = refs[:6]
        p = 6
        prev_ref = None
        if has_prev:
            prev_ref = refs[p]
            p += 1
        dxc_ref, dwa_ref, dwx_ref, sm_ref, at_s, g_s, carry, acarry = refs[p:p + 8]
        c = pl.program_id(1)
        cr = c if reverse else n_chunks - 1 - c
        c0 = pl.multiple_of(cr * tc, tc)

        @pl.when(c == 0)
        def _():
            carry[...] = jnp.zeros_like(carry)
            acarry[...] = jnp.zeros_like(acarry)
            dwa_ref[...] = jnp.zeros_like(dwa_ref)
            dwx_ref[...] = jnp.zeros_like(dwx_ref)
            sm_ref[...] = jnp.zeros_like(sm_ref)

        xc = xc_ref[...]
        wav, wxv = wa_ref[...], wx_ref[...]
        lam = sv_ref[SV_LAM + direction:SV_LAM + direction + 1, :]
        r, i, k, a, s = _lru_gates(xc, wav, wxv, sv_ref[SV_BA + direction:SV_BA + direction + 1, :],
                                   sv_ref[SV_BX + direction:SV_BX + direction + 1, :], lam)
        rowi = lax.broadcasted_iota(jnp.int32, (tc, blk), 0)
        hbody = h_ref[pl.ds(c0, tc), :]
        if not reverse:
            p0 = pl.multiple_of(jnp.maximum(c0 - SUBLANES, 0), SUBLANES)
            edge = jnp.where(cr > 0, h_ref[pl.ds(p0, SUBLANES), :][SUBLANES - 1:SUBLANES, :], 0.0)
            hprev = jnp.where(rowi == 0, edge, pltpu.roll(hbody, 1, 0))
            at = jnp.where(rowi == tc - 1, acarry[0:1, :], pltpu.roll(a, tc - 1, 0))
        else:
            n0 = pl.multiple_of(jnp.minimum(c0 + tc, seq - SUBLANES), SUBLANES)
            edge = jnp.where(cr < n_chunks - 1, h_ref[pl.ds(n0, SUBLANES), :][0:1, :], 0.0)
            hprev = jnp.where(rowi == tc - 1, edge, pltpu.roll(hbody, tc - 1, 0))
            at = jnp.where(rowi == 0, acarry[0:1, :], pltpu.roll(a, 1, 0))
        at_s[...] = at
        _scan_tiles(at_s, dh_ref, g_s, carry, tc // SUBLANES, not reverse)
        acarry[0:1, :] = a[tc - 1:tc, :] if reverse else a[0:1, :]

        gt = g_s[...]
        da = gt * hprev
        di = gt * s * xc
        dxc = gt * s * i
        ds = gt * (i * xc)
        dl = da * a - ds * (a * a) / s
        dpr = (dl * k) * r * (1.0 - r)
        dpi = di * i * (1.0 - i)
        sm_ref[0:1, :] += _colsum(dpr)
        sm_ref[1:2, :] += _colsum(dpi)
        sm_ref[2:3, :] += _colsum(dl * r) * (LRU_C * _sig(-lam))
        xb, dprb, dpib = xc.astype(BF16), dpr.astype(BF16), dpi.astype(BF16)
        dwa_ref[...] += _dot("tn", xb, dprb)
        dwx_ref[...] += _dot("tn", xb, dpib)
        dxc = dxc + _dot("nt", dprb, wav) + _dot("nt", dpib, wxv)
        if has_prev:
            dxc = dxc + prev_ref[...]
        dxc_ref[...] = dxc

    if reverse:
        chunk = lambda h, c: (c, h)
    else:
        chunk = lambda h, c: (n_chunks - 1 - c, h)
    wspec = pl.BlockSpec((None, None, blk, blk), lambda h, c: (direction, h, 0, 0))
    ins = [xc, dh, h_dir, sv, wa, wx] + ([dxc_prev] if has_prev else [])
    in_specs = [pl.BlockSpec((tc, blk), chunk), pl.BlockSpec((tc, blk), chunk),
                pl.BlockSpec((seq, blk), lambda h, c: (0, h)),
                pl.BlockSpec((16, blk), lambda h, c: (0, h)), wspec, wspec]
    if has_prev:
        in_specs.append(pl.BlockSpec((tc, blk), chunk))
    return _pcall(
        "lru_bwd_%d" % direction, kern, (n_h, n_chunks), in_specs,
        [pl.BlockSpec((tc, blk), chunk),
         pl.BlockSpec((None, blk, blk), lambda h, c: (h, 0, 0)),
         pl.BlockSpec((None, blk, blk), lambda h, c: (h, 0, 0)),
         pl.BlockSpec((SUBLANES, blk), lambda h, c: (0, h))],
        [jax.ShapeDtypeStruct((seq, d), F32),
         jax.ShapeDtypeStruct((n_h, blk, blk), F32),
         jax.ShapeDtypeStruct((n_h, blk, blk), F32),
         jax.ShapeDtypeStruct((SUBLANES, d), F32)],
        ins,
        scratch=[pltpu.VMEM((tc, blk), F32), pltpu.VMEM((tc, blk), F32),
                 pltpu.VMEM((SUBLANES, blk), F32), pltpu.VMEM((SUBLANES, blk), F32)],
        job=job)


def _conv_bwd(z, dxc, sv, dz, seq, d, tc, tcol):
    n_chunks = seq // tc
    lru_off = d // tcol

    def kern(z_ref, dx_ref, sv_ref, dz_in, dz_ref, sm_ref):
        del dz_in
        c = pl.program_id(1)
        uext = _ext(z_ref, c, n_chunks, tc, seq)
        dext = _ext(dx_ref, c, n_chunks, tc, seq)
        dbody = dext[HALO:HALO + tc]

        @pl.when(c == 0)
        def _():
            sm_ref[...] = jnp.zeros_like(sm_ref)

        du = jnp.zeros_like(dbody)
        for k in range(4):
            du = du + _shifted(dext, 2 - k, tc) * sv_ref[SV_CONV + k:SV_CONV + k + 1, :]
            sm_ref[k:k + 1, :] += _colsum(dbody * _shifted(uext, k - 2, tc))
        sm_ref[4:5, :] += _colsum(dbody)
        dz_ref[...] = du.astype(BF16)

    return pl.pallas_call(
        kern, name="conv_bwd", grid=(d // tcol, n_chunks),
        in_specs=[pl.BlockSpec((seq, tcol), lambda j, c: (0, lru_off + j)),
                  pl.BlockSpec((seq, tcol), lambda j, c: (0, j)),
                  pl.BlockSpec((16, tcol), lambda j, c: (0, j)),
                  pl.BlockSpec(memory_space=pl.ANY)],
        out_specs=[pl.BlockSpec((None, tc, tcol), lambda j, c: (3, c, j)),
                   pl.BlockSpec((SUBLANES, tcol), lambda j, c: (0, j))],
        out_shape=[jax.ShapeDtypeStruct(dz.shape, dz.dtype), jax.ShapeDtypeStruct((SUBLANES, d), F32)],
        input_output_aliases={3: 0},
        compiler_params=_cparams(("arbitrary", "arbitrary")),
    )(z, dxc, sv, dz)


BIG = (("w_in", "col"), ("pool_w", "row"), ("lru_wa", "row"), ("lru_wx", "row"), ("w_pool_up", "row"),
       ("w_lru_up", "row"), ("w_out", "row"), ("w_ff1", "col"), ("w_ff2", "row"))


def _shard_view(w, fam):
    if fam == "col":
        return w.reshape(w.shape[-2:])
    return w.reshape((-1,) + w.shape[-2:])


def _full_shape(sv_shape, fam):
    if fam == "col":
        return (sv_shape[0], N_DEV * sv_shape[1])
    return (sv_shape[0], N_DEV * sv_shape[1], sv_shape[2])


def _slot(ref, fam, p, size):
    start = pl.multiple_of(p * size, size)
    if fam == "col":
        return ref.at[:, pl.ds(start, size)]
    if fam == "row":
        return ref.at[:, pl.ds(start, size), :]
    return ref.at[p]


def _shard_extent(shape, fam):
    return shape[1]


def _coords():
    return lax.axis_index("x"), lax.axis_index("y"), lax.axis_index("c")


def _ag_job(shards, fams):
    n = len(shards)
    fulls = []
    for s, fam in zip(shards, fams):
        if fam == "lead":
            fulls.append(jax.ShapeDtypeStruct((N_DEV,) + s.shape, s.dtype))
        else:
            fulls.append(jax.ShapeDtypeStruct(_full_shape(s.shape, fam), s.dtype))
    sizes = [1 if fam == "lead" else _shard_extent(s.shape, fam) for s, fam in zip(shards, fams)]

    def ctx(ins, outs, sems):
        send, recv, loc = sems
        x, y, c = _coords()
        chips = [(1 - x, y), (x, 1 - y), (1 - x, 1 - y)]

        def copy(a, k, owner, to, src=None):
            dst = _slot(outs[a], fams[a], owner, sizes[a])
            return pltpu.make_async_remote_copy(
                src_ref=dst if src is None else src, dst_ref=dst,
                send_sem=send.at[a, k], recv_sem=recv.at[a, k], device_id=to, device_id_type=MESH)

        def local(a):
            return pltpu.make_async_copy(ins[a], _slot(outs[a], fams[a], 4 * x + 2 * y + c, sizes[a]), loc.at[a])

        return x, y, c, chips, copy, local

    def start(ins, outs, sems):
        x, y, c, chips, copy, local = ctx(ins, outs, sems)
        me = 4 * x + 2 * y + c
        for a in range(n):
            local(a).start()
            copy(a, 0, me, (x, y, 1 - c), ins[a]).start()
            for j, (cx, cy) in enumerate(chips):
                copy(a, 1 + j, me, (cx, cy, c), ins[a]).start()

    def mid(ins, outs, sems):
        x, y, c, chips, copy, _ = ctx(ins, outs, sems)
        for a in range(n):
            for j, (cx, cy) in enumerate(chips):
                owner = 4 * cx + 2 * cy + c
                copy(a, 1 + j, owner, (x, y, c)).wait_recv()
                copy(a, 4 + j, owner, (x, y, 1 - c)).start()

    def finish(ins, outs, sems):
        x, y, c, chips, copy, local = ctx(ins, outs, sems)
        me = 4 * x + 2 * y + c
        for a in range(n):
            copy(a, 0, 4 * x + 2 * y + (1 - c), (x, y, c)).wait_recv()
            for j, (cx, cy) in enumerate(chips):
                copy(a, 4 + j, 4 * cx + 2 * cy + (1 - c), (x, y, c)).wait_recv()
            copy(a, 0, me, (x, y, 1 - c), ins[a]).wait_send()
            for j, (cx, cy) in enumerate(chips):
                copy(a, 1 + j, me, (cx, cy, c), ins[a]).wait_send()
                copy(a, 4 + j, 4 * cx + 2 * cy + c, (x, y, 1 - c)).wait_send()
            local(a).wait()

    sems = [pltpu.SemaphoreType.DMA((n, 7)), pltpu.SemaphoreType.DMA((n, 7)), pltpu.SemaphoreType.DMA((n,))]
    return _Job(shards, fulls, sems, start, finish, mid)


def _rs_sibling_job(fulls, fams, sizes):
    n = len(fulls)
    outs = []
    for f, fam, sz in zip(fulls, fams, sizes):
        if fam == "col":
            outs.append(jax.ShapeDtypeStruct((4, f.shape[0], sz), f.dtype))
        else:
            outs.append(jax.ShapeDtypeStruct((4, f.shape[0], sz, f.shape[2]), f.dtype))

    def copies(ins, rcv, sems):
        send, recv = sems
        x, y, c = _coords()
        return [pltpu.make_async_remote_copy(
            src_ref=_slot(ins[a], fams[a], 2 * q + (1 - c), sizes[a]), dst_ref=rcv[a].at[q],
            send_sem=send.at[a, q], recv_sem=recv.at[a, q], device_id=(x, y, 1 - c), device_id_type=MESH)
            for a in range(n) for q in range(4)]

    def start(ins, rcv, sems):
        for cp in copies(ins, rcv, sems):
            cp.start()

    def finish(ins, rcv, sems):
        for cp in copies(ins, rcv, sems):
            cp.wait()

    return _Job(fulls, outs, [pltpu.SemaphoreType.DMA((n, 4)), pltpu.SemaphoreType.DMA((n, 4))], start, finish)


def _rs_chips_job(parts):
    n = len(parts)
    outs = [jax.ShapeDtypeStruct((3,) + p.shape[1:], p.dtype) for p in parts]

    def copies(ins, rcv, sems):
        send, recv = sems
        x, y, c = _coords()
        cps = []
        for a in range(n):
            for r in (1, 2, 3):
                tx, ty = (1 - x) if r & 2 else x, (1 - y) if r & 1 else y
                cps.append(pltpu.make_async_remote_copy(
                    src_ref=ins[a].at[2 * tx + ty], dst_ref=rcv[a].at[r - 1],
                    send_sem=send.at[a, r - 1], recv_sem=recv.at[a, r - 1],
                    device_id=(tx, ty, c), device_id_type=MESH))
        return cps

    def start(ins, rcv, sems):
        for cp in copies(ins, rcv, sems):
            cp.start()

    def finish(ins, rcv, sems):
        for cp in copies(ins, rcv, sems):
            cp.wait()

    return _Job(parts, outs, [pltpu.SemaphoreType.DMA((n, 3)), pltpu.SemaphoreType.DMA((n, 3))], start, finish)


def _tile_rows(rows, cols):
    tr = rows
    while tr * cols > (1 << 18) and tr % (2 * SUBLANES) == 0:
        tr //= 2
    return tr


def _rs_add(name, full, recv_a, fam, size, cidx):
    if fam == "col":
        rows = full.shape[0]
        tr = _tile_rows(rows, size)
        grid = (4, rows // tr)
        f_spec = pl.BlockSpec((tr, size), lambda q, i, cr: (i, 2 * q + cr[0]))
        s_spec = pl.BlockSpec((None, tr, size), lambda q, i, cr: (q, i, 0))
    else:
        nb, cols = full.shape[0], full.shape[2]
        tr = _tile_rows(size, cols)
        nt = size // tr
        grid = (4, nb * nt)
        f_spec = pl.BlockSpec((None, tr, cols), lambda q, i, cr: (i // nt, (2 * q + cr[0]) * nt + i % nt, 0))
        s_spec = pl.BlockSpec((None, None, tr, cols), lambda q, i, cr: (q, i // nt, i % nt, 0))

    def kern(c_ref, f_ref, r_ref, o_ref):
        del c_ref
        o_ref[...] = (f_ref[...] + r_ref[...]).astype(BF16)

    return pl.pallas_call(
        kern, name=name,
        grid_spec=pltpu.PrefetchScalarGridSpec(num_scalar_prefetch=1, grid=grid, in_specs=[f_spec, s_spec],
                                               out_specs=s_spec),
        out_shape=jax.ShapeDtypeStruct(recv_a.shape, BF16),
        compiler_params=_cparams(("arbitrary", "arbitrary"), 32),
    )(cidx, full, recv_a)


def _adam(w, g, m, v):
    m2 = ADAM_B1 * m + (1.0 - ADAM_B1) * g
    v2 = ADAM_B2 * v + (1.0 - ADAM_B2) * (g * g)
    m_hat = m2 / (1.0 - ADAM_B1 ** ADAM_STEP)
    v_hat = v2 / (1.0 - ADAM_B2 ** ADAM_STEP)
    delta = -ADAM_LR * (m_hat / (jnp.sqrt(v_hat) + ADAM_EPS) + ADAM_WD * w)
    return delta, m2, v2


def _rs_final_adam(name, parts, recv_b, w, m, v, fam, qidx):
    shp = w.shape
    if fam == "col":
        rows, cols = shp
        tr = _tile_rows(rows, cols)
        grid = (rows // tr,)
        w_spec = pl.BlockSpec((tr, cols), lambda i, qr: (i, 0))
        p_spec = pl.BlockSpec((None, tr, cols), lambda i, qr: (qr[0], i, 0))
        r_spec = pl.BlockSpec((3, tr, cols), lambda i, qr: (0, i, 0))
    else:
        nb, rows, cols = shp
        tr = _tile_rows(rows, cols)
        nt = rows // tr
        grid = (nb * nt,)
        w_spec = pl.BlockSpec((None, tr, cols), lambda i, qr: (i // nt, i % nt, 0))
        p_spec = pl.BlockSpec((None, None, tr, cols), lambda i, qr: (qr[0], i // nt, i % nt, 0))
        r_spec = pl.BlockSpec((3, None, tr, cols), lambda i, qr: (0, i // nt, i % nt, 0))

    def kern(q_ref, p_ref, r_ref, w_ref, m_ref, v_ref, g_out, d_out, m_out, v_out):
        del q_ref
        g = ((p_ref[...].astype(F32) + r_ref[0].astype(F32)) + r_ref[1].astype(F32)) + r_ref[2].astype(F32)
        delta, m2, v2 = _adam(w_ref[...], g, m_ref[...], v_ref[...])
        g_out[...] = g
        d_out[...] = delta
        m_out[...] = m2
        v_out[...] = v2

    sh = jax.ShapeDtypeStruct(shp, F32)
    return pl.pallas_call(
        kern, name=name,
        grid_spec=pltpu.PrefetchScalarGridSpec(
            num_scalar_prefetch=1, grid=grid, in_specs=[p_spec, r_spec, w_spec, w_spec, w_spec],
            out_specs=[w_spec] * 4),
        out_shape=[sh] * 4,
        compiler_params=_cparams(("arbitrary",), 32),
    )(qidx, parts, recv_b, w, m, v)


def _sum8(name, parts):
    def kern(p_ref, o_ref):
        acc = p_ref[0]
        for p in range(1, N_DEV):
            acc = acc + p_ref[p]
        o_ref[...] = acc

    return pl.pallas_call(
        kern, name=name, out_shape=jax.ShapeDtypeStruct(parts.shape[1:], F32),
        compiler_params=pltpu.CompilerParams(vmem_limit_bytes=32 << 20),
    )(parts)


def _adam_small(name, w, g, m, v):
    def kern(w_ref, g_ref, m_ref, v_ref, d_out, m_out, v_out):
        delta, m2, v2 = _adam(w_ref[...], g_ref[...], m_ref[...], v_ref[...])
        d_out[...] = delta
        m_out[...] = m2
        v_out[...] = v2

    sh = jax.ShapeDtypeStruct(w.shape, F32)
    return pl.pallas_call(kern, name=name, out_shape=[sh] * 3)(w, g, m, v)


class _NoComm:
    def __init__(self):
        self.grads = {}

    def job(self, host):
        return None

    def after(self, host, job):
        pass

    def grad(self, name, g):
        self.grads[name] = g


def _local_step(x, target, wts, vec, comm=None):
    comm = comm or _NoComm()
    seq, d = x.shape
    w_in, pool_w, wa, wx = wts["w_in"], wts["pool_w"], wts["lru_wa"], wts["lru_wx"]
    sv = vec["sv"]
    ff = vec["b_ff1"].shape[1]
    n_in = w_in.shape[1]

    def hosted(host, call):
        job = comm.job(host)
        res = call(job)
        comm.after(host, job)
        return res
    blk = wa.shape[2]
    tc = min(512, seq)
    t1k, t512, t256 = min(1024, seq), min(512, seq), min(256, seq)
    n512 = min(512, d)
    tkd = d
    tkf = min(1024, ff)
    sub_rows = min(64, seq)

    x_bf = x.astype(BF16)
    full = lambda i, j, k: (0, 0)

    def epi_store(acc, i, j, ex, outs):
        outs[0][...] = acc

    (z,) = hosted("z_proj", lambda job: _mm(
        "z_proj", "nn", (seq // t1k, n_in // n512, 1),
        x_bf, ((t1k, d), lambda i, j, k: (i, 0)), w_in, ((d, n512), lambda i, j, k: (0, j)),
        [], [jax.ShapeDtypeStruct((seq, n_in), F32)], [((t1k, n512), lambda i, j, k: (i, j))], epi_store,
        job=job))

    y_pool = _pool_fwd(z, pool_w, vec["pool_scale"], seq, d, tc)
    xc, h_f = hosted("lru_fwd", lambda job: _lru_fwd(z, sv, vec["conv_b"], wa, wx, seq, d, tc, job=job))
    h_b, y_lru = hosted("lru_rev", lambda job: _lru_rev(z, sv, wa, wx, xc, h_f, seq, d, tc, job=job))
    w_pu, w_lu, w_out, w1, w2 = wts["w_pool_up"], wts["w_lru_up"], wts["w_out"], wts["w_ff1"], wts["w_ff2"]
    m_bf, p_a, p_b = _merge(y_pool, w_pu, y_lru, w_lu, z, seq, d, t512, n512)

    def epi_ln1(acc, i, j, ex, outs):
        x_ref, bo, g1, b1 = ex
        s1 = DN_ALPHA * x_ref[...] + (acc + bo[...])
        xhat, rstd, x1 = _ln_fwd(s1, g1[...], b1[...])
        outs[0][...] = xhat
        outs[1][...] = x1.astype(BF16)
        outs[2][...] = rstd

    rowd = lambda t: ((t, d), lambda i, j, k: (i, 0))
    vecd = ((1, d), full)
    xhat1, x1_bf, rstd1 = _mm(
        "out_ln1", "nn", (seq // t256, 1, 1), m_bf, rowd(t256), w_out, ((d, d), full),
        [(x, rowd(t256)), (vec["b_out"], vecd), (vec["ln1_g"], vecd), (vec["ln1_b"], vecd)],
        [jax.ShapeDtypeStruct((seq, d), F32), jax.ShapeDtypeStruct((seq, d), BF16),
         jax.ShapeDtypeStruct((seq, 1), F32)],
        [rowd(t256), rowd(t256), ((t256, 1), lambda i, j, k: (i, 0))], epi_ln1)

    def epi_ff1(acc, i, j, ex, outs):
        r = jnp.maximum(acc + ex[0][...], 0.0)
        outs[0][...] = r.astype(BF16)
        outs[1][...] = (r * r).astype(BF16)

    tile_f = ((t1k, n512), lambda i, j, k: (i, j))
    relu_h, hdn = _mm(
        "ff1", "nn", (seq // t1k, ff // n512, 1), x1_bf, rowd(t1k), w1, ((d, n512), lambda i, j, k: (0, j)),
        [(vec["b_ff1"], ((1, n512), lambda i, j, k: (0, j)))],
        [jax.ShapeDtypeStruct((seq, ff), BF16)] * 2, [tile_f, tile_f], epi_ff1)

    def epi_ln2(acc, i, j, ex, outs):
        xh1, tgt, g1, b1, bf2, g2, b2 = ex
        ds_ref, dsb_ref, sm_ref, loss_ref = outs
        x1 = xh1[...] * g1[...] + b1[...]
        s2 = DN_ALPHA * x1 + (acc + bf2[...])
        xhat, rstd, y = _ln_fwd(s2, g2[...], b2[...])
        e = y - tgt[...]
        part = 0.5 * jnp.sum(jnp.mean(e * e, axis=-1, keepdims=True))
        dy = e * (1.0 / d)
        ds2 = _ln_bwd(dy, xhat, rstd, g2[...])
        ds_ref[...] = ds2
        dsb_ref[...] = ds2.astype(BF16)
        sm_ref[0:1, :] += _colsum(dy * xhat)
        sm_ref[1:2, :] += _colsum(dy)
        sm_ref[2:3, :] += _colsum(ds2)
        loss_ref[...] += jnp.full(loss_ref.shape, part, F32)

    def zero_tail(n_tail):
        def init(i, j, outs):
            @pl.when(i == 0)
            def _():
                for o in outs[-n_tail:]:
                    o[...] = jnp.zeros_like(o)
        return init

    ds2, ds2_bf, sm_ln2, loss_blk = _mm(
        "ff2_ln2", "nn", (seq // t512, 1, ff // tkf), hdn, ((t512, tkf), lambda i, j, k: (i, k)),
        w2, ((tkf, d), lambda i, j, k: (k, 0)),
        [(xhat1, rowd(t512)), (target, rowd(t512)), (vec["ln1_g"], vecd), (vec["ln1_b"], vecd),
         (vec["b_ff2"], vecd), (vec["ln2_g"], vecd), (vec["ln2_b"], vecd)],
        [jax.ShapeDtypeStruct((seq, d), F32), jax.ShapeDtypeStruct((seq, d), BF16),
         jax.ShapeDtypeStruct((SUBLANES, d), F32), jax.ShapeDtypeStruct((SUBLANES, 128), F32)],
        [rowd(t512), rowd(t512), ((SUBLANES, d), full), ((SUBLANES, 128), full)],
        epi_ln2, acc_shape=(t512, d), epi_init=zero_tail(2), sub=sub_rows)

    tkw = min(2048, seq)

    def dw(name, wname, a, b, m_dim, n_dim, b_spec=None):
        tm, tn = min(1024, m_dim), min(1024, n_dim)
        b_spec = b_spec or ((tkw, tn), lambda i, j, k: (k, j))
        (out,) = hosted(name, lambda job: _mm(
            name, "tn", (m_dim // tm, n_dim // tn, seq // tkw),
            a, ((tkw, tm), lambda i, j, k: (k, i)), b, b_spec, [],
            [jax.ShapeDtypeStruct((m_dim, n_dim), F32)], [((tm, tn), lambda i, j, k: (i, j))],
            epi_store if seq == tkw else None, job=job))
        comm.grad(wname, out)

    dw("dw_ff2", "w_ff2", hdn, ds2_bf, ff, d)

    def epi_dpre(acc, i, j, ex, outs):
        dpre = acc * (2.0 * ex[0][...].astype(F32))
        outs[0][...] = dpre.astype(BF16)

        @pl.when(i == 0)
        def _():
            outs[1][...] = jnp.zeros_like(outs[1])

        outs[1][0:1, :] += _colsum(dpre)

    dpre, sm_bff1 = hosted("dhdn", lambda job: _mm(
        "dhdn", "nt", (seq // t1k, ff // n512, 1), ds2_bf, rowd(t1k), w2, ((n512, d), lambda i, j, k: (j, 0)),
        [(relu_h, tile_f)],
        [jax.ShapeDtypeStruct((seq, ff), BF16), jax.ShapeDtypeStruct((SUBLANES, ff), F32)],
        [tile_f, ((SUBLANES, n512), lambda i, j, k: (0, j))], epi_dpre, order="ji", job=job))

    dw("dw_ff1", "w_ff1", x1_bf, dpre, d, ff)

    def epi_ln1b(acc, i, j, ex, outs):
        ds2_ref, xh1, rs1, g1 = ex
        ds_ref, dsb_ref, sm_ref = outs
        dy1 = acc + DN_ALPHA * ds2_ref[...]
        xhat = xh1[...]
        ds1 = _ln_bwd(dy1, xhat, rs1[...], g1[...])
        ds_ref[...] = ds1
        dsb_ref[...] = ds1.astype(BF16)
        sm_ref[0:1, :] += _colsum(dy1 * xhat)
        sm_ref[1:2, :] += _colsum(dy1)
        sm_ref[2:3, :] += _colsum(ds1)

    ds1, ds1_bf, sm_ln1 = hosted("dx1_ln1", lambda job: _mm(
        "dx1_ln1", "nt", (seq // t512, 1, ff // tkf), dpre, ((t512, tkf), lambda i, j, k: (i, k)),
        w1, ((d, tkf), lambda i, j, k: (0, k)),
        [(ds2, rowd(t512)), (xhat1, rowd(t512)), (rstd1, ((t512, 1), lambda i, j, k: (i, 0))),
         (vec["ln1_g"], vecd)],
        [jax.ShapeDtypeStruct((seq, d), F32), jax.ShapeDtypeStruct((seq, d), BF16),
         jax.ShapeDtypeStruct((SUBLANES, d), F32)],
        [rowd(t512), rowd(t512), ((SUBLANES, d), full)], epi_ln1b, acc_shape=(t512, d),
        epi_init=zero_tail(1), sub=sub_rows, job=job))

    dw("dw_out", "w_out", m_bf, ds1_bf, d, d)
    n_n = d // n512
    tile_d = ((t512, n512), lambda i, j, k: (i, j))

    def epi_dm(acc, i, j, ex, outs):
        la, lb, pa, pb = ex
        ga, gb = _sig(la[...]), _sig(lb[...])
        outs[0][...] = (acc * ga).astype(BF16)
        outs[1][...] = (acc * gb).astype(BF16)
        outs[2][0] = (acc * pa[...].astype(F32) * ga * (1.0 - ga)).astype(BF16)
        outs[2][1] = (acc * pb[...].astype(F32) * gb * (1.0 - gb)).astype(BF16)

    dp_a, dp_b, dz = _mm(
        "dm", "nt", (seq // t512, n_n, 1), ds1_bf, rowd(t512), w_out, ((n512, d), lambda i, j, k: (j, 0)),
        [(z, ((t512, n512), lambda i, j, k: (i, 3 * n_n + j))),
         (z, ((t512, n512), lambda i, j, k: (i, 4 * n_n + j))), (p_a, tile_d), (p_b, tile_d)],
        [jax.ShapeDtypeStruct((seq, d), BF16), jax.ShapeDtypeStruct((seq, d), BF16),
         jax.ShapeDtypeStruct((5, seq, d), BF16)],
        [tile_d, tile_d, ((2, t512, n512), lambda i, j, k: (0, i, j))], epi_dm)

    dw("dw_pool_up", "w_pool_up", y_pool, dp_a, d, d)
    dw("dw_lru_up", "w_lru_up", y_lru, dp_b, d, d)

    def epi_bf(acc, i, j, ex, outs):
        outs[0][...] = acc.astype(BF16)

    (dy_pool,) = _mm("dy_pool", "nt", (seq // t512, n_n, 1), dp_a, rowd(t512), w_pu,
                     ((n512, d), lambda i, j, k: (j, 0)), [],
                     [jax.ShapeDtypeStruct((seq, d), BF16)], [tile_d], epi_bf)

    def epi_dylru(acc, i, j, ex, outs):
        hf, hb, ug, _ = ex
        u = ug[...]
        outs[0][...] = acc * _gelu(u)
        outs[1][...] = (acc * (hf[...] + hb[...]) * _gelu_grad(u)).astype(BF16)

    dz_in = dz
    dh, dz = hosted("dy_lru", lambda job: _mm(
        "dy_lru", "nt", (seq // t512, n_n, 1), dp_b, rowd(t512), w_lu, ((n512, d), lambda i, j, k: (j, 0)),
        [(h_f, tile_d), (h_b, tile_d), (z, ((t512, n512), lambda i, j, k: (i, 2 * n_n + j))),
         (dz_in, (None, None))],
        [jax.ShapeDtypeStruct((seq, d), F32), jax.ShapeDtypeStruct(dz_in.shape, BF16)],
        [tile_d, ((None, t512, n512), lambda i, j, k: (4, i, j))], epi_dylru, aliases={5: 1}, job=job))

    dz, g_pw, sm_pool = _pool_bwd(z, dy_pool, pool_w, vec["pool_scale"], dz, seq, d, tc)
    comm.grad("pool_w", g_pw)
    dxc0, g_wa0, g_wx0, sm_l0 = hosted("lru_bwd_0", lambda job: _lru_bwd(
        0, xc, dh, h_f, sv, wa, wx, None, seq, d, tc, job=job))
    dxc, g_wa1, g_wx1, sm_l1 = hosted("lru_bwd_1", lambda job: _lru_bwd(
        1, xc, dh, h_b, sv, wa, wx, dxc0, seq, d, tc, job=job))
    comm.grad("lru_wa", jnp.concatenate([g_wa0, g_wa1], axis=0))
    comm.grad("lru_wx", jnp.concatenate([g_wx0, g_wx1], axis=0))
    dz, sm_conv = _conv_bwd(z, dxc, sv, dz, seq, d, tc, blk)

    tnw = min(1024, d)
    per_seg = d // tnw
    dw("dw_in", "w_in", x_bf, dz, d, n_in,
       b_spec=((None, tkw, tnw), lambda i, j, k: ((j // per_seg + 2) % 5, k, j % per_seg)))

    nk = d // tkd

    def epi_dx(acc, i, j, ex, outs):
        outs[0][...] = acc + DN_ALPHA * ex[0][...]

    (grad_x,) = hosted("dx", lambda job: _mm(
        "dx", "nt", (seq // t512, 1, n_in // tkd), dz,
        ((None, t512, tkd), lambda i, j, k: ((k // nk + 2) % 5, i, k % nk)),
        w_in, ((d, tkd), lambda i, j, k: (0, k)), [(ds1, rowd(t512))],
        [jax.ShapeDtypeStruct((seq, d), F32)], [rowd(t512)], epi_dx, acc_shape=(t512, d), job=job))

    small = {"ln2": sm_ln2, "b_ff1": sm_bff1, "ln1": sm_ln1, "pool": sm_pool, "lru0": sm_l0, "lru1": sm_l1,
             "conv": sm_conv}
    return loss_blk[0, 0], grad_x, small


REP = ("pool_scale", "conv_b", "b_out", "ln1_g", "ln1_b", "b_ff2", "ln2_g", "ln2_b")
SHARDED_SMALL = (("conv_w", 4), ("lru_ba", 2), ("lru_bx", 2), ("lru_lambda", 2))
WEIGHT_ORDER = ("w_in", "pool_w", "pool_scale", "conv_w", "conv_b", "lru_wa", "lru_ba", "lru_wx", "lru_bx",
                "lru_lambda", "w_pool_up", "w_lru_up", "w_out", "b_out", "ln1_g", "ln1_b", "w_ff1", "b_ff1",
                "w_ff2", "b_ff2", "ln2_g", "ln2_b")


def _pad_rows(a, rows):
    return jnp.concatenate([a, jnp.zeros((rows - a.shape[0], a.shape[1]), a.dtype)], axis=0)


def kernel(x, w_in, pool_w, pool_scale, conv_w, conv_b, lru_wa, lru_ba, lru_wx, lru_bx, lru_lambda, w_pool_up, w_lru_up, w_out, b_out, ln1_g, ln1_b, w_ff1, b_ff1, w_ff2, b_ff2, ln2_g, ln2_b, loss_target, m_w_in, m_pool_w, m_pool_scale, m_conv_w, m_conv_b, m_lru_wa, m_lru_ba, m_lru_wx, m_lru_bx, m_lru_lambda, m_w_pool_up, m_w_lru_up, m_w_out, m_b_out, m_ln1_g, m_ln1_b, m_w_ff1, m_b_ff1, m_w_ff2, m_b_ff2, m_ln2_g, m_ln2_b, v_w_in, v_pool_w, v_pool_scale, v_conv_w, v_conv_b, v_lru_wa, v_lru_ba, v_lru_wx, v_lru_bx, v_lru_lambda, v_w_pool_up, v_w_lru_up, v_w_out, v_b_out, v_ln1_g, v_ln1_b, v_w_ff1, v_b_ff1, v_w_ff2, v_b_ff2, v_ln2_g, v_ln2_b):
    args = dict(locals())
    w = {n: args[n] for n in WEIGHT_ORDER}
    mom = {n: args["m_" + n] for n in WEIGHT_ORDER}
    var = {n: args["v_" + n] for n in WEIGHT_ORDER}
    seq, d = x.shape[1], x.shape[2]
    n_heads, blk = lru_wa.shape[2], lru_wa.shape[4]
    n_groups = pool_w.shape[1]
    ff = b_ff1.shape[1]
    cx, cy, cc = _coords()
    me = 4 * cx + 2 * cy + cc
    cidx = jnp.reshape(cc, (1,)).astype(jnp.int32)
    qidx = jnp.reshape(2 * cx + cy, (1,)).astype(jnp.int32)

    fam_of = dict(BIG)
    sviews = {n: _shard_view(w[n], fam) for n, fam in BIG}
    size_of = {n: sviews[n].shape[1] for n, _ in BIG}
    wts = {}

    def take_gathered(names, arrays):
        for n, g in zip(names, arrays):
            if n == "pool_w":
                g = g.reshape(n_groups, d // n_groups, d // n_groups)
            elif n in ("lru_wa", "lru_wx"):
                g = g.reshape(2, n_heads, blk, blk)
            elif fam_of[n] == "row":
                g = g.reshape(g.shape[1:])
            wts[n] = g

    def gather_job(names, extra=()):
        return _ag_job([sviews[n].astype(BF16) for n in names] + [e for e, _ in extra],
                       [fam_of[n] for n in names] + [f for _, f in extra])

    class Plan:
        gather = {"z_proj": ("w_pool_up", "w_lru_up", "w_out"), "lru_fwd": ("w_ff1",), "lru_rev": ("w_ff2",)}
        to_sibling = {"dhdn": ("w_ff2",), "dx1_ln1": ("w_ff1",), "dy_lru": ("w_out", "w_pool_up", "w_lru_up"),
                      "dx": ("w_in", "pool_w", "lru_wa", "lru_wx")}
        to_chips = {"dw_ff1": ("w_ff2",), "lru_bwd_0": ("w_ff1",), "lru_bwd_1": ("w_out", "w_pool_up", "w_lru_up")}

        def __init__(self):
            self.grads, self.parts, self.recv_b = {}, {}, {}

        def grad(self, name, g):
            self.grads[name] = g if fam_of[name] == "col" else g.reshape((-1,) + g.shape[-2:])

        def job(self, host):
            if host in self.gather:
                return gather_job(self.gather[host])
            if host in self.to_sibling:
                names = self.to_sibling[host]
                return _rs_sibling_job([self.grads[n] for n in names], [fam_of[n] for n in names],
                                       [size_of[n] for n in names])
            if host in self.to_chips:
                return _rs_chips_job([self.parts[n] for n in self.to_chips[host]])
            return None

        def after(self, host, job):
            if host in self.gather:
                take_gathered(self.gather[host], job.results)
            elif host in self.to_sibling:
                for n, r in zip(self.to_sibling[host], job.results):
                    self.parts[n] = _rs_add("rs_add_" + n, self.grads[n], r, fam_of[n], size_of[n], cidx)
            elif host in self.to_chips:
                self.recv_b.update(zip(self.to_chips[host], job.results))

    first = ("w_in", "pool_w", "lru_wa", "lru_wx")
    sv_shard = _pad_rows(jnp.concatenate([w[n].reshape(r, -1) for n, r in SHARDED_SMALL], axis=0), 16)
    gathered = _run_job("ag_first", gather_job(first, [(sv_shard, "col")]))
    take_gathered(first, gathered[:-1])
    vec = {n: w[n] for n in REP}
    vec["b_ff1"] = b_ff1
    vec["sv"] = gathered[-1]

    plan = Plan()
    loss_part, grad_x, small = _local_step(x.reshape(seq, d), loss_target.reshape(seq, d), wts, vec, plan)
    loss = lax.psum(loss_part, AXES)

    tail = Plan.to_sibling["dx"]
    plan.recv_b.update(zip(tail, _run_job("rs_chips_tail", _rs_chips_job([plan.parts[n] for n in tail]))))
    out_g, out_d, out_m, out_v = {}, {}, {}, {}
    for n, fam in BIG:
        res = _rs_final_adam("adam_" + n, plan.parts[n], plan.recv_b[n], sviews[n], _shard_view(mom[n], fam),
                             _shard_view(var[n], fam), fam, qidx)
        out_g[n], out_d[n], out_m[n], out_v[n] = [r.reshape(w[n].shape) for r in res]

    rows = [small["pool"][0:1], small["conv"][4:5], small["ln1"][2:3], small["ln1"][0:1], small["ln1"][1:2],
            small["ln2"][2:3], small["ln2"][0:1], small["ln2"][1:2], small["b_ff1"][0:1].reshape(ff // d, d),
            small["conv"][0:4], small["lru0"][0:1], small["lru1"][0:1], small["lru0"][1:2], small["lru1"][1:2],
            small["lru0"][2:3], small["lru1"][2:3]]
    n_rep = len(REP) + ff // d
    n_rows = n_rep + sum(r for _, r in SHARDED_SMALL)
    pad_rows = -(-n_rows // SUBLANES) * SUBLANES
    packed = _pad_rows(jnp.concatenate(rows, axis=0), pad_rows)
    (all_small,) = _run_job("ag_small", _ag_job([packed], ["lead"]))
    g_small = _sum8("sum_small", all_small)

    def pack_rep(t):
        return jnp.concatenate([t[n] for n in REP] + [t["b_ff1"].reshape(ff // d, d)], axis=0)

    def pack_sh(t):
        return jnp.concatenate([t[n].reshape(r, -1) for n, r in SHARDED_SMALL], axis=0)

    g_rep = g_small[:n_rep]
    cs = d // N_DEV
    g_sh = lax.dynamic_slice_in_dim(g_small[n_rep:n_rows], me * cs, cs, axis=1)
    d_rep, m_rep, v_rep = _adam_small("adam_rep", pack_rep(w), g_rep, pack_rep(mom), pack_rep(var))
    d_sh, m_sh, v_sh = _adam_small("adam_sharded", pack_sh(w), g_sh, pack_sh(mom), pack_sh(var))

    def unpack(rep_t, sh_t, dst):
        for i, n in enumerate(REP):
            dst[n] = rep_t[i:i + 1].reshape(w[n].shape)
        dst["b_ff1"] = rep_t[len(REP):n_rep].reshape(w["b_ff1"].shape)
        r0 = 0
        for n, r in SHARDED_SMALL:
            dst[n] = sh_t[r0:r0 + r].reshape(w[n].shape)
            r0 += r

    unpack(g_rep, g_sh, out_g)
    unpack(d_rep, d_sh, out_d)
    unpack(m_rep, m_sh, out_m)
    unpack(v_rep, v_sh, out_v)

    outs = [loss, grad_x.reshape(x.shape)]
    for t in (out_g, out_d, out_m, out_v):
        outs += [t[n] for n in WEIGHT_ORDER]
    return tuple(outs)
```

```python
import functools

import jax
import jax.numpy as jnp
from jax import lax
from jax.experimental import pallas as pl
from jax.experimental.pallas import tpu as pltpu

F32 = jnp.float32
BF16 = jnp.bfloat16
MESH = pl.DeviceIdType.MESH
AXES = ("x", "y", "c")
N_DEV = 8

DN_ALPHA = 2.0 ** 0.25
LN_EPS = 1e-5
LRU_C = 8.0
ADAM_LR = 0.001
ADAM_B1 = 0.9
ADAM_B2 = 0.999
ADAM_EPS = 1e-08
ADAM_WD = 0.01
ADAM_STEP = 10
GELU_C = 0.7978845608028654
GELU_K = 0.044715

HALO = 16
SUBLANES = 8
VMEM_MB = 56


def _cparams(sem, vmem_mb=VMEM_MB):
    return pltpu.CompilerParams(dimension_semantics=sem, vmem_limit_bytes=vmem_mb << 20)


HBM_SPEC = pl.BlockSpec(memory_space=pl.ANY)


class _Job:
    def __init__(self, ins, outs, sems, start, finish, mid=None):
        self.ins, self.outs, self.sems = list(ins), list(outs), list(sems)
        self.start, self.finish, self.mid = start, finish, mid
        self.results = None


def _join_jobs(jobs):
    jobs = [j for j in jobs if j is not None]
    if len(jobs) <= 1:
        return jobs[0] if jobs else None

    def split(refs, counts):
        out, p = [], 0
        for n in counts:
            out.append(refs[p:p + n])
            p += n
        return out

    def phase(which):
        def run(ins, outs, sems):
            parts = zip(jobs, split(ins, [len(j.ins) for j in jobs]), split(outs, [len(j.outs) for j in jobs]),
                        split(sems, [len(j.sems) for j in jobs]))
            for j, ji, jo, js in parts:
                fn = getattr(j, which)
                if fn is not None:
                    fn(ji, jo, js)
        return run

    joined = _Job(sum((j.ins for j in jobs), []), sum((j.outs for j in jobs), []), sum((j.sems for j in jobs), []),
                  phase("start"), phase("finish"), phase("mid") if any(j.mid for j in jobs) else None)
    joined.parts = jobs
    return joined


def _job_results(job):
    if job is None:
        return []
    parts = getattr(job, "parts", None)
    if parts is None:
        return [job.results]
    out, p = [], 0
    for j in parts:
        out.append(job.results[p:p + len(j.outs)])
        p += len(j.outs)
    return out


def _pcall(name, body, grid, in_specs, out_specs, out_shape, inputs, scratch=(), aliases=None,
           vmem_mb=VMEM_MB, job=None):
    in_specs, out_specs, out_shape, scratch = list(in_specs), list(out_specs), list(out_shape), list(scratch)
    params = _cparams(("arbitrary",) * len(grid), vmem_mb)
    if job is None:
        return pl.pallas_call(body, name=name, grid=grid, in_specs=in_specs, out_specs=out_specs,
                              out_shape=out_shape, scratch_shapes=scratch,
                              input_output_aliases=aliases or {}, compiler_params=params)(*inputs)
    n_in, n_out, n_scr = len(inputs), len(out_shape), len(scratch)
    ji, jo = len(job.ins), len(job.outs)
    total = 1
    for g in grid:
        total *= g
    mid_step = (3 * total) // 4 if total >= 4 else None

    def wrapped(*refs):
        p = 0
        ins = refs[p:p + n_in]
        p += n_in
        jins = refs[p:p + ji]
        p += ji
        outs = refs[p:p + n_out]
        p += n_out
        jouts = refs[p:p + jo]
        p += jo
        scr = refs[p:p + n_scr]
        sems = refs[p + n_scr:]
        step = pl.program_id(0)
        for ax in range(1, len(grid)):
            step = step * grid[ax] + pl.program_id(ax)

        @pl.when(step == 0)
        def _():
            job.start(jins, jouts, sems)

        if job.mid is not None and mid_step is not None:
            @pl.when(step == mid_step)
            def _():
                job.mid(jins, jouts, sems)

        body(*ins, *outs, *scr)

        @pl.when(step == total - 1)
        def _():
            if job.mid is not None and mid_step is None:
                job.mid(jins, jouts, sems)
            job.finish(jins, jouts, sems)

    res = pl.pallas_call(
        wrapped, name=name, grid=grid, in_specs=in_specs + [HBM_SPEC] * ji,
        out_specs=out_specs + [HBM_SPEC] * jo, out_shape=out_shape + job.outs,
        scratch_shapes=scratch + job.sems, input_output_aliases=aliases or {}, compiler_params=params,
    )(*inputs, *job.ins)
    job.results = list(res[n_out:])
    return list(res[:n_out])


def _run_job(name, job):
    ji, jo = len(job.ins), len(job.outs)

    def body(*refs):
        jins, jouts, sems = refs[:ji], refs[ji:ji + jo], refs[ji + jo:]
        job.start(jins, jouts, sems)
        if job.mid is not None:
            job.mid(jins, jouts, sems)
        job.finish(jins, jouts, sems)

    res = pl.pallas_call(body, name=name, in_specs=[HBM_SPEC] * ji, out_specs=[HBM_SPEC] * jo,
                         out_shape=job.outs, scratch_shapes=job.sems)(*job.ins)
    job.results = list(res)
    return job.results


def _dot(mode, a, b):
    if mode == "nn":
        dims = (((1,), (0,)), ((), ()))
    elif mode == "nt":
        dims = (((1,), (1,)), ((), ()))
    else:
        dims = (((0,), (0,)), ((), ()))
    return lax.dot_general(a, b, dims, preferred_element_type=F32)


def _sig(x):
    return 0.5 * jnp.tanh(0.5 * x) + 0.5


def _gelu(x):
    t = jnp.tanh(GELU_C * (x + GELU_K * x * x * x))
    return 0.5 * x * (1.0 + t)


def _gelu_grad(x):
    x2 = x * x
    t = jnp.tanh(GELU_C * (x + GELU_K * x * x2))
    return 0.5 * (1.0 + t) + 0.5 * x * (1.0 - t * t) * GELU_C * (1.0 + 3.0 * GELU_K * x2)


def _colsum(v):
    return jnp.sum(v, axis=0, keepdims=True)


def _mm(name, mode, grid, a, a_spec, b, b_spec, extras, out_shapes, out_specs, epi, *,
        order="ij", acc_shape=None, aliases=None, vmem_mb=VMEM_MB, epi_init=None, sub=None, job=None):
    gm, gn, gk = grid

    def spec(s):
        bs, f = s
        if f is None:
            return pl.BlockSpec(memory_space=pl.ANY)
        if order == "ij":
            return pl.BlockSpec(bs, lambda i, j, k, f=f: f(i, j, k))
        return pl.BlockSpec(bs, lambda j, i, k, f=f: f(i, j, k))

    ne, no = len(extras), len(out_shapes)

    def kern(*refs):
        a_ref, b_ref = refs[0], refs[1]
        ex = refs[2:2 + ne]
        outs = refs[2 + ne:2 + ne + no]
        if order == "ij":
            i, j = pl.program_id(0), pl.program_id(1)
        else:
            j, i = pl.program_id(0), pl.program_id(1)
        k = pl.program_id(2)
        prod = _dot(mode, a_ref[...], b_ref[...])
        if gk == 1:
            if epi_init is not None:
                epi_init(i, j, outs)
            epi(prod, i, j, ex, outs)
        elif epi is None:
            @pl.when(k == 0)
            def _():
                outs[0][...] = prod

            @pl.when(k > 0)
            def _():
                outs[0][...] += prod
        else:
            acc = refs[-1]

            @pl.when(k == 0)
            def _():
                acc[...] = prod

            @pl.when(k > 0)
            def _():
                acc[...] += prod

            @pl.when(k == gk - 1)
            def _():
                if epi_init is not None:
                    epi_init(i, j, outs)
                if sub is None:
                    epi(acc[...], i, j, ex, outs)
                else:
                    tm = acc_shape[0]

                    def rows_of(r, rs):
                        return r.at[rs, :] if r.shape[0] == tm else r

                    def blk(t, carry):
                        rs = pl.ds(pl.multiple_of(t * sub, sub), sub)
                        epi(acc[rs, :], i, j, [rows_of(r, rs) for r in ex], [rows_of(r, rs) for r in outs])
                        return carry

                    lax.fori_loop(0, tm // sub, blk, 0)

    g = (gm, gn, gk) if order == "ij" else (gn, gm, gk)
    scratch = [pltpu.VMEM(acc_shape, F32)] if gk > 1 and epi is not None else []
    return _pcall(name, kern, g, [spec(a_spec), spec(b_spec)] + [spec(s) for _, s in extras],
                  [spec(s) for s in out_specs], out_shapes, [a, b] + [e for e, _ in extras],
                  scratch=scratch, aliases=aliases, vmem_mb=vmem_mb, job=job)


def _ext(ref, c, n_chunks, tc, seq):
    c0 = pl.multiple_of(c * tc, tc)
    body = ref[pl.ds(c0, tc), :].astype(F32)
    t0 = pl.multiple_of(jnp.maximum(c0 - HALO, 0), HALO)
    b0 = pl.multiple_of(jnp.minimum(c0 + tc, seq - HALO), HALO)
    top = ref[pl.ds(t0, HALO), :].astype(F32)
    bot = ref[pl.ds(b0, HALO), :].astype(F32)
    top = jnp.where(c > 0, top, 0.0)
    bot = jnp.where(c < n_chunks - 1, bot, 0.0)
    return jnp.concatenate([top, body, bot], axis=0)


def _shifted(vext, off, tc):
    n = vext.shape[0]
    r = vext if off == 0 else pltpu.roll(vext, (n - off) % n, 0)
    return r[HALO:HALO + tc]


def _win_sum(vext, g, extra, tc):
    s2 = vext + pltpu.roll(vext, 1, 0)
    s4 = s2 + pltpu.roll(s2, 2, 0)
    s8 = s4 + pltpu.roll(s4, 4, 0)
    s16 = s8 + pltpu.roll(s8, 8, 0)
    outs = [_shifted(s, extra + hw - 1, tc) for s, hw in ((s2, 1), (s4, 2), (s8, 4), (s16, 8))]
    return jnp.where(g == 0, outs[0], jnp.where(g == 1, outs[1], jnp.where(g == 2, outs[2], outs[3])))


def _win_cnt(t, hw, seq):
    return (jnp.minimum(t + hw, seq) - jnp.maximum(t - hw, 0)).astype(F32)


def _pool_d(uext, g, c, tc, seq):
    hw = jnp.left_shift(1, g)
    t = c * tc + lax.broadcasted_iota(jnp.int32, (tc, 1), 0)
    ws = _win_sum(uext, g, 0, tc)
    return ws / _win_cnt(t, hw, seq) - uext[HALO:HALO + tc]


def _scan_tiles(a_ref, b_ref, h_ref, carry_ref, n_tiles, reverse):
    blk = a_ref.shape[1]
    row = lax.broadcasted_iota(jnp.int32, (SUBLANES, blk), 0)

    def tile(j, hc):
        jj = (n_tiles - 1 - j) if reverse else j
        off = pl.multiple_of(jj * SUBLANES, SUBLANES)
        a = a_ref[pl.ds(off, SUBLANES), :]
        b = b_ref[pl.ds(off, SUBLANES), :]
        for kk in (1, 2, 4):
            sh = (SUBLANES - kk) if reverse else kk
            a_s = pltpu.roll(a, sh, 0)
            b_s = pltpu.roll(b, sh, 0)
            m = (row < SUBLANES - kk) if reverse else (row >= kk)
            a_s = jnp.where(m, a_s, 1.0)
            b_s = jnp.where(m, b_s, 0.0)
            b = a * b_s + b
            a = a * a_s
        h = a * hc + b
        h_ref[pl.ds(off, SUBLANES), :] = h
        return h[0:1, :] if reverse else h[SUBLANES - 1:SUBLANES, :]

    group = SUBLANES if n_tiles % SUBLANES == 0 else 1

    def tiles(jg, hc):
        for u in range(group):
            hc = tile(jg * group + u, hc)
        return hc

    hc = lax.fori_loop(0, n_tiles // group, tiles, carry_ref[0:1, :])
    carry_ref[0:1, :] = hc


def _lru_k(lam):
    y = -lam
    e = jnp.exp(-jnp.abs(y))
    u = 1.0 + e
    l1p = jnp.where(u == 1.0, e, jnp.log(u) * (e / (u - 1.0)))
    return -LRU_C * (jnp.maximum(y, 0.0) + l1p)


def _lru_gates(xc, wa, wx, ba, bx, lam):
    xb = xc.astype(BF16)
    r = _sig(jnp.dot(xb, wa, preferred_element_type=F32) + ba)
    i = _sig(jnp.dot(xb, wx, preferred_element_type=F32) + bx)
    k = _lru_k(lam)
    la = k * r
    a = jnp.exp(la)
    s = jnp.sqrt(-jnp.tanh(la) * (a * a + 1.0))
    return r, i, k, a, s


SV_CONV, SV_BA, SV_BX, SV_LAM = 0, 4, 6, 8


def _pool_fwd(z, pw, scale, seq, d, tc):
    n_g = pw.shape[0]
    pg = d // n_g
    n_chunks = seq // tc

    def kern(z_ref, pw_ref, sc_ref, y_ref):
        g, c = pl.program_id(0), pl.program_id(1)
        uext = _ext(z_ref, c, n_chunks, tc, seq)
        dd = _pool_d(uext, g, c, tc, seq)
        q = jnp.dot(dd.astype(BF16), pw_ref[...], preferred_element_type=F32)
        y_ref[...] = (q * sc_ref[...]).astype(BF16)

    return pl.pallas_call(
        kern, name="pool_fwd", grid=(n_g, n_chunks),
        in_specs=[pl.BlockSpec((seq, pg), lambda g, c: (0, g)),
                  pl.BlockSpec((None, pg, pg), lambda g, c: (g, 0, 0)),
                  pl.BlockSpec((1, pg), lambda g, c: (0, g))],
        out_specs=pl.BlockSpec((tc, pg), lambda g, c: (c, g)),
        out_shape=jax.ShapeDtypeStruct((seq, d), BF16),
        compiler_params=_cparams(("arbitrary", "arbitrary")),
    )(z, pw, scale)


def _lru_fwd(z, sv, conv_b, wa, wx, seq, d, tc, job=None):
    n_h, blk = wa.shape[1], wa.shape[2]
    n_chunks = seq // tc
    lru_off = d // blk

    def kern(z_ref, sv_ref, cb_ref, wa_ref, wx_ref, xc_ref, h_ref, a_s, b_s, carry):
        c = pl.program_id(1)
        uext = _ext(z_ref, c, n_chunks, tc, seq)
        xc = cb_ref[...]
        for k in range(4):
            xc = xc + _shifted(uext, k - 2, tc) * sv_ref[SV_CONV + k:SV_CONV + k + 1, :]
        xc_ref[...] = xc
        _, i, _, a, s = _lru_gates(xc, wa_ref[...], wx_ref[...], sv_ref[SV_BA:SV_BA + 1, :],
                                   sv_ref[SV_BX:SV_BX + 1, :], sv_ref[SV_LAM:SV_LAM + 1, :])
        a_s[...] = a
        b_s[...] = s * (i * xc)

        @pl.when(c == 0)
        def _():
            carry[...] = jnp.zeros_like(carry)

        _scan_tiles(a_s, b_s, h_ref, carry, tc // SUBLANES, False)

    col = lambda h, c: (c, h)
    return _pcall(
        "lru_fwd", kern, (n_h, n_chunks),
        [pl.BlockSpec((seq, blk), lambda h, c: (0, lru_off + h)),
         pl.BlockSpec((16, blk), lambda h, c: (0, h)),
         pl.BlockSpec((1, blk), lambda h, c: (0, h)),
         pl.BlockSpec((None, None, blk, blk), lambda h, c: (0, h, 0, 0)),
         pl.BlockSpec((None, None, blk, blk), lambda h, c: (0, h, 0, 0))],
        [pl.BlockSpec((tc, blk), col), pl.BlockSpec((tc, blk), col)],
        [jax.ShapeDtypeStruct((seq, d), F32), jax.ShapeDtypeStruct((seq, d), F32)],
        [z, sv, conv_b, wa, wx],
        scratch=[pltpu.VMEM((tc, blk), F32), pltpu.VMEM((tc, blk), F32), pltpu.VMEM((SUBLANES, blk), F32)],
        job=job)


def _lru_rev(z, sv, wa, wx, xc, h_f, seq, d, tc, job=None):
    n_h, blk = wa.shape[1], wa.shape[2]
    n_chunks = seq // tc
    gate_off = 2 * d // blk

    def kern(z_ref, sv_ref, wa_ref, wx_ref, xc_ref, hf_ref, hb_ref, y_ref, a_s, b_s, carry):
        c = pl.program_id(1)
        xc = xc_ref[...]
        _, i, _, a, s = _lru_gates(xc, wa_ref[...], wx_ref[...], sv_ref[SV_BA + 1:SV_BA + 2, :],
                                   sv_ref[SV_BX + 1:SV_BX + 2, :], sv_ref[SV_LAM + 1:SV_LAM + 2, :])
        a_s[...] = a
        b_s[...] = s * (i * xc)

        @pl.when(c == 0)
        def _():
            carry[...] = jnp.zeros_like(carry)

        _scan_tiles(a_s, b_s, hb_ref, carry, tc // SUBLANES, True)
        y_ref[...] = ((hf_ref[...] + hb_ref[...]) * _gelu(z_ref[...])).astype(BF16)

    rev = lambda h, c: (n_chunks - 1 - c, h)
    return _pcall(
        "lru_rev", kern, (n_h, n_chunks),
        [pl.BlockSpec((tc, blk), lambda h, c: (n_chunks - 1 - c, gate_off + h)),
         pl.BlockSpec((16, blk), lambda h, c: (0, h)),
         pl.BlockSpec((None, None, blk, blk), lambda h, c: (1, h, 0, 0)),
         pl.BlockSpec((None, None, blk, blk), lambda h, c: (1, h, 0, 0)),
         pl.BlockSpec((tc, blk), rev), pl.BlockSpec((tc, blk), rev)],
        [pl.BlockSpec((tc, blk), rev), pl.BlockSpec((tc, blk), rev)],
        [jax.ShapeDtypeStruct((seq, d), F32), jax.ShapeDtypeStruct((seq, d), BF16)],
        [z, sv, wa, wx, xc, h_f],
        scratch=[pltpu.VMEM((tc, blk), F32), pltpu.VMEM((tc, blk), F32), pltpu.VMEM((SUBLANES, blk), F32)],
        job=job)


def _merge(y_pool, w_pu, y_lru, w_lu, z, seq, d, tm, tn):
    n_n = d // tn

    def kern(yp_ref, wp_ref, yl_ref, wl_ref, la_ref, lb_ref, m_ref, pa_ref, pb_ref):
        pa = jnp.dot(yp_ref[...], wp_ref[...], preferred_element_type=F32)
        pb = jnp.dot(yl_ref[...], wl_ref[...], preferred_element_type=F32)
        m_ref[...] = (_sig(la_ref[...]) * pa + _sig(lb_ref[...]) * pb).astype(BF16)
        pa_ref[...] = pa.astype(BF16)
        pb_ref[...] = pb.astype(BF16)

    row = pl.BlockSpec((tm, d), lambda i, j: (i, 0))
    wcol = pl.BlockSpec((d, tn), lambda i, j: (0, j))
    out = pl.BlockSpec((tm, tn), lambda i, j: (i, j))
    sh = jax.ShapeDtypeStruct((seq, d), BF16)
    return pl.pallas_call(
        kern, name="merge", grid=(seq // tm, n_n),
        in_specs=[row, wcol, row, wcol,
                  pl.BlockSpec((tm, tn), lambda i, j: (i, 3 * n_n + j)),
                  pl.BlockSpec((tm, tn), lambda i, j: (i, 4 * n_n + j))],
        out_specs=[out, out, out], out_shape=[sh, sh, sh],
        compiler_params=_cparams(("arbitrary", "arbitrary")),
    )(y_pool, w_pu, y_lru, w_lu, z, z)


def _ln_fwd(s, g, b):
    mu = jnp.mean(s, axis=-1, keepdims=True)
    xc = s - mu
    var = jnp.mean(xc * xc, axis=-1, keepdims=True)
    rstd = lax.rsqrt(var + LN_EPS)
    xhat = xc * rstd
    return xhat, rstd, xhat * g + b


def _ln_bwd(dy, xhat, rstd, g):
    dyg = dy * g
    m1 = jnp.mean(dyg, axis=-1, keepdims=True)
    m2 = jnp.mean(dyg * xhat, axis=-1, keepdims=True)
    return rstd * (dyg - m1 - xhat * m2)


def _pool_bwd(z, dy_pool, pw, scale, dz, seq, d, tc):
    n_g = pw.shape[0]
    pg = d // n_g
    n_chunks = seq // tc

    def kern(z_ref, dy_ref, pw_ref, sc_ref, dz_in, dz_ref, dpw_ref, dsc_ref):
        del dz_in
        g, c = pl.program_id(0), pl.program_id(1)
        hw = jnp.left_shift(1, g)
        uext = _ext(z_ref, c, n_chunks, tc, seq)
        dd = _pool_d(uext, g, c, tc, seq).astype(BF16)
        pwv = pw_ref[...]
        q = jnp.dot(dd, pwv, preferred_element_type=F32)
        dyext = _ext(dy_ref, c, n_chunks, tc, seq)

        @pl.when(c == 0)
        def _():
            dsc_ref[...] = jnp.zeros_like(dsc_ref)
            dpw_ref[...] = jnp.zeros_like(dpw_ref)

        dsc_ref[0:1, :] += _colsum(dyext[HALO:HALO + tc] * q)
        dqext = (dyext * sc_ref[...]).astype(BF16)
        dpw_ref[...] += _dot("tn", dd, dqext[HALO:HALO + tc])
        ddext = _dot("nt", dqext, pwv)
        text = c * tc - HALO + lax.broadcasted_iota(jnp.int32, (tc + 2 * HALO, 1), 0)
        v = ddext / jnp.maximum(_win_cnt(text, hw, seq), 1.0)
        dz_ref[...] = (_win_sum(v, g, 1, tc) - ddext[HALO:HALO + tc]).astype(BF16)

    return pl.pallas_call(
        kern, name="pool_bwd", grid=(n_g, n_chunks),
        in_specs=[pl.BlockSpec((seq, pg), lambda g, c: (0, g)),
                  pl.BlockSpec((seq, pg), lambda g, c: (0, g)),
                  pl.BlockSpec((None, pg, pg), lambda g, c: (g, 0, 0)),
                  pl.BlockSpec((1, pg), lambda g, c: (0, g)),
                  pl.BlockSpec(memory_space=pl.ANY)],
        out_specs=[pl.BlockSpec((None, tc, pg), lambda g, c: (2, c, g)),
                   pl.BlockSpec((None, pg, pg), lambda g, c: (g, 0, 0)),
                   pl.BlockSpec((SUBLANES, pg), lambda g, c: (0, g))],
        out_shape=[jax.ShapeDtypeStruct(dz.shape, dz.dtype),
                   jax.ShapeDtypeStruct((n_g, pg, pg), F32),
                   jax.ShapeDtypeStruct((SUBLANES, d), F32)],
        input_output_aliases={4: 0},
        compiler_params=_cparams(("arbitrary", "arbitrary")),
    )(z, dy_pool, pw, scale, dz)


def _lru_bwd(direction, xc, dh, h_dir, sv, wa, wx, dxc_prev, seq, d, tc, job=None):
    reverse = direction == 1
    n_h, blk = wa.shape[1], wa.shape[2]
    n_chunks = seq // tc
    has_prev = dxc_prev is not None

    def kern(*refs):
        xc_ref, dh_ref, h_ref, sv_ref, wa_ref, wx_ref = refs[:6]
        p = 6
        prev_ref = None
        if has_prev:
            prev_ref = refs[p]
            p += 1
        dxc_ref, dwa_ref, dwx_ref, sm_ref, at_s, g_s, carry, acarry = refs[p:p + 8]
        c = pl.program_id(1)
        cr = c if reverse else n_chunks - 1 - c
        c0 = pl.multiple_of(cr * tc, tc)

        @pl.when(c == 0)
        def _():
            carry[...] = jnp.zeros_like(carry)
            acarry[...] = jnp.zeros_like(acarry)
            dwa_ref[...] = jnp.zeros_like(dwa_ref)
            dwx_ref[...] = jnp.zeros_like(dwx_ref)
            sm_ref[...] = jnp.zeros_like(sm_ref)

        xc = xc_ref[...]
        wav, wxv = wa_ref[...], wx_ref[...]
        lam = sv_ref[SV_LAM + direction:SV_LAM + direction + 1, :]
        r, i, k, a, s = _lru_gates(xc, wav, wxv, sv_ref[SV_BA + direction:SV_BA + direction + 1, :],
                                   sv_ref[SV_BX + direction:SV_BX + direction + 1, :], lam)
        rowi = lax.broadcasted_iota(jnp.int32, (tc, blk), 0)
        hbody = h_ref[pl.ds(c0, tc), :]
        if not reverse:
            p0 = pl.multiple_of(jnp.maximum(c0 - SUBLANES, 0), SUBLANES)
            edge = jnp.where(cr > 0, h_ref[pl.ds(p0, SUBLANES), :][SUBLANES - 1:SUBLANES, :], 0.0)
            hprev = jnp.where(rowi == 0, edge, pltpu.roll(hbody, 1, 0))
            at = jnp.where(rowi == tc - 1, acarry[0:1, :], pltpu.roll(a, tc - 1, 0))
        else:
            n0 = pl.multiple_of(jnp.minimum(c0 + tc, seq - SUBLANES), SUBLANES)
            edge = jnp.where(cr < n_chunks - 1, h_ref[pl.ds(n0, SUBLANES), :][0:1, :], 0.0)
            hprev = jnp.where(rowi == tc - 1, edge, pltpu.roll(hbody, tc - 1, 0))
            at = jnp.where(rowi == 0, acarry[0:1, :], pltpu.roll(a, 1, 0))
        at_s[...] = at
        _scan_tiles(at_s, dh_ref, g_s, carry, tc // SUBLANES, not reverse)
        acarry[0:1, :] = a[tc - 1:tc, :] if reverse else a[0:1, :]

        gt = g_s[...]
        da = gt * hprev
        di = gt * s * xc
        dxc = gt * s * i
        ds = gt * (i * xc)
        dl = da * a - ds * (a * a) / s
        dpr = (dl * k) * r * (1.0 - r)
        dpi = di * i * (1.0 - i)
        sm_ref[0:1, :] += _colsum(dpr)
        sm_ref[1:2, :] += _colsum(dpi)
        sm_ref[2:3, :] += _colsum(dl * r) * (LRU_C * _sig(-lam))
        xb, dprb, dpib = xc.astype(BF16), dpr.astype(BF16), dpi.astype(BF16)
        dwa_ref[...] += _dot("tn", xb, dprb)
        dwx_ref[...] += _dot("tn", xb, dpib)
        dxc = dxc + _dot("nt", dprb, wav) + _dot("nt", dpib, wxv)
        if has_prev:
            dxc = dxc + prev_ref[...]
        dxc_ref[...] = dxc

    if reverse:
        chunk = lambda h, c: (c, h)
    else:
        chunk = lambda h, c: (n_chunks - 1 - c, h)
    wspec = pl.BlockSpec((None, None, blk, blk), lambda h, c: (direction, h, 0, 0))
    ins = [xc, dh, h_dir, sv, wa, wx] + ([dxc_prev] if has_prev else [])
    in_specs = [pl.BlockSpec((tc, blk), chunk), pl.BlockSpec((tc, blk), chunk),
                pl.BlockSpec((seq, blk), lambda h, c: (0, h)),
                pl.BlockSpec((16, blk), lambda h, c: (0, h)), wspec, wspec]
    if has_prev:
        in_specs.append(pl.BlockSpec((tc, blk), chunk))
    return _pcall(
        "lru_bwd_%d" % direction, kern, (n_h, n_chunks), in_specs,
        [pl.BlockSpec((tc, blk), chunk),
         pl.BlockSpec((None, blk, blk), lambda h, c: (h, 0, 0)),
         pl.BlockSpec((None, blk, blk), lambda h, c: (h, 0, 0)),
         pl.BlockSpec((SUBLANES, blk), lambda h, c: (0, h))],
        [jax.ShapeDtypeStruct((seq, d), F32),
         jax.ShapeDtypeStruct((n_h, blk, blk), F32),
         jax.ShapeDtypeStruct((n_h, blk, blk), F32),
         jax.ShapeDtypeStruct((SUBLANES, d), F32)],
        ins,
        scratch=[pltpu.VMEM((tc, blk), F32), pltpu.VMEM((tc, blk), F32),
                 pltpu.VMEM((SUBLANES, blk), F32), pltpu.VMEM((SUBLANES, blk), F32)],
        job=job)


def _conv_bwd(z, dxc, sv, dz, seq, d, tc, tcol):
    n_chunks = seq // tc
    lru_off = d // tcol

    def kern(z_ref, dx_ref, sv_ref, dz_in, dz_ref, sm_ref):
        del dz_in
        c = pl.program_id(1)
        uext = _ext(z_ref, c, n_chunks, tc, seq)
        dext = _ext(dx_ref, c, n_chunks, tc, seq)
        dbody = dext[HALO:HALO + tc]

        @pl.when(c == 0)
        def _():
            sm_ref[...] = jnp.zeros_like(sm_ref)

        du = jnp.zeros_like(dbody)
        for k in range(4):
            du = du + _shifted(dext, 2 - k, tc) * sv_ref[SV_CONV + k:SV_CONV + k + 1, :]
            sm_ref[k:k + 1, :] += _colsum(dbody * _shifted(uext, k - 2, tc))
        sm_ref[4:5, :] += _colsum(dbody)
        dz_ref[...] = du.astype(BF16)

    return pl.pallas_call(
        kern, name="conv_bwd", grid=(d // tcol, n_chunks),
        in_specs=[pl.BlockSpec((seq, tcol), lambda j, c: (0, lru_off + j)),
                  pl.BlockSpec((seq, tcol), lambda j, c: (0, j)),
                  pl.BlockSpec((16, tcol), lambda j, c: (0, j)),
                  pl.BlockSpec(memory_space=pl.ANY)],
        out_specs=[pl.BlockSpec((None, tc, tcol), lambda j, c: (3, c, j)),
                   pl.BlockSpec((SUBLANES, tcol), lambda j, c: (0, j))],
        out_shape=[jax.ShapeDtypeStruct(dz.shape, dz.dtype), jax.ShapeDtypeStruct((SUBLANES, d), F32)],
        input_output_aliases={3: 0},
        compiler_params=_cparams(("arbitrary", "arbitrary")),
    )(z, dxc, sv, dz)


BIG = (("w_in", "col"), ("pool_w", "row"), ("lru_wa", "row"), ("lru_wx", "row"), ("w_pool_up", "row"),
       ("w_lru_up", "row"), ("w_out", "row"), ("w_ff1", "col"), ("w_ff2", "row"))


def _shard_view(w, fam):
    if fam == "col":
        return w.reshape(w.shape[-2:])
    return w.reshape((-1,) + w.shape[-2:])


def _full_shape(sv_shape, fam):
    if fam == "col":
        return (sv_shape[0], N_DEV * sv_shape[1])
    return (sv_shape[0], N_DEV * sv_shape[1], sv_shape[2])


def _slot(ref, fam, p, size):
    start = pl.multiple_of(p * size, size)
    if fam == "col":
        return ref.at[:, pl.ds(start, size)]
    if fam == "row":
        return ref.at[:, pl.ds(start, size), :]
    return ref.at[p]


def _shard_extent(shape, fam):
    return shape[1]


def _coords():
    return lax.axis_index("x"), lax.axis_index("y"), lax.axis_index("c")


def _ag_job(shards, fams):
    n = len(shards)
    fulls = []
    for s, fam in zip(shards, fams):
        if fam == "lead":
            fulls.append(jax.ShapeDtypeStruct((N_DEV,) + s.shape, s.dtype))
        else:
            fulls.append(jax.ShapeDtypeStruct(_full_shape(s.shape, fam), s.dtype))
    sizes = [1 if fam == "lead" else _shard_extent(s.shape, fam) for s, fam in zip(shards, fams)]

    def ctx(ins, outs, sems):
        send, recv, loc = sems
        x, y, c = _coords()
        chips = [(1 - x, y), (x, 1 - y), (1 - x, 1 - y)]

        def copy(a, k, owner, to, src=None):
            dst = _slot(outs[a], fams[a], owner, sizes[a])
            return pltpu.make_async_remote_copy(
                src_ref=dst if src is None else src, dst_ref=dst,
                send_sem=send.at[a, k], recv_sem=recv.at[a, k], device_id=to, device_id_type=MESH)

        def local(a):
            return pltpu.make_async_copy(ins[a], _slot(outs[a], fams[a], 4 * x + 2 * y + c, sizes[a]), loc.at[a])

        return x, y, c, chips, copy, local

    def start(ins, outs, sems):
        x, y, c, chips, copy, local = ctx(ins, outs, sems)
        me = 4 * x + 2 * y + c
        for a in range(n):
            local(a).start()
            copy(a, 0, me, (x, y, 1 - c), ins[a]).start()
            for j, (cx, cy) in enumerate(chips):
                copy(a, 1 + j, me, (cx, cy, c), ins[a]).start()

    def mid(ins, outs, sems):
        x, y, c, chips, copy, _ = ctx(ins, outs, sems)
        for a in range(n):
            for j, (cx, cy) in enumerate(chips):
                owner = 4 * cx + 2 * cy + c
                copy(a, 1 + j, owner, (x, y, c)).wait_recv()
                copy(a, 4 + j, owner, (x, y, 1 - c)).start()

    def finish(ins, outs, sems):
        x, y, c, chips, copy, local = ctx(ins, outs, sems)
        me = 4 * x + 2 * y + c
        for a in range(n):
            copy(a, 0, 4 * x + 2 * y + (1 - c), (x, y, c)).wait_recv()
            for j, (cx, cy) in enumerate(chips):
                copy(a, 4 + j, 4 * cx + 2 * cy + (1 - c), (x, y, c)).wait_recv()
            copy(a, 0, me, (x, y, 1 - c), ins[a]).wait_send()
            for j, (cx, cy) in enumerate(chips):
                copy(a, 1 + j, me, (cx, cy, c), ins[a]).wait_send()
                copy(a, 4 + j, 4 * cx + 2 * cy + c, (x, y, 1 - c)).wait_send()
            local(a).wait()

    sems = [pltpu.SemaphoreType.DMA((n, 7)), pltpu.SemaphoreType.DMA((n, 7)), pltpu.SemaphoreType.DMA((n,))]
    return _Job(shards, fulls, sems, start, finish, mid)


def _rs_sibling_job(fulls, fams, sizes):
    n = len(fulls)
    outs = []
    for f, fam, sz in zip(fulls, fams, sizes):
        if fam == "col":
            outs.append(jax.ShapeDtypeStruct((4, f.shape[0], sz), f.dtype))
        else:
            outs.append(jax.ShapeDtypeStruct((4, f.shape[0], sz, f.shape[2]), f.dtype))

    def copies(ins, rcv, sems):
        send, recv = sems
        x, y, c = _coords()
        return [pltpu.make_async_remote_copy(
            src_ref=_slot(ins[a], fams[a], 2 * q + (1 - c), sizes[a]), dst_ref=rcv[a].at[q],
            send_sem=send.at[a, q], recv_sem=recv.at[a, q], device_id=(x, y, 1 - c), device_id_type=MESH)
            for a in range(n) for q in range(4)]

    def start(ins, rcv, sems):
        for cp in copies(ins, rcv, sems):
            cp.start()

    def finish(ins, rcv, sems):
        for cp in copies(ins, rcv, sems):
            cp.wait()

    return _Job(fulls, outs, [pltpu.SemaphoreType.DMA((n, 4)), pltpu.SemaphoreType.DMA((n, 4))], start, finish)


def _rs_chips_job(parts):
    n = len(parts)
    outs = [jax.ShapeDtypeStruct((3,) + p.shape[1:], p.dtype) for p in parts]

    def copies(ins, rcv, sems):
        send, recv = sems
        x, y, c = _coords()
        cps = []
        for a in range(n):
            for r in (1, 2, 3):
                tx, ty = (1 - x) if r & 2 else x, (1 - y) if r & 1 else y
                cps.append(pltpu.make_async_remote_copy(
                    src_ref=ins[a].at[2 * tx + ty], dst_ref=rcv[a].at[r - 1],
                    send_sem=send.at[a, r - 1], recv_sem=recv.at[a, r - 1],
                    device_id=(tx, ty, c), device_id_type=MESH))
        return cps

    def start(ins, rcv, sems):
        for cp in copies(ins, rcv, sems):
            cp.start()

    def finish(ins, rcv, sems):
        for cp in copies(ins, rcv, sems):
            cp.wait()

    return _Job(parts, outs, [pltpu.SemaphoreType.DMA((n, 3)), pltpu.SemaphoreType.DMA((n, 3))], start, finish)


def _tile_rows(rows, cols):
    tr = rows
    while tr * cols > (1 << 18) and tr % (2 * SUBLANES) == 0:
        tr //= 2
    return tr


def _rs_add(name, full, recv_a, fam, size, cidx):
    if fam == "col":
        rows = full.shape[0]
        tr = _tile_rows(rows, size)
        grid = (4, rows // tr)
        f_spec = pl.BlockSpec((tr, size), lambda q, i, cr: (i, 2 * q + cr[0]))
        s_spec = pl.BlockSpec((None, tr, size), lambda q, i, cr: (q, i, 0))
    else:
        nb, cols = full.shape[0], full.shape[2]
        tr = _tile_rows(nb * size, cols) // nb if nb > 1 else _tile_rows(size, cols)
        nt = size // tr
        grid = (4, nt)
        f_spec = pl.BlockSpec((nb, tr, cols), lambda q, i, cr: (0, (2 * q + cr[0]) * nt + i, 0))
        s_spec = pl.BlockSpec((None, nb, tr, cols), lambda q, i, cr: (q, 0, i, 0))

    def kern(c_ref, f_ref, r_ref, o_ref):
        del c_ref
        o_ref[...] = (f_ref[...] + r_ref[...]).astype(BF16)

    return pl.pallas_call(
        kern, name=name,
        grid_spec=pltpu.PrefetchScalarGridSpec(num_scalar_prefetch=1, grid=grid, in_specs=[f_spec, s_spec],
                                               out_specs=s_spec),
        out_shape=jax.ShapeDtypeStruct(recv_a.shape, BF16),
        compiler_params=_cparams(("arbitrary", "arbitrary"), 32),
    )(cidx, full, recv_a)


def _adam(w, g, m, v):
    m2 = ADAM_B1 * m + (1.0 - ADAM_B1) * g
    v2 = ADAM_B2 * v + (1.0 - ADAM_B2) * (g * g)
    m_hat = m2 / (1.0 - ADAM_B1 ** ADAM_STEP)
    v_hat = v2 / (1.0 - ADAM_B2 ** ADAM_STEP)
    delta = -ADAM_LR * (m_hat / (jnp.sqrt(v_hat) + ADAM_EPS) + ADAM_WD * w)
    return delta, m2, v2


def _rs_final_adam(name, parts, recv_b, w, m, v, fam, qidx):
    shp = w.shape
    pieces = parts if isinstance(parts, (list, tuple)) else [parts]
    recvs = recv_b if isinstance(recv_b, (list, tuple)) else [recv_b]
    n_p = len(pieces)
    if fam == "col":
        rows, cols = shp
        tr = _tile_rows(rows // n_p, cols)
        per = rows // n_p // tr
        grid = (rows // tr,)
        w_spec = pl.BlockSpec((tr, cols), lambda i, qr: (i, 0))

        def piece_row(i, h):
            return jnp.clip(i - h * per, 0, per - 1)

        p_specs = [pl.BlockSpec((None, tr, cols), lambda i, qr, h=h: (qr[0], piece_row(i, h), 0))
                   for h in range(n_p)]
        r_specs = [pl.BlockSpec((3, tr, cols), lambda i, qr, h=h: (0, piece_row(i, h), 0)) for h in range(n_p)]
    else:
        assert n_p == 1
        per = None
        nb, rows, cols = shp
        tr = _tile_rows(rows, cols)
        nt = rows // tr
        grid = (nb * nt,)
        w_spec = pl.BlockSpec((None, tr, cols), lambda i, qr: (i // nt, i % nt, 0))
        p_specs = [pl.BlockSpec((None, None, tr, cols), lambda i, qr: (qr[0], i // nt, i % nt, 0))]
        r_specs = [pl.BlockSpec((3, None, tr, cols), lambda i, qr: (0, i // nt, i % nt, 0))]

    def kern(*refs):
        p_refs, r_refs = refs[1:1 + n_p], refs[1 + n_p:1 + 2 * n_p]
        w_ref, m_ref, v_ref, g_out, d_out, m_out, v_out = refs[1 + 2 * n_p:]

        def total(h):
            p_ref, r_ref = p_refs[h], r_refs[h]
            return ((p_ref[...].astype(F32) + r_ref[0].astype(F32)) + r_ref[1].astype(F32)) + r_ref[2].astype(F32)

        g = total(0)
        for h in range(1, n_p):
            g = jnp.where(pl.program_id(0) >= h * per, total(h), g)
        delta, m2, v2 = _adam(w_ref[...], g, m_ref[...], v_ref[...])
        g_out[...] = g
        d_out[...] = delta
        m_out[...] = m2
        v_out[...] = v2

    sh = jax.ShapeDtypeStruct(shp, F32)
    return pl.pallas_call(
        kern, name=name,
        grid_spec=pltpu.PrefetchScalarGridSpec(
            num_scalar_prefetch=1, grid=grid, in_specs=p_specs + r_specs + [w_spec, w_spec, w_spec],
            out_specs=[w_spec] * 4),
        out_shape=[sh] * 4,
        compiler_params=_cparams(("arbitrary",), 32),
    )(qidx, *pieces, *recvs, w, m, v)


def _sum8(name, parts):
    def kern(p_ref, o_ref):
        acc = p_ref[0]
        for p in range(1, N_DEV):
            acc = acc + p_ref[p]
        o_ref[...] = acc

    return pl.pallas_call(
        kern, name=name, out_shape=jax.ShapeDtypeStruct(parts.shape[1:], F32),
        compiler_params=pltpu.CompilerParams(vmem_limit_bytes=32 << 20),
    )(parts)


def _adam_small(name, w, g, m, v):
    def kern(w_ref, g_ref, m_ref, v_ref, d_out, m_out, v_out):
        delta, m2, v2 = _adam(w_ref[...], g_ref[...], m_ref[...], v_ref[...])
        d_out[...] = delta
        m_out[...] = m2
        v_out[...] = v2

    sh = jax.ShapeDtypeStruct(w.shape, F32)
    return pl.pallas_call(kern, name=name, out_shape=[sh] * 3)(w, g, m, v)


class _NoComm:
    def __init__(self):
        self.grads = {}

    def job(self, host):
        return None

    def after(self, host, job):
        pass

    def grad(self, name, g):
        self.grads[name] = g


def _local_step(x, target, wts, vec, comm=None):
    comm = comm or _NoComm()
    seq, d = x.shape
    w_in, pool_w, wa, wx = wts["w_in"], wts["pool_w"], wts["lru_wa"], wts["lru_wx"]
    sv = vec["sv"]
    ff = vec["b_ff1"].shape[1]
    n_in = w_in.shape[1]

    def hosted(host, call):
        job = comm.job(host)
        res = call(job)
        comm.after(host, job)
        return res
    blk = wa.shape[2]
    tc = min(512, seq)
    t1k, t512, t256 = min(1024, seq), min(512, seq), min(256, seq)
    n512 = min(512, d)
    tkd = d
    tkf = min(1024, ff)
    sub_rows = min(64, seq)

    x_bf = x.astype(BF16)
    full = lambda i, j, k: (0, 0)

    def epi_store(acc, i, j, ex, outs):
        outs[0][...] = acc

    (z,) = hosted("z_proj", lambda job: _mm(
        "z_proj", "nn", (seq // t1k, n_in // n512, 1),
        x_bf, ((t1k, d), lambda i, j, k: (i, 0)), w_in, ((d, n512), lambda i, j, k: (0, j)),
        [], [jax.ShapeDtypeStruct((seq, n_in), F32)], [((t1k, n512), lambda i, j, k: (i, j))], epi_store,
        job=job))

    y_pool = _pool_fwd(z, pool_w, vec["pool_scale"], seq, d, tc)
    xc, h_f = hosted("lru_fwd", lambda job: _lru_fwd(z, sv, vec["conv_b"], wa, wx, seq, d, tc, job=job))
    h_b, y_lru = hosted("lru_rev", lambda job: _lru_rev(z, sv, wa, wx, xc, h_f, seq, d, tc, job=job))
    w_pu, w_lu, w_out, w1 = wts["w_pool_up"], wts["w_lru_up"], wts["w_out"], wts["w_ff1"]
    m_bf, p_a, p_b = _merge(y_pool, w_pu, y_lru, w_lu, z, seq, d, t512, n512)

    def epi_ln1(acc, i, j, ex, outs):
        x_ref, bo, g1, b1 = ex
        s1 = DN_ALPHA * x_ref[...] + (acc + bo[...])
        xhat, rstd, x1 = _ln_fwd(s1, g1[...], b1[...])
        outs[0][...] = xhat
        outs[1][...] = x1.astype(BF16)
        outs[2][...] = rstd

    rowd = lambda t: ((t, d), lambda i, j, k: (i, 0))
    vecd = ((1, d), full)
    xhat1, x1_bf, rstd1 = _mm(
        "out_ln1", "nn", (seq // t256, 1, 1), m_bf, rowd(t256), w_out, ((d, d), full),
        [(x, rowd(t256)), (vec["b_out"], vecd), (vec["ln1_g"], vecd), (vec["ln1_b"], vecd)],
        [jax.ShapeDtypeStruct((seq, d), F32), jax.ShapeDtypeStruct((seq, d), BF16),
         jax.ShapeDtypeStruct((seq, 1), F32)],
        [rowd(t256), rowd(t256), ((t256, 1), lambda i, j, k: (i, 0))], epi_ln1)

    def epi_ff1(acc, i, j, ex, outs):
        r = jnp.maximum(acc + ex[0][...], 0.0)
        outs[0][...] = r.astype(BF16)
        outs[1][...] = (r * r).astype(BF16)

    tile_f = ((t1k, n512), lambda i, j, k: (i, j))
    relu_h, hdn = hosted("ff1", lambda job: _mm(
        "ff1", "nn", (seq // t1k, ff // n512, 1), x1_bf, rowd(t1k), w1, ((d, n512), lambda i, j, k: (0, j)),
        [(vec["b_ff1"], ((1, n512), lambda i, j, k: (0, j)))],
        [jax.ShapeDtypeStruct((seq, ff), BF16)] * 2, [tile_f, tile_f], epi_ff1, job=job))
    w2 = wts["w_ff2"]

    def epi_ln2(acc, i, j, ex, outs):
        xh1, tgt, g1, b1, bf2, g2, b2 = ex
        ds_ref, dsb_ref, sm_ref, loss_ref = outs
        x1 = xh1[...] * g1[...] + b1[...]
        s2 = DN_ALPHA * x1 + (acc + bf2[...])
        xhat, rstd, y = _ln_fwd(s2, g2[...], b2[...])
        e = y - tgt[...]
        part = 0.5 * jnp.sum(jnp.mean(e * e, axis=-1, keepdims=True))
        dy = e * (1.0 / d)
        ds2 = _ln_bwd(dy, xhat, rstd, g2[...])
        ds_ref[...] = ds2
        dsb_ref[...] = ds2.astype(BF16)
        sm_ref[0:1, :] += _colsum(dy * xhat)
        sm_ref[1:2, :] += _colsum(dy)
        sm_ref[2:3, :] += _colsum(ds2)
        loss_ref[...] += jnp.full(loss_ref.shape, part, F32)

    def zero_tail(n_tail):
        def init(i, j, outs):
            @pl.when(i == 0)
            def _():
                for o in outs[-n_tail:]:
                    o[...] = jnp.zeros_like(o)
        return init

    ds2, ds2_bf, sm_ln2, loss_blk = _mm(
        "ff2_ln2", "nn", (seq // t512, 1, ff // tkf), hdn, ((t512, tkf), lambda i, j, k: (i, k)),
        w2, ((tkf, d), lambda i, j, k: (k, 0)),
        [(xhat1, rowd(t512)), (target, rowd(t512)), (vec["ln1_g"], vecd), (vec["ln1_b"], vecd),
         (vec["b_ff2"], vecd), (vec["ln2_g"], vecd), (vec["ln2_b"], vecd)],
        [jax.ShapeDtypeStruct((seq, d), F32), jax.ShapeDtypeStruct((seq, d), BF16),
         jax.ShapeDtypeStruct((SUBLANES, d), F32), jax.ShapeDtypeStruct((SUBLANES, 128), F32)],
        [rowd(t512), rowd(t512), ((SUBLANES, d), full), ((SUBLANES, 128), full)],
        epi_ln2, acc_shape=(t512, d), epi_init=zero_tail(2), sub=sub_rows)

    tkw = min(2048, seq)

    def dw(name, wname, a, b, m_dim, n_dim, b_spec=None, row0=0):
        tm, tn = min(1024, m_dim), min(1024, n_dim)
        b_spec = b_spec or ((tkw, tn), lambda i, j, k: (k, j))
        i0 = row0 // tm
        (out,) = hosted(name, lambda job: _mm(
            name, "tn", (m_dim // tm, n_dim // tn, seq // tkw),
            a, ((tkw, tm), lambda i, j, k: (k, i0 + i)), b, b_spec, [],
            [jax.ShapeDtypeStruct((m_dim, n_dim), F32)], [((tm, tn), lambda i, j, k: (i, j))],
            epi_store if seq == tkw else None, job=job))
        comm.grad(wname, out)

    dw("dw_ff2", "w_ff2", hdn, ds2_bf, ff, d)

    def epi_dpre(acc, i, j, ex, outs):
        dpre = acc * (2.0 * ex[0][...].astype(F32))
        outs[0][...] = dpre.astype(BF16)

        @pl.when(i == 0)
        def _():
            outs[1][...] = jnp.zeros_like(outs[1])

        outs[1][0:1, :] += _colsum(dpre)

    dpre, sm_bff1 = hosted("dhdn", lambda job: _mm(
        "dhdn", "nt", (seq // t1k, ff // n512, 1), ds2_bf, rowd(t1k), w2, ((n512, d), lambda i, j, k: (j, 0)),
        [(relu_h, tile_f)],
        [jax.ShapeDtypeStruct((seq, ff), BF16), jax.ShapeDtypeStruct((SUBLANES, ff), F32)],
        [tile_f, ((SUBLANES, n512), lambda i, j, k: (0, j))], epi_dpre, order="ji", job=job))

    dw("dw_ff1", "w_ff1", x1_bf, dpre, d, ff)

    def epi_ln1b(acc, i, j, ex, outs):
        ds2_ref, xh1, rs1, g1 = ex
        ds_ref, dsb_ref, sm_ref = outs
        dy1 = acc + DN_ALPHA * ds2_ref[...]
        xhat = xh1[...]
        ds1 = _ln_bwd(dy1, xhat, rs1[...], g1[...])
        ds_ref[...] = ds1
        dsb_ref[...] = ds1.astype(BF16)
        sm_ref[0:1, :] += _colsum(dy1 * xhat)
        sm_ref[1:2, :] += _colsum(dy1)
        sm_ref[2:3, :] += _colsum(ds1)

    ds1, ds1_bf, sm_ln1 = hosted("dx1_ln1", lambda job: _mm(
        "dx1_ln1", "nt", (seq // t512, 1, ff // tkf), dpre, ((t512, tkf), lambda i, j, k: (i, k)),
        w1, ((d, tkf), lambda i, j, k: (0, k)),
        [(ds2, rowd(t512)), (xhat1, rowd(t512)), (rstd1, ((t512, 1), lambda i, j, k: (i, 0))),
         (vec["ln1_g"], vecd)],
        [jax.ShapeDtypeStruct((seq, d), F32), jax.ShapeDtypeStruct((seq, d), BF16),
         jax.ShapeDtypeStruct((SUBLANES, d), F32)],
        [rowd(t512), rowd(t512), ((SUBLANES, d), full)], epi_ln1b, acc_shape=(t512, d),
        epi_init=zero_tail(1), sub=sub_rows, job=job))

    dw("dw_out", "w_out", m_bf, ds1_bf, d, d)
    n_n = d // n512
    tile_d = ((t512, n512), lambda i, j, k: (i, j))

    def epi_dm(acc, i, j, ex, outs):
        la, lb, pa, pb = ex
        ga, gb = _sig(la[...]), _sig(lb[...])
        outs[0][...] = (acc * ga).astype(BF16)
        outs[1][...] = (acc * gb).astype(BF16)
        outs[2][0] = (acc * pa[...].astype(F32) * ga * (1.0 - ga)).astype(BF16)
        outs[2][1] = (acc * pb[...].astype(F32) * gb * (1.0 - gb)).astype(BF16)

    dp_a, dp_b, dz = _mm(
        "dm", "nt", (seq // t512, n_n, 1), ds1_bf, rowd(t512), w_out, ((n512, d), lambda i, j, k: (j, 0)),
        [(z, ((t512, n512), lambda i, j, k: (i, 3 * n_n + j))),
         (z, ((t512, n512), lambda i, j, k: (i, 4 * n_n + j))), (p_a, tile_d), (p_b, tile_d)],
        [jax.ShapeDtypeStruct((seq, d), BF16), jax.ShapeDtypeStruct((seq, d), BF16),
         jax.ShapeDtypeStruct((5, seq, d), BF16)],
        [tile_d, tile_d, ((2, t512, n512), lambda i, j, k: (0, i, j))], epi_dm)

    dw("dw_pool_up", "w_pool_up", y_pool, dp_a, d, d)
    dw("dw_lru_up", "w_lru_up", y_lru, dp_b, d, d)

    def epi_bf(acc, i, j, ex, outs):
        outs[0][...] = acc.astype(BF16)

    (dy_pool,) = _mm("dy_pool", "nt", (seq // t512, n_n, 1), dp_a, rowd(t512), w_pu,
                     ((n512, d), lambda i, j, k: (j, 0)), [],
                     [jax.ShapeDtypeStruct((seq, d), BF16)], [tile_d], epi_bf)

    def epi_dylru(acc, i, j, ex, outs):
        hf, hb, ug, _ = ex
        u = ug[...]
        outs[0][...] = acc * _gelu(u)
        outs[1][...] = (acc * (hf[...] + hb[...]) * _gelu_grad(u)).astype(BF16)

    dz_in = dz
    dh, dz = hosted("dy_lru", lambda job: _mm(
        "dy_lru", "nt", (seq // t512, n_n, 1), dp_b, rowd(t512), w_lu, ((n512, d), lambda i, j, k: (j, 0)),
        [(h_f, tile_d), (h_b, tile_d), (z, ((t512, n512), lambda i, j, k: (i, 2 * n_n + j))),
         (dz_in, (None, None))],
        [jax.ShapeDtypeStruct((seq, d), F32), jax.ShapeDtypeStruct(dz_in.shape, BF16)],
        [tile_d, ((None, t512, n512), lambda i, j, k: (4, i, j))], epi_dylru, aliases={5: 1}, job=job))

    dz, g_pw, sm_pool = _pool_bwd(z, dy_pool, pool_w, vec["pool_scale"], dz, seq, d, tc)
    comm.grad("pool_w", g_pw)
    dxc0, g_wa0, g_wx0, sm_l0 = hosted("lru_bwd_0", lambda job: _lru_bwd(
        0, xc, dh, h_f, sv, wa, wx, None, seq, d, tc, job=job))
    dxc, g_wa1, g_wx1, sm_l1 = hosted("lru_bwd_1", lambda job: _lru_bwd(
        1, xc, dh, h_b, sv, wa, wx, dxc0, seq, d, tc, job=job))
    comm.grad("lru_wa", jnp.concatenate([g_wa0, g_wa1], axis=0))
    comm.grad("lru_wx", jnp.concatenate([g_wx0, g_wx1], axis=0))
    dz, sm_conv = _conv_bwd(z, dxc, sv, dz, seq, d, tc, blk)

    tnw = min(1024, d)
    per_seg = d // tnw
    seg_spec = ((None, tkw, tnw), lambda i, j, k: ((j // per_seg + 2) % 5, k, j % per_seg))
    dw("dw_in_lo", "w_in_lo", x_bf, dz, d // 2, n_in, b_spec=seg_spec)
    dw("dw_in_hi", "w_in_hi", x_bf, dz, d // 2, n_in, b_spec=seg_spec, row0=d // 2)

    nk = d // tkd

    def epi_dx(acc, i, j, ex, outs):
        outs[0][...] = acc + DN_ALPHA * ex[0][...]

    (grad_x,) = hosted("dx", lambda job: _mm(
        "dx", "nt", (seq // t512, 1, n_in // tkd), dz,
        ((None, t512, tkd), lambda i, j, k: ((k // nk + 2) % 5, i, k % nk)),
        w_in, ((d, tkd), lambda i, j, k: (0, k)), [(ds1, rowd(t512))],
        [jax.ShapeDtypeStruct((seq, d), F32)], [rowd(t512)], epi_dx, acc_shape=(t512, d), job=job))

    small = {"ln2": sm_ln2, "b_ff1": sm_bff1, "ln1": sm_ln1, "pool": sm_pool, "lru0": sm_l0, "lru1": sm_l1,
             "conv": sm_conv}
    return loss_blk[0, 0], grad_x, small


REP = ("pool_scale", "conv_b", "b_out", "ln1_g", "ln1_b", "b_ff2", "ln2_g", "ln2_b")
SHARDED_SMALL = (("conv_w", 4), ("lru_ba", 2), ("lru_bx", 2), ("lru_lambda", 2))
WEIGHT_ORDER = ("w_in", "pool_w", "pool_scale", "conv_w", "conv_b", "lru_wa", "lru_ba", "lru_wx", "lru_bx",
                "lru_lambda", "w_pool_up", "w_lru_up", "w_out", "b_out", "ln1_g", "ln1_b", "w_ff1", "b_ff1",
                "w_ff2", "b_ff2", "ln2_g", "ln2_b")


def _pad_rows(a, rows):
    return jnp.concatenate([a, jnp.zeros((rows - a.shape[0], a.shape[1]), a.dtype)], axis=0)


def kernel(x, w_in, pool_w, pool_scale, conv_w, conv_b, lru_wa, lru_ba, lru_wx, lru_bx, lru_lambda, w_pool_up, w_lru_up, w_out, b_out, ln1_g, ln1_b, w_ff1, b_ff1, w_ff2, b_ff2, ln2_g, ln2_b, loss_target, m_w_in, m_pool_w, m_pool_scale, m_conv_w, m_conv_b, m_lru_wa, m_lru_ba, m_lru_wx, m_lru_bx, m_lru_lambda, m_w_pool_up, m_w_lru_up, m_w_out, m_b_out, m_ln1_g, m_ln1_b, m_w_ff1, m_b_ff1, m_w_ff2, m_b_ff2, m_ln2_g, m_ln2_b, v_w_in, v_pool_w, v_pool_scale, v_conv_w, v_conv_b, v_lru_wa, v_lru_ba, v_lru_wx, v_lru_bx, v_lru_lambda, v_w_pool_up, v_w_lru_up, v_w_out, v_b_out, v_ln1_g, v_ln1_b, v_w_ff1, v_b_ff1, v_w_ff2, v_b_ff2, v_ln2_g, v_ln2_b):
    args = dict(locals())
    w = {n: args[n] for n in WEIGHT_ORDER}
    mom = {n: args["m_" + n] for n in WEIGHT_ORDER}
    var = {n: args["v_" + n] for n in WEIGHT_ORDER}
    seq, d = x.shape[1], x.shape[2]
    n_heads, blk = lru_wa.shape[2], lru_wa.shape[4]
    n_groups = pool_w.shape[1]
    ff = b_ff1.shape[1]
    cx, cy, cc = _coords()
    me = 4 * cx + 2 * cy + cc
    cidx = jnp.reshape(cc, (1,)).astype(jnp.int32)
    qidx = jnp.reshape(2 * cx + cy, (1,)).astype(jnp.int32)

    fam_of = dict(BIG)
    sviews = {n: _shard_view(w[n], fam) for n, fam in BIG}
    size_of = {n: sviews[n].shape[1] for n, _ in BIG}
    for half in ("w_in_lo", "w_in_hi"):
        fam_of[half], size_of[half] = fam_of["w_in"], size_of["w_in"]
    wts = {}

    def take_gathered(names, arrays):
        for n, g in zip(names, arrays):
            if n == "pool_w":
                g = g.reshape(n_groups, d // n_groups, d // n_groups)
            elif n in ("lru_wa", "lru_wx"):
                g = g.reshape(2, n_heads, blk, blk)
            elif fam_of[n] == "row":
                g = g.reshape(g.shape[1:])
            wts[n] = g

    def gather_job(names, extra=()):
        return _ag_job([sviews[n].astype(BF16) for n in names] + [e for e, _ in extra],
                       [fam_of[n] for n in names] + [f for _, f in extra])

    class Plan:
        gather = {"z_proj": ("w_ff1",), "lru_fwd": ("w_pool_up", "w_lru_up"), "lru_rev": ("w_out",),
                  "ff1": ("w_ff2",)}
        to_sibling = {"dhdn": ("w_ff2",), "dx1_ln1": ("w_ff1",), "dy_lru": ("w_out", "w_pool_up", "w_lru_up"),
                      "dw_in_hi": ("w_in_lo",), "dx": ("w_in_hi", "pool_w", "lru_wa", "lru_wx")}
        to_chips = {"dw_ff1": ("w_ff2",), "lru_bwd_0": ("w_ff1",), "lru_bwd_1": ("w_out", "w_pool_up", "w_lru_up"),
                    "dx": ("w_in_lo",)}

        def __init__(self):
            self.grads, self.parts, self.recv_b = {}, {}, {}

        def grad(self, name, g):
            self.grads[name] = g if fam_of[name] == "col" else g.reshape((-1,) + g.shape[-2:])

        def job(self, host):
            jobs = []
            if host in self.gather:
                jobs.append(gather_job(self.gather[host]))
            if host in self.to_sibling:
                names = self.to_sibling[host]
                jobs.append(_rs_sibling_job([self.grads[n] for n in names], [fam_of[n] for n in names],
                                            [size_of[n] for n in names]))
            if host in self.to_chips:
                jobs.append(_rs_chips_job([self.parts[n] for n in self.to_chips[host]]))
            return _join_jobs(jobs)

        def after(self, host, job):
            results = iter(_job_results(job))
            if host in self.gather:
                take_gathered(self.gather[host], next(results))
            if host in self.to_sibling:
                for n, r in zip(self.to_sibling[host], next(results)):
                    self.parts[n] = _rs_add("rs_add_" + n, self.grads[n], r, fam_of[n], size_of[n], cidx)
            if host in self.to_chips:
                self.recv_b.update(zip(self.to_chips[host], next(results)))

    first = ("w_in", "pool_w", "lru_wa", "lru_wx")
    sv_shard = _pad_rows(jnp.concatenate([w[n].reshape(r, -1) for n, r in SHARDED_SMALL], axis=0), 16)
    gathered = _run_job("ag_first", gather_job(first, [(sv_shard, "col")]))
    take_gathered(first, gathered[:-1])
    vec = {n: w[n] for n in REP}
    vec["b_ff1"] = b_ff1
    vec["sv"] = gathered[-1]

    plan = Plan()
    loss_part, grad_x, small = _local_step(x.reshape(seq, d), loss_target.reshape(seq, d), wts, vec, plan)
    loss = lax.psum(loss_part, AXES)

    tail = Plan.to_sibling["dx"]
    plan.recv_b.update(zip(tail, _run_job("rs_chips_tail", _rs_chips_job([plan.parts[n] for n in tail]))))
    out_g, out_d, out_m, out_v = {}, {}, {}, {}
    for n, fam in BIG:
        halves = [n + "_lo", n + "_hi"] if n == "w_in" else [n]
        res = _rs_final_adam("adam_" + n, [plan.parts[h] for h in halves], [plan.recv_b[h] for h in halves],
                             sviews[n], _shard_view(mom[n], fam), _shard_view(var[n], fam), fam, qidx)
        out_g[n], out_d[n], out_m[n], out_v[n] = [r.reshape(w[n].shape) for r in res]

    rows = [small["pool"][0:1], small["conv"][4:5], small["ln1"][2:3], small["ln1"][0:1], small["ln1"][1:2],
            small["ln2"][2:3], small["ln2"][0:1], small["ln2"][1:2], small["b_ff1"][0:1].reshape(ff // d, d),
            small["conv"][0:4], small["lru0"][0:1], small["lru1"][0:1], small["lru0"][1:2], small["lru1"][1:2],
            small["lru0"][2:3], small["lru1"][2:3]]
    n_rep = len(REP) + ff // d
    n_rows = n_rep + sum(r for _, r in SHARDED_SMALL)
    pad_rows = -(-n_rows // SUBLANES) * SUBLANES
    packed = _pad_rows(jnp.concatenate(rows, axis=0), pad_rows)
    (all_small,) = _run_job("ag_small", _ag_job([packed], ["lead"]))
    g_small = _sum8("sum_small", all_small)

    def pack_rep(t):
        return jnp.concatenate([t[n] for n in REP] + [t["b_ff1"].reshape(ff // d, d)], axis=0)

    def pack_sh(t):
        return jnp.concatenate([t[n].reshape(r, -1) for n, r in SHARDED_SMALL], axis=0)

    g_rep = g_small[:n_rep]
    cs = d // N_DEV
    g_sh = lax.dynamic_slice_in_dim(g_small[n_rep:n_rows], me * cs, cs, axis=1)
    d_rep, m_rep, v_rep = _adam_small("adam_rep", pack_rep(w), g_rep, pack_rep(mom), pack_rep(var))
    d_sh, m_sh, v_sh = _adam_small("adam_sharded", pack_sh(w), g_sh, pack_sh(mom), pack_sh(var))

    def unpack(rep_t, sh_t, dst):
        for i, n in enumerate(REP):
            dst[n] = rep_t[i:i + 1].reshape(w[n].shape)
        dst["b_ff1"] = rep_t[len(REP):n_rep].reshape(w["b_ff1"].shape)
        r0 = 0
        for n, r in SHARDED_SMALL:
            dst[n] = sh_t[r0:r0 + r].reshape(w[n].shape)
            r0 += r

    unpack(g_rep, g_sh, out_g)
    unpack(d_rep, d_sh, out_d)
    unpack(m_rep, m_sh, out_m)
    unpack(v_rep, v_sh, out_v)

    outs = [loss, grad_x.reshape(x.shape)]
    for t in (out_g, out_d, out_m, out_v):
        outs += [t[n] for n in WEIGHT_ORDER]
    return tuple(outs)
```

```python
import functools

import jax
import jax.numpy as jnp
from jax import lax
from jax.experimental import pallas as pl
from jax.experimental.pallas import tpu as pltpu

F32 = jnp.float32
BF16 = jnp.bfloat16
MESH = pl.DeviceIdType.MESH
AXES = ("x", "y", "c")
N_DEV = 8

DN_ALPHA = 2.0 ** 0.25
LN_EPS = 1e-5
LRU_C = 8.0
ADAM_LR = 0.001
ADAM_B1 = 0.9
ADAM_B2 = 0.999
ADAM_EPS = 1e-08
ADAM_WD = 0.01
ADAM_STEP = 10
GELU_C = 0.7978845608028654
GELU_K = 0.044715

W_IN_PASSES = 5
HALO = 16
SUBLANES = 8
VMEM_MB = 56


def _cparams(sem, vmem_mb=VMEM_MB):
    return pltpu.CompilerParams(dimension_semantics=sem, vmem_limit_bytes=vmem_mb << 20)


HBM_SPEC = pl.BlockSpec(memory_space=pl.ANY)


class _Job:
    def __init__(self, ins, outs, sems, start, finish, mid=None):
        self.ins, self.outs, self.sems = list(ins), list(outs), list(sems)
        self.start, self.finish, self.mid = start, finish, mid
        self.alias = {}
        self.results = None


def _join_jobs(jobs):
    jobs = [j for j in jobs if j is not None]
    if len(jobs) <= 1:
        return jobs[0] if jobs else None

    def split(refs, counts):
        out, p = [], 0
        for n in counts:
            out.append(refs[p:p + n])
            p += n
        return out

    def phase(which):
        def run(ins, outs, sems):
            parts = zip(jobs, split(ins, [len(j.ins) for j in jobs]), split(outs, [len(j.outs) for j in jobs]),
                        split(sems, [len(j.sems) for j in jobs]))
            for j, ji, jo, js in parts:
                fn = getattr(j, which)
                if fn is not None:
                    fn(ji, jo, js)
        return run

    joined = _Job(sum((j.ins for j in jobs), []), sum((j.outs for j in jobs), []), sum((j.sems for j in jobs), []),
                  phase("start"), phase("finish"), phase("mid") if any(j.mid for j in jobs) else None)
    joined.parts = jobs
    i0 = o0 = 0
    for j in jobs:
        joined.alias.update({i0 + i: o0 + o for i, o in j.alias.items()})
        i0 += len(j.ins)
        o0 += len(j.outs)
    return joined


def _job_results(job):
    if job is None:
        return []
    parts = getattr(job, "parts", None)
    if parts is None:
        return [job.results]
    out, p = [], 0
    for j in parts:
        out.append(job.results[p:p + len(j.outs)])
        p += len(j.outs)
    return out


def _pcall(name, body, grid, in_specs, out_specs, out_shape, inputs, scratch=(), aliases=None,
           vmem_mb=VMEM_MB, job=None):
    in_specs, out_specs, out_shape, scratch = list(in_specs), list(out_specs), list(out_shape), list(scratch)
    params = _cparams(("arbitrary",) * len(grid), vmem_mb)
    if job is None:
        return pl.pallas_call(body, name=name, grid=grid, in_specs=in_specs, out_specs=out_specs,
                              out_shape=out_shape, scratch_shapes=scratch,
                              input_output_aliases=aliases or {}, compiler_params=params)(*inputs)
    n_in, n_out, n_scr = len(inputs), len(out_shape), len(scratch)
    ji, jo = len(job.ins), len(job.outs)
    total = 1
    for g in grid:
        total *= g
    mid_step = (3 * total) // 4 if total >= 4 else None

    def wrapped(*refs):
        p = 0
        ins = refs[p:p + n_in]
        p += n_in
        jins = refs[p:p + ji]
        p += ji
        outs = refs[p:p + n_out]
        p += n_out
        jouts = refs[p:p + jo]
        p += jo
        scr = refs[p:p + n_scr]
        sems = refs[p + n_scr:]
        step = pl.program_id(0)
        for ax in range(1, len(grid)):
            step = step * grid[ax] + pl.program_id(ax)

        @pl.when(step == 0)
        def _():
            job.start(jins, jouts, sems)

        if job.mid is not None and mid_step is not None:
            @pl.when(step == mid_step)
            def _():
                job.mid(jins, jouts, sems)

        body(*ins, *outs, *scr)

        @pl.when(step == total - 1)
        def _():
            if job.mid is not None and mid_step is None:
                job.mid(jins, jouts, sems)
            job.finish(jins, jouts, sems)

    all_aliases = dict(aliases or {})
    all_aliases.update({n_in + i: n_out + o for i, o in job.alias.items()})
    res = pl.pallas_call(
        wrapped, name=name, grid=grid, in_specs=in_specs + [HBM_SPEC] * ji,
        out_specs=out_specs + [HBM_SPEC] * jo, out_shape=out_shape + job.outs,
        scratch_shapes=scratch + job.sems, input_output_aliases=all_aliases, compiler_params=params,
    )(*inputs, *job.ins)
    job.results = list(res[n_out:])
    return list(res[:n_out])


def _run_job(name, job):
    ji, jo = len(job.ins), len(job.outs)

    def body(*refs):
        jins, jouts, sems = refs[:ji], refs[ji:ji + jo], refs[ji + jo:]
        job.start(jins, jouts, sems)
        if job.mid is not None:
            job.mid(jins, jouts, sems)
        job.finish(jins, jouts, sems)

    res = pl.pallas_call(body, name=name, in_specs=[HBM_SPEC] * ji, out_specs=[HBM_SPEC] * jo,
                         out_shape=job.outs, scratch_shapes=job.sems, input_output_aliases=job.alias)(*job.ins)
    job.results = list(res)
    return job.results


def _dot(mode, a, b):
    if mode == "nn":
        dims = (((1,), (0,)), ((), ()))
    elif mode == "nt":
        dims = (((1,), (1,)), ((), ()))
    else:
        dims = (((0,), (0,)), ((), ()))
    return lax.dot_general(a, b, dims, preferred_element_type=F32)


def _sig(x):
    return 0.5 * jnp.tanh(0.5 * x) + 0.5


def _gelu(x):
    t = jnp.tanh(GELU_C * (x + GELU_K * x * x * x))
    return 0.5 * x * (1.0 + t)


def _gelu_grad(x):
    x2 = x * x
    t = jnp.tanh(GELU_C * (x + GELU_K * x * x2))
    return 0.5 * (1.0 + t) + 0.5 * x * (1.0 - t * t) * GELU_C * (1.0 + 3.0 * GELU_K * x2)


def _colsum(v):
    return jnp.sum(v, axis=0, keepdims=True)


def _mm(name, mode, grid, a, a_spec, b, b_spec, extras, out_shapes, out_specs, epi, *,
        order="ij", acc_shape=None, aliases=None, vmem_mb=VMEM_MB, epi_init=None, sub=None, job=None):
    gm, gn, gk = grid

    def spec(s):
        bs, f = s
        if f is None:
            return pl.BlockSpec(memory_space=pl.ANY)
        if order == "ij":
            return pl.BlockSpec(bs, lambda i, j, k, f=f: f(i, j, k))
        return pl.BlockSpec(bs, lambda j, i, k, f=f: f(i, j, k))

    ne, no = len(extras), len(out_shapes)

    def kern(*refs):
        a_ref, b_ref = refs[0], refs[1]
        ex = refs[2:2 + ne]
        outs = refs[2 + ne:2 + ne + no]
        if order == "ij":
            i, j = pl.program_id(0), pl.program_id(1)
        else:
            j, i = pl.program_id(0), pl.program_id(1)
        k = pl.program_id(2)
        prod = _dot(mode, a_ref[...], b_ref[...])
        if gk == 1:
            if epi_init is not None:
                epi_init(i, j, outs)
            epi(prod, i, j, ex, outs)
        elif epi is None:
            @pl.when(k == 0)
            def _():
                outs[0][...] = prod

            @pl.when(k > 0)
            def _():
                outs[0][...] += prod
        else:
            acc = refs[-1]

            @pl.when(k == 0)
            def _():
                acc[...] = prod

            @pl.when(k > 0)
            def _():
                acc[...] += prod

            @pl.when(k == gk - 1)
            def _():
                if epi_init is not None:
                    epi_init(i, j, outs)
                if sub is None:
                    epi(acc[...], i, j, ex, outs)
                else:
                    tm = acc_shape[0]

                    def rows_of(r, rs):
                        return r.at[rs, :] if r.shape[0] == tm else r

                    def blk(t, carry):
                        rs = pl.ds(pl.multiple_of(t * sub, sub), sub)
                        epi(acc[rs, :], i, j, [rows_of(r, rs) for r in ex], [rows_of(r, rs) for r in outs])
                        return carry

                    lax.fori_loop(0, tm // sub, blk, 0)

    g = (gm, gn, gk) if order == "ij" else (gn, gm, gk)
    scratch = [pltpu.VMEM(acc_shape, F32)] if gk > 1 and epi is not None else []
    return _pcall(name, kern, g, [spec(a_spec), spec(b_spec)] + [spec(s) for _, s in extras],
                  [spec(s) for s in out_specs], out_shapes, [a, b] + [e for e, _ in extras],
                  scratch=scratch, aliases=aliases, vmem_mb=vmem_mb, job=job)


def _ext(ref, c, n_chunks, tc, seq):
    c0 = pl.multiple_of(c * tc, tc)
    body = ref[pl.ds(c0, tc), :].astype(F32)
    t0 = pl.multiple_of(jnp.maximum(c0 - HALO, 0), HALO)
    b0 = pl.multiple_of(jnp.minimum(c0 + tc, seq - HALO), HALO)
    top = ref[pl.ds(t0, HALO), :].astype(F32)
    bot = ref[pl.ds(b0, HALO), :].astype(F32)
    top = jnp.where(c > 0, top, 0.0)
    bot = jnp.where(c < n_chunks - 1, bot, 0.0)
    return jnp.concatenate([top, body, bot], axis=0)


def _shifted(vext, off, tc):
    n = vext.shape[0]
    r = vext if off == 0 else pltpu.roll(vext, (n - off) % n, 0)
    return r[HALO:HALO + tc]


def _win_sum(vext, g, extra, tc):
    s2 = vext + pltpu.roll(vext, 1, 0)
    s4 = s2 + pltpu.roll(s2, 2, 0)
    s8 = s4 + pltpu.roll(s4, 4, 0)
    s16 = s8 + pltpu.roll(s8, 8, 0)
    outs = [_shifted(s, extra + hw - 1, tc) for s, hw in ((s2, 1), (s4, 2), (s8, 4), (s16, 8))]
    return jnp.where(g == 0, outs[0], jnp.where(g == 1, outs[1], jnp.where(g == 2, outs[2], outs[3])))


def _win_cnt(t, hw, seq):
    return (jnp.minimum(t + hw, seq) - jnp.maximum(t - hw, 0)).astype(F32)


def _pool_d(uext, g, c, tc, seq):
    hw = jnp.left_shift(1, g)
    t = c * tc + lax.broadcasted_iota(jnp.int32, (tc, 1), 0)
    ws = _win_sum(uext, g, 0, tc)
    return ws / _win_cnt(t, hw, seq) - uext[HALO:HALO + tc]


def _scan_tiles(a_ref, b_ref, h_ref, carry_ref, n_tiles, reverse):
    blk = a_ref.shape[1]
    row = lax.broadcasted_iota(jnp.int32, (SUBLANES, blk), 0)

    def tile(j, hc):
        jj = (n_tiles - 1 - j) if reverse else j
        off = pl.multiple_of(jj * SUBLANES, SUBLANES)
        a = a_ref[pl.ds(off, SUBLANES), :]
        b = b_ref[pl.ds(off, SUBLANES), :]
        for kk in (1, 2, 4):
            sh = (SUBLANES - kk) if reverse else kk
            a_s = pltpu.roll(a, sh, 0)
            b_s = pltpu.roll(b, sh, 0)
            m = (row < SUBLANES - kk) if reverse else (row >= kk)
            a_s = jnp.where(m, a_s, 1.0)
            b_s = jnp.where(m, b_s, 0.0)
            b = a * b_s + b
            a = a * a_s
        h = a * hc + b
        h_ref[pl.ds(off, SUBLANES), :] = h
        return h[0:1, :] if reverse else h[SUBLANES - 1:SUBLANES, :]

    group = SUBLANES if n_tiles % SUBLANES == 0 else 1

    def tiles(jg, hc):
        for u in range(group):
            hc = tile(jg * group + u, hc)
        return hc

    hc = lax.fori_loop(0, n_tiles // group, tiles, carry_ref[0:1, :])
    carry_ref[0:1, :] = hc


def _lru_k(lam):
    y = -lam
    e = jnp.exp(-jnp.abs(y))
    u = 1.0 + e
    l1p = jnp.where(u == 1.0, e, jnp.log(u) * (e / (u - 1.0)))
    return -LRU_C * (jnp.maximum(y, 0.0) + l1p)


def _lru_gates(xc, wa, wx, ba, bx, lam):
    xb = xc.astype(BF16)
    r = _sig(jnp.dot(xb, wa, preferred_element_type=F32) + ba)
    i = _sig(jnp.dot(xb, wx, preferred_element_type=F32) + bx)
    k = _lru_k(lam)
    la = k * r
    a = jnp.exp(la)
    s = jnp.sqrt(-jnp.tanh(la) * (a * a + 1.0))
    return r, i, k, a, s


SV_CONV, SV_BA, SV_BX, SV_LAM = 0, 4, 6, 8


def _pool_fwd(z, pw, scale, seq, d, tc, job=None):
    n_g = pw.shape[0]
    pg = d // n_g
    n_chunks = seq // tc

    def kern(z_ref, pw_ref, sc_ref, y_ref):
        g, c = pl.program_id(0), pl.program_id(1)
        uext = _ext(z_ref, c, n_chunks, tc, seq)
        dd = _pool_d(uext, g, c, tc, seq)
        q = jnp.dot(dd.astype(BF16), pw_ref[...], preferred_element_type=F32)
        y_ref[...] = (q * sc_ref[...]).astype(BF16)

    (y,) = _pcall(
        "pool_fwd", kern, (n_g, n_chunks),
        [pl.BlockSpec((seq, pg), lambda g, c: (0, g)),
         pl.BlockSpec((None, pg, pg), lambda g, c: (g, 0, 0)),
         pl.BlockSpec((1, pg), lambda g, c: (0, g))],
        [pl.BlockSpec((tc, pg), lambda g, c: (c, g))],
        [jax.ShapeDtypeStruct((seq, d), BF16)], [z, pw, scale], job=job)
    return y


def _lru_fwd(z, sv, conv_b, wa, wx, seq, d, tc, job=None):
    n_h, blk = wa.shape[1], wa.shape[2]
    n_chunks = seq // tc
    lru_off = d // blk

    def kern(z_ref, sv_ref, cb_ref, wa_ref, wx_ref, xc_ref, h_ref, a_s, b_s, carry):
        c = pl.program_id(1)
        uext = _ext(z_ref, c, n_chunks, tc, seq)
        xc = cb_ref[...]
        for k in range(4):
            xc = xc + _shifted(uext, k - 2, tc) * sv_ref[SV_CONV + k:SV_CONV + k + 1, :]
        xc_ref[...] = xc
        _, i, _, a, s = _lru_gates(xc, wa_ref[...], wx_ref[...], sv_ref[SV_BA:SV_BA + 1, :],
                                   sv_ref[SV_BX:SV_BX + 1, :], sv_ref[SV_LAM:SV_LAM + 1, :])
        a_s[...] = a
        b_s[...] = s * (i * xc)

        @pl.when(c == 0)
        def _():
            carry[...] = jnp.zeros_like(carry)

        _scan_tiles(a_s, b_s, h_ref, carry, tc // SUBLANES, False)

    col = lambda h, c: (c, h)
    return _pcall(
        "lru_fwd", kern, (n_h, n_chunks),
        [pl.BlockSpec((seq, blk), lambda h, c: (0, lru_off + h)),
         pl.BlockSpec((16, blk), lambda h, c: (0, h)),
         pl.BlockSpec((1, blk), lambda h, c: (0, h)),
         pl.BlockSpec((None, None, blk, blk), lambda h, c: (0, h, 0, 0)),
         pl.BlockSpec((None, None, blk, blk), lambda h, c: (0, h, 0, 0))],
        [pl.BlockSpec((tc, blk), col), pl.BlockSpec((tc, blk), col)],
        [jax.ShapeDtypeStruct((seq, d), F32), jax.ShapeDtypeStruct((seq, d), F32)],
        [z, sv, conv_b, wa, wx],
        scratch=[pltpu.VMEM((tc, blk), F32), pltpu.VMEM((tc, blk), F32), pltpu.VMEM((SUBLANES, blk), F32)],
        job=job)


def _lru_rev(z, sv, wa, wx, xc, h_f, seq, d, tc, job=None):
    n_h, blk = wa.shape[1], wa.shape[2]
    n_chunks = seq // tc
    gate_off = 2 * d // blk

    def kern(z_ref, sv_ref, wa_ref, wx_ref, xc_ref, hf_ref, hb_ref, y_ref, a_s, b_s, carry):
        c = pl.program_id(1)
        xc = xc_ref[...]
        _, i, _, a, s = _lru_gates(xc, wa_ref[...], wx_ref[...], sv_ref[SV_BA + 1:SV_BA + 2, :],
                                   sv_ref[SV_BX + 1:SV_BX + 2, :], sv_ref[SV_LAM + 1:SV_LAM + 2, :])
        a_s[...] = a
        b_s[...] = s * (i * xc)

        @pl.when(c == 0)
        def _():
            carry[...] = jnp.zeros_like(carry)

        _scan_tiles(a_s, b_s, hb_ref, carry, tc // SUBLANES, True)
        y_ref[...] = ((hf_ref[...] + hb_ref[...]) * _gelu(z_ref[...])).astype(BF16)

    rev = lambda h, c: (n_chunks - 1 - c, h)
    return _pcall(
        "lru_rev", kern, (n_h, n_chunks),
        [pl.BlockSpec((tc, blk), lambda h, c: (n_chunks - 1 - c, gate_off + h)),
         pl.BlockSpec((16, blk), lambda h, c: (0, h)),
         pl.BlockSpec((None, None, blk, blk), lambda h, c: (1, h, 0, 0)),
         pl.BlockSpec((None, None, blk, blk), lambda h, c: (1, h, 0, 0)),
         pl.BlockSpec((tc, blk), rev), pl.BlockSpec((tc, blk), rev)],
        [pl.BlockSpec((tc, blk), rev), pl.BlockSpec((tc, blk), rev)],
        [jax.ShapeDtypeStruct((seq, d), F32), jax.ShapeDtypeStruct((seq, d), BF16)],
        [z, sv, wa, wx, xc, h_f],
        scratch=[pltpu.VMEM((tc, blk), F32), pltpu.VMEM((tc, blk), F32), pltpu.VMEM((SUBLANES, blk), F32)],
        job=job)


def _merge(y_pool, w_pu, y_lru, w_lu, z, seq, d, tm, tn, job=None):
    n_n = d // tn

    def kern(yp_ref, wp_ref, yl_ref, wl_ref, la_ref, lb_ref, m_ref, pa_ref, pb_ref):
        pa = jnp.dot(yp_ref[...], wp_ref[...], preferred_element_type=F32)
        pb = jnp.dot(yl_ref[...], wl_ref[...], preferred_element_type=F32)
        m_ref[...] = (_sig(la_ref[...]) * pa + _sig(lb_ref[...]) * pb).astype(BF16)
        pa_ref[...] = pa.astype(BF16)
        pb_ref[...] = pb.astype(BF16)

    row = pl.BlockSpec((tm, d), lambda i, j: (i, 0))
    wcol = pl.BlockSpec((d, tn), lambda i, j: (0, j))
    out = pl.BlockSpec((tm, tn), lambda i, j: (i, j))
    sh = jax.ShapeDtypeStruct((seq, d), BF16)
    return _pcall(
        "merge", kern, (seq // tm, n_n),
        [row, wcol, row, wcol,
         pl.BlockSpec((tm, tn), lambda i, j: (i, 3 * n_n + j)),
         pl.BlockSpec((tm, tn), lambda i, j: (i, 4 * n_n + j))],
        [out, out, out], [sh, sh, sh], [y_pool, w_pu, y_lru, w_lu, z, z], job=job)


def _ln_fwd(s, g, b):
    mu = jnp.mean(s, axis=-1, keepdims=True)
    xc = s - mu
    var = jnp.mean(xc * xc, axis=-1, keepdims=True)
    rstd = lax.rsqrt(var + LN_EPS)
    xhat = xc * rstd
    return xhat, rstd, xhat * g + b


def _ln_bwd(dy, xhat, rstd, g):
    dyg = dy * g
    m1 = jnp.mean(dyg, axis=-1, keepdims=True)
    m2 = jnp.mean(dyg * xhat, axis=-1, keepdims=True)
    return rstd * (dyg - m1 - xhat * m2)


def _pool_bwd(z, dy_pool, pw, scale, dz, seq, d, tc):
    n_g = pw.shape[0]
    pg = d // n_g
    n_chunks = seq // tc

    def kern(z_ref, dy_ref, pw_ref, sc_ref, dz_in, dz_ref, dpw_ref, dsc_ref):
        del dz_in
        g, c = pl.program_id(0), pl.program_id(1)
        hw = jnp.left_shift(1, g)
        uext = _ext(z_ref, c, n_chunks, tc, seq)
        dd = _pool_d(uext, g, c, tc, seq).astype(BF16)
        pwv = pw_ref[...]
        q = jnp.dot(dd, pwv, preferred_element_type=F32)
        dyext = _ext(dy_ref, c, n_chunks, tc, seq)

        @pl.when(c == 0)
        def _():
            dsc_ref[...] = jnp.zeros_like(dsc_ref)
            dpw_ref[...] = jnp.zeros_like(dpw_ref)

        dsc_ref[0:1, :] += _colsum(dyext[HALO:HALO + tc] * q)
        dqext = (dyext * sc_ref[...]).astype(BF16)
        dpw_ref[...] += _dot("tn", dd, dqext[HALO:HALO + tc])
        ddext = _dot("nt", dqext, pwv)
        text = c * tc - HALO + lax.broadcasted_iota(jnp.int32, (tc + 2 * HALO, 1), 0)
        v = ddext / jnp.maximum(_win_cnt(text, hw, seq), 1.0)
        dz_ref[...] = (_win_sum(v, g, 1, tc) - ddext[HALO:HALO + tc]).astype(BF16)

    return pl.pallas_call(
        kern, name="pool_bwd", grid=(n_g, n_chunks),
        in_specs=[pl.BlockSpec((seq, pg), lambda g, c: (0, g)),
                  pl.BlockSpec((seq, pg), lambda g, c: (0, g)),
                  pl.BlockSpec((None, pg, pg), lambda g, c: (g, 0, 0)),
                  pl.BlockSpec((1, pg), lambda g, c: (0, g)),
                  pl.BlockSpec(memory_space=pl.ANY)],
        out_specs=[pl.BlockSpec((None, tc, pg), lambda g, c: (2, c, g)),
                   pl.BlockSpec((None, pg, pg), lambda g, c: (g, 0, 0)),
                   pl.BlockSpec((SUBLANES, pg), lambda g, c: (0, g))],
        out_shape=[jax.ShapeDtypeStruct(dz.shape, dz.dtype),
                   jax.ShapeDtypeStruct((n_g, pg, pg), F32),
                   jax.ShapeDtypeStruct((SUBLANES, d), F32)],
        input_output_aliases={4: 0},
        compiler_params=_cparams(("arbitrary", "arbitrary")),
    )(z, dy_pool, pw, scale, dz)


def _lru_bwd(direction, xc, dh, h_dir, sv, wa, wx, dxc_prev, seq, d, tc, job=None):
    reverse = direction == 1
    n_h, blk = wa.shape[1], wa.shape[2]
    n_chunks = seq // tc
    has_prev = dxc_prev is not None

    def kern(*refs):
        xc_ref, dh_ref, h_ref, sv_ref, wa_ref, wx_ref = refs[:6]
        p = 6
        prev_ref = None
        if has_prev:
            prev_ref = refs[p]
            p += 1
        dxc_ref, dwa_ref, dwx_ref, sm_ref, at_s, g_s, carry, acarry = refs[p:p + 8]
        c = pl.program_id(1)
        cr = c if reverse else n_chunks - 1 - c
        c0 = pl.multiple_of(cr * tc, tc)

        @pl.when(c == 0)
        def _():
            carry[...] = jnp.zeros_like(carry)
            acarry[...] = jnp.zeros_like(acarry)
            dwa_ref[...] = jnp.zeros_like(dwa_ref)
            dwx_ref[...] = jnp.zeros_like(dwx_ref)
            sm_ref[...] = jnp.zeros_like(sm_ref)

        xc = xc_ref[...]
        wav, wxv = wa_ref[...], wx_ref[...]
        lam = sv_ref[SV_LAM + direction:SV_LAM + direction + 1, :]
        r, i, k, a, s = _lru_gates(xc, wav, wxv, sv_ref[SV_BA + direction:SV_BA + direction + 1, :],
                                   sv_ref[SV_BX + direction:SV_BX + direction + 1, :], lam)
        rowi = lax.broadcasted_iota(jnp.int32, (tc, blk), 0)
        hbody = h_ref[pl.ds(c0, tc), :]
        if not reverse:
            p0 = pl.multiple_of(jnp.maximum(c0 - SUBLANES, 0), SUBLANES)
            edge = jnp.where(cr > 0, h_ref[pl.ds(p0, SUBLANES), :][SUBLANES - 1:SUBLANES, :], 0.0)
            hprev = jnp.where(rowi == 0, edge, pltpu.roll(hbody, 1, 0))
            at = jnp.where(rowi == tc - 1, acarry[0:1, :], pltpu.roll(a, tc - 1, 0))
        else:
            n0 = pl.multiple_of(jnp.minimum(c0 + tc, seq - SUBLANES), SUBLANES)
            edge = jnp.where(cr < n_chunks - 1, h_ref[pl.ds(n0, SUBLANES), :][0:1, :], 0.0)
            hprev = jnp.where(rowi == tc - 1, edge, pltpu.roll(hbody, tc - 1, 0))
            at = jnp.where(rowi == 0, acarry[0:1, :], pltpu.roll(a, 1, 0))
        at_s[...] = at
        _scan_tiles(at_s, dh_ref, g_s, carry, tc // SUBLANES, not reverse)
        acarry[0:1, :] = a[tc - 1:tc, :] if reverse else a[0:1, :]

        gt = g_s[...]
        da = gt * hprev
        di = gt * s * xc
        dxc = gt * s * i
        ds = gt * (i * xc)
        dl = da * a - ds * (a * a) / s
        dpr = (dl * k) * r * (1.0 - r)
        dpi = di * i * (1.0 - i)
        sm_ref[0:1, :] += _colsum(dpr)
        sm_ref[1:2, :] += _colsum(dpi)
        sm_ref[2:3, :] += _colsum(dl * r) * (LRU_C * _sig(-lam))
        xb, dprb, dpib = xc.astype(BF16), dpr.astype(BF16), dpi.astype(BF16)
        dwa_ref[...] += _dot("tn", xb, dprb)
        dwx_ref[...] += _dot("tn", xb, dpib)
        dxc = dxc + _dot("nt", dprb, wav) + _dot("nt", dpib, wxv)
        if has_prev:
            dxc = dxc + prev_ref[...]
        dxc_ref[...] = dxc

    if reverse:
        chunk = lambda h, c: (c, h)
    else:
        chunk = lambda h, c: (n_chunks - 1 - c, h)
    wspec = pl.BlockSpec((None, None, blk, blk), lambda h, c: (direction, h, 0, 0))
    ins = [xc, dh, h_dir, sv, wa, wx] + ([dxc_prev] if has_prev else [])
    in_specs = [pl.BlockSpec((tc, blk), chunk), pl.BlockSpec((tc, blk), chunk),
                pl.BlockSpec((seq, blk), lambda h, c: (0, h)),
                pl.BlockSpec((16, blk), lambda h, c: (0, h)), wspec, wspec]
    if has_prev:
        in_specs.append(pl.BlockSpec((tc, blk), chunk))
    return _pcall(
        "lru_bwd_%d" % direction, kern, (n_h, n_chunks), in_specs,
        [pl.BlockSpec((tc, blk), chunk),
         pl.BlockSpec((None, blk, blk), lambda h, c: (h, 0, 0)),
         pl.BlockSpec((None, blk, blk), lambda h, c: (h, 0, 0)),
         pl.BlockSpec((SUBLANES, blk), lambda h, c: (0, h))],
        [jax.ShapeDtypeStruct((seq, d), F32),
         jax.ShapeDtypeStruct((n_h, blk, blk), F32),
         jax.ShapeDtypeStruct((n_h, blk, blk), F32),
         jax.ShapeDtypeStruct((SUBLANES, d), F32)],
        ins,
        scratch=[pltpu.VMEM((tc, blk), F32), pltpu.VMEM((tc, blk), F32),
                 pltpu.VMEM((SUBLANES, blk), F32), pltpu.VMEM((SUBLANES, blk), F32)],
        job=job)


def _conv_bwd(z, dxc, sv, dz, seq, d, tc, tcol):
    n_chunks = seq // tc
    lru_off = d // tcol

    def kern(z_ref, dx_ref, sv_ref, dz_in, dz_ref, sm_ref):
        del dz_in
        c = pl.program_id(1)
        uext = _ext(z_ref, c, n_chunks, tc, seq)
        dext = _ext(dx_ref, c, n_chunks, tc, seq)
        dbody = dext[HALO:HALO + tc]

        @pl.when(c == 0)
        def _():
            sm_ref[...] = jnp.zeros_like(sm_ref)

        du = jnp.zeros_like(dbody)
        for k in range(4):
            du = du + _shifted(dext, 2 - k, tc) * sv_ref[SV_CONV + k:SV_CONV + k + 1, :]
            sm_ref[k:k + 1, :] += _colsum(dbody * _shifted(uext, k - 2, tc))
        sm_ref[4:5, :] += _colsum(dbody)
        dz_ref[...] = du.astype(BF16)

    return pl.pallas_call(
        kern, name="conv_bwd", grid=(d // tcol, n_chunks),
        in_specs=[pl.BlockSpec((seq, tcol), lambda j, c: (0, lru_off + j)),
                  pl.BlockSpec((seq, tcol), lambda j, c: (0, j)),
                  pl.BlockSpec((16, tcol), lambda j, c: (0, j)),
                  pl.BlockSpec(memory_space=pl.ANY)],
        out_specs=[pl.BlockSpec((None, tc, tcol), lambda j, c: (3, c, j)),
                   pl.BlockSpec((SUBLANES, tcol), lambda j, c: (0, j))],
        out_shape=[jax.ShapeDtypeStruct(dz.shape, dz.dtype), jax.ShapeDtypeStruct((SUBLANES, d), F32)],
        input_output_aliases={3: 0},
        compiler_params=_cparams(("arbitrary", "arbitrary")),
    )(z, dxc, sv, dz)


BIG = (("w_in", "col"), ("pool_w", "row"), ("lru_wa", "row"), ("lru_wx", "row"), ("w_pool_up", "row"),
       ("w_lru_up", "row"), ("w_out", "row"), ("w_ff1", "col"), ("w_ff2", "row"))


def _shard_view(w, fam):
    if fam == "col":
        return w.reshape(w.shape[-2:])
    return w.reshape((-1,) + w.shape[-2:])


def _full_shape(sv_shape, fam):
    if fam == "col":
        return (sv_shape[0], N_DEV * sv_shape[1])
    return (sv_shape[0], N_DEV * sv_shape[1], sv_shape[2])


def _slot(ref, fam, p, size, part=None):
    if fam == "lead":
        return ref.at[p]
    k, n, span = part or (0, 1, 1)
    unit = size // n
    start = pl.multiple_of(p * size + k * unit, unit)
    if fam == "col":
        return ref.at[:, pl.ds(start, span * unit)]
    return ref.at[:, pl.ds(start, span * unit), :]


def _shard_part(ref, fam, size, part):
    if part is None or fam == "lead":
        return ref
    k, n, span = part
    unit = size // n
    if fam == "col":
        return ref.at[:, pl.ds(k * unit, span * unit)]
    return ref.at[:, pl.ds(k * unit, span * unit), :]


def _shard_extent(shape, fam):
    return shape[1]


def _coords():
    return lax.axis_index("x"), lax.axis_index("y"), lax.axis_index("c")


def _ag_job(shards, fams, parts=None, into=None):
    n = len(shards)
    parts = list(parts) if parts is not None else [None] * n
    into = list(into) if into is not None else [None] * n
    fulls = []
    for s, fam in zip(shards, fams):
        if fam == "lead":
            fulls.append(jax.ShapeDtypeStruct((N_DEV,) + s.shape, s.dtype))
        else:
            fulls.append(jax.ShapeDtypeStruct(_full_shape(s.shape, fam), s.dtype))
    sizes = [1 if fam == "lead" else _shard_extent(s.shape, fam) for s, fam in zip(shards, fams)]
    given = [a for a in range(n) if into[a] is not None]

    def ctx(ins, outs, sems):
        send, recv, loc = sems
        x, y, c = _coords()
        chips = [(1 - x, y), (x, 1 - y), (1 - x, 1 - y)]

        def mine(a):
            return _shard_part(ins[a], fams[a], sizes[a], parts[a])

        def copy(a, k, owner, to, src=None):
            dst = _slot(outs[a], fams[a], owner, sizes[a], parts[a])
            return pltpu.make_async_remote_copy(
                src_ref=dst if src is None else src, dst_ref=dst,
                send_sem=send.at[a, k], recv_sem=recv.at[a, k], device_id=to, device_id_type=MESH)

        def local(a):
            return pltpu.make_async_copy(
                mine(a), _slot(outs[a], fams[a], 4 * x + 2 * y + c, sizes[a], parts[a]), loc.at[a])

        return x, y, c, chips, copy, local, mine

    def start(ins, outs, sems):
        x, y, c, chips, copy, local, mine = ctx(ins, outs, sems)
        me = 4 * x + 2 * y + c
        for a in range(n):
            local(a).start()
            copy(a, 0, me, (x, y, 1 - c), mine(a)).start()
            for j, (cx, cy) in enumerate(chips):
                copy(a, 1 + j, me, (cx, cy, c), mine(a)).start()

    def mid(ins, outs, sems):
        x, y, c, chips, copy, _, _ = ctx(ins, outs, sems)
        for a in range(n):
            for j, (cx, cy) in enumerate(chips):
                owner = 4 * cx + 2 * cy + c
                copy(a, 1 + j, owner, (x, y, c)).wait_recv()
                copy(a, 4 + j, owner, (x, y, 1 - c)).start()

    def finish(ins, outs, sems):
        x, y, c, chips, copy, local, mine = ctx(ins, outs, sems)
        me = 4 * x + 2 * y + c
        for a in range(n):
            copy(a, 0, 4 * x + 2 * y + (1 - c), (x, y, c)).wait_recv()
            for j, (cx, cy) in enumerate(chips):
                copy(a, 4 + j, 4 * cx + 2 * cy + (1 - c), (x, y, c)).wait_recv()
            copy(a, 0, me, (x, y, 1 - c), mine(a)).wait_send()
            for j, (cx, cy) in enumerate(chips):
                copy(a, 1 + j, me, (cx, cy, c), mine(a)).wait_send()
                copy(a, 4 + j, 4 * cx + 2 * cy + c, (x, y, 1 - c)).wait_send()
            local(a).wait()

    sems = [pltpu.SemaphoreType.DMA((n, 7)), pltpu.SemaphoreType.DMA((n, 7)), pltpu.SemaphoreType.DMA((n,))]
    job = _Job(list(shards) + [into[a] for a in given], fulls, sems, start, finish, mid)
    job.alias = {n + i: a for i, a in enumerate(given)}
    return job


def _rs_sibling_job(fulls, fams, sizes):
    n = len(fulls)
    outs = []
    for f, fam, sz in zip(fulls, fams, sizes):
        if fam == "col":
            outs.append(jax.ShapeDtypeStruct((4, f.shape[0], sz), f.dtype))
        else:
            outs.append(jax.ShapeDtypeStruct((4, f.shape[0], sz, f.shape[2]), f.dtype))

    def copies(ins, rcv, sems):
        send, recv = sems
        x, y, c = _coords()
        return [pltpu.make_async_remote_copy(
            src_ref=_slot(ins[a], fams[a], 2 * q + (1 - c), sizes[a]), dst_ref=rcv[a].at[q],
            send_sem=send.at[a, q], recv_sem=recv.at[a, q], device_id=(x, y, 1 - c), device_id_type=MESH)
            for a in range(n) for q in range(4)]

    def start(ins, rcv, sems):
        for cp in copies(ins, rcv, sems):
            cp.start()

    def finish(ins, rcv, sems):
        for cp in copies(ins, rcv, sems):
            cp.wait()

    return _Job(fulls, outs, [pltpu.SemaphoreType.DMA((n, 4)), pltpu.SemaphoreType.DMA((n, 4))], start, finish)


def _rs_chips_job(parts):
    n = len(parts)
    outs = [jax.ShapeDtypeStruct((3,) + p.shape[1:], p.dtype) for p in parts]

    def copies(ins, rcv, sems):
        send, recv = sems
        x, y, c = _coords()
        cps = []
        for a in range(n):
            for r in (1, 2, 3):
                tx, ty = (1 - x) if r & 2 else x, (1 - y) if r & 1 else y
                cps.append(pltpu.make_async_remote_copy(
                    src_ref=ins[a].at[2 * tx + ty], dst_ref=rcv[a].at[r - 1],
                    send_sem=send.at[a, r - 1], recv_sem=recv.at[a, r - 1],
                    device_id=(tx, ty, c), device_id_type=MESH))
        return cps

    def start(ins, rcv, sems):
        for cp in copies(ins, rcv, sems):
            cp.start()

    def finish(ins, rcv, sems):
        for cp in copies(ins, rcv, sems):
            cp.wait()

    return _Job(parts, outs, [pltpu.SemaphoreType.DMA((n, 3)), pltpu.SemaphoreType.DMA((n, 3))], start, finish)


def _tile_rows(rows, cols):
    tr = rows
    while tr * cols > (1 << 18) and tr % (2 * SUBLANES) == 0:
        tr //= 2
    return tr


def _rs_add(name, full, recv_a, fam, size, cidx):
    if fam == "col":
        rows = full.shape[0]
        tr = _tile_rows(rows, size)
        grid = (4, rows // tr)
        f_spec = pl.BlockSpec((tr, size), lambda q, i, cr: (i, 2 * q + cr[0]))
        s_spec = pl.BlockSpec((None, tr, size), lambda q, i, cr: (q, i, 0))
    else:
        nb, cols = full.shape[0], full.shape[2]
        tr = _tile_rows(nb * size, cols) // nb if nb > 1 else _tile_rows(size, cols)
        nt = size // tr
        grid = (4, nt)
        f_spec = pl.BlockSpec((nb, tr, cols), lambda q, i, cr: (0, (2 * q + cr[0]) * nt + i, 0))
        s_spec = pl.BlockSpec((None, nb, tr, cols), lambda q, i, cr: (q, 0, i, 0))

    def kern(c_ref, f_ref, r_ref, o_ref):
        del c_ref
        o_ref[...] = (f_ref[...] + r_ref[...]).astype(BF16)

    return pl.pallas_call(
        kern, name=name,
        grid_spec=pltpu.PrefetchScalarGridSpec(num_scalar_prefetch=1, grid=grid, in_specs=[f_spec, s_spec],
                                               out_specs=s_spec),
        out_shape=jax.ShapeDtypeStruct(recv_a.shape, BF16),
        compiler_params=_cparams(("arbitrary", "arbitrary"), 32),
    )(cidx, full, recv_a)


def _adam(w, g, m, v):
    m2 = ADAM_B1 * m + (1.0 - ADAM_B1) * g
    v2 = ADAM_B2 * v + (1.0 - ADAM_B2) * (g * g)
    m_hat = m2 / (1.0 - ADAM_B1 ** ADAM_STEP)
    v_hat = v2 / (1.0 - ADAM_B2 ** ADAM_STEP)
    delta = -ADAM_LR * (m_hat / (jnp.sqrt(v_hat) + ADAM_EPS) + ADAM_WD * w)
    return delta, m2, v2


def _rs_final_adam(name, parts, recv_b, w, m, v, fam, qidx):
    shp = w.shape
    pieces = parts if isinstance(parts, (list, tuple)) else [parts]
    recvs = recv_b if isinstance(recv_b, (list, tuple)) else [recv_b]
    n_p = len(pieces)
    if fam == "col":
        rows, cols = shp
        tr = _tile_rows(rows // n_p, cols)
        per = rows // n_p // tr
        grid = (rows // tr,)
        w_spec = pl.BlockSpec((tr, cols), lambda i, qr: (i, 0))

        def piece_row(i, h):
            return jnp.clip(i - h * per, 0, per - 1)

        p_specs = [pl.BlockSpec((None, tr, cols), lambda i, qr, h=h: (qr[0], piece_row(i, h), 0))
                   for h in range(n_p)]
        r_specs = [pl.BlockSpec((3, tr, cols), lambda i, qr, h=h: (0, piece_row(i, h), 0)) for h in range(n_p)]
    else:
        assert n_p == 1
        per = None
        nb, rows, cols = shp
        tr = _tile_rows(rows, cols)
        nt = rows // tr
        grid = (nb * nt,)
        w_spec = pl.BlockSpec((None, tr, cols), lambda i, qr: (i // nt, i % nt, 0))
        p_specs = [pl.BlockSpec((None, None, tr, cols), lambda i, qr: (qr[0], i // nt, i % nt, 0))]
        r_specs = [pl.BlockSpec((3, None, tr, cols), lambda i, qr: (0, i // nt, i % nt, 0))]

    def kern(*refs):
        p_refs, r_refs = refs[1:1 + n_p], refs[1 + n_p:1 + 2 * n_p]
        w_ref, m_ref, v_ref, g_out, d_out, m_out, v_out = refs[1 + 2 * n_p:]

        def total(h):
            p_ref, r_ref = p_refs[h], r_refs[h]
            return ((p_ref[...].astype(F32) + r_ref[0].astype(F32)) + r_ref[1].astype(F32)) + r_ref[2].astype(F32)

        g = total(0)
        for h in range(1, n_p):
            g = jnp.where(pl.program_id(0) >= h * per, total(h), g)
        delta, m2, v2 = _adam(w_ref[...], g, m_ref[...], v_ref[...])
        g_out[...] = g
        d_out[...] = delta
        m_out[...] = m2
        v_out[...] = v2

    sh = jax.ShapeDtypeStruct(shp, F32)
    return pl.pallas_call(
        kern, name=name,
        grid_spec=pltpu.PrefetchScalarGridSpec(
            num_scalar_prefetch=1, grid=grid, in_specs=p_specs + r_specs + [w_spec, w_spec, w_spec],
            out_specs=[w_spec] * 4),
        out_shape=[sh] * 4,
        compiler_params=_cparams(("arbitrary",), 32),
    )(qidx, *pieces, *recvs, w, m, v)


def _sum8(name, parts):
    def kern(p_ref, o_ref):
        acc = p_ref[0]
        for p in range(1, N_DEV):
            acc = acc + p_ref[p]
        o_ref[...] = acc

    return pl.pallas_call(
        kern, name=name, out_shape=jax.ShapeDtypeStruct(parts.shape[1:], F32),
        compiler_params=pltpu.CompilerParams(vmem_limit_bytes=32 << 20),
    )(parts)


def _adam_small(name, w, g, m, v):
    def kern(w_ref, g_ref, m_ref, v_ref, d_out, m_out, v_out):
        delta, m2, v2 = _adam(w_ref[...], g_ref[...], m_ref[...], v_ref[...])
        d_out[...] = delta
        m_out[...] = m2
        v_out[...] = v2

    sh = jax.ShapeDtypeStruct(w.shape, F32)
    return pl.pallas_call(kern, name=name, out_shape=[sh] * 3)(w, g, m, v)


class _NoComm:
    def __init__(self):
        self.grads = {}

    def job(self, host):
        return None

    def after(self, host, job):
        pass

    def grad(self, name, g):
        self.grads[name] = g


def _local_step(x, target, wts, vec, comm=None):
    comm = comm or _NoComm()
    seq, d = x.shape
    pool_w, wa, wx = wts["pool_w"], wts["lru_wa"], wts["lru_wx"]
    sv = vec["sv"]
    ff = vec["b_ff1"].shape[1]
    n_in = 5 * d

    def hosted(host, call):
        job = comm.job(host)
        res = call(job)
        comm.after(host, job)
        return res
    blk = wa.shape[2]
    tc = min(512, seq)
    t1k, t512, t256 = min(1024, seq), min(512, seq), min(256, seq)
    n512 = min(512, d)
    tkd = d
    tkf = min(1024, ff)
    sub_rows = min(64, seq)

    x_bf = x.astype(BF16)
    full = lambda i, j, k: (0, 0)

    def epi_store(acc, i, j, ex, outs):
        outs[0][...] = acc

    n_pass = W_IN_PASSES
    piece = n_in // (N_DEV * n_pass)
    tz = min(2048, seq)
    z = None
    for k in range(n_pass):
        w_piece = wts["w_in_piece_%d" % k]
        prev = [] if z is None else [(z, (None, None))]
        (z,) = hosted("z_proj_%d" % k, lambda job: _mm(
            "z_proj_%d" % k, "nn", (seq // tz, N_DEV, 1),
            x_bf, ((tz, d), lambda i, j, kk: (i, 0)), w_piece, ((d, piece), lambda i, j, kk: (0, j)),
            prev, [jax.ShapeDtypeStruct((seq, n_in), F32)],
            [((tz, piece), lambda i, j, kk, k=k: (i, n_pass * j + k))], epi_store,
            aliases={2: 0} if prev else None, job=job))

    y_pool = hosted("pool_fwd", lambda job: _pool_fwd(z, pool_w, vec["pool_scale"], seq, d, tc, job=job))
    xc, h_f = hosted("lru_fwd", lambda job: _lru_fwd(z, sv, vec["conv_b"], wa, wx, seq, d, tc, job=job))
    h_b, y_lru = hosted("lru_rev", lambda job: _lru_rev(z, sv, wa, wx, xc, h_f, seq, d, tc, job=job))
    w_pu, w_lu = wts["w_pool_up"], wts["w_lru_up"]
    m_bf, p_a, p_b = hosted("merge", lambda job: _merge(y_pool, w_pu, y_lru, w_lu, z, seq, d, t512, n512, job=job))
    w_out = wts["w_out"]

    def epi_ln1(acc, i, j, ex, outs):
        x_ref, bo, g1, b1 = ex
        s1 = DN_ALPHA * x_ref[...] + (acc + bo[...])
        xhat, rstd, x1 = _ln_fwd(s1, g1[...], b1[...])
        outs[0][...] = xhat
        outs[1][...] = x1.astype(BF16)
        outs[2][...] = rstd

    rowd = lambda t: ((t, d), lambda i, j, k: (i, 0))
    vecd = ((1, d), full)
    xhat1, x1_bf, rstd1 = hosted("out_ln1", lambda job: _mm(
        "out_ln1", "nn", (seq // t256, 1, 1), m_bf, rowd(t256), w_out, ((d, d), full),
        [(x, rowd(t256)), (vec["b_out"], vecd), (vec["ln1_g"], vecd), (vec["ln1_b"], vecd)],
        [jax.ShapeDtypeStruct((seq, d), F32), jax.ShapeDtypeStruct((seq, d), BF16),
         jax.ShapeDtypeStruct((seq, 1), F32)],
        [rowd(t256), rowd(t256), ((t256, 1), lambda i, j, k: (i, 0))], epi_ln1, job=job))
    w1 = wts["w_ff1"]

    def epi_ff1(acc, i, j, ex, outs):
        r = jnp.maximum(acc + ex[0][...], 0.0)
        outs[0][...] = r.astype(BF16)
        outs[1][...] = (r * r).astype(BF16)

    tile_f = ((t1k, n512), lambda i, j, k: (i, j))
    relu_h, hdn = hosted("ff1", lambda job: _mm(
        "ff1", "nn", (seq // t1k, ff // n512, 1), x1_bf, rowd(t1k), w1, ((d, n512), lambda i, j, k: (0, j)),
        [(vec["b_ff1"], ((1, n512), lambda i, j, k: (0, j)))],
        [jax.ShapeDtypeStruct((seq, ff), BF16)] * 2, [tile_f, tile_f], epi_ff1, job=job))
    w2 = wts["w_ff2"]

    def epi_ln2(acc, i, j, ex, outs):
        xh1, tgt, g1, b1, bf2, g2, b2 = ex
        ds_ref, dsb_ref, sm_ref, loss_ref = outs
        x1 = xh1[...] * g1[...] + b1[...]
        s2 = DN_ALPHA * x1 + (acc + bf2[...])
        xhat, rstd, y = _ln_fwd(s2, g2[...], b2[...])
        e = y - tgt[...]
        part = 0.5 * jnp.sum(jnp.mean(e * e, axis=-1, keepdims=True))
        dy = e * (1.0 / d)
        ds2 = _ln_bwd(dy, xhat, rstd, g2[...])
        ds_ref[...] = ds2
        dsb_ref[...] = ds2.astype(BF16)
        sm_ref[0:1, :] += _colsum(dy * xhat)
        sm_ref[1:2, :] += _colsum(dy)
        sm_ref[2:3, :] += _colsum(ds2)
        loss_ref[...] += jnp.full(loss_ref.shape, part, F32)

    def zero_tail(n_tail):
        def init(i, j, outs):
            @pl.when(i == 0)
            def _():
                for o in outs[-n_tail:]:
                    o[...] = jnp.zeros_like(o)
        return init

    ds2, ds2_bf, sm_ln2, loss_blk = hosted("ff2_ln2", lambda job: _mm(
        "ff2_ln2", "nn", (seq // t512, 1, ff // tkf), hdn, ((t512, tkf), lambda i, j, k: (i, k)),
        w2, ((tkf, d), lambda i, j, k: (k, 0)),
        [(xhat1, rowd(t512)), (target, rowd(t512)), (vec["ln1_g"], vecd), (vec["ln1_b"], vecd),
         (vec["b_ff2"], vecd), (vec["ln2_g"], vecd), (vec["ln2_b"], vecd)],
        [jax.ShapeDtypeStruct((seq, d), F32), jax.ShapeDtypeStruct((seq, d), BF16),
         jax.ShapeDtypeStruct((SUBLANES, d), F32), jax.ShapeDtypeStruct((SUBLANES, 128), F32)],
        [rowd(t512), rowd(t512), ((SUBLANES, d), full), ((SUBLANES, 128), full)],
        epi_ln2, acc_shape=(t512, d), epi_init=zero_tail(2), sub=sub_rows, job=job))

    tkw = min(2048, seq)

    def dw(name, wname, a, b, m_dim, n_dim, b_spec=None, row0=0):
        tm, tn = min(1024, m_dim), min(1024, n_dim)
        b_spec = b_spec or ((tkw, tn), lambda i, j, k: (k, j))
        i0 = row0 // tm
        (out,) = hosted(name, lambda job: _mm(
            name, "tn", (m_dim // tm, n_dim // tn, seq // tkw),
            a, ((tkw, tm), lambda i, j, k: (k, i0 + i)), b, b_spec, [],
            [jax.ShapeDtypeStruct((m_dim, n_dim), F32)], [((tm, tn), lambda i, j, k: (i, j))],
            epi_store if seq == tkw else None, job=job))
        comm.grad(wname, out)

    dw("dw_ff2", "w_ff2", hdn, ds2_bf, ff, d)

    def epi_dpre(acc, i, j, ex, outs):
        dpre = acc * (2.0 * ex[0][...].astype(F32))
        outs[0][...] = dpre.astype(BF16)

        @pl.when(i == 0)
        def _():
            outs[1][...] = jnp.zeros_like(outs[1])

        outs[1][0:1, :] += _colsum(dpre)

    dpre, sm_bff1 = hosted("dhdn", lambda job: _mm(
        "dhdn", "nt", (seq // t1k, ff // n512, 1), ds2_bf, rowd(t1k), w2, ((n512, d), lambda i, j, k: (j, 0)),
        [(relu_h, tile_f)],
        [jax.ShapeDtypeStruct((seq, ff), BF16), jax.ShapeDtypeStruct((SUBLANES, ff), F32)],
        [tile_f, ((SUBLANES, n512), lambda i, j, k: (0, j))], epi_dpre, order="ji", job=job))

    dw("dw_ff1", "w_ff1", x1_bf, dpre, d, ff)

    def epi_ln1b(acc, i, j, ex, outs):
        ds2_ref, xh1, rs1, g1 = ex
        ds_ref, dsb_ref, sm_ref = outs
        dy1 = acc + DN_ALPHA * ds2_ref[...]
        xhat = xh1[...]
        ds1 = _ln_bwd(dy1, xhat, rs1[...], g1[...])
        ds_ref[...] = ds1
        dsb_ref[...] = ds1.astype(BF16)
        sm_ref[0:1, :] += _colsum(dy1 * xhat)
        sm_ref[1:2, :] += _colsum(dy1)
        sm_ref[2:3, :] += _colsum(ds1)

    ds1, ds1_bf, sm_ln1 = hosted("dx1_ln1", lambda job: _mm(
        "dx1_ln1", "nt", (seq // t512, 1, ff // tkf), dpre, ((t512, tkf), lambda i, j, k: (i, k)),
        w1, ((d, tkf), lambda i, j, k: (0, k)),
        [(ds2, rowd(t512)), (xhat1, rowd(t512)), (rstd1, ((t512, 1), lambda i, j, k: (i, 0))),
         (vec["ln1_g"], vecd)],
        [jax.ShapeDtypeStruct((seq, d), F32), jax.ShapeDtypeStruct((seq, d), BF16),
         jax.ShapeDtypeStruct((SUBLANES, d), F32)],
        [rowd(t512), rowd(t512), ((SUBLANES, d), full)], epi_ln1b, acc_shape=(t512, d),
        epi_init=zero_tail(1), sub=sub_rows, job=job))

    dw("dw_out", "w_out", m_bf, ds1_bf, d, d)
    n_n = d // n512
    tile_d = ((t512, n512), lambda i, j, k: (i, j))

    def epi_dm(acc, i, j, ex, outs):
        la, lb, pa, pb = ex
        ga, gb = _sig(la[...]), _sig(lb[...])
        outs[0][...] = (acc * ga).astype(BF16)
        outs[1][...] = (acc * gb).astype(BF16)
        outs[2][0] = (acc * pa[...].astype(F32) * ga * (1.0 - ga)).astype(BF16)
        outs[2][1] = (acc * pb[...].astype(F32) * gb * (1.0 - gb)).astype(BF16)

    dp_a, dp_b, dz = _mm(
        "dm", "nt", (seq // t512, n_n, 1), ds1_bf, rowd(t512), w_out, ((n512, d), lambda i, j, k: (j, 0)),
        [(z, ((t512, n512), lambda i, j, k: (i, 3 * n_n + j))),
         (z, ((t512, n512), lambda i, j, k: (i, 4 * n_n + j))), (p_a, tile_d), (p_b, tile_d)],
        [jax.ShapeDtypeStruct((seq, d), BF16), jax.ShapeDtypeStruct((seq, d), BF16),
         jax.ShapeDtypeStruct((5, seq, d), BF16)],
        [tile_d, tile_d, ((2, t512, n512), lambda i, j, k: (0, i, j))], epi_dm)

    dw("dw_pool_up", "w_pool_up", y_pool, dp_a, d, d)
    dw("dw_lru_up", "w_lru_up", y_lru, dp_b, d, d)

    def epi_bf(acc, i, j, ex, outs):
        outs[0][...] = acc.astype(BF16)

    (dy_pool,) = _mm("dy_pool", "nt", (seq // t512, n_n, 1), dp_a, rowd(t512), w_pu,
                     ((n512, d), lambda i, j, k: (j, 0)), [],
                     [jax.ShapeDtypeStruct((seq, d), BF16)], [tile_d], epi_bf)

    def epi_dylru(acc, i, j, ex, outs):
        hf, hb, ug, _ = ex
        u = ug[...]
        outs[0][...] = acc * _gelu(u)
        outs[1][...] = (acc * (hf[...] + hb[...]) * _gelu_grad(u)).astype(BF16)

    dz_in = dz
    dh, dz = hosted("dy_lru", lambda job: _mm(
        "dy_lru", "nt", (seq // t512, n_n, 1), dp_b, rowd(t512), w_lu, ((n512, d), lambda i, j, k: (j, 0)),
        [(h_f, tile_d), (h_b, tile_d), (z, ((t512, n512), lambda i, j, k: (i, 2 * n_n + j))),
         (dz_in, (None, None))],
        [jax.ShapeDtypeStruct((seq, d), F32), jax.ShapeDtypeStruct(dz_in.shape, BF16)],
        [tile_d, ((None, t512, n512), lambda i, j, k: (4, i, j))], epi_dylru, aliases={5: 1}, job=job))

    dz, g_pw, sm_pool = _pool_bwd(z, dy_pool, pool_w, vec["pool_scale"], dz, seq, d, tc)
    comm.grad("pool_w", g_pw)
    dxc0, g_wa0, g_wx0, sm_l0 = hosted("lru_bwd_0", lambda job: _lru_bwd(
        0, xc, dh, h_f, sv, wa, wx, None, seq, d, tc, job=job))
    dxc, g_wa1, g_wx1, sm_l1 = hosted("lru_bwd_1", lambda job: _lru_bwd(
        1, xc, dh, h_b, sv, wa, wx, dxc0, seq, d, tc, job=job))
    comm.grad("lru_wa", jnp.concatenate([g_wa0, g_wa1], axis=0))
    comm.grad("lru_wx", jnp.concatenate([g_wx0, g_wx1], axis=0))
    dz, sm_conv = _conv_bwd(z, dxc, sv, dz, seq, d, tc, blk)

    tnw = min(1024, d)
    per_seg = d // tnw
    seg_spec = ((None, tkw, tnw), lambda i, j, k: ((j // per_seg + 2) % 5, k, j % per_seg))
    dw("dw_in_lo", "w_in_lo", x_bf, dz, d // 2, n_in, b_spec=seg_spec)
    dw("dw_in_hi", "w_in_hi", x_bf, dz, d // 2, n_in, b_spec=seg_spec, row0=d // 2)

    nk = d // tkd

    def epi_dx(acc, i, j, ex, outs):
        outs[0][...] = acc + DN_ALPHA * ex[0][...]

    (grad_x,) = hosted("dx", lambda job: _mm(
        "dx", "nt", (seq // t512, 1, n_in // tkd), dz,
        ((None, t512, tkd), lambda i, j, k: ((k // nk + 2) % 5, i, k % nk)),
        wts["w_in"], ((d, tkd), lambda i, j, k: (0, k)), [(ds1, rowd(t512))],
        [jax.ShapeDtypeStruct((seq, d), F32)], [rowd(t512)], epi_dx, acc_shape=(t512, d), job=job))

    small = {"ln2": sm_ln2, "b_ff1": sm_bff1, "ln1": sm_ln1, "pool": sm_pool, "lru0": sm_l0, "lru1": sm_l1,
             "conv": sm_conv}
    return loss_blk[0, 0], grad_x, small


REP = ("pool_scale", "conv_b", "b_out", "ln1_g", "ln1_b", "b_ff2", "ln2_g", "ln2_b")
SHARDED_SMALL = (("conv_w", 4), ("lru_ba", 2), ("lru_bx", 2), ("lru_lambda", 2))
WEIGHT_ORDER = ("w_in", "pool_w", "pool_scale", "conv_w", "conv_b", "lru_wa", "lru_ba", "lru_wx", "lru_bx",
                "lru_lambda", "w_pool_up", "w_lru_up", "w_out", "b_out", "ln1_g", "ln1_b", "w_ff1", "b_ff1",
                "w_ff2", "b_ff2", "ln2_g", "ln2_b")


def _pad_rows(a, rows):
    return jnp.concatenate([a, jnp.zeros((rows - a.shape[0], a.shape[1]), a.dtype)], axis=0)


def kernel(x, w_in, pool_w, pool_scale, conv_w, conv_b, lru_wa, lru_ba, lru_wx, lru_bx, lru_lambda, w_pool_up, w_lru_up, w_out, b_out, ln1_g, ln1_b, w_ff1, b_ff1, w_ff2, b_ff2, ln2_g, ln2_b, loss_target, m_w_in, m_pool_w, m_pool_scale, m_conv_w, m_conv_b, m_lru_wa, m_lru_ba, m_lru_wx, m_lru_bx, m_lru_lambda, m_w_pool_up, m_w_lru_up, m_w_out, m_b_out, m_ln1_g, m_ln1_b, m_w_ff1, m_b_ff1, m_w_ff2, m_b_ff2, m_ln2_g, m_ln2_b, v_w_in, v_pool_w, v_pool_scale, v_conv_w, v_conv_b, v_lru_wa, v_lru_ba, v_lru_wx, v_lru_bx, v_lru_lambda, v_w_pool_up, v_w_lru_up, v_w_out, v_b_out, v_ln1_g, v_ln1_b, v_w_ff1, v_b_ff1, v_w_ff2, v_b_ff2, v_ln2_g, v_ln2_b):
    args = dict(locals())
    w = {n: args[n] for n in WEIGHT_ORDER}
    mom = {n: args["m_" + n] for n in WEIGHT_ORDER}
    var = {n: args["v_" + n] for n in WEIGHT_ORDER}
    seq, d = x.shape[1], x.shape[2]
    n_heads, blk = lru_wa.shape[2], lru_wa.shape[4]
    n_groups = pool_w.shape[1]
    ff = b_ff1.shape[1]
    cx, cy, cc = _coords()
    me = 4 * cx + 2 * cy + cc
    cidx = jnp.reshape(cc, (1,)).astype(jnp.int32)
    qidx = jnp.reshape(2 * cx + cy, (1,)).astype(jnp.int32)

    fam_of = dict(BIG)
    sviews = {n: _shard_view(w[n], fam) for n, fam in BIG}
    size_of = {n: sviews[n].shape[1] for n, _ in BIG}
    for half in ("w_in_lo", "w_in_hi"):
        fam_of[half], size_of[half] = fam_of["w_in"], size_of["w_in"]
    wts = {}

    def take_gathered(names, arrays):
        for n, g in zip(names, arrays):
            if n == "pool_w":
                g = g.reshape(n_groups, d // n_groups, d // n_groups)
            elif n in ("lru_wa", "lru_wx"):
                g = g.reshape(2, n_heads, blk, blk)
            elif fam_of[n] == "row":
                g = g.reshape(g.shape[1:])
            wts[n] = g

    shard_bf = {n: sviews[n].astype(BF16) for n, _ in BIG}
    piece = sviews["w_in"].shape[1] // W_IN_PASSES
    for k in range(W_IN_PASSES):
        name = "w_in_piece_%d" % k
        shard_bf[name] = shard_bf["w_in"][:, k * piece:(k + 1) * piece]
        fam_of[name] = "col"
    partial = {}

    def gather_job(items, extra=()):
        return _ag_job([shard_bf[n] for n, _ in items] + [e for e, _ in extra],
                       [fam_of[n] for n, _ in items] + [f for _, f in extra],
                       parts=[p for _, p in items] + [None] * len(extra),
                       into=[partial.get(n) if p else None for n, p in items] + [None] * len(extra))

    def take_pieces(items, arrays):
        for (n, p), g in zip(items, arrays):
            if p is None or p[0] + p[2] == p[1]:
                partial.pop(n, None)
                take_gathered([n], [g])
            else:
                partial[n] = g

    class Plan:
        gather = {"z_proj_0": (("w_in_piece_1", None),), "z_proj_1": (("w_in_piece_2", None),),
                  "z_proj_2": (("w_in_piece_3", None),), "z_proj_3": (("w_in_piece_4", None),),
                  "z_proj_4": (("w_pool_up", None),), "pool_fwd": (("w_lru_up", None),),
                  "lru_fwd": (("w_out", None), ("w_ff1", (0, 4, 1))), "lru_rev": (("w_ff1", (1, 4, 2)),),
                  "merge": (("w_ff1", (3, 4, 1)), ("w_ff2", (0, 4, 1))), "out_ln1": (("w_ff2", (1, 4, 1)),),
                  "ff1": (("w_ff2", (2, 4, 2)),), "ff2_ln2": (("w_in", None),)}
        to_sibling = {"dhdn": ("w_ff2",), "dx1_ln1": ("w_ff1",), "dy_lru": ("w_out", "w_pool_up", "w_lru_up"),
                      "dw_in_hi": ("w_in_lo",), "dx": ("w_in_hi", "pool_w", "lru_wa", "lru_wx")}
        to_chips = {"dw_ff1": ("w_ff2",), "lru_bwd_0": ("w_ff1",), "lru_bwd_1": ("w_out", "w_pool_up", "w_lru_up"),
                    "dx": ("w_in_lo",)}

        def __init__(self):
            self.grads, self.parts, self.recv_b = {}, {}, {}

        def grad(self, name, g):
            self.grads[name] = g if fam_of[name] == "col" else g.reshape((-1,) + g.shape[-2:])

        def job(self, host):
            jobs = []
            if host in self.gather:
                jobs.append(gather_job(self.gather[host]))
            if host in self.to_sibling:
                names = self.to_sibling[host]
                jobs.append(_rs_sibling_job([self.grads[n] for n in names], [fam_of[n] for n in names],
                                            [size_of[n] for n in names]))
            if host in self.to_chips:
                jobs.append(_rs_chips_job([self.parts[n] for n in self.to_chips[host]]))
            return _join_jobs(jobs)

        def after(self, host, job):
            results = iter(_job_results(job))
            if host in self.gather:
                take_pieces(self.gather[host], next(results))
            if host in self.to_sibling:
                for n, r in zip(self.to_sibling[host], next(results)):
                    self.parts[n] = _rs_add("rs_add_" + n, self.grads[n], r, fam_of[n], size_of[n], cidx)
            if host in self.to_chips:
                self.recv_b.update(zip(self.to_chips[host], next(results)))

    first = (("w_in_piece_0", None), ("pool_w", None), ("lru_wa", None), ("lru_wx", None))
    sv_shard = _pad_rows(jnp.concatenate([w[n].reshape(r, -1) for n, r in SHARDED_SMALL], axis=0), 16)
    gathered = _run_job("ag_first", gather_job(first, [(sv_shard, "col")]))
    take_pieces(first, gathered[:-1])
    vec = {n: w[n] for n in REP}
    vec["b_ff1"] = b_ff1
    vec["sv"] = gathered[-1]

    plan = Plan()
    loss_part, grad_x, small = _local_step(x.reshape(seq, d), loss_target.reshape(seq, d), wts, vec, plan)
    loss = lax.psum(loss_part, AXES)

    tail = Plan.to_sibling["dx"]
    plan.recv_b.update(zip(tail, _run_job("rs_chips_tail", _rs_chips_job([plan.parts[n] for n in tail]))))
    out_g, out_d, out_m, out_v = {}, {}, {}, {}
    for n, fam in BIG:
        halves = [n + "_lo", n + "_hi"] if n == "w_in" else [n]
        res = _rs_final_adam("adam_" + n, [plan.parts[h] for h in halves], [plan.recv_b[h] for h in halves],
                             sviews[n], _shard_view(mom[n], fam), _shard_view(var[n], fam), fam, qidx)
        out_g[n], out_d[n], out_m[n], out_v[n] = [r.reshape(w[n].shape) for r in res]

    rows = [small["pool"][0:1], small["conv"][4:5], small["ln1"][2:3], small["ln1"][0:1], small["ln1"][1:2],
            small["ln2"][2:3], small["ln2"][0:1], small["ln2"][1:2], small["b_ff1"][0:1].reshape(ff // d, d),
            small["conv"][0:4], small["lru0"][0:1], small["lru1"][0:1], small["lru0"][1:2], small["lru1"][1:2],
            small["lru0"][2:3], small["lru1"][2:3]]
    n_rep = len(REP) + ff // d
    n_rows = n_rep + sum(r for _, r in SHARDED_SMALL)
    pad_rows = -(-n_rows // SUBLANES) * SUBLANES
    packed = _pad_rows(jnp.concatenate(rows, axis=0), pad_rows)
    (all_small,) = _run_job("ag_small", _ag_job([packed], ["lead"]))
    g_small = _sum8("sum_small", all_small)

    def pack_rep(t):
        return jnp.concatenate([t[n] for n in REP] + [t["b_ff1"].reshape(ff // d, d)], axis=0)

    def pack_sh(t):
        return jnp.concatenate([t[n].reshape(r, -1) for n, r in SHARDED_SMALL], axis=0)

    g_rep = g_small[:n_rep]
    cs = d // N_DEV
    g_sh = lax.dynamic_slice_in_dim(g_small[n_rep:n_rows], me * cs, cs, axis=1)
    d_rep, m_rep, v_rep = _adam_small("adam_rep", pack_rep(w), g_rep, pack_rep(mom), pack_rep(var))
    d_sh, m_sh, v_sh = _adam_small("adam_sharded", pack_sh(w), g_sh, pack_sh(mom), pack_sh(var))

    def unpack(rep_t, sh_t, dst):
        for i, n in enumerate(REP):
            dst[n] = rep_t[i:i + 1].reshape(w[n].shape)
        dst["b_ff1"] = rep_t[len(REP):n_rep].reshape(w["b_ff1"].shape)
        r0 = 0
        for n, r in SHARDED_SMALL:
            dst[n] = sh_t[r0:r0 + r].reshape(w[n].shape)
            r0 += r

    unpack(g_rep, g_sh, out_g)
    unpack(d_rep, d_sh, out_d)
    unpack(m_rep, m_sh, out_m)
    unpack(v_rep, v_sh, out_v)

    outs = [loss, grad_x.reshape(x.shape)]
    for t in (out_g, out_d, out_m, out_v):
        outs += [t[n] for n in WEIGHT_ORDER]
    return tuple(outs)
```

```python
import functools

import jax
import jax.numpy as jnp
from jax import lax
from jax.experimental import pallas as pl
from jax.experimental.pallas import tpu as pltpu

F32 = jnp.float32
BF16 = jnp.bfloat16
MESH = pl.DeviceIdType.MESH
AXES = ("x", "y", "c")
N_DEV = 8

DN_ALPHA = 2.0 ** 0.25
LN_EPS = 1e-5
LRU_C = 8.0
ADAM_LR = 0.001
ADAM_B1 = 0.9
ADAM_B2 = 0.999
ADAM_EPS = 1e-08
ADAM_WD = 0.01
ADAM_STEP = 10
GELU_C = 0.7978845608028654
GELU_K = 0.044715

W_IN_PASSES = 5
HALO = 16
SUBLANES = 8
VMEM_MB = 56


def _cparams(sem, vmem_mb=VMEM_MB):
    return pltpu.CompilerParams(dimension_semantics=sem, vmem_limit_bytes=vmem_mb << 20)


HBM_SPEC = pl.BlockSpec(memory_space=pl.ANY)


class _Job:
    def __init__(self, ins, outs, sems, start, finish, mid=None):
        self.ins, self.outs, self.sems = list(ins), list(outs), list(sems)
        self.start, self.finish, self.mid = start, finish, mid
        self.alias = {}
        self.mid_frac = 0.75
        self.results = None


def _join_jobs(jobs):
    jobs = [j for j in jobs if j is not None]
    if len(jobs) <= 1:
        return jobs[0] if jobs else None

    def split(refs, counts):
        out, p = [], 0
        for n in counts:
            out.append(refs[p:p + n])
            p += n
        return out

    def phase(which):
        def run(ins, outs, sems):
            parts = zip(jobs, split(ins, [len(j.ins) for j in jobs]), split(outs, [len(j.outs) for j in jobs]),
                        split(sems, [len(j.sems) for j in jobs]))
            for j, ji, jo, js in parts:
                fn = getattr(j, which)
                if fn is not None:
                    fn(ji, jo, js)
        return run

    joined = _Job(sum((j.ins for j in jobs), []), sum((j.outs for j in jobs), []), sum((j.sems for j in jobs), []),
                  phase("start"), phase("finish"), phase("mid") if any(j.mid for j in jobs) else None)
    joined.parts = jobs
    joined.mid_frac = max(j.mid_frac for j in jobs if j.mid is not None) if any(j.mid for j in jobs) else 0.75
    i0 = o0 = 0
    for j in jobs:
        joined.alias.update({i0 + i: o0 + o for i, o in j.alias.items()})
        i0 += len(j.ins)
        o0 += len(j.outs)
    return joined


def _job_results(job):
    if job is None:
        return []
    parts = getattr(job, "parts", None)
    if parts is None:
        return [job.results]
    out, p = [], 0
    for j in parts:
        out.append(job.results[p:p + len(j.outs)])
        p += len(j.outs)
    return out


def _pcall(name, body, grid, in_specs, out_specs, out_shape, inputs, scratch=(), aliases=None,
           vmem_mb=VMEM_MB, job=None):
    in_specs, out_specs, out_shape, scratch = list(in_specs), list(out_specs), list(out_shape), list(scratch)
    params = _cparams(("arbitrary",) * len(grid), vmem_mb)
    if job is None:
        return pl.pallas_call(body, name=name, grid=grid, in_specs=in_specs, out_specs=out_specs,
                              out_shape=out_shape, scratch_shapes=scratch,
                              input_output_aliases=aliases or {}, compiler_params=params)(*inputs)
    n_in, n_out, n_scr = len(inputs), len(out_shape), len(scratch)
    ji, jo = len(job.ins), len(job.outs)
    total = 1
    for g in grid:
        total *= g
    mid_step = int(job.mid_frac * total) if total >= 4 and job.mid_frac < 1.0 else None

    def wrapped(*refs):
        p = 0
        ins = refs[p:p + n_in]
        p += n_in
        jins = refs[p:p + ji]
        p += ji
        outs = refs[p:p + n_out]
        p += n_out
        jouts = refs[p:p + jo]
        p += jo
        scr = refs[p:p + n_scr]
        sems = refs[p + n_scr:]
        step = pl.program_id(0)
        for ax in range(1, len(grid)):
            step = step * grid[ax] + pl.program_id(ax)

        @pl.when(step == 0)
        def _():
            job.start(jins, jouts, sems)

        if job.mid is not None and mid_step is not None:
            @pl.when(step == mid_step)
            def _():
                job.mid(jins, jouts, sems)

        body(*ins, *outs, *scr)

        @pl.when(step == total - 1)
        def _():
            if job.mid is not None and mid_step is None:
                job.mid(jins, jouts, sems)
            job.finish(jins, jouts, sems)

    all_aliases = dict(aliases or {})
    all_aliases.update({n_in + i: n_out + o for i, o in job.alias.items()})
    res = pl.pallas_call(
        wrapped, name=name, grid=grid, in_specs=in_specs + [HBM_SPEC] * ji,
        out_specs=out_specs + [HBM_SPEC] * jo, out_shape=out_shape + job.outs,
        scratch_shapes=scratch + job.sems, input_output_aliases=all_aliases, compiler_params=params,
    )(*inputs, *job.ins)
    job.results = list(res[n_out:])
    return list(res[:n_out])


def _run_job(name, job):
    ji, jo = len(job.ins), len(job.outs)

    def body(*refs):
        jins, jouts, sems = refs[:ji], refs[ji:ji + jo], refs[ji + jo:]
        job.start(jins, jouts, sems)
        if job.mid is not None:
            job.mid(jins, jouts, sems)
        job.finish(jins, jouts, sems)

    res = pl.pallas_call(body, name=name, in_specs=[HBM_SPEC] * ji, out_specs=[HBM_SPEC] * jo,
                         out_shape=job.outs, scratch_shapes=job.sems, input_output_aliases=job.alias)(*job.ins)
    job.results = list(res)
    return job.results


def _dot(mode, a, b):
    if mode == "nn":
        dims = (((1,), (0,)), ((), ()))
    elif mode == "nt":
        dims = (((1,), (1,)), ((), ()))
    else:
        dims = (((0,), (0,)), ((), ()))
    return lax.dot_general(a, b, dims, preferred_element_type=F32)


def _sig(x):
    return 0.5 * jnp.tanh(0.5 * x) + 0.5


def _gelu(x):
    t = jnp.tanh(GELU_C * (x + GELU_K * x * x * x))
    return 0.5 * x * (1.0 + t)


def _gelu_grad(x):
    x2 = x * x
    t = jnp.tanh(GELU_C * (x + GELU_K * x * x2))
    return 0.5 * (1.0 + t) + 0.5 * x * (1.0 - t * t) * GELU_C * (1.0 + 3.0 * GELU_K * x2)


def _colsum(v):
    return jnp.sum(v, axis=0, keepdims=True)


def _mm(name, mode, grid, a, a_spec, b, b_spec, extras, out_shapes, out_specs, epi, *,
        order="ij", acc_shape=None, aliases=None, vmem_mb=VMEM_MB, epi_init=None, sub=None, job=None):
    gm, gn, gk = grid

    def spec(s):
        bs, f = s
        if f is None:
            return pl.BlockSpec(memory_space=pl.ANY)
        if order == "ij":
            return pl.BlockSpec(bs, lambda i, j, k, f=f: f(i, j, k))
        return pl.BlockSpec(bs, lambda j, i, k, f=f: f(i, j, k))

    ne, no = len(extras), len(out_shapes)

    def kern(*refs):
        a_ref, b_ref = refs[0], refs[1]
        ex = refs[2:2 + ne]
        outs = refs[2 + ne:2 + ne + no]
        if order == "ij":
            i, j = pl.program_id(0), pl.program_id(1)
        else:
            j, i = pl.program_id(0), pl.program_id(1)
        k = pl.program_id(2)
        prod = _dot(mode, a_ref[...], b_ref[...])
        if gk == 1:
            if epi_init is not None:
                epi_init(i, j, outs)
            epi(prod, i, j, ex, outs)
        elif epi is None:
            @pl.when(k == 0)
            def _():
                outs[0][...] = prod

            @pl.when(k > 0)
            def _():
                outs[0][...] += prod
        else:
            acc = refs[-1]

            @pl.when(k == 0)
            def _():
                acc[...] = prod

            @pl.when(k > 0)
            def _():
                acc[...] += prod

            @pl.when(k == gk - 1)
            def _():
                if epi_init is not None:
                    epi_init(i, j, outs)
                if sub is None:
                    epi(acc[...], i, j, ex, outs)
                else:
                    tm = acc_shape[0]

                    def rows_of(r, rs):
                        return r.at[rs, :] if r.shape[0] == tm else r

                    def blk(t, carry):
                        rs = pl.ds(pl.multiple_of(t * sub, sub), sub)
                        epi(acc[rs, :], i, j, [rows_of(r, rs) for r in ex], [rows_of(r, rs) for r in outs])
                        return carry

                    lax.fori_loop(0, tm // sub, blk, 0)

    g = (gm, gn, gk) if order == "ij" else (gn, gm, gk)
    scratch = [pltpu.VMEM(acc_shape, F32)] if gk > 1 and epi is not None else []
    return _pcall(name, kern, g, [spec(a_spec), spec(b_spec)] + [spec(s) for _, s in extras],
                  [spec(s) for s in out_specs], out_shapes, [a, b] + [e for e, _ in extras],
                  scratch=scratch, aliases=aliases, vmem_mb=vmem_mb, job=job)


def _ext(ref, c, n_chunks, tc, seq):
    c0 = pl.multiple_of(c * tc, tc)
    body = ref[pl.ds(c0, tc), :].astype(F32)
    t0 = pl.multiple_of(jnp.maximum(c0 - HALO, 0), HALO)
    b0 = pl.multiple_of(jnp.minimum(c0 + tc, seq - HALO), HALO)
    top = ref[pl.ds(t0, HALO), :].astype(F32)
    bot = ref[pl.ds(b0, HALO), :].astype(F32)
    top = jnp.where(c > 0, top, 0.0)
    bot = jnp.where(c < n_chunks - 1, bot, 0.0)
    return jnp.concatenate([top, body, bot], axis=0)


def _shifted(vext, off, tc):
    n = vext.shape[0]
    r = vext if off == 0 else pltpu.roll(vext, (n - off) % n, 0)
    return r[HALO:HALO + tc]


def _win_sum(vext, g, extra, tc):
    s2 = vext + pltpu.roll(vext, 1, 0)
    s4 = s2 + pltpu.roll(s2, 2, 0)
    s8 = s4 + pltpu.roll(s4, 4, 0)
    s16 = s8 + pltpu.roll(s8, 8, 0)
    outs = [_shifted(s, extra + hw - 1, tc) for s, hw in ((s2, 1), (s4, 2), (s8, 4), (s16, 8))]
    return jnp.where(g == 0, outs[0], jnp.where(g == 1, outs[1], jnp.where(g == 2, outs[2], outs[3])))


def _win_cnt(t, hw, seq):
    return (jnp.minimum(t + hw, seq) - jnp.maximum(t - hw, 0)).astype(F32)


def _pool_d(uext, g, c, tc, seq):
    hw = jnp.left_shift(1, g)
    t = c * tc + lax.broadcasted_iota(jnp.int32, (tc, 1), 0)
    ws = _win_sum(uext, g, 0, tc)
    return ws / _win_cnt(t, hw, seq) - uext[HALO:HALO + tc]


def _scan_tiles(a_ref, b_ref, h_ref, carry_ref, n_tiles, reverse):
    blk = a_ref.shape[1]
    row = lax.broadcasted_iota(jnp.int32, (SUBLANES, blk), 0)

    def tile(j, hc):
        jj = (n_tiles - 1 - j) if reverse else j
        off = pl.multiple_of(jj * SUBLANES, SUBLANES)
        a = a_ref[pl.ds(off, SUBLANES), :]
        b = b_ref[pl.ds(off, SUBLANES), :]
        for kk in (1, 2, 4):
            sh = (SUBLANES - kk) if reverse else kk
            a_s = pltpu.roll(a, sh, 0)
            b_s = pltpu.roll(b, sh, 0)
            m = (row < SUBLANES - kk) if reverse else (row >= kk)
            a_s = jnp.where(m, a_s, 1.0)
            b_s = jnp.where(m, b_s, 0.0)
            b = a * b_s + b
            a = a * a_s
        h = a * hc + b
        h_ref[pl.ds(off, SUBLANES), :] = h
        return h[0:1, :] if reverse else h[SUBLANES - 1:SUBLANES, :]

    group = SUBLANES if n_tiles % SUBLANES == 0 else 1

    def tiles(jg, hc):
        for u in range(group):
            hc = tile(jg * group + u, hc)
        return hc

    hc = lax.fori_loop(0, n_tiles // group, tiles, carry_ref[0:1, :])
    carry_ref[0:1, :] = hc


def _lru_k(lam):
    y = -lam
    e = jnp.exp(-jnp.abs(y))
    u = 1.0 + e
    l1p = jnp.where(u == 1.0, e, jnp.log(u) * (e / (u - 1.0)))
    return -LRU_C * (jnp.maximum(y, 0.0) + l1p)


def _lru_gates(xc, wa, wx, ba, bx, lam):
    xb = xc.astype(BF16)
    r = _sig(jnp.dot(xb, wa, preferred_element_type=F32) + ba)
    i = _sig(jnp.dot(xb, wx, preferred_element_type=F32) + bx)
    k = _lru_k(lam)
    la = k * r
    a = jnp.exp(la)
    s = jnp.sqrt(-jnp.tanh(la) * (a * a + 1.0))
    return r, i, k, a, s


SV_CONV, SV_BA, SV_BX, SV_LAM = 0, 4, 6, 8


def _pool_fwd(z, pw, scale, seq, d, tc, job=None):
    n_g = pw.shape[0]
    pg = d // n_g
    n_chunks = seq // tc

    def kern(z_ref, pw_ref, sc_ref, y_ref):
        g, c = pl.program_id(0), pl.program_id(1)
        uext = _ext(z_ref, c, n_chunks, tc, seq)
        dd = _pool_d(uext, g, c, tc, seq)
        q = jnp.dot(dd.astype(BF16), pw_ref[...], preferred_element_type=F32)
        y_ref[...] = (q * sc_ref[...]).astype(BF16)

    (y,) = _pcall(
        "pool_fwd", kern, (n_g, n_chunks),
        [pl.BlockSpec((seq, pg), lambda g, c: (0, g)),
         pl.BlockSpec((None, pg, pg), lambda g, c: (g, 0, 0)),
         pl.BlockSpec((1, pg), lambda g, c: (0, g))],
        [pl.BlockSpec((tc, pg), lambda g, c: (c, g))],
        [jax.ShapeDtypeStruct((seq, d), BF16)], [z, pw, scale], job=job)
    return y


def _lru_fwd(z, sv, conv_b, wa, wx, seq, d, tc, job=None):
    n_h, blk = wa.shape[1], wa.shape[2]
    n_chunks = seq // tc
    lru_off = d // blk

    def kern(z_ref, sv_ref, cb_ref, wa_ref, wx_ref, xc_ref, h_ref, a_s, b_s, carry):
        c = pl.program_id(1)
        uext = _ext(z_ref, c, n_chunks, tc, seq)
        xc = cb_ref[...]
        for k in range(4):
            xc = xc + _shifted(uext, k - 2, tc) * sv_ref[SV_CONV + k:SV_CONV + k + 1, :]
        xc_ref[...] = xc
        _, i, _, a, s = _lru_gates(xc, wa_ref[...], wx_ref[...], sv_ref[SV_BA:SV_BA + 1, :],
                                   sv_ref[SV_BX:SV_BX + 1, :], sv_ref[SV_LAM:SV_LAM + 1, :])
        a_s[...] = a
        b_s[...] = s * (i * xc)

        @pl.when(c == 0)
        def _():
            carry[...] = jnp.zeros_like(carry)

        _scan_tiles(a_s, b_s, h_ref, carry, tc // SUBLANES, False)

    col = lambda h, c: (c, h)
    return _pcall(
        "lru_fwd", kern, (n_h, n_chunks),
        [pl.BlockSpec((seq, blk), lambda h, c: (0, lru_off + h)),
         pl.BlockSpec((16, blk), lambda h, c: (0, h)),
         pl.BlockSpec((1, blk), lambda h, c: (0, h)),
         pl.BlockSpec((None, None, blk, blk), lambda h, c: (0, h, 0, 0)),
         pl.BlockSpec((None, None, blk, blk), lambda h, c: (0, h, 0, 0))],
        [pl.BlockSpec((tc, blk), col), pl.BlockSpec((tc, blk), col)],
        [jax.ShapeDtypeStruct((seq, d), F32), jax.ShapeDtypeStruct((seq, d), F32)],
        [z, sv, conv_b, wa, wx],
        scratch=[pltpu.VMEM((tc, blk), F32), pltpu.VMEM((tc, blk), F32), pltpu.VMEM((SUBLANES, blk), F32)],
        job=job)


def _lru_rev(z, sv, wa, wx, xc, h_f, seq, d, tc, job=None):
    n_h, blk = wa.shape[1], wa.shape[2]
    n_chunks = seq // tc
    gate_off = 2 * d // blk

    def kern(z_ref, sv_ref, wa_ref, wx_ref, xc_ref, hf_ref, hb_ref, y_ref, a_s, b_s, carry):
        c = pl.program_id(1)
        xc = xc_ref[...]
        _, i, _, a, s = _lru_gates(xc, wa_ref[...], wx_ref[...], sv_ref[SV_BA + 1:SV_BA + 2, :],
                                   sv_ref[SV_BX + 1:SV_BX + 2, :], sv_ref[SV_LAM + 1:SV_LAM + 2, :])
        a_s[...] = a
        b_s[...] = s * (i * xc)

        @pl.when(c == 0)
        def _():
            carry[...] = jnp.zeros_like(carry)

        _scan_tiles(a_s, b_s, hb_ref, carry, tc // SUBLANES, True)
        y_ref[...] = ((hf_ref[...] + hb_ref[...]) * _gelu(z_ref[...])).astype(BF16)

    rev = lambda h, c: (n_chunks - 1 - c, h)
    return _pcall(
        "lru_rev", kern, (n_h, n_chunks),
        [pl.BlockSpec((tc, blk), lambda h, c: (n_chunks - 1 - c, gate_off + h)),
         pl.BlockSpec((16, blk), lambda h, c: (0, h)),
         pl.BlockSpec((None, None, blk, blk), lambda h, c: (1, h, 0, 0)),
         pl.BlockSpec((None, None, blk, blk), lambda h, c: (1, h, 0, 0)),
         pl.BlockSpec((tc, blk), rev), pl.BlockSpec((tc, blk), rev)],
        [pl.BlockSpec((tc, blk), rev), pl.BlockSpec((tc, blk), rev)],
        [jax.ShapeDtypeStruct((seq, d), F32), jax.ShapeDtypeStruct((seq, d), BF16)],
        [z, sv, wa, wx, xc, h_f],
        scratch=[pltpu.VMEM((tc, blk), F32), pltpu.VMEM((tc, blk), F32), pltpu.VMEM((SUBLANES, blk), F32)],
        job=job)


def _merge(y_pool, w_pu, y_lru, w_lu, z, seq, d, tm, tn, job=None):
    n_n = d // tn

    def kern(yp_ref, wp_ref, yl_ref, wl_ref, la_ref, lb_ref, m_ref, pa_ref, pb_ref):
        pa = jnp.dot(yp_ref[...], wp_ref[...], preferred_element_type=F32)
        pb = jnp.dot(yl_ref[...], wl_ref[...], preferred_element_type=F32)
        m_ref[...] = (_sig(la_ref[...]) * pa + _sig(lb_ref[...]) * pb).astype(BF16)
        pa_ref[...] = pa.astype(BF16)
        pb_ref[...] = pb.astype(BF16)

    row = pl.BlockSpec((tm, d), lambda i, j: (i, 0))
    wcol = pl.BlockSpec((d, tn), lambda i, j: (0, j))
    out = pl.BlockSpec((tm, tn), lambda i, j: (i, j))
    sh = jax.ShapeDtypeStruct((seq, d), BF16)
    return _pcall(
        "merge", kern, (seq // tm, n_n),
        [row, wcol, row, wcol,
         pl.BlockSpec((tm, tn), lambda i, j: (i, 3 * n_n + j)),
         pl.BlockSpec((tm, tn), lambda i, j: (i, 4 * n_n + j))],
        [out, out, out], [sh, sh, sh], [y_pool, w_pu, y_lru, w_lu, z, z], job=job)


def _ln_fwd(s, g, b):
    mu = jnp.mean(s, axis=-1, keepdims=True)
    xc = s - mu
    var = jnp.mean(xc * xc, axis=-1, keepdims=True)
    rstd = lax.rsqrt(var + LN_EPS)
    xhat = xc * rstd
    return xhat, rstd, xhat * g + b


def _ln_bwd(dy, xhat, rstd, g):
    dyg = dy * g
    m1 = jnp.mean(dyg, axis=-1, keepdims=True)
    m2 = jnp.mean(dyg * xhat, axis=-1, keepdims=True)
    return rstd * (dyg - m1 - xhat * m2)


def _pool_bwd(z, dy_pool, pw, scale, dz, seq, d, tc):
    n_g = pw.shape[0]
    pg = d // n_g
    n_chunks = seq // tc

    def kern(z_ref, dy_ref, pw_ref, sc_ref, dz_in, dz_ref, dpw_ref, dsc_ref):
        del dz_in
        g, c = pl.program_id(0), pl.program_id(1)
        hw = jnp.left_shift(1, g)
        uext = _ext(z_ref, c, n_chunks, tc, seq)
        dd = _pool_d(uext, g, c, tc, seq).astype(BF16)
        pwv = pw_ref[...]
        q = jnp.dot(dd, pwv, preferred_element_type=F32)
        dyext = _ext(dy_ref, c, n_chunks, tc, seq)

        @pl.when(c == 0)
        def _():
            dsc_ref[...] = jnp.zeros_like(dsc_ref)
            dpw_ref[...] = jnp.zeros_like(dpw_ref)

        dsc_ref[0:1, :] += _colsum(dyext[HALO:HALO + tc] * q)
        dqext = (dyext * sc_ref[...]).astype(BF16)
        dpw_ref[...] += _dot("tn", dd, dqext[HALO:HALO + tc])
        ddext = _dot("nt", dqext, pwv)
        text = c * tc - HALO + lax.broadcasted_iota(jnp.int32, (tc + 2 * HALO, 1), 0)
        v = ddext / jnp.maximum(_win_cnt(text, hw, seq), 1.0)
        dz_ref[...] = (_win_sum(v, g, 1, tc) - ddext[HALO:HALO + tc]).astype(BF16)

    return pl.pallas_call(
        kern, name="pool_bwd", grid=(n_g, n_chunks),
        in_specs=[pl.BlockSpec((seq, pg), lambda g, c: (0, g)),
                  pl.BlockSpec((seq, pg), lambda g, c: (0, g)),
                  pl.BlockSpec((None, pg, pg), lambda g, c: (g, 0, 0)),
                  pl.BlockSpec((1, pg), lambda g, c: (0, g)),
                  pl.BlockSpec(memory_space=pl.ANY)],
        out_specs=[pl.BlockSpec((None, tc, pg), lambda g, c: (2, c, g)),
                   pl.BlockSpec((None, pg, pg), lambda g, c: (g, 0, 0)),
                   pl.BlockSpec((SUBLANES, pg), lambda g, c: (0, g))],
        out_shape=[jax.ShapeDtypeStruct(dz.shape, dz.dtype),
                   jax.ShapeDtypeStruct((n_g, pg, pg), F32),
                   jax.ShapeDtypeStruct((SUBLANES, d), F32)],
        input_output_aliases={4: 0},
        compiler_params=_cparams(("arbitrary", "arbitrary")),
    )(z, dy_pool, pw, scale, dz)


def _lru_bwd(direction, xc, dh, h_dir, sv, wa, wx, dxc_prev, seq, d, tc, job=None):
    reverse = direction == 1
    n_h, blk = wa.shape[1], wa.shape[2]
    n_chunks = seq // tc
    has_prev = dxc_prev is not None

    def kern(*refs):
        xc_ref, dh_ref, h_ref, sv_ref, wa_ref, wx_ref = refs[:6]
        p = 6
        prev_ref = None
        if has_prev:
            prev_ref = refs[p]
            p += 1
        dxc_ref, dwa_ref, dwx_ref, sm_ref, at_s, g_s, carry, acarry = refs[p:p + 8]
        c = pl.program_id(1)
        cr = c if reverse else n_chunks - 1 - c
        c0 = pl.multiple_of(cr * tc, tc)

        @pl.when(c == 0)
        def _():
            carry[...] = jnp.zeros_like(carry)
            acarry[...] = jnp.zeros_like(acarry)
            dwa_ref[...] = jnp.zeros_like(dwa_ref)
            dwx_ref[...] = jnp.zeros_like(dwx_ref)
            sm_ref[...] = jnp.zeros_like(sm_ref)

        xc = xc_ref[...]
        wav, wxv = wa_ref[...], wx_ref[...]
        lam = sv_ref[SV_LAM + direction:SV_LAM + direction + 1, :]
        r, i, k, a, s = _lru_gates(xc, wav, wxv, sv_ref[SV_BA + direction:SV_BA + direction + 1, :],
                                   sv_ref[SV_BX + direction:SV_BX + direction + 1, :], lam)
        rowi = lax.broadcasted_iota(jnp.int32, (tc, blk), 0)
        hbody = h_ref[pl.ds(c0, tc), :]
        if not reverse:
            p0 = pl.multiple_of(jnp.maximum(c0 - SUBLANES, 0), SUBLANES)
            edge = jnp.where(cr > 0, h_ref[pl.ds(p0, SUBLANES), :][SUBLANES - 1:SUBLANES, :], 0.0)
            hprev = jnp.where(rowi == 0, edge, pltpu.roll(hbody, 1, 0))
            at = jnp.where(rowi == tc - 1, acarry[0:1, :], pltpu.roll(a, tc - 1, 0))
        else:
            n0 = pl.multiple_of(jnp.minimum(c0 + tc, seq - SUBLANES), SUBLANES)
            edge = jnp.where(cr < n_chunks - 1, h_ref[pl.ds(n0, SUBLANES), :][0:1, :], 0.0)
            hprev = jnp.where(rowi == tc - 1, edge, pltpu.roll(hbody, tc - 1, 0))
            at = jnp.where(rowi == 0, acarry[0:1, :], pltpu.roll(a, 1, 0))
        at_s[...] = at
        _scan_tiles(at_s, dh_ref, g_s, carry, tc // SUBLANES, not reverse)
        acarry[0:1, :] = a[tc - 1:tc, :] if reverse else a[0:1, :]

        gt = g_s[...]
        da = gt * hprev
        di = gt * s * xc
        dxc = gt * s * i
        ds = gt * (i * xc)
        dl = da * a - ds * (a * a) / s
        dpr = (dl * k) * r * (1.0 - r)
        dpi = di * i * (1.0 - i)
        sm_ref[0:1, :] += _colsum(dpr)
        sm_ref[1:2, :] += _colsum(dpi)
        sm_ref[2:3, :] += _colsum(dl * r) * (LRU_C * _sig(-lam))
        xb, dprb, dpib = xc.astype(BF16), dpr.astype(BF16), dpi.astype(BF16)
        dwa_ref[...] += _dot("tn", xb, dprb)
        dwx_ref[...] += _dot("tn", xb, dpib)
        dxc = dxc + _dot("nt", dprb, wav) + _dot("nt", dpib, wxv)
        if has_prev:
            dxc = dxc + prev_ref[...]
        dxc_ref[...] = dxc

    if reverse:
        chunk = lambda h, c: (c, h)
    else:
        chunk = lambda h, c: (n_chunks - 1 - c, h)
    wspec = pl.BlockSpec((None, None, blk, blk), lambda h, c: (direction, h, 0, 0))
    ins = [xc, dh, h_dir, sv, wa, wx] + ([dxc_prev] if has_prev else [])
    in_specs = [pl.BlockSpec((tc, blk), chunk), pl.BlockSpec((tc, blk), chunk),
                pl.BlockSpec((seq, blk), lambda h, c: (0, h)),
                pl.BlockSpec((16, blk), lambda h, c: (0, h)), wspec, wspec]
    if has_prev:
        in_specs.append(pl.BlockSpec((tc, blk), chunk))
    return _pcall(
        "lru_bwd_%d" % direction, kern, (n_h, n_chunks), in_specs,
        [pl.BlockSpec((tc, blk), chunk),
         pl.BlockSpec((None, blk, blk), lambda h, c: (h, 0, 0)),
         pl.BlockSpec((None, blk, blk), lambda h, c: (h, 0, 0)),
         pl.BlockSpec((SUBLANES, blk), lambda h, c: (0, h))],
        [jax.ShapeDtypeStruct((seq, d), F32),
         jax.ShapeDtypeStruct((n_h, blk, blk), F32),
         jax.ShapeDtypeStruct((n_h, blk, blk), F32),
         jax.ShapeDtypeStruct((SUBLANES, d), F32)],
        ins,
        scratch=[pltpu.VMEM((tc, blk), F32), pltpu.VMEM((tc, blk), F32),
                 pltpu.VMEM((SUBLANES, blk), F32), pltpu.VMEM((SUBLANES, blk), F32)],
        job=job)


def _conv_bwd(z, dxc, sv, dz, seq, d, tc, tcol):
    n_chunks = seq // tc
    lru_off = d // tcol

    def kern(z_ref, dx_ref, sv_ref, dz_in, dz_ref, sm_ref):
        del dz_in
        c = pl.program_id(1)
        uext = _ext(z_ref, c, n_chunks, tc, seq)
        dext = _ext(dx_ref, c, n_chunks, tc, seq)
        dbody = dext[HALO:HALO + tc]

        @pl.when(c == 0)
        def _():
            sm_ref[...] = jnp.zeros_like(sm_ref)

        du = jnp.zeros_like(dbody)
        for k in range(4):
            du = du + _shifted(dext, 2 - k, tc) * sv_ref[SV_CONV + k:SV_CONV + k + 1, :]
            sm_ref[k:k + 1, :] += _colsum(dbody * _shifted(uext, k - 2, tc))
        sm_ref[4:5, :] += _colsum(dbody)
        dz_ref[...] = du.astype(BF16)

    return pl.pallas_call(
        kern, name="conv_bwd", grid=(d // tcol, n_chunks),
        in_specs=[pl.BlockSpec((seq, tcol), lambda j, c: (0, lru_off + j)),
                  pl.BlockSpec((seq, tcol), lambda j, c: (0, j)),
                  pl.BlockSpec((16, tcol), lambda j, c: (0, j)),
                  pl.BlockSpec(memory_space=pl.ANY)],
        out_specs=[pl.BlockSpec((None, tc, tcol), lambda j, c: (3, c, j)),
                   pl.BlockSpec((SUBLANES, tcol), lambda j, c: (0, j))],
        out_shape=[jax.ShapeDtypeStruct(dz.shape, dz.dtype), jax.ShapeDtypeStruct((SUBLANES, d), F32)],
        input_output_aliases={3: 0},
        compiler_params=_cparams(("arbitrary", "arbitrary")),
    )(z, dxc, sv, dz)


BIG = (("w_in", "col"), ("pool_w", "row"), ("lru_wa", "row"), ("lru_wx", "row"), ("w_pool_up", "row"),
       ("w_lru_up", "row"), ("w_out", "row"), ("w_ff1", "col"), ("w_ff2", "row"))


def _shard_view(w, fam):
    if fam == "col":
        return w.reshape(w.shape[-2:])
    return w.reshape((-1,) + w.shape[-2:])


def _full_shape(sv_shape, fam):
    if fam == "col":
        return (sv_shape[0], N_DEV * sv_shape[1])
    return (sv_shape[0], N_DEV * sv_shape[1], sv_shape[2])


def _slot(ref, fam, p, size, part=None):
    if fam == "lead":
        return ref.at[p]
    k, n, span = part or (0, 1, 1)
    unit = size // n
    start = pl.multiple_of(p * size + k * unit, unit)
    if fam == "col":
        return ref.at[:, pl.ds(start, span * unit)]
    return ref.at[:, pl.ds(start, span * unit), :]


def _shard_part(ref, fam, size, part):
    if part is None or fam == "lead":
        return ref
    k, n, span = part
    unit = size // n
    if fam == "col":
        return ref.at[:, pl.ds(k * unit, span * unit)]
    return ref.at[:, pl.ds(k * unit, span * unit), :]


def _shard_extent(shape, fam):
    return shape[1]


def _coords():
    return lax.axis_index("x"), lax.axis_index("y"), lax.axis_index("c")


def _ag_job(shards, fams, parts=None, into=None):
    n = len(shards)
    parts = list(parts) if parts is not None else [None] * n
    into = list(into) if into is not None else [None] * n
    fulls = []
    for s, fam in zip(shards, fams):
        if fam == "lead":
            fulls.append(jax.ShapeDtypeStruct((N_DEV,) + s.shape, s.dtype))
        else:
            fulls.append(jax.ShapeDtypeStruct(_full_shape(s.shape, fam), s.dtype))
    sizes = [1 if fam == "lead" else _shard_extent(s.shape, fam) for s, fam in zip(shards, fams)]
    given = [a for a in range(n) if into[a] is not None]

    def ctx(ins, outs, sems):
        send, recv, loc = sems
        x, y, c = _coords()
        chips = [(1 - x, y), (x, 1 - y), (1 - x, 1 - y)]

        def mine(a):
            return _shard_part(ins[a], fams[a], sizes[a], parts[a])

        def copy(a, k, owner, to, src=None):
            dst = _slot(outs[a], fams[a], owner, sizes[a], parts[a])
            return pltpu.make_async_remote_copy(
                src_ref=dst if src is None else src, dst_ref=dst,
                send_sem=send.at[a, k], recv_sem=recv.at[a, k], device_id=to, device_id_type=MESH)

        def local(a):
            return pltpu.make_async_copy(
                mine(a), _slot(outs[a], fams[a], 4 * x + 2 * y + c, sizes[a], parts[a]), loc.at[a])

        return x, y, c, chips, copy, local, mine

    def start(ins, outs, sems):
        x, y, c, chips, copy, local, mine = ctx(ins, outs, sems)
        me = 4 * x + 2 * y + c
        for a in range(n):
            local(a).start()
            copy(a, 0, me, (x, y, 1 - c), mine(a)).start()
            for j, (cx, cy) in enumerate(chips):
                copy(a, 1 + j, me, (cx, cy, c), mine(a)).start()

    def mid(ins, outs, sems):
        x, y, c, chips, copy, _, _ = ctx(ins, outs, sems)
        for a in range(n):
            for j, (cx, cy) in enumerate(chips):
                owner = 4 * cx + 2 * cy + c
                copy(a, 1 + j, owner, (x, y, c)).wait_recv()
                copy(a, 4 + j, owner, (x, y, 1 - c)).start()

    def finish(ins, outs, sems):
        x, y, c, chips, copy, local, mine = ctx(ins, outs, sems)
        me = 4 * x + 2 * y + c
        for a in range(n):
            copy(a, 0, 4 * x + 2 * y + (1 - c), (x, y, c)).wait_recv()
            for j, (cx, cy) in enumerate(chips):
                copy(a, 4 + j, 4 * cx + 2 * cy + (1 - c), (x, y, c)).wait_recv()
            copy(a, 0, me, (x, y, 1 - c), mine(a)).wait_send()
            for j, (cx, cy) in enumerate(chips):
                copy(a, 1 + j, me, (cx, cy, c), mine(a)).wait_send()
                copy(a, 4 + j, 4 * cx + 2 * cy + c, (x, y, 1 - c)).wait_send()
            local(a).wait()

    sems = [pltpu.SemaphoreType.DMA((n, 7)), pltpu.SemaphoreType.DMA((n, 7)), pltpu.SemaphoreType.DMA((n,))]
    job = _Job(list(shards) + [into[a] for a in given], fulls, sems, start, finish, mid)
    job.alias = {n + i: a for i, a in enumerate(given)}
    return job


def _rs_sibling_job(fulls, fams, sizes):
    n = len(fulls)
    outs = []
    for f, fam, sz in zip(fulls, fams, sizes):
        if fam == "col":
            outs.append(jax.ShapeDtypeStruct((4, f.shape[0], sz), f.dtype))
        else:
            outs.append(jax.ShapeDtypeStruct((4, f.shape[0], sz, f.shape[2]), f.dtype))

    def copies(ins, rcv, sems):
        send, recv = sems
        x, y, c = _coords()
        return [pltpu.make_async_remote_copy(
            src_ref=_slot(ins[a], fams[a], 2 * q + (1 - c), sizes[a]), dst_ref=rcv[a].at[q],
            send_sem=send.at[a, q], recv_sem=recv.at[a, q], device_id=(x, y, 1 - c), device_id_type=MESH)
            for a in range(n) for q in range(4)]

    def start(ins, rcv, sems):
        for cp in copies(ins, rcv, sems):
            cp.start()

    def finish(ins, rcv, sems):
        for cp in copies(ins, rcv, sems):
            cp.wait()

    return _Job(fulls, outs, [pltpu.SemaphoreType.DMA((n, 4)), pltpu.SemaphoreType.DMA((n, 4))], start, finish)


def _rs_chips_job(parts):
    n = len(parts)
    outs = [jax.ShapeDtypeStruct((3,) + p.shape[1:], p.dtype) for p in parts]

    def copies(ins, rcv, sems):
        send, recv = sems
        x, y, c = _coords()
        cps = []
        for a in range(n):
            for r in (1, 2, 3):
                tx, ty = (1 - x) if r & 2 else x, (1 - y) if r & 1 else y
                cps.append(pltpu.make_async_remote_copy(
                    src_ref=ins[a].at[2 * tx + ty], dst_ref=rcv[a].at[r - 1],
                    send_sem=send.at[a, r - 1], recv_sem=recv.at[a, r - 1],
                    device_id=(tx, ty, c), device_id_type=MESH))
        return cps

    def start(ins, rcv, sems):
        for cp in copies(ins, rcv, sems):
            cp.start()

    def finish(ins, rcv, sems):
        for cp in copies(ins, rcv, sems):
            cp.wait()

    return _Job(parts, outs, [pltpu.SemaphoreType.DMA((n, 3)), pltpu.SemaphoreType.DMA((n, 3))], start, finish)


def _tile_rows(rows, cols):
    tr = rows
    while tr * cols > (1 << 18) and tr % (2 * SUBLANES) == 0:
        tr //= 2
    return tr


def _rs_add(name, full, recv_a, fam, size, cidx):
    if fam == "col":
        rows = full.shape[0]
        tr = _tile_rows(rows, size)
        grid = (4, rows // tr)
        f_spec = pl.BlockSpec((tr, size), lambda q, i, cr: (i, 2 * q + cr[0]))
        s_spec = pl.BlockSpec((None, tr, size), lambda q, i, cr: (q, i, 0))
    else:
        nb, cols = full.shape[0], full.shape[2]
        tr = _tile_rows(nb * size, cols) // nb if nb > 1 else _tile_rows(size, cols)
        nt = size // tr
        grid = (4, nt)
        f_spec = pl.BlockSpec((nb, tr, cols), lambda q, i, cr: (0, (2 * q + cr[0]) * nt + i, 0))
        s_spec = pl.BlockSpec((None, nb, tr, cols), lambda q, i, cr: (q, 0, i, 0))

    def kern(c_ref, f_ref, r_ref, o_ref):
        del c_ref
        o_ref[...] = (f_ref[...] + r_ref[...]).astype(BF16)

    return pl.pallas_call(
        kern, name=name,
        grid_spec=pltpu.PrefetchScalarGridSpec(num_scalar_prefetch=1, grid=grid, in_specs=[f_spec, s_spec],
                                               out_specs=s_spec),
        out_shape=jax.ShapeDtypeStruct(recv_a.shape, BF16),
        compiler_params=_cparams(("arbitrary", "arbitrary"), 32),
    )(cidx, full, recv_a)


def _adam(w, g, m, v):
    m2 = ADAM_B1 * m + (1.0 - ADAM_B1) * g
    v2 = ADAM_B2 * v + (1.0 - ADAM_B2) * (g * g)
    m_hat = m2 / (1.0 - ADAM_B1 ** ADAM_STEP)
    v_hat = v2 / (1.0 - ADAM_B2 ** ADAM_STEP)
    delta = -ADAM_LR * (m_hat / (jnp.sqrt(v_hat) + ADAM_EPS) + ADAM_WD * w)
    return delta, m2, v2


def _rs_final_adam(name, parts, recv_b, w, m, v, fam, qidx):
    shp = w.shape
    pieces = parts if isinstance(parts, (list, tuple)) else [parts]
    recvs = recv_b if isinstance(recv_b, (list, tuple)) else [recv_b]
    n_p = len(pieces)
    first_blk = [0] * n_p
    if fam == "col":
        rows, cols = shp
        tr = _tile_rows(min(p.shape[1] for p in pieces), cols)
        per = [p.shape[1] // tr for p in pieces]
        for h in range(1, n_p):
            first_blk[h] = first_blk[h - 1] + per[h - 1]
        grid = (rows // tr,)
        w_spec = pl.BlockSpec((tr, cols), lambda i, qr: (i, 0))

        def piece_row(i, h):
            return jnp.clip(i - first_blk[h], 0, per[h] - 1)

        p_specs = [pl.BlockSpec((None, tr, cols), lambda i, qr, h=h: (qr[0], piece_row(i, h), 0))
                   for h in range(n_p)]
        r_specs = [pl.BlockSpec((3, tr, cols), lambda i, qr, h=h: (0, piece_row(i, h), 0)) for h in range(n_p)]
    else:
        assert n_p == 1
        nb, rows, cols = shp
        tr = _tile_rows(rows, cols)
        nt = rows // tr
        grid = (nb * nt,)
        w_spec = pl.BlockSpec((None, tr, cols), lambda i, qr: (i // nt, i % nt, 0))
        p_specs = [pl.BlockSpec((None, None, tr, cols), lambda i, qr: (qr[0], i // nt, i % nt, 0))]
        r_specs = [pl.BlockSpec((3, None, tr, cols), lambda i, qr: (0, i // nt, i % nt, 0))]

    def kern(*refs):
        p_refs, r_refs = refs[1:1 + n_p], refs[1 + n_p:1 + 2 * n_p]
        w_ref, m_ref, v_ref, g_out, d_out, m_out, v_out = refs[1 + 2 * n_p:]

        def total(h):
            p_ref, r_ref = p_refs[h], r_refs[h]
            return ((p_ref[...].astype(F32) + r_ref[0].astype(F32)) + r_ref[1].astype(F32)) + r_ref[2].astype(F32)

        g = total(0)
        for h in range(1, n_p):
            g = jnp.where(pl.program_id(0) >= first_blk[h], total(h), g)
        delta, m2, v2 = _adam(w_ref[...], g, m_ref[...], v_ref[...])
        g_out[...] = g
        d_out[...] = delta
        m_out[...] = m2
        v_out[...] = v2

    sh = jax.ShapeDtypeStruct(shp, F32)
    return pl.pallas_call(
        kern, name=name,
        grid_spec=pltpu.PrefetchScalarGridSpec(
            num_scalar_prefetch=1, grid=grid, in_specs=p_specs + r_specs + [w_spec, w_spec, w_spec],
            out_specs=[w_spec] * 4),
        out_shape=[sh] * 4,
        compiler_params=_cparams(("arbitrary",), 32),
    )(qidx, *pieces, *recvs, w, m, v)


def _sum8(name, parts):
    def kern(p_ref, o_ref):
        acc = p_ref[0]
        for p in range(1, N_DEV):
            acc = acc + p_ref[p]
        o_ref[...] = acc

    return pl.pallas_call(
        kern, name=name, out_shape=jax.ShapeDtypeStruct(parts.shape[1:], F32),
        compiler_params=pltpu.CompilerParams(vmem_limit_bytes=32 << 20),
    )(parts)


def _adam_small(name, w, g, m, v):
    def kern(w_ref, g_ref, m_ref, v_ref, d_out, m_out, v_out):
        delta, m2, v2 = _adam(w_ref[...], g_ref[...], m_ref[...], v_ref[...])
        d_out[...] = delta
        m_out[...] = m2
        v_out[...] = v2

    sh = jax.ShapeDtypeStruct(w.shape, F32)
    return pl.pallas_call(kern, name=name, out_shape=[sh] * 3)(w, g, m, v)


class _NoComm:
    def __init__(self):
        self.grads = {}

    def job(self, host):
        return None

    def after(self, host, job):
        pass

    def grad(self, name, g):
        self.grads[name] = g


def _local_step(x, target, wts, vec, comm=None):
    comm = comm or _NoComm()
    seq, d = x.shape
    sv = vec["sv"]
    ff = vec["b_ff1"].shape[1]
    n_in = 5 * d

    def hosted(host, call):
        job = comm.job(host)
        res = call(job)
        comm.after(host, job)
        return res

    tc = min(512, seq)
    t1k, t512, t256 = min(1024, seq), min(512, seq), min(256, seq)
    n512 = min(512, d)
    tkd = d
    tkf = min(1024, ff)
    sub_rows = min(64, seq)

    x_bf = x.astype(BF16)
    full = lambda i, j, k: (0, 0)

    def epi_store(acc, i, j, ex, outs):
        outs[0][...] = acc

    n_pass = W_IN_PASSES
    piece = n_in // (N_DEV * n_pass)
    tz = min(2048, seq)
    z = None
    for k in range(n_pass):
        w_piece = wts["w_in_piece_%d" % k]
        prev = [] if z is None else [(z, (None, None))]
        (z,) = hosted("z_proj_%d" % k, lambda job: _mm(
            "z_proj_%d" % k, "nn", (seq // tz, N_DEV, 1),
            x_bf, ((tz, d), lambda i, j, kk: (i, 0)), w_piece, ((d, piece), lambda i, j, kk: (0, j)),
            prev, [jax.ShapeDtypeStruct((seq, n_in), F32)],
            [((tz, piece), lambda i, j, kk, k=k: (i, n_pass * j + k))], epi_store,
            aliases={2: 0} if prev else None, job=job))

    pool_w, wa, wx = wts["pool_w"], wts["lru_wa"], wts["lru_wx"]
    blk = wa.shape[2]
    y_pool = hosted("pool_fwd", lambda job: _pool_fwd(z, pool_w, vec["pool_scale"], seq, d, tc, job=job))
    xc, h_f = hosted("lru_fwd", lambda job: _lru_fwd(z, sv, vec["conv_b"], wa, wx, seq, d, tc, job=job))
    h_b, y_lru = hosted("lru_rev", lambda job: _lru_rev(z, sv, wa, wx, xc, h_f, seq, d, tc, job=job))
    w_pu, w_lu = wts["w_pool_up"], wts["w_lru_up"]
    m_bf, p_a, p_b = hosted("merge", lambda job: _merge(y_pool, w_pu, y_lru, w_lu, z, seq, d, t512, n512, job=job))
    w_out = wts["w_out"]

    def epi_ln1(acc, i, j, ex, outs):
        x_ref, bo, g1, b1 = ex
        s1 = DN_ALPHA * x_ref[...] + (acc + bo[...])
        xhat, rstd, x1 = _ln_fwd(s1, g1[...], b1[...])
        outs[0][...] = xhat
        outs[1][...] = x1.astype(BF16)
        outs[2][...] = rstd

    rowd = lambda t: ((t, d), lambda i, j, k: (i, 0))
    vecd = ((1, d), full)
    xhat1, x1_bf, rstd1 = hosted("out_ln1", lambda job: _mm(
        "out_ln1", "nn", (seq // t256, 1, 1), m_bf, rowd(t256), w_out, ((d, d), full),
        [(x, rowd(t256)), (vec["b_out"], vecd), (vec["ln1_g"], vecd), (vec["ln1_b"], vecd)],
        [jax.ShapeDtypeStruct((seq, d), F32), jax.ShapeDtypeStruct((seq, d), BF16),
         jax.ShapeDtypeStruct((seq, 1), F32)],
        [rowd(t256), rowd(t256), ((t256, 1), lambda i, j, k: (i, 0))], epi_ln1, job=job))
    w1 = wts["w_ff1"]

    def epi_ff1(acc, i, j, ex, outs):
        r = jnp.maximum(acc + ex[0][...], 0.0)
        outs[0][...] = r.astype(BF16)
        outs[1][...] = (r * r).astype(BF16)

    tile_f = ((t1k, n512), lambda i, j, k: (i, j))
    relu_h, hdn = hosted("ff1", lambda job: _mm(
        "ff1", "nn", (seq // t1k, ff // n512, 1), x1_bf, rowd(t1k), w1, ((d, n512), lambda i, j, k: (0, j)),
        [(vec["b_ff1"], ((1, n512), lambda i, j, k: (0, j)))],
        [jax.ShapeDtypeStruct((seq, ff), BF16)] * 2, [tile_f, tile_f], epi_ff1, job=job))
    w2 = wts["w_ff2"]

    def epi_ln2(acc, i, j, ex, outs):
        xh1, tgt, g1, b1, bf2, g2, b2 = ex
        ds_ref, dsb_ref, sm_ref, loss_ref = outs
        x1 = xh1[...] * g1[...] + b1[...]
        s2 = DN_ALPHA * x1 + (acc + bf2[...])
        xhat, rstd, y = _ln_fwd(s2, g2[...], b2[...])
        e = y - tgt[...]
        part = 0.5 * jnp.sum(jnp.mean(e * e, axis=-1, keepdims=True))
        dy = e * (1.0 / d)
        ds2 = _ln_bwd(dy, xhat, rstd, g2[...])
        ds_ref[...] = ds2
        dsb_ref[...] = ds2.astype(BF16)
        sm_ref[0:1, :] += _colsum(dy * xhat)
        sm_ref[1:2, :] += _colsum(dy)
        sm_ref[2:3, :] += _colsum(ds2)
        loss_ref[...] += jnp.full(loss_ref.shape, part, F32)

    def zero_tail(n_tail):
        def init(i, j, outs):
            @pl.when(i == 0)
            def _():
                for o in outs[-n_tail:]:
                    o[...] = jnp.zeros_like(o)
        return init

    ds2, ds2_bf, sm_ln2, loss_blk = hosted("ff2_ln2", lambda job: _mm(
        "ff2_ln2", "nn", (seq // t512, 1, ff // tkf), hdn, ((t512, tkf), lambda i, j, k: (i, k)),
        w2, ((tkf, d), lambda i, j, k: (k, 0)),
        [(xhat1, rowd(t512)), (target, rowd(t512)), (vec["ln1_g"], vecd), (vec["ln1_b"], vecd),
         (vec["b_ff2"], vecd), (vec["ln2_g"], vecd), (vec["ln2_b"], vecd)],
        [jax.ShapeDtypeStruct((seq, d), F32), jax.ShapeDtypeStruct((seq, d), BF16),
         jax.ShapeDtypeStruct((SUBLANES, d), F32), jax.ShapeDtypeStruct((SUBLANES, 128), F32)],
        [rowd(t512), rowd(t512), ((SUBLANES, d), full), ((SUBLANES, 128), full)],
        epi_ln2, acc_shape=(t512, d), epi_init=zero_tail(2), sub=sub_rows, job=job))

    tkw = min(2048, seq)

    def dw(name, wname, a, b, m_dim, n_dim, b_spec=None, row0=0):
        tm, tn = min(1024, m_dim), min(1024, n_dim)
        while m_dim % tm or row0 % tm:
            tm //= 2
        b_spec = b_spec or ((tkw, tn), lambda i, j, k: (k, j))
        i0 = row0 // tm
        (out,) = hosted(name, lambda job: _mm(
            name, "tn", (m_dim // tm, n_dim // tn, seq // tkw),
            a, ((tkw, tm), lambda i, j, k: (k, i0 + i)), b, b_spec, [],
            [jax.ShapeDtypeStruct((m_dim, n_dim), F32)], [((tm, tn), lambda i, j, k: (i, j))],
            epi_store if seq == tkw else None, job=job))
        comm.grad(wname, out)

    dw("dw_ff2", "w_ff2", hdn, ds2_bf, ff, d)

    def epi_dpre(acc, i, j, ex, outs):
        dpre = acc * (2.0 * ex[0][...].astype(F32))
        outs[0][...] = dpre.astype(BF16)

        @pl.when(i == 0)
        def _():
            outs[1][...] = jnp.zeros_like(outs[1])

        outs[1][0:1, :] += _colsum(dpre)

    dpre, sm_bff1 = hosted("dhdn", lambda job: _mm(
        "dhdn", "nt", (seq // t1k, ff // n512, 1), ds2_bf, rowd(t1k), w2, ((n512, d), lambda i, j, k: (j, 0)),
        [(relu_h, tile_f)],
        [jax.ShapeDtypeStruct((seq, ff), BF16), jax.ShapeDtypeStruct((SUBLANES, ff), F32)],
        [tile_f, ((SUBLANES, n512), lambda i, j, k: (0, j))], epi_dpre, order="ji", job=job))

    dw("dw_ff1", "w_ff1", x1_bf, dpre, d, ff)

    def epi_ln1b(acc, i, j, ex, outs):
        ds2_ref, xh1, rs1, g1 = ex
        ds_ref, dsb_ref, sm_ref = outs
        dy1 = acc + DN_ALPHA * ds2_ref[...]
        xhat = xh1[...]
        ds1 = _ln_bwd(dy1, xhat, rs1[...], g1[...])
        ds_ref[...] = ds1
        dsb_ref[...] = ds1.astype(BF16)
        sm_ref[0:1, :] += _colsum(dy1 * xhat)
        sm_ref[1:2, :] += _colsum(dy1)
        sm_ref[2:3, :] += _colsum(ds1)

    ds1, ds1_bf, sm_ln1 = hosted("dx1_ln1", lambda job: _mm(
        "dx1_ln1", "nt", (seq // t512, 1, ff // tkf), dpre, ((t512, tkf), lambda i, j, k: (i, k)),
        w1, ((d, tkf), lambda i, j, k: (0, k)),
        [(ds2, rowd(t512)), (xhat1, rowd(t512)), (rstd1, ((t512, 1), lambda i, j, k: (i, 0))),
         (vec["ln1_g"], vecd)],
        [jax.ShapeDtypeStruct((seq, d), F32), jax.ShapeDtypeStruct((seq, d), BF16),
         jax.ShapeDtypeStruct((SUBLANES, d), F32)],
        [rowd(t512), rowd(t512), ((SUBLANES, d), full)], epi_ln1b, acc_shape=(t512, d),
        epi_init=zero_tail(1), sub=sub_rows, job=job))

    dw("dw_out", "w_out", m_bf, ds1_bf, d, d)
    n_n = d // n512
    tile_d = ((t512, n512), lambda i, j, k: (i, j))

    def epi_dm(acc, i, j, ex, outs):
        la, lb, pa, pb = ex
        ga, gb = _sig(la[...]), _sig(lb[...])
        outs[0][...] = (acc * ga).astype(BF16)
        outs[1][...] = (acc * gb).astype(BF16)
        outs[2][0] = (acc * pa[...].astype(F32) * ga * (1.0 - ga)).astype(BF16)
        outs[2][1] = (acc * pb[...].astype(F32) * gb * (1.0 - gb)).astype(BF16)

    dp_a, dp_b, dz = _mm(
        "dm", "nt", (seq // t512, n_n, 1), ds1_bf, rowd(t512), w_out, ((n512, d), lambda i, j, k: (j, 0)),
        [(z, ((t512, n512), lambda i, j, k: (i, 3 * n_n + j))),
         (z, ((t512, n512), lambda i, j, k: (i, 4 * n_n + j))), (p_a, tile_d), (p_b, tile_d)],
        [jax.ShapeDtypeStruct((seq, d), BF16), jax.ShapeDtypeStruct((seq, d), BF16),
         jax.ShapeDtypeStruct((5, seq, d), BF16)],
        [tile_d, tile_d, ((2, t512, n512), lambda i, j, k: (0, i, j))], epi_dm)

    dw("dw_pool_up", "w_pool_up", y_pool, dp_a, d, d)
    dw("dw_lru_up", "w_lru_up", y_lru, dp_b, d, d)

    def epi_bf(acc, i, j, ex, outs):
        outs[0][...] = acc.astype(BF16)

    (dy_pool,) = _mm("dy_pool", "nt", (seq // t512, n_n, 1), dp_a, rowd(t512), w_pu,
                     ((n512, d), lambda i, j, k: (j, 0)), [],
                     [jax.ShapeDtypeStruct((seq, d), BF16)], [tile_d], epi_bf)

    def epi_dylru(acc, i, j, ex, outs):
        hf, hb, ug, _ = ex
        u = ug[...]
        outs[0][...] = acc * _gelu(u)
        outs[1][...] = (acc * (hf[...] + hb[...]) * _gelu_grad(u)).astype(BF16)

    dz_in = dz
    dh, dz = hosted("dy_lru", lambda job: _mm(
        "dy_lru", "nt", (seq // t512, n_n, 1), dp_b, rowd(t512), w_lu, ((n512, d), lambda i, j, k: (j, 0)),
        [(h_f, tile_d), (h_b, tile_d), (z, ((t512, n512), lambda i, j, k: (i, 2 * n_n + j))),
         (dz_in, (None, None))],
        [jax.ShapeDtypeStruct((seq, d), F32), jax.ShapeDtypeStruct(dz_in.shape, BF16)],
        [tile_d, ((None, t512, n512), lambda i, j, k: (4, i, j))], epi_dylru, aliases={5: 1}, job=job))

    dz, g_pw, sm_pool = _pool_bwd(z, dy_pool, pool_w, vec["pool_scale"], dz, seq, d, tc)
    comm.grad("pool_w", g_pw)
    dxc0, g_wa0, g_wx0, sm_l0 = hosted("lru_bwd_0", lambda job: _lru_bwd(
        0, xc, dh, h_f, sv, wa, wx, None, seq, d, tc, job=job))
    dxc, g_wa1, g_wx1, sm_l1 = hosted("lru_bwd_1", lambda job: _lru_bwd(
        1, xc, dh, h_b, sv, wa, wx, dxc0, seq, d, tc, job=job))
    comm.grad("lru_wa", jnp.concatenate([g_wa0, g_wa1], axis=0))
    comm.grad("lru_wx", jnp.concatenate([g_wx0, g_wx1], axis=0))
    dz, sm_conv = _conv_bwd(z, dxc, sv, dz, seq, d, tc, blk)

    tnw = min(1024, d)
    per_seg = d // tnw
    seg_spec = ((None, tkw, tnw), lambda i, j, k: ((j // per_seg + 2) % 5, k, j % per_seg))
    lo_rows = 3 * d // 4
    dw("dw_in_lo", "w_in_lo", x_bf, dz, lo_rows, n_in, b_spec=seg_spec)
    dw("dw_in_hi", "w_in_hi", x_bf, dz, d - lo_rows, n_in, b_spec=seg_spec, row0=lo_rows)

    nk = d // tkd

    def epi_dx(acc, i, j, ex, outs):
        outs[0][...] = acc + DN_ALPHA * ex[0][...]

    (grad_x,) = hosted("dx", lambda job: _mm(
        "dx", "nt", (seq // t512, 1, n_in // tkd), dz,
        ((None, t512, tkd), lambda i, j, k: ((k // nk + 2) % 5, i, k % nk)),
        wts["w_in"], ((d, tkd), lambda i, j, k: (0, k)), [(ds1, rowd(t512))],
        [jax.ShapeDtypeStruct((seq, d), F32)], [rowd(t512)], epi_dx, acc_shape=(t512, d), job=job))

    small = {"ln2": sm_ln2, "b_ff1": sm_bff1, "ln1": sm_ln1, "pool": sm_pool, "lru0": sm_l0, "lru1": sm_l1,
             "conv": sm_conv}
    return loss_blk[0, 0], grad_x, small


REP = ("pool_scale", "conv_b", "b_out", "ln1_g", "ln1_b", "b_ff2", "ln2_g", "ln2_b")
SHARDED_SMALL = (("conv_w", 4), ("lru_ba", 2), ("lru_bx", 2), ("lru_lambda", 2))
WEIGHT_ORDER = ("w_in", "pool_w", "pool_scale", "conv_w", "conv_b", "lru_wa", "lru_ba", "lru_wx", "lru_bx",
                "lru_lambda", "w_pool_up", "w_lru_up", "w_out", "b_out", "ln1_g", "ln1_b", "w_ff1", "b_ff1",
                "w_ff2", "b_ff2", "ln2_g", "ln2_b")


def _pad_rows(a, rows):
    return jnp.concatenate([a, jnp.zeros((rows - a.shape[0], a.shape[1]), a.dtype)], axis=0)


def kernel(x, w_in, pool_w, pool_scale, conv_w, conv_b, lru_wa, lru_ba, lru_wx, lru_bx, lru_lambda, w_pool_up, w_lru_up, w_out, b_out, ln1_g, ln1_b, w_ff1, b_ff1, w_ff2, b_ff2, ln2_g, ln2_b, loss_target, m_w_in, m_pool_w, m_pool_scale, m_conv_w, m_conv_b, m_lru_wa, m_lru_ba, m_lru_wx, m_lru_bx, m_lru_lambda, m_w_pool_up, m_w_lru_up, m_w_out, m_b_out, m_ln1_g, m_ln1_b, m_w_ff1, m_b_ff1, m_w_ff2, m_b_ff2, m_ln2_g, m_ln2_b, v_w_in, v_pool_w, v_pool_scale, v_conv_w, v_conv_b, v_lru_wa, v_lru_ba, v_lru_wx, v_lru_bx, v_lru_lambda, v_w_pool_up, v_w_lru_up, v_w_out, v_b_out, v_ln1_g, v_ln1_b, v_w_ff1, v_b_ff1, v_w_ff2, v_b_ff2, v_ln2_g, v_ln2_b):
    args = dict(locals())
    w = {n: args[n] for n in WEIGHT_ORDER}
    mom = {n: args["m_" + n] for n in WEIGHT_ORDER}
    var = {n: args["v_" + n] for n in WEIGHT_ORDER}
    seq, d = x.shape[1], x.shape[2]
    n_heads, blk = lru_wa.shape[2], lru_wa.shape[4]
    n_groups = pool_w.shape[1]
    ff = b_ff1.shape[1]
    cx, cy, cc = _coords()
    me = 4 * cx + 2 * cy + cc
    cidx = jnp.reshape(cc, (1,)).astype(jnp.int32)
    qidx = jnp.reshape(2 * cx + cy, (1,)).astype(jnp.int32)

    fam_of = dict(BIG)
    sviews = {n: _shard_view(w[n], fam) for n, fam in BIG}
    size_of = {n: sviews[n].shape[1] for n, _ in BIG}
    for half in ("w_in_lo", "w_in_hi"):
        fam_of[half], size_of[half] = fam_of["w_in"], size_of["w_in"]
    wts = {}

    def take_gathered(names, arrays):
        for n, g in zip(names, arrays):
            if n == "pool_w":
                g = g.reshape(n_groups, d // n_groups, d // n_groups)
            elif n in ("lru_wa", "lru_wx"):
                g = g.reshape(2, n_heads, blk, blk)
            elif fam_of[n] == "row":
                g = g.reshape(g.shape[1:])
            wts[n] = g

    shard_bf = {n: sviews[n].astype(BF16) for n, _ in BIG}
    piece = sviews["w_in"].shape[1] // W_IN_PASSES
    for k in range(W_IN_PASSES):
        name = "w_in_piece_%d" % k
        shard_bf[name] = shard_bf["w_in"][:, k * piece:(k + 1) * piece]
        fam_of[name] = "col"
    partial = {}

    def gather_job(items, extra=()):
        return _ag_job([shard_bf[n] for n, _ in items] + [e for e, _ in extra],
                       [fam_of[n] for n, _ in items] + [f for _, f in extra],
                       parts=[p for _, p in items] + [None] * len(extra),
                       into=[partial.get(n) if p else None for n, p in items] + [None] * len(extra))

    def take_pieces(items, arrays):
        for (n, p), g in zip(items, arrays):
            if p is None or p[0] + p[2] == p[1]:
                partial.pop(n, None)
                take_gathered([n], [g])
            else:
                partial[n] = g

    class Plan:
        gather = {"z_proj_0": (("w_in_piece_1", None),), "z_proj_1": (("w_in_piece_2", None),),
                  "z_proj_2": (("w_in_piece_3", None),), "z_proj_3": (("w_in_piece_4", None),),
                  "z_proj_4": (("pool_w", None), ("lru_wa", None), ("lru_wx", None)),
                  "pool_fwd": (("w_pool_up", None),), "lru_fwd": (("w_lru_up", None), ("w_out", None)),
                  "lru_rev": (("w_ff1", (0, 4, 2)),), "merge": (("w_ff1", (2, 4, 2)),),
                  "out_ln1": (("w_ff2", (0, 4, 1)),), "ff1": (("w_ff2", (1, 4, 3)),),
                  "dw_ff2": (("w_in", (0, 2, 1)),), "dhdn": (("w_in", (1, 2, 1)),)}
        late_mid = ("z_proj_0", "z_proj_1", "z_proj_2", "z_proj_3", "z_proj_4", "pool_fwd", "lru_fwd", "lru_rev",
                    "out_ln1")
        to_sibling = {"dhdn": ("w_ff2",), "dx1_ln1": ("w_ff1",), "lru_bwd_0": ("w_out", "w_pool_up", "w_lru_up"),
                      "dw_in_hi": ("w_in_lo",), "dx": ("w_in_hi", "pool_w", "lru_wa", "lru_wx")}
        to_chips = {"dx1_ln1": ("w_ff2",), "lru_bwd_0": ("w_ff1",), "lru_bwd_1": ("w_out", "w_pool_up", "w_lru_up"),
                    "dx": ("w_in_lo",)}

        def __init__(self):
            self.grads, self.parts, self.recv_b = {}, {}, {}

        def grad(self, name, g):
            self.grads[name] = g if fam_of[name] == "col" else g.reshape((-1,) + g.shape[-2:])

        def job(self, host):
            jobs = []
            if host in self.gather:
                jobs.append(gather_job(self.gather[host]))
                jobs[-1].mid_frac = 1.0 if host in self.late_mid else 0.9
            if host in self.to_sibling:
                names = self.to_sibling[host]
                jobs.append(_rs_sibling_job([self.grads[n] for n in names], [fam_of[n] for n in names],
                                            [size_of[n] for n in names]))
            if host in self.to_chips:
                jobs.append(_rs_chips_job([self.parts[n] for n in self.to_chips[host]]))
            return _join_jobs(jobs)

        def after(self, host, job):
            results = iter(_job_results(job))
            if host in self.gather:
                take_pieces(self.gather[host], next(results))
            if host in self.to_sibling:
                for n, r in zip(self.to_sibling[host], next(results)):
                    self.parts[n] = _rs_add("rs_add_" + n, self.grads[n], r, fam_of[n], size_of[n], cidx)
            if host in self.to_chips:
                self.recv_b.update(zip(self.to_chips[host], next(results)))

    first = (("w_in_piece_0", None),)
    sv_shard = _pad_rows(jnp.concatenate([w[n].reshape(r, -1) for n, r in SHARDED_SMALL], axis=0), 16)
    gathered = _run_job("ag_first", gather_job(first, [(sv_shard, "col")]))
    take_pieces(first, gathered[:-1])
    vec = {n: w[n] for n in REP}
    vec["b_ff1"] = b_ff1
    vec["sv"] = gathered[-1]

    plan = Plan()
    loss_part, grad_x, small = _local_step(x.reshape(seq, d), loss_target.reshape(seq, d), wts, vec, plan)
    loss = lax.psum(loss_part, AXES)

    tail = Plan.to_sibling["dx"]
    plan.recv_b.update(zip(tail, _run_job("rs_chips_tail", _rs_chips_job([plan.parts[n] for n in tail]))))
    out_g, out_d, out_m, out_v = {}, {}, {}, {}
    for n, fam in BIG:
        halves = [n + "_lo", n + "_hi"] if n == "w_in" else [n]
        res = _rs_final_adam("adam_" + n, [plan.parts[h] for h in halves], [plan.recv_b[h] for h in halves],
                             sviews[n], _shard_view(mom[n], fam), _shard_view(var[n], fam), fam, qidx)
        out_g[n], out_d[n], out_m[n], out_v[n] = [r.reshape(w[n].shape) for r in res]

    rows = [small["pool"][0:1], small["conv"][4:5], small["ln1"][2:3], small["ln1"][0:1], small["ln1"][1:2],
            small["ln2"][2:3], small["ln2"][0:1], small["ln2"][1:2], small["b_ff1"][0:1].reshape(ff // d, d),
            small["conv"][0:4], small["lru0"][0:1], small["lru1"][0:1], small["lru0"][1:2], small["lru1"][1:2],
            small["lru0"][2:3], small["lru1"][2:3]]
    n_rep = len(REP) + ff // d
    n_rows = n_rep + sum(r for _, r in SHARDED_SMALL)
    pad_rows = -(-n_rows // SUBLANES) * SUBLANES
    packed = _pad_rows(jnp.concatenate(rows, axis=0), pad_rows)
    (all_small,) = _run_job("ag_small", _ag_job([packed], ["lead"]))
    g_small = _sum8("sum_small", all_small)

    def pack_rep(t):
        return jnp.concatenate([t[n] for n in REP] + [t["b_ff1"].reshape(ff // d, d)], axis=0)

    def pack_sh(t):
        return jnp.concatenate([t[n].reshape(r, -1) for n, r in SHARDED_SMALL], axis=0)

    g_rep = g_small[:n_rep]
    cs = d // N_DEV
    g_sh = lax.dynamic_slice_in_dim(g_small[n_rep:n_rows], me * cs, cs, axis=1)
    d_rep, m_rep, v_rep = _adam_small("adam_rep", pack_rep(w), g_rep, pack_rep(mom), pack_rep(var))
    d_sh, m_sh, v_sh = _adam_small("adam_sharded", pack_sh(w), g_sh, pack_sh(mom), pack_sh(var))

    def unpack(rep_t, sh_t, dst):
        for i, n in enumerate(REP):
            dst[n] = rep_t[i:i + 1].reshape(w[n].shape)
        dst["b_ff1"] = rep_t[len(REP):n_rep].reshape(w["b_ff1"].shape)
        r0 = 0
        for n, r in SHARDED_SMALL:
            dst[n] = sh_t[r0:r0 + r].reshape(w[n].shape)
            r0 += r

    unpack(g_rep, g_sh, out_g)
    unpack(d_rep, d_sh, out_d)
    unpack(m_rep, m_sh, out_m)
    unpack(v_rep, v_sh, out_v)

    outs = [loss, grad_x.reshape(x.shape)]
    for t in (out_g, out_d, out_m, out_v):
        outs += [t[n] for n in WEIGHT_ORDER]
    return tuple(outs)
```

```python
import functools

import jax
import jax.numpy as jnp
from jax import lax
from jax.experimental import pallas as pl
from jax.experimental.pallas import tpu as pltpu
from jax.experimental.pallas import tpu_sc as plsc

F32 = jnp.float32
BF16 = jnp.bfloat16
MESH = pl.DeviceIdType.MESH
AXES = ("x", "y", "c")
N_DEV = 8

DN_ALPHA = 2.0 ** 0.25
LN_EPS = 1e-5
LRU_C = 8.0
ADAM_LR = 0.001
ADAM_B1 = 0.9
ADAM_B2 = 0.999
ADAM_EPS = 1e-08
ADAM_WD = 0.01
ADAM_STEP = 10
GELU_C = 0.7978845608028654
GELU_K = 0.044715

W_IN_PASSES = 5
HALO = 16
SUBLANES = 8
VMEM_MB = 56


def _cparams(sem, vmem_mb=VMEM_MB):
    return pltpu.CompilerParams(dimension_semantics=sem, vmem_limit_bytes=vmem_mb << 20)


HBM_SPEC = pl.BlockSpec(memory_space=pl.ANY)


class _Job:
    def __init__(self, ins, outs, sems, start, finish, mid=None):
        self.ins, self.outs, self.sems = list(ins), list(outs), list(sems)
        self.start, self.finish, self.mid = start, finish, mid
        self.alias = {}
        self.mid_frac = 0.75
        self.results = None


def _join_jobs(jobs):
    jobs = [j for j in jobs if j is not None]
    if len(jobs) <= 1:
        return jobs[0] if jobs else None

    def split(refs, counts):
        out, p = [], 0
        for n in counts:
            out.append(refs[p:p + n])
            p += n
        return out

    def phase(which):
        def run(ins, outs, sems):
            parts = zip(jobs, split(ins, [len(j.ins) for j in jobs]), split(outs, [len(j.outs) for j in jobs]),
                        split(sems, [len(j.sems) for j in jobs]))
            for j, ji, jo, js in parts:
                fn = getattr(j, which)
                if fn is not None:
                    fn(ji, jo, js)
        return run

    joined = _Job(sum((j.ins for j in jobs), []), sum((j.outs for j in jobs), []), sum((j.sems for j in jobs), []),
                  phase("start"), phase("finish"), phase("mid") if any(j.mid for j in jobs) else None)
    joined.parts = jobs
    joined.mid_frac = max(j.mid_frac for j in jobs if j.mid is not None) if any(j.mid for j in jobs) else 0.75
    i0 = o0 = 0
    for j in jobs:
        joined.alias.update({i0 + i: o0 + o for i, o in j.alias.items()})
        i0 += len(j.ins)
        o0 += len(j.outs)
    return joined


def _job_results(job):
    if job is None:
        return []
    parts = getattr(job, "parts", None)
    if parts is None:
        return [job.results]
    out, p = [], 0
    for j in parts:
        out.append(job.results[p:p + len(j.outs)])
        p += len(j.outs)
    return out


def _pcall(name, body, grid, in_specs, out_specs, out_shape, inputs, scratch=(), aliases=None,
           vmem_mb=VMEM_MB, job=None):
    in_specs, out_specs, out_shape, scratch = list(in_specs), list(out_specs), list(out_shape), list(scratch)
    params = _cparams(("arbitrary",) * len(grid), vmem_mb)
    if job is None:
        return pl.pallas_call(body, name=name, grid=grid, in_specs=in_specs, out_specs=out_specs,
                              out_shape=out_shape, scratch_shapes=scratch,
                              input_output_aliases=aliases or {}, compiler_params=params)(*inputs)
    n_in, n_out, n_scr = len(inputs), len(out_shape), len(scratch)
    ji, jo = len(job.ins), len(job.outs)
    total = 1
    for g in grid:
        total *= g
    mid_step = int(job.mid_frac * total) if total >= 4 and job.mid_frac < 1.0 else None

    def wrapped(*refs):
        p = 0
        ins = refs[p:p + n_in]
        p += n_in
        jins = refs[p:p + ji]
        p += ji
        outs = refs[p:p + n_out]
        p += n_out
        jouts = refs[p:p + jo]
        p += jo
        scr = refs[p:p + n_scr]
        sems = refs[p + n_scr:]
        step = pl.program_id(0)
        for ax in range(1, len(grid)):
            step = step * grid[ax] + pl.program_id(ax)

        @pl.when(step == 0)
        def _():
            job.start(jins, jouts, sems)

        if job.mid is not None and mid_step is not None:
            @pl.when(step == mid_step)
            def _():
                job.mid(jins, jouts, sems)

        body(*ins, *outs, *scr)

        @pl.when(step == total - 1)
        def _():
            if job.mid is not None and mid_step is None:
                job.mid(jins, jouts, sems)
            job.finish(jins, jouts, sems)

    all_aliases = dict(aliases or {})
    all_aliases.update({n_in + i: n_out + o for i, o in job.alias.items()})
    res = pl.pallas_call(
        wrapped, name=name, grid=grid, in_specs=in_specs + [HBM_SPEC] * ji,
        out_specs=out_specs + [HBM_SPEC] * jo, out_shape=out_shape + job.outs,
        scratch_shapes=scratch + job.sems, input_output_aliases=all_aliases, compiler_params=params,
    )(*inputs, *job.ins)
    job.results = list(res[n_out:])
    return list(res[:n_out])


def _run_job(name, job):
    ji, jo = len(job.ins), len(job.outs)

    def body(*refs):
        jins, jouts, sems = refs[:ji], refs[ji:ji + jo], refs[ji + jo:]
        job.start(jins, jouts, sems)
        if job.mid is not None:
            job.mid(jins, jouts, sems)
        job.finish(jins, jouts, sems)

    res = pl.pallas_call(body, name=name, in_specs=[HBM_SPEC] * ji, out_specs=[HBM_SPEC] * jo,
                         out_shape=job.outs, scratch_shapes=job.sems, input_output_aliases=job.alias)(*job.ins)
    job.results = list(res)
    return job.results


def _sequencer_job(name, job, collective_id):
    ji, jo = len(job.ins), len(job.outs)

    def body(*refs):
        jins, jouts, sems = refs[:ji], refs[ji:ji + jo], refs[ji + jo:]
        barrier = pltpu.get_barrier_semaphore()
        x, y, c = lax.axis_index("x"), lax.axis_index("y"), lax.axis_index("c")
        for r in range(1, N_DEV):
            peer = ((1 - x) if r & 4 else x, (1 - y) if r & 2 else y, (1 - c) if r & 1 else c)
            pl.semaphore_signal(barrier, inc=1, device_id=peer, device_id_type=MESH)
        pl.semaphore_wait(barrier, N_DEV - 1)
        job.start(jins, jouts, sems)
        if job.mid is not None:
            job.mid(jins, jouts, sems)
        job.finish(jins, jouts, sems)

    res = pl.kernel(
        body, name=name, out_type=job.outs, mesh=plsc.ScalarSubcoreMesh(axis_name="sequencer", num_cores=1),
        scratch_types=job.sems, compiler_params=pltpu.CompilerParams(collective_id=collective_id),
    )(*job.ins)
    job.results = list(res)
    return job.results


def _dot(mode, a, b):
    if mode == "nn":
        dims = (((1,), (0,)), ((), ()))
    elif mode == "nt":
        dims = (((1,), (1,)), ((), ()))
    else:
        dims = (((0,), (0,)), ((), ()))
    return lax.dot_general(a, b, dims, preferred_element_type=F32)


def _sig(x):
    return 0.5 * jnp.tanh(0.5 * x) + 0.5


def _gelu(x):
    t = jnp.tanh(GELU_C * (x + GELU_K * x * x * x))
    return 0.5 * x * (1.0 + t)


def _gelu_grad(x):
    x2 = x * x
    t = jnp.tanh(GELU_C * (x + GELU_K * x * x2))
    return 0.5 * (1.0 + t) + 0.5 * x * (1.0 - t * t) * GELU_C * (1.0 + 3.0 * GELU_K * x2)


def _colsum(v):
    return jnp.sum(v, axis=0, keepdims=True)


def _mm(name, mode, grid, a, a_spec, b, b_spec, extras, out_shapes, out_specs, epi, *,
        order="ij", acc_shape=None, aliases=None, vmem_mb=VMEM_MB, epi_init=None, sub=None, job=None):
    gm, gn, gk = grid

    def spec(s):
        bs, f = s
        if f is None:
            return pl.BlockSpec(memory_space=pl.ANY)
        if order == "ij":
            return pl.BlockSpec(bs, lambda i, j, k, f=f: f(i, j, k))
        return pl.BlockSpec(bs, lambda j, i, k, f=f: f(i, j, k))

    ne, no = len(extras), len(out_shapes)

    def kern(*refs):
        a_ref, b_ref = refs[0], refs[1]
        ex = refs[2:2 + ne]
        outs = refs[2 + ne:2 + ne + no]
        if order == "ij":
            i, j = pl.program_id(0), pl.program_id(1)
        else:
            j, i = pl.program_id(0), pl.program_id(1)
        k = pl.program_id(2)
        prod = _dot(mode, a_ref[...], b_ref[...])
        if gk == 1:
            if epi_init is not None:
                epi_init(i, j, outs)
            epi(prod, i, j, ex, outs)
        elif epi is None:
            @pl.when(k == 0)
            def _():
                outs[0][...] = prod

            @pl.when(k > 0)
            def _():
                outs[0][...] += prod
        else:
            acc = refs[-1]

            @pl.when(k == 0)
            def _():
                acc[...] = prod

            @pl.when(k > 0)
            def _():
                acc[...] += prod

            @pl.when(k == gk - 1)
            def _():
                if epi_init is not None:
                    epi_init(i, j, outs)
                if sub is None:
                    epi(acc[...], i, j, ex, outs)
                else:
                    tm = acc_shape[0]

                    def rows_of(r, rs):
                        return r.at[rs, :] if r.shape[0] == tm else r

                    def blk(t, carry):
                        rs = pl.ds(pl.multiple_of(t * sub, sub), sub)
                        epi(acc[rs, :], i, j, [rows_of(r, rs) for r in ex], [rows_of(r, rs) for r in outs])
                        return carry

                    lax.fori_loop(0, tm // sub, blk, 0)

    g = (gm, gn, gk) if order == "ij" else (gn, gm, gk)
    scratch = [pltpu.VMEM(acc_shape, F32)] if gk > 1 and epi is not None else []
    return _pcall(name, kern, g, [spec(a_spec), spec(b_spec)] + [spec(s) for _, s in extras],
                  [spec(s) for s in out_specs], out_shapes, [a, b] + [e for e, _ in extras],
                  scratch=scratch, aliases=aliases, vmem_mb=vmem_mb, job=job)


def _ext(ref, c, n_chunks, tc, seq):
    c0 = pl.multiple_of(c * tc, tc)
    body = ref[pl.ds(c0, tc), :].astype(F32)
    t0 = pl.multiple_of(jnp.maximum(c0 - HALO, 0), HALO)
    b0 = pl.multiple_of(jnp.minimum(c0 + tc, seq - HALO), HALO)
    top = ref[pl.ds(t0, HALO), :].astype(F32)
    bot = ref[pl.ds(b0, HALO), :].astype(F32)
    top = jnp.where(c > 0, top, 0.0)
    bot = jnp.where(c < n_chunks - 1, bot, 0.0)
    return jnp.concatenate([top, body, bot], axis=0)


def _shifted(vext, off, tc):
    n = vext.shape[0]
    r = vext if off == 0 else pltpu.roll(vext, (n - off) % n, 0)
    return r[HALO:HALO + tc]


def _win_sum(vext, g, extra, tc):
    s2 = vext + pltpu.roll(vext, 1, 0)
    s4 = s2 + pltpu.roll(s2, 2, 0)
    s8 = s4 + pltpu.roll(s4, 4, 0)
    s16 = s8 + pltpu.roll(s8, 8, 0)
    outs = [_shifted(s, extra + hw - 1, tc) for s, hw in ((s2, 1), (s4, 2), (s8, 4), (s16, 8))]
    return jnp.where(g == 0, outs[0], jnp.where(g == 1, outs[1], jnp.where(g == 2, outs[2], outs[3])))


def _win_cnt(t, hw, seq):
    return (jnp.minimum(t + hw, seq) - jnp.maximum(t - hw, 0)).astype(F32)


def _pool_d(uext, g, c, tc, seq):
    hw = jnp.left_shift(1, g)
    t = c * tc + lax.broadcasted_iota(jnp.int32, (tc, 1), 0)
    ws = _win_sum(uext, g, 0, tc)
    return ws / _win_cnt(t, hw, seq) - uext[HALO:HALO + tc]


def _scan_tiles(a_ref, b_ref, h_ref, carry_ref, n_tiles, reverse):
    blk = a_ref.shape[1]
    row = lax.broadcasted_iota(jnp.int32, (SUBLANES, blk), 0)

    def tile(j, hc):
        jj = (n_tiles - 1 - j) if reverse else j
        off = pl.multiple_of(jj * SUBLANES, SUBLANES)
        a = a_ref[pl.ds(off, SUBLANES), :]
        b = b_ref[pl.ds(off, SUBLANES), :]
        for kk in (1, 2, 4):
            sh = (SUBLANES - kk) if reverse else kk
            a_s = pltpu.roll(a, sh, 0)
            b_s = pltpu.roll(b, sh, 0)
            m = (row < SUBLANES - kk) if reverse else (row >= kk)
            a_s = jnp.where(m, a_s, 1.0)
            b_s = jnp.where(m, b_s, 0.0)
            b = a * b_s + b
            a = a * a_s
        h = a * hc + b
        h_ref[pl.ds(off, SUBLANES), :] = h
        return h[0:1, :] if reverse else h[SUBLANES - 1:SUBLANES, :]

    group = SUBLANES if n_tiles % SUBLANES == 0 else 1

    def tiles(jg, hc):
        for u in range(group):
            hc = tile(jg * group + u, hc)
        return hc

    hc = lax.fori_loop(0, n_tiles // group, tiles, carry_ref[0:1, :])
    carry_ref[0:1, :] = hc


def _lru_k(lam):
    y = -lam
    e = jnp.exp(-jnp.abs(y))
    u = 1.0 + e
    l1p = jnp.where(u == 1.0, e, jnp.log(u) * (e / (u - 1.0)))
    return -LRU_C * (jnp.maximum(y, 0.0) + l1p)


def _lru_gates(xc, wa, wx, ba, bx, lam):
    xb = xc.astype(BF16)
    r = _sig(jnp.dot(xb, wa, preferred_element_type=F32) + ba)
    i = _sig(jnp.dot(xb, wx, preferred_element_type=F32) + bx)
    k = _lru_k(lam)
    la = k * r
    a = jnp.exp(la)
    s = jnp.sqrt(-jnp.tanh(la) * (a * a + 1.0))
    return r, i, k, a, s


SV_CONV, SV_BA, SV_BX, SV_LAM = 0, 4, 6, 8


def _pool_fwd(z, pw, scale, seq, d, tc, job=None):
    n_g = pw.shape[0]
    pg = d // n_g
    n_chunks = seq // tc

    def kern(z_ref, pw_ref, sc_ref, y_ref):
        g, c = pl.program_id(0), pl.program_id(1)
        uext = _ext(z_ref, c, n_chunks, tc, seq)
        dd = _pool_d(uext, g, c, tc, seq)
        q = jnp.dot(dd.astype(BF16), pw_ref[...], preferred_element_type=F32)
        y_ref[...] = (q * sc_ref[...]).astype(BF16)

    (y,) = _pcall(
        "pool_fwd", kern, (n_g, n_chunks),
        [pl.BlockSpec((seq, pg), lambda g, c: (0, g)),
         pl.BlockSpec((None, pg, pg), lambda g, c: (g, 0, 0)),
         pl.BlockSpec((1, pg), lambda g, c: (0, g))],
        [pl.BlockSpec((tc, pg), lambda g, c: (c, g))],
        [jax.ShapeDtypeStruct((seq, d), BF16)], [z, pw, scale], job=job)
    return y


def _lru_fwd(z, sv, conv_b, wa, wx, seq, d, tc, job=None):
    n_h, blk = wa.shape[1], wa.shape[2]
    n_chunks = seq // tc
    lru_off = d // blk

    def kern(z_ref, sv_ref, cb_ref, wa_ref, wx_ref, xc_ref, h_ref, a_s, b_s, carry):
        c = pl.program_id(1)
        uext = _ext(z_ref, c, n_chunks, tc, seq)
        xc = cb_ref[...]
        for k in range(4):
            xc = xc + _shifted(uext, k - 2, tc) * sv_ref[SV_CONV + k:SV_CONV + k + 1, :]
        xc_ref[...] = xc
        _, i, _, a, s = _lru_gates(xc, wa_ref[...], wx_ref[...], sv_ref[SV_BA:SV_BA + 1, :],
                                   sv_ref[SV_BX:SV_BX + 1, :], sv_ref[SV_LAM:SV_LAM + 1, :])
        a_s[...] = a
        b_s[...] = s * (i * xc)

        @pl.when(c == 0)
        def _():
            carry[...] = jnp.zeros_like(carry)

        _scan_tiles(a_s, b_s, h_ref, carry, tc // SUBLANES, False)

    col = lambda h, c: (c, h)
    return _pcall(
        "lru_fwd", kern, (n_h, n_chunks),
        [pl.BlockSpec((seq, blk), lambda h, c: (0, lru_off + h)),
         pl.BlockSpec((16, blk), lambda h, c: (0, h)),
         pl.BlockSpec((1, blk), lambda h, c: (0, h)),
         pl.BlockSpec((None, None, blk, blk), lambda h, c: (0, h, 0, 0)),
         pl.BlockSpec((None, None, blk, blk), lambda h, c: (0, h, 0, 0))],
        [pl.BlockSpec((tc, blk), col), pl.BlockSpec((tc, blk), col)],
        [jax.ShapeDtypeStruct((seq, d), F32), jax.ShapeDtypeStruct((seq, d), F32)],
        [z, sv, conv_b, wa, wx],
        scratch=[pltpu.VMEM((tc, blk), F32), pltpu.VMEM((tc, blk), F32), pltpu.VMEM((SUBLANES, blk), F32)],
        job=job)


def _lru_rev(z, sv, wa, wx, xc, h_f, seq, d, tc, job=None):
    n_h, blk = wa.shape[1], wa.shape[2]
    n_chunks = seq // tc
    gate_off = 2 * d // blk

    def kern(z_ref, sv_ref, wa_ref, wx_ref, xc_ref, hf_ref, hb_ref, y_ref, a_s, b_s, carry):
        c = pl.program_id(1)
        xc = xc_ref[...]
        _, i, _, a, s = _lru_gates(xc, wa_ref[...], wx_ref[...], sv_ref[SV_BA + 1:SV_BA + 2, :],
                                   sv_ref[SV_BX + 1:SV_BX + 2, :], sv_ref[SV_LAM + 1:SV_LAM + 2, :])
        a_s[...] = a
        b_s[...] = s * (i * xc)

        @pl.when(c == 0)
        def _():
            carry[...] = jnp.zeros_like(carry)

        _scan_tiles(a_s, b_s, hb_ref, carry, tc // SUBLANES, True)
        y_ref[...] = ((hf_ref[...] + hb_ref[...]) * _gelu(z_ref[...])).astype(BF16)

    rev = lambda h, c: (n_chunks - 1 - c, h)
    return _pcall(
        "lru_rev", kern, (n_h, n_chunks),
        [pl.BlockSpec((tc, blk), lambda h, c: (n_chunks - 1 - c, gate_off + h)),
         pl.BlockSpec((16, blk), lambda h, c: (0, h)),
         pl.BlockSpec((None, None, blk, blk), lambda h, c: (1, h, 0, 0)),
         pl.BlockSpec((None, None, blk, blk), lambda h, c: (1, h, 0, 0)),
         pl.BlockSpec((tc, blk), rev), pl.BlockSpec((tc, blk), rev)],
        [pl.BlockSpec((tc, blk), rev), pl.BlockSpec((tc, blk), rev)],
        [jax.ShapeDtypeStruct((seq, d), F32), jax.ShapeDtypeStruct((seq, d), BF16)],
        [z, sv, wa, wx, xc, h_f],
        scratch=[pltpu.VMEM((tc, blk), F32), pltpu.VMEM((tc, blk), F32), pltpu.VMEM((SUBLANES, blk), F32)],
        job=job)


def _merge(y_pool, w_pu, y_lru, w_lu, z, seq, d, tm, tn, job=None):
    n_n = d // tn

    def kern(yp_ref, wp_ref, yl_ref, wl_ref, la_ref, lb_ref, m_ref, pa_ref, pb_ref):
        pa = jnp.dot(yp_ref[...], wp_ref[...], preferred_element_type=F32)
        pb = jnp.dot(yl_ref[...], wl_ref[...], preferred_element_type=F32)
        m_ref[...] = (_sig(la_ref[...]) * pa + _sig(lb_ref[...]) * pb).astype(BF16)
        pa_ref[...] = pa.astype(BF16)
        pb_ref[...] = pb.astype(BF16)

    row = pl.BlockSpec((tm, d), lambda i, j: (i, 0))
    wcol = pl.BlockSpec((d, tn), lambda i, j: (0, j))
    out = pl.BlockSpec((tm, tn), lambda i, j: (i, j))
    sh = jax.ShapeDtypeStruct((seq, d), BF16)
    return _pcall(
        "merge", kern, (seq // tm, n_n),
        [row, wcol, row, wcol,
         pl.BlockSpec((tm, tn), lambda i, j: (i, 3 * n_n + j)),
         pl.BlockSpec((tm, tn), lambda i, j: (i, 4 * n_n + j))],
        [out, out, out], [sh, sh, sh], [y_pool, w_pu, y_lru, w_lu, z, z], job=job)


def _ln_fwd(s, g, b):
    mu = jnp.mean(s, axis=-1, keepdims=True)
    xc = s - mu
    var = jnp.mean(xc * xc, axis=-1, keepdims=True)
    rstd = lax.rsqrt(var + LN_EPS)
    xhat = xc * rstd
    return xhat, rstd, xhat * g + b


def _ln_bwd(dy, xhat, rstd, g):
    dyg = dy * g
    m1 = jnp.mean(dyg, axis=-1, keepdims=True)
    m2 = jnp.mean(dyg * xhat, axis=-1, keepdims=True)
    return rstd * (dyg - m1 - xhat * m2)


def _pool_bwd(z, dy_pool, pw, scale, dz, seq, d, tc):
    n_g = pw.shape[0]
    pg = d // n_g
    n_chunks = seq // tc

    def kern(z_ref, dy_ref, pw_ref, sc_ref, dz_in, dz_ref, dpw_ref, dsc_ref):
        del dz_in
        g, c = pl.program_id(0), pl.program_id(1)
        hw = jnp.left_shift(1, g)
        uext = _ext(z_ref, c, n_chunks, tc, seq)
        dd = _pool_d(uext, g, c, tc, seq).astype(BF16)
        pwv = pw_ref[...]
        q = jnp.dot(dd, pwv, preferred_element_type=F32)
        dyext = _ext(dy_ref, c, n_chunks, tc, seq)

        @pl.when(c == 0)
        def _():
            dsc_ref[...] = jnp.zeros_like(dsc_ref)
            dpw_ref[...] = jnp.zeros_like(dpw_ref)

        dsc_ref[0:1, :] += _colsum(dyext[HALO:HALO + tc] * q)
        dqext = (dyext * sc_ref[...]).astype(BF16)
        dpw_ref[...] += _dot("tn", dd, dqext[HALO:HALO + tc])
        ddext = _dot("nt", dqext, pwv)
        text = c * tc - HALO + lax.broadcasted_iota(jnp.int32, (tc + 2 * HALO, 1), 0)
        v = ddext / jnp.maximum(_win_cnt(text, hw, seq), 1.0)
        dz_ref[...] = (_win_sum(v, g, 1, tc) - ddext[HALO:HALO + tc]).astype(BF16)

    return pl.pallas_call(
        kern, name="pool_bwd", grid=(n_g, n_chunks),
        in_specs=[pl.BlockSpec((seq, pg), lambda g, c: (0, g)),
                  pl.BlockSpec((seq, pg), lambda g, c: (0, g)),
                  pl.BlockSpec((None, pg, pg), lambda g, c: (g, 0, 0)),
                  pl.BlockSpec((1, pg), lambda g, c: (0, g)),
                  pl.BlockSpec(memory_space=pl.ANY)],
        out_specs=[pl.BlockSpec((None, tc, pg), lambda g, c: (2, c, g)),
                   pl.BlockSpec((None, pg, pg), lambda g, c: (g, 0, 0)),
                   pl.BlockSpec((SUBLANES, pg), lambda g, c: (0, g))],
        out_shape=[jax.ShapeDtypeStruct(dz.shape, dz.dtype),
                   jax.ShapeDtypeStruct((n_g, pg, pg), F32),
                   jax.ShapeDtypeStruct((SUBLANES, d), F32)],
        input_output_aliases={4: 0},
        compiler_params=_cparams(("arbitrary", "arbitrary")),
    )(z, dy_pool, pw, scale, dz)


def _lru_bwd(direction, xc, dh, h_dir, sv, wa, wx, dxc_prev, seq, d, tc, job=None):
    reverse = direction == 1
    n_h, blk = wa.shape[1], wa.shape[2]
    n_chunks = seq // tc
    has_prev = dxc_prev is not None

    def kern(*refs):
        xc_ref, dh_ref, h_ref, sv_ref, wa_ref, wx_ref = refs[:6]
        p = 6
        prev_ref = None
        if has_prev:
            prev_ref = refs[p]
            p += 1
        dxc_ref, dwa_ref, dwx_ref, sm_ref, at_s, g_s, carry, acarry = refs[p:p + 8]
        c = pl.program_id(1)
        cr = c if reverse else n_chunks - 1 - c
        c0 = pl.multiple_of(cr * tc, tc)

        @pl.when(c == 0)
        def _():
            carry[...] = jnp.zeros_like(carry)
            acarry[...] = jnp.zeros_like(acarry)
            dwa_ref[...] = jnp.zeros_like(dwa_ref)
            dwx_ref[...] = jnp.zeros_like(dwx_ref)
            sm_ref[...] = jnp.zeros_like(sm_ref)

        xc = xc_ref[...]
        wav, wxv = wa_ref[...], wx_ref[...]
        lam = sv_ref[SV_LAM + direction:SV_LAM + direction + 1, :]
        r, i, k, a, s = _lru_gates(xc, wav, wxv, sv_ref[SV_BA + direction:SV_BA + direction + 1, :],
                                   sv_ref[SV_BX + direction:SV_BX + direction + 1, :], lam)
        rowi = lax.broadcasted_iota(jnp.int32, (tc, blk), 0)
        hbody = h_ref[pl.ds(c0, tc), :]
        if not reverse:
            p0 = pl.multiple_of(jnp.maximum(c0 - SUBLANES, 0), SUBLANES)
            edge = jnp.where(cr > 0, h_ref[pl.ds(p0, SUBLANES), :][SUBLANES - 1:SUBLANES, :], 0.0)
            hprev = jnp.where(rowi == 0, edge, pltpu.roll(hbody, 1, 0))
            at = jnp.where(rowi == tc - 1, acarry[0:1, :], pltpu.roll(a, tc - 1, 0))
        else:
            n0 = pl.multiple_of(jnp.minimum(c0 + tc, seq - SUBLANES), SUBLANES)
            edge = jnp.where(cr < n_chunks - 1, h_ref[pl.ds(n0, SUBLANES), :][0:1, :], 0.0)
            hprev = jnp.where(rowi == tc - 1, edge, pltpu.roll(hbody, tc - 1, 0))
            at = jnp.where(rowi == 0, acarry[0:1, :], pltpu.roll(a, 1, 0))
        at_s[...] = at
        _scan_tiles(at_s, dh_ref, g_s, carry, tc // SUBLANES, not reverse)
        acarry[0:1, :] = a[tc - 1:tc, :] if reverse else a[0:1, :]

        gt = g_s[...]
        da = gt * hprev
        di = gt * s * xc
        dxc = gt * s * i
        ds = gt * (i * xc)
        dl = da * a - ds * (a * a) / s
        dpr = (dl * k) * r * (1.0 - r)
        dpi = di * i * (1.0 - i)
        sm_ref[0:1, :] += _colsum(dpr)
        sm_ref[1:2, :] += _colsum(dpi)
        sm_ref[2:3, :] += _colsum(dl * r) * (LRU_C * _sig(-lam))
        xb, dprb, dpib = xc.astype(BF16), dpr.astype(BF16), dpi.astype(BF16)
        dwa_ref[...] += _dot("tn", xb, dprb)
        dwx_ref[...] += _dot("tn", xb, dpib)
        dxc = dxc + _dot("nt", dprb, wav) + _dot("nt", dpib, wxv)
        if has_prev:
            dxc = dxc + prev_ref[...]
        dxc_ref[...] = dxc

    if reverse:
        chunk = lambda h, c: (c, h)
    else:
        chunk = lambda h, c: (n_chunks - 1 - c, h)
    wspec = pl.BlockSpec((None, None, blk, blk), lambda h, c: (direction, h, 0, 0))
    ins = [xc, dh, h_dir, sv, wa, wx] + ([dxc_prev] if has_prev else [])
    in_specs = [pl.BlockSpec((tc, blk), chunk), pl.BlockSpec((tc, blk), chunk),
                pl.BlockSpec((seq, blk), lambda h, c: (0, h)),
                pl.BlockSpec((16, blk), lambda h, c: (0, h)), wspec, wspec]
    if has_prev:
        in_specs.append(pl.BlockSpec((tc, blk), chunk))
    return _pcall(
        "lru_bwd_%d" % direction, kern, (n_h, n_chunks), in_specs,
        [pl.BlockSpec((tc, blk), chunk),
         pl.BlockSpec((None, blk, blk), lambda h, c: (h, 0, 0)),
         pl.BlockSpec((None, blk, blk), lambda h, c: (h, 0, 0)),
         pl.BlockSpec((SUBLANES, blk), lambda h, c: (0, h))],
        [jax.ShapeDtypeStruct((seq, d), F32),
         jax.ShapeDtypeStruct((n_h, blk, blk), F32),
         jax.ShapeDtypeStruct((n_h, blk, blk), F32),
         jax.ShapeDtypeStruct((SUBLANES, d), F32)],
        ins,
        scratch=[pltpu.VMEM((tc, blk), F32), pltpu.VMEM((tc, blk), F32),
                 pltpu.VMEM((SUBLANES, blk), F32), pltpu.VMEM((SUBLANES, blk), F32)],
        job=job)


def _conv_bwd(z, dxc, sv, dz, seq, d, tc, tcol):
    n_chunks = seq // tc
    lru_off = d // tcol

    def kern(z_ref, dx_ref, sv_ref, dz_in, dz_ref, sm_ref):
        del dz_in
        c = pl.program_id(1)
        uext = _ext(z_ref, c, n_chunks, tc, seq)
        dext = _ext(dx_ref, c, n_chunks, tc, seq)
        dbody = dext[HALO:HALO + tc]

        @pl.when(c == 0)
        def _():
            sm_ref[...] = jnp.zeros_like(sm_ref)

        du = jnp.zeros_like(dbody)
        for k in range(4):
            du = du + _shifted(dext, 2 - k, tc) * sv_ref[SV_CONV + k:SV_CONV + k + 1, :]
            sm_ref[k:k + 1, :] += _colsum(dbody * _shifted(uext, k - 2, tc))
        sm_ref[4:5, :] += _colsum(dbody)
        dz_ref[...] = du.astype(BF16)

    return pl.pallas_call(
        kern, name="conv_bwd", grid=(d // tcol, n_chunks),
        in_specs=[pl.BlockSpec((seq, tcol), lambda j, c: (0, lru_off + j)),
                  pl.BlockSpec((seq, tcol), lambda j, c: (0, j)),
                  pl.BlockSpec((16, tcol), lambda j, c: (0, j)),
                  pl.BlockSpec(memory_space=pl.ANY)],
        out_specs=[pl.BlockSpec((None, tc, tcol), lambda j, c: (3, c, j)),
                   pl.BlockSpec((SUBLANES, tcol), lambda j, c: (0, j))],
        out_shape=[jax.ShapeDtypeStruct(dz.shape, dz.dtype), jax.ShapeDtypeStruct((SUBLANES, d), F32)],
        input_output_aliases={3: 0},
        compiler_params=_cparams(("arbitrary", "arbitrary")),
    )(z, dxc, sv, dz)


BIG = (("w_in", "col"), ("pool_w", "row"), ("lru_wa", "row"), ("lru_wx", "row"), ("w_pool_up", "row"),
       ("w_lru_up", "row"), ("w_out", "row"), ("w_ff1", "col"), ("w_ff2", "row"))


def _shard_view(w, fam):
    if fam == "col":
        return w.reshape(w.shape[-2:])
    return w.reshape((-1,) + w.shape[-2:])


def _full_shape(sv_shape, fam):
    if fam == "col":
        return (sv_shape[0], N_DEV * sv_shape[1])
    return (sv_shape[0], N_DEV * sv_shape[1], sv_shape[2])


def _slot(ref, fam, p, size, part=None):
    if fam == "lead":
        return ref.at[p]
    k, n, span = part or (0, 1, 1)
    unit = size // n
    start = pl.multiple_of(p * size + k * unit, unit)
    if fam == "col":
        return ref.at[:, pl.ds(start, span * unit)]
    return ref.at[:, pl.ds(start, span * unit), :]


def _shard_part(ref, fam, size, part):
    if part is None or fam == "lead":
        return ref
    k, n, span = part
    unit = size // n
    if fam == "col":
        return ref.at[:, pl.ds(k * unit, span * unit)]
    return ref.at[:, pl.ds(k * unit, span * unit), :]


def _shard_extent(shape, fam):
    return shape[1]


def _coords():
    return lax.axis_index("x"), lax.axis_index("y"), lax.axis_index("c")


def _ag_job(shards, fams, parts=None, into=None):
    n = len(shards)
    parts = list(parts) if parts is not None else [None] * n
    into = list(into) if into is not None else [None] * n
    fulls = []
    for s, fam in zip(shards, fams):
        if fam == "lead":
            fulls.append(jax.ShapeDtypeStruct((N_DEV,) + s.shape, s.dtype))
        else:
            fulls.append(jax.ShapeDtypeStruct(_full_shape(s.shape, fam), s.dtype))
    sizes = [1 if fam == "lead" else _shard_extent(s.shape, fam) for s, fam in zip(shards, fams)]
    given = [a for a in range(n) if into[a] is not None]

    def ctx(ins, outs, sems):
        send, recv, loc = sems
        x, y, c = _coords()
        chips = [(1 - x, y), (x, 1 - y), (1 - x, 1 - y)]

        def mine(a):
            return _shard_part(ins[a], fams[a], sizes[a], parts[a])

        def copy(a, k, owner, to, src=None):
            dst = _slot(outs[a], fams[a], owner, sizes[a], parts[a])
            return pltpu.make_async_remote_copy(
                src_ref=dst if src is None else src, dst_ref=dst,
                send_sem=send.at[a, k], recv_sem=recv.at[a, k], device_id=to, device_id_type=MESH)

        def local(a):
            return pltpu.make_async_copy(
                mine(a), _slot(outs[a], fams[a], 4 * x + 2 * y + c, sizes[a], parts[a]), loc.at[a])

        return x, y, c, chips, copy, local, mine

    def start(ins, outs, sems):
        x, y, c, chips, copy, local, mine = ctx(ins, outs, sems)
        me = 4 * x + 2 * y + c
        for a in range(n):
            local(a).start()
            copy(a, 0, me, (x, y, 1 - c), mine(a)).start()
            for j, (cx, cy) in enumerate(chips):
                copy(a, 1 + j, me, (cx, cy, c), mine(a)).start()

    def mid(ins, outs, sems):
        x, y, c, chips, copy, _, _ = ctx(ins, outs, sems)
        for a in range(n):
            for j, (cx, cy) in enumerate(chips):
                owner = 4 * cx + 2 * cy + c
                copy(a, 1 + j, owner, (x, y, c)).wait_recv()
                copy(a, 4 + j, owner, (x, y, 1 - c)).start()

    def finish(ins, outs, sems):
        x, y, c, chips, copy, local, mine = ctx(ins, outs, sems)
        me = 4 * x + 2 * y + c
        for a in range(n):
            copy(a, 0, 4 * x + 2 * y + (1 - c), (x, y, c)).wait_recv()
            for j, (cx, cy) in enumerate(chips):
                copy(a, 4 + j, 4 * cx + 2 * cy + (1 - c), (x, y, c)).wait_recv()
            copy(a, 0, me, (x, y, 1 - c), mine(a)).wait_send()
            for j, (cx, cy) in enumerate(chips):
                copy(a, 1 + j, me, (cx, cy, c), mine(a)).wait_send()
                copy(a, 4 + j, 4 * cx + 2 * cy + c, (x, y, 1 - c)).wait_send()
            local(a).wait()

    sems = [pltpu.SemaphoreType.DMA((n, 7)), pltpu.SemaphoreType.DMA((n, 7)), pltpu.SemaphoreType.DMA((n,))]
    job = _Job(list(shards) + [into[a] for a in given], fulls, sems, start, finish, mid)
    job.alias = {n + i: a for i, a in enumerate(given)}
    return job


def _rs_sibling_job(fulls, fams, sizes):
    n = len(fulls)
    outs = []
    for f, fam, sz in zip(fulls, fams, sizes):
        if fam == "col":
            outs.append(jax.ShapeDtypeStruct((4, f.shape[0], sz), f.dtype))
        else:
            outs.append(jax.ShapeDtypeStruct((4, f.shape[0], sz, f.shape[2]), f.dtype))

    def copies(ins, rcv, sems):
        send, recv = sems
        x, y, c = _coords()
        return [pltpu.make_async_remote_copy(
            src_ref=_slot(ins[a], fams[a], 2 * q + (1 - c), sizes[a]), dst_ref=rcv[a].at[q],
            send_sem=send.at[a, q], recv_sem=recv.at[a, q], device_id=(x, y, 1 - c), device_id_type=MESH)
            for a in range(n) for q in range(4)]

    def start(ins, rcv, sems):
        for cp in copies(ins, rcv, sems):
            cp.start()

    def finish(ins, rcv, sems):
        for cp in copies(ins, rcv, sems):
            cp.wait()

    return _Job(fulls, outs, [pltpu.SemaphoreType.DMA((n, 4)), pltpu.SemaphoreType.DMA((n, 4))], start, finish)


def _rs_chips_job(parts):
    n = len(parts)
    outs = [jax.ShapeDtypeStruct((3,) + p.shape[1:], p.dtype) for p in parts]

    def copies(ins, rcv, sems):
        send, recv = sems
        x, y, c = _coords()
        cps = []
        for a in range(n):
            for r in (1, 2, 3):
                tx, ty = (1 - x) if r & 2 else x, (1 - y) if r & 1 else y
                cps.append(pltpu.make_async_remote_copy(
                    src_ref=ins[a].at[2 * tx + ty], dst_ref=rcv[a].at[r - 1],
                    send_sem=send.at[a, r - 1], recv_sem=recv.at[a, r - 1],
                    device_id=(tx, ty, c), device_id_type=MESH))
        return cps

    def start(ins, rcv, sems):
        for cp in copies(ins, rcv, sems):
            cp.start()

    def finish(ins, rcv, sems):
        for cp in copies(ins, rcv, sems):
            cp.wait()

    return _Job(parts, outs, [pltpu.SemaphoreType.DMA((n, 3)), pltpu.SemaphoreType.DMA((n, 3))], start, finish)


def _tile_rows(rows, cols):
    tr = rows
    while tr * cols > (1 << 18) and tr % (2 * SUBLANES) == 0:
        tr //= 2
    return tr


def _rs_add(name, full, recv_a, fam, size, cidx):
    if fam == "col":
        rows = full.shape[0]
        tr = _tile_rows(rows, size)
        grid = (4, rows // tr)
        f_spec = pl.BlockSpec((tr, size), lambda q, i, cr: (i, 2 * q + cr[0]))
        s_spec = pl.BlockSpec((None, tr, size), lambda q, i, cr: (q, i, 0))
    else:
        nb, cols = full.shape[0], full.shape[2]
        tr = _tile_rows(nb * size, cols) // nb if nb > 1 else _tile_rows(size, cols)
        nt = size // tr
        grid = (4, nt)
        f_spec = pl.BlockSpec((nb, tr, cols), lambda q, i, cr: (0, (2 * q + cr[0]) * nt + i, 0))
        s_spec = pl.BlockSpec((None, nb, tr, cols), lambda q, i, cr: (q, 0, i, 0))

    def kern(c_ref, f_ref, r_ref, o_ref):
        del c_ref
        o_ref[...] = (f_ref[...] + r_ref[...]).astype(BF16)

    return pl.pallas_call(
        kern, name=name,
        grid_spec=pltpu.PrefetchScalarGridSpec(num_scalar_prefetch=1, grid=grid, in_specs=[f_spec, s_spec],
                                               out_specs=s_spec),
        out_shape=jax.ShapeDtypeStruct(recv_a.shape, BF16),
        compiler_params=_cparams(("arbitrary", "arbitrary"), 32),
    )(cidx, full, recv_a)


def _adam(w, g, m, v):
    m2 = ADAM_B1 * m + (1.0 - ADAM_B1) * g
    v2 = ADAM_B2 * v + (1.0 - ADAM_B2) * (g * g)
    m_hat = m2 / (1.0 - ADAM_B1 ** ADAM_STEP)
    v_hat = v2 / (1.0 - ADAM_B2 ** ADAM_STEP)
    delta = -ADAM_LR * (m_hat / (jnp.sqrt(v_hat) + ADAM_EPS) + ADAM_WD * w)
    return delta, m2, v2


def _rs_final_adam(name, parts, recv_b, w, m, v, fam, qidx):
    shp = w.shape
    pieces = parts if isinstance(parts, (list, tuple)) else [parts]
    recvs = recv_b if isinstance(recv_b, (list, tuple)) else [recv_b]
    n_p = len(pieces)
    first_blk = [0] * n_p
    if fam == "col":
        rows, cols = shp
        tr = _tile_rows(min(p.shape[1] for p in pieces), cols)
        per = [p.shape[1] // tr for p in pieces]
        for h in range(1, n_p):
            first_blk[h] = first_blk[h - 1] + per[h - 1]
        grid = (rows // tr,)
        w_spec = pl.BlockSpec((tr, cols), lambda i, qr: (i, 0))

        def piece_row(i, h):
            return jnp.clip(i - first_blk[h], 0, per[h] - 1)

        p_specs = [pl.BlockSpec((None, tr, cols), lambda i, qr, h=h: (qr[0], piece_row(i, h), 0))
                   for h in range(n_p)]
        r_specs = [pl.BlockSpec((3, tr, cols), lambda i, qr, h=h: (0, piece_row(i, h), 0)) for h in range(n_p)]
    else:
        assert n_p == 1
        nb, rows, cols = shp
        tr = _tile_rows(rows, cols)
        nt = rows // tr
        grid = (nb * nt,)
        w_spec = pl.BlockSpec((None, tr, cols), lambda i, qr: (i // nt, i % nt, 0))
        p_specs = [pl.BlockSpec((None, None, tr, cols), lambda i, qr: (qr[0], i // nt, i % nt, 0))]
        r_specs = [pl.BlockSpec((3, None, tr, cols), lambda i, qr: (0, i // nt, i % nt, 0))]

    def kern(*refs):
        p_refs, r_refs = refs[1:1 + n_p], refs[1 + n_p:1 + 2 * n_p]
        w_ref, m_ref, v_ref, g_out, d_out, m_out, v_out = refs[1 + 2 * n_p:]

        def total(h):
            p_ref, r_ref = p_refs[h], r_refs[h]
            return ((p_ref[...].astype(F32) + r_ref[0].astype(F32)) + r_ref[1].astype(F32)) + r_ref[2].astype(F32)

        g = total(0)
        for h in range(1, n_p):
            g = jnp.where(pl.program_id(0) >= first_blk[h], total(h), g)
        delta, m2, v2 = _adam(w_ref[...], g, m_ref[...], v_ref[...])
        g_out[...] = g
        d_out[...] = delta
        m_out[...] = m2
        v_out[...] = v2

    sh = jax.ShapeDtypeStruct(shp, F32)
    return pl.pallas_call(
        kern, name=name,
        grid_spec=pltpu.PrefetchScalarGridSpec(
            num_scalar_prefetch=1, grid=grid, in_specs=p_specs + r_specs + [w_spec, w_spec, w_spec],
            out_specs=[w_spec] * 4),
        out_shape=[sh] * 4,
        compiler_params=_cparams(("arbitrary",), 32),
    )(qidx, *pieces, *recvs, w, m, v)


def _sum8(name, parts):
    def kern(p_ref, o_ref):
        acc = p_ref[0]
        for p in range(1, N_DEV):
            acc = acc + p_ref[p]
        o_ref[...] = acc

    return pl.pallas_call(
        kern, name=name, out_shape=jax.ShapeDtypeStruct(parts.shape[1:], F32),
        compiler_params=pltpu.CompilerParams(vmem_limit_bytes=32 << 20),
    )(parts)


def _adam_small(name, w, g, m, v):
    def kern(w_ref, g_ref, m_ref, v_ref, d_out, m_out, v_out):
        delta, m2, v2 = _adam(w_ref[...], g_ref[...], m_ref[...], v_ref[...])
        d_out[...] = delta
        m_out[...] = m2
        v_out[...] = v2

    sh = jax.ShapeDtypeStruct(w.shape, F32)
    return pl.pallas_call(kern, name=name, out_shape=[sh] * 3)(w, g, m, v)


class _NoComm:
    def __init__(self):
        self.grads = {}

    def job(self, host):
        return None

    def after(self, host, job):
        pass

    def grad(self, name, g):
        self.grads[name] = g


def _local_step(x, target, wts, vec, comm=None):
    comm = comm or _NoComm()
    seq, d = x.shape
    sv = vec["sv"]
    ff = vec["b_ff1"].shape[1]
    n_in = 5 * d

    def hosted(host, call):
        job = comm.job(host)
        res = call(job)
        comm.after(host, job)
        return res

    tc = min(512, seq)
    t1k, t512, t256 = min(1024, seq), min(512, seq), min(256, seq)
    n512 = min(512, d)
    tkd = d
    tkf = min(1024, ff)
    sub_rows = min(64, seq)

    x_bf = x.astype(BF16)
    full = lambda i, j, k: (0, 0)

    def epi_store(acc, i, j, ex, outs):
        outs[0][...] = acc

    n_pass = W_IN_PASSES
    piece = n_in // (N_DEV * n_pass)
    tz = min(2048, seq)
    z = None
    for k in range(n_pass):
        w_piece = wts["w_in_piece_%d" % k]
        prev = [] if z is None else [(z, (None, None))]
        (z,) = hosted("z_proj_%d" % k, lambda job: _mm(
            "z_proj_%d" % k, "nn", (seq // tz, N_DEV, 1),
            x_bf, ((tz, d), lambda i, j, kk: (i, 0)), w_piece, ((d, piece), lambda i, j, kk: (0, j)),
            prev, [jax.ShapeDtypeStruct((seq, n_in), F32)],
            [((tz, piece), lambda i, j, kk, k=k: (i, n_pass * j + k))], epi_store,
            aliases={2: 0} if prev else None, job=job))

    pool_w, wa, wx = wts["pool_w"], wts["lru_wa"], wts["lru_wx"]
    blk = wa.shape[2]
    y_pool = hosted("pool_fwd", lambda job: _pool_fwd(z, pool_w, vec["pool_scale"], seq, d, tc, job=job))
    xc, h_f = hosted("lru_fwd", lambda job: _lru_fwd(z, sv, vec["conv_b"], wa, wx, seq, d, tc, job=job))
    h_b, y_lru = hosted("lru_rev", lambda job: _lru_rev(z, sv, wa, wx, xc, h_f, seq, d, tc, job=job))
    w_pu, w_lu = wts["w_pool_up"], wts["w_lru_up"]
    m_bf, p_a, p_b = hosted("merge", lambda job: _merge(y_pool, w_pu, y_lru, w_lu, z, seq, d, t512, n512, job=job))
    w_out = wts["w_out"]

    def epi_ln1(acc, i, j, ex, outs):
        x_ref, bo, g1, b1 = ex
        s1 = DN_ALPHA * x_ref[...] + (acc + bo[...])
        xhat, rstd, x1 = _ln_fwd(s1, g1[...], b1[...])
        outs[0][...] = xhat
        outs[1][...] = x1.astype(BF16)
        outs[2][...] = rstd

    rowd = lambda t: ((t, d), lambda i, j, k: (i, 0))
    vecd = ((1, d), full)
    xhat1, x1_bf, rstd1 = hosted("out_ln1", lambda job: _mm(
        "out_ln1", "nn", (seq // t256, 1, 1), m_bf, rowd(t256), w_out, ((d, d), full),
        [(x, rowd(t256)), (vec["b_out"], vecd), (vec["ln1_g"], vecd), (vec["ln1_b"], vecd)],
        [jax.ShapeDtypeStruct((seq, d), F32), jax.ShapeDtypeStruct((seq, d), BF16),
         jax.ShapeDtypeStruct((seq, 1), F32)],
        [rowd(t256), rowd(t256), ((t256, 1), lambda i, j, k: (i, 0))], epi_ln1, job=job))
    w1 = wts["w_ff1"]

    def epi_ff1(acc, i, j, ex, outs):
        r = jnp.maximum(acc + ex[0][...], 0.0)
        outs[0][...] = r.astype(BF16)
        outs[1][...] = (r * r).astype(BF16)

    tile_f = ((t1k, n512), lambda i, j, k: (i, j))
    relu_h, hdn = hosted("ff1", lambda job: _mm(
        "ff1", "nn", (seq // t1k, ff // n512, 1), x1_bf, rowd(t1k), w1, ((d, n512), lambda i, j, k: (0, j)),
        [(vec["b_ff1"], ((1, n512), lambda i, j, k: (0, j)))],
        [jax.ShapeDtypeStruct((seq, ff), BF16)] * 2, [tile_f, tile_f], epi_ff1, job=job))
    w2 = wts["w_ff2"]

    def epi_ln2(acc, i, j, ex, outs):
        xh1, tgt, g1, b1, bf2, g2, b2 = ex
        ds_ref, dsb_ref, sm_ref, loss_ref = outs
        x1 = xh1[...] * g1[...] + b1[...]
        s2 = DN_ALPHA * x1 + (acc + bf2[...])
        xhat, rstd, y = _ln_fwd(s2, g2[...], b2[...])
        e = y - tgt[...]
        part = 0.5 * jnp.sum(jnp.mean(e * e, axis=-1, keepdims=True))
        dy = e * (1.0 / d)
        ds2 = _ln_bwd(dy, xhat, rstd, g2[...])
        ds_ref[...] = ds2
        dsb_ref[...] = ds2.astype(BF16)
        sm_ref[0:1, :] += _colsum(dy * xhat)
        sm_ref[1:2, :] += _colsum(dy)
        sm_ref[2:3, :] += _colsum(ds2)
        loss_ref[...] += jnp.full(loss_ref.shape, part, F32)

    def zero_tail(n_tail):
        def init(i, j, outs):
            @pl.when(i == 0)
            def _():
                for o in outs[-n_tail:]:
                    o[...] = jnp.zeros_like(o)
        return init

    ds2, ds2_bf, sm_ln2, loss_blk = hosted("ff2_ln2", lambda job: _mm(
        "ff2_ln2", "nn", (seq // t512, 1, ff // tkf), hdn, ((t512, tkf), lambda i, j, k: (i, k)),
        w2, ((tkf, d), lambda i, j, k: (k, 0)),
        [(xhat1, rowd(t512)), (target, rowd(t512)), (vec["ln1_g"], vecd), (vec["ln1_b"], vecd),
         (vec["b_ff2"], vecd), (vec["ln2_g"], vecd), (vec["ln2_b"], vecd)],
        [jax.ShapeDtypeStruct((seq, d), F32), jax.ShapeDtypeStruct((seq, d), BF16),
         jax.ShapeDtypeStruct((SUBLANES, d), F32), jax.ShapeDtypeStruct((SUBLANES, 128), F32)],
        [rowd(t512), rowd(t512), ((SUBLANES, d), full), ((SUBLANES, 128), full)],
        epi_ln2, acc_shape=(t512, d), epi_init=zero_tail(2), sub=sub_rows, job=job))

    tkw = min(2048, seq)

    def dw(name, wname, a, b, m_dim, n_dim, b_spec=None, row0=0):
        tm, tn = min(1024, m_dim), min(1024, n_dim)
        while m_dim % tm or row0 % tm:
            tm //= 2
        b_spec = b_spec or ((tkw, tn), lambda i, j, k: (k, j))
        i0 = row0 // tm
        (out,) = hosted(name, lambda job: _mm(
            name, "tn", (m_dim // tm, n_dim // tn, seq // tkw),
            a, ((tkw, tm), lambda i, j, k: (k, i0 + i)), b, b_spec, [],
            [jax.ShapeDtypeStruct((m_dim, n_dim), F32)], [((tm, tn), lambda i, j, k: (i, j))],
            epi_store if seq == tkw else None, job=job))
        comm.grad(wname, out)

    dw("dw_ff2", "w_ff2", hdn, ds2_bf, ff, d)

    def epi_dpre(acc, i, j, ex, outs):
        dpre = acc * (2.0 * ex[0][...].astype(F32))
        outs[0][...] = dpre.astype(BF16)

        @pl.when(i == 0)
        def _():
            outs[1][...] = jnp.zeros_like(outs[1])

        outs[1][0:1, :] += _colsum(dpre)

    dpre, sm_bff1 = hosted("dhdn", lambda job: _mm(
        "dhdn", "nt", (seq // t1k, ff // n512, 1), ds2_bf, rowd(t1k), w2, ((n512, d), lambda i, j, k: (j, 0)),
        [(relu_h, tile_f)],
        [jax.ShapeDtypeStruct((seq, ff), BF16), jax.ShapeDtypeStruct((SUBLANES, ff), F32)],
        [tile_f, ((SUBLANES, n512), lambda i, j, k: (0, j))], epi_dpre, order="ji", job=job))

    dw("dw_ff1", "w_ff1", x1_bf, dpre, d, ff)

    def epi_ln1b(acc, i, j, ex, outs):
        ds2_ref, xh1, rs1, g1 = ex
        ds_ref, dsb_ref, sm_ref = outs
        dy1 = acc + DN_ALPHA * ds2_ref[...]
        xhat = xh1[...]
        ds1 = _ln_bwd(dy1, xhat, rs1[...], g1[...])
        ds_ref[...] = ds1
        dsb_ref[...] = ds1.astype(BF16)
        sm_ref[0:1, :] += _colsum(dy1 * xhat)
        sm_ref[1:2, :] += _colsum(dy1)
        sm_ref[2:3, :] += _colsum(ds1)

    ds1, ds1_bf, sm_ln1 = hosted("dx1_ln1", lambda job: _mm(
        "dx1_ln1", "nt", (seq // t512, 1, ff // tkf), dpre, ((t512, tkf), lambda i, j, k: (i, k)),
        w1, ((d, tkf), lambda i, j, k: (0, k)),
        [(ds2, rowd(t512)), (xhat1, rowd(t512)), (rstd1, ((t512, 1), lambda i, j, k: (i, 0))),
         (vec["ln1_g"], vecd)],
        [jax.ShapeDtypeStruct((seq, d), F32), jax.ShapeDtypeStruct((seq, d), BF16),
         jax.ShapeDtypeStruct((SUBLANES, d), F32)],
        [rowd(t512), rowd(t512), ((SUBLANES, d), full)], epi_ln1b, acc_shape=(t512, d),
        epi_init=zero_tail(1), sub=sub_rows, job=job))

    dw("dw_out", "w_out", m_bf, ds1_bf, d, d)
    n_n = d // n512
    tile_d = ((t512, n512), lambda i, j, k: (i, j))

    def epi_dm(acc, i, j, ex, outs):
        la, lb, pa, pb = ex
        ga, gb = _sig(la[...]), _sig(lb[...])
        outs[0][...] = (acc * ga).astype(BF16)
        outs[1][...] = (acc * gb).astype(BF16)
        outs[2][0] = (acc * pa[...].astype(F32) * ga * (1.0 - ga)).astype(BF16)
        outs[2][1] = (acc * pb[...].astype(F32) * gb * (1.0 - gb)).astype(BF16)

    dp_a, dp_b, dz = _mm(
        "dm", "nt", (seq // t512, n_n, 1), ds1_bf, rowd(t512), w_out, ((n512, d), lambda i, j, k: (j, 0)),
        [(z, ((t512, n512), lambda i, j, k: (i, 3 * n_n + j))),
         (z, ((t512, n512), lambda i, j, k: (i, 4 * n_n + j))), (p_a, tile_d), (p_b, tile_d)],
        [jax.ShapeDtypeStruct((seq, d), BF16), jax.ShapeDtypeStruct((seq, d), BF16),
         jax.ShapeDtypeStruct((5, seq, d), BF16)],
        [tile_d, tile_d, ((2, t512, n512), lambda i, j, k: (0, i, j))], epi_dm)

    dw("dw_pool_up", "w_pool_up", y_pool, dp_a, d, d)
    dw("dw_lru_up", "w_lru_up", y_lru, dp_b, d, d)

    def epi_bf(acc, i, j, ex, outs):
        outs[0][...] = acc.astype(BF16)

    (dy_pool,) = _mm("dy_pool", "nt", (seq // t512, n_n, 1), dp_a, rowd(t512), w_pu,
                     ((n512, d), lambda i, j, k: (j, 0)), [],
                     [jax.ShapeDtypeStruct((seq, d), BF16)], [tile_d], epi_bf)

    def epi_dylru(acc, i, j, ex, outs):
        hf, hb, ug, _ = ex
        u = ug[...]
        outs[0][...] = acc * _gelu(u)
        outs[1][...] = (acc * (hf[...] + hb[...]) * _gelu_grad(u)).astype(BF16)

    dz_in = dz
    dh, dz = hosted("dy_lru", lambda job: _mm(
        "dy_lru", "nt", (seq // t512, n_n, 1), dp_b, rowd(t512), w_lu, ((n512, d), lambda i, j, k: (j, 0)),
        [(h_f, tile_d), (h_b, tile_d), (z, ((t512, n512), lambda i, j, k: (i, 2 * n_n + j))),
         (dz_in, (None, None))],
        [jax.ShapeDtypeStruct((seq, d), F32), jax.ShapeDtypeStruct(dz_in.shape, BF16)],
        [tile_d, ((None, t512, n512), lambda i, j, k: (4, i, j))], epi_dylru, aliases={5: 1}, job=job))

    dz, g_pw, sm_pool = _pool_bwd(z, dy_pool, pool_w, vec["pool_scale"], dz, seq, d, tc)
    comm.grad("pool_w", g_pw)
    dxc0, g_wa0, g_wx0, sm_l0 = hosted("lru_bwd_0", lambda job: _lru_bwd(
        0, xc, dh, h_f, sv, wa, wx, None, seq, d, tc, job=job))
    dxc, g_wa1, g_wx1, sm_l1 = hosted("lru_bwd_1", lambda job: _lru_bwd(
        1, xc, dh, h_b, sv, wa, wx, dxc0, seq, d, tc, job=job))
    comm.grad("lru_wa", jnp.concatenate([g_wa0, g_wa1], axis=0))
    comm.grad("lru_wx", jnp.concatenate([g_wx0, g_wx1], axis=0))
    dz, sm_conv = _conv_bwd(z, dxc, sv, dz, seq, d, tc, blk)

    tnw = min(1024, d)
    per_seg = d // tnw
    seg_spec = ((None, tkw, tnw), lambda i, j, k: ((j // per_seg + 2) % 5, k, j % per_seg))
    lo_rows = 3 * d // 4
    dw("dw_in_lo", "w_in_lo", x_bf, dz, lo_rows, n_in, b_spec=seg_spec)
    dw("dw_in_hi", "w_in_hi", x_bf, dz, d - lo_rows, n_in, b_spec=seg_spec, row0=lo_rows)

    nk = d // tkd

    def epi_dx(acc, i, j, ex, outs):
        outs[0][...] = acc + DN_ALPHA * ex[0][...]

    (grad_x,) = hosted("dx", lambda job: _mm(
        "dx", "nt", (seq // t512, 1, n_in // tkd), dz,
        ((None, t512, tkd), lambda i, j, k: ((k // nk + 2) % 5, i, k % nk)),
        wts["w_in"], ((d, tkd), lambda i, j, k: (0, k)), [(ds1, rowd(t512))],
        [jax.ShapeDtypeStruct((seq, d), F32)], [rowd(t512)], epi_dx, acc_shape=(t512, d), job=job))

    small = {"ln2": sm_ln2, "b_ff1": sm_bff1, "ln1": sm_ln1, "pool": sm_pool, "lru0": sm_l0, "lru1": sm_l1,
             "conv": sm_conv}
    return loss_blk[0, 0], grad_x, small


REP = ("pool_scale", "conv_b", "b_out", "ln1_g", "ln1_b", "b_ff2", "ln2_g", "ln2_b")
SHARDED_SMALL = (("conv_w", 4), ("lru_ba", 2), ("lru_bx", 2), ("lru_lambda", 2))
WEIGHT_ORDER = ("w_in", "pool_w", "pool_scale", "conv_w", "conv_b", "lru_wa", "lru_ba", "lru_wx", "lru_bx",
                "lru_lambda", "w_pool_up", "w_lru_up", "w_out", "b_out", "ln1_g", "ln1_b", "w_ff1", "b_ff1",
                "w_ff2", "b_ff2", "ln2_g", "ln2_b")


def _pad_rows(a, rows):
    return jnp.concatenate([a, jnp.zeros((rows - a.shape[0], a.shape[1]), a.dtype)], axis=0)


def kernel(x, w_in, pool_w, pool_scale, conv_w, conv_b, lru_wa, lru_ba, lru_wx, lru_bx, lru_lambda, w_pool_up, w_lru_up, w_out, b_out, ln1_g, ln1_b, w_ff1, b_ff1, w_ff2, b_ff2, ln2_g, ln2_b, loss_target, m_w_in, m_pool_w, m_pool_scale, m_conv_w, m_conv_b, m_lru_wa, m_lru_ba, m_lru_wx, m_lru_bx, m_lru_lambda, m_w_pool_up, m_w_lru_up, m_w_out, m_b_out, m_ln1_g, m_ln1_b, m_w_ff1, m_b_ff1, m_w_ff2, m_b_ff2, m_ln2_g, m_ln2_b, v_w_in, v_pool_w, v_pool_scale, v_conv_w, v_conv_b, v_lru_wa, v_lru_ba, v_lru_wx, v_lru_bx, v_lru_lambda, v_w_pool_up, v_w_lru_up, v_w_out, v_b_out, v_ln1_g, v_ln1_b, v_w_ff1, v_b_ff1, v_w_ff2, v_b_ff2, v_ln2_g, v_ln2_b):
    args = dict(locals())
    w = {n: args[n] for n in WEIGHT_ORDER}
    mom = {n: args["m_" + n] for n in WEIGHT_ORDER}
    var = {n: args["v_" + n] for n in WEIGHT_ORDER}
    seq, d = x.shape[1], x.shape[2]
    n_heads, blk = lru_wa.shape[2], lru_wa.shape[4]
    n_groups = pool_w.shape[1]
    ff = b_ff1.shape[1]
    cx, cy, cc = _coords()
    me = 4 * cx + 2 * cy + cc
    cidx = jnp.reshape(cc, (1,)).astype(jnp.int32)
    qidx = jnp.reshape(2 * cx + cy, (1,)).astype(jnp.int32)

    fam_of = dict(BIG)
    sviews = {n: _shard_view(w[n], fam) for n, fam in BIG}
    size_of = {n: sviews[n].shape[1] for n, _ in BIG}
    for half in ("w_in_lo", "w_in_hi"):
        fam_of[half], size_of[half] = fam_of["w_in"], size_of["w_in"]
    wts = {}

    def take_gathered(names, arrays):
        for n, g in zip(names, arrays):
            if n == "pool_w":
                g = g.reshape(n_groups, d // n_groups, d // n_groups)
            elif n in ("lru_wa", "lru_wx"):
                g = g.reshape(2, n_heads, blk, blk)
            elif fam_of[n] == "row":
                g = g.reshape(g.shape[1:])
            wts[n] = g

    shard_bf = {n: sviews[n].astype(BF16) for n, _ in BIG}
    piece = sviews["w_in"].shape[1] // W_IN_PASSES
    for k in range(W_IN_PASSES):
        name = "w_in_piece_%d" % k
        shard_bf[name] = shard_bf["w_in"][:, k * piece:(k + 1) * piece]
        fam_of[name] = "col"
    partial = {}

    def gather_job(items, extra=()):
        return _ag_job([shard_bf[n] for n, _ in items] + [e for e, _ in extra],
                       [fam_of[n] for n, _ in items] + [f for _, f in extra],
                       parts=[p for _, p in items] + [None] * len(extra),
                       into=[partial.get(n) if p else None for n, p in items] + [None] * len(extra))

    def take_pieces(items, arrays):
        for (n, p), g in zip(items, arrays):
            if p is None or p[0] + p[2] == p[1]:
                partial.pop(n, None)
                take_gathered([n], [g])
            else:
                partial[n] = g

    class Plan:
        gather = {"dw_ff2": (("w_in", (0, 2, 1)),), "dhdn": (("w_in", (1, 2, 1)),)}
        late_mid = ("z_proj_0", "z_proj_1", "z_proj_2", "z_proj_3", "z_proj_4", "pool_fwd", "lru_fwd", "lru_rev",
                    "out_ln1")
        to_sibling = {"dhdn": ("w_ff2",), "dx1_ln1": ("w_ff1",), "lru_bwd_0": ("w_out", "w_pool_up", "w_lru_up"),
                      "dw_in_hi": ("w_in_lo",), "dx": ("w_in_hi", "pool_w", "lru_wa", "lru_wx")}
        to_chips = {"dx1_ln1": ("w_ff2",), "lru_bwd_0": ("w_ff1",), "lru_bwd_1": ("w_out", "w_pool_up", "w_lru_up"),
                    "dx": ("w_in_lo",)}

        def __init__(self):
            self.grads, self.parts, self.recv_b = {}, {}, {}

        def grad(self, name, g):
            self.grads[name] = g if fam_of[name] == "col" else g.reshape((-1,) + g.shape[-2:])

        def job(self, host):
            jobs = []
            if host in self.gather:
                jobs.append(gather_job(self.gather[host]))
                jobs[-1].mid_frac = 1.0 if host in self.late_mid else 0.9
            if host in self.to_sibling:
                names = self.to_sibling[host]
                jobs.append(_rs_sibling_job([self.grads[n] for n in names], [fam_of[n] for n in names],
                                            [size_of[n] for n in names]))
            if host in self.to_chips:
                jobs.append(_rs_chips_job([self.parts[n] for n in self.to_chips[host]]))
            return _join_jobs(jobs)

        def after(self, host, job):
            results = iter(_job_results(job))
            if host in self.gather:
                take_pieces(self.gather[host], next(results))
            if host in self.to_sibling:
                for n, r in zip(self.to_sibling[host], next(results)):
                    self.parts[n] = _rs_add("rs_add_" + n, self.grads[n], r, fam_of[n], size_of[n], cidx)
            if host in self.to_chips:
                self.recv_b.update(zip(self.to_chips[host], next(results)))

    first = (("w_in_piece_0", None),)
    sv_shard = _pad_rows(jnp.concatenate([w[n].reshape(r, -1) for n, r in SHARDED_SMALL], axis=0), 16)
    gathered = _sequencer_job("sq_gather_0", gather_job(first, [(sv_shard, "col")]), 0)
    take_pieces(first, gathered[:-1])
    vec = {n: w[n] for n in REP}
    vec["b_ff1"] = b_ff1
    vec["sv"] = gathered[-1]
    queue = [(("w_in_piece_%d" % k, None),) for k in range(1, W_IN_PASSES)]
    queue += [(("pool_w", None), ("lru_wa", None), ("lru_wx", None)), (("w_pool_up", None), ("w_lru_up", None)),
              (("w_out", None),), (("w_ff1", None),), (("w_ff2", None),)]
    for gi, items in enumerate(queue):
        take_pieces(items, _sequencer_job("sq_gather_%d" % (gi + 1), gather_job(items), (gi + 1) % 2))

    plan = Plan()
    loss_part, grad_x, small = _local_step(x.reshape(seq, d), loss_target.reshape(seq, d), wts, vec, plan)
    loss = lax.psum(loss_part, AXES)

    tail = Plan.to_sibling["dx"]
    plan.recv_b.update(zip(tail, _run_job("rs_chips_tail", _rs_chips_job([plan.parts[n] for n in tail]))))
    out_g, out_d, out_m, out_v = {}, {}, {}, {}
    for n, fam in BIG:
        halves = [n + "_lo", n + "_hi"] if n == "w_in" else [n]
        res = _rs_final_adam("adam_" + n, [plan.parts[h] for h in halves], [plan.recv_b[h] for h in halves],
                             sviews[n], _shard_view(mom[n], fam), _shard_view(var[n], fam), fam, qidx)
        out_g[n], out_d[n], out_m[n], out_v[n] = [r.reshape(w[n].shape) for r in res]

    rows = [small["pool"][0:1], small["conv"][4:5], small["ln1"][2:3], small["ln1"][0:1], small["ln1"][1:2],
            small["ln2"][2:3], small["ln2"][0:1], small["ln2"][1:2], small["b_ff1"][0:1].reshape(ff // d, d),
            small["conv"][0:4], small["lru0"][0:1], small["lru1"][0:1], small["lru0"][1:2], small["lru1"][1:2],
            small["lru0"][2:3], small["lru1"][2:3]]
    n_rep = len(REP) + ff // d
    n_rows = n_rep + sum(r for _, r in SHARDED_SMALL)
    pad_rows = -(-n_rows // SUBLANES) * SUBLANES
    packed = _pad_rows(jnp.concatenate(rows, axis=0), pad_rows)
    (all_small,) = _run_job("ag_small", _ag_job([packed], ["lead"]))
    g_small = _sum8("sum_small", all_small)

    def pack_rep(t):
        return jnp.concatenate([t[n] for n in REP] + [t["b_ff1"].reshape(ff // d, d)], axis=0)

    def pack_sh(t):
        return jnp.concatenate([t[n].reshape(r, -1) for n, r in SHARDED_SMALL], axis=0)

    g_rep = g_small[:n_rep]
    cs = d // N_DEV
    g_sh = lax.dynamic_slice_in_dim(g_small[n_rep:n_rows], me * cs, cs, axis=1)
    d_rep, m_rep, v_rep = _adam_small("adam_rep", pack_rep(w), g_rep, pack_rep(mom), pack_rep(var))
    d_sh, m_sh, v_sh = _adam_small("adam_sharded", pack_sh(w), g_sh, pack_sh(mom), pack_sh(var))

    def unpack(rep_t, sh_t, dst):
        for i, n in enumerate(REP):
            dst[n] = rep_t[i:i + 1].reshape(w[n].shape)
        dst["b_ff1"] = rep_t[len(REP):n_rep].reshape(w["b_ff1"].shape)
        r0 = 0
        for n, r in SHARDED_SMALL:
            dst[n] = sh_t[r0:r0 + r].reshape(w[n].shape)
            r0 += r

    unpack(g_rep, g_sh, out_g)
    unpack(d_rep, d_sh, out_d)
    unpack(m_rep, m_sh, out_m)
    unpack(v_rep, v_sh, out_v)

    outs = [loss, grad_x.reshape(x.shape)]
    for t in (out_g, out_d, out_m, out_v):
        outs += [t[n] for n in WEIGHT_ORDER]
    return tuple(outs)
```

```python
import functools

import jax
import jax.numpy as jnp
from jax import lax
from jax.experimental import pallas as pl
from jax.experimental.pallas import tpu as pltpu
from jax.experimental.pallas import tpu_sc as plsc

F32 = jnp.float32
BF16 = jnp.bfloat16
MESH = pl.DeviceIdType.MESH
AXES = ("x", "y", "c")
N_DEV = 8

DN_ALPHA = 2.0 ** 0.25
LN_EPS = 1e-5
LRU_C = 8.0
ADAM_LR = 0.001
ADAM_B1 = 0.9
ADAM_B2 = 0.999
ADAM_EPS = 1e-08
ADAM_WD = 0.01
ADAM_STEP = 10
GELU_C = 0.7978845608028654
GELU_K = 0.044715

W_IN_PASSES = 5
HALO = 16
SUBLANES = 8
VMEM_MB = 56


def _cparams(sem, vmem_mb=VMEM_MB):
    return pltpu.CompilerParams(dimension_semantics=sem, vmem_limit_bytes=vmem_mb << 20)


HBM_SPEC = pl.BlockSpec(memory_space=pl.ANY)


class _Job:
    def __init__(self, ins, outs, sems, start, finish, mid=None):
        self.ins, self.outs, self.sems = list(ins), list(outs), list(sems)
        self.start, self.finish, self.mid = start, finish, mid
        self.alias = {}
        self.mid_frac = 0.75
        self.results = None


def _join_jobs(jobs):
    jobs = [j for j in jobs if j is not None]
    if len(jobs) <= 1:
        return jobs[0] if jobs else None

    def split(refs, counts):
        out, p = [], 0
        for n in counts:
            out.append(refs[p:p + n])
            p += n
        return out

    def phase(which):
        def run(ins, outs, sems):
            parts = zip(jobs, split(ins, [len(j.ins) for j in jobs]), split(outs, [len(j.outs) for j in jobs]),
                        split(sems, [len(j.sems) for j in jobs]))
            for j, ji, jo, js in parts:
                fn = getattr(j, which)
                if fn is not None:
                    fn(ji, jo, js)
        return run

    joined = _Job(sum((j.ins for j in jobs), []), sum((j.outs for j in jobs), []), sum((j.sems for j in jobs), []),
                  phase("start"), phase("finish"), phase("mid") if any(j.mid for j in jobs) else None)
    joined.parts = jobs
    joined.mid_frac = max(j.mid_frac for j in jobs if j.mid is not None) if any(j.mid for j in jobs) else 0.75
    i0 = o0 = 0
    for j in jobs:
        joined.alias.update({i0 + i: o0 + o for i, o in j.alias.items()})
        i0 += len(j.ins)
        o0 += len(j.outs)
    return joined


def _job_results(job):
    if job is None:
        return []
    parts = getattr(job, "parts", None)
    if parts is None:
        return [job.results]
    out, p = [], 0
    for j in parts:
        out.append(job.results[p:p + len(j.outs)])
        p += len(j.outs)
    return out


def _pcall(name, body, grid, in_specs, out_specs, out_shape, inputs, scratch=(), aliases=None,
           vmem_mb=VMEM_MB, job=None):
    in_specs, out_specs, out_shape, scratch = list(in_specs), list(out_specs), list(out_shape), list(scratch)
    params = _cparams(("arbitrary",) * len(grid), vmem_mb)
    if job is None:
        return pl.pallas_call(body, name=name, grid=grid, in_specs=in_specs, out_specs=out_specs,
                              out_shape=out_shape, scratch_shapes=scratch,
                              input_output_aliases=aliases or {}, compiler_params=params)(*inputs)
    n_in, n_out, n_scr = len(inputs), len(out_shape), len(scratch)
    ji, jo = len(job.ins), len(job.outs)
    total = 1
    for g in grid:
        total *= g
    mid_step = int(job.mid_frac * total) if total >= 4 and job.mid_frac < 1.0 else None

    def wrapped(*refs):
        p = 0
        ins = refs[p:p + n_in]
        p += n_in
        jins = refs[p:p + ji]
        p += ji
        outs = refs[p:p + n_out]
        p += n_out
        jouts = refs[p:p + jo]
        p += jo
        scr = refs[p:p + n_scr]
        sems = refs[p + n_scr:]
        step = pl.program_id(0)
        for ax in range(1, len(grid)):
            step = step * grid[ax] + pl.program_id(ax)

        @pl.when(step == 0)
        def _():
            job.start(jins, jouts, sems)

        if job.mid is not None and mid_step is not None:
            @pl.when(step == mid_step)
            def _():
                job.mid(jins, jouts, sems)

        body(*ins, *outs, *scr)

        @pl.when(step == total - 1)
        def _():
            if job.mid is not None and mid_step is None:
                job.mid(jins, jouts, sems)
            job.finish(jins, jouts, sems)

    all_aliases = dict(aliases or {})
    all_aliases.update({n_in + i: n_out + o for i, o in job.alias.items()})
    res = pl.pallas_call(
        wrapped, name=name, grid=grid, in_specs=in_specs + [HBM_SPEC] * ji,
        out_specs=out_specs + [HBM_SPEC] * jo, out_shape=out_shape + job.outs,
        scratch_shapes=scratch + job.sems, input_output_aliases=all_aliases, compiler_params=params,
    )(*inputs, *job.ins)
    job.results = list(res[n_out:])
    return list(res[:n_out])


def _run_job(name, job):
    ji, jo = len(job.ins), len(job.outs)

    def body(*refs):
        jins, jouts, sems = refs[:ji], refs[ji:ji + jo], refs[ji + jo:]
        job.start(jins, jouts, sems)
        if job.mid is not None:
            job.mid(jins, jouts, sems)
        job.finish(jins, jouts, sems)

    res = pl.pallas_call(body, name=name, in_specs=[HBM_SPEC] * ji, out_specs=[HBM_SPEC] * jo,
                         out_shape=job.outs, scratch_shapes=job.sems, input_output_aliases=job.alias)(*job.ins)
    job.results = list(res)
    return job.results


def _sequencer_job(name, job, collective_id):
    ji, jo = len(job.ins), len(job.outs)

    def body(*refs):
        jins, jouts, sems = refs[:ji], refs[ji:ji + jo], refs[ji + jo:]
        barrier = pltpu.get_barrier_semaphore()
        x, y, c = lax.axis_index("x"), lax.axis_index("y"), lax.axis_index("c")
        for r in range(1, N_DEV):
            peer = ((1 - x) if r & 4 else x, (1 - y) if r & 2 else y, (1 - c) if r & 1 else c)
            pl.semaphore_signal(barrier, inc=1, device_id=peer, device_id_type=MESH)
        pl.semaphore_wait(barrier, N_DEV - 1)
        job.start(jins, jouts, sems)
        if job.mid is not None:
            job.mid(jins, jouts, sems)
        job.finish(jins, jouts, sems)

    res = pl.kernel(
        body, name=name, out_type=job.outs, mesh=plsc.ScalarSubcoreMesh(axis_name="sequencer", num_cores=1),
        scratch_types=job.sems, compiler_params=pltpu.CompilerParams(collective_id=collective_id),
    )(*job.ins)
    job.results = list(res)
    return job.results


def _dot(mode, a, b):
    if mode == "nn":
        dims = (((1,), (0,)), ((), ()))
    elif mode == "nt":
        dims = (((1,), (1,)), ((), ()))
    else:
        dims = (((0,), (0,)), ((), ()))
    return lax.dot_general(a, b, dims, preferred_element_type=F32)


def _sig(x):
    return 0.5 * jnp.tanh(0.5 * x) + 0.5


def _gelu(x):
    t = jnp.tanh(GELU_C * (x + GELU_K * x * x * x))
    return 0.5 * x * (1.0 + t)


def _gelu_grad(x):
    x2 = x * x
    t = jnp.tanh(GELU_C * (x + GELU_K * x * x2))
    return 0.5 * (1.0 + t) + 0.5 * x * (1.0 - t * t) * GELU_C * (1.0 + 3.0 * GELU_K * x2)


def _colsum(v):
    return jnp.sum(v, axis=0, keepdims=True)


def _mm(name, mode, grid, a, a_spec, b, b_spec, extras, out_shapes, out_specs, epi, *,
        order="ij", acc_shape=None, aliases=None, vmem_mb=VMEM_MB, epi_init=None, sub=None, job=None):
    gm, gn, gk = grid

    def spec(s):
        bs, f = s
        if f is None:
            return pl.BlockSpec(memory_space=pl.ANY)
        if order == "ij":
            return pl.BlockSpec(bs, lambda i, j, k, f=f: f(i, j, k))
        return pl.BlockSpec(bs, lambda j, i, k, f=f: f(i, j, k))

    ne, no = len(extras), len(out_shapes)

    def kern(*refs):
        a_ref, b_ref = refs[0], refs[1]
        ex = refs[2:2 + ne]
        outs = refs[2 + ne:2 + ne + no]
        if order == "ij":
            i, j = pl.program_id(0), pl.program_id(1)
        else:
            j, i = pl.program_id(0), pl.program_id(1)
        k = pl.program_id(2)
        prod = _dot(mode, a_ref[...], b_ref[...])
        if gk == 1:
            if epi_init is not None:
                epi_init(i, j, outs)
            epi(prod, i, j, ex, outs)
        elif epi is None:
            @pl.when(k == 0)
            def _():
                outs[0][...] = prod

            @pl.when(k > 0)
            def _():
                outs[0][...] += prod
        else:
            acc = refs[-1]

            @pl.when(k == 0)
            def _():
                acc[...] = prod

            @pl.when(k > 0)
            def _():
                acc[...] += prod

            @pl.when(k == gk - 1)
            def _():
                if epi_init is not None:
                    epi_init(i, j, outs)
                if sub is None:
                    epi(acc[...], i, j, ex, outs)
                else:
                    tm = acc_shape[0]

                    def rows_of(r, rs):
                        return r.at[rs, :] if r.shape[0] == tm else r

                    def blk(t, carry):
                        rs = pl.ds(pl.multiple_of(t * sub, sub), sub)
                        epi(acc[rs, :], i, j, [rows_of(r, rs) for r in ex], [rows_of(r, rs) for r in outs])
                        return carry

                    lax.fori_loop(0, tm // sub, blk, 0)

    g = (gm, gn, gk) if order == "ij" else (gn, gm, gk)
    scratch = [pltpu.VMEM(acc_shape, F32)] if gk > 1 and epi is not None else []
    return _pcall(name, kern, g, [spec(a_spec), spec(b_spec)] + [spec(s) for _, s in extras],
                  [spec(s) for s in out_specs], out_shapes, [a, b] + [e for e, _ in extras],
                  scratch=scratch, aliases=aliases, vmem_mb=vmem_mb, job=job)


def _ext(ref, c, n_chunks, tc, seq):
    c0 = pl.multiple_of(c * tc, tc)
    body = ref[pl.ds(c0, tc), :].astype(F32)
    t0 = pl.multiple_of(jnp.maximum(c0 - HALO, 0), HALO)
    b0 = pl.multiple_of(jnp.minimum(c0 + tc, seq - HALO), HALO)
    top = ref[pl.ds(t0, HALO), :].astype(F32)
    bot = ref[pl.ds(b0, HALO), :].astype(F32)
    top = jnp.where(c > 0, top, 0.0)
    bot = jnp.where(c < n_chunks - 1, bot, 0.0)
    return jnp.concatenate([top, body, bot], axis=0)


def _shifted(vext, off, tc):
    n = vext.shape[0]
    r = vext if off == 0 else pltpu.roll(vext, (n - off) % n, 0)
    return r[HALO:HALO + tc]


def _win_sum(vext, g, extra, tc):
    s2 = vext + pltpu.roll(vext, 1, 0)
    s4 = s2 + pltpu.roll(s2, 2, 0)
    s8 = s4 + pltpu.roll(s4, 4, 0)
    s16 = s8 + pltpu.roll(s8, 8, 0)
    outs = [_shifted(s, extra + hw - 1, tc) for s, hw in ((s2, 1), (s4, 2), (s8, 4), (s16, 8))]
    return jnp.where(g == 0, outs[0], jnp.where(g == 1, outs[1], jnp.where(g == 2, outs[2], outs[3])))


def _win_cnt(t, hw, seq):
    return (jnp.minimum(t + hw, seq) - jnp.maximum(t - hw, 0)).astype(F32)


def _pool_d(uext, g, c, tc, seq):
    hw = jnp.left_shift(1, g)
    t = c * tc + lax.broadcasted_iota(jnp.int32, (tc, 1), 0)
    ws = _win_sum(uext, g, 0, tc)
    return ws / _win_cnt(t, hw, seq) - uext[HALO:HALO + tc]


def _scan_tiles(a_ref, b_ref, h_ref, carry_ref, n_tiles, reverse):
    blk = a_ref.shape[1]
    row = lax.broadcasted_iota(jnp.int32, (SUBLANES, blk), 0)

    def tile(j, hc):
        jj = (n_tiles - 1 - j) if reverse else j
        off = pl.multiple_of(jj * SUBLANES, SUBLANES)
        a = a_ref[pl.ds(off, SUBLANES), :]
        b = b_ref[pl.ds(off, SUBLANES), :]
        for kk in (1, 2, 4):
            sh = (SUBLANES - kk) if reverse else kk
            a_s = pltpu.roll(a, sh, 0)
            b_s = pltpu.roll(b, sh, 0)
            m = (row < SUBLANES - kk) if reverse else (row >= kk)
            a_s = jnp.where(m, a_s, 1.0)
            b_s = jnp.where(m, b_s, 0.0)
            b = a * b_s + b
            a = a * a_s
        h = a * hc + b
        h_ref[pl.ds(off, SUBLANES), :] = h
        return h[0:1, :] if reverse else h[SUBLANES - 1:SUBLANES, :]

    group = SUBLANES if n_tiles % SUBLANES == 0 else 1

    def tiles(jg, hc):
        for u in range(group):
            hc = tile(jg * group + u, hc)
        return hc

    hc = lax.fori_loop(0, n_tiles // group, tiles, carry_ref[0:1, :])
    carry_ref[0:1, :] = hc


def _lru_k(lam):
    y = -lam
    e = jnp.exp(-jnp.abs(y))
    u = 1.0 + e
    l1p = jnp.where(u == 1.0, e, jnp.log(u) * (e / (u - 1.0)))
    return -LRU_C * (jnp.maximum(y, 0.0) + l1p)


def _lru_gates(xc, wa, wx, ba, bx, lam):
    xb = xc.astype(BF16)
    r = _sig(jnp.dot(xb, wa, preferred_element_type=F32) + ba)
    i = _sig(jnp.dot(xb, wx, preferred_element_type=F32) + bx)
    k = _lru_k(lam)
    la = k * r
    a = jnp.exp(la)
    s = jnp.sqrt(-jnp.tanh(la) * (a * a + 1.0))
    return r, i, k, a, s


SV_CONV, SV_BA, SV_BX, SV_LAM = 0, 4, 6, 8


def _pool_fwd(z, pw, scale, seq, d, tc, job=None):
    n_g = pw.shape[0]
    pg = d // n_g
    n_chunks = seq // tc

    def kern(z_ref, pw_ref, sc_ref, y_ref):
        g, c = pl.program_id(0), pl.program_id(1)
        uext = _ext(z_ref, c, n_chunks, tc, seq)
        dd = _pool_d(uext, g, c, tc, seq)
        q = jnp.dot(dd.astype(BF16), pw_ref[...], preferred_element_type=F32)
        y_ref[...] = (q * sc_ref[...]).astype(BF16)

    (y,) = _pcall(
        "pool_fwd", kern, (n_g, n_chunks),
        [pl.BlockSpec((seq, pg), lambda g, c: (0, g)),
         pl.BlockSpec((None, pg, pg), lambda g, c: (g, 0, 0)),
         pl.BlockSpec((1, pg), lambda g, c: (0, g))],
        [pl.BlockSpec((tc, pg), lambda g, c: (c, g))],
        [jax.ShapeDtypeStruct((seq, d), BF16)], [z, pw, scale], job=job)
    return y


def _lru_fwd(z, sv, conv_b, wa, wx, seq, d, tc, job=None):
    n_h, blk = wa.shape[1], wa.shape[2]
    n_chunks = seq // tc
    lru_off = d // blk

    def kern(z_ref, sv_ref, cb_ref, wa_ref, wx_ref, xc_ref, h_ref, a_s, b_s, carry):
        c = pl.program_id(1)
        uext = _ext(z_ref, c, n_chunks, tc, seq)
        xc = cb_ref[...]
        for k in range(4):
            xc = xc + _shifted(uext, k - 2, tc) * sv_ref[SV_CONV + k:SV_CONV + k + 1, :]
        xc_ref[...] = xc
        _, i, _, a, s = _lru_gates(xc, wa_ref[...], wx_ref[...], sv_ref[SV_BA:SV_BA + 1, :],
                                   sv_ref[SV_BX:SV_BX + 1, :], sv_ref[SV_LAM:SV_LAM + 1, :])
        a_s[...] = a
        b_s[...] = s * (i * xc)

        @pl.when(c == 0)
        def _():
            carry[...] = jnp.zeros_like(carry)

        _scan_tiles(a_s, b_s, h_ref, carry, tc // SUBLANES, False)

    col = lambda h, c: (c, h)
    return _pcall(
        "lru_fwd", kern, (n_h, n_chunks),
        [pl.BlockSpec((seq, blk), lambda h, c: (0, lru_off + h)),
         pl.BlockSpec((16, blk), lambda h, c: (0, h)),
         pl.BlockSpec((1, blk), lambda h, c: (0, h)),
         pl.BlockSpec((None, None, blk, blk), lambda h, c: (0, h, 0, 0)),
         pl.BlockSpec((None, None, blk, blk), lambda h, c: (0, h, 0, 0))],
        [pl.BlockSpec((tc, blk), col), pl.BlockSpec((tc, blk), col)],
        [jax.ShapeDtypeStruct((seq, d), F32), jax.ShapeDtypeStruct((seq, d), F32)],
        [z, sv, conv_b, wa, wx],
        scratch=[pltpu.VMEM((tc, blk), F32), pltpu.VMEM((tc, blk), F32), pltpu.VMEM((SUBLANES, blk), F32)],
        job=job)


def _lru_rev(z, sv, wa, wx, xc, h_f, seq, d, tc, job=None):
    n_h, blk = wa.shape[1], wa.shape[2]
    n_chunks = seq // tc
    gate_off = 2 * d // blk

    def kern(z_ref, sv_ref, wa_ref, wx_ref, xc_ref, hf_ref, hb_ref, y_ref, a_s, b_s, carry):
        c = pl.program_id(1)
        xc = xc_ref[...]
        _, i, _, a, s = _lru_gates(xc, wa_ref[...], wx_ref[...], sv_ref[SV_BA + 1:SV_BA + 2, :],
                                   sv_ref[SV_BX + 1:SV_BX + 2, :], sv_ref[SV_LAM + 1:SV_LAM + 2, :])
        a_s[...] = a
        b_s[...] = s * (i * xc)

        @pl.when(c == 0)
        def _():
            carry[...] = jnp.zeros_like(carry)

        _scan_tiles(a_s, b_s, hb_ref, carry, tc // SUBLANES, True)
        y_ref[...] = ((hf_ref[...] + hb_ref[...]) * _gelu(z_ref[...])).astype(BF16)

    rev = lambda h, c: (n_chunks - 1 - c, h)
    return _pcall(
        "lru_rev", kern, (n_h, n_chunks),
        [pl.BlockSpec((tc, blk), lambda h, c: (n_chunks - 1 - c, gate_off + h)),
         pl.BlockSpec((16, blk), lambda h, c: (0, h)),
         pl.BlockSpec((None, None, blk, blk), lambda h, c: (1, h, 0, 0)),
         pl.BlockSpec((None, None, blk, blk), lambda h, c: (1, h, 0, 0)),
         pl.BlockSpec((tc, blk), rev), pl.BlockSpec((tc, blk), rev)],
        [pl.BlockSpec((tc, blk), rev), pl.BlockSpec((tc, blk), rev)],
        [jax.ShapeDtypeStruct((seq, d), F32), jax.ShapeDtypeStruct((seq, d), BF16)],
        [z, sv, wa, wx, xc, h_f],
        scratch=[pltpu.VMEM((tc, blk), F32), pltpu.VMEM((tc, blk), F32), pltpu.VMEM((SUBLANES, blk), F32)],
        job=job)


def _merge(y_pool, w_pu, y_lru, w_lu, z, seq, d, tm, tn, job=None):
    n_n = d // tn

    def kern(yp_ref, wp_ref, yl_ref, wl_ref, la_ref, lb_ref, m_ref, pa_ref, pb_ref):
        pa = jnp.dot(yp_ref[...], wp_ref[...], preferred_element_type=F32)
        pb = jnp.dot(yl_ref[...], wl_ref[...], preferred_element_type=F32)
        m_ref[...] = (_sig(la_ref[...]) * pa + _sig(lb_ref[...]) * pb).astype(BF16)
        pa_ref[...] = pa.astype(BF16)
        pb_ref[...] = pb.astype(BF16)

    row = pl.BlockSpec((tm, d), lambda i, j: (i, 0))
    wcol = pl.BlockSpec((d, tn), lambda i, j: (0, j))
    out = pl.BlockSpec((tm, tn), lambda i, j: (i, j))
    sh = jax.ShapeDtypeStruct((seq, d), BF16)
    return _pcall(
        "merge", kern, (seq // tm, n_n),
        [row, wcol, row, wcol,
         pl.BlockSpec((tm, tn), lambda i, j: (i, 3 * n_n + j)),
         pl.BlockSpec((tm, tn), lambda i, j: (i, 4 * n_n + j))],
        [out, out, out], [sh, sh, sh], [y_pool, w_pu, y_lru, w_lu, z, z], job=job)


def _ln_fwd(s, g, b):
    mu = jnp.mean(s, axis=-1, keepdims=True)
    xc = s - mu
    var = jnp.mean(xc * xc, axis=-1, keepdims=True)
    rstd = lax.rsqrt(var + LN_EPS)
    xhat = xc * rstd
    return xhat, rstd, xhat * g + b


def _ln_bwd(dy, xhat, rstd, g):
    dyg = dy * g
    m1 = jnp.mean(dyg, axis=-1, keepdims=True)
    m2 = jnp.mean(dyg * xhat, axis=-1, keepdims=True)
    return rstd * (dyg - m1 - xhat * m2)


def _pool_bwd(z, dy_pool, pw, scale, dz, seq, d, tc):
    n_g = pw.shape[0]
    pg = d // n_g
    n_chunks = seq // tc

    def kern(z_ref, dy_ref, pw_ref, sc_ref, dz_in, dz_ref, dpw_ref, dsc_ref):
        del dz_in
        g, c = pl.program_id(0), pl.program_id(1)
        hw = jnp.left_shift(1, g)
        uext = _ext(z_ref, c, n_chunks, tc, seq)
        dd = _pool_d(uext, g, c, tc, seq).astype(BF16)
        pwv = pw_ref[...]
        q = jnp.dot(dd, pwv, preferred_element_type=F32)
        dyext = _ext(dy_ref, c, n_chunks, tc, seq)

        @pl.when(c == 0)
        def _():
            dsc_ref[...] = jnp.zeros_like(dsc_ref)
            dpw_ref[...] = jnp.zeros_like(dpw_ref)

        dsc_ref[0:1, :] += _colsum(dyext[HALO:HALO + tc] * q)
        dqext = (dyext * sc_ref[...]).astype(BF16)
        dpw_ref[...] += _dot("tn", dd, dqext[HALO:HALO + tc])
        ddext = _dot("nt", dqext, pwv)
        text = c * tc - HALO + lax.broadcasted_iota(jnp.int32, (tc + 2 * HALO, 1), 0)
        v = ddext / jnp.maximum(_win_cnt(text, hw, seq), 1.0)
        dz_ref[...] = (_win_sum(v, g, 1, tc) - ddext[HALO:HALO + tc]).astype(BF16)

    return pl.pallas_call(
        kern, name="pool_bwd", grid=(n_g, n_chunks),
        in_specs=[pl.BlockSpec((seq, pg), lambda g, c: (0, g)),
                  pl.BlockSpec((seq, pg), lambda g, c: (0, g)),
                  pl.BlockSpec((None, pg, pg), lambda g, c: (g, 0, 0)),
                  pl.BlockSpec((1, pg), lambda g, c: (0, g)),
                  pl.BlockSpec(memory_space=pl.ANY)],
        out_specs=[pl.BlockSpec((None, tc, pg), lambda g, c: (2, c, g)),
                   pl.BlockSpec((None, pg, pg), lambda g, c: (g, 0, 0)),
                   pl.BlockSpec((SUBLANES, pg), lambda g, c: (0, g))],
        out_shape=[jax.ShapeDtypeStruct(dz.shape, dz.dtype),
                   jax.ShapeDtypeStruct((n_g, pg, pg), F32),
                   jax.ShapeDtypeStruct((SUBLANES, d), F32)],
        input_output_aliases={4: 0},
        compiler_params=_cparams(("arbitrary", "arbitrary")),
    )(z, dy_pool, pw, scale, dz)


def _lru_bwd(direction, xc, dh, h_dir, sv, wa, wx, dxc_prev, seq, d, tc, job=None):
    reverse = direction == 1
    n_h, blk = wa.shape[1], wa.shape[2]
    n_chunks = seq // tc
    has_prev = dxc_prev is not None

    def kern(*refs):
        xc_ref, dh_ref, h_ref, sv_ref, wa_ref, wx_ref = refs[:6]
        p = 6
        prev_ref = None
        if has_prev:
            prev_ref = refs[p]
            p += 1
        dxc_ref, dwa_ref, dwx_ref, sm_ref, at_s, g_s, carry, acarry = refs[p:p + 8]
        c = pl.program_id(1)
        cr = c if reverse else n_chunks - 1 - c
        c0 = pl.multiple_of(cr * tc, tc)

        @pl.when(c == 0)
        def _():
            carry[...] = jnp.zeros_like(carry)
            acarry[...] = jnp.zeros_like(acarry)
            dwa_ref[...] = jnp.zeros_like(dwa_ref)
            dwx_ref[...] = jnp.zeros_like(dwx_ref)
            sm_ref[...] = jnp.zeros_like(sm_ref)

        xc = xc_ref[...]
        wav, wxv = wa_ref[...], wx_ref[...]
        lam = sv_ref[SV_LAM + direction:SV_LAM + direction + 1, :]
        r, i, k, a, s = _lru_gates(xc, wav, wxv, sv_ref[SV_BA + direction:SV_BA + direction + 1, :],
                                   sv_ref[SV_BX + direction:SV_BX + direction + 1, :], lam)
        rowi = lax.broadcasted_iota(jnp.int32, (tc, blk), 0)
        hbody = h_ref[pl.ds(c0, tc), :]
        if not reverse:
            p0 = pl.multiple_of(jnp.maximum(c0 - SUBLANES, 0), SUBLANES)
            edge = jnp.where(cr > 0, h_ref[pl.ds(p0, SUBLANES), :][SUBLANES - 1:SUBLANES, :], 0.0)
            hprev = jnp.where(rowi == 0, edge, pltpu.roll(hbody, 1, 0))
            at = jnp.where(rowi == tc - 1, acarry[0:1, :], pltpu.roll(a, tc - 1, 0))
        else:
            n0 = pl.multiple_of(jnp.minimum(c0 + tc, seq - SUBLANES), SUBLANES)
            edge = jnp.where(cr < n_chunks - 1, h_ref[pl.ds(n0, SUBLANES), :][0:1, :], 0.0)
            hprev = jnp.where(rowi == tc - 1, edge, pltpu.roll(hbody, tc - 1, 0))
            at = jnp.where(rowi == 0, acarry[0:1, :], pltpu.roll(a, 1, 0))
        at_s[...] = at
        _scan_tiles(at_s, dh_ref, g_s, carry, tc // SUBLANES, not reverse)
        acarry[0:1, :] = a[tc - 1:tc, :] if reverse else a[0:1, :]

        gt = g_s[...]
        da = gt * hprev
        di = gt * s * xc
        dxc = gt * s * i
        ds = gt * (i * xc)
        dl = da * a - ds * (a * a) / s
        dpr = (dl * k) * r * (1.0 - r)
        dpi = di * i * (1.0 - i)
        sm_ref[0:1, :] += _colsum(dpr)
        sm_ref[1:2, :] += _colsum(dpi)
        sm_ref[2:3, :] += _colsum(dl * r) * (LRU_C * _sig(-lam))
        xb, dprb, dpib = xc.astype(BF16), dpr.astype(BF16), dpi.astype(BF16)
        dwa_ref[...] += _dot("tn", xb, dprb)
        dwx_ref[...] += _dot("tn", xb, dpib)
        dxc = dxc + _dot("nt", dprb, wav) + _dot("nt", dpib, wxv)
        if has_prev:
            dxc = dxc + prev_ref[...]
        dxc_ref[...] = dxc

    if reverse:
        chunk = lambda h, c: (c, h)
    else:
        chunk = lambda h, c: (n_chunks - 1 - c, h)
    wspec = pl.BlockSpec((None, None, blk, blk), lambda h, c: (direction, h, 0, 0))
    ins = [xc, dh, h_dir, sv, wa, wx] + ([dxc_prev] if has_prev else [])
    in_specs = [pl.BlockSpec((tc, blk), chunk), pl.BlockSpec((tc, blk), chunk),
                pl.BlockSpec((seq, blk), lambda h, c: (0, h)),
                pl.BlockSpec((16, blk), lambda h, c: (0, h)), wspec, wspec]
    if has_prev:
        in_specs.append(pl.BlockSpec((tc, blk), chunk))
    return _pcall(
        "lru_bwd_%d" % direction, kern, (n_h, n_chunks), in_specs,
        [pl.BlockSpec((tc, blk), chunk),
         pl.BlockSpec((None, blk, blk), lambda h, c: (h, 0, 0)),
         pl.BlockSpec((None, blk, blk), lambda h, c: (h, 0, 0)),
         pl.BlockSpec((SUBLANES, blk), lambda h, c: (0, h))],
        [jax.ShapeDtypeStruct((seq, d), F32),
         jax.ShapeDtypeStruct((n_h, blk, blk), F32),
         jax.ShapeDtypeStruct((n_h, blk, blk), F32),
         jax.ShapeDtypeStruct((SUBLANES, d), F32)],
        ins,
        scratch=[pltpu.VMEM((tc, blk), F32), pltpu.VMEM((tc, blk), F32),
                 pltpu.VMEM((SUBLANES, blk), F32), pltpu.VMEM((SUBLANES, blk), F32)],
        job=job)


def _conv_bwd(z, dxc, sv, dz, seq, d, tc, tcol):
    n_chunks = seq // tc
    lru_off = d // tcol

    def kern(z_ref, dx_ref, sv_ref, dz_in, dz_ref, sm_ref):
        del dz_in
        c = pl.program_id(1)
        uext = _ext(z_ref, c, n_chunks, tc, seq)
        dext = _ext(dx_ref, c, n_chunks, tc, seq)
        dbody = dext[HALO:HALO + tc]

        @pl.when(c == 0)
        def _():
            sm_ref[...] = jnp.zeros_like(sm_ref)

        du = jnp.zeros_like(dbody)
        for k in range(4):
            du = du + _shifted(dext, 2 - k, tc) * sv_ref[SV_CONV + k:SV_CONV + k + 1, :]
            sm_ref[k:k + 1, :] += _colsum(dbody * _shifted(uext, k - 2, tc))
        sm_ref[4:5, :] += _colsum(dbody)
        dz_ref[...] = du.astype(BF16)

    return pl.pallas_call(
        kern, name="conv_bwd", grid=(d // tcol, n_chunks),
        in_specs=[pl.BlockSpec((seq, tcol), lambda j, c: (0, lru_off + j)),
                  pl.BlockSpec((seq, tcol), lambda j, c: (0, j)),
                  pl.BlockSpec((16, tcol), lambda j, c: (0, j)),
                  pl.BlockSpec(memory_space=pl.ANY)],
        out_specs=[pl.BlockSpec((None, tc, tcol), lambda j, c: (3, c, j)),
                   pl.BlockSpec((SUBLANES, tcol), lambda j, c: (0, j))],
        out_shape=[jax.ShapeDtypeStruct(dz.shape, dz.dtype), jax.ShapeDtypeStruct((SUBLANES, d), F32)],
        input_output_aliases={3: 0},
        compiler_params=_cparams(("arbitrary", "arbitrary")),
    )(z, dxc, sv, dz)


BIG = (("w_in", "col"), ("pool_w", "row"), ("lru_wa", "row"), ("lru_wx", "row"), ("w_pool_up", "row"),
       ("w_lru_up", "row"), ("w_out", "row"), ("w_ff1", "col"), ("w_ff2", "row"))


def _shard_view(w, fam):
    if fam == "col":
        return w.reshape(w.shape[-2:])
    return w.reshape((-1,) + w.shape[-2:])


def _full_shape(sv_shape, fam):
    if fam == "col":
        return (sv_shape[0], N_DEV * sv_shape[1])
    return (sv_shape[0], N_DEV * sv_shape[1], sv_shape[2])


def _slot(ref, fam, p, size, part=None):
    if fam == "lead":
        return ref.at[p]
    k, n, span = part or (0, 1, 1)
    unit = size // n
    start = pl.multiple_of(p * size + k * unit, unit)
    if fam == "col":
        return ref.at[:, pl.ds(start, span * unit)]
    return ref.at[:, pl.ds(start, span * unit), :]


def _shard_part(ref, fam, size, part):
    if part is None or fam == "lead":
        return ref
    k, n, span = part
    unit = size // n
    if fam == "col":
        return ref.at[:, pl.ds(k * unit, span * unit)]
    return ref.at[:, pl.ds(k * unit, span * unit), :]


def _shard_extent(shape, fam):
    return shape[1]


def _coords():
    return lax.axis_index("x"), lax.axis_index("y"), lax.axis_index("c")


def _ag_job(shards, fams, parts=None, into=None):
    n = len(shards)
    parts = list(parts) if parts is not None else [None] * n
    into = list(into) if into is not None else [None] * n
    fulls = []
    for s, fam in zip(shards, fams):
        if fam == "lead":
            fulls.append(jax.ShapeDtypeStruct((N_DEV,) + s.shape, s.dtype))
        else:
            fulls.append(jax.ShapeDtypeStruct(_full_shape(s.shape, fam), s.dtype))
    sizes = [1 if fam == "lead" else _shard_extent(s.shape, fam) for s, fam in zip(shards, fams)]
    given = [a for a in range(n) if into[a] is not None]

    def ctx(ins, outs, sems):
        send, recv, loc = sems
        x, y, c = _coords()
        chips = [(1 - x, y), (x, 1 - y), (1 - x, 1 - y)]

        def mine(a):
            return _shard_part(ins[a], fams[a], sizes[a], parts[a])

        def copy(a, k, owner, to, src=None):
            dst = _slot(outs[a], fams[a], owner, sizes[a], parts[a])
            return pltpu.make_async_remote_copy(
                src_ref=dst if src is None else src, dst_ref=dst,
                send_sem=send.at[a, k], recv_sem=recv.at[a, k], device_id=to, device_id_type=MESH)

        def local(a):
            return pltpu.make_async_copy(
                mine(a), _slot(outs[a], fams[a], 4 * x + 2 * y + c, sizes[a], parts[a]), loc.at[a])

        return x, y, c, chips, copy, local, mine

    def start(ins, outs, sems):
        x, y, c, chips, copy, local, mine = ctx(ins, outs, sems)
        me = 4 * x + 2 * y + c
        for a in range(n):
            local(a).start()
            copy(a, 0, me, (x, y, 1 - c), mine(a)).start()
            for j, (cx, cy) in enumerate(chips):
                copy(a, 1 + j, me, (cx, cy, c), mine(a)).start()

    def mid(ins, outs, sems):
        x, y, c, chips, copy, _, _ = ctx(ins, outs, sems)
        for a in range(n):
            for j, (cx, cy) in enumerate(chips):
                owner = 4 * cx + 2 * cy + c
                copy(a, 1 + j, owner, (x, y, c)).wait_recv()
                copy(a, 4 + j, owner, (x, y, 1 - c)).start()

    def finish(ins, outs, sems):
        x, y, c, chips, copy, local, mine = ctx(ins, outs, sems)
        me = 4 * x + 2 * y + c
        for a in range(n):
            copy(a, 0, 4 * x + 2 * y + (1 - c), (x, y, c)).wait_recv()
            for j, (cx, cy) in enumerate(chips):
                copy(a, 4 + j, 4 * cx + 2 * cy + (1 - c), (x, y, c)).wait_recv()
            copy(a, 0, me, (x, y, 1 - c), mine(a)).wait_send()
            for j, (cx, cy) in enumerate(chips):
                copy(a, 1 + j, me, (cx, cy, c), mine(a)).wait_send()
                copy(a, 4 + j, 4 * cx + 2 * cy + c, (x, y, 1 - c)).wait_send()
            local(a).wait()

    sems = [pltpu.SemaphoreType.DMA((n, 7)), pltpu.SemaphoreType.DMA((n, 7)), pltpu.SemaphoreType.DMA((n,))]
    job = _Job(list(shards) + [into[a] for a in given], fulls, sems, start, finish, mid)
    job.alias = {n + i: a for i, a in enumerate(given)}
    return job


def _rs_sibling_job(fulls, fams, sizes):
    n = len(fulls)
    outs = []
    for f, fam, sz in zip(fulls, fams, sizes):
        if fam == "col":
            outs.append(jax.ShapeDtypeStruct((4, f.shape[0], sz), f.dtype))
        else:
            outs.append(jax.ShapeDtypeStruct((4, f.shape[0], sz, f.shape[2]), f.dtype))

    def copies(ins, rcv, sems):
        send, recv = sems
        x, y, c = _coords()
        return [pltpu.make_async_remote_copy(
            src_ref=_slot(ins[a], fams[a], 2 * q + (1 - c), sizes[a]), dst_ref=rcv[a].at[q],
            send_sem=send.at[a, q], recv_sem=recv.at[a, q], device_id=(x, y, 1 - c), device_id_type=MESH)
            for a in range(n) for q in range(4)]

    def start(ins, rcv, sems):
        for cp in copies(ins, rcv, sems):
            cp.start()

    def finish(ins, rcv, sems):
        for cp in copies(ins, rcv, sems):
            cp.wait()

    return _Job(fulls, outs, [pltpu.SemaphoreType.DMA((n, 4)), pltpu.SemaphoreType.DMA((n, 4))], start, finish)


def _rs_chips_job(parts):
    n = len(parts)
    outs = [jax.ShapeDtypeStruct((3,) + p.shape[1:], p.dtype) for p in parts]

    def copies(ins, rcv, sems):
        send, recv = sems
        x, y, c = _coords()
        cps = []
        for a in range(n):
            for r in (1, 2, 3):
                tx, ty = (1 - x) if r & 2 else x, (1 - y) if r & 1 else y
                cps.append(pltpu.make_async_remote_copy(
                    src_ref=ins[a].at[2 * tx + ty], dst_ref=rcv[a].at[r - 1],
                    send_sem=send.at[a, r - 1], recv_sem=recv.at[a, r - 1],
                    device_id=(tx, ty, c), device_id_type=MESH))
        return cps

    def start(ins, rcv, sems):
        for cp in copies(ins, rcv, sems):
            cp.start()

    def finish(ins, rcv, sems):
        for cp in copies(ins, rcv, sems):
            cp.wait()

    return _Job(parts, outs, [pltpu.SemaphoreType.DMA((n, 3)), pltpu.SemaphoreType.DMA((n, 3))], start, finish)


def _tile_rows(rows, cols):
    tr = rows
    while tr * cols > (1 << 18) and tr % (2 * SUBLANES) == 0:
        tr //= 2
    return tr


def _rs_add(name, full, recv_a, fam, size, cidx):
    if fam == "col":
        rows = full.shape[0]
        tr = _tile_rows(rows, size)
        grid = (4, rows // tr)
        f_spec = pl.BlockSpec((tr, size), lambda q, i, cr: (i, 2 * q + cr[0]))
        s_spec = pl.BlockSpec((None, tr, size), lambda q, i, cr: (q, i, 0))
    else:
        nb, cols = full.shape[0], full.shape[2]
        tr = _tile_rows(nb * size, cols) // nb if nb > 1 else _tile_rows(size, cols)
        nt = size // tr
        grid = (4, nt)
        f_spec = pl.BlockSpec((nb, tr, cols), lambda q, i, cr: (0, (2 * q + cr[0]) * nt + i, 0))
        s_spec = pl.BlockSpec((None, nb, tr, cols), lambda q, i, cr: (q, 0, i, 0))

    def kern(c_ref, f_ref, r_ref, o_ref):
        del c_ref
        o_ref[...] = (f_ref[...] + r_ref[...]).astype(BF16)

    return pl.pallas_call(
        kern, name=name,
        grid_spec=pltpu.PrefetchScalarGridSpec(num_scalar_prefetch=1, grid=grid, in_specs=[f_spec, s_spec],
                                               out_specs=s_spec),
        out_shape=jax.ShapeDtypeStruct(recv_a.shape, BF16),
        compiler_params=_cparams(("arbitrary", "arbitrary"), 32),
    )(cidx, full, recv_a)


def _adam(w, g, m, v):
    m2 = ADAM_B1 * m + (1.0 - ADAM_B1) * g
    v2 = ADAM_B2 * v + (1.0 - ADAM_B2) * (g * g)
    m_hat = m2 / (1.0 - ADAM_B1 ** ADAM_STEP)
    v_hat = v2 / (1.0 - ADAM_B2 ** ADAM_STEP)
    delta = -ADAM_LR * (m_hat / (jnp.sqrt(v_hat) + ADAM_EPS) + ADAM_WD * w)
    return delta, m2, v2


def _rs_final_adam(name, parts, recv_b, w, m, v, fam, qidx):
    shp = w.shape
    pieces = parts if isinstance(parts, (list, tuple)) else [parts]
    recvs = recv_b if isinstance(recv_b, (list, tuple)) else [recv_b]
    n_p = len(pieces)
    first_blk = [0] * n_p
    if fam == "col":
        rows, cols = shp
        tr = _tile_rows(min(p.shape[1] for p in pieces), cols)
        per = [p.shape[1] // tr for p in pieces]
        for h in range(1, n_p):
            first_blk[h] = first_blk[h - 1] + per[h - 1]
        grid = (rows // tr,)
        w_spec = pl.BlockSpec((tr, cols), lambda i, qr: (i, 0))

        def piece_row(i, h):
            return jnp.clip(i - first_blk[h], 0, per[h] - 1)

        p_specs = [pl.BlockSpec((None, tr, cols), lambda i, qr, h=h: (qr[0], piece_row(i, h), 0))
                   for h in range(n_p)]
        r_specs = [pl.BlockSpec((3, tr, cols), lambda i, qr, h=h: (0, piece_row(i, h), 0)) for h in range(n_p)]
    else:
        assert n_p == 1
        nb, rows, cols = shp
        tr = _tile_rows(rows, cols)
        nt = rows // tr
        grid = (nb * nt,)
        w_spec = pl.BlockSpec((None, tr, cols), lambda i, qr: (i // nt, i % nt, 0))
        p_specs = [pl.BlockSpec((None, None, tr, cols), lambda i, qr: (qr[0], i // nt, i % nt, 0))]
        r_specs = [pl.BlockSpec((3, None, tr, cols), lambda i, qr: (0, i // nt, i % nt, 0))]

    def kern(*refs):
        p_refs, r_refs = refs[1:1 + n_p], refs[1 + n_p:1 + 2 * n_p]
        w_ref, m_ref, v_ref, g_out, d_out, m_out, v_out = refs[1 + 2 * n_p:]

        def total(h):
            p_ref, r_ref = p_refs[h], r_refs[h]
            return ((p_ref[...].astype(F32) + r_ref[0].astype(F32)) + r_ref[1].astype(F32)) + r_ref[2].astype(F32)

        g = total(0)
        for h in range(1, n_p):
            g = jnp.where(pl.program_id(0) >= first_blk[h], total(h), g)
        delta, m2, v2 = _adam(w_ref[...], g, m_ref[...], v_ref[...])
        g_out[...] = g
        d_out[...] = delta
        m_out[...] = m2
        v_out[...] = v2

    sh = jax.ShapeDtypeStruct(shp, F32)
    return pl.pallas_call(
        kern, name=name,
        grid_spec=pltpu.PrefetchScalarGridSpec(
            num_scalar_prefetch=1, grid=grid, in_specs=p_specs + r_specs + [w_spec, w_spec, w_spec],
            out_specs=[w_spec] * 4),
        out_shape=[sh] * 4,
        compiler_params=_cparams(("arbitrary",), 32),
    )(qidx, *pieces, *recvs, w, m, v)


def _sum8(name, parts):
    def kern(p_ref, o_ref):
        acc = p_ref[0]
        for p in range(1, N_DEV):
            acc = acc + p_ref[p]
        o_ref[...] = acc

    return pl.pallas_call(
        kern, name=name, out_shape=jax.ShapeDtypeStruct(parts.shape[1:], F32),
        compiler_params=pltpu.CompilerParams(vmem_limit_bytes=32 << 20),
    )(parts)


def _adam_small(name, w, g, m, v):
    def kern(w_ref, g_ref, m_ref, v_ref, d_out, m_out, v_out):
        delta, m2, v2 = _adam(w_ref[...], g_ref[...], m_ref[...], v_ref[...])
        d_out[...] = delta
        m_out[...] = m2
        v_out[...] = v2

    sh = jax.ShapeDtypeStruct(w.shape, F32)
    return pl.pallas_call(kern, name=name, out_shape=[sh] * 3)(w, g, m, v)


class _NoComm:
    def __init__(self):
        self.grads = {}

    def job(self, host):
        return None

    def after(self, host, job):
        pass

    def grad(self, name, g):
        self.grads[name] = g


def _local_step(x, target, wts, vec, comm=None):
    comm = comm or _NoComm()
    seq, d = x.shape
    sv = vec["sv"]
    ff = vec["b_ff1"].shape[1]
    n_in = 5 * d

    def hosted(host, call):
        job = comm.job(host)
        res = call(job)
        comm.after(host, job)
        return res

    tc = min(512, seq)
    t1k, t512, t256 = min(1024, seq), min(512, seq), min(256, seq)
    n512 = min(512, d)
    tkd = d
    tkf = min(1024, ff)
    sub_rows = min(64, seq)

    x_bf = x.astype(BF16)
    full = lambda i, j, k: (0, 0)

    def epi_store(acc, i, j, ex, outs):
        outs[0][...] = acc

    n_pass = W_IN_PASSES
    piece = n_in // (N_DEV * n_pass)
    tz = min(2048, seq)
    z = None
    for k in range(n_pass):
        w_piece = wts["w_in_piece_%d" % k]
        prev = [] if z is None else [(z, (None, None))]
        (z,) = hosted("z_proj_%d" % k, lambda job: _mm(
            "z_proj_%d" % k, "nn", (seq // tz, N_DEV, 1),
            x_bf, ((tz, d), lambda i, j, kk: (i, 0)), w_piece, ((d, piece), lambda i, j, kk: (0, j)),
            prev, [jax.ShapeDtypeStruct((seq, n_in), F32)],
            [((tz, piece), lambda i, j, kk, k=k: (i, n_pass * j + k))], epi_store,
            aliases={2: 0} if prev else None, job=job))

    pool_w, wa, wx = wts["pool_w"], wts["lru_wa"], wts["lru_wx"]
    blk = wa.shape[2]
    y_pool = hosted("pool_fwd", lambda job: _pool_fwd(z, pool_w, vec["pool_scale"], seq, d, tc, job=job))
    xc, h_f = hosted("lru_fwd", lambda job: _lru_fwd(z, sv, vec["conv_b"], wa, wx, seq, d, tc, job=job))
    h_b, y_lru = hosted("lru_rev", lambda job: _lru_rev(z, sv, wa, wx, xc, h_f, seq, d, tc, job=job))
    w_pu, w_lu = wts["w_pool_up"], wts["w_lru_up"]
    m_bf, p_a, p_b = hosted("merge", lambda job: _merge(y_pool, w_pu, y_lru, w_lu, z, seq, d, t1k, n512, job=job))
    w_out = wts["w_out"]

    def epi_ln1(acc, i, j, ex, outs):
        x_ref, bo, g1, b1 = ex
        s1 = DN_ALPHA * x_ref[...] + (acc + bo[...])
        xhat, rstd, x1 = _ln_fwd(s1, g1[...], b1[...])
        outs[0][...] = xhat
        outs[1][...] = x1.astype(BF16)
        outs[2][...] = rstd

    rowd = lambda t: ((t, d), lambda i, j, k: (i, 0))
    vecd = ((1, d), full)
    xhat1, x1_bf, rstd1 = hosted("out_ln1", lambda job: _mm(
        "out_ln1", "nn", (seq // t256, 1, 1), m_bf, rowd(t256), w_out, ((d, d), full),
        [(x, rowd(t256)), (vec["b_out"], vecd), (vec["ln1_g"], vecd), (vec["ln1_b"], vecd)],
        [jax.ShapeDtypeStruct((seq, d), F32), jax.ShapeDtypeStruct((seq, d), BF16),
         jax.ShapeDtypeStruct((seq, 1), F32)],
        [rowd(t256), rowd(t256), ((t256, 1), lambda i, j, k: (i, 0))], epi_ln1, job=job))
    w1 = wts["w_ff1"]

    def epi_ff1(acc, i, j, ex, outs):
        r = jnp.maximum(acc + ex[0][...], 0.0)
        outs[0][...] = r.astype(BF16)
        outs[1][...] = (r * r).astype(BF16)

    tile_f = ((t1k, n512), lambda i, j, k: (i, j))
    relu_h, hdn = hosted("ff1", lambda job: _mm(
        "ff1", "nn", (seq // t1k, ff // n512, 1), x1_bf, rowd(t1k), w1, ((d, n512), lambda i, j, k: (0, j)),
        [(vec["b_ff1"], ((1, n512), lambda i, j, k: (0, j)))],
        [jax.ShapeDtypeStruct((seq, ff), BF16)] * 2, [tile_f, tile_f], epi_ff1, job=job))
    w2 = wts["w_ff2"]

    def epi_ln2(acc, i, j, ex, outs):
        xh1, tgt, g1, b1, bf2, g2, b2 = ex
        ds_ref, dsb_ref, sm_ref, loss_ref = outs
        x1 = xh1[...] * g1[...] + b1[...]
        s2 = DN_ALPHA * x1 + (acc + bf2[...])
        xhat, rstd, y = _ln_fwd(s2, g2[...], b2[...])
        e = y - tgt[...]
        part = 0.5 * jnp.sum(jnp.mean(e * e, axis=-1, keepdims=True))
        dy = e * (1.0 / d)
        ds2 = _ln_bwd(dy, xhat, rstd, g2[...])
        ds_ref[...] = ds2
        dsb_ref[...] = ds2.astype(BF16)
        sm_ref[0:1, :] += _colsum(dy * xhat)
        sm_ref[1:2, :] += _colsum(dy)
        sm_ref[2:3, :] += _colsum(ds2)
        loss_ref[...] += jnp.full(loss_ref.shape, part, F32)

    def zero_tail(n_tail):
        def init(i, j, outs):
            @pl.when(i == 0)
            def _():
                for o in outs[-n_tail:]:
                    o[...] = jnp.zeros_like(o)
        return init

    ds2, ds2_bf, sm_ln2, loss_blk = hosted("ff2_ln2", lambda job: _mm(
        "ff2_ln2", "nn", (seq // t512, 1, ff // tkf), hdn, ((t512, tkf), lambda i, j, k: (i, k)),
        w2, ((tkf, d), lambda i, j, k: (k, 0)),
        [(xhat1, rowd(t512)), (target, rowd(t512)), (vec["ln1_g"], vecd), (vec["ln1_b"], vecd),
         (vec["b_ff2"], vecd), (vec["ln2_g"], vecd), (vec["ln2_b"], vecd)],
        [jax.ShapeDtypeStruct((seq, d), F32), jax.ShapeDtypeStruct((seq, d), BF16),
         jax.ShapeDtypeStruct((SUBLANES, d), F32), jax.ShapeDtypeStruct((SUBLANES, 128), F32)],
        [rowd(t512), rowd(t512), ((SUBLANES, d), full), ((SUBLANES, 128), full)],
        epi_ln2, acc_shape=(t512, d), epi_init=zero_tail(2), sub=sub_rows, job=job))

    tkw = min(2048, seq)

    def dw(name, wname, a, b, m_dim, n_dim, b_spec=None, row0=0):
        tn = min(1024, n_dim)
        tm = next(t for t in (1024, 768, 512, 256, 128) if m_dim % t == 0 and row0 % t == 0)
        b_spec = b_spec or ((tkw, tn), lambda i, j, k: (k, j))
        i0 = row0 // tm
        (out,) = _mm(
            name, "tn", (m_dim // tm, n_dim // tn, seq // tkw),
            a, ((tkw, tm), lambda i, j, k: (k, i0 + i)), b, b_spec, [],
            [jax.ShapeDtypeStruct((m_dim, n_dim), F32)], [((tm, tn), lambda i, j, k: (i, j))],
            epi_store if seq == tkw else None)
        comm.grad(wname, out)
        comm.after(name, None)

    dw("dw_ff2", "w_ff2", hdn, ds2_bf, ff, d)

    def epi_dpre(acc, i, j, ex, outs):
        dpre = acc * (2.0 * ex[0][...].astype(F32))
        outs[0][...] = dpre.astype(BF16)

        @pl.when(i == 0)
        def _():
            outs[1][...] = jnp.zeros_like(outs[1])

        outs[1][0:1, :] += _colsum(dpre)

    dpre, sm_bff1 = hosted("dhdn", lambda job: _mm(
        "dhdn", "nt", (seq // t1k, ff // n512, 1), ds2_bf, rowd(t1k), w2, ((n512, d), lambda i, j, k: (j, 0)),
        [(relu_h, tile_f)],
        [jax.ShapeDtypeStruct((seq, ff), BF16), jax.ShapeDtypeStruct((SUBLANES, ff), F32)],
        [tile_f, ((SUBLANES, n512), lambda i, j, k: (0, j))], epi_dpre, order="ji", job=job))

    dw("dw_ff1", "w_ff1", x1_bf, dpre, d, ff)

    def epi_ln1b(acc, i, j, ex, outs):
        ds2_ref, xh1, rs1, g1 = ex
        ds_ref, dsb_ref, sm_ref = outs
        dy1 = acc + DN_ALPHA * ds2_ref[...]
        xhat = xh1[...]
        ds1 = _ln_bwd(dy1, xhat, rs1[...], g1[...])
        ds_ref[...] = ds1
        dsb_ref[...] = ds1.astype(BF16)
        sm_ref[0:1, :] += _colsum(dy1 * xhat)
        sm_ref[1:2, :] += _colsum(dy1)
        sm_ref[2:3, :] += _colsum(ds1)

    ds1, ds1_bf, sm_ln1 = hosted("dx1_ln1", lambda job: _mm(
        "dx1_ln1", "nt", (seq // t512, 1, ff // tkf), dpre, ((t512, tkf), lambda i, j, k: (i, k)),
        w1, ((d, tkf), lambda i, j, k: (0, k)),
        [(ds2, rowd(t512)), (xhat1, rowd(t512)), (rstd1, ((t512, 1), lambda i, j, k: (i, 0))),
         (vec["ln1_g"], vecd)],
        [jax.ShapeDtypeStruct((seq, d), F32), jax.ShapeDtypeStruct((seq, d), BF16),
         jax.ShapeDtypeStruct((SUBLANES, d), F32)],
        [rowd(t512), rowd(t512), ((SUBLANES, d), full)], epi_ln1b, acc_shape=(t512, d),
        epi_init=zero_tail(1), sub=sub_rows, job=job))

    dw("dw_out", "w_out", m_bf, ds1_bf, d, d)
    n_n = d // n512
    tile_d = ((t512, n512), lambda i, j, k: (i, j))

    def epi_dm(acc, i, j, ex, outs):
        la, lb, pa, pb = ex
        ga, gb = _sig(la[...]), _sig(lb[...])
        outs[0][...] = (acc * ga).astype(BF16)
        outs[1][...] = (acc * gb).astype(BF16)
        outs[2][0] = (acc * pa[...].astype(F32) * ga * (1.0 - ga)).astype(BF16)
        outs[2][1] = (acc * pb[...].astype(F32) * gb * (1.0 - gb)).astype(BF16)

    dp_a, dp_b, dz = _mm(
        "dm", "nt", (seq // t512, n_n, 1), ds1_bf, rowd(t512), w_out, ((n512, d), lambda i, j, k: (j, 0)),
        [(z, ((t512, n512), lambda i, j, k: (i, 3 * n_n + j))),
         (z, ((t512, n512), lambda i, j, k: (i, 4 * n_n + j))), (p_a, tile_d), (p_b, tile_d)],
        [jax.ShapeDtypeStruct((seq, d), BF16), jax.ShapeDtypeStruct((seq, d), BF16),
         jax.ShapeDtypeStruct((5, seq, d), BF16)],
        [tile_d, tile_d, ((2, t512, n512), lambda i, j, k: (0, i, j))], epi_dm)

    dw("dw_pool_up", "w_pool_up", y_pool, dp_a, d, d)
    dw("dw_lru_up", "w_lru_up", y_lru, dp_b, d, d)

    def epi_bf(acc, i, j, ex, outs):
        outs[0][...] = acc.astype(BF16)

    (dy_pool,) = _mm("dy_pool", "nt", (seq // t512, n_n, 1), dp_a, rowd(t512), w_pu,
                     ((n512, d), lambda i, j, k: (j, 0)), [],
                     [jax.ShapeDtypeStruct((seq, d), BF16)], [tile_d], epi_bf)

    def epi_dylru(acc, i, j, ex, outs):
        hf, hb, ug, _ = ex
        u = ug[...]
        outs[0][...] = acc * _gelu(u)
        outs[1][...] = (acc * (hf[...] + hb[...]) * _gelu_grad(u)).astype(BF16)

    dz_in = dz
    dh, dz = hosted("dy_lru", lambda job: _mm(
        "dy_lru", "nt", (seq // t512, n_n, 1), dp_b, rowd(t512), w_lu, ((n512, d), lambda i, j, k: (j, 0)),
        [(h_f, tile_d), (h_b, tile_d), (z, ((t512, n512), lambda i, j, k: (i, 2 * n_n + j))),
         (dz_in, (None, None))],
        [jax.ShapeDtypeStruct((seq, d), F32), jax.ShapeDtypeStruct(dz_in.shape, BF16)],
        [tile_d, ((None, t512, n512), lambda i, j, k: (4, i, j))], epi_dylru, aliases={5: 1}, job=job))

    dz, g_pw, sm_pool = _pool_bwd(z, dy_pool, pool_w, vec["pool_scale"], dz, seq, d, tc)
    comm.grad("pool_w", g_pw)
    dxc0, g_wa0, g_wx0, sm_l0 = hosted("lru_bwd_0", lambda job: _lru_bwd(
        0, xc, dh, h_f, sv, wa, wx, None, seq, d, tc, job=job))
    dxc, g_wa1, g_wx1, sm_l1 = hosted("lru_bwd_1", lambda job: _lru_bwd(
        1, xc, dh, h_b, sv, wa, wx, dxc0, seq, d, tc, job=job))
    comm.grad("lru_wa", jnp.concatenate([g_wa0, g_wa1], axis=0))
    comm.grad("lru_wx", jnp.concatenate([g_wx0, g_wx1], axis=0))
    dz, sm_conv = _conv_bwd(z, dxc, sv, dz, seq, d, tc, blk)

    tnw = min(1024, d)
    per_seg = d // tnw
    seg_spec = ((None, tkw, tnw), lambda i, j, k: ((j // per_seg + 2) % 5, k, j % per_seg))
    lo_rows = 3 * d // 4
    dw("dw_in_lo", "w_in_lo", x_bf, dz, lo_rows, n_in, b_spec=seg_spec)
    dw("dw_in_hi", "w_in_hi", x_bf, dz, d - lo_rows, n_in, b_spec=seg_spec, row0=lo_rows)

    nk = d // tkd

    def epi_dx(acc, i, j, ex, outs):
        outs[0][...] = acc + DN_ALPHA * ex[0][...]

    (grad_x,) = hosted("dx", lambda job: _mm(
        "dx", "nt", (seq // t512, 1, n_in // tkd), dz,
        ((None, t512, tkd), lambda i, j, k: ((k // nk + 2) % 5, i, k % nk)),
        wts["w_in"], ((d, tkd), lambda i, j, k: (0, k)), [(ds1, rowd(t512))],
        [jax.ShapeDtypeStruct((seq, d), F32)], [rowd(t512)], epi_dx, acc_shape=(t512, d), job=job))

    small = {"ln2": sm_ln2, "b_ff1": sm_bff1, "ln1": sm_ln1, "pool": sm_pool, "lru0": sm_l0, "lru1": sm_l1,
             "conv": sm_conv}
    return loss_blk[0, 0], grad_x, small


REP = ("pool_scale", "conv_b", "b_out", "ln1_g", "ln1_b", "b_ff2", "ln2_g", "ln2_b")
SHARDED_SMALL = (("conv_w", 4), ("lru_ba", 2), ("lru_bx", 2), ("lru_lambda", 2))
WEIGHT_ORDER = ("w_in", "pool_w", "pool_scale", "conv_w", "conv_b", "lru_wa", "lru_ba", "lru_wx", "lru_bx",
                "lru_lambda", "w_pool_up", "w_lru_up", "w_out", "b_out", "ln1_g", "ln1_b", "w_ff1", "b_ff1",
                "w_ff2", "b_ff2", "ln2_g", "ln2_b")


def _pad_rows(a, rows):
    return jnp.concatenate([a, jnp.zeros((rows - a.shape[0], a.shape[1]), a.dtype)], axis=0)


def kernel(x, w_in, pool_w, pool_scale, conv_w, conv_b, lru_wa, lru_ba, lru_wx, lru_bx, lru_lambda, w_pool_up, w_lru_up, w_out, b_out, ln1_g, ln1_b, w_ff1, b_ff1, w_ff2, b_ff2, ln2_g, ln2_b, loss_target, m_w_in, m_pool_w, m_pool_scale, m_conv_w, m_conv_b, m_lru_wa, m_lru_ba, m_lru_wx, m_lru_bx, m_lru_lambda, m_w_pool_up, m_w_lru_up, m_w_out, m_b_out, m_ln1_g, m_ln1_b, m_w_ff1, m_b_ff1, m_w_ff2, m_b_ff2, m_ln2_g, m_ln2_b, v_w_in, v_pool_w, v_pool_scale, v_conv_w, v_conv_b, v_lru_wa, v_lru_ba, v_lru_wx, v_lru_bx, v_lru_lambda, v_w_pool_up, v_w_lru_up, v_w_out, v_b_out, v_ln1_g, v_ln1_b, v_w_ff1, v_b_ff1, v_w_ff2, v_b_ff2, v_ln2_g, v_ln2_b):
    args = dict(locals())
    w = {n: args[n] for n in WEIGHT_ORDER}
    mom = {n: args["m_" + n] for n in WEIGHT_ORDER}
    var = {n: args["v_" + n] for n in WEIGHT_ORDER}
    seq, d = x.shape[1], x.shape[2]
    n_heads, blk = lru_wa.shape[2], lru_wa.shape[4]
    n_groups = pool_w.shape[1]
    ff = b_ff1.shape[1]
    cx, cy, cc = _coords()
    me = 4 * cx + 2 * cy + cc
    cidx = jnp.reshape(cc, (1,)).astype(jnp.int32)
    qidx = jnp.reshape(2 * cx + cy, (1,)).astype(jnp.int32)

    fam_of = dict(BIG)
    sviews = {n: _shard_view(w[n], fam) for n, fam in BIG}
    size_of = {n: sviews[n].shape[1] for n, _ in BIG}
    for half in ("w_in_lo", "w_in_hi"):
        fam_of[half], size_of[half] = fam_of["w_in"], size_of["w_in"]
    wts = {}

    def take_gathered(names, arrays):
        for n, g in zip(names, arrays):
            if n == "pool_w":
                g = g.reshape(n_groups, d // n_groups, d // n_groups)
            elif n in ("lru_wa", "lru_wx"):
                g = g.reshape(2, n_heads, blk, blk)
            elif fam_of[n] == "row":
                g = g.reshape(g.shape[1:])
            wts[n] = g

    shard_bf = {n: sviews[n].astype(BF16) for n, _ in BIG}
    piece = sviews["w_in"].shape[1] // W_IN_PASSES
    for k in range(W_IN_PASSES):
        name = "w_in_piece_%d" % k
        shard_bf[name] = shard_bf["w_in"][:, k * piece:(k + 1) * piece]
        fam_of[name] = "col"
    partial = {}

    def gather_job(items, extra=()):
        return _ag_job([shard_bf[n] for n, _ in items] + [e for e, _ in extra],
                       [fam_of[n] for n, _ in items] + [f for _, f in extra],
                       parts=[p for _, p in items] + [None] * len(extra),
                       into=[partial.get(n) if p else None for n, p in items] + [None] * len(extra))

    def take_pieces(items, arrays):
        for (n, p), g in zip(items, arrays):
            if p is None or p[0] + p[2] == p[1]:
                partial.pop(n, None)
                take_gathered([n], [g])
            else:
                partial[n] = g

    launched = [0]

    def on_sequencer(kind, job):
        launched[0] += 1
        return _sequencer_job("sq_%s_%d" % (kind, launched[0]), job, launched[0] % 2)

    class Plan:
        sibling = {"dw_ff2": ("w_ff2",), "dw_ff1": ("w_ff1",), "dw_lru_up": ("w_out", "w_pool_up", "w_lru_up"),
                   "dw_in_lo": ("w_in_lo",), "dw_in_hi": ("w_in_hi", "pool_w", "lru_wa", "lru_wx")}
        chips = {"dw_ff1": ("w_ff2",), "dw_out": ("w_ff1",), "dy_lru": ("w_out", "w_pool_up", "w_lru_up"),
                 "dw_in_hi": ("w_in_lo",), "dx": ("w_in_hi", "pool_w", "lru_wa", "lru_wx")}

        def __init__(self):
            self.grads, self.recv_a, self.parts, self.recv_b = {}, {}, {}, {}

        def grad(self, name, g):
            self.grads[name] = g if fam_of[name] == "col" else g.reshape((-1,) + g.shape[-2:])

        def job(self, host):
            return None

        def after(self, host, job):
            if host in self.chips:
                names = self.chips[host]
                for n in names:
                    self.parts[n] = _rs_add("rs_add_" + n, self.grads[n], self.recv_a[n], fam_of[n], size_of[n], cidx)
                res = on_sequencer("chips", _rs_chips_job([self.parts[n] for n in names]))
                self.recv_b.update(zip(names, res))
            if host in self.sibling:
                names = self.sibling[host]
                res = on_sequencer("sibling", _rs_sibling_job(
                    [self.grads[n] for n in names], [fam_of[n] for n in names], [size_of[n] for n in names]))
                self.recv_a.update(zip(names, res))

    first = (("w_in_piece_0", None),)
    sv_shard = _pad_rows(jnp.concatenate([w[n].reshape(r, -1) for n, r in SHARDED_SMALL], axis=0), 16)
    gathered = on_sequencer("gather", gather_job(first, [(sv_shard, "col")]))
    take_pieces(first, gathered[:-1])
    vec = {n: w[n] for n in REP}
    vec["b_ff1"] = b_ff1
    vec["sv"] = gathered[-1]
    queue = [(("w_in_piece_%d" % k, None),) for k in range(1, W_IN_PASSES)]
    queue += [(("pool_w", None), ("lru_wa", None), ("lru_wx", None)), (("w_pool_up", None), ("w_lru_up", None)),
              (("w_out", None),), (("w_ff1", None),), (("w_ff2", None),), (("w_in", None),)]
    for items in queue:
        take_pieces(items, on_sequencer("gather", gather_job(items)))

    plan = Plan()
    loss_part, grad_x, small = _local_step(x.reshape(seq, d), loss_target.reshape(seq, d), wts, vec, plan)
    loss = lax.psum(loss_part, AXES)

    out_g, out_d, out_m, out_v = {}, {}, {}, {}
    for n, fam in sorted(BIG, key=lambda nf: nf[0] in ("w_in", "pool_w", "lru_wa", "lru_wx")):
        halves = [n + "_lo", n + "_hi"] if n == "w_in" else [n]
        res = _rs_final_adam("adam_" + n, [plan.parts[h] for h in halves], [plan.recv_b[h] for h in halves],
                             sviews[n], _shard_view(mom[n], fam), _shard_view(var[n], fam), fam, qidx)
        out_g[n], out_d[n], out_m[n], out_v[n] = [r.reshape(w[n].shape) for r in res]

    rows = [small["pool"][0:1], small["conv"][4:5], small["ln1"][2:3], small["ln1"][0:1], small["ln1"][1:2],
            small["ln2"][2:3], small["ln2"][0:1], small["ln2"][1:2], small["b_ff1"][0:1].reshape(ff // d, d),
            small["conv"][0:4], small["lru0"][0:1], small["lru1"][0:1], small["lru0"][1:2], small["lru1"][1:2],
            small["lru0"][2:3], small["lru1"][2:3]]
    n_rep = len(REP) + ff // d
    n_rows = n_rep + sum(r for _, r in SHARDED_SMALL)
    pad_rows = -(-n_rows // SUBLANES) * SUBLANES
    packed = _pad_rows(jnp.concatenate(rows, axis=0), pad_rows)
    (all_small,) = _run_job("ag_small", _ag_job([packed], ["lead"]))
    g_small = _sum8("sum_small", all_small)

    def pack_rep(t):
        return jnp.concatenate([t[n] for n in REP] + [t["b_ff1"].reshape(ff // d, d)], axis=0)

    def pack_sh(t):
        return jnp.concatenate([t[n].reshape(r, -1) for n, r in SHARDED_SMALL], axis=0)

    g_rep = g_small[:n_rep]
    cs = d // N_DEV
    g_sh = lax.dynamic_slice_in_dim(g_small[n_rep:n_rows], me * cs, cs, axis=1)
    d_rep, m_rep, v_rep = _adam_small("adam_rep", pack_rep(w), g_rep, pack_rep(mom), pack_rep(var))
    d_sh, m_sh, v_sh = _adam_small("adam_sharded", pack_sh(w), g_sh, pack_sh(mom), pack_sh(var))

    def unpack(rep_t, sh_t, dst):
        for i, n in enumerate(REP):
            dst[n] = rep_t[i:i + 1].reshape(w[n].shape)
        dst["b_ff1"] = rep_t[len(REP):n_rep].reshape(w["b_ff1"].shape)
        r0 = 0
        for n, r in SHARDED_SMALL:
            dst[n] = sh_t[r0:r0 + r].reshape(w[n].shape)
            r0 += r

    unpack(g_rep, g_sh, out_g)
    unpack(d_rep, d_sh, out_d)
    unpack(m_rep, m_sh, out_m)
    unpack(v_rep, v_sh, out_v)

    outs = [loss, grad_x.reshape(x.shape)]
    for t in (out_g, out_d, out_m, out_v):
        outs += [t[n] for n in WEIGHT_ORDER]
    return tuple(outs)
```

```python
import functools

import jax
import jax.numpy as jnp
from jax import lax
from jax.experimental import pallas as pl
from jax.experimental.pallas import tpu as pltpu
from jax.experimental.pallas import tpu_sc as plsc

F32 = jnp.float32
BF16 = jnp.bfloat16
MESH = pl.DeviceIdType.MESH
AXES = ("x", "y", "c")
N_DEV = 8

DN_ALPHA = 2.0 ** 0.25
LN_EPS = 1e-5
LRU_C = 8.0
ADAM_LR = 0.001
ADAM_B1 = 0.9
ADAM_B2 = 0.999
ADAM_EPS = 1e-08
ADAM_WD = 0.01
ADAM_STEP = 10
GELU_C = 0.7978845608028654
GELU_K = 0.044715

W_IN_PASSES = 5
HALO = 16
SUBLANES = 8
VMEM_MB = 56


def _cparams(sem, vmem_mb=VMEM_MB):
    return pltpu.CompilerParams(dimension_semantics=sem, vmem_limit_bytes=vmem_mb << 20)


HBM_SPEC = pl.BlockSpec(memory_space=pl.ANY)


class _Job:
    def __init__(self, ins, outs, sems, start, finish, mid=None):
        self.ins, self.outs, self.sems = list(ins), list(outs), list(sems)
        self.start, self.finish, self.mid = start, finish, mid
        self.alias = {}
        self.mid_frac = 0.75
        self.results = None


_ORDER = []


def _join_jobs(jobs):
    jobs = [j for j in jobs if j is not None]
    if len(jobs) <= 1:
        return jobs[0] if jobs else None

    def split(refs, counts):
        out, p = [], 0
        for n in counts:
            out.append(refs[p:p + n])
            p += n
        return out

    def phase(which):
        def run(ins, outs, sems):
            parts = zip(jobs, split(ins, [len(j.ins) for j in jobs]), split(outs, [len(j.outs) for j in jobs]),
                        split(sems, [len(j.sems) for j in jobs]))
            for j, ji, jo, js in parts:
                fn = getattr(j, which)
                if fn is not None:
                    fn(ji, jo, js)
        return run

    joined = _Job(sum((j.ins for j in jobs), []), sum((j.outs for j in jobs), []), sum((j.sems for j in jobs), []),
                  phase("start"), phase("finish"), phase("mid") if any(j.mid for j in jobs) else None)
    joined.parts = jobs
    joined.mid_frac = max(j.mid_frac for j in jobs if j.mid is not None) if any(j.mid for j in jobs) else 0.75
    i0 = o0 = 0
    for j in jobs:
        joined.alias.update({i0 + i: o0 + o for i, o in j.alias.items()})
        i0 += len(j.ins)
        o0 += len(j.outs)
    return joined


def _job_results(job):
    if job is None:
        return []
    parts = getattr(job, "parts", None)
    if parts is None:
        return [job.results]
    out, p = [], 0
    for j in parts:
        out.append(job.results[p:p + len(j.outs)])
        p += len(j.outs)
    return out


def _pcall(name, body, grid, in_specs, out_specs, out_shape, inputs, scratch=(), aliases=None,
           vmem_mb=VMEM_MB, job=None):
    in_specs, out_specs, out_shape, scratch = list(in_specs), list(out_specs), list(out_shape), list(scratch)
    params = _cparams(("arbitrary",) * len(grid), vmem_mb)
    if job is None:
        token = [t for t in _ORDER if not any(t is a for a in inputs)]
        n_in = len(inputs)

        def ordered(*refs):
            return body(*refs[:n_in], *refs[n_in + len(token):])

        res = pl.pallas_call(ordered, name=name, grid=grid, in_specs=in_specs + [HBM_SPEC] * len(token),
                             out_specs=out_specs, out_shape=out_shape, scratch_shapes=scratch,
                             input_output_aliases=aliases or {}, compiler_params=params)(*inputs, *token)
        _ORDER[:] = [res[0]]
        return res
    n_in, n_out, n_scr = len(inputs), len(out_shape), len(scratch)
    ji, jo = len(job.ins), len(job.outs)
    total = 1
    for g in grid:
        total *= g
    mid_step = int(job.mid_frac * total) if total >= 4 and job.mid_frac < 1.0 else None

    def wrapped(*refs):
        p = 0
        ins = refs[p:p + n_in]
        p += n_in
        jins = refs[p:p + ji]
        p += ji
        outs = refs[p:p + n_out]
        p += n_out
        jouts = refs[p:p + jo]
        p += jo
        scr = refs[p:p + n_scr]
        sems = refs[p + n_scr:]
        step = pl.program_id(0)
        for ax in range(1, len(grid)):
            step = step * grid[ax] + pl.program_id(ax)

        @pl.when(step == 0)
        def _():
            job.start(jins, jouts, sems)

        if job.mid is not None and mid_step is not None:
            @pl.when(step == mid_step)
            def _():
                job.mid(jins, jouts, sems)

        body(*ins, *outs, *scr)

        @pl.when(step == total - 1)
        def _():
            if job.mid is not None and mid_step is None:
                job.mid(jins, jouts, sems)
            job.finish(jins, jouts, sems)

    all_aliases = dict(aliases or {})
    all_aliases.update({n_in + i: n_out + o for i, o in job.alias.items()})
    res = pl.pallas_call(
        wrapped, name=name, grid=grid, in_specs=in_specs + [HBM_SPEC] * ji,
        out_specs=out_specs + [HBM_SPEC] * jo, out_shape=out_shape + job.outs,
        scratch_shapes=scratch + job.sems, input_output_aliases=all_aliases, compiler_params=params,
    )(*inputs, *job.ins)
    job.results = list(res[n_out:])
    return list(res[:n_out])


def _run_job(name, job):
    ji, jo = len(job.ins), len(job.outs)

    def body(*refs):
        jins, jouts, sems = refs[:ji], refs[ji:ji + jo], refs[ji + jo:]
        job.start(jins, jouts, sems)
        if job.mid is not None:
            job.mid(jins, jouts, sems)
        job.finish(jins, jouts, sems)

    res = pl.pallas_call(body, name=name, in_specs=[HBM_SPEC] * ji, out_specs=[HBM_SPEC] * jo,
                         out_shape=job.outs, scratch_shapes=job.sems, input_output_aliases=job.alias)(*job.ins)
    job.results = list(res)
    return job.results


def _sequencer_job(name, job, collective_id):
    ji, jo = len(job.ins), len(job.outs)

    def body(*refs):
        jins, jouts, sems = refs[:ji], refs[ji:ji + jo], refs[ji + jo:]
        barrier = pltpu.get_barrier_semaphore()
        x, y, c = lax.axis_index("x"), lax.axis_index("y"), lax.axis_index("c")
        for r in range(1, N_DEV):
            peer = ((1 - x) if r & 4 else x, (1 - y) if r & 2 else y, (1 - c) if r & 1 else c)
            pl.semaphore_signal(barrier, inc=1, device_id=peer, device_id_type=MESH)
        pl.semaphore_wait(barrier, N_DEV - 1)
        job.start(jins, jouts, sems)
        if job.mid is not None:
            job.mid(jins, jouts, sems)
        job.finish(jins, jouts, sems)

    res = pl.kernel(
        body, name=name, out_type=job.outs, mesh=plsc.ScalarSubcoreMesh(axis_name="sequencer", num_cores=1),
        scratch_types=job.sems, compiler_params=pltpu.CompilerParams(collective_id=collective_id),
    )(*job.ins)
    job.results = list(res)
    return job.results


def _dot(mode, a, b):
    if mode == "nn":
        dims = (((1,), (0,)), ((), ()))
    elif mode == "nt":
        dims = (((1,), (1,)), ((), ()))
    else:
        dims = (((0,), (0,)), ((), ()))
    return lax.dot_general(a, b, dims, preferred_element_type=F32)


def _sig(x):
    return 0.5 * jnp.tanh(0.5 * x) + 0.5


def _gelu(x):
    t = jnp.tanh(GELU_C * (x + GELU_K * x * x * x))
    return 0.5 * x * (1.0 + t)


def _gelu_grad(x):
    x2 = x * x
    t = jnp.tanh(GELU_C * (x + GELU_K * x * x2))
    return 0.5 * (1.0 + t) + 0.5 * x * (1.0 - t * t) * GELU_C * (1.0 + 3.0 * GELU_K * x2)


def _colsum(v):
    return jnp.sum(v, axis=0, keepdims=True)


def _mm(name, mode, grid, a, a_spec, b, b_spec, extras, out_shapes, out_specs, epi, *,
        order="ij", acc_shape=None, aliases=None, vmem_mb=VMEM_MB, epi_init=None, sub=None, job=None):
    gm, gn, gk = grid

    def spec(s):
        bs, f = s
        if f is None:
            return pl.BlockSpec(memory_space=pl.ANY)
        if order == "ij":
            return pl.BlockSpec(bs, lambda i, j, k, f=f: f(i, j, k))
        return pl.BlockSpec(bs, lambda j, i, k, f=f: f(i, j, k))

    ne, no = len(extras), len(out_shapes)

    def kern(*refs):
        a_ref, b_ref = refs[0], refs[1]
        ex = refs[2:2 + ne]
        outs = refs[2 + ne:2 + ne + no]
        if order == "ij":
            i, j = pl.program_id(0), pl.program_id(1)
        else:
            j, i = pl.program_id(0), pl.program_id(1)
        k = pl.program_id(2)
        prod = _dot(mode, a_ref[...], b_ref[...])
        if gk == 1:
            if epi_init is not None:
                epi_init(i, j, outs)
            epi(prod, i, j, ex, outs)
        elif epi is None:
            @pl.when(k == 0)
            def _():
                outs[0][...] = prod

            @pl.when(k > 0)
            def _():
                outs[0][...] += prod
        else:
            acc = refs[-1]

            @pl.when(k == 0)
            def _():
                acc[...] = prod

            @pl.when(k > 0)
            def _():
                acc[...] += prod

            @pl.when(k == gk - 1)
            def _():
                if epi_init is not None:
                    epi_init(i, j, outs)
                if sub is None:
                    epi(acc[...], i, j, ex, outs)
                else:
                    tm = acc_shape[0]

                    def rows_of(r, rs):
                        return r.at[rs, :] if r.shape[0] == tm else r

                    def blk(t, carry):
                        rs = pl.ds(pl.multiple_of(t * sub, sub), sub)
                        epi(acc[rs, :], i, j, [rows_of(r, rs) for r in ex], [rows_of(r, rs) for r in outs])
                        return carry

                    lax.fori_loop(0, tm // sub, blk, 0)

    g = (gm, gn, gk) if order == "ij" else (gn, gm, gk)
    scratch = [pltpu.VMEM(acc_shape, F32)] if gk > 1 and epi is not None else []
    return _pcall(name, kern, g, [spec(a_spec), spec(b_spec)] + [spec(s) for _, s in extras],
                  [spec(s) for s in out_specs], out_shapes, [a, b] + [e for e, _ in extras],
                  scratch=scratch, aliases=aliases, vmem_mb=vmem_mb, job=job)


def _ext(ref, c, n_chunks, tc, seq):
    c0 = pl.multiple_of(c * tc, tc)
    body = ref[pl.ds(c0, tc), :].astype(F32)
    t0 = pl.multiple_of(jnp.maximum(c0 - HALO, 0), HALO)
    b0 = pl.multiple_of(jnp.minimum(c0 + tc, seq - HALO), HALO)
    top = ref[pl.ds(t0, HALO), :].astype(F32)
    bot = ref[pl.ds(b0, HALO), :].astype(F32)
    top = jnp.where(c > 0, top, 0.0)
    bot = jnp.where(c < n_chunks - 1, bot, 0.0)
    return jnp.concatenate([top, body, bot], axis=0)


def _shifted(vext, off, tc):
    n = vext.shape[0]
    r = vext if off == 0 else pltpu.roll(vext, (n - off) % n, 0)
    return r[HALO:HALO + tc]


def _win_sum(vext, g, extra, tc):
    s2 = vext + pltpu.roll(vext, 1, 0)
    s4 = s2 + pltpu.roll(s2, 2, 0)
    s8 = s4 + pltpu.roll(s4, 4, 0)
    s16 = s8 + pltpu.roll(s8, 8, 0)
    outs = [_shifted(s, extra + hw - 1, tc) for s, hw in ((s2, 1), (s4, 2), (s8, 4), (s16, 8))]
    return jnp.where(g == 0, outs[0], jnp.where(g == 1, outs[1], jnp.where(g == 2, outs[2], outs[3])))


def _win_cnt(t, hw, seq):
    return (jnp.minimum(t + hw, seq) - jnp.maximum(t - hw, 0)).astype(F32)


def _pool_d(uext, g, c, tc, seq):
    hw = jnp.left_shift(1, g)
    t = c * tc + lax.broadcasted_iota(jnp.int32, (tc, 1), 0)
    ws = _win_sum(uext, g, 0, tc)
    return ws / _win_cnt(t, hw, seq) - uext[HALO:HALO + tc]


def _scan_tiles(a_ref, b_ref, h_ref, carry_ref, n_tiles, reverse):
    blk = a_ref.shape[1]
    row = lax.broadcasted_iota(jnp.int32, (SUBLANES, blk), 0)

    def tile(j, hc):
        jj = (n_tiles - 1 - j) if reverse else j
        off = pl.multiple_of(jj * SUBLANES, SUBLANES)
        a = a_ref[pl.ds(off, SUBLANES), :]
        b = b_ref[pl.ds(off, SUBLANES), :]
        for kk in (1, 2, 4):
            sh = (SUBLANES - kk) if reverse else kk
            a_s = pltpu.roll(a, sh, 0)
            b_s = pltpu.roll(b, sh, 0)
            m = (row < SUBLANES - kk) if reverse else (row >= kk)
            a_s = jnp.where(m, a_s, 1.0)
            b_s = jnp.where(m, b_s, 0.0)
            b = a * b_s + b
            a = a * a_s
        h = a * hc + b
        h_ref[pl.ds(off, SUBLANES), :] = h
        return h[0:1, :] if reverse else h[SUBLANES - 1:SUBLANES, :]

    group = SUBLANES if n_tiles % SUBLANES == 0 else 1

    def tiles(jg, hc):
        for u in range(group):
            hc = tile(jg * group + u, hc)
        return hc

    hc = lax.fori_loop(0, n_tiles // group, tiles, carry_ref[0:1, :])
    carry_ref[0:1, :] = hc


def _lru_k(lam):
    y = -lam
    e = jnp.exp(-jnp.abs(y))
    u = 1.0 + e
    l1p = jnp.where(u == 1.0, e, jnp.log(u) * (e / (u - 1.0)))
    return -LRU_C * (jnp.maximum(y, 0.0) + l1p)


def _lru_gates(xc, wa, wx, ba, bx, lam):
    xb = xc.astype(BF16)
    r = _sig(jnp.dot(xb, wa, preferred_element_type=F32) + ba)
    i = _sig(jnp.dot(xb, wx, preferred_element_type=F32) + bx)
    k = _lru_k(lam)
    la = k * r
    a = jnp.exp(la)
    s = jnp.sqrt(-jnp.tanh(la) * (a * a + 1.0))
    return r, i, k, a, s


SV_CONV, SV_BA, SV_BX, SV_LAM = 0, 4, 6, 8


def _pool_fwd(z, pw, scale, seq, d, tc, job=None):
    n_g = pw.shape[0]
    pg = d // n_g
    n_chunks = seq // tc

    def kern(z_ref, pw_ref, sc_ref, y_ref):
        g, c = pl.program_id(0), pl.program_id(1)
        uext = _ext(z_ref, c, n_chunks, tc, seq)
        dd = _pool_d(uext, g, c, tc, seq)
        q = jnp.dot(dd.astype(BF16), pw_ref[...], preferred_element_type=F32)
        y_ref[...] = (q * sc_ref[...]).astype(BF16)

    (y,) = _pcall(
        "pool_fwd", kern, (n_g, n_chunks),
        [pl.BlockSpec((seq, pg), lambda g, c: (0, g)),
         pl.BlockSpec((None, pg, pg), lambda g, c: (g, 0, 0)),
         pl.BlockSpec((1, pg), lambda g, c: (0, g))],
        [pl.BlockSpec((tc, pg), lambda g, c: (c, g))],
        [jax.ShapeDtypeStruct((seq, d), BF16)], [z, pw, scale], job=job)
    return y


def _lru_fwd(z, sv, conv_b, wa, wx, seq, d, tc, job=None):
    n_h, blk = wa.shape[1], wa.shape[2]
    n_chunks = seq // tc
    lru_off = d // blk

    def kern(z_ref, sv_ref, cb_ref, wa_ref, wx_ref, xc_ref, h_ref, a_s, b_s, carry):
        c = pl.program_id(1)
        uext = _ext(z_ref, c, n_chunks, tc, seq)
        xc = cb_ref[...]
        for k in range(4):
            xc = xc + _shifted(uext, k - 2, tc) * sv_ref[SV_CONV + k:SV_CONV + k + 1, :]
        xc_ref[...] = xc
        _, i, _, a, s = _lru_gates(xc, wa_ref[...], wx_ref[...], sv_ref[SV_BA:SV_BA + 1, :],
                                   sv_ref[SV_BX:SV_BX + 1, :], sv_ref[SV_LAM:SV_LAM + 1, :])
        a_s[...] = a
        b_s[...] = s * (i * xc)

        @pl.when(c == 0)
        def _():
            carry[...] = jnp.zeros_like(carry)

        _scan_tiles(a_s, b_s, h_ref, carry, tc // SUBLANES, False)

    col = lambda h, c: (c, h)
    return _pcall(
        "lru_fwd", kern, (n_h, n_chunks),
        [pl.BlockSpec((seq, blk), lambda h, c: (0, lru_off + h)),
         pl.BlockSpec((16, blk), lambda h, c: (0, h)),
         pl.BlockSpec((1, blk), lambda h, c: (0, h)),
         pl.BlockSpec((None, None, blk, blk), lambda h, c: (0, h, 0, 0)),
         pl.BlockSpec((None, None, blk, blk), lambda h, c: (0, h, 0, 0))],
        [pl.BlockSpec((tc, blk), col), pl.BlockSpec((tc, blk), col)],
        [jax.ShapeDtypeStruct((seq, d), F32), jax.ShapeDtypeStruct((seq, d), F32)],
        [z, sv, conv_b, wa, wx],
        scratch=[pltpu.VMEM((tc, blk), F32), pltpu.VMEM((tc, blk), F32), pltpu.VMEM((SUBLANES, blk), F32)],
        job=job)


def _lru_rev(z, sv, wa, wx, xc, h_f, seq, d, tc, job=None):
    n_h, blk = wa.shape[1], wa.shape[2]
    n_chunks = seq // tc
    gate_off = 2 * d // blk

    def kern(z_ref, sv_ref, wa_ref, wx_ref, xc_ref, hf_ref, hb_ref, y_ref, a_s, b_s, carry):
        c = pl.program_id(1)
        xc = xc_ref[...]
        _, i, _, a, s = _lru_gates(xc, wa_ref[...], wx_ref[...], sv_ref[SV_BA + 1:SV_BA + 2, :],
                                   sv_ref[SV_BX + 1:SV_BX + 2, :], sv_ref[SV_LAM + 1:SV_LAM + 2, :])
        a_s[...] = a
        b_s[...] = s * (i * xc)

        @pl.when(c == 0)
        def _():
            carry[...] = jnp.zeros_like(carry)

        _scan_tiles(a_s, b_s, hb_ref, carry, tc // SUBLANES, True)
        y_ref[...] = ((hf_ref[...] + hb_ref[...]) * _gelu(z_ref[...])).astype(BF16)

    rev = lambda h, c: (n_chunks - 1 - c, h)
    return _pcall(
        "lru_rev", kern, (n_h, n_chunks),
        [pl.BlockSpec((tc, blk), lambda h, c: (n_chunks - 1 - c, gate_off + h)),
         pl.BlockSpec((16, blk), lambda h, c: (0, h)),
         pl.BlockSpec((None, None, blk, blk), lambda h, c: (1, h, 0, 0)),
         pl.BlockSpec((None, None, blk, blk), lambda h, c: (1, h, 0, 0)),
         pl.BlockSpec((tc, blk), rev), pl.BlockSpec((tc, blk), rev)],
        [pl.BlockSpec((tc, blk), rev), pl.BlockSpec((tc, blk), rev)],
        [jax.ShapeDtypeStruct((seq, d), F32), jax.ShapeDtypeStruct((seq, d), BF16)],
        [z, sv, wa, wx, xc, h_f],
        scratch=[pltpu.VMEM((tc, blk), F32), pltpu.VMEM((tc, blk), F32), pltpu.VMEM((SUBLANES, blk), F32)],
        job=job)


def _merge(y_pool, w_pu, y_lru, w_lu, z, seq, d, tm, tn, job=None):
    n_n = d // tn

    def kern(yp_ref, wp_ref, yl_ref, wl_ref, la_ref, lb_ref, m_ref, pa_ref, pb_ref):
        pa = jnp.dot(yp_ref[...], wp_ref[...], preferred_element_type=F32)
        pb = jnp.dot(yl_ref[...], wl_ref[...], preferred_element_type=F32)
        m_ref[...] = (_sig(la_ref[...]) * pa + _sig(lb_ref[...]) * pb).astype(BF16)
        pa_ref[...] = pa.astype(BF16)
        pb_ref[...] = pb.astype(BF16)

    row = pl.BlockSpec((tm, d), lambda i, j: (i, 0))
    wcol = pl.BlockSpec((d, tn), lambda i, j: (0, j))
    out = pl.BlockSpec((tm, tn), lambda i, j: (i, j))
    sh = jax.ShapeDtypeStruct((seq, d), BF16)
    return _pcall(
        "merge", kern, (seq // tm, n_n),
        [row, wcol, row, wcol,
         pl.BlockSpec((tm, tn), lambda i, j: (i, 3 * n_n + j)),
         pl.BlockSpec((tm, tn), lambda i, j: (i, 4 * n_n + j))],
        [out, out, out], [sh, sh, sh], [y_pool, w_pu, y_lru, w_lu, z, z], job=job)


def _ln_fwd(s, g, b):
    mu = jnp.mean(s, axis=-1, keepdims=True)
    xc = s - mu
    var = jnp.mean(xc * xc, axis=-1, keepdims=True)
    rstd = lax.rsqrt(var + LN_EPS)
    xhat = xc * rstd
    return xhat, rstd, xhat * g + b


def _ln_bwd(dy, xhat, rstd, g):
    dyg = dy * g
    m1 = jnp.mean(dyg, axis=-1, keepdims=True)
    m2 = jnp.mean(dyg * xhat, axis=-1, keepdims=True)
    return rstd * (dyg - m1 - xhat * m2)


def _pool_bwd(z, dy_pool, pw, scale, dz, seq, d, tc):
    n_g = pw.shape[0]
    pg = d // n_g
    n_chunks = seq // tc

    def kern(z_ref, dy_ref, pw_ref, sc_ref, dz_in, dz_ref, dpw_ref, dsc_ref):
        del dz_in
        g, c = pl.program_id(0), pl.program_id(1)
        hw = jnp.left_shift(1, g)
        uext = _ext(z_ref, c, n_chunks, tc, seq)
        dd = _pool_d(uext, g, c, tc, seq).astype(BF16)
        pwv = pw_ref[...]
        q = jnp.dot(dd, pwv, preferred_element_type=F32)
        dyext = _ext(dy_ref, c, n_chunks, tc, seq)

        @pl.when(c == 0)
        def _():
            dsc_ref[...] = jnp.zeros_like(dsc_ref)
            dpw_ref[...] = jnp.zeros_like(dpw_ref)

        dsc_ref[0:1, :] += _colsum(dyext[HALO:HALO + tc] * q)
        dqext = (dyext * sc_ref[...]).astype(BF16)
        dpw_ref[...] += _dot("tn", dd, dqext[HALO:HALO + tc])
        ddext = _dot("nt", dqext, pwv)
        text = c * tc - HALO + lax.broadcasted_iota(jnp.int32, (tc + 2 * HALO, 1), 0)
        v = ddext / jnp.maximum(_win_cnt(text, hw, seq), 1.0)
        dz_ref[...] = (_win_sum(v, g, 1, tc) - ddext[HALO:HALO + tc]).astype(BF16)

    return _pcall(
        "pool_bwd", kern, (n_g, n_chunks),
        [pl.BlockSpec((seq, pg), lambda g, c: (0, g)),
         pl.BlockSpec((seq, pg), lambda g, c: (0, g)),
         pl.BlockSpec((None, pg, pg), lambda g, c: (g, 0, 0)),
         pl.BlockSpec((1, pg), lambda g, c: (0, g)),
         pl.BlockSpec(memory_space=pl.ANY)],
        [pl.BlockSpec((None, tc, pg), lambda g, c: (2, c, g)),
         pl.BlockSpec((None, pg, pg), lambda g, c: (g, 0, 0)),
         pl.BlockSpec((SUBLANES, pg), lambda g, c: (0, g))],
        [jax.ShapeDtypeStruct(dz.shape, dz.dtype),
         jax.ShapeDtypeStruct((n_g, pg, pg), F32),
         jax.ShapeDtypeStruct((SUBLANES, d), F32)],
        [z, dy_pool, pw, scale, dz], aliases={4: 0})


def _lru_bwd(direction, xc, dh, h_dir, sv, wa, wx, dxc_prev, seq, d, tc, job=None):
    reverse = direction == 1
    n_h, blk = wa.shape[1], wa.shape[2]
    n_chunks = seq // tc
    has_prev = dxc_prev is not None

    def kern(*refs):
        xc_ref, dh_ref, h_ref, sv_ref, wa_ref, wx_ref = refs[:6]
        p = 6
        prev_ref = None
        if has_prev:
            prev_ref = refs[p]
            p += 1
        dxc_ref, dwa_ref, dwx_ref, sm_ref, at_s, g_s, carry, acarry = refs[p:p + 8]
        c = pl.program_id(1)
        cr = c if reverse else n_chunks - 1 - c
        c0 = pl.multiple_of(cr * tc, tc)

        @pl.when(c == 0)
        def _():
            carry[...] = jnp.zeros_like(carry)
            acarry[...] = jnp.zeros_like(acarry)
            dwa_ref[...] = jnp.zeros_like(dwa_ref)
            dwx_ref[...] = jnp.zeros_like(dwx_ref)
            sm_ref[...] = jnp.zeros_like(sm_ref)

        xc = xc_ref[...]
        wav, wxv = wa_ref[...], wx_ref[...]
        lam = sv_ref[SV_LAM + direction:SV_LAM + direction + 1, :]
        r, i, k, a, s = _lru_gates(xc, wav, wxv, sv_ref[SV_BA + direction:SV_BA + direction + 1, :],
                                   sv_ref[SV_BX + direction:SV_BX + direction + 1, :], lam)
        rowi = lax.broadcasted_iota(jnp.int32, (tc, blk), 0)
        hbody = h_ref[pl.ds(c0, tc), :]
        if not reverse:
            p0 = pl.multiple_of(jnp.maximum(c0 - SUBLANES, 0), SUBLANES)
            edge = jnp.where(cr > 0, h_ref[pl.ds(p0, SUBLANES), :][SUBLANES - 1:SUBLANES, :], 0.0)
            hprev = jnp.where(rowi == 0, edge, pltpu.roll(hbody, 1, 0))
            at = jnp.where(rowi == tc - 1, acarry[0:1, :], pltpu.roll(a, tc - 1, 0))
        else:
            n0 = pl.multiple_of(jnp.minimum(c0 + tc, seq - SUBLANES), SUBLANES)
            edge = jnp.where(cr < n_chunks - 1, h_ref[pl.ds(n0, SUBLANES), :][0:1, :], 0.0)
            hprev = jnp.where(rowi == tc - 1, edge, pltpu.roll(hbody, tc - 1, 0))
            at = jnp.where(rowi == 0, acarry[0:1, :], pltpu.roll(a, 1, 0))
        at_s[...] = at
        _scan_tiles(at_s, dh_ref, g_s, carry, tc // SUBLANES, not reverse)
        acarry[0:1, :] = a[tc - 1:tc, :] if reverse else a[0:1, :]

        gt = g_s[...]
        da = gt * hprev
        di = gt * s * xc
        dxc = gt * s * i
        ds = gt * (i * xc)
        dl = da * a - ds * (a * a) / s
        dpr = (dl * k) * r * (1.0 - r)
        dpi = di * i * (1.0 - i)
        sm_ref[0:1, :] += _colsum(dpr)
        sm_ref[1:2, :] += _colsum(dpi)
        sm_ref[2:3, :] += _colsum(dl * r) * (LRU_C * _sig(-lam))
        xb, dprb, dpib = xc.astype(BF16), dpr.astype(BF16), dpi.astype(BF16)
        dwa_ref[...] += _dot("tn", xb, dprb)
        dwx_ref[...] += _dot("tn", xb, dpib)
        dxc = dxc + _dot("nt", dprb, wav) + _dot("nt", dpib, wxv)
        if has_prev:
            dxc = dxc + prev_ref[...]
        dxc_ref[...] = dxc

    if reverse:
        chunk = lambda h, c: (c, h)
    else:
        chunk = lambda h, c: (n_chunks - 1 - c, h)
    wspec = pl.BlockSpec((None, None, blk, blk), lambda h, c: (direction, h, 0, 0))
    ins = [xc, dh, h_dir, sv, wa, wx] + ([dxc_prev] if has_prev else [])
    in_specs = [pl.BlockSpec((tc, blk), chunk), pl.BlockSpec((tc, blk), chunk),
                pl.BlockSpec((seq, blk), lambda h, c: (0, h)),
                pl.BlockSpec((16, blk), lambda h, c: (0, h)), wspec, wspec]
    if has_prev:
        in_specs.append(pl.BlockSpec((tc, blk), chunk))
    return _pcall(
        "lru_bwd_%d" % direction, kern, (n_h, n_chunks), in_specs,
        [pl.BlockSpec((tc, blk), chunk),
         pl.BlockSpec((None, blk, blk), lambda h, c: (h, 0, 0)),
         pl.BlockSpec((None, blk, blk), lambda h, c: (h, 0, 0)),
         pl.BlockSpec((SUBLANES, blk), lambda h, c: (0, h))],
        [jax.ShapeDtypeStruct((seq, d), F32),
         jax.ShapeDtypeStruct((n_h, blk, blk), F32),
         jax.ShapeDtypeStruct((n_h, blk, blk), F32),
         jax.ShapeDtypeStruct((SUBLANES, d), F32)],
        ins,
        scratch=[pltpu.VMEM((tc, blk), F32), pltpu.VMEM((tc, blk), F32),
                 pltpu.VMEM((SUBLANES, blk), F32), pltpu.VMEM((SUBLANES, blk), F32)],
        job=job)


def _conv_bwd(z, dxc, sv, dz, seq, d, tc, tcol):
    n_chunks = seq // tc
    lru_off = d // tcol

    def kern(z_ref, dx_ref, sv_ref, dz_in, dz_ref, sm_ref):
        del dz_in
        c = pl.program_id(1)
        uext = _ext(z_ref, c, n_chunks, tc, seq)
        dext = _ext(dx_ref, c, n_chunks, tc, seq)
        dbody = dext[HALO:HALO + tc]

        @pl.when(c == 0)
        def _():
            sm_ref[...] = jnp.zeros_like(sm_ref)

        du = jnp.zeros_like(dbody)
        for k in range(4):
            du = du + _shifted(dext, 2 - k, tc) * sv_ref[SV_CONV + k:SV_CONV + k + 1, :]
            sm_ref[k:k + 1, :] += _colsum(dbody * _shifted(uext, k - 2, tc))
        sm_ref[4:5, :] += _colsum(dbody)
        dz_ref[...] = du.astype(BF16)

    return _pcall(
        "conv_bwd", kern, (d // tcol, n_chunks),
        [pl.BlockSpec((seq, tcol), lambda j, c: (0, lru_off + j)),
         pl.BlockSpec((seq, tcol), lambda j, c: (0, j)),
         pl.BlockSpec((16, tcol), lambda j, c: (0, j)),
         pl.BlockSpec(memory_space=pl.ANY)],
        [pl.BlockSpec((None, tc, tcol), lambda j, c: (3, c, j)),
         pl.BlockSpec((SUBLANES, tcol), lambda j, c: (0, j))],
        [jax.ShapeDtypeStruct(dz.shape, dz.dtype), jax.ShapeDtypeStruct((SUBLANES, d), F32)],
        [z, dxc, sv, dz], aliases={3: 0})


BIG = (("w_in", "col"), ("pool_w", "row"), ("lru_wa", "row"), ("lru_wx", "row"), ("w_pool_up", "row"),
       ("w_lru_up", "row"), ("w_out", "row"), ("w_ff1", "col"), ("w_ff2", "row"))


def _shard_view(w, fam):
    if fam == "col":
        return w.reshape(w.shape[-2:])
    return w.reshape((-1,) + w.shape[-2:])


def _full_shape(sv_shape, fam):
    if fam == "col":
        return (sv_shape[0], N_DEV * sv_shape[1])
    return (sv_shape[0], N_DEV * sv_shape[1], sv_shape[2])


def _slot(ref, fam, p, size, part=None):
    if fam == "lead":
        return ref.at[p]
    k, n, span = part or (0, 1, 1)
    unit = size // n
    start = pl.multiple_of(p * size + k * unit, unit)
    if fam == "col":
        return ref.at[:, pl.ds(start, span * unit)]
    return ref.at[:, pl.ds(start, span * unit), :]


def _shard_part(ref, fam, size, part):
    if part is None or fam == "lead":
        return ref
    k, n, span = part
    unit = size // n
    if fam == "col":
        return ref.at[:, pl.ds(k * unit, span * unit)]
    return ref.at[:, pl.ds(k * unit, span * unit), :]


def _shard_extent(shape, fam):
    return shape[1]


def _coords():
    return lax.axis_index("x"), lax.axis_index("y"), lax.axis_index("c")


def _ag_job(shards, fams, parts=None, into=None):
    n = len(shards)
    parts = list(parts) if parts is not None else [None] * n
    into = list(into) if into is not None else [None] * n
    fulls = []
    for s, fam in zip(shards, fams):
        if fam == "lead":
            fulls.append(jax.ShapeDtypeStruct((N_DEV,) + s.shape, s.dtype))
        else:
            fulls.append(jax.ShapeDtypeStruct(_full_shape(s.shape, fam), s.dtype))
    sizes = [1 if fam == "lead" else _shard_extent(s.shape, fam) for s, fam in zip(shards, fams)]
    given = [a for a in range(n) if into[a] is not None]

    def ctx(ins, outs, sems):
        send, recv, loc = sems
        x, y, c = _coords()
        chips = [(1 - x, y), (x, 1 - y), (1 - x, 1 - y)]

        def mine(a):
            return _shard_part(ins[a], fams[a], sizes[a], parts[a])

        def copy(a, k, owner, to, src=None):
            dst = _slot(outs[a], fams[a], owner, sizes[a], parts[a])
            return pltpu.make_async_remote_copy(
                src_ref=dst if src is None else src, dst_ref=dst,
                send_sem=send.at[a, k], recv_sem=recv.at[a, k], device_id=to, device_id_type=MESH)

        def local(a):
            return pltpu.make_async_copy(
                mine(a), _slot(outs[a], fams[a], 4 * x + 2 * y + c, sizes[a], parts[a]), loc.at[a])

        return x, y, c, chips, copy, local, mine

    def start(ins, outs, sems):
        x, y, c, chips, copy, local, mine = ctx(ins, outs, sems)
        me = 4 * x + 2 * y + c
        for a in range(n):
            local(a).start()
            copy(a, 0, me, (x, y, 1 - c), mine(a)).start()
            for j, (cx, cy) in enumerate(chips):
                copy(a, 1 + j, me, (cx, cy, c), mine(a)).start()

    def mid(ins, outs, sems):
        x, y, c, chips, copy, _, _ = ctx(ins, outs, sems)
        for a in range(n):
            for j, (cx, cy) in enumerate(chips):
                owner = 4 * cx + 2 * cy + c
                copy(a, 1 + j, owner, (x, y, c)).wait_recv()
                copy(a, 4 + j, owner, (x, y, 1 - c)).start()

    def finish(ins, outs, sems):
        x, y, c, chips, copy, local, mine = ctx(ins, outs, sems)
        me = 4 * x + 2 * y + c
        for a in range(n):
            copy(a, 0, 4 * x + 2 * y + (1 - c), (x, y, c)).wait_recv()
            for j, (cx, cy) in enumerate(chips):
                copy(a, 4 + j, 4 * cx + 2 * cy + (1 - c), (x, y, c)).wait_recv()
            copy(a, 0, me, (x, y, 1 - c), mine(a)).wait_send()
            for j, (cx, cy) in enumerate(chips):
                copy(a, 1 + j, me, (cx, cy, c), mine(a)).wait_send()
                copy(a, 4 + j, 4 * cx + 2 * cy + c, (x, y, 1 - c)).wait_send()
            local(a).wait()

    sems = [pltpu.SemaphoreType.DMA((n, 7)), pltpu.SemaphoreType.DMA((n, 7)), pltpu.SemaphoreType.DMA((n,))]
    job = _Job(list(shards) + [into[a] for a in given], fulls, sems, start, finish, mid)
    job.alias = {n + i: a for i, a in enumerate(given)}
    return job


def _rs_sibling_job(fulls, fams, sizes):
    n = len(fulls)
    outs = []
    for f, fam, sz in zip(fulls, fams, sizes):
        if fam == "col":
            outs.append(jax.ShapeDtypeStruct((4, f.shape[0], sz), f.dtype))
        else:
            outs.append(jax.ShapeDtypeStruct((4, f.shape[0], sz, f.shape[2]), f.dtype))

    def copies(ins, rcv, sems):
        send, recv = sems
        x, y, c = _coords()
        return [pltpu.make_async_remote_copy(
            src_ref=_slot(ins[a], fams[a], 2 * q + (1 - c), sizes[a]), dst_ref=rcv[a].at[q],
            send_sem=send.at[a, q], recv_sem=recv.at[a, q], device_id=(x, y, 1 - c), device_id_type=MESH)
            for a in range(n) for q in range(4)]

    def start(ins, rcv, sems):
        for cp in copies(ins, rcv, sems):
            cp.start()

    def finish(ins, rcv, sems):
        for cp in copies(ins, rcv, sems):
            cp.wait()

    return _Job(fulls, outs, [pltpu.SemaphoreType.DMA((n, 4)), pltpu.SemaphoreType.DMA((n, 4))], start, finish)


def _rs_chips_job(parts):
    n = len(parts)
    outs = [jax.ShapeDtypeStruct((3,) + p.shape[1:], p.dtype) for p in parts]

    def copies(ins, rcv, sems):
        send, recv = sems
        x, y, c = _coords()
        cps = []
        for a in range(n):
            for r in (1, 2, 3):
                tx, ty = (1 - x) if r & 2 else x, (1 - y) if r & 1 else y
                cps.append(pltpu.make_async_remote_copy(
                    src_ref=ins[a].at[2 * tx + ty], dst_ref=rcv[a].at[r - 1],
                    send_sem=send.at[a, r - 1], recv_sem=recv.at[a, r - 1],
                    device_id=(tx, ty, c), device_id_type=MESH))
        return cps

    def start(ins, rcv, sems):
        for cp in copies(ins, rcv, sems):
            cp.start()

    def finish(ins, rcv, sems):
        for cp in copies(ins, rcv, sems):
            cp.wait()

    return _Job(parts, outs, [pltpu.SemaphoreType.DMA((n, 3)), pltpu.SemaphoreType.DMA((n, 3))], start, finish)


def _tile_rows(rows, cols):
    tr = rows
    while tr * cols > (1 << 18) and tr % (2 * SUBLANES) == 0:
        tr //= 2
    return tr


def _rs_add(name, full, recv_a, fam, size, cidx):
    if fam == "col":
        rows = full.shape[0]
        tr = _tile_rows(rows, size)
        grid = (4, rows // tr)
        f_spec = pl.BlockSpec((tr, size), lambda q, i, cr: (i, 2 * q + cr[0]))
        s_spec = pl.BlockSpec((None, tr, size), lambda q, i, cr: (q, i, 0))
    else:
        nb, cols = full.shape[0], full.shape[2]
        tr = _tile_rows(nb * size, cols) // nb if nb > 1 else _tile_rows(size, cols)
        nt = size // tr
        grid = (4, nt)
        f_spec = pl.BlockSpec((nb, tr, cols), lambda q, i, cr: (0, (2 * q + cr[0]) * nt + i, 0))
        s_spec = pl.BlockSpec((None, nb, tr, cols), lambda q, i, cr: (q, 0, i, 0))

    token = [t for t in _ORDER if t is not full]

    def kern(c_ref, f_ref, r_ref, *rest):
        del c_ref
        rest[-1][...] = (f_ref[...] + r_ref[...]).astype(BF16)

    out = pl.pallas_call(
        kern, name=name,
        grid_spec=pltpu.PrefetchScalarGridSpec(num_scalar_prefetch=1, grid=grid,
                                               in_specs=[f_spec, s_spec] + [HBM_SPEC] * len(token), out_specs=s_spec),
        out_shape=jax.ShapeDtypeStruct(recv_a.shape, BF16),
        compiler_params=_cparams(("arbitrary", "arbitrary"), 32),
    )(cidx, full, recv_a, *token)
    _ORDER[:] = [out]
    return out


def _adam(w, g, m, v):
    m2 = ADAM_B1 * m + (1.0 - ADAM_B1) * g
    v2 = ADAM_B2 * v + (1.0 - ADAM_B2) * (g * g)
    m_hat = m2 / (1.0 - ADAM_B1 ** ADAM_STEP)
    v_hat = v2 / (1.0 - ADAM_B2 ** ADAM_STEP)
    delta = -ADAM_LR * (m_hat / (jnp.sqrt(v_hat) + ADAM_EPS) + ADAM_WD * w)
    return delta, m2, v2


def _rs_final_adam(name, parts, recv_b, w, m, v, fam, qidx):
    shp = w.shape
    pieces = parts if isinstance(parts, (list, tuple)) else [parts]
    recvs = recv_b if isinstance(recv_b, (list, tuple)) else [recv_b]
    n_p = len(pieces)
    first_blk = [0] * n_p
    if fam == "col":
        rows, cols = shp
        tr = _tile_rows(min(p.shape[1] for p in pieces), cols)
        per = [p.shape[1] // tr for p in pieces]
        for h in range(1, n_p):
            first_blk[h] = first_blk[h - 1] + per[h - 1]
        grid = (rows // tr,)
        w_spec = pl.BlockSpec((tr, cols), lambda i, qr: (i, 0))

        def piece_row(i, h):
            return jnp.clip(i - first_blk[h], 0, per[h] - 1)

        p_specs = [pl.BlockSpec((None, tr, cols), lambda i, qr, h=h: (qr[0], piece_row(i, h), 0))
                   for h in range(n_p)]
        r_specs = [pl.BlockSpec((3, tr, cols), lambda i, qr, h=h: (0, piece_row(i, h), 0)) for h in range(n_p)]
    else:
        assert n_p == 1
        nb, rows, cols = shp
        tr = _tile_rows(rows, cols)
        nt = rows // tr
        grid = (nb * nt,)
        w_spec = pl.BlockSpec((None, tr, cols), lambda i, qr: (i // nt, i % nt, 0))
        p_specs = [pl.BlockSpec((None, None, tr, cols), lambda i, qr: (qr[0], i // nt, i % nt, 0))]
        r_specs = [pl.BlockSpec((3, None, tr, cols), lambda i, qr: (0, i // nt, i % nt, 0))]

    token = list(_ORDER)

    def kern(*refs):
        p_refs, r_refs = refs[1:1 + n_p], refs[1 + n_p:1 + 2 * n_p]
        w_ref, m_ref, v_ref = refs[1 + 2 * n_p:4 + 2 * n_p]
        g_out, d_out, m_out, v_out = refs[4 + 2 * n_p + len(token):]

        def total(h):
            p_ref, r_ref = p_refs[h], r_refs[h]
            return ((p_ref[...].astype(F32) + r_ref[0].astype(F32)) + r_ref[1].astype(F32)) + r_ref[2].astype(F32)

        g = total(0)
        for h in range(1, n_p):
            g = jnp.where(pl.program_id(0) >= first_blk[h], total(h), g)
        delta, m2, v2 = _adam(w_ref[...], g, m_ref[...], v_ref[...])
        g_out[...] = g
        d_out[...] = delta
        m_out[...] = m2
        v_out[...] = v2

    sh = jax.ShapeDtypeStruct(shp, F32)
    res = pl.pallas_call(
        kern, name=name,
        grid_spec=pltpu.PrefetchScalarGridSpec(
            num_scalar_prefetch=1, grid=grid,
            in_specs=p_specs + r_specs + [w_spec, w_spec, w_spec] + [HBM_SPEC] * len(token),
            out_specs=[w_spec] * 4),
        out_shape=[sh] * 4,
        compiler_params=_cparams(("arbitrary",), 32),
    )(qidx, *pieces, *recvs, w, m, v, *token)
    _ORDER[:] = [res[1]]
    return res


def _sum8(name, parts):
    def kern(p_ref, o_ref):
        acc = p_ref[0]
        for p in range(1, N_DEV):
            acc = acc + p_ref[p]
        o_ref[...] = acc

    return pl.pallas_call(
        kern, name=name, out_shape=jax.ShapeDtypeStruct(parts.shape[1:], F32),
        compiler_params=pltpu.CompilerParams(vmem_limit_bytes=32 << 20),
    )(parts)


def _adam_small(name, w, g, m, v):
    def kern(w_ref, g_ref, m_ref, v_ref, d_out, m_out, v_out):
        delta, m2, v2 = _adam(w_ref[...], g_ref[...], m_ref[...], v_ref[...])
        d_out[...] = delta
        m_out[...] = m2
        v_out[...] = v2

    sh = jax.ShapeDtypeStruct(w.shape, F32)
    return pl.pallas_call(kern, name=name, out_shape=[sh] * 3)(w, g, m, v)


class _NoComm:
    def __init__(self):
        self.grads = {}

    def job(self, host):
        return None

    def settled_before(self, host):
        return []

    def after(self, host, job):
        pass

    def grad(self, name, g):
        self.grads[name] = g


def _local_step(x, target, wts, vec, comm=None):
    comm = comm or _NoComm()
    _ORDER[:] = []
    seq, d = x.shape
    sv = vec["sv"]
    ff = vec["b_ff1"].shape[1]
    n_in = 5 * d

    def hosted(host, call):
        _ORDER.extend(comm.settled_before(host))
        job = comm.job(host)
        res = call(job)
        comm.after(host, job)
        return res

    tc = min(512, seq)
    t1k, t512, t256 = min(1024, seq), min(512, seq), min(256, seq)
    n512 = min(512, d)
    tkd = d
    tkf = min(1024, ff)
    sub_rows = min(64, seq)

    x_bf = x.astype(BF16)
    full = lambda i, j, k: (0, 0)

    def epi_store(acc, i, j, ex, outs):
        outs[0][...] = acc

    n_pass = W_IN_PASSES
    piece = n_in // (N_DEV * n_pass)
    tz = min(2048, seq)
    z = None
    for k in range(n_pass):
        w_piece = wts["w_in_piece_%d" % k]
        prev = [] if z is None else [(z, (None, None))]
        (z,) = hosted("z_proj_%d" % k, lambda job: _mm(
            "z_proj_%d" % k, "nn", (seq // tz, N_DEV, 1),
            x_bf, ((tz, d), lambda i, j, kk: (i, 0)), w_piece, ((d, piece), lambda i, j, kk: (0, j)),
            prev, [jax.ShapeDtypeStruct((seq, n_in), F32)],
            [((tz, piece), lambda i, j, kk, k=k: (i, n_pass * j + k))], epi_store,
            aliases={2: 0} if prev else None, job=job))

    pool_w, wa, wx = wts["pool_w"], wts["lru_wa"], wts["lru_wx"]
    blk = wa.shape[2]
    y_pool = hosted("pool_fwd", lambda job: _pool_fwd(z, pool_w, vec["pool_scale"], seq, d, tc, job=job))
    xc, h_f = hosted("lru_fwd", lambda job: _lru_fwd(z, sv, vec["conv_b"], wa, wx, seq, d, tc, job=job))
    h_b, y_lru = hosted("lru_rev", lambda job: _lru_rev(z, sv, wa, wx, xc, h_f, seq, d, tc, job=job))
    w_pu, w_lu = wts["w_pool_up"], wts["w_lru_up"]
    m_bf, p_a, p_b = hosted("merge", lambda job: _merge(y_pool, w_pu, y_lru, w_lu, z, seq, d, t1k, n512, job=job))
    w_out = wts["w_out"]

    def epi_ln1(acc, i, j, ex, outs):
        x_ref, bo, g1, b1 = ex
        s1 = DN_ALPHA * x_ref[...] + (acc + bo[...])
        xhat, rstd, x1 = _ln_fwd(s1, g1[...], b1[...])
        outs[0][...] = xhat
        outs[1][...] = x1.astype(BF16)
        outs[2][...] = rstd

    rowd = lambda t: ((t, d), lambda i, j, k: (i, 0))
    vecd = ((1, d), full)
    xhat1, x1_bf, rstd1 = hosted("out_ln1", lambda job: _mm(
        "out_ln1", "nn", (seq // t256, 1, 1), m_bf, rowd(t256), w_out, ((d, d), full),
        [(x, rowd(t256)), (vec["b_out"], vecd), (vec["ln1_g"], vecd), (vec["ln1_b"], vecd)],
        [jax.ShapeDtypeStruct((seq, d), F32), jax.ShapeDtypeStruct((seq, d), BF16),
         jax.ShapeDtypeStruct((seq, 1), F32)],
        [rowd(t256), rowd(t256), ((t256, 1), lambda i, j, k: (i, 0))], epi_ln1, job=job))
    w1 = wts["w_ff1"]

    def epi_ff1(acc, i, j, ex, outs):
        r = jnp.maximum(acc + ex[0][...], 0.0)
        outs[0][...] = r.astype(BF16)
        outs[1][...] = (r * r).astype(BF16)

    tile_f = ((t1k, n512), lambda i, j, k: (i, j))
    relu_h, hdn = hosted("ff1", lambda job: _mm(
        "ff1", "nn", (seq // t1k, ff // n512, 1), x1_bf, rowd(t1k), w1, ((d, n512), lambda i, j, k: (0, j)),
        [(vec["b_ff1"], ((1, n512), lambda i, j, k: (0, j)))],
        [jax.ShapeDtypeStruct((seq, ff), BF16)] * 2, [tile_f, tile_f], epi_ff1, job=job))
    w2 = wts["w_ff2"]

    def epi_ln2(acc, i, j, ex, outs):
        xh1, tgt, g1, b1, bf2, g2, b2 = ex
        ds_ref, dsb_ref, sm_ref, loss_ref = outs
        x1 = xh1[...] * g1[...] + b1[...]
        s2 = DN_ALPHA * x1 + (acc + bf2[...])
        xhat, rstd, y = _ln_fwd(s2, g2[...], b2[...])
        e = y - tgt[...]
        part = 0.5 * jnp.sum(jnp.mean(e * e, axis=-1, keepdims=True))
        dy = e * (1.0 / d)
        ds2 = _ln_bwd(dy, xhat, rstd, g2[...])
        ds_ref[...] = ds2
        dsb_ref[...] = ds2.astype(BF16)
        sm_ref[0:1, :] += _colsum(dy * xhat)
        sm_ref[1:2, :] += _colsum(dy)
        sm_ref[2:3, :] += _colsum(ds2)
        loss_ref[...] += jnp.full(loss_ref.shape, part, F32)

    def zero_tail(n_tail):
        def init(i, j, outs):
            @pl.when(i == 0)
            def _():
                for o in outs[-n_tail:]:
                    o[...] = jnp.zeros_like(o)
        return init

    ds2, ds2_bf, sm_ln2, loss_blk = hosted("ff2_ln2", lambda job: _mm(
        "ff2_ln2", "nn", (seq // t512, 1, ff // tkf), hdn, ((t512, tkf), lambda i, j, k: (i, k)),
        w2, ((tkf, d), lambda i, j, k: (k, 0)),
        [(xhat1, rowd(t512)), (target, rowd(t512)), (vec["ln1_g"], vecd), (vec["ln1_b"], vecd),
         (vec["b_ff2"], vecd), (vec["ln2_g"], vecd), (vec["ln2_b"], vecd)],
        [jax.ShapeDtypeStruct((seq, d), F32), jax.ShapeDtypeStruct((seq, d), BF16),
         jax.ShapeDtypeStruct((SUBLANES, d), F32), jax.ShapeDtypeStruct((SUBLANES, 128), F32)],
        [rowd(t512), rowd(t512), ((SUBLANES, d), full), ((SUBLANES, 128), full)],
        epi_ln2, acc_shape=(t512, d), epi_init=zero_tail(2), sub=sub_rows, job=job))

    tkw = min(2048, seq)

    def dw(name, wname, a, b, m_dim, n_dim, b_spec=None, row0=0):
        _ORDER.extend(comm.settled_before(name))
        tn = min(1024, n_dim)
        tm = next(t for t in (1024, 768, 512, 256, 128) if m_dim % t == 0 and row0 % t == 0)
        b_spec = b_spec or ((tkw, tn), lambda i, j, k: (k, j))
        i0 = row0 // tm
        (out,) = _mm(
            name, "tn", (m_dim // tm, n_dim // tn, seq // tkw),
            a, ((tkw, tm), lambda i, j, k: (k, i0 + i)), b, b_spec, [],
            [jax.ShapeDtypeStruct((m_dim, n_dim), F32)], [((tm, tn), lambda i, j, k: (i, j))],
            epi_store if seq == tkw else None)
        comm.grad(wname, out)
        comm.after(name, None)

    dw("dw_ff2", "w_ff2", hdn, ds2_bf, ff, d)

    def epi_dpre(acc, i, j, ex, outs):
        dpre = acc * (2.0 * ex[0][...].astype(F32))
        outs[0][...] = dpre.astype(BF16)

        @pl.when(i == 0)
        def _():
            outs[1][...] = jnp.zeros_like(outs[1])

        outs[1][0:1, :] += _colsum(dpre)

    dpre, sm_bff1 = hosted("dhdn", lambda job: _mm(
        "dhdn", "nt", (seq // t1k, ff // n512, 1), ds2_bf, rowd(t1k), w2, ((n512, d), lambda i, j, k: (j, 0)),
        [(relu_h, tile_f)],
        [jax.ShapeDtypeStruct((seq, ff), BF16), jax.ShapeDtypeStruct((SUBLANES, ff), F32)],
        [tile_f, ((SUBLANES, n512), lambda i, j, k: (0, j))], epi_dpre, order="ji", job=job))

    dw("dw_ff1", "w_ff1", x1_bf, dpre, d, ff)

    def epi_ln1b(acc, i, j, ex, outs):
        ds2_ref, xh1, rs1, g1 = ex
        ds_ref, dsb_ref, sm_ref = outs
        dy1 = acc + DN_ALPHA * ds2_ref[...]
        xhat = xh1[...]
        ds1 = _ln_bwd(dy1, xhat, rs1[...], g1[...])
        ds_ref[...] = ds1
        dsb_ref[...] = ds1.astype(BF16)
        sm_ref[0:1, :] += _colsum(dy1 * xhat)
        sm_ref[1:2, :] += _colsum(dy1)
        sm_ref[2:3, :] += _colsum(ds1)

    ds1, ds1_bf, sm_ln1 = hosted("dx1_ln1", lambda job: _mm(
        "dx1_ln1", "nt", (seq // t512, 1, ff // tkf), dpre, ((t512, tkf), lambda i, j, k: (i, k)),
        w1, ((d, tkf), lambda i, j, k: (0, k)),
        [(ds2, rowd(t512)), (xhat1, rowd(t512)), (rstd1, ((t512, 1), lambda i, j, k: (i, 0))),
         (vec["ln1_g"], vecd)],
        [jax.ShapeDtypeStruct((seq, d), F32), jax.ShapeDtypeStruct((seq, d), BF16),
         jax.ShapeDtypeStruct((SUBLANES, d), F32)],
        [rowd(t512), rowd(t512), ((SUBLANES, d), full)], epi_ln1b, acc_shape=(t512, d),
        epi_init=zero_tail(1), sub=sub_rows, job=job))

    dw("dw_out", "w_out", m_bf, ds1_bf, d, d)
    n_n = d // n512
    tile_d = ((t512, n512), lambda i, j, k: (i, j))

    def epi_dm(acc, i, j, ex, outs):
        la, lb, pa, pb = ex
        ga, gb = _sig(la[...]), _sig(lb[...])
        outs[0][...] = (acc * ga).astype(BF16)
        outs[1][...] = (acc * gb).astype(BF16)
        outs[2][0] = (acc * pa[...].astype(F32) * ga * (1.0 - ga)).astype(BF16)
        outs[2][1] = (acc * pb[...].astype(F32) * gb * (1.0 - gb)).astype(BF16)

    dp_a, dp_b, dz = _mm(
        "dm", "nt", (seq // t512, n_n, 1), ds1_bf, rowd(t512), w_out, ((n512, d), lambda i, j, k: (j, 0)),
        [(z, ((t512, n512), lambda i, j, k: (i, 3 * n_n + j))),
         (z, ((t512, n512), lambda i, j, k: (i, 4 * n_n + j))), (p_a, tile_d), (p_b, tile_d)],
        [jax.ShapeDtypeStruct((seq, d), BF16), jax.ShapeDtypeStruct((seq, d), BF16),
         jax.ShapeDtypeStruct((5, seq, d), BF16)],
        [tile_d, tile_d, ((2, t512, n512), lambda i, j, k: (0, i, j))], epi_dm)

    dw("dw_pool_up", "w_pool_up", y_pool, dp_a, d, d)
    dw("dw_lru_up", "w_lru_up", y_lru, dp_b, d, d)

    def epi_bf(acc, i, j, ex, outs):
        outs[0][...] = acc.astype(BF16)

    (dy_pool,) = hosted("dy_pool", lambda job: _mm(
        "dy_pool", "nt", (seq // t512, n_n, 1), dp_a, rowd(t512), w_pu, ((n512, d), lambda i, j, k: (j, 0)), [],
        [jax.ShapeDtypeStruct((seq, d), BF16)], [tile_d], epi_bf, job=job))

    def epi_dylru(acc, i, j, ex, outs):
        hf, hb, ug, _ = ex
        u = ug[...]
        outs[0][...] = acc * _gelu(u)
        outs[1][...] = (acc * (hf[...] + hb[...]) * _gelu_grad(u)).astype(BF16)

    dz_in = dz
    dh, dz = hosted("dy_lru", lambda job: _mm(
        "dy_lru", "nt", (seq // t512, n_n, 1), dp_b, rowd(t512), w_lu, ((n512, d), lambda i, j, k: (j, 0)),
        [(h_f, tile_d), (h_b, tile_d), (z, ((t512, n512), lambda i, j, k: (i, 2 * n_n + j))),
         (dz_in, (None, None))],
        [jax.ShapeDtypeStruct((seq, d), F32), jax.ShapeDtypeStruct(dz_in.shape, BF16)],
        [tile_d, ((None, t512, n512), lambda i, j, k: (4, i, j))], epi_dylru, aliases={5: 1}, job=job))

    dz, g_pw, sm_pool = _pool_bwd(z, dy_pool, pool_w, vec["pool_scale"], dz, seq, d, tc)
    comm.grad("pool_w", g_pw)
    dxc0, g_wa0, g_wx0, sm_l0 = hosted("lru_bwd_0", lambda job: _lru_bwd(
        0, xc, dh, h_f, sv, wa, wx, None, seq, d, tc, job=job))
    dxc, g_wa1, g_wx1, sm_l1 = hosted("lru_bwd_1", lambda job: _lru_bwd(
        1, xc, dh, h_b, sv, wa, wx, dxc0, seq, d, tc, job=job))
    comm.grad("lru_wa", jnp.concatenate([g_wa0, g_wa1], axis=0))
    comm.grad("lru_wx", jnp.concatenate([g_wx0, g_wx1], axis=0))
    dz, sm_conv = _conv_bwd(z, dxc, sv, dz, seq, d, tc, blk)

    tnw = min(1024, d)
    per_seg = d // tnw
    seg_spec = ((None, tkw, tnw), lambda i, j, k: ((j // per_seg + 2) % 5, k, j % per_seg))
    lo_rows = 3 * d // 4
    dw("dw_in_lo", "w_in_lo", x_bf, dz, lo_rows, n_in, b_spec=seg_spec)
    dw("dw_in_hi", "w_in_hi", x_bf, dz, d - lo_rows, n_in, b_spec=seg_spec, row0=lo_rows)

    nk = d // tkd

    def epi_dx(acc, i, j, ex, outs):
        outs[0][...] = acc + DN_ALPHA * ex[0][...]

    (grad_x,) = hosted("dx", lambda job: _mm(
        "dx", "nt", (seq // t512, 1, n_in // tkd), dz,
        ((None, t512, tkd), lambda i, j, k: ((k // nk + 2) % 5, i, k % nk)),
        wts["w_in"], ((d, tkd), lambda i, j, k: (0, k)), [(ds1, rowd(t512))],
        [jax.ShapeDtypeStruct((seq, d), F32)], [rowd(t512)], epi_dx, acc_shape=(t512, d), job=job))

    small = {"ln2": sm_ln2, "b_ff1": sm_bff1, "ln1": sm_ln1, "pool": sm_pool, "lru0": sm_l0, "lru1": sm_l1,
             "conv": sm_conv}
    return loss_blk[0, 0], grad_x, small


REP = ("pool_scale", "conv_b", "b_out", "ln1_g", "ln1_b", "b_ff2", "ln2_g", "ln2_b")
SHARDED_SMALL = (("conv_w", 4), ("lru_ba", 2), ("lru_bx", 2), ("lru_lambda", 2))
WEIGHT_ORDER = ("w_in", "pool_w", "pool_scale", "conv_w", "conv_b", "lru_wa", "lru_ba", "lru_wx", "lru_bx",
                "lru_lambda", "w_pool_up", "w_lru_up", "w_out", "b_out", "ln1_g", "ln1_b", "w_ff1", "b_ff1",
                "w_ff2", "b_ff2", "ln2_g", "ln2_b")


def _pad_rows(a, rows):
    return jnp.concatenate([a, jnp.zeros((rows - a.shape[0], a.shape[1]), a.dtype)], axis=0)


def kernel(x, w_in, pool_w, pool_scale, conv_w, conv_b, lru_wa, lru_ba, lru_wx, lru_bx, lru_lambda, w_pool_up, w_lru_up, w_out, b_out, ln1_g, ln1_b, w_ff1, b_ff1, w_ff2, b_ff2, ln2_g, ln2_b, loss_target, m_w_in, m_pool_w, m_pool_scale, m_conv_w, m_conv_b, m_lru_wa, m_lru_ba, m_lru_wx, m_lru_bx, m_lru_lambda, m_w_pool_up, m_w_lru_up, m_w_out, m_b_out, m_ln1_g, m_ln1_b, m_w_ff1, m_b_ff1, m_w_ff2, m_b_ff2, m_ln2_g, m_ln2_b, v_w_in, v_pool_w, v_pool_scale, v_conv_w, v_conv_b, v_lru_wa, v_lru_ba, v_lru_wx, v_lru_bx, v_lru_lambda, v_w_pool_up, v_w_lru_up, v_w_out, v_b_out, v_ln1_g, v_ln1_b, v_w_ff1, v_b_ff1, v_w_ff2, v_b_ff2, v_ln2_g, v_ln2_b):
    args = dict(locals())
    w = {n: args[n] for n in WEIGHT_ORDER}
    mom = {n: args["m_" + n] for n in WEIGHT_ORDER}
    var = {n: args["v_" + n] for n in WEIGHT_ORDER}
    seq, d = x.shape[1], x.shape[2]
    n_heads, blk = lru_wa.shape[2], lru_wa.shape[4]
    n_groups = pool_w.shape[1]
    ff = b_ff1.shape[1]
    cx, cy, cc = _coords()
    me = 4 * cx + 2 * cy + cc
    cidx = jnp.reshape(cc, (1,)).astype(jnp.int32)
    qidx = jnp.reshape(2 * cx + cy, (1,)).astype(jnp.int32)

    fam_of = dict(BIG)
    sviews = {n: _shard_view(w[n], fam) for n, fam in BIG}
    size_of = {n: sviews[n].shape[1] for n, _ in BIG}
    for half in ("w_in_lo", "w_in_hi"):
        fam_of[half], size_of[half] = fam_of["w_in"], size_of["w_in"]
    wts = {}

    def take_gathered(names, arrays):
        for n, g in zip(names, arrays):
            if n == "pool_w":
                g = g.reshape(n_groups, d // n_groups, d // n_groups)
            elif n in ("lru_wa", "lru_wx"):
                g = g.reshape(2, n_heads, blk, blk)
            elif fam_of[n] == "row":
                g = g.reshape(g.shape[1:])
            wts[n] = g

    shard_bf = {n: sviews[n].astype(BF16) for n, _ in BIG}
    piece = sviews["w_in"].shape[1] // W_IN_PASSES
    for k in range(W_IN_PASSES):
        name = "w_in_piece_%d" % k
        shard_bf[name] = shard_bf["w_in"][:, k * piece:(k + 1) * piece]
        fam_of[name] = "col"
    partial = {}

    def gather_job(items, extra=()):
        return _ag_job([shard_bf[n] for n, _ in items] + [e for e, _ in extra],
                       [fam_of[n] for n, _ in items] + [f for _, f in extra],
                       parts=[p for _, p in items] + [None] * len(extra),
                       into=[partial.get(n) if p else None for n, p in items] + [None] * len(extra))

    def take_pieces(items, arrays):
        for (n, p), g in zip(items, arrays):
            if p is None or p[0] + p[2] == p[1]:
                partial.pop(n, None)
                take_gathered([n], [g])
            else:
                partial[n] = g

    launched = [0]

    def on_sequencer(kind, job):
        launched[0] += 1
        return _sequencer_job("sq_%s_%d" % (kind, launched[0]), job, launched[0] % 2)

    class Plan:
        sibling = {"dw_ff2": ("w_ff2",), "dw_ff1": ("w_ff1",), "dw_lru_up": ("w_out", "w_pool_up", "w_lru_up"),
                   "dw_in_lo": ("w_in_lo",), "dw_in_hi": ("w_in_hi", "pool_w", "lru_wa", "lru_wx")}
        chips = {"dw_ff1": ("w_ff2",), "dw_out": ("w_ff1",), "dy_lru": ("w_out", "w_pool_up", "w_lru_up"),
                 "dw_in_hi": ("w_in_lo",), "dx": ("w_in_hi", "pool_w", "lru_wa", "lru_wx")}

        def __init__(self):
            self.grads, self.recv_a, self.parts, self.recv_b = {}, {}, {}, {}

        def grad(self, name, g):
            self.grads[name] = g if fam_of[name] == "col" else g.reshape((-1,) + g.shape[-2:])

        settle = {"dw_ff1": ("w_in",), "dw_out": ("w_ff2",), "dy_pool": ("w_ff1",), "lru_bwd_1": ("w_out",)}
        settle_add = {"dx": ("w_in_lo",)}

        def job(self, host):
            return None

        def settled_before(self, host):
            return [wts[n] if n == "w_in" else self.recv_b[n] for n in self.settle.get(host, ())]

        def after(self, host, job):
            if host in self.chips:
                names = self.chips[host]
                _ORDER.extend(self.recv_b[n] for n in self.settle_add.get(host, ()))
                for n in names:
                    self.parts[n] = _rs_add("rs_add_" + n, self.grads[n], self.recv_a[n], fam_of[n], size_of[n], cidx)
                res = on_sequencer("chips", _rs_chips_job([self.parts[n] for n in names]))
                self.recv_b.update(zip(names, res))
            if host in self.sibling:
                names = self.sibling[host]
                res = on_sequencer("sibling", _rs_sibling_job(
                    [self.grads[n] for n in names], [fam_of[n] for n in names], [size_of[n] for n in names]))
                self.recv_a.update(zip(names, res))

    first = (("w_in_piece_0", None),)
    sv_shard = _pad_rows(jnp.concatenate([w[n].reshape(r, -1) for n, r in SHARDED_SMALL], axis=0), 16)
    gathered = on_sequencer("gather", gather_job(first, [(sv_shard, "col")]))
    take_pieces(first, gathered[:-1])
    vec = {n: w[n] for n in REP}
    vec["b_ff1"] = b_ff1
    vec["sv"] = gathered[-1]
    queue = [(("w_in_piece_%d" % k, None),) for k in range(1, W_IN_PASSES)]
    queue += [(("pool_w", None), ("lru_wa", None), ("lru_wx", None)), (("w_pool_up", None), ("w_lru_up", None)),
              (("w_out", None),), (("w_ff1", None),), (("w_ff2", None),), (("w_in", None),)]
    for items in queue:
        take_pieces(items, on_sequencer("gather", gather_job(items)))

    plan = Plan()
    loss_part, grad_x, small = _local_step(x.reshape(seq, d), loss_target.reshape(seq, d), wts, vec, plan)
    loss = lax.psum(loss_part, AXES)

    out_g, out_d, out_m, out_v = {}, {}, {}, {}
    for n, fam in sorted(BIG, key=lambda nf: nf[0] in ("w_in", "pool_w", "lru_wa", "lru_wx")):
        halves = [n + "_lo", n + "_hi"] if n == "w_in" else [n]
        res = _rs_final_adam("adam_" + n, [plan.parts[h] for h in halves], [plan.recv_b[h] for h in halves],
                             sviews[n], _shard_view(mom[n], fam), _shard_view(var[n], fam), fam, qidx)
        out_g[n], out_d[n], out_m[n], out_v[n] = [r.reshape(w[n].shape) for r in res]

    rows = [small["pool"][0:1], small["conv"][4:5], small["ln1"][2:3], small["ln1"][0:1], small["ln1"][1:2],
            small["ln2"][2:3], small["ln2"][0:1], small["ln2"][1:2], small["b_ff1"][0:1].reshape(ff // d, d),
            small["conv"][0:4], small["lru0"][0:1], small["lru1"][0:1], small["lru0"][1:2], small["lru1"][1:2],
            small["lru0"][2:3], small["lru1"][2:3]]
    n_rep = len(REP) + ff // d
    n_rows = n_rep + sum(r for _, r in SHARDED_SMALL)
    pad_rows = -(-n_rows // SUBLANES) * SUBLANES
    packed = _pad_rows(jnp.concatenate(rows, axis=0), pad_rows)
    (all_small,) = _run_job("ag_small", _ag_job([packed], ["lead"]))
    g_small = _sum8("sum_small", all_small)

    def pack_rep(t):
        return jnp.concatenate([t[n] for n in REP] + [t["b_ff1"].reshape(ff // d, d)], axis=0)

    def pack_sh(t):
        return jnp.concatenate([t[n].reshape(r, -1) for n, r in SHARDED_SMALL], axis=0)

    g_rep = g_small[:n_rep]
    cs = d // N_DEV
    g_sh = lax.dynamic_slice_in_dim(g_small[n_rep:n_rows], me * cs, cs, axis=1)
    d_rep, m_rep, v_rep = _adam_small("adam_rep", pack_rep(w), g_rep, pack_rep(mom), pack_rep(var))
    d_sh, m_sh, v_sh = _adam_small("adam_sharded", pack_sh(w), g_sh, pack_sh(mom), pack_sh(var))

    def unpack(rep_t, sh_t, dst):
        for i, n in enumerate(REP):
            dst[n] = rep_t[i:i + 1].reshape(w[n].shape)
        dst["b_ff1"] = rep_t[len(REP):n_rep].reshape(w["b_ff1"].shape)
        r0 = 0
        for n, r in SHARDED_SMALL:
            dst[n] = sh_t[r0:r0 + r].reshape(w[n].shape)
            r0 += r

    unpack(g_rep, g_sh, out_g)
    unpack(d_rep, d_sh, out_d)
    unpack(m_rep, m_sh, out_m)
    unpack(v_rep, v_sh, out_v)

    _ORDER[:] = []
    outs = [loss, grad_x.reshape(x.shape)]
    for t in (out_g, out_d, out_m, out_v):
        outs += [t[n] for n in WEIGHT_ORDER]
    return tuple(outs)
```

```python
import jax
import jax.numpy as jnp
from jax import lax
from jax.experimental import pallas as pl
from jax.experimental.pallas import tpu as pltpu
from jax.experimental.pallas import tpu_sc as plsc

F32 = jnp.float32
BF16 = jnp.bfloat16
MESH = pl.DeviceIdType.MESH
AXES = ("x", "y", "c")
N_DEV = 8

DN_ALPHA = 2.0 ** 0.25
LN_EPS = 1e-5
LRU_C = 8.0
ADAM_LR = 0.001
ADAM_B1 = 0.9
ADAM_B2 = 0.999
ADAM_EPS = 1e-08
ADAM_WD = 0.01
ADAM_STEP = 10
GELU_C = 0.7978845608028654
GELU_K = 0.044715

W_IN_PASSES = 5
HALO = 16
SUBLANES = 8
VMEM_MB = 56


def _cparams(sem, vmem_mb=VMEM_MB):
    return pltpu.CompilerParams(dimension_semantics=sem, vmem_limit_bytes=vmem_mb << 20)


HBM_SPEC = pl.BlockSpec(memory_space=pl.ANY)


class _Job:
    def __init__(self, ins, outs, sems, start, finish, mid=None):
        self.ins, self.outs, self.sems = list(ins), list(outs), list(sems)
        self.start, self.finish, self.mid = start, finish, mid
        self.alias = {}
        self.results = None


_ORDER = []


def _pcall(name, body, grid, in_specs, out_specs, out_shape, inputs, scratch=(), aliases=None,
           vmem_mb=VMEM_MB):
    token = [t for t in _ORDER if not any(t is a for a in inputs)]
    n_in = len(inputs)

    def ordered(*refs):
        return body(*refs[:n_in], *refs[n_in + len(token):])

    res = pl.pallas_call(ordered, name=name, grid=grid, in_specs=list(in_specs) + [HBM_SPEC] * len(token),
                         out_specs=list(out_specs), out_shape=list(out_shape), scratch_shapes=list(scratch),
                         input_output_aliases=aliases or {},
                         compiler_params=_cparams(("arbitrary",) * len(grid), vmem_mb))(*inputs, *token)
    _ORDER[:] = [res[0]]
    return res


def _run_job(name, job):
    ji, jo = len(job.ins), len(job.outs)

    def body(*refs):
        jins, jouts, sems = refs[:ji], refs[ji:ji + jo], refs[ji + jo:]
        job.start(jins, jouts, sems)
        if job.mid is not None:
            job.mid(jins, jouts, sems)
        job.finish(jins, jouts, sems)

    res = pl.pallas_call(body, name=name, in_specs=[HBM_SPEC] * ji, out_specs=[HBM_SPEC] * jo,
                         out_shape=job.outs, scratch_shapes=job.sems, input_output_aliases=job.alias)(*job.ins)
    job.results = list(res)
    return job.results


def _sequencer_job(name, job, collective_id):
    ji, jo = len(job.ins), len(job.outs)

    def body(*refs):
        jins, jouts, sems = refs[:ji], refs[ji:ji + jo], refs[ji + jo:]
        barrier = pltpu.get_barrier_semaphore()
        x, y, c = lax.axis_index("x"), lax.axis_index("y"), lax.axis_index("c")
        for r in range(1, N_DEV):
            peer = ((1 - x) if r & 4 else x, (1 - y) if r & 2 else y, (1 - c) if r & 1 else c)
            pl.semaphore_signal(barrier, inc=1, device_id=peer, device_id_type=MESH)
        pl.semaphore_wait(barrier, N_DEV - 1)
        job.start(jins, jouts, sems)
        if job.mid is not None:
            job.mid(jins, jouts, sems)
        job.finish(jins, jouts, sems)

    res = pl.kernel(
        body, name=name, out_type=job.outs, mesh=plsc.ScalarSubcoreMesh(axis_name="sequencer", num_cores=1),
        scratch_types=job.sems, compiler_params=pltpu.CompilerParams(collective_id=collective_id),
    )(*job.ins)
    job.results = list(res)
    return job.results


def _dot(mode, a, b):
    if mode == "nn":
        dims = (((1,), (0,)), ((), ()))
    elif mode == "nt":
        dims = (((1,), (1,)), ((), ()))
    else:
        dims = (((0,), (0,)), ((), ()))
    return lax.dot_general(a, b, dims, preferred_element_type=F32)


def _sig(x):
    return 0.5 * jnp.tanh(0.5 * x) + 0.5


def _gelu(x):
    t = jnp.tanh(GELU_C * (x + GELU_K * x * x * x))
    return 0.5 * x * (1.0 + t)


def _gelu_grad(x):
    x2 = x * x
    t = jnp.tanh(GELU_C * (x + GELU_K * x * x2))
    return 0.5 * (1.0 + t) + 0.5 * x * (1.0 - t * t) * GELU_C * (1.0 + 3.0 * GELU_K * x2)


def _colsum(v):
    return jnp.sum(v, axis=0, keepdims=True)


def _mm(name, mode, grid, a, a_spec, b, b_spec, extras, out_shapes, out_specs, epi, *,
        order="ij", acc_shape=None, aliases=None, vmem_mb=VMEM_MB):
    gm, gn, gk = grid

    def spec(s):
        bs, f = s
        if f is None:
            return pl.BlockSpec(memory_space=pl.ANY)
        if order == "ij":
            return pl.BlockSpec(bs, lambda i, j, k, f=f: f(i, j, k))
        return pl.BlockSpec(bs, lambda j, i, k, f=f: f(i, j, k))

    ne, no = len(extras), len(out_shapes)

    def kern(*refs):
        a_ref, b_ref = refs[0], refs[1]
        ex = refs[2:2 + ne]
        outs = refs[2 + ne:2 + ne + no]
        if order == "ij":
            i, j = pl.program_id(0), pl.program_id(1)
        else:
            j, i = pl.program_id(0), pl.program_id(1)
        k = pl.program_id(2)
        prod = _dot(mode, a_ref[...], b_ref[...])
        if gk == 1:
            epi(prod, i, j, ex, outs)
        elif epi is None:
            @pl.when(k == 0)
            def _():
                outs[0][...] = prod

            @pl.when(k > 0)
            def _():
                outs[0][...] += prod
        else:
            acc = refs[-1]

            @pl.when(k == 0)
            def _():
                acc[...] = prod

            @pl.when(k > 0)
            def _():
                acc[...] += prod

            @pl.when(k == gk - 1)
            def _():
                epi(acc[...], i, j, ex, outs)

    g = (gm, gn, gk) if order == "ij" else (gn, gm, gk)
    scratch = [pltpu.VMEM(acc_shape, F32)] if gk > 1 and epi is not None else []
    return _pcall(name, kern, g, [spec(a_spec), spec(b_spec)] + [spec(s) for _, s in extras],
                  [spec(s) for s in out_specs], out_shapes, [a, b] + [e for e, _ in extras],
                  scratch=scratch, aliases=aliases, vmem_mb=vmem_mb)


def _ext(ref, c, n_chunks, tc, seq):
    c0 = pl.multiple_of(c * tc, tc)
    body = ref[pl.ds(c0, tc), :].astype(F32)
    t0 = pl.multiple_of(jnp.maximum(c0 - HALO, 0), HALO)
    b0 = pl.multiple_of(jnp.minimum(c0 + tc, seq - HALO), HALO)
    top = ref[pl.ds(t0, HALO), :].astype(F32)
    bot = ref[pl.ds(b0, HALO), :].astype(F32)
    top = jnp.where(c > 0, top, 0.0)
    bot = jnp.where(c < n_chunks - 1, bot, 0.0)
    return jnp.concatenate([top, body, bot], axis=0)


def _shifted(vext, off, tc):
    n = vext.shape[0]
    r = vext if off == 0 else pltpu.roll(vext, (n - off) % n, 0)
    return r[HALO:HALO + tc]


def _win_sum(vext, g, extra, tc):
    s2 = vext + pltpu.roll(vext, 1, 0)
    s4 = s2 + pltpu.roll(s2, 2, 0)
    s8 = s4 + pltpu.roll(s4, 4, 0)
    s16 = s8 + pltpu.roll(s8, 8, 0)
    outs = [_shifted(s, extra + hw - 1, tc) for s, hw in ((s2, 1), (s4, 2), (s8, 4), (s16, 8))]
    return jnp.where(g == 0, outs[0], jnp.where(g == 1, outs[1], jnp.where(g == 2, outs[2], outs[3])))


def _win_cnt(t, hw, seq):
    return (jnp.minimum(t + hw, seq) - jnp.maximum(t - hw, 0)).astype(F32)


def _pool_d(uext, g, c, tc, seq):
    hw = jnp.left_shift(1, g)
    t = c * tc + lax.broadcasted_iota(jnp.int32, (tc, 1), 0)
    ws = _win_sum(uext, g, 0, tc)
    return ws / _win_cnt(t, hw, seq) - uext[HALO:HALO + tc]


def _scan_tiles(a_ref, b_ref, h_ref, carry_ref, n_tiles, reverse):
    blk = a_ref.shape[1]
    row = lax.broadcasted_iota(jnp.int32, (SUBLANES, blk), 0)

    def tile(j, hc):
        jj = (n_tiles - 1 - j) if reverse else j
        off = pl.multiple_of(jj * SUBLANES, SUBLANES)
        a = a_ref[pl.ds(off, SUBLANES), :]
        b = b_ref[pl.ds(off, SUBLANES), :]
        for kk in (1, 2, 4):
            sh = (SUBLANES - kk) if reverse else kk
            a_s = pltpu.roll(a, sh, 0)
            b_s = pltpu.roll(b, sh, 0)
            m = (row < SUBLANES - kk) if reverse else (row >= kk)
            a_s = jnp.where(m, a_s, 1.0)
            b_s = jnp.where(m, b_s, 0.0)
            b = a * b_s + b
            a = a * a_s
        h = a * hc + b
        h_ref[pl.ds(off, SUBLANES), :] = h
        return h[0:1, :] if reverse else h[SUBLANES - 1:SUBLANES, :]

    group = SUBLANES if n_tiles % SUBLANES == 0 else 1

    def tiles(jg, hc):
        for u in range(group):
            hc = tile(jg * group + u, hc)
        return hc

    hc = lax.fori_loop(0, n_tiles // group, tiles, carry_ref[0:1, :])
    carry_ref[0:1, :] = hc


def _lru_k(lam):
    y = -lam
    e = jnp.exp(-jnp.abs(y))
    u = 1.0 + e
    l1p = jnp.where(u == 1.0, e, jnp.log(u) * (e / (u - 1.0)))
    return -LRU_C * (jnp.maximum(y, 0.0) + l1p)


def _lru_gates(xc, wa, wx, ba, bx, lam):
    xb = xc.astype(BF16)
    r = _sig(jnp.dot(xb, wa, preferred_element_type=F32) + ba)
    i = _sig(jnp.dot(xb, wx, preferred_element_type=F32) + bx)
    k = _lru_k(lam)
    la = k * r
    a = jnp.exp(la)
    s = jnp.sqrt(-jnp.tanh(la) * (a * a + 1.0))
    return r, i, k, a, s


SV_CONV, SV_BA, SV_BX, SV_LAM = 0, 4, 6, 8


def _pool_fwd(z, pw, scale, seq, d, tc):
    n_g = pw.shape[0]
    pg = d // n_g
    n_chunks = seq // tc

    def kern(z_ref, pw_ref, sc_ref, y_ref):
        g, c = pl.program_id(0), pl.program_id(1)
        uext = _ext(z_ref, c, n_chunks, tc, seq)
        dd = _pool_d(uext, g, c, tc, seq)
        q = jnp.dot(dd.astype(BF16), pw_ref[...], preferred_element_type=F32)
        y_ref[...] = (q * sc_ref[...]).astype(BF16)

    (y,) = _pcall(
        "pool_fwd", kern, (n_g, n_chunks),
        [pl.BlockSpec((seq, pg), lambda g, c: (0, g)),
         pl.BlockSpec((None, pg, pg), lambda g, c: (g, 0, 0)),
         pl.BlockSpec((1, pg), lambda g, c: (0, g))],
        [pl.BlockSpec((tc, pg), lambda g, c: (c, g))],
        [jax.ShapeDtypeStruct((seq, d), BF16)], [z, pw, scale])
    return y


def _lru_fwd(z, sv, conv_b, wa, wx, seq, d, tc):
    n_h, blk = wa.shape[1], wa.shape[2]
    n_chunks = seq // tc
    lru_off = d // blk

    def kern(z_ref, sv_ref, cb_ref, wa_ref, wx_ref, xc_ref, h_ref, a_s, b_s, carry):
        c = pl.program_id(1)
        uext = _ext(z_ref, c, n_chunks, tc, seq)
        xc = cb_ref[...]
        for k in range(4):
            xc = xc + _shifted(uext, k - 2, tc) * sv_ref[SV_CONV + k:SV_CONV + k + 1, :]
        xc_ref[...] = xc
        _, i, _, a, s = _lru_gates(xc, wa_ref[...], wx_ref[...], sv_ref[SV_BA:SV_BA + 1, :],
                                   sv_ref[SV_BX:SV_BX + 1, :], sv_ref[SV_LAM:SV_LAM + 1, :])
        a_s[...] = a
        b_s[...] = s * (i * xc)

        @pl.when(c == 0)
        def _():
            carry[...] = jnp.zeros_like(carry)

        _scan_tiles(a_s, b_s, h_ref, carry, tc // SUBLANES, False)

    col = lambda h, c: (c, h)
    return _pcall(
        "lru_fwd", kern, (n_h, n_chunks),
        [pl.BlockSpec((seq, blk), lambda h, c: (0, lru_off + h)),
         pl.BlockSpec((16, blk), lambda h, c: (0, h)),
         pl.BlockSpec((1, blk), lambda h, c: (0, h)),
         pl.BlockSpec((None, None, blk, blk), lambda h, c: (0, h, 0, 0)),
         pl.BlockSpec((None, None, blk, blk), lambda h, c: (0, h, 0, 0))],
        [pl.BlockSpec((tc, blk), col), pl.BlockSpec((tc, blk), col)],
        [jax.ShapeDtypeStruct((seq, d), F32), jax.ShapeDtypeStruct((seq, d), F32)],
        [z, sv, conv_b, wa, wx],
        scratch=[pltpu.VMEM((tc, blk), F32), pltpu.VMEM((tc, blk), F32), pltpu.VMEM((SUBLANES, blk), F32)])


def _lru_rev(z, sv, wa, wx, xc, h_f, seq, d, tc):
    n_h, blk = wa.shape[1], wa.shape[2]
    n_chunks = seq // tc
    gate_off = 2 * d // blk

    def kern(z_ref, sv_ref, wa_ref, wx_ref, xc_ref, hf_ref, hb_ref, y_ref, a_s, b_s, carry):
        c = pl.program_id(1)
        xc = xc_ref[...]
        _, i, _, a, s = _lru_gates(xc, wa_ref[...], wx_ref[...], sv_ref[SV_BA + 1:SV_BA + 2, :],
                                   sv_ref[SV_BX + 1:SV_BX + 2, :], sv_ref[SV_LAM + 1:SV_LAM + 2, :])
        a_s[...] = a
        b_s[...] = s * (i * xc)

        @pl.when(c == 0)
        def _():
            carry[...] = jnp.zeros_like(carry)

        _scan_tiles(a_s, b_s, hb_ref, carry, tc // SUBLANES, True)
        y_ref[...] = ((hf_ref[...] + hb_ref[...]) * _gelu(z_ref[...])).astype(BF16)

    rev = lambda h, c: (n_chunks - 1 - c, h)
    return _pcall(
        "lru_rev", kern, (n_h, n_chunks),
        [pl.BlockSpec((tc, blk), lambda h, c: (n_chunks - 1 - c, gate_off + h)),
         pl.BlockSpec((16, blk), lambda h, c: (0, h)),
         pl.BlockSpec((None, None, blk, blk), lambda h, c: (1, h, 0, 0)),
         pl.BlockSpec((None, None, blk, blk), lambda h, c: (1, h, 0, 0)),
         pl.BlockSpec((tc, blk), rev), pl.BlockSpec((tc, blk), rev)],
        [pl.BlockSpec((tc, blk), rev), pl.BlockSpec((tc, blk), rev)],
        [jax.ShapeDtypeStruct((seq, d), F32), jax.ShapeDtypeStruct((seq, d), BF16)],
        [z, sv, wa, wx, xc, h_f],
        scratch=[pltpu.VMEM((tc, blk), F32), pltpu.VMEM((tc, blk), F32), pltpu.VMEM((SUBLANES, blk), F32)])


def _merge(y_pool, w_pu, y_lru, w_lu, z, seq, d, tm, tn):
    n_n = d // tn

    def kern(yp_ref, wp_ref, yl_ref, wl_ref, la_ref, lb_ref, m_ref, pa_ref, pb_ref):
        pa = jnp.dot(yp_ref[...], wp_ref[...], preferred_element_type=F32)
        pb = jnp.dot(yl_ref[...], wl_ref[...], preferred_element_type=F32)
        m_ref[...] = (_sig(la_ref[...]) * pa + _sig(lb_ref[...]) * pb).astype(BF16)
        pa_ref[...] = pa.astype(BF16)
        pb_ref[...] = pb.astype(BF16)

    row = pl.BlockSpec((tm, d), lambda i, j: (i, 0))
    wcol = pl.BlockSpec((d, tn), lambda i, j: (0, j))
    out = pl.BlockSpec((tm, tn), lambda i, j: (i, j))
    sh = jax.ShapeDtypeStruct((seq, d), BF16)
    return _pcall(
        "merge", kern, (seq // tm, n_n),
        [row, wcol, row, wcol,
         pl.BlockSpec((tm, tn), lambda i, j: (i, 3 * n_n + j)),
         pl.BlockSpec((tm, tn), lambda i, j: (i, 4 * n_n + j))],
        [out, out, out], [sh, sh, sh], [y_pool, w_pu, y_lru, w_lu, z, z])


def _ln_fwd(s, g, b):
    mu = jnp.mean(s, axis=-1, keepdims=True)
    xc = s - mu
    var = jnp.mean(xc * xc, axis=-1, keepdims=True)
    rstd = lax.rsqrt(var + LN_EPS)
    xhat = xc * rstd
    return xhat, rstd, xhat * g + b


def _ln_bwd(dy, xhat, rstd, g):
    dyg = dy * g
    m1 = jnp.mean(dyg, axis=-1, keepdims=True)
    m2 = jnp.mean(dyg * xhat, axis=-1, keepdims=True)
    return rstd * (dyg - m1 - xhat * m2)


def _pool_bwd(z, dy_pool, pw, scale, dz, seq, d, tc):
    n_g = pw.shape[0]
    pg = d // n_g
    n_chunks = seq // tc

    def kern(z_ref, dy_ref, pw_ref, sc_ref, dz_in, dz_ref, dpw_ref, dsc_ref):
        del dz_in
        g, c = pl.program_id(0), pl.program_id(1)
        hw = jnp.left_shift(1, g)
        uext = _ext(z_ref, c, n_chunks, tc, seq)
        dd = _pool_d(uext, g, c, tc, seq).astype(BF16)
        pwv = pw_ref[...]
        q = jnp.dot(dd, pwv, preferred_element_type=F32)
        dyext = _ext(dy_ref, c, n_chunks, tc, seq)

        @pl.when(c == 0)
        def _():
            dsc_ref[...] = jnp.zeros_like(dsc_ref)
            dpw_ref[...] = jnp.zeros_like(dpw_ref)

        dsc_ref[0:1, :] += _colsum(dyext[HALO:HALO + tc] * q)
        dqext = (dyext * sc_ref[...]).astype(BF16)
        dpw_ref[...] += _dot("tn", dd, dqext[HALO:HALO + tc])
        ddext = _dot("nt", dqext, pwv)
        text = c * tc - HALO + lax.broadcasted_iota(jnp.int32, (tc + 2 * HALO, 1), 0)
        v = ddext / jnp.maximum(_win_cnt(text, hw, seq), 1.0)
        dz_ref[...] = (_win_sum(v, g, 1, tc) - ddext[HALO:HALO + tc]).astype(BF16)

    return _pcall(
        "pool_bwd", kern, (n_g, n_chunks),
        [pl.BlockSpec((seq, pg), lambda g, c: (0, g)),
         pl.BlockSpec((seq, pg), lambda g, c: (0, g)),
         pl.BlockSpec((None, pg, pg), lambda g, c: (g, 0, 0)),
         pl.BlockSpec((1, pg), lambda g, c: (0, g)),
         pl.BlockSpec(memory_space=pl.ANY)],
        [pl.BlockSpec((None, tc, pg), lambda g, c: (2, c, g)),
         pl.BlockSpec((None, pg, pg), lambda g, c: (g, 0, 0)),
         pl.BlockSpec((SUBLANES, pg), lambda g, c: (0, g))],
        [jax.ShapeDtypeStruct(dz.shape, dz.dtype),
         jax.ShapeDtypeStruct((n_g, pg, pg), F32),
         jax.ShapeDtypeStruct((SUBLANES, d), F32)],
        [z, dy_pool, pw, scale, dz], aliases={4: 0})


def _lru_bwd(direction, xc, dh, h_dir, sv, wa, wx, dxc_prev, seq, d, tc):
    reverse = direction == 1
    n_h, blk = wa.shape[1], wa.shape[2]
    n_chunks = seq // tc
    has_prev = dxc_prev is not None

    def kern(*refs):
        xc_ref, dh_ref, h_ref, sv_ref, wa_ref, wx_ref = refs[:6]
        p = 6
        prev_ref = None
        if has_prev:
            prev_ref = refs[p]
            p += 1
        dxc_ref, dwa_ref, dwx_ref, sm_ref, at_s, g_s, carry, acarry = refs[p:p + 8]
        c = pl.program_id(1)
        cr = c if reverse else n_chunks - 1 - c
        c0 = pl.multiple_of(cr * tc, tc)

        @pl.when(c == 0)
        def _():
            carry[...] = jnp.zeros_like(carry)
            acarry[...] = jnp.zeros_like(acarry)
            dwa_ref[...] = jnp.zeros_like(dwa_ref)
            dwx_ref[...] = jnp.zeros_like(dwx_ref)
            sm_ref[...] = jnp.zeros_like(sm_ref)

        xc = xc_ref[...]
        wav, wxv = wa_ref[...], wx_ref[...]
        lam = sv_ref[SV_LAM + direction:SV_LAM + direction + 1, :]
        r, i, k, a, s = _lru_gates(xc, wav, wxv, sv_ref[SV_BA + direction:SV_BA + direction + 1, :],
                                   sv_ref[SV_BX + direction:SV_BX + direction + 1, :], lam)
        rowi = lax.broadcasted_iota(jnp.int32, (tc, blk), 0)
        hbody = h_ref[pl.ds(c0, tc), :]
        if not reverse:
            p0 = pl.multiple_of(jnp.maximum(c0 - SUBLANES, 0), SUBLANES)
            edge = jnp.where(cr > 0, h_ref[pl.ds(p0, SUBLANES), :][SUBLANES - 1:SUBLANES, :], 0.0)
            hprev = jnp.where(rowi == 0, edge, pltpu.roll(hbody, 1, 0))
            at = jnp.where(rowi == tc - 1, acarry[0:1, :], pltpu.roll(a, tc - 1, 0))
        else:
            n0 = pl.multiple_of(jnp.minimum(c0 + tc, seq - SUBLANES), SUBLANES)
            edge = jnp.where(cr < n_chunks - 1, h_ref[pl.ds(n0, SUBLANES), :][0:1, :], 0.0)
            hprev = jnp.where(rowi == tc - 1, edge, pltpu.roll(hbody, tc - 1, 0))
            at = jnp.where(rowi == 0, acarry[0:1, :], pltpu.roll(a, 1, 0))
        at_s[...] = at
        _scan_tiles(at_s, dh_ref, g_s, carry, tc // SUBLANES, not reverse)
        acarry[0:1, :] = a[tc - 1:tc, :] if reverse else a[0:1, :]

        gt = g_s[...]
        da = gt * hprev
        di = gt * s * xc
        dxc = gt * s * i
        ds = gt * (i * xc)
        dl = da * a - ds * (a * a) / s
        dpr = (dl * k) * r * (1.0 - r)
        dpi = di * i * (1.0 - i)
        sm_ref[0:1, :] += _colsum(dpr)
        sm_ref[1:2, :] += _colsum(dpi)
        sm_ref[2:3, :] += _colsum(dl * r) * (LRU_C * _sig(-lam))
        xb, dprb, dpib = xc.astype(BF16), dpr.astype(BF16), dpi.astype(BF16)
        dwa_ref[...] += _dot("tn", xb, dprb)
        dwx_ref[...] += _dot("tn", xb, dpib)
        dxc = dxc + _dot("nt", dprb, wav) + _dot("nt", dpib, wxv)
        if has_prev:
            dxc = dxc + prev_ref[...]
        dxc_ref[...] = dxc

    if reverse:
        chunk = lambda h, c: (c, h)
    else:
        chunk = lambda h, c: (n_chunks - 1 - c, h)
    wspec = pl.BlockSpec((None, None, blk, blk), lambda h, c: (direction, h, 0, 0))
    ins = [xc, dh, h_dir, sv, wa, wx] + ([dxc_prev] if has_prev else [])
    in_specs = [pl.BlockSpec((tc, blk), chunk), pl.BlockSpec((tc, blk), chunk),
                pl.BlockSpec((seq, blk), lambda h, c: (0, h)),
                pl.BlockSpec((16, blk), lambda h, c: (0, h)), wspec, wspec]
    if has_prev:
        in_specs.append(pl.BlockSpec((tc, blk), chunk))
    return _pcall(
        "lru_bwd_%d" % direction, kern, (n_h, n_chunks), in_specs,
        [pl.BlockSpec((tc, blk), chunk),
         pl.BlockSpec((None, blk, blk), lambda h, c: (h, 0, 0)),
         pl.BlockSpec((None, blk, blk), lambda h, c: (h, 0, 0)),
         pl.BlockSpec((SUBLANES, blk), lambda h, c: (0, h))],
        [jax.ShapeDtypeStruct((seq, d), F32),
         jax.ShapeDtypeStruct((n_h, blk, blk), F32),
         jax.ShapeDtypeStruct((n_h, blk, blk), F32),
         jax.ShapeDtypeStruct((SUBLANES, d), F32)],
        ins,
        scratch=[pltpu.VMEM((tc, blk), F32), pltpu.VMEM((tc, blk), F32),
                 pltpu.VMEM((SUBLANES, blk), F32), pltpu.VMEM((SUBLANES, blk), F32)])


def _conv_bwd(z, dxc, sv, dz, seq, d, tc, tcol):
    n_chunks = seq // tc
    lru_off = d // tcol

    def kern(z_ref, dx_ref, sv_ref, dz_in, dz_ref, sm_ref):
        del dz_in
        c = pl.program_id(1)
        uext = _ext(z_ref, c, n_chunks, tc, seq)
        dext = _ext(dx_ref, c, n_chunks, tc, seq)
        dbody = dext[HALO:HALO + tc]

        @pl.when(c == 0)
        def _():
            sm_ref[...] = jnp.zeros_like(sm_ref)

        du = jnp.zeros_like(dbody)
        for k in range(4):
            du = du + _shifted(dext, 2 - k, tc) * sv_ref[SV_CONV + k:SV_CONV + k + 1, :]
            sm_ref[k:k + 1, :] += _colsum(dbody * _shifted(uext, k - 2, tc))
        sm_ref[4:5, :] += _colsum(dbody)
        dz_ref[...] = du.astype(BF16)

    return _pcall(
        "conv_bwd", kern, (d // tcol, n_chunks),
        [pl.BlockSpec((seq, tcol), lambda j, c: (0, lru_off + j)),
         pl.BlockSpec((seq, tcol), lambda j, c: (0, j)),
         pl.BlockSpec((16, tcol), lambda j, c: (0, j)),
         pl.BlockSpec(memory_space=pl.ANY)],
        [pl.BlockSpec((None, tc, tcol), lambda j, c: (3, c, j)),
         pl.BlockSpec((SUBLANES, tcol), lambda j, c: (0, j))],
        [jax.ShapeDtypeStruct(dz.shape, dz.dtype), jax.ShapeDtypeStruct((SUBLANES, d), F32)],
        [z, dxc, sv, dz], aliases={3: 0})


BIG = (("w_in", "col"), ("pool_w", "row"), ("lru_wa", "row"), ("lru_wx", "row"), ("w_pool_up", "row"),
       ("w_lru_up", "row"), ("w_out", "row"), ("w_ff1", "col"), ("w_ff2", "row"))


def _shard_view(w, fam):
    if fam == "col":
        return w.reshape(w.shape[-2:])
    return w.reshape((-1,) + w.shape[-2:])


def _full_shape(sv_shape, fam):
    if fam == "col":
        return (sv_shape[0], N_DEV * sv_shape[1])
    return (sv_shape[0], N_DEV * sv_shape[1], sv_shape[2])


def _slot(ref, fam, p, size, part=None):
    if fam == "lead":
        return ref.at[p]
    k, n, span = part or (0, 1, 1)
    unit = size // n
    start = pl.multiple_of(p * size + k * unit, unit)
    if fam == "col":
        return ref.at[:, pl.ds(start, span * unit)]
    return ref.at[:, pl.ds(start, span * unit), :]


def _shard_part(ref, fam, size, part):
    if part is None or fam == "lead":
        return ref
    k, n, span = part
    unit = size // n
    if fam == "col":
        return ref.at[:, pl.ds(k * unit, span * unit)]
    return ref.at[:, pl.ds(k * unit, span * unit), :]


def _shard_extent(shape, fam):
    return shape[1]


def _coords():
    return lax.axis_index("x"), lax.axis_index("y"), lax.axis_index("c")


def _ag_job(shards, fams, parts=None, into=None):
    n = len(shards)
    parts = list(parts) if parts is not None else [None] * n
    into = list(into) if into is not None else [None] * n
    fulls = []
    for s, fam in zip(shards, fams):
        if fam == "lead":
            fulls.append(jax.ShapeDtypeStruct((N_DEV,) + s.shape, s.dtype))
        else:
            fulls.append(jax.ShapeDtypeStruct(_full_shape(s.shape, fam), s.dtype))
    sizes = [1 if fam == "lead" else _shard_extent(s.shape, fam) for s, fam in zip(shards, fams)]
    given = [a for a in range(n) if into[a] is not None]

    def ctx(ins, outs, sems):
        send, recv, loc = sems
        x, y, c = _coords()
        chips = [(1 - x, y), (x, 1 - y), (1 - x, 1 - y)]

        def mine(a):
            return _shard_part(ins[a], fams[a], sizes[a], parts[a])

        def copy(a, k, owner, to, src=None):
            dst = _slot(outs[a], fams[a], owner, sizes[a], parts[a])
            return pltpu.make_async_remote_copy(
                src_ref=dst if src is None else src, dst_ref=dst,
                send_sem=send.at[a, k], recv_sem=recv.at[a, k], device_id=to, device_id_type=MESH)

        def local(a):
            return pltpu.make_async_copy(
                mine(a), _slot(outs[a], fams[a], 4 * x + 2 * y + c, sizes[a], parts[a]), loc.at[a])

        return x, y, c, chips, copy, local, mine

    def start(ins, outs, sems):
        x, y, c, chips, copy, local, mine = ctx(ins, outs, sems)
        me = 4 * x + 2 * y + c
        for a in range(n):
            local(a).start()
            copy(a, 0, me, (x, y, 1 - c), mine(a)).start()
            for j, (cx, cy) in enumerate(chips):
                copy(a, 1 + j, me, (cx, cy, c), mine(a)).start()

    def mid(ins, outs, sems):
        x, y, c, chips, copy, _, _ = ctx(ins, outs, sems)
        for a in range(n):
            for j, (cx, cy) in enumerate(chips):
                owner = 4 * cx + 2 * cy + c
                copy(a, 1 + j, owner, (x, y, c)).wait_recv()
                copy(a, 4 + j, owner, (x, y, 1 - c)).start()

    def finish(ins, outs, sems):
        x, y, c, chips, copy, local, mine = ctx(ins, outs, sems)
        me = 4 * x + 2 * y + c
        for a in range(n):
            copy(a, 0, 4 * x + 2 * y + (1 - c), (x, y, c)).wait_recv()
            for j, (cx, cy) in enumerate(chips):
                copy(a, 4 + j, 4 * cx + 2 * cy + (1 - c), (x, y, c)).wait_recv()
            copy(a, 0, me, (x, y, 1 - c), mine(a)).wait_send()
            for j, (cx, cy) in enumerate(chips):
                copy(a, 1 + j, me, (cx, cy, c), mine(a)).wait_send()
                copy(a, 4 + j, 4 * cx + 2 * cy + c, (x, y, 1 - c)).wait_send()
            local(a).wait()

    sems = [pltpu.SemaphoreType.DMA((n, 7)), pltpu.SemaphoreType.DMA((n, 7)), pltpu.SemaphoreType.DMA((n,))]
    job = _Job(list(shards) + [into[a] for a in given], fulls, sems, start, finish, mid)
    job.alias = {n + i: a for i, a in enumerate(given)}
    return job


def _rs_sibling_job(fulls, fams, sizes):
    n = len(fulls)
    outs = []
    for f, fam, sz in zip(fulls, fams, sizes):
        if fam == "col":
            outs.append(jax.ShapeDtypeStruct((4, f.shape[0], sz), f.dtype))
        else:
            outs.append(jax.ShapeDtypeStruct((4, f.shape[0], sz, f.shape[2]), f.dtype))

    def copies(ins, rcv, sems):
        send, recv = sems
        x, y, c = _coords()
        return [pltpu.make_async_remote_copy(
            src_ref=_slot(ins[a], fams[a], 2 * q + (1 - c), sizes[a]), dst_ref=rcv[a].at[q],
            send_sem=send.at[a, q], recv_sem=recv.at[a, q], device_id=(x, y, 1 - c), device_id_type=MESH)
            for a in range(n) for q in range(4)]

    def start(ins, rcv, sems):
        for cp in copies(ins, rcv, sems):
            cp.start()

    def finish(ins, rcv, sems):
        for cp in copies(ins, rcv, sems):
            cp.wait()

    return _Job(fulls, outs, [pltpu.SemaphoreType.DMA((n, 4)), pltpu.SemaphoreType.DMA((n, 4))], start, finish)


def _rs_chips_job(parts):
    n = len(parts)
    outs = [jax.ShapeDtypeStruct((3,) + p.shape[1:], p.dtype) for p in parts]

    def copies(ins, rcv, sems):
        send, recv = sems
        x, y, c = _coords()
        cps = []
        for a in range(n):
            for r in (1, 2, 3):
                tx, ty = (1 - x) if r & 2 else x, (1 - y) if r & 1 else y
                cps.append(pltpu.make_async_remote_copy(
                    src_ref=ins[a].at[2 * tx + ty], dst_ref=rcv[a].at[r - 1],
                    send_sem=send.at[a, r - 1], recv_sem=recv.at[a, r - 1],
                    device_id=(tx, ty, c), device_id_type=MESH))
        return cps

    def start(ins, rcv, sems):
        for cp in copies(ins, rcv, sems):
            cp.start()

    def finish(ins, rcv, sems):
        for cp in copies(ins, rcv, sems):
            cp.wait()

    return _Job(parts, outs, [pltpu.SemaphoreType.DMA((n, 3)), pltpu.SemaphoreType.DMA((n, 3))], start, finish)


def _tile_rows(rows, cols):
    tr = rows
    while tr * cols > (1 << 18) and tr % (2 * SUBLANES) == 0:
        tr //= 2
    return tr


def _rs_add(name, full, recv_a, fam, size, cidx):
    if fam == "col":
        rows = full.shape[0]
        tr = _tile_rows(rows, size)
        grid = (4, rows // tr)
        f_spec = pl.BlockSpec((tr, size), lambda q, i, cr: (i, 2 * q + cr[0]))
        s_spec = pl.BlockSpec((None, tr, size), lambda q, i, cr: (q, i, 0))
    else:
        nb, cols = full.shape[0], full.shape[2]
        tr = _tile_rows(nb * size, cols) // nb if nb > 1 else _tile_rows(size, cols)
        nt = size // tr
        grid = (4, nt)
        f_spec = pl.BlockSpec((nb, tr, cols), lambda q, i, cr: (0, (2 * q + cr[0]) * nt + i, 0))
        s_spec = pl.BlockSpec((None, nb, tr, cols), lambda q, i, cr: (q, 0, i, 0))

    token = [t for t in _ORDER if t is not full]

    def kern(c_ref, f_ref, r_ref, *rest):
        del c_ref
        rest[-1][...] = (f_ref[...] + r_ref[...]).astype(BF16)

    out = pl.pallas_call(
        kern, name=name,
        grid_spec=pltpu.PrefetchScalarGridSpec(num_scalar_prefetch=1, grid=grid,
                                               in_specs=[f_spec, s_spec] + [HBM_SPEC] * len(token), out_specs=s_spec),
        out_shape=jax.ShapeDtypeStruct(recv_a.shape, BF16),
        compiler_params=_cparams(("arbitrary", "arbitrary"), 32),
    )(cidx, full, recv_a, *token)
    _ORDER[:] = [out]
    return out


def _adam(w, g, m, v):
    m2 = ADAM_B1 * m + (1.0 - ADAM_B1) * g
    v2 = ADAM_B2 * v + (1.0 - ADAM_B2) * (g * g)
    m_hat = m2 / (1.0 - ADAM_B1 ** ADAM_STEP)
    v_hat = v2 / (1.0 - ADAM_B2 ** ADAM_STEP)
    delta = -ADAM_LR * (m_hat / (jnp.sqrt(v_hat) + ADAM_EPS) + ADAM_WD * w)
    return delta, m2, v2


def _rs_final_adam(name, parts, recv_b, w, m, v, fam, qidx):
    shp = w.shape
    pieces = parts if isinstance(parts, (list, tuple)) else [parts]
    recvs = recv_b if isinstance(recv_b, (list, tuple)) else [recv_b]
    n_p = len(pieces)
    first_blk = [0] * n_p
    if fam == "col":
        rows, cols = shp
        tr = _tile_rows(min(p.shape[1] for p in pieces), cols)
        per = [p.shape[1] // tr for p in pieces]
        for h in range(1, n_p):
            first_blk[h] = first_blk[h - 1] + per[h - 1]
        grid = (rows // tr,)
        w_spec = pl.BlockSpec((tr, cols), lambda i, qr: (i, 0))

        def piece_row(i, h):
            return jnp.clip(i - first_blk[h], 0, per[h] - 1)

        p_specs = [pl.BlockSpec((None, tr, cols), lambda i, qr, h=h: (qr[0], piece_row(i, h), 0))
                   for h in range(n_p)]
        r_specs = [pl.BlockSpec((3, tr, cols), lambda i, qr, h=h: (0, piece_row(i, h), 0)) for h in range(n_p)]
    else:
        assert n_p == 1
        nb, rows, cols = shp
        tr = _tile_rows(rows, cols)
        nt = rows // tr
        grid = (nb * nt,)
        w_spec = pl.BlockSpec((None, tr, cols), lambda i, qr: (i // nt, i % nt, 0))
        p_specs = [pl.BlockSpec((None, None, tr, cols), lambda i, qr: (qr[0], i // nt, i % nt, 0))]
        r_specs = [pl.BlockSpec((3, None, tr, cols), lambda i, qr: (0, i // nt, i % nt, 0))]

    token = list(_ORDER)

    def kern(*refs):
        p_refs, r_refs = refs[1:1 + n_p], refs[1 + n_p:1 + 2 * n_p]
        w_ref, m_ref, v_ref = refs[1 + 2 * n_p:4 + 2 * n_p]
        g_out, d_out, m_out, v_out = refs[4 + 2 * n_p + len(token):]

        def total(h):
            p_ref, r_ref = p_refs[h], r_refs[h]
            return ((p_ref[...].astype(F32) + r_ref[0].astype(F32)) + r_ref[1].astype(F32)) + r_ref[2].astype(F32)

        g = total(0)
        for h in range(1, n_p):
            g = jnp.where(pl.program_id(0) >= first_blk[h], total(h), g)
        delta, m2, v2 = _adam(w_ref[...], g, m_ref[...], v_ref[...])
        g_out[...] = g
        d_out[...] = delta
        m_out[...] = m2
        v_out[...] = v2

    sh = jax.ShapeDtypeStruct(shp, F32)
    res = pl.pallas_call(
        kern, name=name,
        grid_spec=pltpu.PrefetchScalarGridSpec(
            num_scalar_prefetch=1, grid=grid,
            in_specs=p_specs + r_specs + [w_spec, w_spec, w_spec] + [HBM_SPEC] * len(token),
            out_specs=[w_spec] * 4),
        out_shape=[sh] * 4,
        compiler_params=_cparams(("arbitrary",), 32),
    )(qidx, *pieces, *recvs, w, m, v, *token)
    _ORDER[:] = [res[1]]
    return res


def _sum8(name, parts):
    def kern(p_ref, o_ref):
        acc = p_ref[0]
        for p in range(1, N_DEV):
            acc = acc + p_ref[p]
        o_ref[...] = acc

    return pl.pallas_call(
        kern, name=name, out_shape=jax.ShapeDtypeStruct(parts.shape[1:], F32),
        compiler_params=pltpu.CompilerParams(vmem_limit_bytes=32 << 20),
    )(parts)


def _adam_small(name, w, g, m, v):
    def kern(w_ref, g_ref, m_ref, v_ref, d_out, m_out, v_out):
        delta, m2, v2 = _adam(w_ref[...], g_ref[...], m_ref[...], v_ref[...])
        d_out[...] = delta
        m_out[...] = m2
        v_out[...] = v2

    sh = jax.ShapeDtypeStruct(w.shape, F32)
    return pl.pallas_call(kern, name=name, out_shape=[sh] * 3)(w, g, m, v)


class _NoComm:
    def __init__(self):
        self.grads = {}

    def settled_before(self, kernel_name):
        return []

    def after(self, kernel_name):
        pass

    def grad(self, name, g):
        self.grads[name] = g


def _local_step(x, target, wts, vec, comm=None):
    comm = comm or _NoComm()
    _ORDER[:] = []
    seq, d = x.shape
    sv = vec["sv"]
    ff = vec["b_ff1"].shape[1]
    n_in = 5 * d

    def issue(name, call):
        _ORDER.extend(comm.settled_before(name))
        res = call()
        comm.after(name)
        return res

    tc = min(512, seq)
    t1k, t512, t256 = min(1024, seq), min(512, seq), min(256, seq)
    n512 = min(512, d)
    tkd = d

    x_bf = x.astype(BF16)
    full = lambda i, j, k: (0, 0)

    def epi_store(acc, i, j, ex, outs):
        outs[0][...] = acc

    n_pass = W_IN_PASSES
    piece = n_in // (N_DEV * n_pass)
    tz = min(2048, seq)
    z = None
    for k in range(n_pass):
        w_piece = wts["w_in_piece_%d" % k]
        prev = [] if z is None else [(z, (None, None))]
        (z,) = issue("z_proj_%d" % k, lambda: _mm(
            "z_proj_%d" % k, "nn", (seq // tz, N_DEV, 1),
            x_bf, ((tz, d), lambda i, j, kk: (i, 0)), w_piece, ((d, piece), lambda i, j, kk: (0, j)),
            prev, [jax.ShapeDtypeStruct((seq, n_in), F32)],
            [((tz, piece), lambda i, j, kk, k=k: (i, n_pass * j + k))], epi_store,
            aliases={2: 0} if prev else None))

    pool_w, wa, wx = wts["pool_w"], wts["lru_wa"], wts["lru_wx"]
    blk = wa.shape[2]
    y_pool = issue("pool_fwd", lambda: _pool_fwd(z, pool_w, vec["pool_scale"], seq, d, tc))
    xc, h_f = issue("lru_fwd", lambda: _lru_fwd(z, sv, vec["conv_b"], wa, wx, seq, d, tc))
    h_b, y_lru = issue("lru_rev", lambda: _lru_rev(z, sv, wa, wx, xc, h_f, seq, d, tc))
    w_pu, w_lu = wts["w_pool_up"], wts["w_lru_up"]
    m_bf, p_a, p_b = issue("merge", lambda: _merge(y_pool, w_pu, y_lru, w_lu, z, seq, d, t1k, n512))
    w_out = wts["w_out"]

    def epi_ln1(acc, i, j, ex, outs):
        x_ref, bo, g1, b1 = ex
        s1 = DN_ALPHA * x_ref[...] + (acc + bo[...])
        xhat, rstd, x1 = _ln_fwd(s1, g1[...], b1[...])
        outs[0][...] = xhat
        outs[1][...] = x1.astype(BF16)
        outs[2][...] = rstd

    rowd = lambda t: ((t, d), lambda i, j, k: (i, 0))
    vecd = ((1, d), full)
    xhat1, x1_bf, rstd1 = issue("out_ln1", lambda: _mm(
        "out_ln1", "nn", (seq // t256, 1, 1), m_bf, rowd(t256), w_out, ((d, d), full),
        [(x, rowd(t256)), (vec["b_out"], vecd), (vec["ln1_g"], vecd), (vec["ln1_b"], vecd)],
        [jax.ShapeDtypeStruct((seq, d), F32), jax.ShapeDtypeStruct((seq, d), BF16),
         jax.ShapeDtypeStruct((seq, 1), F32)],
        [rowd(t256), rowd(t256), ((t256, 1), lambda i, j, k: (i, 0))], epi_ln1))
    w1 = wts["w_ff1"]

    def epi_ff1(acc, i, j, ex, outs):
        r = jnp.maximum(acc + ex[0][...], 0.0)
        outs[0][...] = r.astype(BF16)
        outs[1][...] = (r * r).astype(BF16)

    tile_f = ((t1k, n512), lambda i, j, k: (i, j))
    relu_h, hdn = issue("ff1", lambda: _mm(
        "ff1", "nn", (seq // t1k, ff // n512, 1), x1_bf, rowd(t1k), w1, ((d, n512), lambda i, j, k: (0, j)),
        [(vec["b_ff1"], ((1, n512), lambda i, j, k: (0, j)))],
        [jax.ShapeDtypeStruct((seq, ff), BF16)] * 2, [tile_f, tile_f], epi_ff1))
    w2 = wts["w_ff2"]

    def epi_ln2(acc, i, j, ex, outs):
        xh1, tgt, g1, b1, bf2, g2, b2 = ex
        ds_ref, dsb_ref, sm_ref, loss_ref = outs
        x1 = xh1[...] * g1[...] + b1[...]
        s2 = DN_ALPHA * x1 + (acc + bf2[...])
        xhat, rstd, y = _ln_fwd(s2, g2[...], b2[...])
        e = y - tgt[...]
        part = 0.5 * jnp.sum(jnp.mean(e * e, axis=-1, keepdims=True))
        dy = e * (1.0 / d)
        ds2 = _ln_bwd(dy, xhat, rstd, g2[...])
        ds_ref[...] = ds2
        dsb_ref[...] = ds2.astype(BF16)
        sm_ref[0:1, :] += _colsum(dy * xhat)
        sm_ref[1:2, :] += _colsum(dy)
        sm_ref[2:3, :] += _colsum(ds2)
        loss_ref[...] += jnp.full(loss_ref.shape, part, F32)

    def zero_tail(n_tail):
        def init(i, j, outs):
            @pl.when(i == 0)
            def _():
                for o in outs[-n_tail:]:
                    o[...] = jnp.zeros_like(o)
        return init

    def rows_epilogue(name, lead, extras, out_shapes, out_specs, epi, n_tail):
        ne = len(extras)

        def kern(*refs):
            i = pl.program_id(0)
            outs = refs[1 + ne:]
            zero_tail(n_tail)(i, 0, outs)
            epi(refs[0][...], i, 0, refs[1:1 + ne], outs)

        def spec(s):
            bs, f = s
            return pl.BlockSpec(bs, lambda i, f=f: f(i, 0, 0))

        return _pcall(name, kern, (seq // t256,), [spec(rowd(t256))] + [spec(s) for _, s in extras],
                      [spec(s) for s in out_specs], out_shapes, [lead] + [e for e, _ in extras])

    tkw2, tw = min(2048, ff), min(1024, d)
    (ff_out,) = _mm("ff2", "nn", (seq // t1k, d // tw, ff // tkw2), hdn, ((t1k, tkw2), lambda i, j, k: (i, k)),
                    w2, ((tkw2, tw), lambda i, j, k: (k, j)), [], [jax.ShapeDtypeStruct((seq, d), F32)],
                    [((t1k, tw), lambda i, j, k: (i, j))], None)
    ds2, ds2_bf, sm_ln2, loss_blk = issue("ff2_ln2", lambda: rows_epilogue(
        "ln2_loss", ff_out,
        [(xhat1, rowd(t256)), (target, rowd(t256)), (vec["ln1_g"], vecd), (vec["ln1_b"], vecd),
         (vec["b_ff2"], vecd), (vec["ln2_g"], vecd), (vec["ln2_b"], vecd)],
        [jax.ShapeDtypeStruct((seq, d), F32), jax.ShapeDtypeStruct((seq, d), BF16),
         jax.ShapeDtypeStruct((SUBLANES, d), F32), jax.ShapeDtypeStruct((SUBLANES, 128), F32)],
        [rowd(t256), rowd(t256), ((SUBLANES, d), full), ((SUBLANES, 128), full)], epi_ln2, 2))

    tkw = min(2048, seq)

    def dw(name, wname, a, b, m_dim, n_dim, b_spec=None, row0=0):
        _ORDER.extend(comm.settled_before(name))
        tn = min(1024, n_dim)
        tm = next(t for t in (1024, 768, 512, 256, 128) if m_dim % t == 0 and row0 % t == 0)
        b_spec = b_spec or ((tkw, tn), lambda i, j, k: (k, j))
        i0 = row0 // tm
        (out,) = _mm(
            name, "tn", (m_dim // tm, n_dim // tn, seq // tkw),
            a, ((tkw, tm), lambda i, j, k: (k, i0 + i)), b, b_spec, [],
            [jax.ShapeDtypeStruct((m_dim, n_dim), F32)], [((tm, tn), lambda i, j, k: (i, j))],
            epi_store if seq == tkw else None)
        comm.grad(wname, out)
        comm.after(name)

    dw("dw_ff2", "w_ff2", hdn, ds2_bf, ff, d)

    def epi_dpre(acc, i, j, ex, outs):
        dpre = acc * (2.0 * ex[0][...].astype(F32))
        outs[0][...] = dpre.astype(BF16)

        @pl.when(i == 0)
        def _():
            outs[1][...] = jnp.zeros_like(outs[1])

        outs[1][0:1, :] += _colsum(dpre)

    dpre, sm_bff1 = issue("dhdn", lambda: _mm(
        "dhdn", "nt", (seq // t1k, ff // n512, 1), ds2_bf, rowd(t1k), w2, ((n512, d), lambda i, j, k: (j, 0)),
        [(relu_h, tile_f)],
        [jax.ShapeDtypeStruct((seq, ff), BF16), jax.ShapeDtypeStruct((SUBLANES, ff), F32)],
        [tile_f, ((SUBLANES, n512), lambda i, j, k: (0, j))], epi_dpre, order="ji"))

    dw("dw_ff1", "w_ff1", x1_bf, dpre, d, ff)

    def epi_ln1b(acc, i, j, ex, outs):
        ds2_ref, xh1, rs1, g1 = ex
        ds_ref, dsb_ref, sm_ref = outs
        dy1 = acc + DN_ALPHA * ds2_ref[...]
        xhat = xh1[...]
        ds1 = _ln_bwd(dy1, xhat, rs1[...], g1[...])
        ds_ref[...] = ds1
        dsb_ref[...] = ds1.astype(BF16)
        sm_ref[0:1, :] += _colsum(dy1 * xhat)
        sm_ref[1:2, :] += _colsum(dy1)
        sm_ref[2:3, :] += _colsum(ds1)

    (dx1,) = _mm("dx1", "nt", (seq // t1k, d // tw, ff // tkw2), dpre, ((t1k, tkw2), lambda i, j, k: (i, k)),
                 w1, ((tw, tkw2), lambda i, j, k: (j, k)), [], [jax.ShapeDtypeStruct((seq, d), F32)],
                 [((t1k, tw), lambda i, j, k: (i, j))], None)
    ds1, ds1_bf, sm_ln1 = issue("dx1_ln1", lambda: rows_epilogue(
        "ln1_bwd", dx1,
        [(ds2, rowd(t256)), (xhat1, rowd(t256)), (rstd1, ((t256, 1), lambda i, j, k: (i, 0))),
         (vec["ln1_g"], vecd)],
        [jax.ShapeDtypeStruct((seq, d), F32), jax.ShapeDtypeStruct((seq, d), BF16),
         jax.ShapeDtypeStruct((SUBLANES, d), F32)],
        [rowd(t256), rowd(t256), ((SUBLANES, d), full)], epi_ln1b, 1))

    dw("dw_out", "w_out", m_bf, ds1_bf, d, d)
    n_n = d // n512
    tile_d = ((t512, n512), lambda i, j, k: (i, j))

    def epi_dm(acc, i, j, ex, outs):
        la, lb, pa, pb = ex
        ga, gb = _sig(la[...]), _sig(lb[...])
        outs[0][...] = (acc * ga).astype(BF16)
        outs[1][...] = (acc * gb).astype(BF16)
        outs[2][0] = (acc * pa[...].astype(F32) * ga * (1.0 - ga)).astype(BF16)
        outs[2][1] = (acc * pb[...].astype(F32) * gb * (1.0 - gb)).astype(BF16)

    dp_a, dp_b, dz = _mm(
        "dm", "nt", (seq // t512, n_n, 1), ds1_bf, rowd(t512), w_out, ((n512, d), lambda i, j, k: (j, 0)),
        [(z, ((t512, n512), lambda i, j, k: (i, 3 * n_n + j))),
         (z, ((t512, n512), lambda i, j, k: (i, 4 * n_n + j))), (p_a, tile_d), (p_b, tile_d)],
        [jax.ShapeDtypeStruct((seq, d), BF16), jax.ShapeDtypeStruct((seq, d), BF16),
         jax.ShapeDtypeStruct((5, seq, d), BF16)],
        [tile_d, tile_d, ((2, t512, n512), lambda i, j, k: (0, i, j))], epi_dm)

    dw("dw_pool_up", "w_pool_up", y_pool, dp_a, d, d)
    dw("dw_lru_up", "w_lru_up", y_lru, dp_b, d, d)

    def epi_bf(acc, i, j, ex, outs):
        outs[0][...] = acc.astype(BF16)

    (dy_pool,) = issue("dy_pool", lambda: _mm(
        "dy_pool", "nt", (seq // t512, n_n, 1), dp_a, rowd(t512), w_pu, ((n512, d), lambda i, j, k: (j, 0)), [],
        [jax.ShapeDtypeStruct((seq, d), BF16)], [tile_d], epi_bf))

    def epi_dylru(acc, i, j, ex, outs):
        hf, hb, ug, _ = ex
        u = ug[...]
        outs[0][...] = acc * _gelu(u)
        outs[1][...] = (acc * (hf[...] + hb[...]) * _gelu_grad(u)).astype(BF16)

    dz_in = dz
    dh, dz = issue("dy_lru", lambda: _mm(
        "dy_lru", "nt", (seq // t512, n_n, 1), dp_b, rowd(t512), w_lu, ((n512, d), lambda i, j, k: (j, 0)),
        [(h_f, tile_d), (h_b, tile_d), (z, ((t512, n512), lambda i, j, k: (i, 2 * n_n + j))),
         (dz_in, (None, None))],
        [jax.ShapeDtypeStruct((seq, d), F32), jax.ShapeDtypeStruct(dz_in.shape, BF16)],
        [tile_d, ((None, t512, n512), lambda i, j, k: (4, i, j))], epi_dylru, aliases={5: 1}))

    dz, g_pw, sm_pool = _pool_bwd(z, dy_pool, pool_w, vec["pool_scale"], dz, seq, d, tc)
    comm.grad("pool_w", g_pw)
    dxc0, g_wa0, g_wx0, sm_l0 = issue("lru_bwd_0", lambda: _lru_bwd(
        0, xc, dh, h_f, sv, wa, wx, None, seq, d, tc))
    dxc, g_wa1, g_wx1, sm_l1 = issue("lru_bwd_1", lambda: _lru_bwd(
        1, xc, dh, h_b, sv, wa, wx, dxc0, seq, d, tc))
    comm.grad("lru_wa", jnp.concatenate([g_wa0, g_wa1], axis=0))
    comm.grad("lru_wx", jnp.concatenate([g_wx0, g_wx1], axis=0))
    dz, sm_conv = _conv_bwd(z, dxc, sv, dz, seq, d, tc, blk)

    tnw = min(1024, d)
    per_seg = d // tnw
    seg_spec = ((None, tkw, tnw), lambda i, j, k: ((j // per_seg + 2) % 5, k, j % per_seg))
    lo_rows = 3 * d // 4
    dw("dw_in_lo", "w_in_lo", x_bf, dz, lo_rows, n_in, b_spec=seg_spec)
    dw("dw_in_hi", "w_in_hi", x_bf, dz, d - lo_rows, n_in, b_spec=seg_spec, row0=lo_rows)

    nk = d // tkd

    def epi_dx(acc, i, j, ex, outs):
        outs[0][...] = acc + DN_ALPHA * ex[0][...]

    (grad_x,) = issue("dx", lambda: _mm(
        "dx", "nt", (seq // t512, 1, n_in // tkd), dz,
        ((None, t512, tkd), lambda i, j, k: ((k // nk + 2) % 5, i, k % nk)),
        wts["w_in"], ((d, tkd), lambda i, j, k: (0, k)), [(ds1, rowd(t512))],
        [jax.ShapeDtypeStruct((seq, d), F32)], [rowd(t512)], epi_dx, acc_shape=(t512, d)))

    small = {"ln2": sm_ln2, "b_ff1": sm_bff1, "ln1": sm_ln1, "pool": sm_pool, "lru0": sm_l0, "lru1": sm_l1,
             "conv": sm_conv}
    return loss_blk[0, 0], grad_x, small


REP = ("pool_scale", "conv_b", "b_out", "ln1_g", "ln1_b", "b_ff2", "ln2_g", "ln2_b")
SHARDED_SMALL = (("conv_w", 4), ("lru_ba", 2), ("lru_bx", 2), ("lru_lambda", 2))
WEIGHT_ORDER = ("w_in", "pool_w", "pool_scale", "conv_w", "conv_b", "lru_wa", "lru_ba", "lru_wx", "lru_bx",
                "lru_lambda", "w_pool_up", "w_lru_up", "w_out", "b_out", "ln1_g", "ln1_b", "w_ff1", "b_ff1",
                "w_ff2", "b_ff2", "ln2_g", "ln2_b")


def _pad_rows(a, rows):
    return jnp.concatenate([a, jnp.zeros((rows - a.shape[0], a.shape[1]), a.dtype)], axis=0)


def kernel(x, w_in, pool_w, pool_scale, conv_w, conv_b, lru_wa, lru_ba, lru_wx, lru_bx, lru_lambda, w_pool_up, w_lru_up, w_out, b_out, ln1_g, ln1_b, w_ff1, b_ff1, w_ff2, b_ff2, ln2_g, ln2_b, loss_target, m_w_in, m_pool_w, m_pool_scale, m_conv_w, m_conv_b, m_lru_wa, m_lru_ba, m_lru_wx, m_lru_bx, m_lru_lambda, m_w_pool_up, m_w_lru_up, m_w_out, m_b_out, m_ln1_g, m_ln1_b, m_w_ff1, m_b_ff1, m_w_ff2, m_b_ff2, m_ln2_g, m_ln2_b, v_w_in, v_pool_w, v_pool_scale, v_conv_w, v_conv_b, v_lru_wa, v_lru_ba, v_lru_wx, v_lru_bx, v_lru_lambda, v_w_pool_up, v_w_lru_up, v_w_out, v_b_out, v_ln1_g, v_ln1_b, v_w_ff1, v_b_ff1, v_w_ff2, v_b_ff2, v_ln2_g, v_ln2_b):
    args = dict(locals())
    w = {n: args[n] for n in WEIGHT_ORDER}
    mom = {n: args["m_" + n] for n in WEIGHT_ORDER}
    var = {n: args["v_" + n] for n in WEIGHT_ORDER}
    seq, d = x.shape[1], x.shape[2]
    n_heads, blk = lru_wa.shape[2], lru_wa.shape[4]
    n_groups = pool_w.shape[1]
    ff = b_ff1.shape[1]
    cx, cy, cc = _coords()
    me = 4 * cx + 2 * cy + cc
    cidx = jnp.reshape(cc, (1,)).astype(jnp.int32)
    qidx = jnp.reshape(2 * cx + cy, (1,)).astype(jnp.int32)

    fam_of = dict(BIG)
    sviews = {n: _shard_view(w[n], fam) for n, fam in BIG}
    size_of = {n: sviews[n].shape[1] for n, _ in BIG}
    for half in ("w_in_lo", "w_in_hi"):
        fam_of[half], size_of[half] = fam_of["w_in"], size_of["w_in"]
    wts = {}

    def take_gathered(names, arrays):
        for n, g in zip(names, arrays):
            if n == "pool_w":
                g = g.reshape(n_groups, d // n_groups, d // n_groups)
            elif n in ("lru_wa", "lru_wx"):
                g = g.reshape(2, n_heads, blk, blk)
            elif fam_of[n] == "row":
                g = g.reshape(g.shape[1:])
            wts[n] = g

    shard_bf = {n: sviews[n].astype(BF16) for n, _ in BIG}
    piece = sviews["w_in"].shape[1] // W_IN_PASSES
    for k in range(W_IN_PASSES):
        name = "w_in_piece_%d" % k
        shard_bf[name] = shard_bf["w_in"][:, k * piece:(k + 1) * piece]
        fam_of[name] = "col"
    partial = {}

    def gather_job(items, extra=()):
        return _ag_job([shard_bf[n] for n, _ in items] + [e for e, _ in extra],
                       [fam_of[n] for n, _ in items] + [f for _, f in extra],
                       parts=[p for _, p in items] + [None] * len(extra),
                       into=[partial.get(n) if p else None for n, p in items] + [None] * len(extra))

    def take_pieces(items, arrays):
        for (n, p), g in zip(items, arrays):
            if p is None or p[0] + p[2] == p[1]:
                partial.pop(n, None)
                take_gathered([n], [g])
            else:
                partial[n] = g

    launched = [0]

    def on_sequencer(kind, job):
        launched[0] += 1
        return _sequencer_job("sq_%s_%d" % (kind, launched[0]), job, launched[0] % 2)

    class Plan:
        sibling = {"dw_ff2": ("w_ff2",), "dw_ff1": ("w_ff1",), "dw_lru_up": ("w_out", "w_pool_up", "w_lru_up"),
                   "dw_in_lo": ("w_in_lo",), "dw_in_hi": ("w_in_hi", "pool_w", "lru_wa", "lru_wx")}
        chips = {"dw_ff1": ("w_ff2",), "dw_out": ("w_ff1",), "dy_lru": ("w_out", "w_pool_up", "w_lru_up"),
                 "dw_in_hi": ("w_in_lo",), "dx": ("w_in_hi", "pool_w", "lru_wa", "lru_wx")}

        def __init__(self):
            self.grads, self.recv_a, self.parts, self.recv_b = {}, {}, {}, {}

        def grad(self, name, g):
            self.grads[name] = g if fam_of[name] == "col" else g.reshape((-1,) + g.shape[-2:])

        settle = {"dw_ff1": ("w_in",), "dw_out": ("w_ff2",), "dy_pool": ("w_ff1",), "lru_bwd_1": ("w_out",)}
        settle_add = {"dx": ("w_in_lo",)}

        def settled_before(self, host):
            return [wts[n] if n == "w_in" else self.recv_b[n] for n in self.settle.get(host, ())]

        def after(self, host):
            if host in self.chips:
                names = self.chips[host]
                _ORDER.extend(self.recv_b[n] for n in self.settle_add.get(host, ()))
                for n in names:
                    self.parts[n] = _rs_add("rs_add_" + n, self.grads[n], self.recv_a[n], fam_of[n], size_of[n], cidx)
                res = on_sequencer("chips", _rs_chips_job([self.parts[n] for n in names]))
                self.recv_b.update(zip(names, res))
            if host in self.sibling:
                names = self.sibling[host]
                res = on_sequencer("sibling", _rs_sibling_job(
                    [self.grads[n] for n in names], [fam_of[n] for n in names], [size_of[n] for n in names]))
                self.recv_a.update(zip(names, res))

    first = (("w_in_piece_0", None),)
    sv_shard = _pad_rows(jnp.concatenate([w[n].reshape(r, -1) for n, r in SHARDED_SMALL], axis=0), 16)
    gathered = on_sequencer("gather", gather_job(first, [(sv_shard, "col")]))
    take_pieces(first, gathered[:-1])
    vec = {n: w[n] for n in REP}
    vec["b_ff1"] = b_ff1
    vec["sv"] = gathered[-1]
    queue = [(("w_in_piece_%d" % k, None),) for k in range(1, W_IN_PASSES)]
    queue += [(("pool_w", None), ("lru_wa", None), ("lru_wx", None)), (("w_pool_up", None), ("w_lru_up", None)),
              (("w_out", None),), (("w_ff1", None),), (("w_ff2", None),), (("w_in", None),)]
    for items in queue:
        take_pieces(items, on_sequencer("gather", gather_job(items)))

    plan = Plan()
    loss_part, grad_x, small = _local_step(x.reshape(seq, d), loss_target.reshape(seq, d), wts, vec, plan)
    loss = lax.psum(loss_part, AXES)

    out_g, out_d, out_m, out_v = {}, {}, {}, {}
    for n, fam in sorted(BIG, key=lambda nf: nf[0] in ("w_in", "pool_w", "lru_wa", "lru_wx")):
        halves = [n + "_lo", n + "_hi"] if n == "w_in" else [n]
        res = _rs_final_adam("adam_" + n, [plan.parts[h] for h in halves], [plan.recv_b[h] for h in halves],
                             sviews[n], _shard_view(mom[n], fam), _shard_view(var[n], fam), fam, qidx)
        out_g[n], out_d[n], out_m[n], out_v[n] = [r.reshape(w[n].shape) for r in res]

    rows = [small["pool"][0:1], small["conv"][4:5], small["ln1"][2:3], small["ln1"][0:1], small["ln1"][1:2],
            small["ln2"][2:3], small["ln2"][0:1], small["ln2"][1:2], small["b_ff1"][0:1].reshape(ff // d, d),
            small["conv"][0:4], small["lru0"][0:1], small["lru1"][0:1], small["lru0"][1:2], small["lru1"][1:2],
            small["lru0"][2:3], small["lru1"][2:3]]
    n_rep = len(REP) + ff // d
    n_rows = n_rep + sum(r for _, r in SHARDED_SMALL)
    pad_rows = -(-n_rows // SUBLANES) * SUBLANES
    packed = _pad_rows(jnp.concatenate(rows, axis=0), pad_rows)
    (all_small,) = _run_job("ag_small", _ag_job([packed], ["lead"]))
    g_small = _sum8("sum_small", all_small)

    def pack_rep(t):
        return jnp.concatenate([t[n] for n in REP] + [t["b_ff1"].reshape(ff // d, d)], axis=0)

    def pack_sh(t):
        return jnp.concatenate([t[n].reshape(r, -1) for n, r in SHARDED_SMALL], axis=0)

    g_rep = g_small[:n_rep]
    cs = d // N_DEV
    g_sh = lax.dynamic_slice_in_dim(g_small[n_rep:n_rows], me * cs, cs, axis=1)
    d_rep, m_rep, v_rep = _adam_small("adam_rep", pack_rep(w), g_rep, pack_rep(mom), pack_rep(var))
    d_sh, m_sh, v_sh = _adam_small("adam_sharded", pack_sh(w), g_sh, pack_sh(mom), pack_sh(var))

    def unpack(rep_t, sh_t, dst):
        for i, n in enumerate(REP):
            dst[n] = rep_t[i:i + 1].reshape(w[n].shape)
        dst["b_ff1"] = rep_t[len(REP):n_rep].reshape(w["b_ff1"].shape)
        r0 = 0
        for n, r in SHARDED_SMALL:
            dst[n] = sh_t[r0:r0 + r].reshape(w[n].shape)
            r0 += r

    unpack(g_rep, g_sh, out_g)
    unpack(d_rep, d_sh, out_d)
    unpack(m_rep, m_sh, out_m)
    unpack(v_rep, v_sh, out_v)

    _ORDER[:] = []
    outs = [loss, grad_x.reshape(x.shape)]
    for t in (out_g, out_d, out_m, out_v):
        outs += [t[n] for n in WEIGHT_ORDER]
    return tuple(outs)
```

```python
import jax
import jax.numpy as jnp
from jax import lax
from jax.experimental import pallas as pl
from jax.experimental.pallas import tpu as pltpu
from jax.experimental.pallas import tpu_sc as plsc

F32 = jnp.float32
BF16 = jnp.bfloat16
MESH = pl.DeviceIdType.MESH
AXES = ("x", "y", "c")
N_DEV = 8

DN_ALPHA = 2.0 ** 0.25
LN_EPS = 1e-5
LRU_C = 8.0
ADAM_LR = 0.001
ADAM_B1 = 0.9
ADAM_B2 = 0.999
ADAM_EPS = 1e-08
ADAM_WD = 0.01
ADAM_STEP = 10
GELU_C = 0.7978845608028654
GELU_K = 0.044715

W_IN_PASSES = 5
HALO = 16
SUBLANES = 8
VMEM_MB = 56


def _cparams(sem, vmem_mb=VMEM_MB):
    return pltpu.CompilerParams(dimension_semantics=sem, vmem_limit_bytes=vmem_mb << 20)


HBM_SPEC = pl.BlockSpec(memory_space=pl.ANY)


class _Job:
    def __init__(self, ins, outs, sems, start, finish, mid=None):
        self.ins, self.outs, self.sems = list(ins), list(outs), list(sems)
        self.start, self.finish, self.mid = start, finish, mid
        self.results = None


_ORDER = []


def _pcall(name, body, grid, in_specs, out_specs, out_shape, inputs, scratch=(), aliases=None,
           vmem_mb=VMEM_MB):
    token = [t for t in _ORDER if not any(t is a for a in inputs)]
    n_in = len(inputs)

    def ordered(*refs):
        return body(*refs[:n_in], *refs[n_in + len(token):])

    res = pl.pallas_call(ordered, name=name, grid=grid, in_specs=list(in_specs) + [HBM_SPEC] * len(token),
                         out_specs=list(out_specs), out_shape=list(out_shape), scratch_shapes=list(scratch),
                         input_output_aliases=aliases or {},
                         compiler_params=_cparams(("arbitrary",) * len(grid), vmem_mb))(*inputs, *token)
    _ORDER[:] = [res[0]]
    return res


def _run_job(name, job):
    ji, jo = len(job.ins), len(job.outs)

    def body(*refs):
        jins, jouts, sems = refs[:ji], refs[ji:ji + jo], refs[ji + jo:]
        job.start(jins, jouts, sems)
        if job.mid is not None:
            job.mid(jins, jouts, sems)
        job.finish(jins, jouts, sems)

    res = pl.pallas_call(body, name=name, in_specs=[HBM_SPEC] * ji, out_specs=[HBM_SPEC] * jo,
                         out_shape=job.outs, scratch_shapes=job.sems)(*job.ins)
    job.results = list(res)
    return job.results


def _sequencer_job(name, job, collective_id):
    ji, jo = len(job.ins), len(job.outs)

    def body(*refs):
        jins, jouts, sems = refs[:ji], refs[ji:ji + jo], refs[ji + jo:]
        barrier = pltpu.get_barrier_semaphore()
        x, y, c = lax.axis_index("x"), lax.axis_index("y"), lax.axis_index("c")
        for r in range(1, N_DEV):
            peer = ((1 - x) if r & 4 else x, (1 - y) if r & 2 else y, (1 - c) if r & 1 else c)
            pl.semaphore_signal(barrier, inc=1, device_id=peer, device_id_type=MESH)
        pl.semaphore_wait(barrier, N_DEV - 1)
        job.start(jins, jouts, sems)
        if job.mid is not None:
            job.mid(jins, jouts, sems)
        job.finish(jins, jouts, sems)

    res = pl.kernel(
        body, name=name, out_type=job.outs, mesh=plsc.ScalarSubcoreMesh(axis_name="sequencer", num_cores=1),
        scratch_types=job.sems, compiler_params=pltpu.CompilerParams(collective_id=collective_id),
    )(*job.ins)
    job.results = list(res)
    return job.results


def _dot(mode, a, b):
    if mode == "nn":
        dims = (((1,), (0,)), ((), ()))
    elif mode == "nt":
        dims = (((1,), (1,)), ((), ()))
    else:
        dims = (((0,), (0,)), ((), ()))
    return lax.dot_general(a, b, dims, preferred_element_type=F32)


def _sig(x):
    return 0.5 * jnp.tanh(0.5 * x) + 0.5


def _gelu(x):
    t = jnp.tanh(GELU_C * (x + GELU_K * x * x * x))
    return 0.5 * x * (1.0 + t)


def _gelu_grad(x):
    x2 = x * x
    t = jnp.tanh(GELU_C * (x + GELU_K * x * x2))
    return 0.5 * (1.0 + t) + 0.5 * x * (1.0 - t * t) * GELU_C * (1.0 + 3.0 * GELU_K * x2)


def _colsum(v):
    return jnp.sum(v, axis=0, keepdims=True)


def _mm(name, mode, grid, a, a_spec, b, b_spec, extras, out_shapes, out_specs, epi, *,
        order="ij", acc_shape=None, aliases=None, vmem_mb=VMEM_MB):
    gm, gn, gk = grid

    def spec(s):
        bs, f = s
        if f is None:
            return pl.BlockSpec(memory_space=pl.ANY)
        if order == "ij":
            return pl.BlockSpec(bs, lambda i, j, k, f=f: f(i, j, k))
        return pl.BlockSpec(bs, lambda j, i, k, f=f: f(i, j, k))

    ne, no = len(extras), len(out_shapes)

    def kern(*refs):
        a_ref, b_ref = refs[0], refs[1]
        ex = refs[2:2 + ne]
        outs = refs[2 + ne:2 + ne + no]
        if order == "ij":
            i, j = pl.program_id(0), pl.program_id(1)
        else:
            j, i = pl.program_id(0), pl.program_id(1)
        k = pl.program_id(2)
        prod = _dot(mode, a_ref[...], b_ref[...])
        if gk == 1:
            epi(prod, i, j, ex, outs)
        elif epi is None:
            @pl.when(k == 0)
            def _():
                outs[0][...] = prod

            @pl.when(k > 0)
            def _():
                outs[0][...] += prod
        else:
            acc = refs[-1]

            @pl.when(k == 0)
            def _():
                acc[...] = prod

            @pl.when(k > 0)
            def _():
                acc[...] += prod

            @pl.when(k == gk - 1)
            def _():
                epi(acc[...], i, j, ex, outs)

    g = (gm, gn, gk) if order == "ij" else (gn, gm, gk)
    scratch = [pltpu.VMEM(acc_shape, F32)] if gk > 1 and epi is not None else []
    return _pcall(name, kern, g, [spec(a_spec), spec(b_spec)] + [spec(s) for _, s in extras],
                  [spec(s) for s in out_specs], out_shapes, [a, b] + [e for e, _ in extras],
                  scratch=scratch, aliases=aliases, vmem_mb=vmem_mb)


def _ext(ref, c, n_chunks, tc, seq):
    c0 = pl.multiple_of(c * tc, tc)
    body = ref[pl.ds(c0, tc), :].astype(F32)
    t0 = pl.multiple_of(jnp.maximum(c0 - HALO, 0), HALO)
    b0 = pl.multiple_of(jnp.minimum(c0 + tc, seq - HALO), HALO)
    top = ref[pl.ds(t0, HALO), :].astype(F32)
    bot = ref[pl.ds(b0, HALO), :].astype(F32)
    top = jnp.where(c > 0, top, 0.0)
    bot = jnp.where(c < n_chunks - 1, bot, 0.0)
    return jnp.concatenate([top, body, bot], axis=0)


def _shifted(vext, off, tc):
    n = vext.shape[0]
    r = vext if off == 0 else pltpu.roll(vext, (n - off) % n, 0)
    return r[HALO:HALO + tc]


def _win_sum(vext, g, extra, tc):
    s2 = vext + pltpu.roll(vext, 1, 0)
    s4 = s2 + pltpu.roll(s2, 2, 0)
    s8 = s4 + pltpu.roll(s4, 4, 0)
    s16 = s8 + pltpu.roll(s8, 8, 0)
    outs = [_shifted(s, extra + hw - 1, tc) for s, hw in ((s2, 1), (s4, 2), (s8, 4), (s16, 8))]
    return jnp.where(g == 0, outs[0], jnp.where(g == 1, outs[1], jnp.where(g == 2, outs[2], outs[3])))


def _win_cnt(t, hw, seq):
    return (jnp.minimum(t + hw, seq) - jnp.maximum(t - hw, 0)).astype(F32)


def _pool_d(uext, g, c, tc, seq):
    hw = jnp.left_shift(1, g)
    t = c * tc + lax.broadcasted_iota(jnp.int32, (tc, 1), 0)
    ws = _win_sum(uext, g, 0, tc)
    return ws / _win_cnt(t, hw, seq) - uext[HALO:HALO + tc]


def _scan_tiles(a_ref, b_ref, h_ref, carry_ref, n_tiles, reverse):
    blk = a_ref.shape[1]
    row = lax.broadcasted_iota(jnp.int32, (SUBLANES, blk), 0)

    def tile(j, hc):
        jj = (n_tiles - 1 - j) if reverse else j
        off = pl.multiple_of(jj * SUBLANES, SUBLANES)
        a = a_ref[pl.ds(off, SUBLANES), :]
        b = b_ref[pl.ds(off, SUBLANES), :]
        for kk in (1, 2, 4):
            sh = (SUBLANES - kk) if reverse else kk
            a_s = pltpu.roll(a, sh, 0)
            b_s = pltpu.roll(b, sh, 0)
            m = (row < SUBLANES - kk) if reverse else (row >= kk)
            a_s = jnp.where(m, a_s, 1.0)
            b_s = jnp.where(m, b_s, 0.0)
            b = a * b_s + b
            a = a * a_s
        h = a * hc + b
        h_ref[pl.ds(off, SUBLANES), :] = h
        return h[0:1, :] if reverse else h[SUBLANES - 1:SUBLANES, :]

    group = SUBLANES if n_tiles % SUBLANES == 0 else 1

    def tiles(jg, hc):
        for u in range(group):
            hc = tile(jg * group + u, hc)
        return hc

    hc = lax.fori_loop(0, n_tiles // group, tiles, carry_ref[0:1, :])
    carry_ref[0:1, :] = hc


def _lru_k(lam):
    y = -lam
    e = jnp.exp(-jnp.abs(y))
    u = 1.0 + e
    l1p = jnp.where(u == 1.0, e, jnp.log(u) * (e / (u - 1.0)))
    return -LRU_C * (jnp.maximum(y, 0.0) + l1p)


def _lru_gates(xc, wa, wx, ba, bx, lam):
    xb = xc.astype(BF16)
    r = _sig(jnp.dot(xb, wa, preferred_element_type=F32) + ba)
    i = _sig(jnp.dot(xb, wx, preferred_element_type=F32) + bx)
    k = _lru_k(lam)
    la = k * r
    a = jnp.exp(la)
    s = jnp.sqrt(-jnp.tanh(la) * (a * a + 1.0))
    return r, i, k, a, s


SV_CONV, SV_BA, SV_BX, SV_LAM = 0, 4, 6, 8


def _pool_fwd(z, pw, scale, seq, d, tc):
    n_g = pw.shape[0]
    pg = d // n_g
    n_chunks = seq // tc

    def kern(z_ref, pw_ref, sc_ref, y_ref):
        g, c = pl.program_id(0), pl.program_id(1)
        uext = _ext(z_ref, c, n_chunks, tc, seq)
        dd = _pool_d(uext, g, c, tc, seq)
        q = jnp.dot(dd.astype(BF16), pw_ref[...], preferred_element_type=F32)
        y_ref[...] = (q * sc_ref[...]).astype(BF16)

    (y,) = _pcall(
        "pool_fwd", kern, (n_g, n_chunks),
        [pl.BlockSpec((seq, pg), lambda g, c: (0, g)),
         pl.BlockSpec((None, pg, pg), lambda g, c: (g, 0, 0)),
         pl.BlockSpec((1, pg), lambda g, c: (0, g))],
        [pl.BlockSpec((tc, pg), lambda g, c: (c, g))],
        [jax.ShapeDtypeStruct((seq, d), BF16)], [z, pw, scale])
    return y


def _lru_fwd(z, sv, conv_b, wa, wx, seq, d, tc):
    n_h, blk = wa.shape[1], wa.shape[2]
    n_chunks = seq // tc
    lru_off = d // blk

    def kern(z_ref, sv_ref, cb_ref, wa_ref, wx_ref, xc_ref, h_ref, a_s, b_s, carry):
        c = pl.program_id(1)
        uext = _ext(z_ref, c, n_chunks, tc, seq)
        xc = cb_ref[...]
        for k in range(4):
            xc = xc + _shifted(uext, k - 2, tc) * sv_ref[SV_CONV + k:SV_CONV + k + 1, :]
        xc_ref[...] = xc
        _, i, _, a, s = _lru_gates(xc, wa_ref[...], wx_ref[...], sv_ref[SV_BA:SV_BA + 1, :],
                                   sv_ref[SV_BX:SV_BX + 1, :], sv_ref[SV_LAM:SV_LAM + 1, :])
        a_s[...] = a
        b_s[...] = s * (i * xc)

        @pl.when(c == 0)
        def _():
            carry[...] = jnp.zeros_like(carry)

        _scan_tiles(a_s, b_s, h_ref, carry, tc // SUBLANES, False)

    col = lambda h, c: (c, h)
    return _pcall(
        "lru_fwd", kern, (n_h, n_chunks),
        [pl.BlockSpec((seq, blk), lambda h, c: (0, lru_off + h)),
         pl.BlockSpec((16, blk), lambda h, c: (0, h)),
         pl.BlockSpec((1, blk), lambda h, c: (0, h)),
         pl.BlockSpec((None, None, blk, blk), lambda h, c: (0, h, 0, 0)),
         pl.BlockSpec((None, None, blk, blk), lambda h, c: (0, h, 0, 0))],
        [pl.BlockSpec((tc, blk), col), pl.BlockSpec((tc, blk), col)],
        [jax.ShapeDtypeStruct((seq, d), F32), jax.ShapeDtypeStruct((seq, d), F32)],
        [z, sv, conv_b, wa, wx],
        scratch=[pltpu.VMEM((tc, blk), F32), pltpu.VMEM((tc, blk), F32), pltpu.VMEM((SUBLANES, blk), F32)])


def _lru_rev(z, sv, wa, wx, xc, h_f, seq, d, tc):
    n_h, blk = wa.shape[1], wa.shape[2]
    n_chunks = seq // tc
    gate_off = 2 * d // blk

    def kern(z_ref, sv_ref, wa_ref, wx_ref, xc_ref, hf_ref, hb_ref, y_ref, a_s, b_s, carry):
        c = pl.program_id(1)
        xc = xc_ref[...]
        _, i, _, a, s = _lru_gates(xc, wa_ref[...], wx_ref[...], sv_ref[SV_BA + 1:SV_BA + 2, :],
                                   sv_ref[SV_BX + 1:SV_BX + 2, :], sv_ref[SV_LAM + 1:SV_LAM + 2, :])
        a_s[...] = a
        b_s[...] = s * (i * xc)

        @pl.when(c == 0)
        def _():
            carry[...] = jnp.zeros_like(carry)

        _scan_tiles(a_s, b_s, hb_ref, carry, tc // SUBLANES, True)
        y_ref[...] = ((hf_ref[...] + hb_ref[...]) * _gelu(z_ref[...])).astype(BF16)

    rev = lambda h, c: (n_chunks - 1 - c, h)
    return _pcall(
        "lru_rev", kern, (n_h, n_chunks),
        [pl.BlockSpec((tc, blk), lambda h, c: (n_chunks - 1 - c, gate_off + h)),
         pl.BlockSpec((16, blk), lambda h, c: (0, h)),
         pl.BlockSpec((None, None, blk, blk), lambda h, c: (1, h, 0, 0)),
         pl.BlockSpec((None, None, blk, blk), lambda h, c: (1, h, 0, 0)),
         pl.BlockSpec((tc, blk), rev), pl.BlockSpec((tc, blk), rev)],
        [pl.BlockSpec((tc, blk), rev), pl.BlockSpec((tc, blk), rev)],
        [jax.ShapeDtypeStruct((seq, d), F32), jax.ShapeDtypeStruct((seq, d), BF16)],
        [z, sv, wa, wx, xc, h_f],
        scratch=[pltpu.VMEM((tc, blk), F32), pltpu.VMEM((tc, blk), F32), pltpu.VMEM((SUBLANES, blk), F32)])


def _merge(y_pool, w_pu, y_lru, w_lu, z, seq, d, tm, tn):
    n_n = d // tn

    def kern(yp_ref, wp_ref, yl_ref, wl_ref, la_ref, lb_ref, m_ref, pa_ref, pb_ref):
        pa = jnp.dot(yp_ref[...], wp_ref[...], preferred_element_type=F32)
        pb = jnp.dot(yl_ref[...], wl_ref[...], preferred_element_type=F32)
        m_ref[...] = (_sig(la_ref[...]) * pa + _sig(lb_ref[...]) * pb).astype(BF16)
        pa_ref[...] = pa.astype(BF16)
        pb_ref[...] = pb.astype(BF16)

    row = pl.BlockSpec((tm, d), lambda i, j: (i, 0))
    wcol = pl.BlockSpec((d, tn), lambda i, j: (0, j))
    out = pl.BlockSpec((tm, tn), lambda i, j: (i, j))
    sh = jax.ShapeDtypeStruct((seq, d), BF16)
    return _pcall(
        "merge", kern, (seq // tm, n_n),
        [row, wcol, row, wcol,
         pl.BlockSpec((tm, tn), lambda i, j: (i, 3 * n_n + j)),
         pl.BlockSpec((tm, tn), lambda i, j: (i, 4 * n_n + j))],
        [out, out, out], [sh, sh, sh], [y_pool, w_pu, y_lru, w_lu, z, z])


def _ln_fwd(s, g, b):
    mu = jnp.mean(s, axis=-1, keepdims=True)
    xc = s - mu
    var = jnp.mean(xc * xc, axis=-1, keepdims=True)
    rstd = lax.rsqrt(var + LN_EPS)
    xhat = xc * rstd
    return xhat, rstd, xhat * g + b


def _ln_bwd(dy, xhat, rstd, g):
    dyg = dy * g
    m1 = jnp.mean(dyg, axis=-1, keepdims=True)
    m2 = jnp.mean(dyg * xhat, axis=-1, keepdims=True)
    return rstd * (dyg - m1 - xhat * m2)


def _pool_bwd(z, dy_pool, pw, scale, dz, seq, d, tc):
    n_g = pw.shape[0]
    pg = d // n_g
    n_chunks = seq // tc

    def kern(z_ref, dy_ref, pw_ref, sc_ref, dz_in, dz_ref, dpw_ref, dsc_ref):
        del dz_in
        g, c = pl.program_id(0), pl.program_id(1)
        hw = jnp.left_shift(1, g)
        uext = _ext(z_ref, c, n_chunks, tc, seq)
        dd = _pool_d(uext, g, c, tc, seq).astype(BF16)
        pwv = pw_ref[...]
        q = jnp.dot(dd, pwv, preferred_element_type=F32)
        dyext = _ext(dy_ref, c, n_chunks, tc, seq)

        @pl.when(c == 0)
        def _():
            dsc_ref[...] = jnp.zeros_like(dsc_ref)
            dpw_ref[...] = jnp.zeros_like(dpw_ref)

        dsc_ref[0:1, :] += _colsum(dyext[HALO:HALO + tc] * q)
        dqext = (dyext * sc_ref[...]).astype(BF16)
        dpw_ref[...] += _dot("tn", dd, dqext[HALO:HALO + tc])
        ddext = _dot("nt", dqext, pwv)
        text = c * tc - HALO + lax.broadcasted_iota(jnp.int32, (tc + 2 * HALO, 1), 0)
        v = ddext / jnp.maximum(_win_cnt(text, hw, seq), 1.0)
        dz_ref[...] = (_win_sum(v, g, 1, tc) - ddext[HALO:HALO + tc]).astype(BF16)

    return _pcall(
        "pool_bwd", kern, (n_g, n_chunks),
        [pl.BlockSpec((seq, pg), lambda g, c: (0, g)),
         pl.BlockSpec((seq, pg), lambda g, c: (0, g)),
         pl.BlockSpec((None, pg, pg), lambda g, c: (g, 0, 0)),
         pl.BlockSpec((1, pg), lambda g, c: (0, g)),
         pl.BlockSpec(memory_space=pl.ANY)],
        [pl.BlockSpec((None, tc, pg), lambda g, c: (2, c, g)),
         pl.BlockSpec((None, pg, pg), lambda g, c: (g, 0, 0)),
         pl.BlockSpec((SUBLANES, pg), lambda g, c: (0, g))],
        [jax.ShapeDtypeStruct(dz.shape, dz.dtype),
         jax.ShapeDtypeStruct((n_g, pg, pg), F32),
         jax.ShapeDtypeStruct((SUBLANES, d), F32)],
        [z, dy_pool, pw, scale, dz], aliases={4: 0})


def _lru_bwd(direction, xc, dh, h_dir, sv, wa, wx, dxc_prev, seq, d, tc):
    reverse = direction == 1
    n_h, blk = wa.shape[1], wa.shape[2]
    n_chunks = seq // tc
    has_prev = dxc_prev is not None

    def kern(*refs):
        xc_ref, dh_ref, h_ref, sv_ref, wa_ref, wx_ref = refs[:6]
        p = 6
        prev_ref = None
        if has_prev:
            prev_ref = refs[p]
            p += 1
        dxc_ref, dwa_ref, dwx_ref, sm_ref, at_s, g_s, carry, acarry = refs[p:p + 8]
        c = pl.program_id(1)
        cr = c if reverse else n_chunks - 1 - c
        c0 = pl.multiple_of(cr * tc, tc)

        @pl.when(c == 0)
        def _():
            carry[...] = jnp.zeros_like(carry)
            acarry[...] = jnp.zeros_like(acarry)
            dwa_ref[...] = jnp.zeros_like(dwa_ref)
            dwx_ref[...] = jnp.zeros_like(dwx_ref)
            sm_ref[...] = jnp.zeros_like(sm_ref)

        xc = xc_ref[...]
        wav, wxv = wa_ref[...], wx_ref[...]
        lam = sv_ref[SV_LAM + direction:SV_LAM + direction + 1, :]
        r, i, k, a, s = _lru_gates(xc, wav, wxv, sv_ref[SV_BA + direction:SV_BA + direction + 1, :],
                                   sv_ref[SV_BX + direction:SV_BX + direction + 1, :], lam)
        rowi = lax.broadcasted_iota(jnp.int32, (tc, blk), 0)
        hbody = h_ref[pl.ds(c0, tc), :]
        if not reverse:
            p0 = pl.multiple_of(jnp.maximum(c0 - SUBLANES, 0), SUBLANES)
            edge = jnp.where(cr > 0, h_ref[pl.ds(p0, SUBLANES), :][SUBLANES - 1:SUBLANES, :], 0.0)
            hprev = jnp.where(rowi == 0, edge, pltpu.roll(hbody, 1, 0))
            at = jnp.where(rowi == tc - 1, acarry[0:1, :], pltpu.roll(a, tc - 1, 0))
        else:
            n0 = pl.multiple_of(jnp.minimum(c0 + tc, seq - SUBLANES), SUBLANES)
            edge = jnp.where(cr < n_chunks - 1, h_ref[pl.ds(n0, SUBLANES), :][0:1, :], 0.0)
            hprev = jnp.where(rowi == tc - 1, edge, pltpu.roll(hbody, tc - 1, 0))
            at = jnp.where(rowi == 0, acarry[0:1, :], pltpu.roll(a, 1, 0))
        at_s[...] = at
        _scan_tiles(at_s, dh_ref, g_s, carry, tc // SUBLANES, not reverse)
        acarry[0:1, :] = a[tc - 1:tc, :] if reverse else a[0:1, :]

        gt = g_s[...]
        da = gt * hprev
        di = gt * s * xc
        dxc = gt * s * i
        ds = gt * (i * xc)
        dl = da * a - ds * (a * a) / s
        dpr = (dl * k) * r * (1.0 - r)
        dpi = di * i * (1.0 - i)
        sm_ref[0:1, :] += _colsum(dpr)
        sm_ref[1:2, :] += _colsum(dpi)
        sm_ref[2:3, :] += _colsum(dl * r) * (LRU_C * _sig(-lam))
        xb, dprb, dpib = xc.astype(BF16), dpr.astype(BF16), dpi.astype(BF16)
        dwa_ref[...] += _dot("tn", xb, dprb)
        dwx_ref[...] += _dot("tn", xb, dpib)
        dxc = dxc + _dot("nt", dprb, wav) + _dot("nt", dpib, wxv)
        if has_prev:
            dxc = dxc + prev_ref[...]
        dxc_ref[...] = dxc

    if reverse:
        chunk = lambda h, c: (c, h)
    else:
        chunk = lambda h, c: (n_chunks - 1 - c, h)
    wspec = pl.BlockSpec((None, None, blk, blk), lambda h, c: (direction, h, 0, 0))
    ins = [xc, dh, h_dir, sv, wa, wx] + ([dxc_prev] if has_prev else [])
    in_specs = [pl.BlockSpec((tc, blk), chunk), pl.BlockSpec((tc, blk), chunk),
                pl.BlockSpec((seq, blk), lambda h, c: (0, h)),
                pl.BlockSpec((16, blk), lambda h, c: (0, h)), wspec, wspec]
    if has_prev:
        in_specs.append(pl.BlockSpec((tc, blk), chunk))
    return _pcall(
        "lru_bwd_%d" % direction, kern, (n_h, n_chunks), in_specs,
        [pl.BlockSpec((tc, blk), chunk),
         pl.BlockSpec((None, blk, blk), lambda h, c: (h, 0, 0)),
         pl.BlockSpec((None, blk, blk), lambda h, c: (h, 0, 0)),
         pl.BlockSpec((SUBLANES, blk), lambda h, c: (0, h))],
        [jax.ShapeDtypeStruct((seq, d), F32),
         jax.ShapeDtypeStruct((n_h, blk, blk), F32),
         jax.ShapeDtypeStruct((n_h, blk, blk), F32),
         jax.ShapeDtypeStruct((SUBLANES, d), F32)],
        ins,
        scratch=[pltpu.VMEM((tc, blk), F32), pltpu.VMEM((tc, blk), F32),
                 pltpu.VMEM((SUBLANES, blk), F32), pltpu.VMEM((SUBLANES, blk), F32)])


def _conv_bwd(z, dxc, sv, dz, seq, d, tc, tcol):
    n_chunks = seq // tc
    lru_off = d // tcol

    def kern(z_ref, dx_ref, sv_ref, dz_in, dz_ref, sm_ref):
        del dz_in
        c = pl.program_id(1)
        uext = _ext(z_ref, c, n_chunks, tc, seq)
        dext = _ext(dx_ref, c, n_chunks, tc, seq)
        dbody = dext[HALO:HALO + tc]

        @pl.when(c == 0)
        def _():
            sm_ref[...] = jnp.zeros_like(sm_ref)

        du = jnp.zeros_like(dbody)
        for k in range(4):
            du = du + _shifted(dext, 2 - k, tc) * sv_ref[SV_CONV + k:SV_CONV + k + 1, :]
            sm_ref[k:k + 1, :] += _colsum(dbody * _shifted(uext, k - 2, tc))
        sm_ref[4:5, :] += _colsum(dbody)
        dz_ref[...] = du.astype(BF16)

    return _pcall(
        "conv_bwd", kern, (d // tcol, n_chunks),
        [pl.BlockSpec((seq, tcol), lambda j, c: (0, lru_off + j)),
         pl.BlockSpec((seq, tcol), lambda j, c: (0, j)),
         pl.BlockSpec((16, tcol), lambda j, c: (0, j)),
         pl.BlockSpec(memory_space=pl.ANY)],
        [pl.BlockSpec((None, tc, tcol), lambda j, c: (3, c, j)),
         pl.BlockSpec((SUBLANES, tcol), lambda j, c: (0, j))],
        [jax.ShapeDtypeStruct(dz.shape, dz.dtype), jax.ShapeDtypeStruct((SUBLANES, d), F32)],
        [z, dxc, sv, dz], aliases={3: 0})


BIG = (("w_in", "col"), ("pool_w", "row"), ("lru_wa", "row"), ("lru_wx", "row"), ("w_pool_up", "row"),
       ("w_lru_up", "row"), ("w_out", "row"), ("w_ff1", "col"), ("w_ff2", "row"))


def _shard_view(w, fam):
    if fam == "col":
        return w.reshape(w.shape[-2:])
    return w.reshape((-1,) + w.shape[-2:])


def _full_shape(sv_shape, fam):
    if fam == "col":
        return (sv_shape[0], N_DEV * sv_shape[1])
    return (sv_shape[0], N_DEV * sv_shape[1], sv_shape[2])


def _slot(ref, fam, p, size):
    if fam == "lead":
        return ref.at[p]
    start = pl.multiple_of(p * size, size)
    if fam == "col":
        return ref.at[:, pl.ds(start, size)]
    return ref.at[:, pl.ds(start, size), :]


def _coords():
    return lax.axis_index("x"), lax.axis_index("y"), lax.axis_index("c")


def _ag_job(shards, fams):
    n = len(shards)
    fulls = []
    for s, fam in zip(shards, fams):
        if fam == "lead":
            fulls.append(jax.ShapeDtypeStruct((N_DEV,) + s.shape, s.dtype))
        else:
            fulls.append(jax.ShapeDtypeStruct(_full_shape(s.shape, fam), s.dtype))
    sizes = [1 if fam == "lead" else s.shape[1] for s, fam in zip(shards, fams)]

    def ctx(ins, outs, sems):
        send, recv, loc = sems
        x, y, c = _coords()
        chips = [(1 - x, y), (x, 1 - y), (1 - x, 1 - y)]

        def copy(a, k, owner, to, src=None):
            dst = _slot(outs[a], fams[a], owner, sizes[a])
            return pltpu.make_async_remote_copy(
                src_ref=dst if src is None else src, dst_ref=dst,
                send_sem=send.at[a, k], recv_sem=recv.at[a, k], device_id=to, device_id_type=MESH)

        def local(a):
            return pltpu.make_async_copy(ins[a], _slot(outs[a], fams[a], 4 * x + 2 * y + c, sizes[a]), loc.at[a])

        return x, y, c, chips, copy, local

    def start(ins, outs, sems):
        x, y, c, chips, copy, local = ctx(ins, outs, sems)
        me = 4 * x + 2 * y + c
        for a in range(n):
            local(a).start()
            copy(a, 0, me, (x, y, 1 - c), ins[a]).start()
            for j, (cx, cy) in enumerate(chips):
                copy(a, 1 + j, me, (cx, cy, c), ins[a]).start()

    def mid(ins, outs, sems):
        x, y, c, chips, copy, _ = ctx(ins, outs, sems)
        for a in range(n):
            for j, (cx, cy) in enumerate(chips):
                owner = 4 * cx + 2 * cy + c
                copy(a, 1 + j, owner, (x, y, c)).wait_recv()
                copy(a, 4 + j, owner, (x, y, 1 - c)).start()

    def finish(ins, outs, sems):
        x, y, c, chips, copy, local = ctx(ins, outs, sems)
        me = 4 * x + 2 * y + c
        for a in range(n):
            copy(a, 0, 4 * x + 2 * y + (1 - c), (x, y, c)).wait_recv()
            for j, (cx, cy) in enumerate(chips):
                copy(a, 4 + j, 4 * cx + 2 * cy + (1 - c), (x, y, c)).wait_recv()
            copy(a, 0, me, (x, y, 1 - c), ins[a]).wait_send()
            for j, (cx, cy) in enumerate(chips):
                copy(a, 1 + j, me, (cx, cy, c), ins[a]).wait_send()
                copy(a, 4 + j, 4 * cx + 2 * cy + c, (x, y, 1 - c)).wait_send()
            local(a).wait()

    sems = [pltpu.SemaphoreType.DMA((n, 7)), pltpu.SemaphoreType.DMA((n, 7)), pltpu.SemaphoreType.DMA((n,))]
    return _Job(shards, fulls, sems, start, finish, mid)


def _rs_sibling_job(fulls, fams, sizes):
    n = len(fulls)
    outs = []
    for f, fam, sz in zip(fulls, fams, sizes):
        if fam == "col":
            outs.append(jax.ShapeDtypeStruct((4, f.shape[0], sz), f.dtype))
        else:
            outs.append(jax.ShapeDtypeStruct((4, f.shape[0], sz, f.shape[2]), f.dtype))

    def copies(ins, rcv, sems):
        send, recv = sems
        x, y, c = _coords()
        return [pltpu.make_async_remote_copy(
            src_ref=_slot(ins[a], fams[a], 2 * q + (1 - c), sizes[a]), dst_ref=rcv[a].at[q],
            send_sem=send.at[a, q], recv_sem=recv.at[a, q], device_id=(x, y, 1 - c), device_id_type=MESH)
            for a in range(n) for q in range(4)]

    def start(ins, rcv, sems):
        for cp in copies(ins, rcv, sems):
            cp.start()

    def finish(ins, rcv, sems):
        for cp in copies(ins, rcv, sems):
            cp.wait()

    return _Job(fulls, outs, [pltpu.SemaphoreType.DMA((n, 4)), pltpu.SemaphoreType.DMA((n, 4))], start, finish)


def _rs_chips_job(parts):
    n = len(parts)
    outs = [jax.ShapeDtypeStruct((3,) + p.shape[1:], p.dtype) for p in parts]

    def copies(ins, rcv, sems):
        send, recv = sems
        x, y, c = _coords()
        cps = []
        for a in range(n):
            for r in (1, 2, 3):
                tx, ty = (1 - x) if r & 2 else x, (1 - y) if r & 1 else y
                cps.append(pltpu.make_async_remote_copy(
                    src_ref=ins[a].at[2 * tx + ty], dst_ref=rcv[a].at[r - 1],
                    send_sem=send.at[a, r - 1], recv_sem=recv.at[a, r - 1],
                    device_id=(tx, ty, c), device_id_type=MESH))
        return cps

    def start(ins, rcv, sems):
        for cp in copies(ins, rcv, sems):
            cp.start()

    def finish(ins, rcv, sems):
        for cp in copies(ins, rcv, sems):
            cp.wait()

    return _Job(parts, outs, [pltpu.SemaphoreType.DMA((n, 3)), pltpu.SemaphoreType.DMA((n, 3))], start, finish)


def _tile_rows(rows, cols):
    tr = rows
    while tr * cols > (1 << 18) and tr % (2 * SUBLANES) == 0:
        tr //= 2
    return tr


def _rs_add(name, full, recv_a, fam, size, cidx):
    if fam == "col":
        rows = full.shape[0]
        tr = _tile_rows(rows, size)
        grid = (4, rows // tr)
        f_spec = pl.BlockSpec((tr, size), lambda q, i, cr: (i, 2 * q + cr[0]))
        s_spec = pl.BlockSpec((None, tr, size), lambda q, i, cr: (q, i, 0))
    else:
        nb, cols = full.shape[0], full.shape[2]
        tr = _tile_rows(nb * size, cols) // nb if nb > 1 else _tile_rows(size, cols)
        nt = size // tr
        grid = (4, nt)
        f_spec = pl.BlockSpec((nb, tr, cols), lambda q, i, cr: (0, (2 * q + cr[0]) * nt + i, 0))
        s_spec = pl.BlockSpec((None, nb, tr, cols), lambda q, i, cr: (q, 0, i, 0))

    token = [t for t in _ORDER if t is not full]

    def kern(c_ref, f_ref, r_ref, *rest):
        del c_ref
        rest[-1][...] = (f_ref[...] + r_ref[...]).astype(BF16)

    out = pl.pallas_call(
        kern, name=name,
        grid_spec=pltpu.PrefetchScalarGridSpec(num_scalar_prefetch=1, grid=grid,
                                               in_specs=[f_spec, s_spec] + [HBM_SPEC] * len(token), out_specs=s_spec),
        out_shape=jax.ShapeDtypeStruct(recv_a.shape, BF16),
        compiler_params=_cparams(("arbitrary", "arbitrary"), 32),
    )(cidx, full, recv_a, *token)
    _ORDER[:] = [out]
    return out


def _adam(w, g, m, v):
    m2 = ADAM_B1 * m + (1.0 - ADAM_B1) * g
    v2 = ADAM_B2 * v + (1.0 - ADAM_B2) * (g * g)
    m_hat = m2 / (1.0 - ADAM_B1 ** ADAM_STEP)
    v_hat = v2 / (1.0 - ADAM_B2 ** ADAM_STEP)
    delta = -ADAM_LR * (m_hat / (jnp.sqrt(v_hat) + ADAM_EPS) + ADAM_WD * w)
    return delta, m2, v2


def _rs_final_adam(name, parts, recv_b, w, m, v, fam, qidx):
    shp = w.shape
    pieces = parts if isinstance(parts, (list, tuple)) else [parts]
    recvs = recv_b if isinstance(recv_b, (list, tuple)) else [recv_b]
    n_p = len(pieces)
    first_blk = [0] * n_p
    if fam == "col":
        rows, cols = shp
        tr = _tile_rows(min(p.shape[1] for p in pieces), cols)
        per = [p.shape[1] // tr for p in pieces]
        for h in range(1, n_p):
            first_blk[h] = first_blk[h - 1] + per[h - 1]
        grid = (rows // tr,)
        w_spec = pl.BlockSpec((tr, cols), lambda i, qr: (i, 0))

        def piece_row(i, h):
            return jnp.clip(i - first_blk[h], 0, per[h] - 1)

        p_specs = [pl.BlockSpec((None, tr, cols), lambda i, qr, h=h: (qr[0], piece_row(i, h), 0))
                   for h in range(n_p)]
        r_specs = [pl.BlockSpec((3, tr, cols), lambda i, qr, h=h: (0, piece_row(i, h), 0)) for h in range(n_p)]
    else:
        assert n_p == 1
        nb, rows, cols = shp
        tr = _tile_rows(rows, cols)
        nt = rows // tr
        grid = (nb * nt,)
        w_spec = pl.BlockSpec((None, tr, cols), lambda i, qr: (i // nt, i % nt, 0))
        p_specs = [pl.BlockSpec((None, None, tr, cols), lambda i, qr: (qr[0], i // nt, i % nt, 0))]
        r_specs = [pl.BlockSpec((3, None, tr, cols), lambda i, qr: (0, i // nt, i % nt, 0))]

    token = list(_ORDER)

    def kern(*refs):
        p_refs, r_refs = refs[1:1 + n_p], refs[1 + n_p:1 + 2 * n_p]
        w_ref, m_ref, v_ref = refs[1 + 2 * n_p:4 + 2 * n_p]
        g_out, d_out, m_out, v_out = refs[4 + 2 * n_p + len(token):]

        def total(h):
            p_ref, r_ref = p_refs[h], r_refs[h]
            return ((p_ref[...].astype(F32) + r_ref[0].astype(F32)) + r_ref[1].astype(F32)) + r_ref[2].astype(F32)

        g = total(0)
        for h in range(1, n_p):
            g = jnp.where(pl.program_id(0) >= first_blk[h], total(h), g)
        delta, m2, v2 = _adam(w_ref[...], g, m_ref[...], v_ref[...])
        g_out[...] = g
        d_out[...] = delta
        m_out[...] = m2
        v_out[...] = v2

    sh = jax.ShapeDtypeStruct(shp, F32)
    res = pl.pallas_call(
        kern, name=name,
        grid_spec=pltpu.PrefetchScalarGridSpec(
            num_scalar_prefetch=1, grid=grid,
            in_specs=p_specs + r_specs + [w_spec, w_spec, w_spec] + [HBM_SPEC] * len(token),
            out_specs=[w_spec] * 4),
        out_shape=[sh] * 4,
        compiler_params=_cparams(("arbitrary",), 32),
    )(qidx, *pieces, *recvs, w, m, v, *token)
    _ORDER[:] = [res[1]]
    return res


def _sum8(name, parts):
    def kern(p_ref, o_ref):
        acc = p_ref[0]
        for p in range(1, N_DEV):
            acc = acc + p_ref[p]
        o_ref[...] = acc

    return pl.pallas_call(
        kern, name=name, out_shape=jax.ShapeDtypeStruct(parts.shape[1:], F32),
        compiler_params=pltpu.CompilerParams(vmem_limit_bytes=32 << 20),
    )(parts)


def _adam_small(name, w, g, m, v):
    def kern(w_ref, g_ref, m_ref, v_ref, d_out, m_out, v_out):
        delta, m2, v2 = _adam(w_ref[...], g_ref[...], m_ref[...], v_ref[...])
        d_out[...] = delta
        m_out[...] = m2
        v_out[...] = v2

    sh = jax.ShapeDtypeStruct(w.shape, F32)
    return pl.pallas_call(kern, name=name, out_shape=[sh] * 3)(w, g, m, v)


class _NoComm:
    def __init__(self):
        self.grads = {}

    def settled_before(self, kernel_name):
        return []

    def after(self, kernel_name):
        pass

    def grad(self, name, g):
        self.grads[name] = g


def _local_step(x, target, wts, vec, comm=None):
    comm = comm or _NoComm()
    _ORDER[:] = []
    seq, d = x.shape
    sv = vec["sv"]
    ff = vec["b_ff1"].shape[1]
    n_in = 5 * d

    def issue(name, call):
        _ORDER.extend(comm.settled_before(name))
        res = call()
        comm.after(name)
        return res

    tc = min(512, seq)
    t1k, t512, t256 = min(1024, seq), min(512, seq), min(256, seq)
    n512 = min(512, d)
    tkd = d

    x_bf = x.astype(BF16)
    full = lambda i, j, k: (0, 0)

    def epi_store(acc, i, j, ex, outs):
        outs[0][...] = acc

    n_pass = W_IN_PASSES
    piece = n_in // (N_DEV * n_pass)
    tz = min(2048, seq)
    z = None
    for k in range(n_pass):
        w_piece = wts["w_in_piece_%d" % k]
        prev = [] if z is None else [(z, (None, None))]
        (z,) = issue("z_proj_%d" % k, lambda: _mm(
            "z_proj_%d" % k, "nn", (seq // tz, N_DEV, 1),
            x_bf, ((tz, d), lambda i, j, kk: (i, 0)), w_piece, ((d, piece), lambda i, j, kk: (0, j)),
            prev, [jax.ShapeDtypeStruct((seq, n_in), F32)],
            [((tz, piece), lambda i, j, kk, k=k: (i, n_pass * j + k))], epi_store,
            aliases={2: 0} if prev else None))

    pool_w, wa, wx = wts["pool_w"], wts["lru_wa"], wts["lru_wx"]
    blk = wa.shape[2]
    y_pool = issue("pool_fwd", lambda: _pool_fwd(z, pool_w, vec["pool_scale"], seq, d, tc))
    xc, h_f = issue("lru_fwd", lambda: _lru_fwd(z, sv, vec["conv_b"], wa, wx, seq, d, tc))
    h_b, y_lru = issue("lru_rev", lambda: _lru_rev(z, sv, wa, wx, xc, h_f, seq, d, tc))
    w_pu, w_lu = wts["w_pool_up"], wts["w_lru_up"]
    m_bf, p_a, p_b = issue("merge", lambda: _merge(y_pool, w_pu, y_lru, w_lu, z, seq, d, t1k, n512))
    w_out = wts["w_out"]

    def epi_ln1(acc, i, j, ex, outs):
        x_ref, bo, g1, b1 = ex
        s1 = DN_ALPHA * x_ref[...] + (acc + bo[...])
        xhat, rstd, x1 = _ln_fwd(s1, g1[...], b1[...])
        outs[0][...] = xhat
        outs[1][...] = x1.astype(BF16)
        outs[2][...] = rstd

    rowd = lambda t: ((t, d), lambda i, j, k: (i, 0))
    vecd = ((1, d), full)
    xhat1, x1_bf, rstd1 = issue("out_ln1", lambda: _mm(
        "out_ln1", "nn", (seq // t256, 1, 1), m_bf, rowd(t256), w_out, ((d, d), full),
        [(x, rowd(t256)), (vec["b_out"], vecd), (vec["ln1_g"], vecd), (vec["ln1_b"], vecd)],
        [jax.ShapeDtypeStruct((seq, d), F32), jax.ShapeDtypeStruct((seq, d), BF16),
         jax.ShapeDtypeStruct((seq, 1), F32)],
        [rowd(t256), rowd(t256), ((t256, 1), lambda i, j, k: (i, 0))], epi_ln1))
    w1 = wts["w_ff1"]

    def epi_ff1(acc, i, j, ex, outs):
        r = jnp.maximum(acc + ex[0][...], 0.0)
        outs[0][...] = r.astype(BF16)
        outs[1][...] = (r * r).astype(BF16)

    nf = min(1024, ff)
    tile_f = ((t1k, nf), lambda i, j, k: (i, j))
    relu_h, hdn = issue("ff1", lambda: _mm(
        "ff1", "nn", (seq // t1k, ff // nf, 1), x1_bf, rowd(t1k), w1, ((d, nf), lambda i, j, k: (0, j)),
        [(vec["b_ff1"], ((1, nf), lambda i, j, k: (0, j)))],
        [jax.ShapeDtypeStruct((seq, ff), BF16)] * 2, [tile_f, tile_f], epi_ff1))
    w2 = wts["w_ff2"]

    def epi_ln2(acc, i, j, ex, outs):
        xh1, tgt, g1, b1, bf2, g2, b2 = ex
        ds_ref, dsb_ref, sm_ref, loss_ref = outs
        x1 = xh1[...] * g1[...] + b1[...]
        s2 = DN_ALPHA * x1 + (acc + bf2[...])
        xhat, rstd, y = _ln_fwd(s2, g2[...], b2[...])
        e = y - tgt[...]
        part = 0.5 * jnp.sum(jnp.mean(e * e, axis=-1, keepdims=True))
        dy = e * (1.0 / d)
        ds2 = _ln_bwd(dy, xhat, rstd, g2[...])
        ds_ref[...] = ds2
        dsb_ref[...] = ds2.astype(BF16)
        sm_ref[0:1, :] += _colsum(dy * xhat)
        sm_ref[1:2, :] += _colsum(dy)
        sm_ref[2:3, :] += _colsum(ds2)
        loss_ref[...] += jnp.full(loss_ref.shape, part, F32)

    def zero_tail(n_tail):
        def init(i, j, outs):
            @pl.when(i == 0)
            def _():
                for o in outs[-n_tail:]:
                    o[...] = jnp.zeros_like(o)
        return init

    def rows_epilogue(name, lead, extras, out_shapes, out_specs, epi, n_tail):
        ne = len(extras)

        def kern(*refs):
            i = pl.program_id(0)
            outs = refs[1 + ne:]
            zero_tail(n_tail)(i, 0, outs)
            epi(refs[0][...], i, 0, refs[1:1 + ne], outs)

        def spec(s):
            bs, f = s
            return pl.BlockSpec(bs, lambda i, f=f: f(i, 0, 0))

        return _pcall(name, kern, (seq // t256,), [spec(rowd(t256))] + [spec(s) for _, s in extras],
                      [spec(s) for s in out_specs], out_shapes, [lead] + [e for e, _ in extras])

    tkw2, tw = min(2048, ff), min(1024, d)
    (ff_out,) = _mm("ff2", "nn", (seq // t1k, d // tw, ff // tkw2), hdn, ((t1k, tkw2), lambda i, j, k: (i, k)),
                    w2, ((tkw2, tw), lambda i, j, k: (k, j)), [], [jax.ShapeDtypeStruct((seq, d), F32)],
                    [((t1k, tw), lambda i, j, k: (i, j))], None)
    ds2, ds2_bf, sm_ln2, loss_blk = issue("ff2_ln2", lambda: rows_epilogue(
        "ln2_loss", ff_out,
        [(xhat1, rowd(t256)), (target, rowd(t256)), (vec["ln1_g"], vecd), (vec["ln1_b"], vecd),
         (vec["b_ff2"], vecd), (vec["ln2_g"], vecd), (vec["ln2_b"], vecd)],
        [jax.ShapeDtypeStruct((seq, d), F32), jax.ShapeDtypeStruct((seq, d), BF16),
         jax.ShapeDtypeStruct((SUBLANES, d), F32), jax.ShapeDtypeStruct((SUBLANES, 128), F32)],
        [rowd(t256), rowd(t256), ((SUBLANES, d), full), ((SUBLANES, 128), full)], epi_ln2, 2))

    tkw = min(2048, seq)

    def dw(name, wname, a, b, m_dim, n_dim, b_spec=None, row0=0):
        _ORDER.extend(comm.settled_before(name))
        tn = min(1024, n_dim)
        tm = next(t for t in (1024, 768, 512, 256, 128) if m_dim % t == 0 and row0 % t == 0)
        b_spec = b_spec or ((tkw, tn), lambda i, j, k: (k, j))
        i0 = row0 // tm
        (out,) = _mm(
            name, "tn", (m_dim // tm, n_dim // tn, seq // tkw),
            a, ((tkw, tm), lambda i, j, k: (k, i0 + i)), b, b_spec, [],
            [jax.ShapeDtypeStruct((m_dim, n_dim), F32)], [((tm, tn), lambda i, j, k: (i, j))],
            epi_store if seq == tkw else None)
        comm.grad(wname, out)
        comm.after(name)

    dw("dw_ff2", "w_ff2", hdn, ds2_bf, ff, d)

    def epi_dpre(acc, i, j, ex, outs):
        dpre = acc * (2.0 * ex[0][...].astype(F32))
        outs[0][...] = dpre.astype(BF16)

        @pl.when(i == 0)
        def _():
            outs[1][...] = jnp.zeros_like(outs[1])

        outs[1][0:1, :] += _colsum(dpre)

    dpre, sm_bff1 = issue("dhdn", lambda: _mm(
        "dhdn", "nt", (seq // t1k, ff // nf, 1), ds2_bf, rowd(t1k), w2, ((nf, d), lambda i, j, k: (j, 0)),
        [(relu_h, tile_f)],
        [jax.ShapeDtypeStruct((seq, ff), BF16), jax.ShapeDtypeStruct((SUBLANES, ff), F32)],
        [tile_f, ((SUBLANES, nf), lambda i, j, k: (0, j))], epi_dpre, order="ji"))

    dw("dw_ff1", "w_ff1", x1_bf, dpre, d, ff)

    def epi_ln1b(acc, i, j, ex, outs):
        ds2_ref, xh1, rs1, g1 = ex
        ds_ref, dsb_ref, sm_ref = outs
        dy1 = acc + DN_ALPHA * ds2_ref[...]
        xhat = xh1[...]
        ds1 = _ln_bwd(dy1, xhat, rs1[...], g1[...])
        ds_ref[...] = ds1
        dsb_ref[...] = ds1.astype(BF16)
        sm_ref[0:1, :] += _colsum(dy1 * xhat)
        sm_ref[1:2, :] += _colsum(dy1)
        sm_ref[2:3, :] += _colsum(ds1)

    (dx1,) = _mm("dx1", "nt", (seq // t1k, d // tw, ff // tkw2), dpre, ((t1k, tkw2), lambda i, j, k: (i, k)),
                 w1, ((tw, tkw2), lambda i, j, k: (j, k)), [], [jax.ShapeDtypeStruct((seq, d), F32)],
                 [((t1k, tw), lambda i, j, k: (i, j))], None)
    ds1, ds1_bf, sm_ln1 = issue("dx1_ln1", lambda: rows_epilogue(
        "ln1_bwd", dx1,
        [(ds2, rowd(t256)), (xhat1, rowd(t256)), (rstd1, ((t256, 1), lambda i, j, k: (i, 0))),
         (vec["ln1_g"], vecd)],
        [jax.ShapeDtypeStruct((seq, d), F32), jax.ShapeDtypeStruct((seq, d), BF16),
         jax.ShapeDtypeStruct((SUBLANES, d), F32)],
        [rowd(t256), rowd(t256), ((SUBLANES, d), full)], epi_ln1b, 1))

    dw("dw_out", "w_out", m_bf, ds1_bf, d, d)
    n_n = d // n512
    tile_d = ((t512, n512), lambda i, j, k: (i, j))

    def epi_dm(acc, i, j, ex, outs):
        la, lb, pa, pb = ex
        ga, gb = _sig(la[...]), _sig(lb[...])
        outs[0][...] = (acc * ga).astype(BF16)
        outs[1][...] = (acc * gb).astype(BF16)
        outs[2][0] = (acc * pa[...].astype(F32) * ga * (1.0 - ga)).astype(BF16)
        outs[2][1] = (acc * pb[...].astype(F32) * gb * (1.0 - gb)).astype(BF16)

    dp_a, dp_b, dz = _mm(
        "dm", "nt", (seq // t512, n_n, 1), ds1_bf, rowd(t512), w_out, ((n512, d), lambda i, j, k: (j, 0)),
        [(z, ((t512, n512), lambda i, j, k: (i, 3 * n_n + j))),
         (z, ((t512, n512), lambda i, j, k: (i, 4 * n_n + j))), (p_a, tile_d), (p_b, tile_d)],
        [jax.ShapeDtypeStruct((seq, d), BF16), jax.ShapeDtypeStruct((seq, d), BF16),
         jax.ShapeDtypeStruct((5, seq, d), BF16)],
        [tile_d, tile_d, ((2, t512, n512), lambda i, j, k: (0, i, j))], epi_dm)

    dw("dw_pool_up", "w_pool_up", y_pool, dp_a, d, d)
    dw("dw_lru_up", "w_lru_up", y_lru, dp_b, d, d)

    def epi_bf(acc, i, j, ex, outs):
        outs[0][...] = acc.astype(BF16)

    (dy_pool,) = issue("dy_pool", lambda: _mm(
        "dy_pool", "nt", (seq // t512, n_n, 1), dp_a, rowd(t512), w_pu, ((n512, d), lambda i, j, k: (j, 0)), [],
        [jax.ShapeDtypeStruct((seq, d), BF16)], [tile_d], epi_bf))

    def epi_dylru(acc, i, j, ex, outs):
        hf, hb, ug, _ = ex
        u = ug[...]
        outs[0][...] = acc * _gelu(u)
        outs[1][...] = (acc * (hf[...] + hb[...]) * _gelu_grad(u)).astype(BF16)

    dz_in = dz
    dh, dz = issue("dy_lru", lambda: _mm(
        "dy_lru", "nt", (seq // t512, n_n, 1), dp_b, rowd(t512), w_lu, ((n512, d), lambda i, j, k: (j, 0)),
        [(h_f, tile_d), (h_b, tile_d), (z, ((t512, n512), lambda i, j, k: (i, 2 * n_n + j))),
         (dz_in, (None, None))],
        [jax.ShapeDtypeStruct((seq, d), F32), jax.ShapeDtypeStruct(dz_in.shape, BF16)],
        [tile_d, ((None, t512, n512), lambda i, j, k: (4, i, j))], epi_dylru, aliases={5: 1}))

    dz, g_pw, sm_pool = _pool_bwd(z, dy_pool, pool_w, vec["pool_scale"], dz, seq, d, tc)
    comm.grad("pool_w", g_pw)
    dxc0, g_wa0, g_wx0, sm_l0 = issue("lru_bwd_0", lambda: _lru_bwd(
        0, xc, dh, h_f, sv, wa, wx, None, seq, d, tc))
    dxc, g_wa1, g_wx1, sm_l1 = issue("lru_bwd_1", lambda: _lru_bwd(
        1, xc, dh, h_b, sv, wa, wx, dxc0, seq, d, tc))
    comm.grad("lru_wa", jnp.concatenate([g_wa0, g_wa1], axis=0))
    comm.grad("lru_wx", jnp.concatenate([g_wx0, g_wx1], axis=0))
    dz, sm_conv = _conv_bwd(z, dxc, sv, dz, seq, d, tc, blk)

    tnw = min(1024, d)
    per_seg = d // tnw
    seg_spec = ((None, tkw, tnw), lambda i, j, k: ((j // per_seg + 2) % 5, k, j % per_seg))
    lo_rows = 3 * d // 4
    dw("dw_in_lo", "w_in_lo", x_bf, dz, lo_rows, n_in, b_spec=seg_spec)
    dw("dw_in_hi", "w_in_hi", x_bf, dz, d - lo_rows, n_in, b_spec=seg_spec, row0=lo_rows)

    nk = d // tkd

    def epi_dx(acc, i, j, ex, outs):
        outs[0][...] = acc + DN_ALPHA * ex[0][...]

    (grad_x,) = issue("dx", lambda: _mm(
        "dx", "nt", (seq // t512, 1, n_in // tkd), dz,
        ((None, t512, tkd), lambda i, j, k: ((k // nk + 2) % 5, i, k % nk)),
        wts["w_in"], ((d, tkd), lambda i, j, k: (0, k)), [(ds1, rowd(t512))],
        [jax.ShapeDtypeStruct((seq, d), F32)], [rowd(t512)], epi_dx, acc_shape=(t512, d)))

    small = {"ln2": sm_ln2, "b_ff1": sm_bff1, "ln1": sm_ln1, "pool": sm_pool, "lru0": sm_l0, "lru1": sm_l1,
             "conv": sm_conv}
    return loss_blk[0, 0], grad_x, small


REP = ("pool_scale", "conv_b", "b_out", "ln1_g", "ln1_b", "b_ff2", "ln2_g", "ln2_b")
SHARDED_SMALL = (("conv_w", 4), ("lru_ba", 2), ("lru_bx", 2), ("lru_lambda", 2))
WEIGHT_ORDER = ("w_in", "pool_w", "pool_scale", "conv_w", "conv_b", "lru_wa", "lru_ba", "lru_wx", "lru_bx",
                "lru_lambda", "w_pool_up", "w_lru_up", "w_out", "b_out", "ln1_g", "ln1_b", "w_ff1", "b_ff1",
                "w_ff2", "b_ff2", "ln2_g", "ln2_b")


def _pad_rows(a, rows):
    return jnp.concatenate([a, jnp.zeros((rows - a.shape[0], a.shape[1]), a.dtype)], axis=0)


def kernel(x, w_in, pool_w, pool_scale, conv_w, conv_b, lru_wa, lru_ba, lru_wx, lru_bx, lru_lambda, w_pool_up, w_lru_up, w_out, b_out, ln1_g, ln1_b, w_ff1, b_ff1, w_ff2, b_ff2, ln2_g, ln2_b, loss_target, m_w_in, m_pool_w, m_pool_scale, m_conv_w, m_conv_b, m_lru_wa, m_lru_ba, m_lru_wx, m_lru_bx, m_lru_lambda, m_w_pool_up, m_w_lru_up, m_w_out, m_b_out, m_ln1_g, m_ln1_b, m_w_ff1, m_b_ff1, m_w_ff2, m_b_ff2, m_ln2_g, m_ln2_b, v_w_in, v_pool_w, v_pool_scale, v_conv_w, v_conv_b, v_lru_wa, v_lru_ba, v_lru_wx, v_lru_bx, v_lru_lambda, v_w_pool_up, v_w_lru_up, v_w_out, v_b_out, v_ln1_g, v_ln1_b, v_w_ff1, v_b_ff1, v_w_ff2, v_b_ff2, v_ln2_g, v_ln2_b):
    args = dict(locals())
    w = {n: args[n] for n in WEIGHT_ORDER}
    mom = {n: args["m_" + n] for n in WEIGHT_ORDER}
    var = {n: args["v_" + n] for n in WEIGHT_ORDER}
    seq, d = x.shape[1], x.shape[2]
    n_heads, blk = lru_wa.shape[2], lru_wa.shape[4]
    n_groups = pool_w.shape[1]
    ff = b_ff1.shape[1]
    cx, cy, cc = _coords()
    me = 4 * cx + 2 * cy + cc
    cidx = jnp.reshape(cc, (1,)).astype(jnp.int32)
    qidx = jnp.reshape(2 * cx + cy, (1,)).astype(jnp.int32)

    fam_of = dict(BIG)
    sviews = {n: _shard_view(w[n], fam) for n, fam in BIG}
    size_of = {n: sviews[n].shape[1] for n, _ in BIG}
    for piece_name in ("w_in_lo", "w_in_hi"):
        fam_of[piece_name], size_of[piece_name] = fam_of["w_in"], size_of["w_in"]
    wts = {}

    def take_gathered(names, arrays):
        for n, g in zip(names, arrays):
            if n == "pool_w":
                g = g.reshape(n_groups, d // n_groups, d // n_groups)
            elif n in ("lru_wa", "lru_wx"):
                g = g.reshape(2, n_heads, blk, blk)
            elif fam_of[n] == "row":
                g = g.reshape(g.shape[1:])
            wts[n] = g

    shard_bf = {n: sviews[n].astype(BF16) for n, _ in BIG}
    piece = sviews["w_in"].shape[1] // W_IN_PASSES
    for k in range(W_IN_PASSES):
        name = "w_in_piece_%d" % k
        shard_bf[name] = shard_bf["w_in"][:, k * piece:(k + 1) * piece]
        fam_of[name] = "col"

    def gather_job(names, extra=()):
        return _ag_job([shard_bf[n] for n in names] + [e for e, _ in extra],
                       [fam_of[n] for n in names] + [f for _, f in extra])

    launched = [0]

    def on_sequencer(kind, job):
        launched[0] += 1
        return _sequencer_job("sq_%s_%d" % (kind, launched[0]), job, launched[0] % 2)

    class Plan:
        sibling = {"dw_ff2": ("w_ff2",), "dw_ff1": ("w_ff1",), "dw_lru_up": ("w_out", "w_pool_up", "w_lru_up"),
                   "dw_in_lo": ("w_in_lo",), "dw_in_hi": ("w_in_hi", "pool_w", "lru_wa", "lru_wx")}
        chips = {"dw_ff1": ("w_ff2",), "dw_out": ("w_ff1",), "dy_lru": ("w_out", "w_pool_up", "w_lru_up"),
                 "dw_in_hi": ("w_in_lo",), "dx": ("w_in_hi", "pool_w", "lru_wa", "lru_wx")}

        def __init__(self):
            self.grads, self.recv_a, self.parts, self.recv_b = {}, {}, {}, {}

        def grad(self, name, g):
            self.grads[name] = g if fam_of[name] == "col" else g.reshape((-1,) + g.shape[-2:])

        settle = {"dw_ff1": ("w_in",), "dw_out": ("w_ff2",), "dy_pool": ("w_ff1",), "lru_bwd_1": ("w_out",)}
        settle_add = {"dx": ("w_in_lo",)}

        def settled_before(self, host):
            return [wts[n] if n == "w_in" else self.recv_b[n] for n in self.settle.get(host, ())]

        def after(self, host):
            if host in self.chips:
                names = self.chips[host]
                _ORDER.extend(self.recv_b[n] for n in self.settle_add.get(host, ()))
                for n in names:
                    self.parts[n] = _rs_add("rs_add_" + n, self.grads[n], self.recv_a[n], fam_of[n], size_of[n], cidx)
                res = on_sequencer("chips", _rs_chips_job([self.parts[n] for n in names]))
                self.recv_b.update(zip(names, res))
            if host in self.sibling:
                names = self.sibling[host]
                res = on_sequencer("sibling", _rs_sibling_job(
                    [self.grads[n] for n in names], [fam_of[n] for n in names], [size_of[n] for n in names]))
                self.recv_a.update(zip(names, res))

    first = ("w_in_piece_0",)
    sv_shard = _pad_rows(jnp.concatenate([w[n].reshape(r, -1) for n, r in SHARDED_SMALL], axis=0), 16)
    gathered = on_sequencer("gather", gather_job(first, [(sv_shard, "col")]))
    take_gathered(first, gathered[:-1])
    vec = {n: w[n] for n in REP}
    vec["b_ff1"] = b_ff1
    vec["sv"] = gathered[-1]
    queue = [("w_in_piece_%d" % k,) for k in range(1, W_IN_PASSES)]
    queue += [("pool_w", "lru_wa", "lru_wx"), ("w_pool_up", "w_lru_up"), ("w_out",), ("w_ff1",), ("w_ff2",),
              ("w_in",)]
    for names in queue:
        take_gathered(names, on_sequencer("gather", gather_job(names)))

    plan = Plan()
    loss_part, grad_x, small = _local_step(x.reshape(seq, d), loss_target.reshape(seq, d), wts, vec, plan)
    loss = lax.psum(loss_part, AXES)

    out_g, out_d, out_m, out_v = {}, {}, {}, {}
    for n, fam in sorted(BIG, key=lambda nf: nf[0] in ("w_in", "pool_w", "lru_wa", "lru_wx")):
        halves = [n + "_lo", n + "_hi"] if n == "w_in" else [n]
        res = _rs_final_adam("adam_" + n, [plan.parts[h] for h in halves], [plan.recv_b[h] for h in halves],
                             sviews[n], _shard_view(mom[n], fam), _shard_view(var[n], fam), fam, qidx)
        out_g[n], out_d[n], out_m[n], out_v[n] = [r.reshape(w[n].shape) for r in res]

    rows = [small["pool"][0:1], small["conv"][4:5], small["ln1"][2:3], small["ln1"][0:1], small["ln1"][1:2],
            small["ln2"][2:3], small["ln2"][0:1], small["ln2"][1:2], small["b_ff1"][0:1].reshape(ff // d, d),
            small["conv"][0:4], small["lru0"][0:1], small["lru1"][0:1], small["lru0"][1:2], small["lru1"][1:2],
            small["lru0"][2:3], small["lru1"][2:3]]
    n_rep = len(REP) + ff // d
    n_rows = n_rep + sum(r for _, r in SHARDED_SMALL)
    pad_rows = -(-n_rows // SUBLANES) * SUBLANES
    packed = _pad_rows(jnp.concatenate(rows, axis=0), pad_rows)
    (all_small,) = _run_job("ag_small", _ag_job([packed], ["lead"]))
    g_small = _sum8("sum_small", all_small)

    def pack_rep(t):
        return jnp.concatenate([t[n] for n in REP] + [t["b_ff1"].reshape(ff // d, d)], axis=0)

    def pack_sh(t):
        return jnp.concatenate([t[n].reshape(r, -1) for n, r in SHARDED_SMALL], axis=0)

    g_rep = g_small[:n_rep]
    cs = d // N_DEV
    g_sh = lax.dynamic_slice_in_dim(g_small[n_rep:n_rows], me * cs, cs, axis=1)
    d_rep, m_rep, v_rep = _adam_small("adam_rep", pack_rep(w), g_rep, pack_rep(mom), pack_rep(var))
    d_sh, m_sh, v_sh = _adam_small("adam_sharded", pack_sh(w), g_sh, pack_sh(mom), pack_sh(var))

    def unpack(rep_t, sh_t, dst):
        for i, n in enumerate(REP):
            dst[n] = rep_t[i:i + 1].reshape(w[n].shape)
        dst["b_ff1"] = rep_t[len(REP):n_rep].reshape(w["b_ff1"].shape)
        r0 = 0
        for n, r in SHARDED_SMALL:
            dst[n] = sh_t[r0:r0 + r].reshape(w[n].shape)
            r0 += r

    unpack(g_rep, g_sh, out_g)
    unpack(d_rep, d_sh, out_d)
    unpack(m_rep, m_sh, out_m)
    unpack(v_rep, v_sh, out_v)

    _ORDER[:] = []
    outs = [loss, grad_x.reshape(x.shape)]
    for t in (out_g, out_d, out_m, out_v):
        outs += [t[n] for n in WEIGHT_ORDER]
    return tuple(outs)
```

```python
import jax
import jax.numpy as jnp
from jax import lax
from jax.experimental import pallas as pl
from jax.experimental.pallas import tpu as pltpu
from jax.experimental.pallas import tpu_sc as plsc

F32 = jnp.float32
BF16 = jnp.bfloat16
MESH = pl.DeviceIdType.MESH
AXES = ("x", "y", "c")
N_DEV = 8

DN_ALPHA = 2.0 ** 0.25
LN_EPS = 1e-5
LRU_C = 8.0
ADAM_LR = 0.001
ADAM_B1 = 0.9
ADAM_B2 = 0.999
ADAM_EPS = 1e-08
ADAM_WD = 0.01
ADAM_STEP = 10
GELU_C = 0.7978845608028654
GELU_K = 0.044715

W_IN_PASSES = 5
HALO = 16
SUBLANES = 8
VMEM_MB = 56


def _cparams(sem, vmem_mb=VMEM_MB):
    return pltpu.CompilerParams(dimension_semantics=sem, vmem_limit_bytes=vmem_mb << 20)


HBM_SPEC = pl.BlockSpec(memory_space=pl.ANY)


class _Job:
    def __init__(self, ins, outs, sems, start, finish, mid=None):
        self.ins, self.outs, self.sems = list(ins), list(outs), list(sems)
        self.start, self.finish, self.mid = start, finish, mid
        self.results = None


_ORDER = []


def _pcall(name, body, grid, in_specs, out_specs, out_shape, inputs, scratch=(), aliases=None,
           vmem_mb=VMEM_MB):
    token = [t for t in _ORDER if not any(t is a for a in inputs)]
    n_in = len(inputs)

    def ordered(*refs):
        return body(*refs[:n_in], *refs[n_in + len(token):])

    res = pl.pallas_call(ordered, name=name, grid=grid, in_specs=list(in_specs) + [HBM_SPEC] * len(token),
                         out_specs=list(out_specs), out_shape=list(out_shape), scratch_shapes=list(scratch),
                         input_output_aliases=aliases or {},
                         compiler_params=_cparams(("arbitrary",) * len(grid), vmem_mb))(*inputs, *token)
    _ORDER[:] = [res[0]]
    return res


def _run_job(name, job):
    ji, jo = len(job.ins), len(job.outs)

    def body(*refs):
        jins, jouts, sems = refs[:ji], refs[ji:ji + jo], refs[ji + jo:]
        job.start(jins, jouts, sems)
        if job.mid is not None:
            job.mid(jins, jouts, sems)
        job.finish(jins, jouts, sems)

    res = pl.pallas_call(body, name=name, in_specs=[HBM_SPEC] * ji, out_specs=[HBM_SPEC] * jo,
                         out_shape=job.outs, scratch_shapes=job.sems)(*job.ins)
    job.results = list(res)
    return job.results


def _sequencer_job(name, job, collective_id):
    ji, jo = len(job.ins), len(job.outs)

    def body(*refs):
        jins, jouts, sems = refs[:ji], refs[ji:ji + jo], refs[ji + jo:]
        barrier = pltpu.get_barrier_semaphore()
        x, y, c = lax.axis_index("x"), lax.axis_index("y"), lax.axis_index("c")
        for r in range(1, N_DEV):
            peer = ((1 - x) if r & 4 else x, (1 - y) if r & 2 else y, (1 - c) if r & 1 else c)
            pl.semaphore_signal(barrier, inc=1, device_id=peer, device_id_type=MESH)
        pl.semaphore_wait(barrier, N_DEV - 1)
        job.start(jins, jouts, sems)
        if job.mid is not None:
            job.mid(jins, jouts, sems)
        job.finish(jins, jouts, sems)

    res = pl.kernel(
        body, name=name, out_type=job.outs, mesh=plsc.ScalarSubcoreMesh(axis_name="sequencer", num_cores=1),
        scratch_types=job.sems, compiler_params=pltpu.CompilerParams(collective_id=collective_id),
    )(*job.ins)
    job.results = list(res)
    return job.results


def _dot(mode, a, b):
    if mode == "nn":
        dims = (((1,), (0,)), ((), ()))
    elif mode == "nt":
        dims = (((1,), (1,)), ((), ()))
    else:
        dims = (((0,), (0,)), ((), ()))
    return lax.dot_general(a, b, dims, preferred_element_type=F32)


def _sig(x):
    return 0.5 * jnp.tanh(0.5 * x) + 0.5


def _gelu(x):
    t = jnp.tanh(GELU_C * (x + GELU_K * x * x * x))
    return 0.5 * x * (1.0 + t)


def _gelu_grad(x):
    x2 = x * x
    t = jnp.tanh(GELU_C * (x + GELU_K * x * x2))
    return 0.5 * (1.0 + t) + 0.5 * x * (1.0 - t * t) * GELU_C * (1.0 + 3.0 * GELU_K * x2)


def _colsum(v):
    return jnp.sum(v, axis=0, keepdims=True)


def _mm(name, mode, grid, a, a_spec, b, b_spec, extras, out_shapes, out_specs, epi, *,
        order="ij", acc_shape=None, aliases=None, vmem_mb=VMEM_MB):
    gm, gn, gk = grid

    def spec(s):
        bs, f = s
        if f is None:
            return pl.BlockSpec(memory_space=pl.ANY)
        if order == "ij":
            return pl.BlockSpec(bs, lambda i, j, k, f=f: f(i, j, k))
        return pl.BlockSpec(bs, lambda j, i, k, f=f: f(i, j, k))

    ne, no = len(extras), len(out_shapes)

    def kern(*refs):
        a_ref, b_ref = refs[0], refs[1]
        ex = refs[2:2 + ne]
        outs = refs[2 + ne:2 + ne + no]
        if order == "ij":
            i, j = pl.program_id(0), pl.program_id(1)
        else:
            j, i = pl.program_id(0), pl.program_id(1)
        k = pl.program_id(2)
        prod = _dot(mode, a_ref[...], b_ref[...])
        if gk == 1:
            epi(prod, i, j, ex, outs)
        elif epi is None:
            @pl.when(k == 0)
            def _():
                outs[0][...] = prod

            @pl.when(k > 0)
            def _():
                outs[0][...] += prod
        else:
            acc = refs[-1]

            @pl.when(k == 0)
            def _():
                acc[...] = prod

            @pl.when(k > 0)
            def _():
                acc[...] += prod

            @pl.when(k == gk - 1)
            def _():
                epi(acc[...], i, j, ex, outs)

    g = (gm, gn, gk) if order == "ij" else (gn, gm, gk)
    scratch = [pltpu.VMEM(acc_shape, F32)] if gk > 1 and epi is not None else []
    return _pcall(name, kern, g, [spec(a_spec), spec(b_spec)] + [spec(s) for _, s in extras],
                  [spec(s) for s in out_specs], out_shapes, [a, b] + [e for e, _ in extras],
                  scratch=scratch, aliases=aliases, vmem_mb=vmem_mb)


def _ext(ref, c, n_chunks, tc, seq):
    c0 = pl.multiple_of(c * tc, tc)
    body = ref[pl.ds(c0, tc), :].astype(F32)
    t0 = pl.multiple_of(jnp.maximum(c0 - HALO, 0), HALO)
    b0 = pl.multiple_of(jnp.minimum(c0 + tc, seq - HALO), HALO)
    top = ref[pl.ds(t0, HALO), :].astype(F32)
    bot = ref[pl.ds(b0, HALO), :].astype(F32)
    top = jnp.where(c > 0, top, 0.0)
    bot = jnp.where(c < n_chunks - 1, bot, 0.0)
    return jnp.concatenate([top, body, bot], axis=0)


def _shifted(vext, off, tc):
    n = vext.shape[0]
    r = vext if off == 0 else pltpu.roll(vext, (n - off) % n, 0)
    return r[HALO:HALO + tc]


def _win_sum(vext, g, extra, tc):
    s2 = vext + pltpu.roll(vext, 1, 0)
    s4 = s2 + pltpu.roll(s2, 2, 0)
    s8 = s4 + pltpu.roll(s4, 4, 0)
    s16 = s8 + pltpu.roll(s8, 8, 0)
    outs = [_shifted(s, extra + hw - 1, tc) for s, hw in ((s2, 1), (s4, 2), (s8, 4), (s16, 8))]
    return jnp.where(g == 0, outs[0], jnp.where(g == 1, outs[1], jnp.where(g == 2, outs[2], outs[3])))


def _win_cnt(t, hw, seq):
    return (jnp.minimum(t + hw, seq) - jnp.maximum(t - hw, 0)).astype(F32)


def _pool_d(uext, g, c, tc, seq):
    hw = jnp.left_shift(1, g)
    t = c * tc + lax.broadcasted_iota(jnp.int32, (tc, 1), 0)
    ws = _win_sum(uext, g, 0, tc)
    return ws / _win_cnt(t, hw, seq) - uext[HALO:HALO + tc]


def _scan_tiles(a_ref, b_ref, h_ref, carry_ref, n_tiles, reverse):
    blk = a_ref.shape[1]
    row = lax.broadcasted_iota(jnp.int32, (SUBLANES, blk), 0)

    def tile(j, hc):
        jj = (n_tiles - 1 - j) if reverse else j
        off = pl.multiple_of(jj * SUBLANES, SUBLANES)
        a = a_ref[pl.ds(off, SUBLANES), :]
        b = b_ref[pl.ds(off, SUBLANES), :]
        for kk in (1, 2, 4):
            sh = (SUBLANES - kk) if reverse else kk
            a_s = pltpu.roll(a, sh, 0)
            b_s = pltpu.roll(b, sh, 0)
            m = (row < SUBLANES - kk) if reverse else (row >= kk)
            a_s = jnp.where(m, a_s, 1.0)
            b_s = jnp.where(m, b_s, 0.0)
            b = a * b_s + b
            a = a * a_s
        h = a * hc + b
        h_ref[pl.ds(off, SUBLANES), :] = h
        return h[0:1, :] if reverse else h[SUBLANES - 1:SUBLANES, :]

    group = SUBLANES if n_tiles % SUBLANES == 0 else 1

    def tiles(jg, hc):
        for u in range(group):
            hc = tile(jg * group + u, hc)
        return hc

    hc = lax.fori_loop(0, n_tiles // group, tiles, carry_ref[0:1, :])
    carry_ref[0:1, :] = hc


def _lru_k(lam):
    y = -lam
    e = jnp.exp(-jnp.abs(y))
    u = 1.0 + e
    l1p = jnp.where(u == 1.0, e, jnp.log(u) * (e / (u - 1.0)))
    return -LRU_C * (jnp.maximum(y, 0.0) + l1p)


def _lru_gates(xc, wa, wx, ba, bx, lam):
    xb = xc.astype(BF16)
    r = _sig(jnp.dot(xb, wa, preferred_element_type=F32) + ba)
    i = _sig(jnp.dot(xb, wx, preferred_element_type=F32) + bx)
    k = _lru_k(lam)
    la = k * r
    a = jnp.exp(la)
    s = jnp.sqrt(-jnp.tanh(la) * (a * a + 1.0))
    return r, i, k, a, s


SV_CONV, SV_BA, SV_BX, SV_LAM = 0, 4, 6, 8


def _pool_fwd(z, pw, scale, seq, d, tc):
    n_g = pw.shape[0]
    pg = d // n_g
    n_chunks = seq // tc

    def kern(z_ref, pw_ref, sc_ref, y_ref):
        g, c = pl.program_id(0), pl.program_id(1)
        uext = _ext(z_ref, c, n_chunks, tc, seq)
        dd = _pool_d(uext, g, c, tc, seq)
        q = jnp.dot(dd.astype(BF16), pw_ref[...], preferred_element_type=F32)
        y_ref[...] = (q * sc_ref[...]).astype(BF16)

    (y,) = _pcall(
        "pool_fwd", kern, (n_g, n_chunks),
        [pl.BlockSpec((seq, pg), lambda g, c: (0, g)),
         pl.BlockSpec((None, pg, pg), lambda g, c: (g, 0, 0)),
         pl.BlockSpec((1, pg), lambda g, c: (0, g))],
        [pl.BlockSpec((tc, pg), lambda g, c: (c, g))],
        [jax.ShapeDtypeStruct((seq, d), BF16)], [z, pw, scale])
    return y


def _lru_fwd(z, sv, conv_b, wa, wx, seq, d, tc):
    n_h, blk = wa.shape[1], wa.shape[2]
    n_chunks = seq // tc
    lru_off = d // blk

    def kern(z_ref, sv_ref, cb_ref, wa_ref, wx_ref, xc_ref, h_ref, a_s, b_s, carry):
        c = pl.program_id(1)
        uext = _ext(z_ref, c, n_chunks, tc, seq)
        xc = cb_ref[...]
        for k in range(4):
            xc = xc + _shifted(uext, k - 2, tc) * sv_ref[SV_CONV + k:SV_CONV + k + 1, :]
        xc_ref[...] = xc
        _, i, _, a, s = _lru_gates(xc, wa_ref[...], wx_ref[...], sv_ref[SV_BA:SV_BA + 1, :],
                                   sv_ref[SV_BX:SV_BX + 1, :], sv_ref[SV_LAM:SV_LAM + 1, :])
        a_s[...] = a
        b_s[...] = s * (i * xc)

        @pl.when(c == 0)
        def _():
            carry[...] = jnp.zeros_like(carry)

        _scan_tiles(a_s, b_s, h_ref, carry, tc // SUBLANES, False)

    col = lambda h, c: (c, h)
    return _pcall(
        "lru_fwd", kern, (n_h, n_chunks),
        [pl.BlockSpec((seq, blk), lambda h, c: (0, lru_off + h)),
         pl.BlockSpec((16, blk), lambda h, c: (0, h)),
         pl.BlockSpec((1, blk), lambda h, c: (0, h)),
         pl.BlockSpec((None, None, blk, blk), lambda h, c: (0, h, 0, 0)),
         pl.BlockSpec((None, None, blk, blk), lambda h, c: (0, h, 0, 0))],
        [pl.BlockSpec((tc, blk), col), pl.BlockSpec((tc, blk), col)],
        [jax.ShapeDtypeStruct((seq, d), F32), jax.ShapeDtypeStruct((seq, d), F32)],
        [z, sv, conv_b, wa, wx],
        scratch=[pltpu.VMEM((tc, blk), F32), pltpu.VMEM((tc, blk), F32), pltpu.VMEM((SUBLANES, blk), F32)])


def _lru_rev(z, sv, wa, wx, xc, h_f, seq, d, tc):
    n_h, blk = wa.shape[1], wa.shape[2]
    n_chunks = seq // tc
    gate_off = 2 * d // blk

    def kern(z_ref, sv_ref, wa_ref, wx_ref, xc_ref, hf_ref, hb_ref, y_ref, a_s, b_s, carry):
        c = pl.program_id(1)
        xc = xc_ref[...]
        _, i, _, a, s = _lru_gates(xc, wa_ref[...], wx_ref[...], sv_ref[SV_BA + 1:SV_BA + 2, :],
                                   sv_ref[SV_BX + 1:SV_BX + 2, :], sv_ref[SV_LAM + 1:SV_LAM + 2, :])
        a_s[...] = a
        b_s[...] = s * (i * xc)

        @pl.when(c == 0)
        def _():
            carry[...] = jnp.zeros_like(carry)

        _scan_tiles(a_s, b_s, hb_ref, carry, tc // SUBLANES, True)
        y_ref[...] = ((hf_ref[...] + hb_ref[...]) * _gelu(z_ref[...])).astype(BF16)

    rev = lambda h, c: (n_chunks - 1 - c, h)
    return _pcall(
        "lru_rev", kern, (n_h, n_chunks),
        [pl.BlockSpec((tc, blk), lambda h, c: (n_chunks - 1 - c, gate_off + h)),
         pl.BlockSpec((16, blk), lambda h, c: (0, h)),
         pl.BlockSpec((None, None, blk, blk), lambda h, c: (1, h, 0, 0)),
         pl.BlockSpec((None, None, blk, blk), lambda h, c: (1, h, 0, 0)),
         pl.BlockSpec((tc, blk), rev), pl.BlockSpec((tc, blk), rev)],
        [pl.BlockSpec((tc, blk), rev), pl.BlockSpec((tc, blk), rev)],
        [jax.ShapeDtypeStruct((seq, d), F32), jax.ShapeDtypeStruct((seq, d), BF16)],
        [z, sv, wa, wx, xc, h_f],
        scratch=[pltpu.VMEM((tc, blk), F32), pltpu.VMEM((tc, blk), F32), pltpu.VMEM((SUBLANES, blk), F32)])


def _merge(y_pool, w_pu, y_lru, w_lu, z, seq, d, tm, tn):
    n_n = d // tn

    def kern(yp_ref, wp_ref, yl_ref, wl_ref, la_ref, lb_ref, m_ref, pa_ref, pb_ref):
        pa = jnp.dot(yp_ref[...], wp_ref[...], preferred_element_type=F32)
        pb = jnp.dot(yl_ref[...], wl_ref[...], preferred_element_type=F32)
        m_ref[...] = (_sig(la_ref[...]) * pa + _sig(lb_ref[...]) * pb).astype(BF16)
        pa_ref[...] = pa.astype(BF16)
        pb_ref[...] = pb.astype(BF16)

    row = pl.BlockSpec((tm, d), lambda i, j: (i, 0))
    wcol = pl.BlockSpec((d, tn), lambda i, j: (0, j))
    out = pl.BlockSpec((tm, tn), lambda i, j: (i, j))
    sh = jax.ShapeDtypeStruct((seq, d), BF16)
    return _pcall(
        "merge", kern, (seq // tm, n_n),
        [row, wcol, row, wcol,
         pl.BlockSpec((tm, tn), lambda i, j: (i, 3 * n_n + j)),
         pl.BlockSpec((tm, tn), lambda i, j: (i, 4 * n_n + j))],
        [out, out, out], [sh, sh, sh], [y_pool, w_pu, y_lru, w_lu, z, z])


def _ln_fwd(s, g, b):
    mu = jnp.mean(s, axis=-1, keepdims=True)
    xc = s - mu
    var = jnp.mean(xc * xc, axis=-1, keepdims=True)
    rstd = lax.rsqrt(var + LN_EPS)
    xhat = xc * rstd
    return xhat, rstd, xhat * g + b


def _ln_bwd(dy, xhat, rstd, g):
    dyg = dy * g
    m1 = jnp.mean(dyg, axis=-1, keepdims=True)
    m2 = jnp.mean(dyg * xhat, axis=-1, keepdims=True)
    return rstd * (dyg - m1 - xhat * m2)


def _pool_bwd(z, dy_pool, pw, scale, dz, seq, d, tc):
    n_g = pw.shape[0]
    pg = d // n_g
    n_chunks = seq // tc

    def kern(z_ref, dy_ref, pw_ref, sc_ref, dz_in, dz_ref, dpw_ref, dsc_ref):
        del dz_in
        g, c = pl.program_id(0), pl.program_id(1)
        hw = jnp.left_shift(1, g)
        uext = _ext(z_ref, c, n_chunks, tc, seq)
        dd = _pool_d(uext, g, c, tc, seq).astype(BF16)
        pwv = pw_ref[...]
        q = jnp.dot(dd, pwv, preferred_element_type=F32)
        dyext = _ext(dy_ref, c, n_chunks, tc, seq)

        @pl.when(c == 0)
        def _():
            dsc_ref[...] = jnp.zeros_like(dsc_ref)
            dpw_ref[...] = jnp.zeros_like(dpw_ref)

        dsc_ref[0:1, :] += _colsum(dyext[HALO:HALO + tc] * q)
        dqext = (dyext * sc_ref[...]).astype(BF16)
        dpw_ref[...] += _dot("tn", dd, dqext[HALO:HALO + tc])
        ddext = _dot("nt", dqext, pwv)
        text = c * tc - HALO + lax.broadcasted_iota(jnp.int32, (tc + 2 * HALO, 1), 0)
        v = ddext / jnp.maximum(_win_cnt(text, hw, seq), 1.0)
        dz_ref[...] = (_win_sum(v, g, 1, tc) - ddext[HALO:HALO + tc]).astype(BF16)

    return _pcall(
        "pool_bwd", kern, (n_g, n_chunks),
        [pl.BlockSpec((seq, pg), lambda g, c: (0, g)),
         pl.BlockSpec((seq, pg), lambda g, c: (0, g)),
         pl.BlockSpec((None, pg, pg), lambda g, c: (g, 0, 0)),
         pl.BlockSpec((1, pg), lambda g, c: (0, g)),
         pl.BlockSpec(memory_space=pl.ANY)],
        [pl.BlockSpec((None, tc, pg), lambda g, c: (2, c, g)),
         pl.BlockSpec((None, pg, pg), lambda g, c: (g, 0, 0)),
         pl.BlockSpec((SUBLANES, pg), lambda g, c: (0, g))],
        [jax.ShapeDtypeStruct(dz.shape, dz.dtype),
         jax.ShapeDtypeStruct((n_g, pg, pg), F32),
         jax.ShapeDtypeStruct((SUBLANES, d), F32)],
        [z, dy_pool, pw, scale, dz], aliases={4: 0})


def _lru_bwd(direction, xc, dh, h_dir, sv, wa, wx, dxc_prev, seq, d, tc):
    reverse = direction == 1
    n_h, blk = wa.shape[1], wa.shape[2]
    n_chunks = seq // tc
    has_prev = dxc_prev is not None

    def kern(*refs):
        xc_ref, dh_ref, h_ref, sv_ref, wa_ref, wx_ref = refs[:6]
        p = 6
        prev_ref = None
        if has_prev:
            prev_ref = refs[p]
            p += 1
        dxc_ref, dwa_ref, dwx_ref, sm_ref, at_s, g_s, carry, acarry = refs[p:p + 8]
        c = pl.program_id(1)
        cr = c if reverse else n_chunks - 1 - c
        c0 = pl.multiple_of(cr * tc, tc)

        @pl.when(c == 0)
        def _():
            carry[...] = jnp.zeros_like(carry)
            acarry[...] = jnp.zeros_like(acarry)
            dwa_ref[...] = jnp.zeros_like(dwa_ref)
            dwx_ref[...] = jnp.zeros_like(dwx_ref)
            sm_ref[...] = jnp.zeros_like(sm_ref)

        xc = xc_ref[...]
        wav, wxv = wa_ref[...], wx_ref[...]
        lam = sv_ref[SV_LAM + direction:SV_LAM + direction + 1, :]
        r, i, k, a, s = _lru_gates(xc, wav, wxv, sv_ref[SV_BA + direction:SV_BA + direction + 1, :],
                                   sv_ref[SV_BX + direction:SV_BX + direction + 1, :], lam)
        rowi = lax.broadcasted_iota(jnp.int32, (tc, blk), 0)
        hbody = h_ref[pl.ds(c0, tc), :]
        if not reverse:
            p0 = pl.multiple_of(jnp.maximum(c0 - SUBLANES, 0), SUBLANES)
            edge = jnp.where(cr > 0, h_ref[pl.ds(p0, SUBLANES), :][SUBLANES - 1:SUBLANES, :], 0.0)
            hprev = jnp.where(rowi == 0, edge, pltpu.roll(hbody, 1, 0))
            at = jnp.where(rowi == tc - 1, acarry[0:1, :], pltpu.roll(a, tc - 1, 0))
        else:
            n0 = pl.multiple_of(jnp.minimum(c0 + tc, seq - SUBLANES), SUBLANES)
            edge = jnp.where(cr < n_chunks - 1, h_ref[pl.ds(n0, SUBLANES), :][0:1, :], 0.0)
            hprev = jnp.where(rowi == tc - 1, edge, pltpu.roll(hbody, tc - 1, 0))
            at = jnp.where(rowi == 0, acarry[0:1, :], pltpu.roll(a, 1, 0))
        at_s[...] = at
        _scan_tiles(at_s, dh_ref, g_s, carry, tc // SUBLANES, not reverse)
        acarry[0:1, :] = a[tc - 1:tc, :] if reverse else a[0:1, :]

        gt = g_s[...]
        da = gt * hprev
        di = gt * s * xc
        dxc = gt * s * i
        ds = gt * (i * xc)
        dl = da * a - ds * (a * a) / s
        dpr = (dl * k) * r * (1.0 - r)
        dpi = di * i * (1.0 - i)
        sm_ref[0:1, :] += _colsum(dpr)
        sm_ref[1:2, :] += _colsum(dpi)
        sm_ref[2:3, :] += _colsum(dl * r) * (LRU_C * _sig(-lam))
        xb, dprb, dpib = xc.astype(BF16), dpr.astype(BF16), dpi.astype(BF16)
        dwa_ref[...] += _dot("tn", xb, dprb)
        dwx_ref[...] += _dot("tn", xb, dpib)
        dxc = dxc + _dot("nt", dprb, wav) + _dot("nt", dpib, wxv)
        if has_prev:
            dxc = dxc + prev_ref[...]
        dxc_ref[...] = dxc

    if reverse:
        chunk = lambda h, c: (c, h)
    else:
        chunk = lambda h, c: (n_chunks - 1 - c, h)
    wspec = pl.BlockSpec((None, None, blk, blk), lambda h, c: (direction, h, 0, 0))
    ins = [xc, dh, h_dir, sv, wa, wx] + ([dxc_prev] if has_prev else [])
    in_specs = [pl.BlockSpec((tc, blk), chunk), pl.BlockSpec((tc, blk), chunk),
                pl.BlockSpec((seq, blk), lambda h, c: (0, h)),
                pl.BlockSpec((16, blk), lambda h, c: (0, h)), wspec, wspec]
    if has_prev:
        in_specs.append(pl.BlockSpec((tc, blk), chunk))
    return _pcall(
        "lru_bwd_%d" % direction, kern, (n_h, n_chunks), in_specs,
        [pl.BlockSpec((tc, blk), chunk),
         pl.BlockSpec((None, blk, blk), lambda h, c: (h, 0, 0)),
         pl.BlockSpec((None, blk, blk), lambda h, c: (h, 0, 0)),
         pl.BlockSpec((SUBLANES, blk), lambda h, c: (0, h))],
        [jax.ShapeDtypeStruct((seq, d), F32),
         jax.ShapeDtypeStruct((n_h, blk, blk), F32),
         jax.ShapeDtypeStruct((n_h, blk, blk), F32),
         jax.ShapeDtypeStruct((SUBLANES, d), F32)],
        ins,
        scratch=[pltpu.VMEM((tc, blk), F32), pltpu.VMEM((tc, blk), F32),
                 pltpu.VMEM((SUBLANES, blk), F32), pltpu.VMEM((SUBLANES, blk), F32)])


def _conv_bwd(z, dxc, sv, dz, seq, d, tc, tcol):
    n_chunks = seq // tc
    lru_off = d // tcol

    def kern(z_ref, dx_ref, sv_ref, dz_in, dz_ref, sm_ref):
        del dz_in
        c = pl.program_id(1)
        uext = _ext(z_ref, c, n_chunks, tc, seq)
        dext = _ext(dx_ref, c, n_chunks, tc, seq)
        dbody = dext[HALO:HALO + tc]

        @pl.when(c == 0)
        def _():
            sm_ref[...] = jnp.zeros_like(sm_ref)

        du = jnp.zeros_like(dbody)
        for k in range(4):
            du = du + _shifted(dext, 2 - k, tc) * sv_ref[SV_CONV + k:SV_CONV + k + 1, :]
            sm_ref[k:k + 1, :] += _colsum(dbody * _shifted(uext, k - 2, tc))
        sm_ref[4:5, :] += _colsum(dbody)
        dz_ref[...] = du.astype(BF16)

    return _pcall(
        "conv_bwd", kern, (d // tcol, n_chunks),
        [pl.BlockSpec((seq, tcol), lambda j, c: (0, lru_off + j)),
         pl.BlockSpec((seq, tcol), lambda j, c: (0, j)),
         pl.BlockSpec((16, tcol), lambda j, c: (0, j)),
         pl.BlockSpec(memory_space=pl.ANY)],
        [pl.BlockSpec((None, tc, tcol), lambda j, c: (3, c, j)),
         pl.BlockSpec((SUBLANES, tcol), lambda j, c: (0, j))],
        [jax.ShapeDtypeStruct(dz.shape, dz.dtype), jax.ShapeDtypeStruct((SUBLANES, d), F32)],
        [z, dxc, sv, dz], aliases={3: 0})


BIG = (("w_in", "col"), ("pool_w", "row"), ("lru_wa", "row"), ("lru_wx", "row"), ("w_pool_up", "row"),
       ("w_lru_up", "row"), ("w_out", "row"), ("w_ff1", "col"), ("w_ff2", "row"))


def _shard_view(w, fam):
    if fam == "col":
        return w.reshape(w.shape[-2:])
    return w.reshape((-1,) + w.shape[-2:])


def _full_shape(sv_shape, fam):
    if fam == "col":
        return (sv_shape[0], N_DEV * sv_shape[1])
    return (sv_shape[0], N_DEV * sv_shape[1], sv_shape[2])


def _slot(ref, fam, p, size):
    if fam == "lead":
        return ref.at[p]
    start = pl.multiple_of(p * size, size)
    if fam == "col":
        return ref.at[:, pl.ds(start, size)]
    return ref.at[:, pl.ds(start, size), :]


def _coords():
    return lax.axis_index("x"), lax.axis_index("y"), lax.axis_index("c")


def _ag_job(shards, fams):
    n = len(shards)
    fulls = []
    for s, fam in zip(shards, fams):
        if fam == "lead":
            fulls.append(jax.ShapeDtypeStruct((N_DEV,) + s.shape, s.dtype))
        else:
            fulls.append(jax.ShapeDtypeStruct(_full_shape(s.shape, fam), s.dtype))
    sizes = [1 if fam == "lead" else s.shape[1] for s, fam in zip(shards, fams)]

    def ctx(ins, outs, sems):
        send, recv, loc = sems
        x, y, c = _coords()
        chips = [(1 - x, y), (x, 1 - y), (1 - x, 1 - y)]

        def copy(a, k, owner, to, src=None):
            dst = _slot(outs[a], fams[a], owner, sizes[a])
            return pltpu.make_async_remote_copy(
                src_ref=dst if src is None else src, dst_ref=dst,
                send_sem=send.at[a, k], recv_sem=recv.at[a, k], device_id=to, device_id_type=MESH)

        def local(a):
            return pltpu.make_async_copy(ins[a], _slot(outs[a], fams[a], 4 * x + 2 * y + c, sizes[a]), loc.at[a])

        return x, y, c, chips, copy, local

    def start(ins, outs, sems):
        x, y, c, chips, copy, local = ctx(ins, outs, sems)
        me = 4 * x + 2 * y + c
        for a in range(n):
            local(a).start()
            copy(a, 0, me, (x, y, 1 - c), ins[a]).start()
            for j, (cx, cy) in enumerate(chips):
                copy(a, 1 + j, me, (cx, cy, c), ins[a]).start()

    def mid(ins, outs, sems):
        x, y, c, chips, copy, _ = ctx(ins, outs, sems)
        for a in range(n):
            for j, (cx, cy) in enumerate(chips):
                owner = 4 * cx + 2 * cy + c
                copy(a, 1 + j, owner, (x, y, c)).wait_recv()
                copy(a, 4 + j, owner, (x, y, 1 - c)).start()

    def finish(ins, outs, sems):
        x, y, c, chips, copy, local = ctx(ins, outs, sems)
        me = 4 * x + 2 * y + c
        for a in range(n):
            copy(a, 0, 4 * x + 2 * y + (1 - c), (x, y, c)).wait_recv()
            for j, (cx, cy) in enumerate(chips):
                copy(a, 4 + j, 4 * cx + 2 * cy + (1 - c), (x, y, c)).wait_recv()
            copy(a, 0, me, (x, y, 1 - c), ins[a]).wait_send()
            for j, (cx, cy) in enumerate(chips):
                copy(a, 1 + j, me, (cx, cy, c), ins[a]).wait_send()
                copy(a, 4 + j, 4 * cx + 2 * cy + c, (x, y, 1 - c)).wait_send()
            local(a).wait()

    sems = [pltpu.SemaphoreType.DMA((n, 7)), pltpu.SemaphoreType.DMA((n, 7)), pltpu.SemaphoreType.DMA((n,))]
    return _Job(shards, fulls, sems, start, finish, mid)


def _rs_sibling_job(fulls, fams, sizes):
    n = len(fulls)
    outs = []
    for f, fam, sz in zip(fulls, fams, sizes):
        if fam == "col":
            outs.append(jax.ShapeDtypeStruct((4, f.shape[0], sz), f.dtype))
        else:
            outs.append(jax.ShapeDtypeStruct((4, f.shape[0], sz, f.shape[2]), f.dtype))

    def copies(ins, rcv, sems):
        send, recv = sems
        x, y, c = _coords()
        return [pltpu.make_async_remote_copy(
            src_ref=_slot(ins[a], fams[a], 2 * q + (1 - c), sizes[a]), dst_ref=rcv[a].at[q],
            send_sem=send.at[a, q], recv_sem=recv.at[a, q], device_id=(x, y, 1 - c), device_id_type=MESH)
            for a in range(n) for q in range(4)]

    def start(ins, rcv, sems):
        for cp in copies(ins, rcv, sems):
            cp.start()

    def finish(ins, rcv, sems):
        for cp in copies(ins, rcv, sems):
            cp.wait()

    return _Job(fulls, outs, [pltpu.SemaphoreType.DMA((n, 4)), pltpu.SemaphoreType.DMA((n, 4))], start, finish)


def _rs_chips_job(parts):
    n = len(parts)
    outs = [jax.ShapeDtypeStruct((3,) + p.shape[1:], p.dtype) for p in parts]

    def copies(ins, rcv, sems):
        send, recv = sems
        x, y, c = _coords()
        cps = []
        for a in range(n):
            for r in (1, 2, 3):
                tx, ty = (1 - x) if r & 2 else x, (1 - y) if r & 1 else y
                cps.append(pltpu.make_async_remote_copy(
                    src_ref=ins[a].at[2 * tx + ty], dst_ref=rcv[a].at[r - 1],
                    send_sem=send.at[a, r - 1], recv_sem=recv.at[a, r - 1],
                    device_id=(tx, ty, c), device_id_type=MESH))
        return cps

    def start(ins, rcv, sems):
        for cp in copies(ins, rcv, sems):
            cp.start()

    def finish(ins, rcv, sems):
        for cp in copies(ins, rcv, sems):
            cp.wait()

    return _Job(parts, outs, [pltpu.SemaphoreType.DMA((n, 3)), pltpu.SemaphoreType.DMA((n, 3))], start, finish)


def _tile_rows(rows, cols):
    tr = rows
    while tr * cols > (1 << 18) and tr % (2 * SUBLANES) == 0:
        tr //= 2
    return tr


def _rs_add(name, full, recv_a, fam, size, cidx):
    if fam == "col":
        rows = full.shape[0]
        tr = _tile_rows(rows, size)
        grid = (4, rows // tr)
        f_spec = pl.BlockSpec((tr, size), lambda q, i, cr: (i, 2 * q + cr[0]))
        s_spec = pl.BlockSpec((None, tr, size), lambda q, i, cr: (q, i, 0))
    else:
        nb, cols = full.shape[0], full.shape[2]
        tr = _tile_rows(nb * size, cols) // nb if nb > 1 else _tile_rows(size, cols)
        nt = size // tr
        grid = (4, nt)
        f_spec = pl.BlockSpec((nb, tr, cols), lambda q, i, cr: (0, (2 * q + cr[0]) * nt + i, 0))
        s_spec = pl.BlockSpec((None, nb, tr, cols), lambda q, i, cr: (q, 0, i, 0))

    token = [t for t in _ORDER if t is not full]

    def kern(c_ref, f_ref, r_ref, *rest):
        del c_ref
        rest[-1][...] = (f_ref[...] + r_ref[...]).astype(BF16)

    out = pl.pallas_call(
        kern, name=name,
        grid_spec=pltpu.PrefetchScalarGridSpec(num_scalar_prefetch=1, grid=grid,
                                               in_specs=[f_spec, s_spec] + [HBM_SPEC] * len(token), out_specs=s_spec),
        out_shape=jax.ShapeDtypeStruct(recv_a.shape, BF16),
        compiler_params=_cparams(("arbitrary", "arbitrary"), 32),
    )(cidx, full, recv_a, *token)
    _ORDER[:] = [out]
    return out


def _adam(w, g, m, v):
    m2 = ADAM_B1 * m + (1.0 - ADAM_B1) * g
    v2 = ADAM_B2 * v + (1.0 - ADAM_B2) * (g * g)
    m_hat = m2 / (1.0 - ADAM_B1 ** ADAM_STEP)
    v_hat = v2 / (1.0 - ADAM_B2 ** ADAM_STEP)
    delta = -ADAM_LR * (m_hat / (jnp.sqrt(v_hat) + ADAM_EPS) + ADAM_WD * w)
    return delta, m2, v2


def _rs_final_adam(name, parts, recv_b, w, m, v, fam, qidx):
    shp = w.shape
    pieces = parts if isinstance(parts, (list, tuple)) else [parts]
    recvs = recv_b if isinstance(recv_b, (list, tuple)) else [recv_b]
    n_p = len(pieces)
    first_blk = [0] * n_p
    if fam == "col":
        rows, cols = shp
        tr = _tile_rows(min(p.shape[1] for p in pieces), cols)
        per = [p.shape[1] // tr for p in pieces]
        for h in range(1, n_p):
            first_blk[h] = first_blk[h - 1] + per[h - 1]
        grid = (rows // tr,)
        w_spec = pl.BlockSpec((tr, cols), lambda i, qr: (i, 0))

        def piece_row(i, h):
            return jnp.clip(i - first_blk[h], 0, per[h] - 1)

        p_specs = [pl.BlockSpec((None, tr, cols), lambda i, qr, h=h: (qr[0], piece_row(i, h), 0))
                   for h in range(n_p)]
        r_specs = [pl.BlockSpec((3, tr, cols), lambda i, qr, h=h: (0, piece_row(i, h), 0)) for h in range(n_p)]
    else:
        assert n_p == 1
        nb, rows, cols = shp
        tr = _tile_rows(rows, cols)
        nt = rows // tr
        grid = (nb * nt,)
        w_spec = pl.BlockSpec((None, tr, cols), lambda i, qr: (i // nt, i % nt, 0))
        p_specs = [pl.BlockSpec((None, None, tr, cols), lambda i, qr: (qr[0], i // nt, i % nt, 0))]
        r_specs = [pl.BlockSpec((3, None, tr, cols), lambda i, qr: (0, i // nt, i % nt, 0))]

    token = list(_ORDER)

    def kern(*refs):
        p_refs, r_refs = refs[1:1 + n_p], refs[1 + n_p:1 + 2 * n_p]
        w_ref, m_ref, v_ref = refs[1 + 2 * n_p:4 + 2 * n_p]
        g_out, d_out, m_out, v_out = refs[4 + 2 * n_p + len(token):]

        def total(h):
            p_ref, r_ref = p_refs[h], r_refs[h]
            return ((p_ref[...].astype(F32) + r_ref[0].astype(F32)) + r_ref[1].astype(F32)) + r_ref[2].astype(F32)

        g = total(0)
        for h in range(1, n_p):
            g = jnp.where(pl.program_id(0) >= first_blk[h], total(h), g)
        delta, m2, v2 = _adam(w_ref[...], g, m_ref[...], v_ref[...])
        g_out[...] = g
        d_out[...] = delta
        m_out[...] = m2
        v_out[...] = v2

    sh = jax.ShapeDtypeStruct(shp, F32)
    res = pl.pallas_call(
        kern, name=name,
        grid_spec=pltpu.PrefetchScalarGridSpec(
            num_scalar_prefetch=1, grid=grid,
            in_specs=p_specs + r_specs + [w_spec, w_spec, w_spec] + [HBM_SPEC] * len(token),
            out_specs=[w_spec] * 4),
        out_shape=[sh] * 4,
        compiler_params=_cparams(("arbitrary",), 32),
    )(qidx, *pieces, *recvs, w, m, v, *token)
    _ORDER[:] = [res[1]]
    return res


def _sum8(name, parts):
    def kern(p_ref, o_ref):
        acc = p_ref[0]
        for p in range(1, N_DEV):
            acc = acc + p_ref[p]
        o_ref[...] = acc

    return pl.pallas_call(
        kern, name=name, out_shape=jax.ShapeDtypeStruct(parts.shape[1:], F32),
        compiler_params=pltpu.CompilerParams(vmem_limit_bytes=32 << 20),
    )(parts)


def _adam_small(name, w, g, m, v):
    def kern(w_ref, g_ref, m_ref, v_ref, d_out, m_out, v_out):
        delta, m2, v2 = _adam(w_ref[...], g_ref[...], m_ref[...], v_ref[...])
        d_out[...] = delta
        m_out[...] = m2
        v_out[...] = v2

    sh = jax.ShapeDtypeStruct(w.shape, F32)
    return pl.pallas_call(kern, name=name, out_shape=[sh] * 3)(w, g, m, v)


class _NoComm:
    def __init__(self):
        self.grads = {}

    def settled_before(self, kernel_name):
        return []

    def after(self, kernel_name):
        pass

    def grad(self, name, g):
        self.grads[name] = g


def _local_step(x, target, wts, vec, comm=None):
    comm = comm or _NoComm()
    _ORDER[:] = []
    seq, d = x.shape
    sv = vec["sv"]
    ff = vec["b_ff1"].shape[1]
    n_in = 5 * d

    def issue(name, call):
        _ORDER.extend(comm.settled_before(name))
        res = call()
        comm.after(name)
        return res

    tc = min(512, seq)
    t1k, t512, t256 = min(1024, seq), min(512, seq), min(256, seq)
    n512 = min(512, d)
    tkd = d

    x_bf = x.astype(BF16)
    full = lambda i, j, k: (0, 0)

    def epi_store(acc, i, j, ex, outs):
        outs[0][...] = acc

    n_pass = W_IN_PASSES
    piece = n_in // (N_DEV * n_pass)
    tz = min(2048, seq)
    z = None
    for k in range(n_pass):
        w_piece = wts["w_in_piece_%d" % k]
        prev = [] if z is None else [(z, (None, None))]
        (z,) = issue("z_proj_%d" % k, lambda: _mm(
            "z_proj_%d" % k, "nn", (seq // tz, N_DEV, 1),
            x_bf, ((tz, d), lambda i, j, kk: (i, 0)), w_piece, ((d, piece), lambda i, j, kk: (0, j)),
            prev, [jax.ShapeDtypeStruct((seq, n_in), F32)],
            [((tz, piece), lambda i, j, kk, k=k: (i, n_pass * j + k))], epi_store,
            aliases={2: 0} if prev else None))

    pool_w, wa, wx = wts["pool_w"], wts["lru_wa"], wts["lru_wx"]
    blk = wa.shape[2]
    y_pool = issue("pool_fwd", lambda: _pool_fwd(z, pool_w, vec["pool_scale"], seq, d, tc))
    xc, h_f = issue("lru_fwd", lambda: _lru_fwd(z, sv, vec["conv_b"], wa, wx, seq, d, tc))
    h_b, y_lru = issue("lru_rev", lambda: _lru_rev(z, sv, wa, wx, xc, h_f, seq, d, tc))
    w_pu, w_lu = wts["w_pool_up"], wts["w_lru_up"]
    m_bf, p_a, p_b = issue("merge", lambda: _merge(y_pool, w_pu, y_lru, w_lu, z, seq, d, t1k, n512))
    w_out = wts["w_out"]

    def epi_ln1(acc, i, j, ex, outs):
        x_ref, bo, g1, b1 = ex
        s1 = DN_ALPHA * x_ref[...] + (acc + bo[...])
        xhat, rstd, x1 = _ln_fwd(s1, g1[...], b1[...])
        outs[0][...] = xhat
        outs[1][...] = x1.astype(BF16)
        outs[2][...] = rstd

    rowd = lambda t: ((t, d), lambda i, j, k: (i, 0))
    vecd = ((1, d), full)
    xhat1, x1_bf, rstd1 = issue("out_ln1", lambda: _mm(
        "out_ln1", "nn", (seq // t256, 1, 1), m_bf, rowd(t256), w_out, ((d, d), full),
        [(x, rowd(t256)), (vec["b_out"], vecd), (vec["ln1_g"], vecd), (vec["ln1_b"], vecd)],
        [jax.ShapeDtypeStruct((seq, d), F32), jax.ShapeDtypeStruct((seq, d), BF16),
         jax.ShapeDtypeStruct((seq, 1), F32)],
        [rowd(t256), rowd(t256), ((t256, 1), lambda i, j, k: (i, 0))], epi_ln1))
    w1 = wts["w_ff1"]

    def epi_ff1(acc, i, j, ex, outs):
        r = jnp.maximum(acc + ex[0][...], 0.0)
        outs[0][...] = r.astype(BF16)
        outs[1][...] = (r * r).astype(BF16)

    nf = min(1024, ff)
    tile_f = ((t1k, nf), lambda i, j, k: (i, j))
    relu_h, hdn = issue("ff1", lambda: _mm(
        "ff1", "nn", (seq // t1k, ff // nf, 1), x1_bf, rowd(t1k), w1, ((d, nf), lambda i, j, k: (0, j)),
        [(vec["b_ff1"], ((1, nf), lambda i, j, k: (0, j)))],
        [jax.ShapeDtypeStruct((seq, ff), BF16)] * 2, [tile_f, tile_f], epi_ff1))
    w2 = wts["w_ff2"]

    def epi_ln2(acc, i, j, ex, outs):
        xh1, tgt, g1, b1, bf2, g2, b2 = ex
        ds_ref, dsb_ref, sm_ref, loss_ref = outs
        x1 = xh1[...] * g1[...] + b1[...]
        s2 = DN_ALPHA * x1 + (acc + bf2[...])
        xhat, rstd, y = _ln_fwd(s2, g2[...], b2[...])
        e = y - tgt[...]
        part = 0.5 * jnp.sum(jnp.mean(e * e, axis=-1, keepdims=True))
        dy = e * (1.0 / d)
        ds2 = _ln_bwd(dy, xhat, rstd, g2[...])
        ds_ref[...] = ds2
        dsb_ref[...] = ds2.astype(BF16)
        sm_ref[0:1, :] += _colsum(dy * xhat)
        sm_ref[1:2, :] += _colsum(dy)
        sm_ref[2:3, :] += _colsum(ds2)
        loss_ref[...] += jnp.full(loss_ref.shape, part, F32)

    def zero_tail(n_tail):
        def init(i, j, outs):
            @pl.when(i == 0)
            def _():
                for o in outs[-n_tail:]:
                    o[...] = jnp.zeros_like(o)
        return init

    def rows_epilogue(name, lead, extras, out_shapes, out_specs, epi, n_tail):
        ne = len(extras)

        def kern(*refs):
            i = pl.program_id(0)
            outs = refs[1 + ne:]
            zero_tail(n_tail)(i, 0, outs)
            epi(refs[0][...], i, 0, refs[1:1 + ne], outs)

        def spec(s):
            bs, f = s
            return pl.BlockSpec(bs, lambda i, f=f: f(i, 0, 0))

        return _pcall(name, kern, (seq // t256,), [spec(rowd(t256))] + [spec(s) for _, s in extras],
                      [spec(s) for s in out_specs], out_shapes, [lead] + [e for e, _ in extras])

    tkw2, tw = min(2048, ff), min(1024, d)
    (ff_out,) = _mm("ff2", "nn", (seq // t1k, d // tw, ff // tkw2), hdn, ((t1k, tkw2), lambda i, j, k: (i, k)),
                    w2, ((tkw2, tw), lambda i, j, k: (k, j)), [], [jax.ShapeDtypeStruct((seq, d), F32)],
                    [((t1k, tw), lambda i, j, k: (i, j))], None)
    ds2, ds2_bf, sm_ln2, loss_blk = issue("ff2_ln2", lambda: rows_epilogue(
        "ln2_loss", ff_out,
        [(xhat1, rowd(t256)), (target, rowd(t256)), (vec["ln1_g"], vecd), (vec["ln1_b"], vecd),
         (vec["b_ff2"], vecd), (vec["ln2_g"], vecd), (vec["ln2_b"], vecd)],
        [jax.ShapeDtypeStruct((seq, d), F32), jax.ShapeDtypeStruct((seq, d), BF16),
         jax.ShapeDtypeStruct((SUBLANES, d), F32), jax.ShapeDtypeStruct((SUBLANES, 128), F32)],
        [rowd(t256), rowd(t256), ((SUBLANES, d), full), ((SUBLANES, 128), full)], epi_ln2, 2))

    tkw = min(2048, seq)

    def dw(name, wname, a, b, m_dim, n_dim, b_spec=None, row0=0):
        _ORDER.extend(comm.settled_before(name))
        tn = min(1024, n_dim)
        tm = next(t for t in (1024, 768, 512, 256, 128) if m_dim % t == 0 and row0 % t == 0)
        b_spec = b_spec or ((tkw, tn), lambda i, j, k: (k, j))
        i0 = row0 // tm
        (out,) = _mm(
            name, "tn", (m_dim // tm, n_dim // tn, seq // tkw),
            a, ((tkw, tm), lambda i, j, k: (k, i0 + i)), b, b_spec, [],
            [jax.ShapeDtypeStruct((m_dim, n_dim), F32)], [((tm, tn), lambda i, j, k: (i, j))],
            epi_store if seq == tkw else None)
        comm.grad(wname, out)
        comm.after(name)

    dw("dw_ff2", "w_ff2", hdn, ds2_bf, ff, d)

    def epi_dpre(acc, i, j, ex, outs):
        dpre = acc * (2.0 * ex[0][...].astype(F32))
        outs[0][...] = dpre.astype(BF16)

        @pl.when(i == 0)
        def _():
            outs[1][...] = jnp.zeros_like(outs[1])

        outs[1][0:1, :] += _colsum(dpre)

    dpre, sm_bff1 = issue("dhdn", lambda: _mm(
        "dhdn", "nt", (seq // t1k, ff // nf, 1), ds2_bf, rowd(t1k), w2, ((nf, d), lambda i, j, k: (j, 0)),
        [(relu_h, tile_f)],
        [jax.ShapeDtypeStruct((seq, ff), BF16), jax.ShapeDtypeStruct((SUBLANES, ff), F32)],
        [tile_f, ((SUBLANES, nf), lambda i, j, k: (0, j))], epi_dpre, order="ji"))

    dw("dw_ff1", "w_ff1", x1_bf, dpre, d, ff)

    def epi_ln1b(acc, i, j, ex, outs):
        ds2_ref, xh1, rs1, g1 = ex
        ds_ref, dsb_ref, sm_ref = outs
        dy1 = acc + DN_ALPHA * ds2_ref[...]
        xhat = xh1[...]
        ds1 = _ln_bwd(dy1, xhat, rs1[...], g1[...])
        ds_ref[...] = ds1
        dsb_ref[...] = ds1.astype(BF16)
        sm_ref[0:1, :] += _colsum(dy1 * xhat)
        sm_ref[1:2, :] += _colsum(dy1)
        sm_ref[2:3, :] += _colsum(ds1)

    (dx1,) = _mm("dx1", "nt", (seq // t1k, d // tw, ff // tkw2), dpre, ((t1k, tkw2), lambda i, j, k: (i, k)),
                 w1, ((tw, tkw2), lambda i, j, k: (j, k)), [], [jax.ShapeDtypeStruct((seq, d), F32)],
                 [((t1k, tw), lambda i, j, k: (i, j))], None)
    ds1, ds1_bf, sm_ln1 = issue("dx1_ln1", lambda: rows_epilogue(
        "ln1_bwd", dx1,
        [(ds2, rowd(t256)), (xhat1, rowd(t256)), (rstd1, ((t256, 1), lambda i, j, k: (i, 0))),
         (vec["ln1_g"], vecd)],
        [jax.ShapeDtypeStruct((seq, d), F32), jax.ShapeDtypeStruct((seq, d), BF16),
         jax.ShapeDtypeStruct((SUBLANES, d), F32)],
        [rowd(t256), rowd(t256), ((SUBLANES, d), full)], epi_ln1b, 1))

    dw("dw_out", "w_out", m_bf, ds1_bf, d, d)
    nd = min(1024, d)
    n_n = d // nd
    tile_d = ((t512, nd), lambda i, j, k: (i, j))

    def epi_dm(acc, i, j, ex, outs):
        la, lb, pa, pb = ex
        ga, gb = _sig(la[...]), _sig(lb[...])
        outs[0][...] = (acc * ga).astype(BF16)
        outs[1][...] = (acc * gb).astype(BF16)
        outs[2][0] = (acc * pa[...].astype(F32) * ga * (1.0 - ga)).astype(BF16)
        outs[2][1] = (acc * pb[...].astype(F32) * gb * (1.0 - gb)).astype(BF16)

    dp_a, dp_b, dz = _mm(
        "dm", "nt", (seq // t512, n_n, 1), ds1_bf, rowd(t512), w_out, ((nd, d), lambda i, j, k: (j, 0)),
        [(z, ((t512, nd), lambda i, j, k: (i, 3 * n_n + j))),
         (z, ((t512, nd), lambda i, j, k: (i, 4 * n_n + j))), (p_a, tile_d), (p_b, tile_d)],
        [jax.ShapeDtypeStruct((seq, d), BF16), jax.ShapeDtypeStruct((seq, d), BF16),
         jax.ShapeDtypeStruct((5, seq, d), BF16)],
        [tile_d, tile_d, ((2, t512, nd), lambda i, j, k: (0, i, j))], epi_dm)

    dw("dw_pool_up", "w_pool_up", y_pool, dp_a, d, d)
    dw("dw_lru_up", "w_lru_up", y_lru, dp_b, d, d)

    def epi_bf(acc, i, j, ex, outs):
        outs[0][...] = acc.astype(BF16)

    (dy_pool,) = issue("dy_pool", lambda: _mm(
        "dy_pool", "nt", (seq // t512, n_n, 1), dp_a, rowd(t512), w_pu, ((nd, d), lambda i, j, k: (j, 0)), [],
        [jax.ShapeDtypeStruct((seq, d), BF16)], [tile_d], epi_bf))

    def epi_dylru(acc, i, j, ex, outs):
        hf, hb, ug, _ = ex
        u = ug[...]
        outs[0][...] = acc * _gelu(u)
        outs[1][...] = (acc * (hf[...] + hb[...]) * _gelu_grad(u)).astype(BF16)

    dz_in = dz
    dh, dz = issue("dy_lru", lambda: _mm(
        "dy_lru", "nt", (seq // t512, n_n, 1), dp_b, rowd(t512), w_lu, ((nd, d), lambda i, j, k: (j, 0)),
        [(h_f, tile_d), (h_b, tile_d), (z, ((t512, nd), lambda i, j, k: (i, 2 * n_n + j))),
         (dz_in, (None, None))],
        [jax.ShapeDtypeStruct((seq, d), F32), jax.ShapeDtypeStruct(dz_in.shape, BF16)],
        [tile_d, ((None, t512, nd), lambda i, j, k: (4, i, j))], epi_dylru, aliases={5: 1}))

    dz, g_pw, sm_pool = _pool_bwd(z, dy_pool, pool_w, vec["pool_scale"], dz, seq, d, tc)
    comm.grad("pool_w", g_pw)
    dxc0, g_wa0, g_wx0, sm_l0 = issue("lru_bwd_0", lambda: _lru_bwd(
        0, xc, dh, h_f, sv, wa, wx, None, seq, d, tc))
    dxc, g_wa1, g_wx1, sm_l1 = issue("lru_bwd_1", lambda: _lru_bwd(
        1, xc, dh, h_b, sv, wa, wx, dxc0, seq, d, tc))
    comm.grad("lru_wa", jnp.concatenate([g_wa0, g_wa1], axis=0))
    comm.grad("lru_wx", jnp.concatenate([g_wx0, g_wx1], axis=0))
    dz, sm_conv = _conv_bwd(z, dxc, sv, dz, seq, d, tc, blk)

    tnw = min(1024, d)
    per_seg = d // tnw
    seg_spec = ((None, tkw, tnw), lambda i, j, k: ((j // per_seg + 2) % 5, k, j % per_seg))
    lo_rows = 3 * d // 4
    dw("dw_in_lo", "w_in_lo", x_bf, dz, lo_rows, n_in, b_spec=seg_spec)
    dw("dw_in_hi", "w_in_hi", x_bf, dz, d - lo_rows, n_in, b_spec=seg_spec, row0=lo_rows)

    nk = d // tkd

    def epi_dx(acc, i, j, ex, outs):
        outs[0][...] = acc + DN_ALPHA * ex[0][...]

    (grad_x,) = issue("dx", lambda: _mm(
        "dx", "nt", (seq // t512, 1, n_in // tkd), dz,
        ((None, t512, tkd), lambda i, j, k: ((k // nk + 2) % 5, i, k % nk)),
        wts["w_in"], ((d, tkd), lambda i, j, k: (0, k)), [(ds1, rowd(t512))],
        [jax.ShapeDtypeStruct((seq, d), F32)], [rowd(t512)], epi_dx, acc_shape=(t512, d)))

    small = {"ln2": sm_ln2, "b_ff1": sm_bff1, "ln1": sm_ln1, "pool": sm_pool, "lru0": sm_l0, "lru1": sm_l1,
             "conv": sm_conv}
    return loss_blk[0, 0], grad_x, small


REP = ("pool_scale", "conv_b", "b_out", "ln1_g", "ln1_b", "b_ff2", "ln2_g", "ln2_b")
SHARDED_SMALL = (("conv_w", 4), ("lru_ba", 2), ("lru_bx", 2), ("lru_lambda", 2))
WEIGHT_ORDER = ("w_in", "pool_w", "pool_scale", "conv_w", "conv_b", "lru_wa", "lru_ba", "lru_wx", "lru_bx",
                "lru_lambda", "w_pool_up", "w_lru_up", "w_out", "b_out", "ln1_g", "ln1_b", "w_ff1", "b_ff1",
                "w_ff2", "b_ff2", "ln2_g", "ln2_b")


def _pad_rows(a, rows):
    return jnp.concatenate([a, jnp.zeros((rows - a.shape[0], a.shape[1]), a.dtype)], axis=0)


def kernel(x, w_in, pool_w, pool_scale, conv_w, conv_b, lru_wa, lru_ba, lru_wx, lru_bx, lru_lambda, w_pool_up, w_lru_up, w_out, b_out, ln1_g, ln1_b, w_ff1, b_ff1, w_ff2, b_ff2, ln2_g, ln2_b, loss_target, m_w_in, m_pool_w, m_pool_scale, m_conv_w, m_conv_b, m_lru_wa, m_lru_ba, m_lru_wx, m_lru_bx, m_lru_lambda, m_w_pool_up, m_w_lru_up, m_w_out, m_b_out, m_ln1_g, m_ln1_b, m_w_ff1, m_b_ff1, m_w_ff2, m_b_ff2, m_ln2_g, m_ln2_b, v_w_in, v_pool_w, v_pool_scale, v_conv_w, v_conv_b, v_lru_wa, v_lru_ba, v_lru_wx, v_lru_bx, v_lru_lambda, v_w_pool_up, v_w_lru_up, v_w_out, v_b_out, v_ln1_g, v_ln1_b, v_w_ff1, v_b_ff1, v_w_ff2, v_b_ff2, v_ln2_g, v_ln2_b):
    args = dict(locals())
    w = {n: args[n] for n in WEIGHT_ORDER}
    mom = {n: args["m_" + n] for n in WEIGHT_ORDER}
    var = {n: args["v_" + n] for n in WEIGHT_ORDER}
    seq, d = x.shape[1], x.shape[2]
    n_heads, blk = lru_wa.shape[2], lru_wa.shape[4]
    n_groups = pool_w.shape[1]
    ff = b_ff1.shape[1]
    cx, cy, cc = _coords()
    me = 4 * cx + 2 * cy + cc
    cidx = jnp.reshape(cc, (1,)).astype(jnp.int32)
    qidx = jnp.reshape(2 * cx + cy, (1,)).astype(jnp.int32)

    fam_of = dict(BIG)
    sviews = {n: _shard_view(w[n], fam) for n, fam in BIG}
    size_of = {n: sviews[n].shape[1] for n, _ in BIG}
    for piece_name in ("w_in_lo", "w_in_hi"):
        fam_of[piece_name], size_of[piece_name] = fam_of["w_in"], size_of["w_in"]
    wts = {}

    def take_gathered(names, arrays):
        for n, g in zip(names, arrays):
            if n == "pool_w":
                g = g.reshape(n_groups, d // n_groups, d // n_groups)
            elif n in ("lru_wa", "lru_wx"):
                g = g.reshape(2, n_heads, blk, blk)
            elif fam_of[n] == "row":
                g = g.reshape(g.shape[1:])
            wts[n] = g

    shard_bf = {n: sviews[n].astype(BF16) for n, _ in BIG}
    piece = sviews["w_in"].shape[1] // W_IN_PASSES
    for k in range(W_IN_PASSES):
        name = "w_in_piece_%d" % k
        shard_bf[name] = shard_bf["w_in"][:, k * piece:(k + 1) * piece]
        fam_of[name] = "col"

    def gather_job(names, extra=()):
        return _ag_job([shard_bf[n] for n in names] + [e for e, _ in extra],
                       [fam_of[n] for n in names] + [f for _, f in extra])

    launched = [0]

    def on_sequencer(kind, job):
        launched[0] += 1
        return _sequencer_job("sq_%s_%d" % (kind, launched[0]), job, launched[0] % 2)

    class Plan:
        sibling = {"dw_ff2": ("w_ff2",), "dw_ff1": ("w_ff1",), "dw_lru_up": ("w_out", "w_pool_up", "w_lru_up"),
                   "dw_in_lo": ("w_in_lo",), "dw_in_hi": ("w_in_hi", "pool_w", "lru_wa", "lru_wx")}
        chips = {"dw_ff1": ("w_ff2",), "dw_out": ("w_ff1",), "dy_lru": ("w_out", "w_pool_up", "w_lru_up"),
                 "dw_in_hi": ("w_in_lo",), "dx": ("w_in_hi", "pool_w", "lru_wa", "lru_wx")}

        def __init__(self):
            self.grads, self.recv_a, self.parts, self.recv_b = {}, {}, {}, {}

        def grad(self, name, g):
            self.grads[name] = g if fam_of[name] == "col" else g.reshape((-1,) + g.shape[-2:])

        settle = {"dw_ff1": ("w_in",), "dw_out": ("w_ff2",), "dy_pool": ("w_ff1",), "lru_bwd_1": ("w_out",)}
        settle_add = {"dx": ("w_in_lo",)}

        def settled_before(self, host):
            return [wts[n] if n == "w_in" else self.recv_b[n] for n in self.settle.get(host, ())]

        def after(self, host):
            if host in self.chips:
                names = self.chips[host]
                _ORDER.extend(self.recv_b[n] for n in self.settle_add.get(host, ()))
                for n in names:
                    self.parts[n] = _rs_add("rs_add_" + n, self.grads[n], self.recv_a[n], fam_of[n], size_of[n], cidx)
                res = on_sequencer("chips", _rs_chips_job([self.parts[n] for n in names]))
                self.recv_b.update(zip(names, res))
            if host in self.sibling:
                names = self.sibling[host]
                res = on_sequencer("sibling", _rs_sibling_job(
                    [self.grads[n] for n in names], [fam_of[n] for n in names], [size_of[n] for n in names]))
                self.recv_a.update(zip(names, res))

    first = ("w_in_piece_0",)
    sv_shard = _pad_rows(jnp.concatenate([w[n].reshape(r, -1) for n, r in SHARDED_SMALL], axis=0), 16)
    gathered = on_sequencer("gather", gather_job(first, [(sv_shard, "col")]))
    take_gathered(first, gathered[:-1])
    vec = {n: w[n] for n in REP}
    vec["b_ff1"] = b_ff1
    vec["sv"] = gathered[-1]
    queue = [("w_in_piece_%d" % k,) for k in range(1, W_IN_PASSES)]
    queue += [("pool_w", "lru_wa", "lru_wx"), ("w_pool_up", "w_lru_up"), ("w_out",), ("w_ff1",), ("w_ff2",),
              ("w_in",)]
    for names in queue:
        take_gathered(names, on_sequencer("gather", gather_job(names)))

    plan = Plan()
    loss_part, grad_x, small = _local_step(x.reshape(seq, d), loss_target.reshape(seq, d), wts, vec, plan)
    loss = lax.psum(loss_part, AXES)

    out_g, out_d, out_m, out_v = {}, {}, {}, {}
    for n, fam in sorted(BIG, key=lambda nf: nf[0] in ("w_in", "pool_w", "lru_wa", "lru_wx")):
        halves = [n + "_lo", n + "_hi"] if n == "w_in" else [n]
        res = _rs_final_adam("adam_" + n, [plan.parts[h] for h in halves], [plan.recv_b[h] for h in halves],
                             sviews[n], _shard_view(mom[n], fam), _shard_view(var[n], fam), fam, qidx)
        out_g[n], out_d[n], out_m[n], out_v[n] = [r.reshape(w[n].shape) for r in res]

    rows = [small["pool"][0:1], small["conv"][4:5], small["ln1"][2:3], small["ln1"][0:1], small["ln1"][1:2],
            small["ln2"][2:3], small["ln2"][0:1], small["ln2"][1:2], small["b_ff1"][0:1].reshape(ff // d, d),
            small["conv"][0:4], small["lru0"][0:1], small["lru1"][0:1], small["lru0"][1:2], small["lru1"][1:2],
            small["lru0"][2:3], small["lru1"][2:3]]
    n_rep = len(REP) + ff // d
    n_rows = n_rep + sum(r for _, r in SHARDED_SMALL)
    pad_rows = -(-n_rows // SUBLANES) * SUBLANES
    packed = _pad_rows(jnp.concatenate(rows, axis=0), pad_rows)
    (all_small,) = _run_job("ag_small", _ag_job([packed], ["lead"]))
    g_small = _sum8("sum_small", all_small)

    def pack_rep(t):
        return jnp.concatenate([t[n] for n in REP] + [t["b_ff1"].reshape(ff // d, d)], axis=0)

    def pack_sh(t):
        return jnp.concatenate([t[n].reshape(r, -1) for n, r in SHARDED_SMALL], axis=0)

    g_rep = g_small[:n_rep]
    cs = d // N_DEV
    g_sh = lax.dynamic_slice_in_dim(g_small[n_rep:n_rows], me * cs, cs, axis=1)
    d_rep, m_rep, v_rep = _adam_small("adam_rep", pack_rep(w), g_rep, pack_rep(mom), pack_rep(var))
    d_sh, m_sh, v_sh = _adam_small("adam_sharded", pack_sh(w), g_sh, pack_sh(mom), pack_sh(var))

    def unpack(rep_t, sh_t, dst):
        for i, n in enumerate(REP):
            dst[n] = rep_t[i:i + 1].reshape(w[n].shape)
        dst["b_ff1"] = rep_t[len(REP):n_rep].reshape(w["b_ff1"].shape)
        r0 = 0
        for n, r in SHARDED_SMALL:
            dst[n] = sh_t[r0:r0 + r].reshape(w[n].shape)
            r0 += r

    unpack(g_rep, g_sh, out_g)
    unpack(d_rep, d_sh, out_d)
    unpack(m_rep, m_sh, out_m)
    unpack(v_rep, v_sh, out_v)

    _ORDER[:] = []
    outs = [loss, grad_x.reshape(x.shape)]
    for t in (out_g, out_d, out_m, out_v):
        outs += [t[n] for n in WEIGHT_ORDER]
    return tuple(outs)
```

```python
import jax
import jax.numpy as jnp
from jax import lax
from jax.experimental import pallas as pl
from jax.experimental.pallas import tpu as pltpu
from jax.experimental.pallas import tpu_sc as plsc

F32 = jnp.float32
BF16 = jnp.bfloat16
MESH = pl.DeviceIdType.MESH
N_DEV = 8

DN_ALPHA = 2.0 ** 0.25
LN_EPS = 1e-5
LRU_C = 8.0
ADAM_LR = 0.001
ADAM_B1 = 0.9
ADAM_B2 = 0.999
ADAM_EPS = 1e-08
ADAM_WD = 0.01
ADAM_STEP = 10
GELU_C = 0.7978845608028654
GELU_K = 0.044715

W_IN_PASSES = 5
HALO = 16
SUBLANES = 8
VMEM_MB = 56


def _cparams(sem, vmem_mb=VMEM_MB):
    return pltpu.CompilerParams(dimension_semantics=sem, vmem_limit_bytes=vmem_mb << 20)


HBM_SPEC = pl.BlockSpec(memory_space=pl.ANY)


class _Job:
    def __init__(self, ins, outs, sems, start, finish, mid=None):
        self.ins, self.outs, self.sems = list(ins), list(outs), list(sems)
        self.start, self.finish, self.mid = start, finish, mid
        self.results = None


_ORDER = []


def _pcall(name, body, grid, in_specs, out_specs, out_shape, inputs, scratch=(), aliases=None,
           vmem_mb=VMEM_MB):
    token = [t for t in _ORDER if not any(t is a for a in inputs)]
    n_in = len(inputs)

    def ordered(*refs):
        return body(*refs[:n_in], *refs[n_in + len(token):])

    res = pl.pallas_call(ordered, name=name, grid=grid, in_specs=list(in_specs) + [HBM_SPEC] * len(token),
                         out_specs=list(out_specs), out_shape=list(out_shape), scratch_shapes=list(scratch),
                         input_output_aliases=aliases or {},
                         compiler_params=_cparams(("arbitrary",) * len(grid), vmem_mb))(*inputs, *token)
    _ORDER[:] = [res[0]]
    return res


def _sequencer_job(name, job, collective_id):
    ji, jo = len(job.ins), len(job.outs)

    def body(*refs):
        jins, jouts, sems = refs[:ji], refs[ji:ji + jo], refs[ji + jo:]
        barrier = pltpu.get_barrier_semaphore()
        x, y, c = lax.axis_index("x"), lax.axis_index("y"), lax.axis_index("c")
        for r in range(1, N_DEV):
            peer = ((1 - x) if r & 4 else x, (1 - y) if r & 2 else y, (1 - c) if r & 1 else c)
            pl.semaphore_signal(barrier, inc=1, device_id=peer, device_id_type=MESH)
        pl.semaphore_wait(barrier, N_DEV - 1)
        job.start(jins, jouts, sems)
        if job.mid is not None:
            job.mid(jins, jouts, sems)
        job.finish(jins, jouts, sems)

    res = pl.kernel(
        body, name=name, out_type=job.outs, mesh=plsc.ScalarSubcoreMesh(axis_name="sequencer", num_cores=1),
        scratch_types=job.sems, compiler_params=pltpu.CompilerParams(collective_id=collective_id),
    )(*job.ins)
    job.results = list(res)
    return job.results


def _dot(mode, a, b):
    if mode == "nn":
        dims = (((1,), (0,)), ((), ()))
    elif mode == "nt":
        dims = (((1,), (1,)), ((), ()))
    else:
        dims = (((0,), (0,)), ((), ()))
    return lax.dot_general(a, b, dims, preferred_element_type=F32)


def _sig(x):
    return 0.5 * jnp.tanh(0.5 * x) + 0.5


def _gelu(x):
    t = jnp.tanh(GELU_C * (x + GELU_K * x * x * x))
    return 0.5 * x * (1.0 + t)


def _gelu_grad(x):
    x2 = x * x
    t = jnp.tanh(GELU_C * (x + GELU_K * x * x2))
    return 0.5 * (1.0 + t) + 0.5 * x * (1.0 - t * t) * GELU_C * (1.0 + 3.0 * GELU_K * x2)


def _colsum(v):
    return jnp.sum(v, axis=0, keepdims=True)


def _mm(name, mode, grid, a, a_spec, b, b_spec, extras, out_shapes, out_specs, epi, *,
        order="ij", acc_shape=None, aliases=None, vmem_mb=VMEM_MB):
    gm, gn, gk = grid

    def spec(s):
        bs, f = s
        if f is None:
            return pl.BlockSpec(memory_space=pl.ANY)
        if order == "ij":
            return pl.BlockSpec(bs, lambda i, j, k, f=f: f(i, j, k))
        return pl.BlockSpec(bs, lambda j, i, k, f=f: f(i, j, k))

    ne, no = len(extras), len(out_shapes)

    def kern(*refs):
        a_ref, b_ref = refs[0], refs[1]
        ex = refs[2:2 + ne]
        outs = refs[2 + ne:2 + ne + no]
        if order == "ij":
            i, j = pl.program_id(0), pl.program_id(1)
        else:
            j, i = pl.program_id(0), pl.program_id(1)
        k = pl.program_id(2)
        prod = _dot(mode, a_ref[...], b_ref[...])
        if gk == 1:
            epi(prod, i, j, ex, outs)
        elif epi is None:
            @pl.when(k == 0)
            def _():
                outs[0][...] = prod

            @pl.when(k > 0)
            def _():
                outs[0][...] += prod
        else:
            acc = refs[-1]

            @pl.when(k == 0)
            def _():
                acc[...] = prod

            @pl.when(k > 0)
            def _():
                acc[...] += prod

            @pl.when(k == gk - 1)
            def _():
                epi(acc[...], i, j, ex, outs)

    g = (gm, gn, gk) if order == "ij" else (gn, gm, gk)
    scratch = [pltpu.VMEM(acc_shape, F32)] if gk > 1 and epi is not None else []
    return _pcall(name, kern, g, [spec(a_spec), spec(b_spec)] + [spec(s) for _, s in extras],
                  [spec(s) for s in out_specs], out_shapes, [a, b] + [e for e, _ in extras],
                  scratch=scratch, aliases=aliases, vmem_mb=vmem_mb)


def _ext(ref, c, n_chunks, tc, seq):
    c0 = pl.multiple_of(c * tc, tc)
    body = ref[pl.ds(c0, tc), :].astype(F32)
    t0 = pl.multiple_of(jnp.maximum(c0 - HALO, 0), HALO)
    b0 = pl.multiple_of(jnp.minimum(c0 + tc, seq - HALO), HALO)
    top = ref[pl.ds(t0, HALO), :].astype(F32)
    bot = ref[pl.ds(b0, HALO), :].astype(F32)
    top = jnp.where(c > 0, top, 0.0)
    bot = jnp.where(c < n_chunks - 1, bot, 0.0)
    return jnp.concatenate([top, body, bot], axis=0)


def _shifted(vext, off, tc):
    n = vext.shape[0]
    r = vext if off == 0 else pltpu.roll(vext, (n - off) % n, 0)
    return r[HALO:HALO + tc]


def _win_sum(vext, g, extra, tc):
    s2 = vext + pltpu.roll(vext, 1, 0)
    s4 = s2 + pltpu.roll(s2, 2, 0)
    s8 = s4 + pltpu.roll(s4, 4, 0)
    s16 = s8 + pltpu.roll(s8, 8, 0)
    outs = [_shifted(s, extra + hw - 1, tc) for s, hw in ((s2, 1), (s4, 2), (s8, 4), (s16, 8))]
    return jnp.where(g == 0, outs[0], jnp.where(g == 1, outs[1], jnp.where(g == 2, outs[2], outs[3])))


def _win_cnt(t, hw, seq):
    return (jnp.minimum(t + hw, seq) - jnp.maximum(t - hw, 0)).astype(F32)


def _pool_d(uext, g, c, tc, seq):
    hw = jnp.left_shift(1, g)
    t = c * tc + lax.broadcasted_iota(jnp.int32, (tc, 1), 0)
    ws = _win_sum(uext, g, 0, tc)
    return ws / _win_cnt(t, hw, seq) - uext[HALO:HALO + tc]


def _scan_tiles(a_ref, b_ref, h_ref, carry_ref, n_tiles, reverse):
    blk = a_ref.shape[1]
    row = lax.broadcasted_iota(jnp.int32, (SUBLANES, blk), 0)

    def tile(j, hc):
        jj = (n_tiles - 1 - j) if reverse else j
        off = pl.multiple_of(jj * SUBLANES, SUBLANES)
        a = a_ref[pl.ds(off, SUBLANES), :]
        b = b_ref[pl.ds(off, SUBLANES), :]
        for kk in (1, 2, 4):
            sh = (SUBLANES - kk) if reverse else kk
            a_s = pltpu.roll(a, sh, 0)
            b_s = pltpu.roll(b, sh, 0)
            m = (row < SUBLANES - kk) if reverse else (row >= kk)
            a_s = jnp.where(m, a_s, 1.0)
            b_s = jnp.where(m, b_s, 0.0)
            b = a * b_s + b
            a = a * a_s
        h = a * hc + b
        h_ref[pl.ds(off, SUBLANES), :] = h
        return h[0:1, :] if reverse else h[SUBLANES - 1:SUBLANES, :]

    group = SUBLANES if n_tiles % SUBLANES == 0 else 1

    def tiles(jg, hc):
        for u in range(group):
            hc = tile(jg * group + u, hc)
        return hc

    hc = lax.fori_loop(0, n_tiles // group, tiles, carry_ref[0:1, :])
    carry_ref[0:1, :] = hc


def _lru_k(lam):
    y = -lam
    e = jnp.exp(-jnp.abs(y))
    u = 1.0 + e
    l1p = jnp.where(u == 1.0, e, jnp.log(u) * (e / (u - 1.0)))
    return -LRU_C * (jnp.maximum(y, 0.0) + l1p)


def _lru_gates(xc, wa, wx, ba, bx, lam):
    xb = xc.astype(BF16)
    r = _sig(jnp.dot(xb, wa, preferred_element_type=F32) + ba)
    i = _sig(jnp.dot(xb, wx, preferred_element_type=F32) + bx)
    k = _lru_k(lam)
    la = k * r
    a = jnp.exp(la)
    s = jnp.sqrt(-jnp.tanh(la) * (a * a + 1.0))
    return r, i, k, a, s


SV_CONV, SV_BA, SV_BX, SV_LAM = 0, 4, 6, 8


def _pool_fwd(z, pw, scale, seq, d, tc):
    n_g = pw.shape[0]
    pg = d // n_g
    n_chunks = seq // tc

    def kern(z_ref, pw_ref, sc_ref, y_ref):
        g, c = pl.program_id(0), pl.program_id(1)
        uext = _ext(z_ref, c, n_chunks, tc, seq)
        dd = _pool_d(uext, g, c, tc, seq)
        q = jnp.dot(dd.astype(BF16), pw_ref[...], preferred_element_type=F32)
        y_ref[...] = (q * sc_ref[...]).astype(BF16)

    (y,) = _pcall(
        "pool_fwd", kern, (n_g, n_chunks),
        [pl.BlockSpec((seq, pg), lambda g, c: (0, g)),
         pl.BlockSpec((None, pg, pg), lambda g, c: (g, 0, 0)),
         pl.BlockSpec((1, pg), lambda g, c: (0, g))],
        [pl.BlockSpec((tc, pg), lambda g, c: (c, g))],
        [jax.ShapeDtypeStruct((seq, d), BF16)], [z, pw, scale])
    return y


def _lru_fwd(z, sv, conv_b, wa, wx, seq, d, tc):
    n_h, blk = wa.shape[1], wa.shape[2]
    n_chunks = seq // tc
    lru_off = d // blk

    def kern(z_ref, sv_ref, cb_ref, wa_ref, wx_ref, xc_ref, h_ref, a_s, b_s, carry):
        c = pl.program_id(1)
        uext = _ext(z_ref, c, n_chunks, tc, seq)
        xc = cb_ref[...]
        for k in range(4):
            xc = xc + _shifted(uext, k - 2, tc) * sv_ref[SV_CONV + k:SV_CONV + k + 1, :]
        xc_ref[...] = xc
        _, i, _, a, s = _lru_gates(xc, wa_ref[...], wx_ref[...], sv_ref[SV_BA:SV_BA + 1, :],
                                   sv_ref[SV_BX:SV_BX + 1, :], sv_ref[SV_LAM:SV_LAM + 1, :])
        a_s[...] = a
        b_s[...] = s * (i * xc)

        @pl.when(c == 0)
        def _():
            carry[...] = jnp.zeros_like(carry)

        _scan_tiles(a_s, b_s, h_ref, carry, tc // SUBLANES, False)

    col = lambda h, c: (c, h)
    return _pcall(
        "lru_fwd", kern, (n_h, n_chunks),
        [pl.BlockSpec((seq, blk), lambda h, c: (0, lru_off + h)),
         pl.BlockSpec((16, blk), lambda h, c: (0, h)),
         pl.BlockSpec((1, blk), lambda h, c: (0, h)),
         pl.BlockSpec((None, None, blk, blk), lambda h, c: (0, h, 0, 0)),
         pl.BlockSpec((None, None, blk, blk), lambda h, c: (0, h, 0, 0))],
        [pl.BlockSpec((tc, blk), col), pl.BlockSpec((tc, blk), col)],
        [jax.ShapeDtypeStruct((seq, d), F32), jax.ShapeDtypeStruct((seq, d), F32)],
        [z, sv, conv_b, wa, wx],
        scratch=[pltpu.VMEM((tc, blk), F32), pltpu.VMEM((tc, blk), F32), pltpu.VMEM((SUBLANES, blk), F32)])


def _lru_rev(z, sv, wa, wx, xc, h_f, seq, d, tc):
    n_h, blk = wa.shape[1], wa.shape[2]
    n_chunks = seq // tc
    gate_off = 2 * d // blk

    def kern(z_ref, sv_ref, wa_ref, wx_ref, xc_ref, hf_ref, hb_ref, y_ref, a_s, b_s, carry):
        c = pl.program_id(1)
        xc = xc_ref[...]
        _, i, _, a, s = _lru_gates(xc, wa_ref[...], wx_ref[...], sv_ref[SV_BA + 1:SV_BA + 2, :],
                                   sv_ref[SV_BX + 1:SV_BX + 2, :], sv_ref[SV_LAM + 1:SV_LAM + 2, :])
        a_s[...] = a
        b_s[...] = s * (i * xc)

        @pl.when(c == 0)
        def _():
            carry[...] = jnp.zeros_like(carry)

        _scan_tiles(a_s, b_s, hb_ref, carry, tc // SUBLANES, True)
        y_ref[...] = ((hf_ref[...] + hb_ref[...]) * _gelu(z_ref[...])).astype(BF16)

    rev = lambda h, c: (n_chunks - 1 - c, h)
    return _pcall(
        "lru_rev", kern, (n_h, n_chunks),
        [pl.BlockSpec((tc, blk), lambda h, c: (n_chunks - 1 - c, gate_off + h)),
         pl.BlockSpec((16, blk), lambda h, c: (0, h)),
         pl.BlockSpec((None, None, blk, blk), lambda h, c: (1, h, 0, 0)),
         pl.BlockSpec((None, None, blk, blk), lambda h, c: (1, h, 0, 0)),
         pl.BlockSpec((tc, blk), rev), pl.BlockSpec((tc, blk), rev)],
        [pl.BlockSpec((tc, blk), rev), pl.BlockSpec((tc, blk), rev)],
        [jax.ShapeDtypeStruct((seq, d), F32), jax.ShapeDtypeStruct((seq, d), BF16)],
        [z, sv, wa, wx, xc, h_f],
        scratch=[pltpu.VMEM((tc, blk), F32), pltpu.VMEM((tc, blk), F32), pltpu.VMEM((SUBLANES, blk), F32)])


def _merge(y_pool, w_pu, y_lru, w_lu, z, seq, d, tm, tn):
    n_n = d // tn

    def kern(yp_ref, wp_ref, yl_ref, wl_ref, la_ref, lb_ref, m_ref, pa_ref, pb_ref):
        pa = jnp.dot(yp_ref[...], wp_ref[...], preferred_element_type=F32)
        pb = jnp.dot(yl_ref[...], wl_ref[...], preferred_element_type=F32)
        m_ref[...] = (_sig(la_ref[...]) * pa + _sig(lb_ref[...]) * pb).astype(BF16)
        pa_ref[...] = pa.astype(BF16)
        pb_ref[...] = pb.astype(BF16)

    row = pl.BlockSpec((tm, d), lambda i, j: (i, 0))
    wcol = pl.BlockSpec((d, tn), lambda i, j: (0, j))
    out = pl.BlockSpec((tm, tn), lambda i, j: (i, j))
    sh = jax.ShapeDtypeStruct((seq, d), BF16)
    return _pcall(
        "merge", kern, (seq // tm, n_n),
        [row, wcol, row, wcol,
         pl.BlockSpec((tm, tn), lambda i, j: (i, 3 * n_n + j)),
         pl.BlockSpec((tm, tn), lambda i, j: (i, 4 * n_n + j))],
        [out, out, out], [sh, sh, sh], [y_pool, w_pu, y_lru, w_lu, z, z])


def _ln_fwd(s, g, b):
    mu = jnp.mean(s, axis=-1, keepdims=True)
    xc = s - mu
    var = jnp.mean(xc * xc, axis=-1, keepdims=True)
    rstd = lax.rsqrt(var + LN_EPS)
    xhat = xc * rstd
    return xhat, rstd, xhat * g + b


def _ln_bwd(dy, xhat, rstd, g):
    dyg = dy * g
    m1 = jnp.mean(dyg, axis=-1, keepdims=True)
    m2 = jnp.mean(dyg * xhat, axis=-1, keepdims=True)
    return rstd * (dyg - m1 - xhat * m2)


def _pool_bwd(z, dy_pool, pw, scale, dz, seq, d, tc):
    n_g = pw.shape[0]
    pg = d // n_g
    n_chunks = seq // tc

    def kern(z_ref, dy_ref, pw_ref, sc_ref, dz_in, dz_ref, dpw_ref, dsc_ref):
        del dz_in
        g, c = pl.program_id(0), pl.program_id(1)
        hw = jnp.left_shift(1, g)
        uext = _ext(z_ref, c, n_chunks, tc, seq)
        dd = _pool_d(uext, g, c, tc, seq).astype(BF16)
        pwv = pw_ref[...]
        q = jnp.dot(dd, pwv, preferred_element_type=F32)
        dyext = _ext(dy_ref, c, n_chunks, tc, seq)

        @pl.when(c == 0)
        def _():
            dsc_ref[...] = jnp.zeros_like(dsc_ref)
            dpw_ref[...] = jnp.zeros_like(dpw_ref)

        dsc_ref[0:1, :] += _colsum(dyext[HALO:HALO + tc] * q)
        dqext = (dyext * sc_ref[...]).astype(BF16)
        dpw_ref[...] += _dot("tn", dd, dqext[HALO:HALO + tc])
        ddext = _dot("nt", dqext, pwv)
        text = c * tc - HALO + lax.broadcasted_iota(jnp.int32, (tc + 2 * HALO, 1), 0)
        v = ddext / jnp.maximum(_win_cnt(text, hw, seq), 1.0)
        dz_ref[...] = (_win_sum(v, g, 1, tc) - ddext[HALO:HALO + tc]).astype(BF16)

    return _pcall(
        "pool_bwd", kern, (n_g, n_chunks),
        [pl.BlockSpec((seq, pg), lambda g, c: (0, g)),
         pl.BlockSpec((seq, pg), lambda g, c: (0, g)),
         pl.BlockSpec((None, pg, pg), lambda g, c: (g, 0, 0)),
         pl.BlockSpec((1, pg), lambda g, c: (0, g)),
         pl.BlockSpec(memory_space=pl.ANY)],
        [pl.BlockSpec((None, tc, pg), lambda g, c: (2, c, g)),
         pl.BlockSpec((None, pg, pg), lambda g, c: (g, 0, 0)),
         pl.BlockSpec((SUBLANES, pg), lambda g, c: (0, g))],
        [jax.ShapeDtypeStruct(dz.shape, dz.dtype),
         jax.ShapeDtypeStruct((n_g, pg, pg), F32),
         jax.ShapeDtypeStruct((SUBLANES, d), F32)],
        [z, dy_pool, pw, scale, dz], aliases={4: 0})


def _lru_bwd(direction, xc, dh, h_dir, sv, wa, wx, dxc_prev, seq, d, tc):
    reverse = direction == 1
    n_h, blk = wa.shape[1], wa.shape[2]
    n_chunks = seq // tc
    has_prev = dxc_prev is not None

    def kern(*refs):
        xc_ref, dh_ref, h_ref, sv_ref, wa_ref, wx_ref = refs[:6]
        p = 6
        prev_ref = None
        if has_prev:
            prev_ref = refs[p]
            p += 1
        dxc_ref, dwa_ref, dwx_ref, sm_ref, at_s, g_s, carry, acarry = refs[p:p + 8]
        c = pl.program_id(1)
        cr = c if reverse else n_chunks - 1 - c
        c0 = pl.multiple_of(cr * tc, tc)

        @pl.when(c == 0)
        def _():
            carry[...] = jnp.zeros_like(carry)
            acarry[...] = jnp.zeros_like(acarry)
            dwa_ref[...] = jnp.zeros_like(dwa_ref)
            dwx_ref[...] = jnp.zeros_like(dwx_ref)
            sm_ref[...] = jnp.zeros_like(sm_ref)

        xc = xc_ref[...]
        wav, wxv = wa_ref[...], wx_ref[...]
        lam = sv_ref[SV_LAM + direction:SV_LAM + direction + 1, :]
        r, i, k, a, s = _lru_gates(xc, wav, wxv, sv_ref[SV_BA + direction:SV_BA + direction + 1, :],
                                   sv_ref[SV_BX + direction:SV_BX + direction + 1, :], lam)
        rowi = lax.broadcasted_iota(jnp.int32, (tc, blk), 0)
        hbody = h_ref[pl.ds(c0, tc), :]
        if not reverse:
            p0 = pl.multiple_of(jnp.maximum(c0 - SUBLANES, 0), SUBLANES)
            edge = jnp.where(cr > 0, h_ref[pl.ds(p0, SUBLANES), :][SUBLANES - 1:SUBLANES, :], 0.0)
            hprev = jnp.where(rowi == 0, edge, pltpu.roll(hbody, 1, 0))
            at = jnp.where(rowi == tc - 1, acarry[0:1, :], pltpu.roll(a, tc - 1, 0))
        else:
            n0 = pl.multiple_of(jnp.minimum(c0 + tc, seq - SUBLANES), SUBLANES)
            edge = jnp.where(cr < n_chunks - 1, h_ref[pl.ds(n0, SUBLANES), :][0:1, :], 0.0)
            hprev = jnp.where(rowi == tc - 1, edge, pltpu.roll(hbody, tc - 1, 0))
            at = jnp.where(rowi == 0, acarry[0:1, :], pltpu.roll(a, 1, 0))
        at_s[...] = at
        _scan_tiles(at_s, dh_ref, g_s, carry, tc // SUBLANES, not reverse)
        acarry[0:1, :] = a[tc - 1:tc, :] if reverse else a[0:1, :]

        gt = g_s[...]
        da = gt * hprev
        di = gt * s * xc
        dxc = gt * s * i
        ds = gt * (i * xc)
        dl = da * a - ds * (a * a) / s
        dpr = (dl * k) * r * (1.0 - r)
        dpi = di * i * (1.0 - i)
        sm_ref[0:1, :] += _colsum(dpr)
        sm_ref[1:2, :] += _colsum(dpi)
        sm_ref[2:3, :] += _colsum(dl * r) * (LRU_C * _sig(-lam))
        xb, dprb, dpib = xc.astype(BF16), dpr.astype(BF16), dpi.astype(BF16)
        dwa_ref[...] += _dot("tn", xb, dprb)
        dwx_ref[...] += _dot("tn", xb, dpib)
        dxc = dxc + _dot("nt", dprb, wav) + _dot("nt", dpib, wxv)
        if has_prev:
            dxc = dxc + prev_ref[...]
        dxc_ref[...] = dxc

    if reverse:
        chunk = lambda h, c: (c, h)
    else:
        chunk = lambda h, c: (n_chunks - 1 - c, h)
    wspec = pl.BlockSpec((None, None, blk, blk), lambda h, c: (direction, h, 0, 0))
    ins = [xc, dh, h_dir, sv, wa, wx] + ([dxc_prev] if has_prev else [])
    in_specs = [pl.BlockSpec((tc, blk), chunk), pl.BlockSpec((tc, blk), chunk),
                pl.BlockSpec((seq, blk), lambda h, c: (0, h)),
                pl.BlockSpec((16, blk), lambda h, c: (0, h)), wspec, wspec]
    if has_prev:
        in_specs.append(pl.BlockSpec((tc, blk), chunk))
    return _pcall(
        "lru_bwd_%d" % direction, kern, (n_h, n_chunks), in_specs,
        [pl.BlockSpec((tc, blk), chunk),
         pl.BlockSpec((None, blk, blk), lambda h, c: (h, 0, 0)),
         pl.BlockSpec((None, blk, blk), lambda h, c: (h, 0, 0)),
         pl.BlockSpec((SUBLANES, blk), lambda h, c: (0, h))],
        [jax.ShapeDtypeStruct((seq, d), F32),
         jax.ShapeDtypeStruct((n_h, blk, blk), F32),
         jax.ShapeDtypeStruct((n_h, blk, blk), F32),
         jax.ShapeDtypeStruct((SUBLANES, d), F32)],
        ins,
        scratch=[pltpu.VMEM((tc, blk), F32), pltpu.VMEM((tc, blk), F32),
                 pltpu.VMEM((SUBLANES, blk), F32), pltpu.VMEM((SUBLANES, blk), F32)])


def _conv_bwd(z, dxc, sv, dz, seq, d, tc, tcol):
    n_chunks = seq // tc
    lru_off = d // tcol

    def kern(z_ref, dx_ref, sv_ref, dz_in, dz_ref, sm_ref):
        del dz_in
        c = pl.program_id(1)
        uext = _ext(z_ref, c, n_chunks, tc, seq)
        dext = _ext(dx_ref, c, n_chunks, tc, seq)
        dbody = dext[HALO:HALO + tc]

        @pl.when(c == 0)
        def _():
            sm_ref[...] = jnp.zeros_like(sm_ref)

        du = jnp.zeros_like(dbody)
        for k in range(4):
            du = du + _shifted(dext, 2 - k, tc) * sv_ref[SV_CONV + k:SV_CONV + k + 1, :]
            sm_ref[k:k + 1, :] += _colsum(dbody * _shifted(uext, k - 2, tc))
        sm_ref[4:5, :] += _colsum(dbody)
        dz_ref[...] = du.astype(BF16)

    return _pcall(
        "conv_bwd", kern, (d // tcol, n_chunks),
        [pl.BlockSpec((seq, tcol), lambda j, c: (0, lru_off + j)),
         pl.BlockSpec((seq, tcol), lambda j, c: (0, j)),
         pl.BlockSpec((16, tcol), lambda j, c: (0, j)),
         pl.BlockSpec(memory_space=pl.ANY)],
        [pl.BlockSpec((None, tc, tcol), lambda j, c: (3, c, j)),
         pl.BlockSpec((SUBLANES, tcol), lambda j, c: (0, j))],
        [jax.ShapeDtypeStruct(dz.shape, dz.dtype), jax.ShapeDtypeStruct((SUBLANES, d), F32)],
        [z, dxc, sv, dz], aliases={3: 0})


BIG = (("w_in", "col"), ("pool_w", "row"), ("lru_wa", "row"), ("lru_wx", "row"), ("w_pool_up", "row"),
       ("w_lru_up", "row"), ("w_out", "row"), ("w_ff1", "col"), ("w_ff2", "row"))


def _shard_view(w, fam):
    if fam == "col":
        return w.reshape(w.shape[-2:])
    return w.reshape((-1,) + w.shape[-2:])


def _full_shape(sv_shape, fam):
    if fam == "col":
        return (sv_shape[0], N_DEV * sv_shape[1])
    return (sv_shape[0], N_DEV * sv_shape[1], sv_shape[2])


def _slot(ref, fam, p, size):
    if fam == "lead":
        return ref.at[p]
    start = pl.multiple_of(p * size, size)
    if fam == "col":
        return ref.at[:, pl.ds(start, size)]
    return ref.at[:, pl.ds(start, size), :]


def _coords():
    return lax.axis_index("x"), lax.axis_index("y"), lax.axis_index("c")


def _ag_job(shards, fams):
    n = len(shards)
    fulls = []
    for s, fam in zip(shards, fams):
        if fam == "lead":
            fulls.append(jax.ShapeDtypeStruct((N_DEV,) + s.shape, s.dtype))
        else:
            fulls.append(jax.ShapeDtypeStruct(_full_shape(s.shape, fam), s.dtype))
    sizes = [1 if fam == "lead" else s.shape[1] for s, fam in zip(shards, fams)]

    def ctx(ins, outs, sems):
        send, recv, loc = sems
        x, y, c = _coords()
        chips = [(1 - x, y), (x, 1 - y), (1 - x, 1 - y)]

        def copy(a, k, owner, to, src=None):
            dst = _slot(outs[a], fams[a], owner, sizes[a])
            return pltpu.make_async_remote_copy(
                src_ref=dst if src is None else src, dst_ref=dst,
                send_sem=send.at[a, k], recv_sem=recv.at[a, k], device_id=to, device_id_type=MESH)

        def local(a):
            return pltpu.make_async_copy(ins[a], _slot(outs[a], fams[a], 4 * x + 2 * y + c, sizes[a]), loc.at[a])

        return x, y, c, chips, copy, local

    def start(ins, outs, sems):
        x, y, c, chips, copy, local = ctx(ins, outs, sems)
        me = 4 * x + 2 * y + c
        for a in range(n):
            local(a).start()
            copy(a, 0, me, (x, y, 1 - c), ins[a]).start()
            for j, (cx, cy) in enumerate(chips):
                copy(a, 1 + j, me, (cx, cy, c), ins[a]).start()

    def mid(ins, outs, sems):
        x, y, c, chips, copy, _ = ctx(ins, outs, sems)
        for a in range(n):
            for j, (cx, cy) in enumerate(chips):
                owner = 4 * cx + 2 * cy + c
                copy(a, 1 + j, owner, (x, y, c)).wait_recv()
                copy(a, 4 + j, owner, (x, y, 1 - c)).start()

    def finish(ins, outs, sems):
        x, y, c, chips, copy, local = ctx(ins, outs, sems)
        me = 4 * x + 2 * y + c
        for a in range(n):
            copy(a, 0, 4 * x + 2 * y + (1 - c), (x, y, c)).wait_recv()
            for j, (cx, cy) in enumerate(chips):
                copy(a, 4 + j, 4 * cx + 2 * cy + (1 - c), (x, y, c)).wait_recv()
            copy(a, 0, me, (x, y, 1 - c), ins[a]).wait_send()
            for j, (cx, cy) in enumerate(chips):
                copy(a, 1 + j, me, (cx, cy, c), ins[a]).wait_send()
                copy(a, 4 + j, 4 * cx + 2 * cy + c, (x, y, 1 - c)).wait_send()
            local(a).wait()

    sems = [pltpu.SemaphoreType.DMA((n, 7)), pltpu.SemaphoreType.DMA((n, 7)), pltpu.SemaphoreType.DMA((n,))]
    return _Job(shards, fulls, sems, start, finish, mid)


def _rs_sibling_job(fulls, fams, sizes):
    n = len(fulls)
    outs = []
    for f, fam, sz in zip(fulls, fams, sizes):
        if fam == "col":
            outs.append(jax.ShapeDtypeStruct((4, f.shape[0], sz), f.dtype))
        else:
            outs.append(jax.ShapeDtypeStruct((4, f.shape[0], sz, f.shape[2]), f.dtype))

    def copies(ins, rcv, sems):
        send, recv = sems
        x, y, c = _coords()
        return [pltpu.make_async_remote_copy(
            src_ref=_slot(ins[a], fams[a], 2 * q + (1 - c), sizes[a]), dst_ref=rcv[a].at[q],
            send_sem=send.at[a, q], recv_sem=recv.at[a, q], device_id=(x, y, 1 - c), device_id_type=MESH)
            for a in range(n) for q in range(4)]

    def start(ins, rcv, sems):
        for cp in copies(ins, rcv, sems):
            cp.start()

    def finish(ins, rcv, sems):
        for cp in copies(ins, rcv, sems):
            cp.wait()

    return _Job(fulls, outs, [pltpu.SemaphoreType.DMA((n, 4)), pltpu.SemaphoreType.DMA((n, 4))], start, finish)


def _rs_chips_job(parts):
    n = len(parts)
    outs = [jax.ShapeDtypeStruct((3,) + p.shape[1:], p.dtype) for p in parts]

    def copies(ins, rcv, sems):
        send, recv = sems
        x, y, c = _coords()
        cps = []
        for a in range(n):
            for r in (1, 2, 3):
                tx, ty = (1 - x) if r & 2 else x, (1 - y) if r & 1 else y
                cps.append(pltpu.make_async_remote_copy(
                    src_ref=ins[a].at[2 * tx + ty], dst_ref=rcv[a].at[r - 1],
                    send_sem=send.at[a, r - 1], recv_sem=recv.at[a, r - 1],
                    device_id=(tx, ty, c), device_id_type=MESH))
        return cps

    def start(ins, rcv, sems):
        for cp in copies(ins, rcv, sems):
            cp.start()

    def finish(ins, rcv, sems):
        for cp in copies(ins, rcv, sems):
            cp.wait()

    return _Job(parts, outs, [pltpu.SemaphoreType.DMA((n, 3)), pltpu.SemaphoreType.DMA((n, 3))], start, finish)


def _tile_rows(rows, cols):
    tr = rows
    while tr * cols > (1 << 18) and tr % (2 * SUBLANES) == 0:
        tr //= 2
    return tr


def _rs_add(name, full, recv_a, fam, size, cidx):
    if fam == "col":
        rows = full.shape[0]
        tr = _tile_rows(rows, size)
        grid = (4, rows // tr)
        f_spec = pl.BlockSpec((tr, size), lambda q, i, cr: (i, 2 * q + cr[0]))
        s_spec = pl.BlockSpec((None, tr, size), lambda q, i, cr: (q, i, 0))
    else:
        nb, cols = full.shape[0], full.shape[2]
        tr = _tile_rows(nb * size, cols) // nb if nb > 1 else _tile_rows(size, cols)
        nt = size // tr
        grid = (4, nt)
        f_spec = pl.BlockSpec((nb, tr, cols), lambda q, i, cr: (0, (2 * q + cr[0]) * nt + i, 0))
        s_spec = pl.BlockSpec((None, nb, tr, cols), lambda q, i, cr: (q, 0, i, 0))

    token = [t for t in _ORDER if t is not full]

    def kern(c_ref, f_ref, r_ref, *rest):
        del c_ref
        rest[-1][...] = (f_ref[...] + r_ref[...]).astype(BF16)

    out = pl.pallas_call(
        kern, name=name,
        grid_spec=pltpu.PrefetchScalarGridSpec(num_scalar_prefetch=1, grid=grid,
                                               in_specs=[f_spec, s_spec] + [HBM_SPEC] * len(token), out_specs=s_spec),
        out_shape=jax.ShapeDtypeStruct(recv_a.shape, BF16),
        compiler_params=_cparams(("arbitrary", "arbitrary"), 32),
    )(cidx, full, recv_a, *token)
    _ORDER[:] = [out]
    return out


def _adam(w, g, m, v):
    m2 = ADAM_B1 * m + (1.0 - ADAM_B1) * g
    v2 = ADAM_B2 * v + (1.0 - ADAM_B2) * (g * g)
    m_hat = m2 / (1.0 - ADAM_B1 ** ADAM_STEP)
    v_hat = v2 / (1.0 - ADAM_B2 ** ADAM_STEP)
    delta = -ADAM_LR * (m_hat / (jnp.sqrt(v_hat) + ADAM_EPS) + ADAM_WD * w)
    return delta, m2, v2


def _rs_final_adam(name, parts, recv_b, w, m, v, fam, qidx):
    shp = w.shape
    pieces = parts if isinstance(parts, (list, tuple)) else [parts]
    recvs = recv_b if isinstance(recv_b, (list, tuple)) else [recv_b]
    n_p = len(pieces)
    first_blk = [0] * n_p
    if fam == "col":
        rows, cols = shp
        tr = _tile_rows(min(p.shape[1] for p in pieces), cols)
        per = [p.shape[1] // tr for p in pieces]
        for h in range(1, n_p):
            first_blk[h] = first_blk[h - 1] + per[h - 1]
        grid = (rows // tr,)
        w_spec = pl.BlockSpec((tr, cols), lambda i, qr: (i, 0))

        def piece_row(i, h):
            return jnp.clip(i - first_blk[h], 0, per[h] - 1)

        p_specs = [pl.BlockSpec((None, tr, cols), lambda i, qr, h=h: (qr[0], piece_row(i, h), 0))
                   for h in range(n_p)]
        r_specs = [pl.BlockSpec((3, tr, cols), lambda i, qr, h=h: (0, piece_row(i, h), 0)) for h in range(n_p)]
    else:
        assert n_p == 1
        nb, rows, cols = shp
        tr = _tile_rows(rows, cols)
        nt = rows // tr
        grid = (nb * nt,)
        w_spec = pl.BlockSpec((None, tr, cols), lambda i, qr: (i // nt, i % nt, 0))
        p_specs = [pl.BlockSpec((None, None, tr, cols), lambda i, qr: (qr[0], i // nt, i % nt, 0))]
        r_specs = [pl.BlockSpec((3, None, tr, cols), lambda i, qr: (0, i // nt, i % nt, 0))]

    token = list(_ORDER)

    def kern(*refs):
        p_refs, r_refs = refs[1:1 + n_p], refs[1 + n_p:1 + 2 * n_p]
        w_ref, m_ref, v_ref = refs[1 + 2 * n_p:4 + 2 * n_p]
        g_out, d_out, m_out, v_out = refs[4 + 2 * n_p + len(token):]

        def total(h):
            p_ref, r_ref = p_refs[h], r_refs[h]
            return ((p_ref[...].astype(F32) + r_ref[0].astype(F32)) + r_ref[1].astype(F32)) + r_ref[2].astype(F32)

        g = total(0)
        for h in range(1, n_p):
            g = jnp.where(pl.program_id(0) >= first_blk[h], total(h), g)
        delta, m2, v2 = _adam(w_ref[...], g, m_ref[...], v_ref[...])
        g_out[...] = g
        d_out[...] = delta
        m_out[...] = m2
        v_out[...] = v2

    sh = jax.ShapeDtypeStruct(shp, F32)
    res = pl.pallas_call(
        kern, name=name,
        grid_spec=pltpu.PrefetchScalarGridSpec(
            num_scalar_prefetch=1, grid=grid,
            in_specs=p_specs + r_specs + [w_spec, w_spec, w_spec] + [HBM_SPEC] * len(token),
            out_specs=[w_spec] * 4),
        out_shape=[sh] * 4,
        compiler_params=_cparams(("arbitrary",), 32),
    )(qidx, *pieces, *recvs, w, m, v, *token)
    _ORDER[:] = [res[1]]
    return res


def _sum8(name, parts):
    def kern(p_ref, o_ref):
        acc = p_ref[0]
        for p in range(1, N_DEV):
            acc = acc + p_ref[p]
        o_ref[...] = acc

    return pl.pallas_call(
        kern, name=name, out_shape=jax.ShapeDtypeStruct(parts.shape[1:], F32),
        compiler_params=pltpu.CompilerParams(vmem_limit_bytes=32 << 20),
    )(parts)


def _adam_small(name, w, g, m, v):
    def kern(w_ref, g_ref, m_ref, v_ref, d_out, m_out, v_out):
        delta, m2, v2 = _adam(w_ref[...], g_ref[...], m_ref[...], v_ref[...])
        d_out[...] = delta
        m_out[...] = m2
        v_out[...] = v2

    sh = jax.ShapeDtypeStruct(w.shape, F32)
    return pl.pallas_call(kern, name=name, out_shape=[sh] * 3)(w, g, m, v)


class _NoComm:
    def __init__(self):
        self.grads = {}

    def settled_before(self, kernel_name):
        return []

    def after(self, kernel_name):
        pass

    def grad(self, name, g):
        self.grads[name] = g


def _local_step(x, target, wts, vec, comm=None):
    comm = comm or _NoComm()
    _ORDER[:] = []
    seq, d = x.shape
    sv = vec["sv"]
    ff = vec["b_ff1"].shape[1]
    n_in = 5 * d

    def issue(name, call):
        _ORDER.extend(comm.settled_before(name))
        res = call()
        comm.after(name)
        return res

    tc = min(512, seq)
    t1k, t512, t256 = min(1024, seq), min(512, seq), min(256, seq)
    n512 = min(512, d)
    tkd = d

    x_bf = x.astype(BF16)
    full = lambda i, j, k: (0, 0)

    def epi_store(acc, i, j, ex, outs):
        outs[0][...] = acc

    n_pass = W_IN_PASSES
    piece = n_in // (N_DEV * n_pass)
    tz = min(2048, seq)
    z = None
    for k in range(n_pass):
        w_piece = wts["w_in_piece_%d" % k]
        prev = [] if z is None else [(z, (None, None))]
        (z,) = issue("z_proj_%d" % k, lambda: _mm(
            "z_proj_%d" % k, "nn", (seq // tz, N_DEV, 1),
            x_bf, ((tz, d), lambda i, j, kk: (i, 0)), w_piece, ((d, piece), lambda i, j, kk: (0, j)),
            prev, [jax.ShapeDtypeStruct((seq, n_in), F32)],
            [((tz, piece), lambda i, j, kk, k=k: (i, n_pass * j + k))], epi_store,
            aliases={2: 0} if prev else None))

    pool_w, wa, wx = wts["pool_w"], wts["lru_wa"], wts["lru_wx"]
    blk = wa.shape[2]
    y_pool = issue("pool_fwd", lambda: _pool_fwd(z, pool_w, vec["pool_scale"], seq, d, tc))
    xc, h_f = issue("lru_fwd", lambda: _lru_fwd(z, sv, vec["conv_b"], wa, wx, seq, d, tc))
    h_b, y_lru = issue("lru_rev", lambda: _lru_rev(z, sv, wa, wx, xc, h_f, seq, d, tc))
    w_pu, w_lu = wts["w_pool_up"], wts["w_lru_up"]
    m_bf, p_a, p_b = issue("merge", lambda: _merge(y_pool, w_pu, y_lru, w_lu, z, seq, d, t1k, n512))
    w_out = wts["w_out"]

    def epi_ln1(acc, i, j, ex, outs):
        x_ref, bo, g1, b1 = ex
        s1 = DN_ALPHA * x_ref[...] + (acc + bo[...])
        xhat, rstd, x1 = _ln_fwd(s1, g1[...], b1[...])
        outs[0][...] = xhat
        outs[1][...] = x1.astype(BF16)
        outs[2][...] = rstd

    rowd = lambda t: ((t, d), lambda i, j, k: (i, 0))
    vecd = ((1, d), full)
    xhat1, x1_bf, rstd1 = issue("out_ln1", lambda: _mm(
        "out_ln1", "nn", (seq // t256, 1, 1), m_bf, rowd(t256), w_out, ((d, d), full),
        [(x, rowd(t256)), (vec["b_out"], vecd), (vec["ln1_g"], vecd), (vec["ln1_b"], vecd)],
        [jax.ShapeDtypeStruct((seq, d), F32), jax.ShapeDtypeStruct((seq, d), BF16),
         jax.ShapeDtypeStruct((seq, 1), F32)],
        [rowd(t256), rowd(t256), ((t256, 1), lambda i, j, k: (i, 0))], epi_ln1))
    w1 = wts["w_ff1"]

    def epi_ff1(acc, i, j, ex, outs):
        r = jnp.maximum(acc + ex[0][...], 0.0)
        outs[0][...] = r.astype(BF16)
        outs[1][...] = (r * r).astype(BF16)

    nf = min(1024, ff)
    tile_f = ((t1k, nf), lambda i, j, k: (i, j))
    relu_h, hdn = issue("ff1", lambda: _mm(
        "ff1", "nn", (seq // t1k, ff // nf, 1), x1_bf, rowd(t1k), w1, ((d, nf), lambda i, j, k: (0, j)),
        [(vec["b_ff1"], ((1, nf), lambda i, j, k: (0, j)))],
        [jax.ShapeDtypeStruct((seq, ff), BF16)] * 2, [tile_f, tile_f], epi_ff1))
    w2 = wts["w_ff2"]

    def epi_ln2(acc, i, j, ex, outs):
        xh1, tgt, g1, b1, bf2, g2, b2 = ex
        ds_ref, dsb_ref, sm_ref, loss_ref = outs
        x1 = xh1[...] * g1[...] + b1[...]
        s2 = DN_ALPHA * x1 + (acc + bf2[...])
        xhat, rstd, y = _ln_fwd(s2, g2[...], b2[...])
        e = y - tgt[...]
        part = 0.5 * jnp.sum(jnp.mean(e * e, axis=-1, keepdims=True))
        dy = e * (1.0 / d)
        ds2 = _ln_bwd(dy, xhat, rstd, g2[...])
        ds_ref[...] = ds2
        dsb_ref[...] = ds2.astype(BF16)
        sm_ref[0:1, :] += _colsum(dy * xhat)
        sm_ref[1:2, :] += _colsum(dy)
        sm_ref[2:3, :] += _colsum(ds2)
        loss_ref[...] += jnp.full(loss_ref.shape, part, F32)

    def zero_tail(n_tail):
        def init(i, j, outs):
            @pl.when(i == 0)
            def _():
                for o in outs[-n_tail:]:
                    o[...] = jnp.zeros_like(o)
        return init

    def rows_epilogue(name, lead, extras, out_shapes, out_specs, epi, n_tail):
        ne = len(extras)

        def kern(*refs):
            i = pl.program_id(0)
            outs = refs[1 + ne:]
            zero_tail(n_tail)(i, 0, outs)
            epi(refs[0][...], i, 0, refs[1:1 + ne], outs)

        def spec(s):
            bs, f = s
            return pl.BlockSpec(bs, lambda i, f=f: f(i, 0, 0))

        return _pcall(name, kern, (seq // t256,), [spec(rowd(t256))] + [spec(s) for _, s in extras],
                      [spec(s) for s in out_specs], out_shapes, [lead] + [e for e, _ in extras])

    tkw2, tw = min(2048, ff), min(1024, d)
    (ff_out,) = _mm("ff2", "nn", (seq // t1k, d // tw, ff // tkw2), hdn, ((t1k, tkw2), lambda i, j, k: (i, k)),
                    w2, ((tkw2, tw), lambda i, j, k: (k, j)), [], [jax.ShapeDtypeStruct((seq, d), F32)],
                    [((t1k, tw), lambda i, j, k: (i, j))], None)
    ds2, ds2_bf, sm_ln2, loss_blk = issue("ff2_ln2", lambda: rows_epilogue(
        "ln2_loss", ff_out,
        [(xhat1, rowd(t256)), (target, rowd(t256)), (vec["ln1_g"], vecd), (vec["ln1_b"], vecd),
         (vec["b_ff2"], vecd), (vec["ln2_g"], vecd), (vec["ln2_b"], vecd)],
        [jax.ShapeDtypeStruct((seq, d), F32), jax.ShapeDtypeStruct((seq, d), BF16),
         jax.ShapeDtypeStruct((SUBLANES, d), F32), jax.ShapeDtypeStruct((SUBLANES, 128), F32)],
        [rowd(t256), rowd(t256), ((SUBLANES, d), full), ((SUBLANES, 128), full)], epi_ln2, 2))

    tkw = min(2048, seq)

    def dw(name, wname, a, b, m_dim, n_dim, b_spec=None, row0=0):
        _ORDER.extend(comm.settled_before(name))
        tn = min(1024, n_dim)
        tm = next(t for t in (1024, 768, 512, 256, 128) if m_dim % t == 0 and row0 % t == 0)
        b_spec = b_spec or ((tkw, tn), lambda i, j, k: (k, j))
        i0 = row0 // tm
        (out,) = _mm(
            name, "tn", (m_dim // tm, n_dim // tn, seq // tkw),
            a, ((tkw, tm), lambda i, j, k: (k, i0 + i)), b, b_spec, [],
            [jax.ShapeDtypeStruct((m_dim, n_dim), F32)], [((tm, tn), lambda i, j, k: (i, j))],
            epi_store if seq == tkw else None)
        comm.grad(wname, out)
        comm.after(name)

    dw("dw_ff2", "w_ff2", hdn, ds2_bf, ff, d)

    def epi_dpre(acc, i, j, ex, outs):
        dpre = acc * (2.0 * ex[0][...].astype(F32))
        outs[0][...] = dpre.astype(BF16)

        @pl.when(i == 0)
        def _():
            outs[1][...] = jnp.zeros_like(outs[1])

        outs[1][0:1, :] += _colsum(dpre)

    dpre, sm_bff1 = issue("dhdn", lambda: _mm(
        "dhdn", "nt", (seq // t1k, ff // nf, 1), ds2_bf, rowd(t1k), w2, ((nf, d), lambda i, j, k: (j, 0)),
        [(relu_h, tile_f)],
        [jax.ShapeDtypeStruct((seq, ff), BF16), jax.ShapeDtypeStruct((SUBLANES, ff), F32)],
        [tile_f, ((SUBLANES, nf), lambda i, j, k: (0, j))], epi_dpre, order="ji"))

    dw("dw_ff1", "w_ff1", x1_bf, dpre, d, ff)

    def epi_ln1b(acc, i, j, ex, outs):
        ds2_ref, xh1, rs1, g1 = ex
        ds_ref, dsb_ref, sm_ref = outs
        dy1 = acc + DN_ALPHA * ds2_ref[...]
        xhat = xh1[...]
        ds1 = _ln_bwd(dy1, xhat, rs1[...], g1[...])
        ds_ref[...] = ds1
        dsb_ref[...] = ds1.astype(BF16)
        sm_ref[0:1, :] += _colsum(dy1 * xhat)
        sm_ref[1:2, :] += _colsum(dy1)
        sm_ref[2:3, :] += _colsum(ds1)

    (dx1,) = _mm("dx1", "nt", (seq // t1k, d // tw, ff // tkw2), dpre, ((t1k, tkw2), lambda i, j, k: (i, k)),
                 w1, ((tw, tkw2), lambda i, j, k: (j, k)), [], [jax.ShapeDtypeStruct((seq, d), F32)],
                 [((t1k, tw), lambda i, j, k: (i, j))], None)
    ds1, ds1_bf, sm_ln1 = issue("dx1_ln1", lambda: rows_epilogue(
        "ln1_bwd", dx1,
        [(ds2, rowd(t256)), (xhat1, rowd(t256)), (rstd1, ((t256, 1), lambda i, j, k: (i, 0))),
         (vec["ln1_g"], vecd)],
        [jax.ShapeDtypeStruct((seq, d), F32), jax.ShapeDtypeStruct((seq, d), BF16),
         jax.ShapeDtypeStruct((SUBLANES, d), F32)],
        [rowd(t256), rowd(t256), ((SUBLANES, d), full)], epi_ln1b, 1))

    dw("dw_out", "w_out", m_bf, ds1_bf, d, d)
    nd = min(1024, d)
    n_n = d // nd
    tile_d = ((t512, nd), lambda i, j, k: (i, j))

    def epi_dm(acc, i, j, ex, outs):
        la, lb, pa, pb = ex
        ga, gb = _sig(la[...]), _sig(lb[...])
        outs[0][...] = (acc * ga).astype(BF16)
        outs[1][...] = (acc * gb).astype(BF16)
        outs[2][0] = (acc * pa[...].astype(F32) * ga * (1.0 - ga)).astype(BF16)
        outs[2][1] = (acc * pb[...].astype(F32) * gb * (1.0 - gb)).astype(BF16)

    dp_a, dp_b, dz = _mm(
        "dm", "nt", (seq // t512, n_n, 1), ds1_bf, rowd(t512), w_out, ((nd, d), lambda i, j, k: (j, 0)),
        [(z, ((t512, nd), lambda i, j, k: (i, 3 * n_n + j))),
         (z, ((t512, nd), lambda i, j, k: (i, 4 * n_n + j))), (p_a, tile_d), (p_b, tile_d)],
        [jax.ShapeDtypeStruct((seq, d), BF16), jax.ShapeDtypeStruct((seq, d), BF16),
         jax.ShapeDtypeStruct((5, seq, d), BF16)],
        [tile_d, tile_d, ((2, t512, nd), lambda i, j, k: (0, i, j))], epi_dm)

    dw("dw_pool_up", "w_pool_up", y_pool, dp_a, d, d)
    dw("dw_lru_up", "w_lru_up", y_lru, dp_b, d, d)

    def epi_bf(acc, i, j, ex, outs):
        outs[0][...] = acc.astype(BF16)

    (dy_pool,) = issue("dy_pool", lambda: _mm(
        "dy_pool", "nt", (seq // t512, n_n, 1), dp_a, rowd(t512), w_pu, ((nd, d), lambda i, j, k: (j, 0)), [],
        [jax.ShapeDtypeStruct((seq, d), BF16)], [tile_d], epi_bf))

    def epi_dylru(acc, i, j, ex, outs):
        hf, hb, ug, _ = ex
        u = ug[...]
        outs[0][...] = acc * _gelu(u)
        outs[1][...] = (acc * (hf[...] + hb[...]) * _gelu_grad(u)).astype(BF16)

    dz_in = dz
    dh, dz = issue("dy_lru", lambda: _mm(
        "dy_lru", "nt", (seq // t512, n_n, 1), dp_b, rowd(t512), w_lu, ((nd, d), lambda i, j, k: (j, 0)),
        [(h_f, tile_d), (h_b, tile_d), (z, ((t512, nd), lambda i, j, k: (i, 2 * n_n + j))),
         (dz_in, (None, None))],
        [jax.ShapeDtypeStruct((seq, d), F32), jax.ShapeDtypeStruct(dz_in.shape, BF16)],
        [tile_d, ((None, t512, nd), lambda i, j, k: (4, i, j))], epi_dylru, aliases={5: 1}))

    dz, g_pw, sm_pool = _pool_bwd(z, dy_pool, pool_w, vec["pool_scale"], dz, seq, d, tc)
    comm.grad("pool_w", g_pw)
    dxc0, g_wa0, g_wx0, sm_l0 = issue("lru_bwd_0", lambda: _lru_bwd(
        0, xc, dh, h_f, sv, wa, wx, None, seq, d, tc))
    dxc, g_wa1, g_wx1, sm_l1 = issue("lru_bwd_1", lambda: _lru_bwd(
        1, xc, dh, h_b, sv, wa, wx, dxc0, seq, d, tc))
    comm.grad("lru_wa", jnp.concatenate([g_wa0, g_wa1], axis=0))
    comm.grad("lru_wx", jnp.concatenate([g_wx0, g_wx1], axis=0))
    dz, sm_conv = _conv_bwd(z, dxc, sv, dz, seq, d, tc, blk)

    tnw = min(1024, d)
    per_seg = d // tnw
    seg_spec = ((None, tkw, tnw), lambda i, j, k: ((j // per_seg + 2) % 5, k, j % per_seg))
    lo_rows = 3 * d // 4
    dw("dw_in_lo", "w_in_lo", x_bf, dz, lo_rows, n_in, b_spec=seg_spec)
    dw("dw_in_hi", "w_in_hi", x_bf, dz, d - lo_rows, n_in, b_spec=seg_spec, row0=lo_rows)

    nk = d // tkd

    def epi_dx(acc, i, j, ex, outs):
        outs[0][...] = acc + DN_ALPHA * ex[0][...]

    (grad_x,) = issue("dx", lambda: _mm(
        "dx", "nt", (seq // t512, 1, n_in // tkd), dz,
        ((None, t512, tkd), lambda i, j, k: ((k // nk + 2) % 5, i, k % nk)),
        wts["w_in"], ((d, tkd), lambda i, j, k: (0, k)), [(ds1, rowd(t512))],
        [jax.ShapeDtypeStruct((seq, d), F32)], [rowd(t512)], epi_dx, acc_shape=(t512, d)))

    small = {"ln2": sm_ln2, "b_ff1": sm_bff1, "ln1": sm_ln1, "pool": sm_pool, "lru0": sm_l0, "lru1": sm_l1,
             "conv": sm_conv}
    return loss_blk[0, 0], grad_x, small


REP = ("pool_scale", "conv_b", "b_out", "ln1_g", "ln1_b", "b_ff2", "ln2_g", "ln2_b")
SHARDED_SMALL = (("conv_w", 4), ("lru_ba", 2), ("lru_bx", 2), ("lru_lambda", 2))
WEIGHT_ORDER = ("w_in", "pool_w", "pool_scale", "conv_w", "conv_b", "lru_wa", "lru_ba", "lru_wx", "lru_bx",
                "lru_lambda", "w_pool_up", "w_lru_up", "w_out", "b_out", "ln1_g", "ln1_b", "w_ff1", "b_ff1",
                "w_ff2", "b_ff2", "ln2_g", "ln2_b")


def _pad_rows(a, rows):
    return jnp.concatenate([a, jnp.zeros((rows - a.shape[0], a.shape[1]), a.dtype)], axis=0)


def kernel(x, w_in, pool_w, pool_scale, conv_w, conv_b, lru_wa, lru_ba, lru_wx, lru_bx, lru_lambda, w_pool_up, w_lru_up, w_out, b_out, ln1_g, ln1_b, w_ff1, b_ff1, w_ff2, b_ff2, ln2_g, ln2_b, loss_target, m_w_in, m_pool_w, m_pool_scale, m_conv_w, m_conv_b, m_lru_wa, m_lru_ba, m_lru_wx, m_lru_bx, m_lru_lambda, m_w_pool_up, m_w_lru_up, m_w_out, m_b_out, m_ln1_g, m_ln1_b, m_w_ff1, m_b_ff1, m_w_ff2, m_b_ff2, m_ln2_g, m_ln2_b, v_w_in, v_pool_w, v_pool_scale, v_conv_w, v_conv_b, v_lru_wa, v_lru_ba, v_lru_wx, v_lru_bx, v_lru_lambda, v_w_pool_up, v_w_lru_up, v_w_out, v_b_out, v_ln1_g, v_ln1_b, v_w_ff1, v_b_ff1, v_w_ff2, v_b_ff2, v_ln2_g, v_ln2_b):
    args = dict(locals())
    w = {n: args[n] for n in WEIGHT_ORDER}
    mom = {n: args["m_" + n] for n in WEIGHT_ORDER}
    var = {n: args["v_" + n] for n in WEIGHT_ORDER}
    seq, d = x.shape[1], x.shape[2]
    n_heads, blk = lru_wa.shape[2], lru_wa.shape[4]
    n_groups = pool_w.shape[1]
    ff = b_ff1.shape[1]
    cx, cy, cc = _coords()
    me = 4 * cx + 2 * cy + cc
    cidx = jnp.reshape(cc, (1,)).astype(jnp.int32)
    qidx = jnp.reshape(2 * cx + cy, (1,)).astype(jnp.int32)

    fam_of = dict(BIG)
    sviews = {n: _shard_view(w[n], fam) for n, fam in BIG}
    size_of = {n: sviews[n].shape[1] for n, _ in BIG}
    for piece_name in ("w_in_lo", "w_in_hi"):
        fam_of[piece_name], size_of[piece_name] = fam_of["w_in"], size_of["w_in"]
    wts = {}

    def take_gathered(names, arrays):
        for n, g in zip(names, arrays):
            if n == "pool_w":
                g = g.reshape(n_groups, d // n_groups, d // n_groups)
            elif n in ("lru_wa", "lru_wx"):
                g = g.reshape(2, n_heads, blk, blk)
            elif fam_of[n] == "row":
                g = g.reshape(g.shape[1:])
            wts[n] = g

    shard_bf = {n: sviews[n].astype(BF16) for n, _ in BIG}
    piece = sviews["w_in"].shape[1] // W_IN_PASSES
    for k in range(W_IN_PASSES):
        name = "w_in_piece_%d" % k
        shard_bf[name] = shard_bf["w_in"][:, k * piece:(k + 1) * piece]
        fam_of[name] = "col"

    def gather_job(names, extra=()):
        return _ag_job([shard_bf[n] for n in names] + [e for e, _ in extra],
                       [fam_of[n] for n in names] + [f for _, f in extra])

    launched = [0]

    def on_sequencer(kind, job):
        launched[0] += 1
        return _sequencer_job("sq_%s_%d" % (kind, launched[0]), job, launched[0] % 2)

    class Plan:
        sibling = {"dw_ff2": ("w_ff2",), "dw_ff1": ("w_ff1",), "dw_lru_up": ("w_out", "w_pool_up", "w_lru_up"),
                   "dw_in_lo": ("w_in_lo",), "dw_in_hi": ("w_in_hi", "pool_w", "lru_wa", "lru_wx")}
        chips = {"dw_ff1": ("w_ff2",), "dw_out": ("w_ff1",), "dy_lru": ("w_out", "w_pool_up", "w_lru_up"),
                 "dw_in_hi": ("w_in_lo",), "dx": ("w_in_hi", "pool_w", "lru_wa", "lru_wx")}

        def __init__(self):
            self.grads, self.recv_a, self.parts, self.recv_b = {}, {}, {}, {}

        def grad(self, name, g):
            self.grads[name] = g if fam_of[name] == "col" else g.reshape((-1,) + g.shape[-2:])

        settle = {"dw_ff1": ("w_in",), "dw_out": ("w_ff2",), "dy_pool": ("w_ff1",), "lru_bwd_1": ("w_out",)}
        settle_add = {"dx": ("w_in_lo",)}

        def settled_before(self, host):
            return [wts[n] if n == "w_in" else self.recv_b[n] for n in self.settle.get(host, ())]

        def after(self, host):
            if host in self.chips:
                names = self.chips[host]
                _ORDER.extend(self.recv_b[n] for n in self.settle_add.get(host, ()))
                for n in names:
                    self.parts[n] = _rs_add("rs_add_" + n, self.grads[n], self.recv_a[n], fam_of[n], size_of[n], cidx)
                res = on_sequencer("chips", _rs_chips_job([self.parts[n] for n in names]))
                self.recv_b.update(zip(names, res))
            if host in self.sibling:
                names = self.sibling[host]
                res = on_sequencer("sibling", _rs_sibling_job(
                    [self.grads[n] for n in names], [fam_of[n] for n in names], [size_of[n] for n in names]))
                self.recv_a.update(zip(names, res))

    first = ("w_in_piece_0",)
    sv_shard = _pad_rows(jnp.concatenate([w[n].reshape(r, -1) for n, r in SHARDED_SMALL], axis=0), 16)
    gathered = on_sequencer("gather", gather_job(first, [(sv_shard, "col")]))
    take_gathered(first, gathered[:-1])
    vec = {n: w[n] for n in REP}
    vec["b_ff1"] = b_ff1
    vec["sv"] = gathered[-1]
    queue = [("w_in_piece_%d" % k,) for k in range(1, W_IN_PASSES)]
    queue += [("pool_w", "lru_wa", "lru_wx"), ("w_pool_up", "w_lru_up"), ("w_out",), ("w_ff1",), ("w_ff2",),
              ("w_in",)]
    for names in queue:
        take_gathered(names, on_sequencer("gather", gather_job(names)))

    plan = Plan()
    loss_part, grad_x, small = _local_step(x.reshape(seq, d), loss_target.reshape(seq, d), wts, vec, plan)

    out_g, out_d, out_m, out_v = {}, {}, {}, {}
    for n, fam in sorted(BIG, key=lambda nf: nf[0] in ("w_in", "pool_w", "lru_wa", "lru_wx")):
        halves = [n + "_lo", n + "_hi"] if n == "w_in" else [n]
        res = _rs_final_adam("adam_" + n, [plan.parts[h] for h in halves], [plan.recv_b[h] for h in halves],
                             sviews[n], _shard_view(mom[n], fam), _shard_view(var[n], fam), fam, qidx)
        out_g[n], out_d[n], out_m[n], out_v[n] = [r.reshape(w[n].shape) for r in res]

    loss_row = jnp.concatenate([loss_part.reshape(1, 1), jnp.zeros((1, d - 1), F32)], axis=1)
    rows = [small["pool"][0:1], small["conv"][4:5], small["ln1"][2:3], small["ln1"][0:1], small["ln1"][1:2],
            small["ln2"][2:3], small["ln2"][0:1], small["ln2"][1:2], small["b_ff1"][0:1].reshape(ff // d, d),
            small["conv"][0:4], small["lru0"][0:1], small["lru1"][0:1], small["lru0"][1:2], small["lru1"][1:2],
            small["lru0"][2:3], small["lru1"][2:3], loss_row]
    n_rep = len(REP) + ff // d
    n_rows = n_rep + sum(r for _, r in SHARDED_SMALL)
    pad_rows = -(-(n_rows + 1) // SUBLANES) * SUBLANES
    packed = _pad_rows(jnp.concatenate(rows, axis=0), pad_rows)
    (all_small,) = on_sequencer("gather", _ag_job([packed], ["lead"]))
    g_small = _sum8("sum_small", all_small)
    loss = g_small[n_rows, 0]

    def pack_rep(t):
        return jnp.concatenate([t[n] for n in REP] + [t["b_ff1"].reshape(ff // d, d)], axis=0)

    def pack_sh(t):
        return jnp.concatenate([t[n].reshape(r, -1) for n, r in SHARDED_SMALL], axis=0)

    g_rep = g_small[:n_rep]
    cs = d // N_DEV
    g_sh = lax.dynamic_slice_in_dim(g_small[n_rep:n_rows], me * cs, cs, axis=1)
    d_rep, m_rep, v_rep = _adam_small("adam_rep", pack_rep(w), g_rep, pack_rep(mom), pack_rep(var))
    d_sh, m_sh, v_sh = _adam_small("adam_sharded", pack_sh(w), g_sh, pack_sh(mom), pack_sh(var))

    def unpack(rep_t, sh_t, dst):
        for i, n in enumerate(REP):
            dst[n] = rep_t[i:i + 1].reshape(w[n].shape)
        dst["b_ff1"] = rep_t[len(REP):n_rep].reshape(w["b_ff1"].shape)
        r0 = 0
        for n, r in SHARDED_SMALL:
            dst[n] = sh_t[r0:r0 + r].reshape(w[n].shape)
            r0 += r

    unpack(g_rep, g_sh, out_g)
    unpack(d_rep, d_sh, out_d)
    unpack(m_rep, m_sh, out_m)
    unpack(v_rep, v_sh, out_v)

    _ORDER[:] = []
    outs = [loss, grad_x.reshape(x.shape)]
    for t in (out_g, out_d, out_m, out_v):
        outs += [t[n] for n in WEIGHT_ORDER]
    return tuple(outs)
```

```python
import jax
import jax.numpy as jnp
from jax import lax
from jax.experimental import pallas as pl
from jax.experimental.pallas import tpu as pltpu
from jax.experimental.pallas import tpu_sc as plsc

F32 = jnp.float32
BF16 = jnp.bfloat16
MESH = pl.DeviceIdType.MESH
N_DEV = 8

DN_ALPHA = 2.0 ** 0.25
LN_EPS = 1e-5
LRU_C = 8.0
ADAM_LR = 0.001
ADAM_B1 = 0.9
ADAM_B2 = 0.999
ADAM_EPS = 1e-08
ADAM_WD = 0.01
ADAM_STEP = 10
GELU_C = 0.7978845608028654
GELU_K = 0.044715

W_IN_PASSES = 5
HALO = 16
SUBLANES = 8
VMEM_MB = 56


def _cparams(sem, vmem_mb=VMEM_MB):
    return pltpu.CompilerParams(dimension_semantics=sem, vmem_limit_bytes=vmem_mb << 20)


HBM_SPEC = pl.BlockSpec(memory_space=pl.ANY)


class _Job:
    def __init__(self, ins, outs, sems, start, finish, mid=None):
        self.ins, self.outs, self.sems = list(ins), list(outs), list(sems)
        self.start, self.finish, self.mid = start, finish, mid
        self.results = None


_ORDER = []


def _pcall(name, body, grid, in_specs, out_specs, out_shape, inputs, scratch=(), aliases=None,
           vmem_mb=VMEM_MB):
    token = [t for t in _ORDER if not any(t is a for a in inputs)]
    n_in = len(inputs)

    def ordered(*refs):
        return body(*refs[:n_in], *refs[n_in + len(token):])

    res = pl.pallas_call(ordered, name=name, grid=grid, in_specs=list(in_specs) + [HBM_SPEC] * len(token),
                         out_specs=list(out_specs), out_shape=list(out_shape), scratch_shapes=list(scratch),
                         input_output_aliases=aliases or {},
                         compiler_params=_cparams(("arbitrary",) * len(grid), vmem_mb))(*inputs, *token)
    _ORDER[:] = [res[0]]
    return res


def _sequencer_job(name, job, collective_id):
    ji, jo = len(job.ins), len(job.outs)

    def body(*refs):
        jins, jouts, sems = refs[:ji], refs[ji:ji + jo], refs[ji + jo:]
        barrier = pltpu.get_barrier_semaphore()
        x, y, c = lax.axis_index("x"), lax.axis_index("y"), lax.axis_index("c")
        for r in range(1, N_DEV):
            peer = ((1 - x) if r & 4 else x, (1 - y) if r & 2 else y, (1 - c) if r & 1 else c)
            pl.semaphore_signal(barrier, inc=1, device_id=peer, device_id_type=MESH)
        pl.semaphore_wait(barrier, N_DEV - 1)
        job.start(jins, jouts, sems)
        if job.mid is not None:
            job.mid(jins, jouts, sems)
        job.finish(jins, jouts, sems)

    res = pl.kernel(
        body, name=name, out_type=job.outs, mesh=plsc.ScalarSubcoreMesh(axis_name="sequencer", num_cores=1),
        scratch_types=job.sems, compiler_params=pltpu.CompilerParams(collective_id=collective_id),
    )(*job.ins)
    job.results = list(res)
    return job.results


def _dot(mode, a, b):
    if mode == "nn":
        dims = (((1,), (0,)), ((), ()))
    elif mode == "nt":
        dims = (((1,), (1,)), ((), ()))
    else:
        dims = (((0,), (0,)), ((), ()))
    return lax.dot_general(a, b, dims, preferred_element_type=F32)


def _sig(x):
    return 0.5 * jnp.tanh(0.5 * x) + 0.5


def _gelu(x):
    t = jnp.tanh(GELU_C * (x + GELU_K * x * x * x))
    return 0.5 * x * (1.0 + t)


def _gelu_grad(x):
    x2 = x * x
    t = jnp.tanh(GELU_C * (x + GELU_K * x * x2))
    return 0.5 * (1.0 + t) + 0.5 * x * (1.0 - t * t) * GELU_C * (1.0 + 3.0 * GELU_K * x2)


def _colsum(v):
    return jnp.sum(v, axis=0, keepdims=True)


def _mm(name, mode, grid, a, a_spec, b, b_spec, extras, out_shapes, out_specs, epi, *,
        order="ij", acc_shape=None, aliases=None, vmem_mb=VMEM_MB):
    gm, gn, gk = grid

    def spec(s):
        bs, f = s
        if f is None:
            return pl.BlockSpec(memory_space=pl.ANY)
        if order == "ij":
            return pl.BlockSpec(bs, lambda i, j, k, f=f: f(i, j, k))
        return pl.BlockSpec(bs, lambda j, i, k, f=f: f(i, j, k))

    ne, no = len(extras), len(out_shapes)

    def kern(*refs):
        a_ref, b_ref = refs[0], refs[1]
        ex = refs[2:2 + ne]
        outs = refs[2 + ne:2 + ne + no]
        if order == "ij":
            i, j = pl.program_id(0), pl.program_id(1)
        else:
            j, i = pl.program_id(0), pl.program_id(1)
        k = pl.program_id(2)
        prod = _dot(mode, a_ref[...], b_ref[...])
        if gk == 1:
            epi(prod, i, j, ex, outs)
        elif epi is None:
            @pl.when(k == 0)
            def _():
                outs[0][...] = prod

            @pl.when(k > 0)
            def _():
                outs[0][...] += prod
        else:
            acc = refs[-1]

            @pl.when(k == 0)
            def _():
                acc[...] = prod

            @pl.when(k > 0)
            def _():
                acc[...] += prod

            @pl.when(k == gk - 1)
            def _():
                epi(acc[...], i, j, ex, outs)

    g = (gm, gn, gk) if order == "ij" else (gn, gm, gk)
    scratch = [pltpu.VMEM(acc_shape, F32)] if gk > 1 and epi is not None else []
    return _pcall(name, kern, g, [spec(a_spec), spec(b_spec)] + [spec(s) for _, s in extras],
                  [spec(s) for s in out_specs], out_shapes, [a, b] + [e for e, _ in extras],
                  scratch=scratch, aliases=aliases, vmem_mb=vmem_mb)


def _ext(ref, c, n_chunks, tc, seq):
    c0 = pl.multiple_of(c * tc, tc)
    body = ref[pl.ds(c0, tc), :].astype(F32)
    t0 = pl.multiple_of(jnp.maximum(c0 - HALO, 0), HALO)
    b0 = pl.multiple_of(jnp.minimum(c0 + tc, seq - HALO), HALO)
    top = ref[pl.ds(t0, HALO), :].astype(F32)
    bot = ref[pl.ds(b0, HALO), :].astype(F32)
    top = jnp.where(c > 0, top, 0.0)
    bot = jnp.where(c < n_chunks - 1, bot, 0.0)
    return jnp.concatenate([top, body, bot], axis=0)


def _shifted(vext, off, tc):
    n = vext.shape[0]
    r = vext if off == 0 else pltpu.roll(vext, (n - off) % n, 0)
    return r[HALO:HALO + tc]


def _win_sum(vext, g, extra, tc):
    s2 = vext + pltpu.roll(vext, 1, 0)
    s4 = s2 + pltpu.roll(s2, 2, 0)
    s8 = s4 + pltpu.roll(s4, 4, 0)
    s16 = s8 + pltpu.roll(s8, 8, 0)
    outs = [_shifted(s, extra + hw - 1, tc) for s, hw in ((s2, 1), (s4, 2), (s8, 4), (s16, 8))]
    return jnp.where(g == 0, outs[0], jnp.where(g == 1, outs[1], jnp.where(g == 2, outs[2], outs[3])))


def _win_cnt(t, hw, seq):
    return (jnp.minimum(t + hw, seq) - jnp.maximum(t - hw, 0)).astype(F32)


def _pool_d(uext, g, c, tc, seq):
    hw = jnp.left_shift(1, g)
    t = c * tc + lax.broadcasted_iota(jnp.int32, (tc, 1), 0)
    ws = _win_sum(uext, g, 0, tc)
    return ws / _win_cnt(t, hw, seq) - uext[HALO:HALO + tc]


def _scan_tiles(a_ref, b_ref, h_ref, carry_ref, n_tiles, reverse):
    blk = a_ref.shape[1]
    row = lax.broadcasted_iota(jnp.int32, (SUBLANES, blk), 0)

    def tile(j, hc):
        jj = (n_tiles - 1 - j) if reverse else j
        off = pl.multiple_of(jj * SUBLANES, SUBLANES)
        a = a_ref[pl.ds(off, SUBLANES), :]
        b = b_ref[pl.ds(off, SUBLANES), :]
        for kk in (1, 2, 4):
            sh = (SUBLANES - kk) if reverse else kk
            a_s = pltpu.roll(a, sh, 0)
            b_s = pltpu.roll(b, sh, 0)
            m = (row < SUBLANES - kk) if reverse else (row >= kk)
            a_s = jnp.where(m, a_s, 1.0)
            b_s = jnp.where(m, b_s, 0.0)
            b = a * b_s + b
            a = a * a_s
        h = a * hc + b
        h_ref[pl.ds(off, SUBLANES), :] = h
        return h[0:1, :] if reverse else h[SUBLANES - 1:SUBLANES, :]

    group = SUBLANES if n_tiles % SUBLANES == 0 else 1

    def tiles(jg, hc):
        for u in range(group):
            hc = tile(jg * group + u, hc)
        return hc

    hc = lax.fori_loop(0, n_tiles // group, tiles, carry_ref[0:1, :])
    carry_ref[0:1, :] = hc


def _lru_k(lam):
    y = -lam
    e = jnp.exp(-jnp.abs(y))
    u = 1.0 + e
    l1p = jnp.where(u == 1.0, e, jnp.log(u) * (e / (u - 1.0)))
    return -LRU_C * (jnp.maximum(y, 0.0) + l1p)


def _lru_gates(xc, wa, wx, ba, bx, lam):
    xb = xc.astype(BF16)
    r = _sig(jnp.dot(xb, wa, preferred_element_type=F32) + ba)
    i = _sig(jnp.dot(xb, wx, preferred_element_type=F32) + bx)
    k = _lru_k(lam)
    la = k * r
    a = jnp.exp(la)
    s = jnp.sqrt(-jnp.tanh(la) * (a * a + 1.0))
    return r, i, k, a, s


SV_CONV, SV_BA, SV_BX, SV_LAM = 0, 4, 6, 8


def _pool_fwd(z, pw, scale, seq, d, tc):
    n_g = pw.shape[0]
    pg = d // n_g
    n_chunks = seq // tc

    def kern(z_ref, pw_ref, sc_ref, y_ref):
        g, c = pl.program_id(0), pl.program_id(1)
        uext = _ext(z_ref, c, n_chunks, tc, seq)
        dd = _pool_d(uext, g, c, tc, seq)
        q = jnp.dot(dd.astype(BF16), pw_ref[...], preferred_element_type=F32)
        y_ref[...] = (q * sc_ref[...]).astype(BF16)

    (y,) = _pcall(
        "pool_fwd", kern, (n_g, n_chunks),
        [pl.BlockSpec((seq, pg), lambda g, c: (0, g)),
         pl.BlockSpec((None, pg, pg), lambda g, c: (g, 0, 0)),
         pl.BlockSpec((1, pg), lambda g, c: (0, g))],
        [pl.BlockSpec((tc, pg), lambda g, c: (c, g))],
        [jax.ShapeDtypeStruct((seq, d), BF16)], [z, pw, scale])
    return y


def _lru_fwd(z, sv, conv_b, wa, wx, seq, d, tc):
    n_h, blk = wa.shape[1], wa.shape[2]
    n_chunks = seq // tc
    lru_off = d // blk

    def kern(z_ref, sv_ref, cb_ref, wa_ref, wx_ref, xc_ref, h_ref, a_s, b_s, carry):
        c = pl.program_id(1)
        uext = _ext(z_ref, c, n_chunks, tc, seq)
        xc = cb_ref[...]
        for k in range(4):
            xc = xc + _shifted(uext, k - 2, tc) * sv_ref[SV_CONV + k:SV_CONV + k + 1, :]
        xc_ref[...] = xc
        _, i, _, a, s = _lru_gates(xc, wa_ref[...], wx_ref[...], sv_ref[SV_BA:SV_BA + 1, :],
                                   sv_ref[SV_BX:SV_BX + 1, :], sv_ref[SV_LAM:SV_LAM + 1, :])
        a_s[...] = a
        b_s[...] = s * (i * xc)

        @pl.when(c == 0)
        def _():
            carry[...] = jnp.zeros_like(carry)

        _scan_tiles(a_s, b_s, h_ref, carry, tc // SUBLANES, False)

    col = lambda h, c: (c, h)
    return _pcall(
        "lru_fwd", kern, (n_h, n_chunks),
        [pl.BlockSpec((seq, blk), lambda h, c: (0, lru_off + h)),
         pl.BlockSpec((16, blk), lambda h, c: (0, h)),
         pl.BlockSpec((1, blk), lambda h, c: (0, h)),
         pl.BlockSpec((None, None, blk, blk), lambda h, c: (0, h, 0, 0)),
         pl.BlockSpec((None, None, blk, blk), lambda h, c: (0, h, 0, 0))],
        [pl.BlockSpec((tc, blk), col), pl.BlockSpec((tc, blk), col)],
        [jax.ShapeDtypeStruct((seq, d), F32), jax.ShapeDtypeStruct((seq, d), F32)],
        [z, sv, conv_b, wa, wx],
        scratch=[pltpu.VMEM((tc, blk), F32), pltpu.VMEM((tc, blk), F32), pltpu.VMEM((SUBLANES, blk), F32)])


def _lru_rev(z, sv, wa, wx, xc, h_f, seq, d, tc):
    n_h, blk = wa.shape[1], wa.shape[2]
    n_chunks = seq // tc
    gate_off = 2 * d // blk

    def kern(z_ref, sv_ref, wa_ref, wx_ref, xc_ref, hf_ref, hb_ref, y_ref, a_s, b_s, carry):
        c = pl.program_id(1)
        xc = xc_ref[...]
        _, i, _, a, s = _lru_gates(xc, wa_ref[...], wx_ref[...], sv_ref[SV_BA + 1:SV_BA + 2, :],
                                   sv_ref[SV_BX + 1:SV_BX + 2, :], sv_ref[SV_LAM + 1:SV_LAM + 2, :])
        a_s[...] = a
        b_s[...] = s * (i * xc)

        @pl.when(c == 0)
        def _():
            carry[...] = jnp.zeros_like(carry)

        _scan_tiles(a_s, b_s, hb_ref, carry, tc // SUBLANES, True)
        y_ref[...] = ((hf_ref[...] + hb_ref[...]) * _gelu(z_ref[...])).astype(BF16)

    rev = lambda h, c: (n_chunks - 1 - c, h)
    return _pcall(
        "lru_rev", kern, (n_h, n_chunks),
        [pl.BlockSpec((tc, blk), lambda h, c: (n_chunks - 1 - c, gate_off + h)),
         pl.BlockSpec((16, blk), lambda h, c: (0, h)),
         pl.BlockSpec((None, None, blk, blk), lambda h, c: (1, h, 0, 0)),
         pl.BlockSpec((None, None, blk, blk), lambda h, c: (1, h, 0, 0)),
         pl.BlockSpec((tc, blk), rev), pl.BlockSpec((tc, blk), rev)],
        [pl.BlockSpec((tc, blk), rev), pl.BlockSpec((tc, blk), rev)],
        [jax.ShapeDtypeStruct((seq, d), F32), jax.ShapeDtypeStruct((seq, d), BF16)],
        [z, sv, wa, wx, xc, h_f],
        scratch=[pltpu.VMEM((tc, blk), F32), pltpu.VMEM((tc, blk), F32), pltpu.VMEM((SUBLANES, blk), F32)])


def _merge(y_pool, w_pu, y_lru, w_lu, z, seq, d, tm, tn):
    n_n = d // tn

    def kern(yp_ref, wp_ref, yl_ref, wl_ref, la_ref, lb_ref, m_ref, pa_ref, pb_ref):
        pa = jnp.dot(yp_ref[...], wp_ref[...], preferred_element_type=F32)
        pb = jnp.dot(yl_ref[...], wl_ref[...], preferred_element_type=F32)
        m_ref[...] = (_sig(la_ref[...]) * pa + _sig(lb_ref[...]) * pb).astype(BF16)
        pa_ref[...] = pa.astype(BF16)
        pb_ref[...] = pb.astype(BF16)

    row = pl.BlockSpec((tm, d), lambda i, j: (i, 0))
    wcol = pl.BlockSpec((d, tn), lambda i, j: (0, j))
    out = pl.BlockSpec((tm, tn), lambda i, j: (i, j))
    sh = jax.ShapeDtypeStruct((seq, d), BF16)
    return _pcall(
        "merge", kern, (seq // tm, n_n),
        [row, wcol, row, wcol,
         pl.BlockSpec((tm, tn), lambda i, j: (i, 3 * n_n + j)),
         pl.BlockSpec((tm, tn), lambda i, j: (i, 4 * n_n + j))],
        [out, out, out], [sh, sh, sh], [y_pool, w_pu, y_lru, w_lu, z, z])


def _ln_fwd(s, g, b):
    mu = jnp.mean(s, axis=-1, keepdims=True)
    xc = s - mu
    var = jnp.mean(xc * xc, axis=-1, keepdims=True)
    rstd = lax.rsqrt(var + LN_EPS)
    xhat = xc * rstd
    return xhat, rstd, xhat * g + b


def _ln_bwd(dy, xhat, rstd, g):
    dyg = dy * g
    m1 = jnp.mean(dyg, axis=-1, keepdims=True)
    m2 = jnp.mean(dyg * xhat, axis=-1, keepdims=True)
    return rstd * (dyg - m1 - xhat * m2)


def _pool_bwd(z, dy_pool, pw, scale, dz, seq, d, tc):
    n_g = pw.shape[0]
    pg = d // n_g
    n_chunks = seq // tc

    def kern(z_ref, dy_ref, pw_ref, sc_ref, dz_in, dz_ref, dpw_ref, dsc_ref):
        del dz_in
        g, c = pl.program_id(0), pl.program_id(1)
        hw = jnp.left_shift(1, g)
        uext = _ext(z_ref, c, n_chunks, tc, seq)
        dd = _pool_d(uext, g, c, tc, seq).astype(BF16)
        pwv = pw_ref[...]
        q = jnp.dot(dd, pwv, preferred_element_type=F32)
        dyext = _ext(dy_ref, c, n_chunks, tc, seq)

        @pl.when(c == 0)
        def _():
            dsc_ref[...] = jnp.zeros_like(dsc_ref)
            dpw_ref[...] = jnp.zeros_like(dpw_ref)

        dsc_ref[0:1, :] += _colsum(dyext[HALO:HALO + tc] * q)
        dqext = (dyext * sc_ref[...]).astype(BF16)
        dpw_ref[...] += _dot("tn", dd, dqext[HALO:HALO + tc])
        ddext = _dot("nt", dqext, pwv)
        text = c * tc - HALO + lax.broadcasted_iota(jnp.int32, (tc + 2 * HALO, 1), 0)
        v = ddext / jnp.maximum(_win_cnt(text, hw, seq), 1.0)
        dz_ref[...] = (_win_sum(v, g, 1, tc) - ddext[HALO:HALO + tc]).astype(BF16)

    return _pcall(
        "pool_bwd", kern, (n_g, n_chunks),
        [pl.BlockSpec((seq, pg), lambda g, c: (0, g)),
         pl.BlockSpec((seq, pg), lambda g, c: (0, g)),
         pl.BlockSpec((None, pg, pg), lambda g, c: (g, 0, 0)),
         pl.BlockSpec((1, pg), lambda g, c: (0, g)),
         pl.BlockSpec(memory_space=pl.ANY)],
        [pl.BlockSpec((None, tc, pg), lambda g, c: (2, c, g)),
         pl.BlockSpec((None, pg, pg), lambda g, c: (g, 0, 0)),
         pl.BlockSpec((SUBLANES, pg), lambda g, c: (0, g))],
        [jax.ShapeDtypeStruct(dz.shape, dz.dtype),
         jax.ShapeDtypeStruct((n_g, pg, pg), F32),
         jax.ShapeDtypeStruct((SUBLANES, d), F32)],
        [z, dy_pool, pw, scale, dz], aliases={4: 0})


def _lru_bwd(direction, xc, dh, h_dir, sv, wa, wx, dxc_prev, seq, d, tc):
    reverse = direction == 1
    n_h, blk = wa.shape[1], wa.shape[2]
    n_chunks = seq // tc
    has_prev = dxc_prev is not None

    def kern(*refs):
        xc_ref, dh_ref, h_ref, sv_ref, wa_ref, wx_ref = refs[:6]
        p = 6
        prev_ref = None
        if has_prev:
            prev_ref = refs[p]
            p += 1
        dxc_ref, dwa_ref, dwx_ref, sm_ref, at_s, g_s, carry, acarry = refs[p:p + 8]
        c = pl.program_id(1)
        cr = c if reverse else n_chunks - 1 - c
        c0 = pl.multiple_of(cr * tc, tc)

        @pl.when(c == 0)
        def _():
            carry[...] = jnp.zeros_like(carry)
            acarry[...] = jnp.zeros_like(acarry)
            dwa_ref[...] = jnp.zeros_like(dwa_ref)
            dwx_ref[...] = jnp.zeros_like(dwx_ref)
            sm_ref[...] = jnp.zeros_like(sm_ref)

        xc = xc_ref[...]
        wav, wxv = wa_ref[...], wx_ref[...]
        lam = sv_ref[SV_LAM + direction:SV_LAM + direction + 1, :]
        r, i, k, a, s = _lru_gates(xc, wav, wxv, sv_ref[SV_BA + direction:SV_BA + direction + 1, :],
                                   sv_ref[SV_BX + direction:SV_BX + direction + 1, :], lam)
        rowi = lax.broadcasted_iota(jnp.int32, (tc, blk), 0)
        hbody = h_ref[pl.ds(c0, tc), :]
        if not reverse:
            p0 = pl.multiple_of(jnp.maximum(c0 - SUBLANES, 0), SUBLANES)
            edge = jnp.where(cr > 0, h_ref[pl.ds(p0, SUBLANES), :][SUBLANES - 1:SUBLANES, :], 0.0)
            hprev = jnp.where(rowi == 0, edge, pltpu.roll(hbody, 1, 0))
            at = jnp.where(rowi == tc - 1, acarry[0:1, :], pltpu.roll(a, tc - 1, 0))
        else:
            n0 = pl.multiple_of(jnp.minimum(c0 + tc, seq - SUBLANES), SUBLANES)
            edge = jnp.where(cr < n_chunks - 1, h_ref[pl.ds(n0, SUBLANES), :][0:1, :], 0.0)
            hprev = jnp.where(rowi == tc - 1, edge, pltpu.roll(hbody, tc - 1, 0))
            at = jnp.where(rowi == 0, acarry[0:1, :], pltpu.roll(a, 1, 0))
        at_s[...] = at
        _scan_tiles(at_s, dh_ref, g_s, carry, tc // SUBLANES, not reverse)
        acarry[0:1, :] = a[tc - 1:tc, :] if reverse else a[0:1, :]

        gt = g_s[...]
        da = gt * hprev
        di = gt * s * xc
        dxc = gt * s * i
        ds = gt * (i * xc)
        dl = da * a - ds * (a * a) / s
        dpr = (dl * k) * r * (1.0 - r)
        dpi = di * i * (1.0 - i)
        sm_ref[0:1, :] += _colsum(dpr)
        sm_ref[1:2, :] += _colsum(dpi)
        sm_ref[2:3, :] += _colsum(dl * r) * (LRU_C * _sig(-lam))
        xb, dprb, dpib = xc.astype(BF16), dpr.astype(BF16), dpi.astype(BF16)
        dwa_ref[...] += _dot("tn", xb, dprb)
        dwx_ref[...] += _dot("tn", xb, dpib)
        dxc = dxc + _dot("nt", dprb, wav) + _dot("nt", dpib, wxv)
        if has_prev:
            dxc = dxc + prev_ref[...]
        dxc_ref[...] = dxc

    if reverse:
        chunk = lambda h, c: (c, h)
    else:
        chunk = lambda h, c: (n_chunks - 1 - c, h)
    wspec = pl.BlockSpec((None, None, blk, blk), lambda h, c: (direction, h, 0, 0))
    ins = [xc, dh, h_dir, sv, wa, wx] + ([dxc_prev] if has_prev else [])
    in_specs = [pl.BlockSpec((tc, blk), chunk), pl.BlockSpec((tc, blk), chunk),
                pl.BlockSpec((seq, blk), lambda h, c: (0, h)),
                pl.BlockSpec((16, blk), lambda h, c: (0, h)), wspec, wspec]
    if has_prev:
        in_specs.append(pl.BlockSpec((tc, blk), chunk))
    return _pcall(
        "lru_bwd_%d" % direction, kern, (n_h, n_chunks), in_specs,
        [pl.BlockSpec((tc, blk), chunk),
         pl.BlockSpec((None, blk, blk), lambda h, c: (h, 0, 0)),
         pl.BlockSpec((None, blk, blk), lambda h, c: (h, 0, 0)),
         pl.BlockSpec((SUBLANES, blk), lambda h, c: (0, h))],
        [jax.ShapeDtypeStruct((seq, d), F32),
         jax.ShapeDtypeStruct((n_h, blk, blk), F32),
         jax.ShapeDtypeStruct((n_h, blk, blk), F32),
         jax.ShapeDtypeStruct((SUBLANES, d), F32)],
        ins,
        scratch=[pltpu.VMEM((tc, blk), F32), pltpu.VMEM((tc, blk), F32),
                 pltpu.VMEM((SUBLANES, blk), F32), pltpu.VMEM((SUBLANES, blk), F32)])


def _conv_bwd(z, dxc, sv, dz, seq, d, tc, tcol):
    n_chunks = seq // tc
    lru_off = d // tcol

    def kern(z_ref, dx_ref, sv_ref, dz_in, dz_ref, sm_ref):
        del dz_in
        c = pl.program_id(1)
        uext = _ext(z_ref, c, n_chunks, tc, seq)
        dext = _ext(dx_ref, c, n_chunks, tc, seq)
        dbody = dext[HALO:HALO + tc]

        @pl.when(c == 0)
        def _():
            sm_ref[...] = jnp.zeros_like(sm_ref)

        du = jnp.zeros_like(dbody)
        for k in range(4):
            du = du + _shifted(dext, 2 - k, tc) * sv_ref[SV_CONV + k:SV_CONV + k + 1, :]
            sm_ref[k:k + 1, :] += _colsum(dbody * _shifted(uext, k - 2, tc))
        sm_ref[4:5, :] += _colsum(dbody)
        dz_ref[...] = du.astype(BF16)

    return _pcall(
        "conv_bwd", kern, (d // tcol, n_chunks),
        [pl.BlockSpec((seq, tcol), lambda j, c: (0, lru_off + j)),
         pl.BlockSpec((seq, tcol), lambda j, c: (0, j)),
         pl.BlockSpec((16, tcol), lambda j, c: (0, j)),
         pl.BlockSpec(memory_space=pl.ANY)],
        [pl.BlockSpec((None, tc, tcol), lambda j, c: (3, c, j)),
         pl.BlockSpec((SUBLANES, tcol), lambda j, c: (0, j))],
        [jax.ShapeDtypeStruct(dz.shape, dz.dtype), jax.ShapeDtypeStruct((SUBLANES, d), F32)],
        [z, dxc, sv, dz], aliases={3: 0})


BIG = (("w_in", "col"), ("pool_w", "row"), ("lru_wa", "row"), ("lru_wx", "row"), ("w_pool_up", "row"),
       ("w_lru_up", "row"), ("w_out", "row"), ("w_ff1", "col"), ("w_ff2", "row"))


def _shard_view(w, fam):
    if fam == "col":
        return w.reshape(w.shape[-2:])
    return w.reshape((-1,) + w.shape[-2:])


def _full_shape(sv_shape, fam):
    if fam == "col":
        return (sv_shape[0], N_DEV * sv_shape[1])
    return (sv_shape[0], N_DEV * sv_shape[1], sv_shape[2])


def _slot(ref, fam, p, size):
    if fam == "lead":
        return ref.at[p]
    start = pl.multiple_of(p * size, size)
    if fam == "col":
        return ref.at[:, pl.ds(start, size)]
    return ref.at[:, pl.ds(start, size), :]


def _coords():
    return lax.axis_index("x"), lax.axis_index("y"), lax.axis_index("c")


def _ag_job(shards, fams):
    n = len(shards)
    fulls = []
    for s, fam in zip(shards, fams):
        if fam == "lead":
            fulls.append(jax.ShapeDtypeStruct((N_DEV,) + s.shape, s.dtype))
        else:
            fulls.append(jax.ShapeDtypeStruct(_full_shape(s.shape, fam), s.dtype))
    sizes = [1 if fam == "lead" else s.shape[1] for s, fam in zip(shards, fams)]

    def ctx(ins, outs, sems):
        send, recv, loc = sems
        x, y, c = _coords()
        chips = [(1 - x, y), (x, 1 - y), (1 - x, 1 - y)]

        def copy(a, k, owner, to, src=None):
            dst = _slot(outs[a], fams[a], owner, sizes[a])
            return pltpu.make_async_remote_copy(
                src_ref=dst if src is None else src, dst_ref=dst,
                send_sem=send.at[a, k], recv_sem=recv.at[a, k], device_id=to, device_id_type=MESH)

        def local(a):
            return pltpu.make_async_copy(ins[a], _slot(outs[a], fams[a], 4 * x + 2 * y + c, sizes[a]), loc.at[a])

        return x, y, c, chips, copy, local

    def start(ins, outs, sems):
        x, y, c, chips, copy, local = ctx(ins, outs, sems)
        me = 4 * x + 2 * y + c
        for a in range(n):
            local(a).start()
            copy(a, 0, me, (x, y, 1 - c), ins[a]).start()
            for j, (cx, cy) in enumerate(chips):
                copy(a, 1 + j, me, (cx, cy, c), ins[a]).start()

    def mid(ins, outs, sems):
        x, y, c, chips, copy, _ = ctx(ins, outs, sems)
        for a in range(n):
            for j, (cx, cy) in enumerate(chips):
                owner = 4 * cx + 2 * cy + c
                copy(a, 1 + j, owner, (x, y, c)).wait_recv()
                copy(a, 4 + j, owner, (x, y, 1 - c)).start()

    def finish(ins, outs, sems):
        x, y, c, chips, copy, local = ctx(ins, outs, sems)
        me = 4 * x + 2 * y + c
        for a in range(n):
            copy(a, 0, 4 * x + 2 * y + (1 - c), (x, y, c)).wait_recv()
            for j, (cx, cy) in enumerate(chips):
                copy(a, 4 + j, 4 * cx + 2 * cy + (1 - c), (x, y, c)).wait_recv()
            copy(a, 0, me, (x, y, 1 - c), ins[a]).wait_send()
            for j, (cx, cy) in enumerate(chips):
                copy(a, 1 + j, me, (cx, cy, c), ins[a]).wait_send()
                copy(a, 4 + j, 4 * cx + 2 * cy + c, (x, y, 1 - c)).wait_send()
            local(a).wait()

    sems = [pltpu.SemaphoreType.DMA((n, 7)), pltpu.SemaphoreType.DMA((n, 7)), pltpu.SemaphoreType.DMA((n,))]
    return _Job(shards, fulls, sems, start, finish, mid)


def _rs_sibling_job(fulls, fams, sizes):
    n = len(fulls)
    outs = []
    for f, fam, sz in zip(fulls, fams, sizes):
        if fam == "col":
            outs.append(jax.ShapeDtypeStruct((4, f.shape[0], sz), f.dtype))
        else:
            outs.append(jax.ShapeDtypeStruct((4, f.shape[0], sz, f.shape[2]), f.dtype))

    def copies(ins, rcv, sems):
        send, recv = sems
        x, y, c = _coords()
        return [pltpu.make_async_remote_copy(
            src_ref=_slot(ins[a], fams[a], 2 * q + (1 - c), sizes[a]), dst_ref=rcv[a].at[q],
            send_sem=send.at[a, q], recv_sem=recv.at[a, q], device_id=(x, y, 1 - c), device_id_type=MESH)
            for a in range(n) for q in range(4)]

    def start(ins, rcv, sems):
        for cp in copies(ins, rcv, sems):
            cp.start()

    def finish(ins, rcv, sems):
        for cp in copies(ins, rcv, sems):
            cp.wait()

    return _Job(fulls, outs, [pltpu.SemaphoreType.DMA((n, 4)), pltpu.SemaphoreType.DMA((n, 4))], start, finish)


def _rs_chips_job(parts):
    n = len(parts)
    outs = [jax.ShapeDtypeStruct((3,) + p.shape[1:], p.dtype) for p in parts]

    def copies(ins, rcv, sems):
        send, recv = sems
        x, y, c = _coords()
        cps = []
        for a in range(n):
            for r in (1, 2, 3):
                tx, ty = (1 - x) if r & 2 else x, (1 - y) if r & 1 else y
                cps.append(pltpu.make_async_remote_copy(
                    src_ref=ins[a].at[2 * tx + ty], dst_ref=rcv[a].at[r - 1],
                    send_sem=send.at[a, r - 1], recv_sem=recv.at[a, r - 1],
                    device_id=(tx, ty, c), device_id_type=MESH))
        return cps

    def start(ins, rcv, sems):
        for cp in copies(ins, rcv, sems):
            cp.start()

    def finish(ins, rcv, sems):
        for cp in copies(ins, rcv, sems):
            cp.wait()

    return _Job(parts, outs, [pltpu.SemaphoreType.DMA((n, 3)), pltpu.SemaphoreType.DMA((n, 3))], start, finish)


def _tile_rows(rows, cols):
    tr = rows
    while tr * cols > (1 << 18) and tr % (2 * SUBLANES) == 0:
        tr //= 2
    return tr


def _rs_add(name, full, recv_a, fam, size, cidx):
    if fam == "col":
        rows = full.shape[0]
        tr = _tile_rows(rows, size)
        grid = (4, rows // tr)
        f_spec = pl.BlockSpec((tr, size), lambda q, i, cr: (i, 2 * q + cr[0]))
        s_spec = pl.BlockSpec((None, tr, size), lambda q, i, cr: (q, i, 0))
    else:
        nb, cols = full.shape[0], full.shape[2]
        tr = _tile_rows(nb * size, cols) // nb if nb > 1 else _tile_rows(size, cols)
        nt = size // tr
        grid = (4, nt)
        f_spec = pl.BlockSpec((nb, tr, cols), lambda q, i, cr: (0, (2 * q + cr[0]) * nt + i, 0))
        s_spec = pl.BlockSpec((None, nb, tr, cols), lambda q, i, cr: (q, 0, i, 0))

    token = [t for t in _ORDER if t is not full]

    def kern(c_ref, f_ref, r_ref, *rest):
        del c_ref
        rest[-1][...] = (f_ref[...] + r_ref[...]).astype(BF16)

    out = pl.pallas_call(
        kern, name=name,
        grid_spec=pltpu.PrefetchScalarGridSpec(num_scalar_prefetch=1, grid=grid,
                                               in_specs=[f_spec, s_spec] + [HBM_SPEC] * len(token), out_specs=s_spec),
        out_shape=jax.ShapeDtypeStruct(recv_a.shape, BF16),
        compiler_params=_cparams(("arbitrary", "arbitrary"), 32),
    )(cidx, full, recv_a, *token)
    _ORDER[:] = [out]
    return out


def _adam(w, g, m, v):
    m2 = ADAM_B1 * m + (1.0 - ADAM_B1) * g
    v2 = ADAM_B2 * v + (1.0 - ADAM_B2) * (g * g)
    m_hat = m2 / (1.0 - ADAM_B1 ** ADAM_STEP)
    v_hat = v2 / (1.0 - ADAM_B2 ** ADAM_STEP)
    delta = -ADAM_LR * (m_hat / (jnp.sqrt(v_hat) + ADAM_EPS) + ADAM_WD * w)
    return delta, m2, v2


def _rs_final_adam(name, parts, recv_b, w, m, v, fam, qidx):
    shp = w.shape
    pieces = parts if isinstance(parts, (list, tuple)) else [parts]
    recvs = recv_b if isinstance(recv_b, (list, tuple)) else [recv_b]
    n_p = len(pieces)
    first_blk = [0] * n_p
    if fam == "col":
        rows, cols = shp
        tr = _tile_rows(min(p.shape[1] for p in pieces), cols)
        per = [p.shape[1] // tr for p in pieces]
        for h in range(1, n_p):
            first_blk[h] = first_blk[h - 1] + per[h - 1]
        grid = (rows // tr,)
        w_spec = pl.BlockSpec((tr, cols), lambda i, qr: (i, 0))

        def piece_row(i, h):
            return jnp.clip(i - first_blk[h], 0, per[h] - 1)

        p_specs = [pl.BlockSpec((None, tr, cols), lambda i, qr, h=h: (qr[0], piece_row(i, h), 0))
                   for h in range(n_p)]
        r_specs = [pl.BlockSpec((3, tr, cols), lambda i, qr, h=h: (0, piece_row(i, h), 0)) for h in range(n_p)]
    else:
        assert n_p == 1
        nb, rows, cols = shp
        tr = _tile_rows(rows, cols)
        nt = rows // tr
        grid = (nb * nt,)
        w_spec = pl.BlockSpec((None, tr, cols), lambda i, qr: (i // nt, i % nt, 0))
        p_specs = [pl.BlockSpec((None, None, tr, cols), lambda i, qr: (qr[0], i // nt, i % nt, 0))]
        r_specs = [pl.BlockSpec((3, None, tr, cols), lambda i, qr: (0, i // nt, i % nt, 0))]

    token = list(_ORDER)

    def kern(*refs):
        p_refs, r_refs = refs[1:1 + n_p], refs[1 + n_p:1 + 2 * n_p]
        w_ref, m_ref, v_ref = refs[1 + 2 * n_p:4 + 2 * n_p]
        g_out, d_out, m_out, v_out = refs[4 + 2 * n_p + len(token):]

        def total(h):
            p_ref, r_ref = p_refs[h], r_refs[h]
            return ((p_ref[...].astype(F32) + r_ref[0].astype(F32)) + r_ref[1].astype(F32)) + r_ref[2].astype(F32)

        g = total(0)
        for h in range(1, n_p):
            g = jnp.where(pl.program_id(0) >= first_blk[h], total(h), g)
        delta, m2, v2 = _adam(w_ref[...], g, m_ref[...], v_ref[...])
        g_out[...] = g
        d_out[...] = delta
        m_out[...] = m2
        v_out[...] = v2

    sh = jax.ShapeDtypeStruct(shp, F32)
    res = pl.pallas_call(
        kern, name=name,
        grid_spec=pltpu.PrefetchScalarGridSpec(
            num_scalar_prefetch=1, grid=grid,
            in_specs=p_specs + r_specs + [w_spec, w_spec, w_spec] + [HBM_SPEC] * len(token),
            out_specs=[w_spec] * 4),
        out_shape=[sh] * 4,
        compiler_params=_cparams(("arbitrary",), 32),
    )(qidx, *pieces, *recvs, w, m, v, *token)
    _ORDER[:] = [res[1]]
    return res


def _sum8(name, parts):
    def kern(p_ref, o_ref):
        acc = p_ref[0]
        for p in range(1, N_DEV):
            acc = acc + p_ref[p]
        o_ref[...] = acc

    return pl.pallas_call(
        kern, name=name, out_shape=jax.ShapeDtypeStruct(parts.shape[1:], F32),
        compiler_params=pltpu.CompilerParams(vmem_limit_bytes=32 << 20),
    )(parts)


def _adam_small(name, w, g, m, v):
    def kern(w_ref, g_ref, m_ref, v_ref, d_out, m_out, v_out):
        delta, m2, v2 = _adam(w_ref[...], g_ref[...], m_ref[...], v_ref[...])
        d_out[...] = delta
        m_out[...] = m2
        v_out[...] = v2

    sh = jax.ShapeDtypeStruct(w.shape, F32)
    return pl.pallas_call(kern, name=name, out_shape=[sh] * 3)(w, g, m, v)


class _NoComm:
    def __init__(self):
        self.grads = {}

    def settled_before(self, kernel_name):
        return []

    def after(self, kernel_name):
        pass

    def grad(self, name, g):
        self.grads[name] = g


def _local_step(x, target, wts, vec, comm=None):
    comm = comm or _NoComm()
    _ORDER[:] = []
    seq, d = x.shape
    sv = vec["sv"]
    ff = vec["b_ff1"].shape[1]
    n_in = 5 * d

    def issue(name, call):
        _ORDER.extend(comm.settled_before(name))
        res = call()
        comm.after(name)
        return res

    tc = min(1024, seq)
    t1k, t512, t256 = min(1024, seq), min(512, seq), min(256, seq)
    n512 = min(512, d)
    tkd = d

    x_bf = x.astype(BF16)
    full = lambda i, j, k: (0, 0)

    def epi_store(acc, i, j, ex, outs):
        outs[0][...] = acc

    n_pass = W_IN_PASSES
    piece = n_in // (N_DEV * n_pass)
    tz = min(4096, seq)
    z = None
    for k in range(n_pass):
        w_piece = wts["w_in_piece_%d" % k]
        prev = [] if z is None else [(z, (None, None))]
        (z,) = issue("z_proj_%d" % k, lambda: _mm(
            "z_proj_%d" % k, "nn", (seq // tz, N_DEV, 1),
            x_bf, ((tz, d), lambda i, j, kk: (i, 0)), w_piece, ((d, piece), lambda i, j, kk: (0, j)),
            prev, [jax.ShapeDtypeStruct((seq, n_in), F32)],
            [((tz, piece), lambda i, j, kk, k=k: (i, n_pass * j + k))], epi_store,
            aliases={2: 0} if prev else None))

    pool_w, wa, wx = wts["pool_w"], wts["lru_wa"], wts["lru_wx"]
    blk = wa.shape[2]
    y_pool = issue("pool_fwd", lambda: _pool_fwd(z, pool_w, vec["pool_scale"], seq, d, tc))
    xc, h_f = issue("lru_fwd", lambda: _lru_fwd(z, sv, vec["conv_b"], wa, wx, seq, d, tc))
    h_b, y_lru = issue("lru_rev", lambda: _lru_rev(z, sv, wa, wx, xc, h_f, seq, d, tc))
    w_pu, w_lu = wts["w_pool_up"], wts["w_lru_up"]
    m_bf, p_a, p_b = issue("merge", lambda: _merge(y_pool, w_pu, y_lru, w_lu, z, seq, d, t1k, n512))
    w_out = wts["w_out"]

    def epi_ln1(acc, i, j, ex, outs):
        x_ref, bo, g1, b1 = ex
        s1 = DN_ALPHA * x_ref[...] + (acc + bo[...])
        xhat, rstd, x1 = _ln_fwd(s1, g1[...], b1[...])
        outs[0][...] = xhat
        outs[1][...] = x1.astype(BF16)
        outs[2][...] = rstd

    rowd = lambda t: ((t, d), lambda i, j, k: (i, 0))
    vecd = ((1, d), full)
    xhat1, x1_bf, rstd1 = issue("out_ln1", lambda: _mm(
        "out_ln1", "nn", (seq // t256, 1, 1), m_bf, rowd(t256), w_out, ((d, d), full),
        [(x, rowd(t256)), (vec["b_out"], vecd), (vec["ln1_g"], vecd), (vec["ln1_b"], vecd)],
        [jax.ShapeDtypeStruct((seq, d), F32), jax.ShapeDtypeStruct((seq, d), BF16),
         jax.ShapeDtypeStruct((seq, 1), F32)],
        [rowd(t256), rowd(t256), ((t256, 1), lambda i, j, k: (i, 0))], epi_ln1))
    w1 = wts["w_ff1"]

    def epi_ff1(acc, i, j, ex, outs):
        r = jnp.maximum(acc + ex[0][...], 0.0)
        outs[0][...] = r.astype(BF16)
        outs[1][...] = (r * r).astype(BF16)

    nf = min(1024, ff)
    tile_f = ((t1k, nf), lambda i, j, k: (i, j))
    relu_h, hdn = issue("ff1", lambda: _mm(
        "ff1", "nn", (seq // t1k, ff // nf, 1), x1_bf, rowd(t1k), w1, ((d, nf), lambda i, j, k: (0, j)),
        [(vec["b_ff1"], ((1, nf), lambda i, j, k: (0, j)))],
        [jax.ShapeDtypeStruct((seq, ff), BF16)] * 2, [tile_f, tile_f], epi_ff1))
    w2 = wts["w_ff2"]

    def epi_ln2(acc, i, j, ex, outs):
        xh1, tgt, g1, b1, bf2, g2, b2 = ex
        ds_ref, dsb_ref, sm_ref, loss_ref = outs
        x1 = xh1[...] * g1[...] + b1[...]
        s2 = DN_ALPHA * x1 + (acc + bf2[...])
        xhat, rstd, y = _ln_fwd(s2, g2[...], b2[...])
        e = y - tgt[...]
        part = 0.5 * jnp.sum(jnp.mean(e * e, axis=-1, keepdims=True))
        dy = e * (1.0 / d)
        ds2 = _ln_bwd(dy, xhat, rstd, g2[...])
        ds_ref[...] = ds2
        dsb_ref[...] = ds2.astype(BF16)
        sm_ref[0:1, :] += _colsum(dy * xhat)
        sm_ref[1:2, :] += _colsum(dy)
        sm_ref[2:3, :] += _colsum(ds2)
        loss_ref[...] += jnp.full(loss_ref.shape, part, F32)

    def zero_tail(n_tail):
        def init(i, j, outs):
            @pl.when(i == 0)
            def _():
                for o in outs[-n_tail:]:
                    o[...] = jnp.zeros_like(o)
        return init

    def rows_epilogue(name, lead, extras, out_shapes, out_specs, epi, n_tail):
        ne = len(extras)

        def kern(*refs):
            i = pl.program_id(0)
            outs = refs[1 + ne:]
            zero_tail(n_tail)(i, 0, outs)
            epi(refs[0][...], i, 0, refs[1:1 + ne], outs)

        def spec(s):
            bs, f = s
            return pl.BlockSpec(bs, lambda i, f=f: f(i, 0, 0))

        return _pcall(name, kern, (seq // t256,), [spec(rowd(t256))] + [spec(s) for _, s in extras],
                      [spec(s) for s in out_specs], out_shapes, [lead] + [e for e, _ in extras])

    tkw2, tw = min(2048, ff), min(1024, d)
    (ff_out,) = _mm("ff2", "nn", (seq // t1k, d // tw, ff // tkw2), hdn, ((t1k, tkw2), lambda i, j, k: (i, k)),
                    w2, ((tkw2, tw), lambda i, j, k: (k, j)), [], [jax.ShapeDtypeStruct((seq, d), F32)],
                    [((t1k, tw), lambda i, j, k: (i, j))], None)
    ds2, ds2_bf, sm_ln2, loss_blk = issue("ff2_ln2", lambda: rows_epilogue(
        "ln2_loss", ff_out,
        [(xhat1, rowd(t256)), (target, rowd(t256)), (vec["ln1_g"], vecd), (vec["ln1_b"], vecd),
         (vec["b_ff2"], vecd), (vec["ln2_g"], vecd), (vec["ln2_b"], vecd)],
        [jax.ShapeDtypeStruct((seq, d), F32), jax.ShapeDtypeStruct((seq, d), BF16),
         jax.ShapeDtypeStruct((SUBLANES, d), F32), jax.ShapeDtypeStruct((SUBLANES, 128), F32)],
        [rowd(t256), rowd(t256), ((SUBLANES, d), full), ((SUBLANES, 128), full)], epi_ln2, 2))

    tkw = min(2048, seq)

    def dw(name, wname, a, b, m_dim, n_dim, b_spec=None, row0=0):
        _ORDER.extend(comm.settled_before(name))
        tn = min(1024, n_dim)
        tm = next(t for t in (1024, 768, 512, 256, 128) if m_dim % t == 0 and row0 % t == 0)
        b_spec = b_spec or ((tkw, tn), lambda i, j, k: (k, j))
        i0 = row0 // tm
        (out,) = _mm(
            name, "tn", (m_dim // tm, n_dim // tn, seq // tkw),
            a, ((tkw, tm), lambda i, j, k: (k, i0 + i)), b, b_spec, [],
            [jax.ShapeDtypeStruct((m_dim, n_dim), F32)], [((tm, tn), lambda i, j, k: (i, j))],
            epi_store if seq == tkw else None)
        comm.grad(wname, out)
        comm.after(name)

    dw("dw_ff2", "w_ff2", hdn, ds2_bf, ff, d)

    def epi_dpre(acc, i, j, ex, outs):
        dpre = acc * (2.0 * ex[0][...].astype(F32))
        outs[0][...] = dpre.astype(BF16)

        @pl.when(i == 0)
        def _():
            outs[1][...] = jnp.zeros_like(outs[1])

        outs[1][0:1, :] += _colsum(dpre)

    dpre, sm_bff1 = issue("dhdn", lambda: _mm(
        "dhdn", "nt", (seq // t1k, ff // nf, 1), ds2_bf, rowd(t1k), w2, ((nf, d), lambda i, j, k: (j, 0)),
        [(relu_h, tile_f)],
        [jax.ShapeDtypeStruct((seq, ff), BF16), jax.ShapeDtypeStruct((SUBLANES, ff), F32)],
        [tile_f, ((SUBLANES, nf), lambda i, j, k: (0, j))], epi_dpre, order="ji"))

    dw("dw_ff1", "w_ff1", x1_bf, dpre, d, ff)

    def epi_ln1b(acc, i, j, ex, outs):
        ds2_ref, xh1, rs1, g1 = ex
        ds_ref, dsb_ref, sm_ref = outs
        dy1 = acc + DN_ALPHA * ds2_ref[...]
        xhat = xh1[...]
        ds1 = _ln_bwd(dy1, xhat, rs1[...], g1[...])
        ds_ref[...] = ds1
        dsb_ref[...] = ds1.astype(BF16)
        sm_ref[0:1, :] += _colsum(dy1 * xhat)
        sm_ref[1:2, :] += _colsum(dy1)
        sm_ref[2:3, :] += _colsum(ds1)

    (dx1,) = _mm("dx1", "nt", (seq // t1k, d // tw, ff // tkw2), dpre, ((t1k, tkw2), lambda i, j, k: (i, k)),
                 w1, ((tw, tkw2), lambda i, j, k: (j, k)), [], [jax.ShapeDtypeStruct((seq, d), F32)],
                 [((t1k, tw), lambda i, j, k: (i, j))], None)
    ds1, ds1_bf, sm_ln1 = issue("dx1_ln1", lambda: rows_epilogue(
        "ln1_bwd", dx1,
        [(ds2, rowd(t256)), (xhat1, rowd(t256)), (rstd1, ((t256, 1), lambda i, j, k: (i, 0))),
         (vec["ln1_g"], vecd)],
        [jax.ShapeDtypeStruct((seq, d), F32), jax.ShapeDtypeStruct((seq, d), BF16),
         jax.ShapeDtypeStruct((SUBLANES, d), F32)],
        [rowd(t256), rowd(t256), ((SUBLANES, d), full)], epi_ln1b, 1))

    dw("dw_out", "w_out", m_bf, ds1_bf, d, d)
    nd = min(1024, d)
    n_n = d // nd
    tile_d = ((t512, nd), lambda i, j, k: (i, j))

    def epi_dm(acc, i, j, ex, outs):
        la, lb, pa, pb = ex
        ga, gb = _sig(la[...]), _sig(lb[...])
        outs[0][...] = (acc * ga).astype(BF16)
        outs[1][...] = (acc * gb).astype(BF16)
        outs[2][0] = (acc * pa[...].astype(F32) * ga * (1.0 - ga)).astype(BF16)
        outs[2][1] = (acc * pb[...].astype(F32) * gb * (1.0 - gb)).astype(BF16)

    dp_a, dp_b, dz = _mm(
        "dm", "nt", (seq // t512, n_n, 1), ds1_bf, rowd(t512), w_out, ((nd, d), lambda i, j, k: (j, 0)),
        [(z, ((t512, nd), lambda i, j, k: (i, 3 * n_n + j))),
         (z, ((t512, nd), lambda i, j, k: (i, 4 * n_n + j))), (p_a, tile_d), (p_b, tile_d)],
        [jax.ShapeDtypeStruct((seq, d), BF16), jax.ShapeDtypeStruct((seq, d), BF16),
         jax.ShapeDtypeStruct((5, seq, d), BF16)],
        [tile_d, tile_d, ((2, t512, nd), lambda i, j, k: (0, i, j))], epi_dm)

    dw("dw_pool_up", "w_pool_up", y_pool, dp_a, d, d)
    dw("dw_lru_up", "w_lru_up", y_lru, dp_b, d, d)

    def epi_bf(acc, i, j, ex, outs):
        outs[0][...] = acc.astype(BF16)

    (dy_pool,) = issue("dy_pool", lambda: _mm(
        "dy_pool", "nt", (seq // t512, n_n, 1), dp_a, rowd(t512), w_pu, ((nd, d), lambda i, j, k: (j, 0)), [],
        [jax.ShapeDtypeStruct((seq, d), BF16)], [tile_d], epi_bf))

    def epi_dylru(acc, i, j, ex, outs):
        hf, hb, ug, _ = ex
        u = ug[...]
        outs[0][...] = acc * _gelu(u)
        outs[1][...] = (acc * (hf[...] + hb[...]) * _gelu_grad(u)).astype(BF16)

    dz_in = dz
    dh, dz = issue("dy_lru", lambda: _mm(
        "dy_lru", "nt", (seq // t512, n_n, 1), dp_b, rowd(t512), w_lu, ((nd, d), lambda i, j, k: (j, 0)),
        [(h_f, tile_d), (h_b, tile_d), (z, ((t512, nd), lambda i, j, k: (i, 2 * n_n + j))),
         (dz_in, (None, None))],
        [jax.ShapeDtypeStruct((seq, d), F32), jax.ShapeDtypeStruct(dz_in.shape, BF16)],
        [tile_d, ((None, t512, nd), lambda i, j, k: (4, i, j))], epi_dylru, aliases={5: 1}))

    dz, g_pw, sm_pool = _pool_bwd(z, dy_pool, pool_w, vec["pool_scale"], dz, seq, d, tc)
    comm.grad("pool_w", g_pw)
    dxc0, g_wa0, g_wx0, sm_l0 = issue("lru_bwd_0", lambda: _lru_bwd(
        0, xc, dh, h_f, sv, wa, wx, None, seq, d, tc))
    dxc, g_wa1, g_wx1, sm_l1 = issue("lru_bwd_1", lambda: _lru_bwd(
        1, xc, dh, h_b, sv, wa, wx, dxc0, seq, d, tc))
    comm.grad("lru_wa", jnp.concatenate([g_wa0, g_wa1], axis=0))
    comm.grad("lru_wx", jnp.concatenate([g_wx0, g_wx1], axis=0))
    dz, sm_conv = _conv_bwd(z, dxc, sv, dz, seq, d, tc, blk)

    tnw = min(1024, d)
    per_seg = d // tnw
    seg_spec = ((None, tkw, tnw), lambda i, j, k: ((j // per_seg + 2) % 5, k, j % per_seg))
    lo_rows = 3 * d // 4
    dw("dw_in_lo", "w_in_lo", x_bf, dz, lo_rows, n_in, b_spec=seg_spec)
    dw("dw_in_hi", "w_in_hi", x_bf, dz, d - lo_rows, n_in, b_spec=seg_spec, row0=lo_rows)

    nk = d // tkd

    def epi_dx(acc, i, j, ex, outs):
        outs[0][...] = acc + DN_ALPHA * ex[0][...]

    (grad_x,) = issue("dx", lambda: _mm(
        "dx", "nt", (seq // t512, 1, n_in // tkd), dz,
        ((None, t512, tkd), lambda i, j, k: ((k // nk + 2) % 5, i, k % nk)),
        wts["w_in"], ((d, tkd), lambda i, j, k: (0, k)), [(ds1, rowd(t512))],
        [jax.ShapeDtypeStruct((seq, d), F32)], [rowd(t512)], epi_dx, acc_shape=(t512, d)))

    small = {"ln2": sm_ln2, "b_ff1": sm_bff1, "ln1": sm_ln1, "pool": sm_pool, "lru0": sm_l0, "lru1": sm_l1,
             "conv": sm_conv}
    return loss_blk[0, 0], grad_x, small


REP = ("pool_scale", "conv_b", "b_out", "ln1_g", "ln1_b", "b_ff2", "ln2_g", "ln2_b")
SHARDED_SMALL = (("conv_w", 4), ("lru_ba", 2), ("lru_bx", 2), ("lru_lambda", 2))
WEIGHT_ORDER = ("w_in", "pool_w", "pool_scale", "conv_w", "conv_b", "lru_wa", "lru_ba", "lru_wx", "lru_bx",
                "lru_lambda", "w_pool_up", "w_lru_up", "w_out", "b_out", "ln1_g", "ln1_b", "w_ff1", "b_ff1",
                "w_ff2", "b_ff2", "ln2_g", "ln2_b")


def _pad_rows(a, rows):
    return jnp.concatenate([a, jnp.zeros((rows - a.shape[0], a.shape[1]), a.dtype)], axis=0)


def kernel(x, w_in, pool_w, pool_scale, conv_w, conv_b, lru_wa, lru_ba, lru_wx, lru_bx, lru_lambda, w_pool_up, w_lru_up, w_out, b_out, ln1_g, ln1_b, w_ff1, b_ff1, w_ff2, b_ff2, ln2_g, ln2_b, loss_target, m_w_in, m_pool_w, m_pool_scale, m_conv_w, m_conv_b, m_lru_wa, m_lru_ba, m_lru_wx, m_lru_bx, m_lru_lambda, m_w_pool_up, m_w_lru_up, m_w_out, m_b_out, m_ln1_g, m_ln1_b, m_w_ff1, m_b_ff1, m_w_ff2, m_b_ff2, m_ln2_g, m_ln2_b, v_w_in, v_pool_w, v_pool_scale, v_conv_w, v_conv_b, v_lru_wa, v_lru_ba, v_lru_wx, v_lru_bx, v_lru_lambda, v_w_pool_up, v_w_lru_up, v_w_out, v_b_out, v_ln1_g, v_ln1_b, v_w_ff1, v_b_ff1, v_w_ff2, v_b_ff2, v_ln2_g, v_ln2_b):
    args = dict(locals())
    w = {n: args[n] for n in WEIGHT_ORDER}
    mom = {n: args["m_" + n] for n in WEIGHT_ORDER}
    var = {n: args["v_" + n] for n in WEIGHT_ORDER}
    seq, d = x.shape[1], x.shape[2]
    n_heads, blk = lru_wa.shape[2], lru_wa.shape[4]
    n_groups = pool_w.shape[1]
    ff = b_ff1.shape[1]
    cx, cy, cc = _coords()
    me = 4 * cx + 2 * cy + cc
    cidx = jnp.reshape(cc, (1,)).astype(jnp.int32)
    qidx = jnp.reshape(2 * cx + cy, (1,)).astype(jnp.int32)

    fam_of = dict(BIG)
    sviews = {n: _shard_view(w[n], fam) for n, fam in BIG}
    size_of = {n: sviews[n].shape[1] for n, _ in BIG}
    for piece_name in ("w_in_lo", "w_in_hi"):
        fam_of[piece_name], size_of[piece_name] = fam_of["w_in"], size_of["w_in"]
    wts = {}

    def take_gathered(names, arrays):
        for n, g in zip(names, arrays):
            if n == "pool_w":
                g = g.reshape(n_groups, d // n_groups, d // n_groups)
            elif n in ("lru_wa", "lru_wx"):
                g = g.reshape(2, n_heads, blk, blk)
            elif fam_of[n] == "row":
                g = g.reshape(g.shape[1:])
            wts[n] = g

    shard_bf = {n: sviews[n].astype(BF16) for n, _ in BIG}
    piece = sviews["w_in"].shape[1] // W_IN_PASSES
    for k in range(W_IN_PASSES):
        name = "w_in_piece_%d" % k
        shard_bf[name] = shard_bf["w_in"][:, k * piece:(k + 1) * piece]
        fam_of[name] = "col"

    def gather_job(names, extra=()):
        return _ag_job([shard_bf[n] for n in names] + [e for e, _ in extra],
                       [fam_of[n] for n in names] + [f for _, f in extra])

    launched = [0]

    def on_sequencer(kind, job):
        launched[0] += 1
        return _sequencer_job("sq_%s_%d" % (kind, launched[0]), job, launched[0] % 2)

    class Plan:
        sibling = {"dw_ff2": ("w_ff2",), "dw_ff1": ("w_ff1",), "dw_lru_up": ("w_out", "w_pool_up", "w_lru_up"),
                   "dw_in_lo": ("w_in_lo",), "dw_in_hi": ("w_in_hi", "pool_w", "lru_wa", "lru_wx")}
        chips = {"dw_ff1": ("w_ff2",), "dw_out": ("w_ff1",), "dy_lru": ("w_out", "w_pool_up", "w_lru_up"),
                 "dw_in_hi": ("w_in_lo",), "dx": ("w_in_hi", "pool_w", "lru_wa", "lru_wx")}

        def __init__(self):
            self.grads, self.recv_a, self.parts, self.recv_b = {}, {}, {}, {}

        def grad(self, name, g):
            self.grads[name] = g if fam_of[name] == "col" else g.reshape((-1,) + g.shape[-2:])

        settle = {"dw_ff1": ("w_in",), "dw_out": ("w_ff2",), "dy_pool": ("w_ff1",), "lru_bwd_1": ("w_out",)}
        settle_add = {"dx": ("w_in_lo",)}

        def settled_before(self, host):
            return [wts[n] if n == "w_in" else self.recv_b[n] for n in self.settle.get(host, ())]

        def after(self, host):
            if host in self.chips:
                names = self.chips[host]
                _ORDER.extend(self.recv_b[n] for n in self.settle_add.get(host, ()))
                for n in names:
                    self.parts[n] = _rs_add("rs_add_" + n, self.grads[n], self.recv_a[n], fam_of[n], size_of[n], cidx)
                res = on_sequencer("chips", _rs_chips_job([self.parts[n] for n in names]))
                self.recv_b.update(zip(names, res))
            if host in self.sibling:
                names = self.sibling[host]
                res = on_sequencer("sibling", _rs_sibling_job(
                    [self.grads[n] for n in names], [fam_of[n] for n in names], [size_of[n] for n in names]))
                self.recv_a.update(zip(names, res))

    first = ("w_in_piece_0",)
    sv_shard = _pad_rows(jnp.concatenate([w[n].reshape(r, -1) for n, r in SHARDED_SMALL], axis=0), 16)
    gathered = on_sequencer("gather", gather_job(first, [(sv_shard, "col")]))
    take_gathered(first, gathered[:-1])
    vec = {n: w[n] for n in REP}
    vec["b_ff1"] = b_ff1
    vec["sv"] = gathered[-1]
    queue = [("w_in_piece_%d" % k,) for k in range(1, W_IN_PASSES)]
    queue += [("pool_w", "lru_wa", "lru_wx"), ("w_pool_up", "w_lru_up"), ("w_out",), ("w_ff1",), ("w_ff2",),
              ("w_in",)]
    for names in queue:
        take_gathered(names, on_sequencer("gather", gather_job(names)))

    plan = Plan()
    loss_part, grad_x, small = _local_step(x.reshape(seq, d), loss_target.reshape(seq, d), wts, vec, plan)

    out_g, out_d, out_m, out_v = {}, {}, {}, {}
    for n, fam in sorted(BIG, key=lambda nf: nf[0] in ("w_in", "pool_w", "lru_wa", "lru_wx")):
        halves = [n + "_lo", n + "_hi"] if n == "w_in" else [n]
        res = _rs_final_adam("adam_" + n, [plan.parts[h] for h in halves], [plan.recv_b[h] for h in halves],
                             sviews[n], _shard_view(mom[n], fam), _shard_view(var[n], fam), fam, qidx)
        out_g[n], out_d[n], out_m[n], out_v[n] = [r.reshape(w[n].shape) for r in res]

    loss_row = jnp.concatenate([loss_part.reshape(1, 1), jnp.zeros((1, d - 1), F32)], axis=1)
    rows = [small["pool"][0:1], small["conv"][4:5], small["ln1"][2:3], small["ln1"][0:1], small["ln1"][1:2],
            small["ln2"][2:3], small["ln2"][0:1], small["ln2"][1:2], small["b_ff1"][0:1].reshape(ff // d, d),
            small["conv"][0:4], small["lru0"][0:1], small["lru1"][0:1], small["lru0"][1:2], small["lru1"][1:2],
            small["lru0"][2:3], small["lru1"][2:3], loss_row]
    n_rep = len(REP) + ff // d
    n_rows = n_rep + sum(r for _, r in SHARDED_SMALL)
    pad_rows = -(-(n_rows + 1) // SUBLANES) * SUBLANES
    packed = _pad_rows(jnp.concatenate(rows, axis=0), pad_rows)
    (all_small,) = on_sequencer("gather", _ag_job([packed], ["lead"]))
    g_small = _sum8("sum_small", all_small)
    loss = g_small[n_rows, 0]

    def pack_rep(t):
        return jnp.concatenate([t[n] for n in REP] + [t["b_ff1"].reshape(ff // d, d)], axis=0)

    def pack_sh(t):
        return jnp.concatenate([t[n].reshape(r, -1) for n, r in SHARDED_SMALL], axis=0)

    g_rep = g_small[:n_rep]
    cs = d // N_DEV
    g_sh = lax.dynamic_slice_in_dim(g_small[n_rep:n_rows], me * cs, cs, axis=1)
    d_rep, m_rep, v_rep = _adam_small("adam_rep", pack_rep(w), g_rep, pack_rep(mom), pack_rep(var))
    d_sh, m_sh, v_sh = _adam_small("adam_sharded", pack_sh(w), g_sh, pack_sh(mom), pack_sh(var))

    def unpack(rep_t, sh_t, dst):
        for i, n in enumerate(REP):
            dst[n] = rep_t[i:i + 1].reshape(w[n].shape)
        dst["b_ff1"] = rep_t[len(REP):n_rep].reshape(w["b_ff1"].shape)
        r0 = 0
        for n, r in SHARDED_SMALL:
            dst[n] = sh_t[r0:r0 + r].reshape(w[n].shape)
            r0 += r

    unpack(g_rep, g_sh, out_g)
    unpack(d_rep, d_sh, out_d)
    unpack(m_rep, m_sh, out_m)
    unpack(v_rep, v_sh, out_v)

    _ORDER[:] = []
    outs = [loss, grad_x.reshape(x.shape)]
    for t in (out_g, out_d, out_m, out_v):
        outs += [t[n] for n in WEIGHT_ORDER]
    return tuple(outs)
```

```python
import jax
import jax.numpy as jnp
from jax import lax
from jax.experimental import pallas as pl
from jax.experimental.pallas import tpu as pltpu
from jax.experimental.pallas import tpu_sc as plsc

F32 = jnp.float32
BF16 = jnp.bfloat16
MESH = pl.DeviceIdType.MESH
N_DEV = 8

DN_ALPHA = 2.0 ** 0.25
LN_EPS = 1e-5
LRU_C = 8.0
ADAM_LR = 0.001
ADAM_B1 = 0.9
ADAM_B2 = 0.999
ADAM_EPS = 1e-08
ADAM_WD = 0.01
ADAM_STEP = 10
GELU_C = 0.7978845608028654
GELU_K = 0.044715

W_IN_PASSES = 5
HALO = 16
SUBLANES = 8
VMEM_MB = 56


def _cparams(sem, vmem_mb=VMEM_MB):
    return pltpu.CompilerParams(dimension_semantics=sem, vmem_limit_bytes=vmem_mb << 20)


HBM_SPEC = pl.BlockSpec(memory_space=pl.ANY)


class _Job:
    def __init__(self, ins, outs, sems, start, finish, mid=None):
        self.ins, self.outs, self.sems = list(ins), list(outs), list(sems)
        self.start, self.finish, self.mid = start, finish, mid
        self.results = None


_ORDER = []


def _pcall(name, body, grid, in_specs, out_specs, out_shape, inputs, scratch=(), aliases=None,
           vmem_mb=VMEM_MB):
    token = [t for t in _ORDER if not any(t is a for a in inputs)]
    n_in = len(inputs)

    def ordered(*refs):
        return body(*refs[:n_in], *refs[n_in + len(token):])

    res = pl.pallas_call(ordered, name=name, grid=grid, in_specs=list(in_specs) + [HBM_SPEC] * len(token),
                         out_specs=list(out_specs), out_shape=list(out_shape), scratch_shapes=list(scratch),
                         input_output_aliases=aliases or {},
                         compiler_params=_cparams(("arbitrary",) * len(grid), vmem_mb))(*inputs, *token)
    _ORDER[:] = [res[0]]
    return res


def _sequencer_job(name, job, collective_id):
    ji, jo = len(job.ins), len(job.outs)

    def body(*refs):
        jins, jouts, sems = refs[:ji], refs[ji:ji + jo], refs[ji + jo:]
        barrier = pltpu.get_barrier_semaphore()
        x, y, c = lax.axis_index("x"), lax.axis_index("y"), lax.axis_index("c")
        for r in range(1, N_DEV):
            peer = ((1 - x) if r & 4 else x, (1 - y) if r & 2 else y, (1 - c) if r & 1 else c)
            pl.semaphore_signal(barrier, inc=1, device_id=peer, device_id_type=MESH)
        pl.semaphore_wait(barrier, N_DEV - 1)
        job.start(jins, jouts, sems)
        if job.mid is not None:
            job.mid(jins, jouts, sems)
        job.finish(jins, jouts, sems)

    res = pl.kernel(
        body, name=name, out_type=job.outs, mesh=plsc.ScalarSubcoreMesh(axis_name="sequencer", num_cores=1),
        scratch_types=job.sems, compiler_params=pltpu.CompilerParams(collective_id=collective_id),
    )(*job.ins)
    job.results = list(res)
    return job.results


def _dot(mode, a, b):
    if mode == "nn":
        dims = (((1,), (0,)), ((), ()))
    elif mode == "nt":
        dims = (((1,), (1,)), ((), ()))
    else:
        dims = (((0,), (0,)), ((), ()))
    return lax.dot_general(a, b, dims, preferred_element_type=F32)


def _sig(x):
    return 0.5 * jnp.tanh(0.5 * x) + 0.5


def _gelu(x):
    t = jnp.tanh(GELU_C * (x + GELU_K * x * x * x))
    return 0.5 * x * (1.0 + t)


def _gelu_grad(x):
    x2 = x * x
    t = jnp.tanh(GELU_C * (x + GELU_K * x * x2))
    return 0.5 * (1.0 + t) + 0.5 * x * (1.0 - t * t) * GELU_C * (1.0 + 3.0 * GELU_K * x2)


def _colsum(v):
    return jnp.sum(v, axis=0, keepdims=True)


def _mm(name, mode, grid, a, a_spec, b, b_spec, extras, out_shapes, out_specs, epi, *,
        order="ij", acc_shape=None, aliases=None, vmem_mb=VMEM_MB):
    gm, gn, gk = grid

    def spec(s):
        bs, f = s
        if f is None:
            return pl.BlockSpec(memory_space=pl.ANY)
        if order == "ij":
            return pl.BlockSpec(bs, lambda i, j, k, f=f: f(i, j, k))
        return pl.BlockSpec(bs, lambda j, i, k, f=f: f(i, j, k))

    ne, no = len(extras), len(out_shapes)

    def kern(*refs):
        a_ref, b_ref = refs[0], refs[1]
        ex = refs[2:2 + ne]
        outs = refs[2 + ne:2 + ne + no]
        if order == "ij":
            i, j = pl.program_id(0), pl.program_id(1)
        else:
            j, i = pl.program_id(0), pl.program_id(1)
        k = pl.program_id(2)
        prod = _dot(mode, a_ref[...], b_ref[...])
        if gk == 1:
            epi(prod, i, j, ex, outs)
        elif epi is None:
            @pl.when(k == 0)
            def _():
                outs[0][...] = prod

            @pl.when(k > 0)
            def _():
                outs[0][...] += prod
        else:
            acc = refs[-1]

            @pl.when(k == 0)
            def _():
                acc[...] = prod

            @pl.when(k > 0)
            def _():
                acc[...] += prod

            @pl.when(k == gk - 1)
            def _():
                epi(acc[...], i, j, ex, outs)

    g = (gm, gn, gk) if order == "ij" else (gn, gm, gk)
    scratch = [pltpu.VMEM(acc_shape, F32)] if gk > 1 and epi is not None else []
    return _pcall(name, kern, g, [spec(a_spec), spec(b_spec)] + [spec(s) for _, s in extras],
                  [spec(s) for s in out_specs], out_shapes, [a, b] + [e for e, _ in extras],
                  scratch=scratch, aliases=aliases, vmem_mb=vmem_mb)


def _ext(ref, c, n_chunks, tc, seq):
    c0 = pl.multiple_of(c * tc, tc)
    body = ref[pl.ds(c0, tc), :].astype(F32)
    t0 = pl.multiple_of(jnp.maximum(c0 - HALO, 0), HALO)
    b0 = pl.multiple_of(jnp.minimum(c0 + tc, seq - HALO), HALO)
    top = ref[pl.ds(t0, HALO), :].astype(F32)
    bot = ref[pl.ds(b0, HALO), :].astype(F32)
    top = jnp.where(c > 0, top, 0.0)
    bot = jnp.where(c < n_chunks - 1, bot, 0.0)
    return jnp.concatenate([top, body, bot], axis=0)


def _shifted(vext, off, tc):
    n = vext.shape[0]
    r = vext if off == 0 else pltpu.roll(vext, (n - off) % n, 0)
    return r[HALO:HALO + tc]


def _win_sum(vext, g, extra, tc):
    s2 = vext + pltpu.roll(vext, 1, 0)
    s4 = s2 + pltpu.roll(s2, 2, 0)
    s8 = s4 + pltpu.roll(s4, 4, 0)
    s16 = s8 + pltpu.roll(s8, 8, 0)
    outs = [_shifted(s, extra + hw - 1, tc) for s, hw in ((s2, 1), (s4, 2), (s8, 4), (s16, 8))]
    return jnp.where(g == 0, outs[0], jnp.where(g == 1, outs[1], jnp.where(g == 2, outs[2], outs[3])))


def _win_cnt(t, hw, seq):
    return (jnp.minimum(t + hw, seq) - jnp.maximum(t - hw, 0)).astype(F32)


def _pool_d(uext, g, c, tc, seq):
    hw = jnp.left_shift(1, g)
    t = c * tc + lax.broadcasted_iota(jnp.int32, (tc, 1), 0)
    ws = _win_sum(uext, g, 0, tc)
    return ws / _win_cnt(t, hw, seq) - uext[HALO:HALO + tc]


def _scan_tiles(a_ref, b_ref, h_ref, carry_ref, n_tiles, reverse):
    blk = a_ref.shape[1]
    row = lax.broadcasted_iota(jnp.int32, (SUBLANES, blk), 0)

    def tile(j, hc):
        jj = (n_tiles - 1 - j) if reverse else j
        off = pl.multiple_of(jj * SUBLANES, SUBLANES)
        a = a_ref[pl.ds(off, SUBLANES), :]
        b = b_ref[pl.ds(off, SUBLANES), :]
        for kk in (1, 2, 4):
            sh = (SUBLANES - kk) if reverse else kk
            a_s = pltpu.roll(a, sh, 0)
            b_s = pltpu.roll(b, sh, 0)
            m = (row < SUBLANES - kk) if reverse else (row >= kk)
            a_s = jnp.where(m, a_s, 1.0)
            b_s = jnp.where(m, b_s, 0.0)
            b = a * b_s + b
            a = a * a_s
        h = a * hc + b
        h_ref[pl.ds(off, SUBLANES), :] = h
        return h[0:1, :] if reverse else h[SUBLANES - 1:SUBLANES, :]

    group = SUBLANES if n_tiles % SUBLANES == 0 else 1

    def tiles(jg, hc):
        for u in range(group):
            hc = tile(jg * group + u, hc)
        return hc

    hc = lax.fori_loop(0, n_tiles // group, tiles, carry_ref[0:1, :])
    carry_ref[0:1, :] = hc


def _lru_k(lam):
    y = -lam
    e = jnp.exp(-jnp.abs(y))
    u = 1.0 + e
    l1p = jnp.where(u == 1.0, e, jnp.log(u) * (e / (u - 1.0)))
    return -LRU_C * (jnp.maximum(y, 0.0) + l1p)


def _lru_gates(xc, wa, wx, ba, bx, lam):
    xb = xc.astype(BF16)
    r = _sig(jnp.dot(xb, wa, preferred_element_type=F32) + ba)
    i = _sig(jnp.dot(xb, wx, preferred_element_type=F32) + bx)
    k = _lru_k(lam)
    la = k * r
    a = jnp.exp(la)
    s = jnp.sqrt(-jnp.tanh(la) * (a * a + 1.0))
    return r, i, k, a, s


SV_CONV, SV_BA, SV_BX, SV_LAM = 0, 4, 6, 8


def _pool_fwd(z, pw, scale, seq, d, tc):
    n_g = pw.shape[0]
    pg = d // n_g
    n_chunks = seq // tc

    def kern(z_ref, pw_ref, sc_ref, y_ref):
        g, c = pl.program_id(0), pl.program_id(1)
        uext = _ext(z_ref, c, n_chunks, tc, seq)
        dd = _pool_d(uext, g, c, tc, seq)
        q = jnp.dot(dd.astype(BF16), pw_ref[...], preferred_element_type=F32)
        y_ref[...] = (q * sc_ref[...]).astype(BF16)

    (y,) = _pcall(
        "pool_fwd", kern, (n_g, n_chunks),
        [pl.BlockSpec((seq, pg), lambda g, c: (0, g)),
         pl.BlockSpec((None, pg, pg), lambda g, c: (g, 0, 0)),
         pl.BlockSpec((1, pg), lambda g, c: (0, g))],
        [pl.BlockSpec((tc, pg), lambda g, c: (c, g))],
        [jax.ShapeDtypeStruct((seq, d), BF16)], [z, pw, scale])
    return y


def _lru_fwd(z, sv, conv_b, wa, wx, seq, d, tc):
    n_h, blk = wa.shape[1], wa.shape[2]
    n_chunks = seq // tc
    lru_off = d // blk

    def kern(z_ref, sv_ref, cb_ref, wa_ref, wx_ref, xc_ref, h_ref, a_s, b_s, carry):
        c = pl.program_id(1)
        uext = _ext(z_ref, c, n_chunks, tc, seq)
        xc = cb_ref[...]
        for k in range(4):
            xc = xc + _shifted(uext, k - 2, tc) * sv_ref[SV_CONV + k:SV_CONV + k + 1, :]
        xc_ref[...] = xc
        _, i, _, a, s = _lru_gates(xc, wa_ref[...], wx_ref[...], sv_ref[SV_BA:SV_BA + 1, :],
                                   sv_ref[SV_BX:SV_BX + 1, :], sv_ref[SV_LAM:SV_LAM + 1, :])
        a_s[...] = a
        b_s[...] = s * (i * xc)

        @pl.when(c == 0)
        def _():
            carry[...] = jnp.zeros_like(carry)

        _scan_tiles(a_s, b_s, h_ref, carry, tc // SUBLANES, False)

    col = lambda h, c: (c, h)
    return _pcall(
        "lru_fwd", kern, (n_h, n_chunks),
        [pl.BlockSpec((seq, blk), lambda h, c: (0, lru_off + h)),
         pl.BlockSpec((16, blk), lambda h, c: (0, h)),
         pl.BlockSpec((1, blk), lambda h, c: (0, h)),
         pl.BlockSpec((None, None, blk, blk), lambda h, c: (0, h, 0, 0)),
         pl.BlockSpec((None, None, blk, blk), lambda h, c: (0, h, 0, 0))],
        [pl.BlockSpec((tc, blk), col), pl.BlockSpec((tc, blk), col)],
        [jax.ShapeDtypeStruct((seq, d), F32), jax.ShapeDtypeStruct((seq, d), F32)],
        [z, sv, conv_b, wa, wx],
        scratch=[pltpu.VMEM((tc, blk), F32), pltpu.VMEM((tc, blk), F32), pltpu.VMEM((SUBLANES, blk), F32)])


def _lru_rev(z, sv, wa, wx, xc, h_f, seq, d, tc):
    n_h, blk = wa.shape[1], wa.shape[2]
    n_chunks = seq // tc
    gate_off = 2 * d // blk

    def kern(z_ref, sv_ref, wa_ref, wx_ref, xc_ref, hf_ref, hb_ref, y_ref, a_s, b_s, carry):
        c = pl.program_id(1)
        xc = xc_ref[...]
        _, i, _, a, s = _lru_gates(xc, wa_ref[...], wx_ref[...], sv_ref[SV_BA + 1:SV_BA + 2, :],
                                   sv_ref[SV_BX + 1:SV_BX + 2, :], sv_ref[SV_LAM + 1:SV_LAM + 2, :])
        a_s[...] = a
        b_s[...] = s * (i * xc)

        @pl.when(c == 0)
        def _():
            carry[...] = jnp.zeros_like(carry)

        _scan_tiles(a_s, b_s, hb_ref, carry, tc // SUBLANES, True)
        y_ref[...] = ((hf_ref[...] + hb_ref[...]) * _gelu(z_ref[...])).astype(BF16)

    rev = lambda h, c: (n_chunks - 1 - c, h)
    return _pcall(
        "lru_rev", kern, (n_h, n_chunks),
        [pl.BlockSpec((tc, blk), lambda h, c: (n_chunks - 1 - c, gate_off + h)),
         pl.BlockSpec((16, blk), lambda h, c: (0, h)),
         pl.BlockSpec((None, None, blk, blk), lambda h, c: (1, h, 0, 0)),
         pl.BlockSpec((None, None, blk, blk), lambda h, c: (1, h, 0, 0)),
         pl.BlockSpec((tc, blk), rev), pl.BlockSpec((tc, blk), rev)],
        [pl.BlockSpec((tc, blk), rev), pl.BlockSpec((tc, blk), rev)],
        [jax.ShapeDtypeStruct((seq, d), F32), jax.ShapeDtypeStruct((seq, d), BF16)],
        [z, sv, wa, wx, xc, h_f],
        scratch=[pltpu.VMEM((tc, blk), F32), pltpu.VMEM((tc, blk), F32), pltpu.VMEM((SUBLANES, blk), F32)])


def _merge(y_pool, w_pu, y_lru, w_lu, z, seq, d, tm, tn):
    n_n = d // tn

    def kern(yp_ref, wp_ref, yl_ref, wl_ref, la_ref, lb_ref, m_ref, pa_ref, pb_ref):
        pa = jnp.dot(yp_ref[...], wp_ref[...], preferred_element_type=F32)
        pb = jnp.dot(yl_ref[...], wl_ref[...], preferred_element_type=F32)
        m_ref[...] = (_sig(la_ref[...]) * pa + _sig(lb_ref[...]) * pb).astype(BF16)
        pa_ref[...] = pa.astype(BF16)
        pb_ref[...] = pb.astype(BF16)

    row = pl.BlockSpec((tm, d), lambda i, j: (i, 0))
    wcol = pl.BlockSpec((d, tn), lambda i, j: (0, j))
    out = pl.BlockSpec((tm, tn), lambda i, j: (i, j))
    sh = jax.ShapeDtypeStruct((seq, d), BF16)
    return _pcall(
        "merge", kern, (seq // tm, n_n),
        [row, wcol, row, wcol,
         pl.BlockSpec((tm, tn), lambda i, j: (i, 3 * n_n + j)),
         pl.BlockSpec((tm, tn), lambda i, j: (i, 4 * n_n + j))],
        [out, out, out], [sh, sh, sh], [y_pool, w_pu, y_lru, w_lu, z, z])


def _ln_fwd(s, g, b):
    mu = jnp.mean(s, axis=-1, keepdims=True)
    xc = s - mu
    var = jnp.mean(xc * xc, axis=-1, keepdims=True)
    rstd = lax.rsqrt(var + LN_EPS)
    xhat = xc * rstd
    return xhat, rstd, xhat * g + b


def _ln_bwd(dy, xhat, rstd, g):
    dyg = dy * g
    m1 = jnp.mean(dyg, axis=-1, keepdims=True)
    m2 = jnp.mean(dyg * xhat, axis=-1, keepdims=True)
    return rstd * (dyg - m1 - xhat * m2)


def _pool_bwd(z, dy_pool, pw, scale, dz, seq, d, tc):
    n_g = pw.shape[0]
    pg = d // n_g
    n_chunks = seq // tc

    def kern(z_ref, dy_ref, pw_ref, sc_ref, dz_in, dz_ref, dpw_ref, dsc_ref):
        del dz_in
        g, c = pl.program_id(0), pl.program_id(1)
        hw = jnp.left_shift(1, g)
        uext = _ext(z_ref, c, n_chunks, tc, seq)
        dd = _pool_d(uext, g, c, tc, seq).astype(BF16)
        pwv = pw_ref[...]
        q = jnp.dot(dd, pwv, preferred_element_type=F32)
        dyext = _ext(dy_ref, c, n_chunks, tc, seq)

        @pl.when(c == 0)
        def _():
            dsc_ref[...] = jnp.zeros_like(dsc_ref)
            dpw_ref[...] = jnp.zeros_like(dpw_ref)

        dsc_ref[0:1, :] += _colsum(dyext[HALO:HALO + tc] * q)
        dqext = (dyext * sc_ref[...]).astype(BF16)
        dpw_ref[...] += _dot("tn", dd, dqext[HALO:HALO + tc])
        ddext = _dot("nt", dqext, pwv)
        text = c * tc - HALO + lax.broadcasted_iota(jnp.int32, (tc + 2 * HALO, 1), 0)
        v = ddext / jnp.maximum(_win_cnt(text, hw, seq), 1.0)
        dz_ref[...] = (_win_sum(v, g, 1, tc) - ddext[HALO:HALO + tc]).astype(BF16)

    return _pcall(
        "pool_bwd", kern, (n_g, n_chunks),
        [pl.BlockSpec((seq, pg), lambda g, c: (0, g)),
         pl.BlockSpec((seq, pg), lambda g, c: (0, g)),
         pl.BlockSpec((None, pg, pg), lambda g, c: (g, 0, 0)),
         pl.BlockSpec((1, pg), lambda g, c: (0, g)),
         pl.BlockSpec(memory_space=pl.ANY)],
        [pl.BlockSpec((None, tc, pg), lambda g, c: (2, c, g)),
         pl.BlockSpec((None, pg, pg), lambda g, c: (g, 0, 0)),
         pl.BlockSpec((SUBLANES, pg), lambda g, c: (0, g))],
        [jax.ShapeDtypeStruct(dz.shape, dz.dtype),
         jax.ShapeDtypeStruct((n_g, pg, pg), F32),
         jax.ShapeDtypeStruct((SUBLANES, d), F32)],
        [z, dy_pool, pw, scale, dz], aliases={4: 0})


def _lru_bwd(direction, xc, dh, h_dir, sv, wa, wx, dxc_prev, seq, d, tc):
    reverse = direction == 1
    n_h, blk = wa.shape[1], wa.shape[2]
    n_chunks = seq // tc
    has_prev = dxc_prev is not None

    def kern(*refs):
        xc_ref, dh_ref, h_ref, sv_ref, wa_ref, wx_ref = refs[:6]
        p = 6
        prev_ref = None
        if has_prev:
            prev_ref = refs[p]
            p += 1
        dxc_ref, dwa_ref, dwx_ref, sm_ref, at_s, g_s, carry, acarry = refs[p:p + 8]
        c = pl.program_id(1)
        cr = c if reverse else n_chunks - 1 - c
        c0 = pl.multiple_of(cr * tc, tc)

        @pl.when(c == 0)
        def _():
            carry[...] = jnp.zeros_like(carry)
            acarry[...] = jnp.zeros_like(acarry)
            dwa_ref[...] = jnp.zeros_like(dwa_ref)
            dwx_ref[...] = jnp.zeros_like(dwx_ref)
            sm_ref[...] = jnp.zeros_like(sm_ref)

        xc = xc_ref[...]
        wav, wxv = wa_ref[...], wx_ref[...]
        lam = sv_ref[SV_LAM + direction:SV_LAM + direction + 1, :]
        r, i, k, a, s = _lru_gates(xc, wav, wxv, sv_ref[SV_BA + direction:SV_BA + direction + 1, :],
                                   sv_ref[SV_BX + direction:SV_BX + direction + 1, :], lam)
        rowi = lax.broadcasted_iota(jnp.int32, (tc, blk), 0)
        hbody = h_ref[pl.ds(c0, tc), :]
        if not reverse:
            p0 = pl.multiple_of(jnp.maximum(c0 - SUBLANES, 0), SUBLANES)
            edge = jnp.where(cr > 0, h_ref[pl.ds(p0, SUBLANES), :][SUBLANES - 1:SUBLANES, :], 0.0)
            hprev = jnp.where(rowi == 0, edge, pltpu.roll(hbody, 1, 0))
            at = jnp.where(rowi == tc - 1, acarry[0:1, :], pltpu.roll(a, tc - 1, 0))
        else:
            n0 = pl.multiple_of(jnp.minimum(c0 + tc, seq - SUBLANES), SUBLANES)
            edge = jnp.where(cr < n_chunks - 1, h_ref[pl.ds(n0, SUBLANES), :][0:1, :], 0.0)
            hprev = jnp.where(rowi == tc - 1, edge, pltpu.roll(hbody, tc - 1, 0))
            at = jnp.where(rowi == 0, acarry[0:1, :], pltpu.roll(a, 1, 0))
        at_s[...] = at
        _scan_tiles(at_s, dh_ref, g_s, carry, tc // SUBLANES, not reverse)
        acarry[0:1, :] = a[tc - 1:tc, :] if reverse else a[0:1, :]

        gt = g_s[...]
        da = gt * hprev
        di = gt * s * xc
        dxc = gt * s * i
        ds = gt * (i * xc)
        dl = da * a - ds * (a * a) / s
        dpr = (dl * k) * r * (1.0 - r)
        dpi = di * i * (1.0 - i)
        sm_ref[0:1, :] += _colsum(dpr)
        sm_ref[1:2, :] += _colsum(dpi)
        sm_ref[2:3, :] += _colsum(dl * r) * (LRU_C * _sig(-lam))
        xb, dprb, dpib = xc.astype(BF16), dpr.astype(BF16), dpi.astype(BF16)
        dwa_ref[...] += _dot("tn", xb, dprb)
        dwx_ref[...] += _dot("tn", xb, dpib)
        dxc = dxc + _dot("nt", dprb, wav) + _dot("nt", dpib, wxv)
        if has_prev:
            dxc = dxc + prev_ref[...]
        dxc_ref[...] = dxc

    if reverse:
        chunk = lambda h, c: (c, h)
    else:
        chunk = lambda h, c: (n_chunks - 1 - c, h)
    wspec = pl.BlockSpec((None, None, blk, blk), lambda h, c: (direction, h, 0, 0))
    ins = [xc, dh, h_dir, sv, wa, wx] + ([dxc_prev] if has_prev else [])
    in_specs = [pl.BlockSpec((tc, blk), chunk), pl.BlockSpec((tc, blk), chunk),
                pl.BlockSpec((seq, blk), lambda h, c: (0, h)),
                pl.BlockSpec((16, blk), lambda h, c: (0, h)), wspec, wspec]
    if has_prev:
        in_specs.append(pl.BlockSpec((tc, blk), chunk))
    return _pcall(
        "lru_bwd_%d" % direction, kern, (n_h, n_chunks), in_specs,
        [pl.BlockSpec((tc, blk), chunk),
         pl.BlockSpec((None, blk, blk), lambda h, c: (h, 0, 0)),
         pl.BlockSpec((None, blk, blk), lambda h, c: (h, 0, 0)),
         pl.BlockSpec((SUBLANES, blk), lambda h, c: (0, h))],
        [jax.ShapeDtypeStruct((seq, d), F32),
         jax.ShapeDtypeStruct((n_h, blk, blk), F32),
         jax.ShapeDtypeStruct((n_h, blk, blk), F32),
         jax.ShapeDtypeStruct((SUBLANES, d), F32)],
        ins,
        scratch=[pltpu.VMEM((tc, blk), F32), pltpu.VMEM((tc, blk), F32),
                 pltpu.VMEM((SUBLANES, blk), F32), pltpu.VMEM((SUBLANES, blk), F32)])


def _conv_bwd(z, dxc, sv, dz, seq, d, tc, tcol):
    n_chunks = seq // tc
    lru_off = d // tcol

    def kern(z_ref, dx_ref, sv_ref, dz_in, dz_ref, sm_ref):
        del dz_in
        c = pl.program_id(1)
        uext = _ext(z_ref, c, n_chunks, tc, seq)
        dext = _ext(dx_ref, c, n_chunks, tc, seq)
        dbody = dext[HALO:HALO + tc]

        @pl.when(c == 0)
        def _():
            sm_ref[...] = jnp.zeros_like(sm_ref)

        du = jnp.zeros_like(dbody)
        for k in range(4):
            du = du + _shifted(dext, 2 - k, tc) * sv_ref[SV_CONV + k:SV_CONV + k + 1, :]
            sm_ref[k:k + 1, :] += _colsum(dbody * _shifted(uext, k - 2, tc))
        sm_ref[4:5, :] += _colsum(dbody)
        dz_ref[...] = du.astype(BF16)

    return _pcall(
        "conv_bwd", kern, (d // tcol, n_chunks),
        [pl.BlockSpec((seq, tcol), lambda j, c: (0, lru_off + j)),
         pl.BlockSpec((seq, tcol), lambda j, c: (0, j)),
         pl.BlockSpec((16, tcol), lambda j, c: (0, j)),
         pl.BlockSpec(memory_space=pl.ANY)],
        [pl.BlockSpec((None, tc, tcol), lambda j, c: (3, c, j)),
         pl.BlockSpec((SUBLANES, tcol), lambda j, c: (0, j))],
        [jax.ShapeDtypeStruct(dz.shape, dz.dtype), jax.ShapeDtypeStruct((SUBLANES, d), F32)],
        [z, dxc, sv, dz], aliases={3: 0})


BIG = (("w_in", "col"), ("pool_w", "row"), ("lru_wa", "row"), ("lru_wx", "row"), ("w_pool_up", "row"),
       ("w_lru_up", "row"), ("w_out", "row"), ("w_ff1", "col"), ("w_ff2", "row"))


def _shard_view(w, fam):
    if fam == "col":
        return w.reshape(w.shape[-2:])
    return w.reshape((-1,) + w.shape[-2:])


def _full_shape(sv_shape, fam):
    if fam == "col":
        return (sv_shape[0], N_DEV * sv_shape[1])
    return (sv_shape[0], N_DEV * sv_shape[1], sv_shape[2])


def _slot(ref, fam, p, size):
    if fam == "lead":
        return ref.at[p]
    start = pl.multiple_of(p * size, size)
    if fam == "col":
        return ref.at[:, pl.ds(start, size)]
    return ref.at[:, pl.ds(start, size), :]


def _coords():
    return lax.axis_index("x"), lax.axis_index("y"), lax.axis_index("c")


def _ag_job(shards, fams):
    n = len(shards)
    fulls = []
    for s, fam in zip(shards, fams):
        if fam == "lead":
            fulls.append(jax.ShapeDtypeStruct((N_DEV,) + s.shape, s.dtype))
        else:
            fulls.append(jax.ShapeDtypeStruct(_full_shape(s.shape, fam), s.dtype))
    sizes = [1 if fam == "lead" else s.shape[1] for s, fam in zip(shards, fams)]

    def ctx(ins, outs, sems):
        send, recv, loc = sems
        x, y, c = _coords()
        chips = [(1 - x, y), (x, 1 - y), (1 - x, 1 - y)]

        def copy(a, k, owner, to, src=None):
            dst = _slot(outs[a], fams[a], owner, sizes[a])
            return pltpu.make_async_remote_copy(
                src_ref=dst if src is None else src, dst_ref=dst,
                send_sem=send.at[a, k], recv_sem=recv.at[a, k], device_id=to, device_id_type=MESH)

        def local(a):
            return pltpu.make_async_copy(ins[a], _slot(outs[a], fams[a], 4 * x + 2 * y + c, sizes[a]), loc.at[a])

        return x, y, c, chips, copy, local

    def start(ins, outs, sems):
        x, y, c, chips, copy, local = ctx(ins, outs, sems)
        me = 4 * x + 2 * y + c
        for a in range(n):
            local(a).start()
            copy(a, 0, me, (x, y, 1 - c), ins[a]).start()
            for j, (cx, cy) in enumerate(chips):
                copy(a, 1 + j, me, (cx, cy, c), ins[a]).start()

    def mid(ins, outs, sems):
        x, y, c, chips, copy, _ = ctx(ins, outs, sems)
        for a in range(n):
            for j, (cx, cy) in enumerate(chips):
                owner = 4 * cx + 2 * cy + c
                copy(a, 1 + j, owner, (x, y, c)).wait_recv()
                copy(a, 4 + j, owner, (x, y, 1 - c)).start()

    def finish(ins, outs, sems):
        x, y, c, chips, copy, local = ctx(ins, outs, sems)
        me = 4 * x + 2 * y + c
        for a in range(n):
            copy(a, 0, 4 * x + 2 * y + (1 - c), (x, y, c)).wait_recv()
            for j, (cx, cy) in enumerate(chips):
                copy(a, 4 + j, 4 * cx + 2 * cy + (1 - c), (x, y, c)).wait_recv()
            copy(a, 0, me, (x, y, 1 - c), ins[a]).wait_send()
            for j, (cx, cy) in enumerate(chips):
                copy(a, 1 + j, me, (cx, cy, c), ins[a]).wait_send()
                copy(a, 4 + j, 4 * cx + 2 * cy + c, (x, y, 1 - c)).wait_send()
            local(a).wait()

    sems = [pltpu.SemaphoreType.DMA((n, 7)), pltpu.SemaphoreType.DMA((n, 7)), pltpu.SemaphoreType.DMA((n,))]
    return _Job(shards, fulls, sems, start, finish, mid)


def _rs_sibling_job(fulls, fams, sizes):
    n = len(fulls)
    outs = []
    for f, fam, sz in zip(fulls, fams, sizes):
        if fam == "col":
            outs.append(jax.ShapeDtypeStruct((4, f.shape[0], sz), f.dtype))
        else:
            outs.append(jax.ShapeDtypeStruct((4, f.shape[0], sz, f.shape[2]), f.dtype))

    def copies(ins, rcv, sems):
        send, recv = sems
        x, y, c = _coords()
        return [pltpu.make_async_remote_copy(
            src_ref=_slot(ins[a], fams[a], 2 * q + (1 - c), sizes[a]), dst_ref=rcv[a].at[q],
            send_sem=send.at[a, q], recv_sem=recv.at[a, q], device_id=(x, y, 1 - c), device_id_type=MESH)
            for a in range(n) for q in range(4)]

    def start(ins, rcv, sems):
        for cp in copies(ins, rcv, sems):
            cp.start()

    def finish(ins, rcv, sems):
        for cp in copies(ins, rcv, sems):
            cp.wait()

    return _Job(fulls, outs, [pltpu.SemaphoreType.DMA((n, 4)), pltpu.SemaphoreType.DMA((n, 4))], start, finish)


def _rs_chips_job(parts):
    n = len(parts)
    outs = [jax.ShapeDtypeStruct((3,) + p.shape[1:], p.dtype) for p in parts]

    def copies(ins, rcv, sems):
        send, recv = sems
        x, y, c = _coords()
        cps = []
        for a in range(n):
            for r in (1, 2, 3):
                tx, ty = (1 - x) if r & 2 else x, (1 - y) if r & 1 else y
                cps.append(pltpu.make_async_remote_copy(
                    src_ref=ins[a].at[2 * tx + ty], dst_ref=rcv[a].at[r - 1],
                    send_sem=send.at[a, r - 1], recv_sem=recv.at[a, r - 1],
                    device_id=(tx, ty, c), device_id_type=MESH))
        return cps

    def start(ins, rcv, sems):
        for cp in copies(ins, rcv, sems):
            cp.start()

    def finish(ins, rcv, sems):
        for cp in copies(ins, rcv, sems):
            cp.wait()

    return _Job(parts, outs, [pltpu.SemaphoreType.DMA((n, 3)), pltpu.SemaphoreType.DMA((n, 3))], start, finish)


def _tile_rows(rows, cols):
    tr = rows
    while tr * cols > (1 << 18) and tr % (2 * SUBLANES) == 0:
        tr //= 2
    return tr


def _rs_add(name, full, recv_a, fam, size, cidx):
    if fam == "col":
        rows = full.shape[0]
        tr = _tile_rows(rows, size)
        grid = (4, rows // tr)
        f_spec = pl.BlockSpec((tr, size), lambda q, i, cr: (i, 2 * q + cr[0]))
        s_spec = pl.BlockSpec((None, tr, size), lambda q, i, cr: (q, i, 0))
    else:
        nb, cols = full.shape[0], full.shape[2]
        tr = _tile_rows(nb * size, cols) // nb if nb > 1 else _tile_rows(size, cols)
        nt = size // tr
        grid = (4, nt)
        f_spec = pl.BlockSpec((nb, tr, cols), lambda q, i, cr: (0, (2 * q + cr[0]) * nt + i, 0))
        s_spec = pl.BlockSpec((None, nb, tr, cols), lambda q, i, cr: (q, 0, i, 0))

    token = [t for t in _ORDER if t is not full]

    def kern(c_ref, f_ref, r_ref, *rest):
        del c_ref
        rest[-1][...] = (f_ref[...] + r_ref[...]).astype(BF16)

    out = pl.pallas_call(
        kern, name=name,
        grid_spec=pltpu.PrefetchScalarGridSpec(num_scalar_prefetch=1, grid=grid,
                                               in_specs=[f_spec, s_spec] + [HBM_SPEC] * len(token), out_specs=s_spec),
        out_shape=jax.ShapeDtypeStruct(recv_a.shape, BF16),
        compiler_params=_cparams(("arbitrary", "arbitrary"), 32),
    )(cidx, full, recv_a, *token)
    _ORDER[:] = [out]
    return out


def _adam(w, g, m, v):
    m2 = ADAM_B1 * m + (1.0 - ADAM_B1) * g
    v2 = ADAM_B2 * v + (1.0 - ADAM_B2) * (g * g)
    m_hat = m2 / (1.0 - ADAM_B1 ** ADAM_STEP)
    v_hat = v2 / (1.0 - ADAM_B2 ** ADAM_STEP)
    delta = -ADAM_LR * (m_hat / (jnp.sqrt(v_hat) + ADAM_EPS) + ADAM_WD * w)
    return delta, m2, v2


def _rs_final_adam(name, parts, recv_b, w, m, v, fam, qidx):
    shp = w.shape
    pieces = parts if isinstance(parts, (list, tuple)) else [parts]
    recvs = recv_b if isinstance(recv_b, (list, tuple)) else [recv_b]
    n_p = len(pieces)
    first_blk = [0] * n_p
    if fam == "col":
        rows, cols = shp
        tr = _tile_rows(min(p.shape[1] for p in pieces), cols)
        per = [p.shape[1] // tr for p in pieces]
        for h in range(1, n_p):
            first_blk[h] = first_blk[h - 1] + per[h - 1]
        grid = (rows // tr,)
        w_spec = pl.BlockSpec((tr, cols), lambda i, qr: (i, 0))

        def piece_row(i, h):
            return jnp.clip(i - first_blk[h], 0, per[h] - 1)

        p_specs = [pl.BlockSpec((None, tr, cols), lambda i, qr, h=h: (qr[0], piece_row(i, h), 0))
                   for h in range(n_p)]
        r_specs = [pl.BlockSpec((3, tr, cols), lambda i, qr, h=h: (0, piece_row(i, h), 0)) for h in range(n_p)]
    else:
        assert n_p == 1
        nb, rows, cols = shp
        tr = _tile_rows(rows, cols)
        nt = rows // tr
        grid = (nb * nt,)
        w_spec = pl.BlockSpec((None, tr, cols), lambda i, qr: (i // nt, i % nt, 0))
        p_specs = [pl.BlockSpec((None, None, tr, cols), lambda i, qr: (qr[0], i // nt, i % nt, 0))]
        r_specs = [pl.BlockSpec((3, None, tr, cols), lambda i, qr: (0, i // nt, i % nt, 0))]

    token = list(_ORDER)

    def kern(*refs):
        p_refs, r_refs = refs[1:1 + n_p], refs[1 + n_p:1 + 2 * n_p]
        w_ref, m_ref, v_ref = refs[1 + 2 * n_p:4 + 2 * n_p]
        g_out, d_out, m_out, v_out = refs[4 + 2 * n_p + len(token):]

        def total(h):
            p_ref, r_ref = p_refs[h], r_refs[h]
            return ((p_ref[...].astype(F32) + r_ref[0].astype(F32)) + r_ref[1].astype(F32)) + r_ref[2].astype(F32)

        g = total(0)
        for h in range(1, n_p):
            g = jnp.where(pl.program_id(0) >= first_blk[h], total(h), g)
        delta, m2, v2 = _adam(w_ref[...], g, m_ref[...], v_ref[...])
        g_out[...] = g
        d_out[...] = delta
        m_out[...] = m2
        v_out[...] = v2

    sh = jax.ShapeDtypeStruct(shp, F32)
    res = pl.pallas_call(
        kern, name=name,
        grid_spec=pltpu.PrefetchScalarGridSpec(
            num_scalar_prefetch=1, grid=grid,
            in_specs=p_specs + r_specs + [w_spec, w_spec, w_spec] + [HBM_SPEC] * len(token),
            out_specs=[w_spec] * 4),
        out_shape=[sh] * 4,
        compiler_params=_cparams(("arbitrary",), 32),
    )(qidx, *pieces, *recvs, w, m, v, *token)
    _ORDER[:] = [res[1]]
    return res


def _sum8(name, parts):
    def kern(p_ref, o_ref):
        acc = p_ref[0]
        for p in range(1, N_DEV):
            acc = acc + p_ref[p]
        o_ref[...] = acc

    return pl.pallas_call(
        kern, name=name, out_shape=jax.ShapeDtypeStruct(parts.shape[1:], F32),
        compiler_params=pltpu.CompilerParams(vmem_limit_bytes=32 << 20),
    )(parts)


def _adam_small(name, w, g, m, v):
    def kern(w_ref, g_ref, m_ref, v_ref, d_out, m_out, v_out):
        delta, m2, v2 = _adam(w_ref[...], g_ref[...], m_ref[...], v_ref[...])
        d_out[...] = delta
        m_out[...] = m2
        v_out[...] = v2

    sh = jax.ShapeDtypeStruct(w.shape, F32)
    return pl.pallas_call(kern, name=name, out_shape=[sh] * 3)(w, g, m, v)


class _NoComm:
    def __init__(self):
        self.grads = {}

    def settled_before(self, kernel_name):
        return []

    def after(self, kernel_name):
        pass

    def grad(self, name, g):
        self.grads[name] = g


def _local_step(x, target, wts, vec, comm=None):
    comm = comm or _NoComm()
    _ORDER[:] = []
    seq, d = x.shape
    sv = vec["sv"]
    ff = vec["b_ff1"].shape[1]
    n_in = 5 * d

    def issue(name, call):
        _ORDER.extend(comm.settled_before(name))
        res = call()
        comm.after(name)
        return res

    tc = min(1024, seq)
    t1k, t512, t256 = min(1024, seq), min(512, seq), min(256, seq)
    n512 = min(512, d)
    tkd = d

    x_bf = x.astype(BF16)
    full = lambda i, j, k: (0, 0)

    def epi_store(acc, i, j, ex, outs):
        outs[0][...] = acc

    n_pass = W_IN_PASSES
    piece = n_in // (N_DEV * n_pass)
    tz = min(4096, seq)
    z = None
    for k in range(n_pass):
        w_piece = wts["w_in_piece_%d" % k]
        prev = [] if z is None else [(z, (None, None))]
        (z,) = issue("z_proj_%d" % k, lambda: _mm(
            "z_proj_%d" % k, "nn", (seq // tz, N_DEV, 1),
            x_bf, ((tz, d), lambda i, j, kk: (i, 0)), w_piece, ((d, piece), lambda i, j, kk: (0, j)),
            prev, [jax.ShapeDtypeStruct((seq, n_in), F32)],
            [((tz, piece), lambda i, j, kk, k=k: (i, n_pass * j + k))], epi_store,
            aliases={2: 0} if prev else None))

    pool_w, wa, wx = wts["pool_w"], wts["lru_wa"], wts["lru_wx"]
    blk = wa.shape[2]
    y_pool = issue("pool_fwd", lambda: _pool_fwd(z, pool_w, vec["pool_scale"], seq, d, tc))
    tcl = min(2048, seq)
    xc, h_f = issue("lru_fwd", lambda: _lru_fwd(z, sv, vec["conv_b"], wa, wx, seq, d, tcl))
    h_b, y_lru = issue("lru_rev", lambda: _lru_rev(z, sv, wa, wx, xc, h_f, seq, d, tcl))
    w_pu, w_lu = wts["w_pool_up"], wts["w_lru_up"]
    m_bf, p_a, p_b = issue("merge", lambda: _merge(y_pool, w_pu, y_lru, w_lu, z, seq, d, t1k, n512))
    w_out = wts["w_out"]

    def epi_ln1(acc, i, j, ex, outs):
        x_ref, bo, g1, b1 = ex
        s1 = DN_ALPHA * x_ref[...] + (acc + bo[...])
        xhat, rstd, x1 = _ln_fwd(s1, g1[...], b1[...])
        outs[0][...] = xhat
        outs[1][...] = x1.astype(BF16)
        outs[2][...] = rstd

    rowd = lambda t: ((t, d), lambda i, j, k: (i, 0))
    vecd = ((1, d), full)
    xhat1, x1_bf, rstd1 = issue("out_ln1", lambda: _mm(
        "out_ln1", "nn", (seq // t256, 1, 1), m_bf, rowd(t256), w_out, ((d, d), full),
        [(x, rowd(t256)), (vec["b_out"], vecd), (vec["ln1_g"], vecd), (vec["ln1_b"], vecd)],
        [jax.ShapeDtypeStruct((seq, d), F32), jax.ShapeDtypeStruct((seq, d), BF16),
         jax.ShapeDtypeStruct((seq, 1), F32)],
        [rowd(t256), rowd(t256), ((t256, 1), lambda i, j, k: (i, 0))], epi_ln1))
    w1 = wts["w_ff1"]

    def epi_ff1(acc, i, j, ex, outs):
        r = jnp.maximum(acc + ex[0][...], 0.0)
        outs[0][...] = r.astype(BF16)
        outs[1][...] = (r * r).astype(BF16)

    nf = min(1024, ff)
    tile_f = ((t1k, nf), lambda i, j, k: (i, j))
    relu_h, hdn = issue("ff1", lambda: _mm(
        "ff1", "nn", (seq // t1k, ff // nf, 1), x1_bf, rowd(t1k), w1, ((d, nf), lambda i, j, k: (0, j)),
        [(vec["b_ff1"], ((1, nf), lambda i, j, k: (0, j)))],
        [jax.ShapeDtypeStruct((seq, ff), BF16)] * 2, [tile_f, tile_f], epi_ff1))
    w2 = wts["w_ff2"]

    def epi_ln2(acc, i, j, ex, outs):
        xh1, tgt, g1, b1, bf2, g2, b2 = ex
        ds_ref, dsb_ref, sm_ref, loss_ref = outs
        x1 = xh1[...] * g1[...] + b1[...]
        s2 = DN_ALPHA * x1 + (acc + bf2[...])
        xhat, rstd, y = _ln_fwd(s2, g2[...], b2[...])
        e = y - tgt[...]
        part = 0.5 * jnp.sum(jnp.mean(e * e, axis=-1, keepdims=True))
        dy = e * (1.0 / d)
        ds2 = _ln_bwd(dy, xhat, rstd, g2[...])
        ds_ref[...] = ds2
        dsb_ref[...] = ds2.astype(BF16)
        sm_ref[0:1, :] += _colsum(dy * xhat)
        sm_ref[1:2, :] += _colsum(dy)
        sm_ref[2:3, :] += _colsum(ds2)
        loss_ref[...] += jnp.full(loss_ref.shape, part, F32)

    def zero_tail(n_tail):
        def init(i, j, outs):
            @pl.when(i == 0)
            def _():
                for o in outs[-n_tail:]:
                    o[...] = jnp.zeros_like(o)
        return init

    def rows_epilogue(name, lead, extras, out_shapes, out_specs, epi, n_tail):
        ne = len(extras)

        def kern(*refs):
            i = pl.program_id(0)
            outs = refs[1 + ne:]
            zero_tail(n_tail)(i, 0, outs)
            epi(refs[0][...], i, 0, refs[1:1 + ne], outs)

        def spec(s):
            bs, f = s
            return pl.BlockSpec(bs, lambda i, f=f: f(i, 0, 0))

        return _pcall(name, kern, (seq // t256,), [spec(rowd(t256))] + [spec(s) for _, s in extras],
                      [spec(s) for s in out_specs], out_shapes, [lead] + [e for e, _ in extras])

    tkw2, tw = min(2048, ff), min(1024, d)
    (ff_out,) = _mm("ff2", "nn", (seq // t1k, d // tw, ff // tkw2), hdn, ((t1k, tkw2), lambda i, j, k: (i, k)),
                    w2, ((tkw2, tw), lambda i, j, k: (k, j)), [], [jax.ShapeDtypeStruct((seq, d), F32)],
                    [((t1k, tw), lambda i, j, k: (i, j))], None)
    ds2, ds2_bf, sm_ln2, loss_blk = issue("ff2_ln2", lambda: rows_epilogue(
        "ln2_loss", ff_out,
        [(xhat1, rowd(t256)), (target, rowd(t256)), (vec["ln1_g"], vecd), (vec["ln1_b"], vecd),
         (vec["b_ff2"], vecd), (vec["ln2_g"], vecd), (vec["ln2_b"], vecd)],
        [jax.ShapeDtypeStruct((seq, d), F32), jax.ShapeDtypeStruct((seq, d), BF16),
         jax.ShapeDtypeStruct((SUBLANES, d), F32), jax.ShapeDtypeStruct((SUBLANES, 128), F32)],
        [rowd(t256), rowd(t256), ((SUBLANES, d), full), ((SUBLANES, 128), full)], epi_ln2, 2))

    tkw = min(2048, seq)

    def dw(name, wname, a, b, m_dim, n_dim, b_spec=None, row0=0):
        _ORDER.extend(comm.settled_before(name))
        tn = min(1024, n_dim)
        tm = next(t for t in (1024, 768, 512, 256, 128) if m_dim % t == 0 and row0 % t == 0)
        b_spec = b_spec or ((tkw, tn), lambda i, j, k: (k, j))
        i0 = row0 // tm
        (out,) = _mm(
            name, "tn", (m_dim // tm, n_dim // tn, seq // tkw),
            a, ((tkw, tm), lambda i, j, k: (k, i0 + i)), b, b_spec, [],
            [jax.ShapeDtypeStruct((m_dim, n_dim), F32)], [((tm, tn), lambda i, j, k: (i, j))],
            epi_store if seq == tkw else None)
        comm.grad(wname, out)
        comm.after(name)

    dw("dw_ff2", "w_ff2", hdn, ds2_bf, ff, d)

    def epi_dpre(acc, i, j, ex, outs):
        dpre = acc * (2.0 * ex[0][...].astype(F32))
        outs[0][...] = dpre.astype(BF16)

        @pl.when(i == 0)
        def _():
            outs[1][...] = jnp.zeros_like(outs[1])

        outs[1][0:1, :] += _colsum(dpre)

    dpre, sm_bff1 = issue("dhdn", lambda: _mm(
        "dhdn", "nt", (seq // t1k, ff // nf, 1), ds2_bf, rowd(t1k), w2, ((nf, d), lambda i, j, k: (j, 0)),
        [(relu_h, tile_f)],
        [jax.ShapeDtypeStruct((seq, ff), BF16), jax.ShapeDtypeStruct((SUBLANES, ff), F32)],
        [tile_f, ((SUBLANES, nf), lambda i, j, k: (0, j))], epi_dpre, order="ji"))

    dw("dw_ff1", "w_ff1", x1_bf, dpre, d, ff)

    def epi_ln1b(acc, i, j, ex, outs):
        ds2_ref, xh1, rs1, g1 = ex
        ds_ref, dsb_ref, sm_ref = outs
        dy1 = acc + DN_ALPHA * ds2_ref[...]
        xhat = xh1[...]
        ds1 = _ln_bwd(dy1, xhat, rs1[...], g1[...])
        ds_ref[...] = ds1
        dsb_ref[...] = ds1.astype(BF16)
        sm_ref[0:1, :] += _colsum(dy1 * xhat)
        sm_ref[1:2, :] += _colsum(dy1)
        sm_ref[2:3, :] += _colsum(ds1)

    (dx1,) = _mm("dx1", "nt", (seq // t1k, d // tw, ff // tkw2), dpre, ((t1k, tkw2), lambda i, j, k: (i, k)),
                 w1, ((tw, tkw2), lambda i, j, k: (j, k)), [], [jax.ShapeDtypeStruct((seq, d), F32)],
                 [((t1k, tw), lambda i, j, k: (i, j))], None)
    ds1, ds1_bf, sm_ln1 = issue("dx1_ln1", lambda: rows_epilogue(
        "ln1_bwd", dx1,
        [(ds2, rowd(t256)), (xhat1, rowd(t256)), (rstd1, ((t256, 1), lambda i, j, k: (i, 0))),
         (vec["ln1_g"], vecd)],
        [jax.ShapeDtypeStruct((seq, d), F32), jax.ShapeDtypeStruct((seq, d), BF16),
         jax.ShapeDtypeStruct((SUBLANES, d), F32)],
        [rowd(t256), rowd(t256), ((SUBLANES, d), full)], epi_ln1b, 1))

    dw("dw_out", "w_out", m_bf, ds1_bf, d, d)
    nd = min(1024, d)
    n_n = d // nd
    tile_d = ((t512, nd), lambda i, j, k: (i, j))

    def epi_dm(acc, i, j, ex, outs):
        la, lb, pa, pb = ex
        ga, gb = _sig(la[...]), _sig(lb[...])
        outs[0][...] = (acc * ga).astype(BF16)
        outs[1][...] = (acc * gb).astype(BF16)
        outs[2][0] = (acc * pa[...].astype(F32) * ga * (1.0 - ga)).astype(BF16)
        outs[2][1] = (acc * pb[...].astype(F32) * gb * (1.0 - gb)).astype(BF16)

    dp_a, dp_b, dz = _mm(
        "dm", "nt", (seq // t512, n_n, 1), ds1_bf, rowd(t512), w_out, ((nd, d), lambda i, j, k: (j, 0)),
        [(z, ((t512, nd), lambda i, j, k: (i, 3 * n_n + j))),
         (z, ((t512, nd), lambda i, j, k: (i, 4 * n_n + j))), (p_a, tile_d), (p_b, tile_d)],
        [jax.ShapeDtypeStruct((seq, d), BF16), jax.ShapeDtypeStruct((seq, d), BF16),
         jax.ShapeDtypeStruct((5, seq, d), BF16)],
        [tile_d, tile_d, ((2, t512, nd), lambda i, j, k: (0, i, j))], epi_dm)

    dw("dw_pool_up", "w_pool_up", y_pool, dp_a, d, d)
    dw("dw_lru_up", "w_lru_up", y_lru, dp_b, d, d)

    def epi_bf(acc, i, j, ex, outs):
        outs[0][...] = acc.astype(BF16)

    (dy_pool,) = issue("dy_pool", lambda: _mm(
        "dy_pool", "nt", (seq // t512, n_n, 1), dp_a, rowd(t512), w_pu, ((nd, d), lambda i, j, k: (j, 0)), [],
        [jax.ShapeDtypeStruct((seq, d), BF16)], [tile_d], epi_bf))

    def epi_dylru(acc, i, j, ex, outs):
        hf, hb, ug, _ = ex
        u = ug[...]
        outs[0][...] = acc * _gelu(u)
        outs[1][...] = (acc * (hf[...] + hb[...]) * _gelu_grad(u)).astype(BF16)

    dz_in = dz
    dh, dz = issue("dy_lru", lambda: _mm(
        "dy_lru", "nt", (seq // t512, n_n, 1), dp_b, rowd(t512), w_lu, ((nd, d), lambda i, j, k: (j, 0)),
        [(h_f, tile_d), (h_b, tile_d), (z, ((t512, nd), lambda i, j, k: (i, 2 * n_n + j))),
         (dz_in, (None, None))],
        [jax.ShapeDtypeStruct((seq, d), F32), jax.ShapeDtypeStruct(dz_in.shape, BF16)],
        [tile_d, ((None, t512, nd), lambda i, j, k: (4, i, j))], epi_dylru, aliases={5: 1}))

    dz, g_pw, sm_pool = _pool_bwd(z, dy_pool, pool_w, vec["pool_scale"], dz, seq, d, tc)
    comm.grad("pool_w", g_pw)
    dxc0, g_wa0, g_wx0, sm_l0 = issue("lru_bwd_0", lambda: _lru_bwd(
        0, xc, dh, h_f, sv, wa, wx, None, seq, d, tcl))
    dxc, g_wa1, g_wx1, sm_l1 = issue("lru_bwd_1", lambda: _lru_bwd(
        1, xc, dh, h_b, sv, wa, wx, dxc0, seq, d, tcl))
    comm.grad("lru_wa", jnp.concatenate([g_wa0, g_wa1], axis=0))
    comm.grad("lru_wx", jnp.concatenate([g_wx0, g_wx1], axis=0))
    dz, sm_conv = _conv_bwd(z, dxc, sv, dz, seq, d, tcl, blk)

    tnw = min(1024, d)
    per_seg = d // tnw
    seg_spec = ((None, tkw, tnw), lambda i, j, k: ((j // per_seg + 2) % 5, k, j % per_seg))
    lo_rows = 3 * d // 4
    dw("dw_in_lo", "w_in_lo", x_bf, dz, lo_rows, n_in, b_spec=seg_spec)
    dw("dw_in_hi", "w_in_hi", x_bf, dz, d - lo_rows, n_in, b_spec=seg_spec, row0=lo_rows)

    nk = d // tkd

    def epi_dx(acc, i, j, ex, outs):
        outs[0][...] = acc + DN_ALPHA * ex[0][...]

    (grad_x,) = issue("dx", lambda: _mm(
        "dx", "nt", (seq // t512, 1, n_in // tkd), dz,
        ((None, t512, tkd), lambda i, j, k: ((k // nk + 2) % 5, i, k % nk)),
        wts["w_in"], ((d, tkd), lambda i, j, k: (0, k)), [(ds1, rowd(t512))],
        [jax.ShapeDtypeStruct((seq, d), F32)], [rowd(t512)], epi_dx, acc_shape=(t512, d)))

    small = {"ln2": sm_ln2, "b_ff1": sm_bff1, "ln1": sm_ln1, "pool": sm_pool, "lru0": sm_l0, "lru1": sm_l1,
             "conv": sm_conv}
    return loss_blk[0, 0], grad_x, small


REP = ("pool_scale", "conv_b", "b_out", "ln1_g", "ln1_b", "b_ff2", "ln2_g", "ln2_b")
SHARDED_SMALL = (("conv_w", 4), ("lru_ba", 2), ("lru_bx", 2), ("lru_lambda", 2))
WEIGHT_ORDER = ("w_in", "pool_w", "pool_scale", "conv_w", "conv_b", "lru_wa", "lru_ba", "lru_wx", "lru_bx",
                "lru_lambda", "w_pool_up", "w_lru_up", "w_out", "b_out", "ln1_g", "ln1_b", "w_ff1", "b_ff1",
                "w_ff2", "b_ff2", "ln2_g", "ln2_b")


def _pad_rows(a, rows):
    return jnp.concatenate([a, jnp.zeros((rows - a.shape[0], a.shape[1]), a.dtype)], axis=0)


def kernel(x, w_in, pool_w, pool_scale, conv_w, conv_b, lru_wa, lru_ba, lru_wx, lru_bx, lru_lambda, w_pool_up, w_lru_up, w_out, b_out, ln1_g, ln1_b, w_ff1, b_ff1, w_ff2, b_ff2, ln2_g, ln2_b, loss_target, m_w_in, m_pool_w, m_pool_scale, m_conv_w, m_conv_b, m_lru_wa, m_lru_ba, m_lru_wx, m_lru_bx, m_lru_lambda, m_w_pool_up, m_w_lru_up, m_w_out, m_b_out, m_ln1_g, m_ln1_b, m_w_ff1, m_b_ff1, m_w_ff2, m_b_ff2, m_ln2_g, m_ln2_b, v_w_in, v_pool_w, v_pool_scale, v_conv_w, v_conv_b, v_lru_wa, v_lru_ba, v_lru_wx, v_lru_bx, v_lru_lambda, v_w_pool_up, v_w_lru_up, v_w_out, v_b_out, v_ln1_g, v_ln1_b, v_w_ff1, v_b_ff1, v_w_ff2, v_b_ff2, v_ln2_g, v_ln2_b):
    args = dict(locals())
    w = {n: args[n] for n in WEIGHT_ORDER}
    mom = {n: args["m_" + n] for n in WEIGHT_ORDER}
    var = {n: args["v_" + n] for n in WEIGHT_ORDER}
    seq, d = x.shape[1], x.shape[2]
    n_heads, blk = lru_wa.shape[2], lru_wa.shape[4]
    n_groups = pool_w.shape[1]
    ff = b_ff1.shape[1]
    cx, cy, cc = _coords()
    me = 4 * cx + 2 * cy + cc
    cidx = jnp.reshape(cc, (1,)).astype(jnp.int32)
    qidx = jnp.reshape(2 * cx + cy, (1,)).astype(jnp.int32)

    fam_of = dict(BIG)
    sviews = {n: _shard_view(w[n], fam) for n, fam in BIG}
    size_of = {n: sviews[n].shape[1] for n, _ in BIG}
    for piece_name in ("w_in_lo", "w_in_hi"):
        fam_of[piece_name], size_of[piece_name] = fam_of["w_in"], size_of["w_in"]
    wts = {}

    def take_gathered(names, arrays):
        for n, g in zip(names, arrays):
            if n == "pool_w":
                g = g.reshape(n_groups, d // n_groups, d // n_groups)
            elif n in ("lru_wa", "lru_wx"):
                g = g.reshape(2, n_heads, blk, blk)
            elif fam_of[n] == "row":
                g = g.reshape(g.shape[1:])
            wts[n] = g

    shard_bf = {n: sviews[n].astype(BF16) for n, _ in BIG}
    piece = sviews["w_in"].shape[1] // W_IN_PASSES
    for k in range(W_IN_PASSES):
        name = "w_in_piece_%d" % k
        shard_bf[name] = shard_bf["w_in"][:, k * piece:(k + 1) * piece]
        fam_of[name] = "col"

    def gather_job(names, extra=()):
        return _ag_job([shard_bf[n] for n in names] + [e for e, _ in extra],
                       [fam_of[n] for n in names] + [f for _, f in extra])

    launched = [0]

    def on_sequencer(kind, job):
        launched[0] += 1
        return _sequencer_job("sq_%s_%d" % (kind, launched[0]), job, launched[0] % 2)

    class Plan:
        sibling = {"dw_ff2": ("w_ff2",), "dw_ff1": ("w_ff1",), "dw_lru_up": ("w_out", "w_pool_up", "w_lru_up"),
                   "dw_in_lo": ("w_in_lo",), "dw_in_hi": ("w_in_hi", "pool_w", "lru_wa", "lru_wx")}
        chips = {"dw_ff1": ("w_ff2",), "dw_out": ("w_ff1",), "dy_lru": ("w_out", "w_pool_up", "w_lru_up"),
                 "dw_in_hi": ("w_in_lo",), "dx": ("w_in_hi", "pool_w", "lru_wa", "lru_wx")}

        def __init__(self):
            self.grads, self.recv_a, self.parts, self.recv_b = {}, {}, {}, {}

        def grad(self, name, g):
            self.grads[name] = g if fam_of[name] == "col" else g.reshape((-1,) + g.shape[-2:])

        settle = {"dw_ff1": ("w_in",), "dw_out": ("w_ff2",), "dy_pool": ("w_ff1",), "lru_bwd_1": ("w_out",)}
        settle_add = {"dx": ("w_in_lo",)}

        def settled_before(self, host):
            return [wts[n] if n == "w_in" else self.recv_b[n] for n in self.settle.get(host, ())]

        def after(self, host):
            if host in self.chips:
                names = self.chips[host]
                _ORDER.extend(self.recv_b[n] for n in self.settle_add.get(host, ()))
                for n in names:
                    self.parts[n] = _rs_add("rs_add_" + n, self.grads[n], self.recv_a[n], fam_of[n], size_of[n], cidx)
                res = on_sequencer("chips", _rs_chips_job([self.parts[n] for n in names]))
                self.recv_b.update(zip(names, res))
            if host in self.sibling:
                names = self.sibling[host]
                res = on_sequencer("sibling", _rs_sibling_job(
                    [self.grads[n] for n in names], [fam_of[n] for n in names], [size_of[n] for n in names]))
                self.recv_a.update(zip(names, res))

    first = ("w_in_piece_0",)
    sv_shard = _pad_rows(jnp.concatenate([w[n].reshape(r, -1) for n, r in SHARDED_SMALL], axis=0), 16)
    gathered = on_sequencer("gather", gather_job(first, [(sv_shard, "col")]))
    take_gathered(first, gathered[:-1])
    vec = {n: w[n] for n in REP}
    vec["b_ff1"] = b_ff1
    vec["sv"] = gathered[-1]
    queue = [("w_in_piece_%d" % k,) for k in range(1, W_IN_PASSES)]
    queue += [("pool_w", "lru_wa", "lru_wx"), ("w_pool_up", "w_lru_up"), ("w_out",), ("w_ff1",), ("w_ff2",),
              ("w_in",)]
    for names in queue:
        take_gathered(names, on_sequencer("gather", gather_job(names)))

    plan = Plan()
    loss_part, grad_x, small = _local_step(x.reshape(seq, d), loss_target.reshape(seq, d), wts, vec, plan)

    out_g, out_d, out_m, out_v = {}, {}, {}, {}
    for n, fam in sorted(BIG, key=lambda nf: nf[0] in ("w_in", "pool_w", "lru_wa", "lru_wx")):
        halves = [n + "_lo", n + "_hi"] if n == "w_in" else [n]
        res = _rs_final_adam("adam_" + n, [plan.parts[h] for h in halves], [plan.recv_b[h] for h in halves],
                             sviews[n], _shard_view(mom[n], fam), _shard_view(var[n], fam), fam, qidx)
        out_g[n], out_d[n], out_m[n], out_v[n] = [r.reshape(w[n].shape) for r in res]

    loss_row = jnp.concatenate([loss_part.reshape(1, 1), jnp.zeros((1, d - 1), F32)], axis=1)
    rows = [small["pool"][0:1], small["conv"][4:5], small["ln1"][2:3], small["ln1"][0:1], small["ln1"][1:2],
            small["ln2"][2:3], small["ln2"][0:1], small["ln2"][1:2], small["b_ff1"][0:1].reshape(ff // d, d),
            small["conv"][0:4], small["lru0"][0:1], small["lru1"][0:1], small["lru0"][1:2], small["lru1"][1:2],
            small["lru0"][2:3], small["lru1"][2:3], loss_row]
    n_rep = len(REP) + ff // d
    n_rows = n_rep + sum(r for _, r in SHARDED_SMALL)
    pad_rows = -(-(n_rows + 1) // SUBLANES) * SUBLANES
    packed = _pad_rows(jnp.concatenate(rows, axis=0), pad_rows)
    (all_small,) = on_sequencer("gather", _ag_job([packed], ["lead"]))
    g_small = _sum8("sum_small", all_small)
    loss = g_small[n_rows, 0]

    def pack_rep(t):
        return jnp.concatenate([t[n] for n in REP] + [t["b_ff1"].reshape(ff // d, d)], axis=0)

    def pack_sh(t):
        return jnp.concatenate([t[n].reshape(r, -1) for n, r in SHARDED_SMALL], axis=0)

    g_rep = g_small[:n_rep]
    cs = d // N_DEV
    g_sh = lax.dynamic_slice_in_dim(g_small[n_rep:n_rows], me * cs, cs, axis=1)
    d_rep, m_rep, v_rep = _adam_small("adam_rep", pack_rep(w), g_rep, pack_rep(mom), pack_rep(var))
    d_sh, m_sh, v_sh = _adam_small("adam_sharded", pack_sh(w), g_sh, pack_sh(mom), pack_sh(var))

    def unpack(rep_t, sh_t, dst):
        for i, n in enumerate(REP):
            dst[n] = rep_t[i:i + 1].reshape(w[n].shape)
        dst["b_ff1"] = rep_t[len(REP):n_rep].reshape(w["b_ff1"].shape)
        r0 = 0
        for n, r in SHARDED_SMALL:
            dst[n] = sh_t[r0:r0 + r].reshape(w[n].shape)
            r0 += r

    unpack(g_rep, g_sh, out_g)
    unpack(d_rep, d_sh, out_d)
    unpack(m_rep, m_sh, out_m)
    unpack(v_rep, v_sh, out_v)

    _ORDER[:] = []
    outs = [loss, grad_x.reshape(x.shape)]
    for t in (out_g, out_d, out_m, out_v):
        outs += [t[n] for n in WEIGHT_ORDER]
    return tuple(outs)
```
